```python
import jax, jax.numpy as jnp
from jax import lax
import numpy as np

D_MODEL = 2048
BATCH = 8
SEQ = 8192
DEPTH = 1

PLE_DIM = 256
EPS = 1e-6

N_HEADS_MLA = 16
Q_LORA = 512
KV_LORA = 512
QK_NOPE = 128
QK_ROPE = 64
V_DIM = 128
QK_DIM = QK_NOPE + QK_ROPE
ROPE_THETA = 10000.0
Q_BLOCK = 128

SSM_EXPAND = 2
D_INNER = SSM_EXPAND * D_MODEL
SSM_HEADDIM = 64
N_HEADS_SSM = D_INNER // SSM_HEADDIM
SSM_GROUPS = 8
HEADS_PER_GROUP = N_HEADS_SSM // SSM_GROUPS
D_STATE = 128
CONV_WIDTH = 4
CHUNK = 256
CONV_DIM = D_INNER + 2 * SSM_GROUPS * D_STATE

D_FF = ((8 * D_MODEL // 3 + 255) // 256) * 256

IN_SPLITS = (Q_LORA, KV_LORA, QK_ROPE, D_INNER, CONV_DIM, N_HEADS_SSM, D_MODEL, D_MODEL)
D_IN_PROJ = Q_LORA + KV_LORA + QK_ROPE + D_INNER + CONV_DIM + N_HEADS_SSM + 2 * D_MODEL

kernel_name = "hybrid_mla_ssd_gated_block"


def rms_norm(x, w):
    xf = x.astype(jnp.float32)
    y = xf * lax.rsqrt(jnp.mean(xf * xf, axis=-1, keepdims=True) + EPS)
    return (y * w.astype(jnp.float32)).astype(x.dtype)


def split_cols(t, sizes):
    outs, off = [], 0
    for s in sizes:
        outs.append(t[..., off:off + s])
        off += s
    return outs


def rotary_tables(positions, dim):
    inv_freq = ROPE_THETA ** (-jnp.arange(0, dim, 2, dtype=jnp.float32) / dim)
    ang = positions.astype(jnp.float32)[..., None] * inv_freq
    return jnp.cos(ang)[:, :, None, :], jnp.sin(ang)[:, :, None, :]


def apply_rope(x, cos, sin):
    xf = x.astype(jnp.float32)
    x1, x2 = jnp.split(xf, 2, axis=-1)
    out = jnp.concatenate([x1 * cos - x2 * sin, x2 * cos + x1 * sin], axis=-1)
    return out.astype(x.dtype)


def causal_block_attention(q, k, v):
    b, s, h, dk = q.shape
    nblk = s // Q_BLOCK
    scale = dk ** -0.5
    qb = jnp.moveaxis(q.reshape(b, nblk, Q_BLOCK, h, dk), 1, 0)
    key_idx = jnp.arange(s)

    def one_block(args):
        q_blk, blk = args
        sc = jnp.einsum('bqhd,bkhd->bhqk', q_blk, k, preferred_element_type=jnp.float32) * scale
        q_idx = blk * Q_BLOCK + jnp.arange(Q_BLOCK)
        mask = key_idx[None, :] <= q_idx[:, None]
        sc = jnp.where(mask[None, None], sc, -jnp.inf)
        prob = jax.nn.softmax(sc, axis=-1).astype(v.dtype)
        return jnp.einsum('bhqk,bkhd->bqhd', prob, v)

    o = lax.map(one_block, (qb, jnp.arange(nblk)))
    return jnp.moveaxis(o, 0, 1).reshape(b, s, h * v.shape[-1])


def mla_branch(c_q, c_kv, k_r, cos, sin, q_norm, w_uq, kv_norm, w_ukv):
    b, s, _ = c_q.shape
    q = (rms_norm(c_q, q_norm) @ w_uq).reshape(b, s, N_HEADS_MLA, QK_DIM)
    q_nope, q_pe = q[..., :QK_NOPE], apply_rope(q[..., QK_NOPE:], cos, sin)
    kv = (rms_norm(c_kv, kv_norm) @ w_ukv).reshape(b, s, N_HEADS_MLA, QK_NOPE + V_DIM)
    k_nope, v = kv[..., :QK_NOPE], kv[..., QK_NOPE:]
    k_pe = apply_rope(k_r[:, :, None, :], cos, sin)
    k = jnp.concatenate([k_nope, jnp.broadcast_to(k_pe, (b, s, N_HEADS_MLA, QK_ROPE))], axis=-1)
    q = jnp.concatenate([q_nope, q_pe], axis=-1)
    return causal_block_attention(q, k, v)


def causal_depthwise_conv(t, w, bias):
    out = lax.conv_general_dilated(
        t, w[:, None, :].astype(t.dtype), window_strides=(1,),
        padding=[(CONV_WIDTH - 1, 0)], dimension_numbers=('NWC', 'WIO', 'NWC'),
        feature_group_count=t.shape[-1])
    return out + bias


def ssd_chunked(x, dt, a, bm, cm):
    b, s, g, r, p = x.shape
    n = bm.shape[-1]
    nc = -(-s // CHUNK)
    pad = nc * CHUNK - s
    padw = lambda t: jnp.pad(t, [(0, 0), (0, pad)] + [(0, 0)] * (t.ndim - 2))
    xdt = padw(x * dt[..., None])
    da = padw(dt * a)
    bm, cm = padw(bm), padw(cm)
    chunks = lambda t: jnp.moveaxis(t.reshape((b, nc, CHUNK) + t.shape[2:]), 1, 0)
    causal = jnp.tril(jnp.ones((CHUNK, CHUNK), dtype=bool))[None, :, :, None, None]

    def step(state, inp):
        xc, ac, bc, cc = inp
        cum = jnp.cumsum(ac, axis=1)
        seg = cum[:, :, None] - cum[:, None, :]
        decay = jnp.exp(jnp.where(causal, seg, -jnp.inf))
        cb = jnp.einsum('bign,bjgn->bijg', cc, bc)
        y_diag = jnp.einsum('bijgr,bjgrp->bigrp', cb[..., None] * decay, xc)
        y_off = jnp.einsum('bign,bgrpn->bigrp', cc, state) * jnp.exp(cum)[..., None]
        last = cum[:, -1]
        w_end = jnp.exp(last[:, None] - cum)
        state = state * jnp.exp(last)[..., None, None] + jnp.einsum(
            'bjgn,bjgrp->bgrpn', bc, xc * w_end[..., None])
        return state, y_diag + y_off

    state0 = jnp.zeros((b, g, r, p, n), jnp.float32)
    _, ys = lax.scan(step, state0, (chunks(xdt), chunks(da), chunks(bm), chunks(cm)))
    ys = jnp.moveaxis(ys, 0, 1).reshape(b, nc * CHUNK, g, r, p)
    return ys[:, :s]


def mamba2_branch(z, xbc, dt_raw, conv_w, conv_b, dt_bias, a_log, d_skip, ssm_norm):
    b, s, _ = z.shape
    xbc = jax.nn.silu(causal_depthwise_conv(xbc, conv_w, conv_b))
    xs, bm, cm = split_cols(xbc, (D_INNER, SSM_GROUPS * D_STATE, SSM_GROUPS * D_STATE))
    xh = xs.reshape(b, s, SSM_GROUPS, HEADS_PER_GROUP, SSM_HEADDIM).astype(jnp.float32)
    dt = jax.nn.softplus(dt_raw.astype(jnp.float32) + dt_bias.astype(jnp.float32))
    dt = dt.reshape(b, s, SSM_GROUPS, HEADS_PER_GROUP)
    a = -jnp.exp(a_log.astype(jnp.float32)).reshape(SSM_GROUPS, HEADS_PER_GROUP)
    y = ssd_chunked(xh, dt, a,
                    bm.reshape(b, s, SSM_GROUPS, D_STATE).astype(jnp.float32),
                    cm.reshape(b, s, SSM_GROUPS, D_STATE).astype(jnp.float32))
    y = y + d_skip.astype(jnp.float32).reshape(SSM_GROUPS, HEADS_PER_GROUP)[..., None] * xh
    y = y.reshape(b, s, D_INNER).astype(z.dtype) * jax.nn.silu(z)
    y = rms_norm(y.reshape(b, s, SSM_GROUPS, D_INNER // SSM_GROUPS),
                 ssm_norm.reshape(SSM_GROUPS, D_INNER // SSM_GROUPS))
    return y.reshape(b, s, D_INNER)


def _fwd_setup_inputs(seed: int = 0) -> dict:
    key = jax.random.key(seed)
    ks = jax.random.split(key, 32)
    f32 = jnp.float32
    nrm = lambda k, shape, fan_in: jax.random.normal(k, shape, f32) * fan_in ** -0.5
    gain = lambda k, dim: 1.0 + 0.02 * jax.random.normal(k, (DEPTH, dim), f32)
    dt0 = jnp.exp(jax.random.uniform(ks[12], (DEPTH, N_HEADS_SSM), f32)
                  * (np.log(0.1) - np.log(0.001)) + np.log(0.001))
    return {
        "x": jax.random.normal(ks[0], (BATCH, SEQ, D_MODEL), f32),
        "p": jax.random.normal(ks[1], (DEPTH, BATCH, SEQ, PLE_DIM), f32),
        "positions": jnp.broadcast_to(jnp.arange(SEQ, dtype=jnp.int32), (BATCH, SEQ)),
        "mix_norm_pre": gain(ks[2], D_MODEL),
        "mix_norm_post": gain(ks[3], D_MODEL),
        "w_in": nrm(ks[4], (DEPTH, D_MODEL, D_IN_PROJ), D_MODEL),
        "q_norm": gain(ks[5], Q_LORA),
        "w_uq": nrm(ks[6], (DEPTH, Q_LORA, N_HEADS_MLA * QK_DIM), Q_LORA),
        "kv_norm": gain(ks[7], KV_LORA),
        "w_ukv": nrm(ks[8], (DEPTH, KV_LORA, N_HEADS_MLA * (QK_NOPE + V_DIM)), KV_LORA),
        "conv_w": nrm(ks[9], (DEPTH, CONV_WIDTH, CONV_DIM), CONV_WIDTH),
        "conv_b": 0.01 * jax.random.normal(ks[10], (DEPTH, CONV_DIM), f32),
        "dt_bias": dt0 + jnp.log(-jnp.expm1(-dt0)),
        "a_log": jnp.log(jax.random.uniform(ks[11], (DEPTH, N_HEADS_SSM), f32, 1.0, 16.0)),
        "d_skip": 1.0 + 0.02 * jax.random.normal(ks[13], (DEPTH, N_HEADS_SSM), f32),
        "ssm_norm": gain(ks[14], D_INNER),
        "w_attn_o": nrm(ks[15], (DEPTH, N_HEADS_MLA * V_DIM, D_MODEL), N_HEADS_MLA * V_DIM),
        "w_ssm_o": nrm(ks[16], (DEPTH, D_INNER, D_MODEL), D_INNER),
        "w_out": nrm(ks[17], (DEPTH, D_MODEL, D_MODEL), D_MODEL),
        "ffn_norm_pre": gain(ks[18], D_MODEL),
        "ffn_norm_post": gain(ks[19], D_MODEL),
        "w_gate": nrm(ks[20], (DEPTH, D_MODEL, D_FF), D_MODEL),
        "w_up": nrm(ks[21], (DEPTH, D_MODEL, D_FF), D_MODEL),
        "w_down": nrm(ks[22], (DEPTH, D_FF, D_MODEL), D_FF),
        "ple_norm_pre": gain(ks[23], D_MODEL),
        "ple_norm_post": gain(ks[24], D_MODEL),
        "w_ple_gate": nrm(ks[25], (DEPTH, D_MODEL, D_MODEL), D_MODEL),
        "w_ple": nrm(ks[26], (DEPTH, PLE_DIM, D_MODEL), PLE_DIM),
    }


def _fwd_reference(x, p, positions, mix_norm_pre, mix_norm_post, w_in, q_norm, w_uq, kv_norm, w_ukv,
              conv_w, conv_b, dt_bias, a_log, d_skip, ssm_norm, w_attn_o, w_ssm_o, w_out,
              ffn_norm_pre, ffn_norm_post, w_gate, w_up, w_down,
              ple_norm_pre, ple_norm_post, w_ple_gate, w_ple):
    cos, sin = rotary_tables(positions, QK_ROPE)
    h = x
    for i in range(DEPTH):
        u = rms_norm(h, mix_norm_pre[i])
        proj = u @ w_in[i]
        c_q, c_kv, k_r, z, xbc, dt_raw, g_attn, g_ssm = split_cols(proj, IN_SPLITS)
        attn = mla_branch(c_q, c_kv, k_r, cos, sin, q_norm[i], w_uq[i], kv_norm[i], w_ukv[i])
        ssm = mamba2_branch(z, xbc, dt_raw, conv_w[i], conv_b[i], dt_bias[i], a_log[i],
                            d_skip[i], ssm_norm[i])
        mixed = jax.nn.sigmoid(g_attn) * (attn @ w_attn_o[i]) + jax.nn.sigmoid(g_ssm) * (ssm @ w_ssm_o[i])
        h = h + rms_norm(mixed @ w_out[i], mix_norm_post[i])
        f = rms_norm(h, ffn_norm_pre[i])
        f = (jax.nn.silu(f @ w_gate[i]) * (f @ w_up[i])) @ w_down[i]
        h = h + rms_norm(f, ffn_norm_post[i])
        gate = jax.nn.sigmoid(rms_norm(h, ple_norm_pre[i]) @ w_ple_gate[i])
        e = (p[i].astype(h.dtype) @ w_ple[i]) * gate
        h = h + rms_norm(e, ple_norm_post[i])
    return h


import jax as _jax
import jax.numpy as _jnp

TWIN_FORMAT = 'train_step'
FWD_PARAMS = ['x', 'p', 'positions', 'mix_norm_pre', 'mix_norm_post', 'w_in', 'q_norm', 'w_uq', 'kv_norm', 'w_ukv', 'conv_w', 'conv_b', 'dt_bias', 'a_log', 'd_skip', 'ssm_norm', 'w_attn_o', 'w_ssm_o', 'w_out', 'ffn_norm_pre', 'ffn_norm_post', 'w_gate', 'w_up', 'w_down', 'ple_norm_pre', 'ple_norm_post', 'w_ple_gate', 'w_ple']
TWIN_WEIGHTS = ['mix_norm_pre', 'mix_norm_post', 'w_in', 'q_norm', 'w_uq', 'kv_norm', 'w_ukv', 'conv_w', 'conv_b', 'dt_bias', 'a_log', 'd_skip', 'ssm_norm', 'w_attn_o', 'w_ssm_o', 'w_out', 'ffn_norm_pre', 'ffn_norm_post', 'w_gate', 'w_up', 'w_down', 'ple_norm_pre', 'ple_norm_post', 'w_ple_gate', 'w_ple']
TWIN_DIFF_INPUT = 'x'
TWIN_INPUTS = ['x', 'p', 'positions', 'mix_norm_pre', 'mix_norm_post', 'w_in', 'q_norm', 'w_uq', 'kv_norm', 'w_ukv', 'conv_w', 'conv_b', 'dt_bias', 'a_log', 'd_skip', 'ssm_norm', 'w_attn_o', 'w_ssm_o', 'w_out', 'ffn_norm_pre', 'ffn_norm_post', 'w_gate', 'w_up', 'w_down', 'ple_norm_pre', 'ple_norm_post', 'w_ple_gate', 'w_ple', 'loss_target', 'm_mix_norm_pre', 'm_mix_norm_post', 'm_w_in', 'm_q_norm', 'm_w_uq', 'm_kv_norm', 'm_w_ukv', 'm_conv_w', 'm_conv_b', 'm_dt_bias', 'm_a_log', 'm_d_skip', 'm_ssm_norm', 'm_w_attn_o', 'm_w_ssm_o', 'm_w_out', 'm_ffn_norm_pre', 'm_ffn_norm_post', 'm_w_gate', 'm_w_up', 'm_w_down', 'm_ple_norm_pre', 'm_ple_norm_post', 'm_w_ple_gate', 'm_w_ple', 'v_mix_norm_pre', 'v_mix_norm_post', 'v_w_in', 'v_q_norm', 'v_w_uq', 'v_kv_norm', 'v_w_ukv', 'v_conv_w', 'v_conv_b', 'v_dt_bias', 'v_a_log', 'v_d_skip', 'v_ssm_norm', 'v_w_attn_o', 'v_w_ssm_o', 'v_w_out', 'v_ffn_norm_pre', 'v_ffn_norm_post', 'v_w_gate', 'v_w_up', 'v_w_down', 'v_ple_norm_pre', 'v_ple_norm_post', 'v_w_ple_gate', 'v_w_ple']
TWIN_OUTPUTS = ['loss', 'grad_x', 'grad_mix_norm_pre', 'grad_mix_norm_post', 'grad_w_in', 'grad_q_norm', 'grad_w_uq', 'grad_kv_norm', 'grad_w_ukv', 'grad_conv_w', 'grad_conv_b', 'grad_dt_bias', 'grad_a_log', 'grad_d_skip', 'grad_ssm_norm', 'grad_w_attn_o', 'grad_w_ssm_o', 'grad_w_out', 'grad_ffn_norm_pre', 'grad_ffn_norm_post', 'grad_w_gate', 'grad_w_up', 'grad_w_down', 'grad_ple_norm_pre', 'grad_ple_norm_post', 'grad_w_ple_gate', 'grad_w_ple', 'delta_mix_norm_pre', 'delta_mix_norm_post', 'delta_w_in', 'delta_q_norm', 'delta_w_uq', 'delta_kv_norm', 'delta_w_ukv', 'delta_conv_w', 'delta_conv_b', 'delta_dt_bias', 'delta_a_log', 'delta_d_skip', 'delta_ssm_norm', 'delta_w_attn_o', 'delta_w_ssm_o', 'delta_w_out', 'delta_ffn_norm_pre', 'delta_ffn_norm_post', 'delta_w_gate', 'delta_w_up', 'delta_w_down', 'delta_ple_norm_pre', 'delta_ple_norm_post', 'delta_w_ple_gate', 'delta_w_ple', 'new_m_mix_norm_pre', 'new_m_mix_norm_post', 'new_m_w_in', 'new_m_q_norm', 'new_m_w_uq', 'new_m_kv_norm', 'new_m_w_ukv', 'new_m_conv_w', 'new_m_conv_b', 'new_m_dt_bias', 'new_m_a_log', 'new_m_d_skip', 'new_m_ssm_norm', 'new_m_w_attn_o', 'new_m_w_ssm_o', 'new_m_w_out', 'new_m_ffn_norm_pre', 'new_m_ffn_norm_post', 'new_m_w_gate', 'new_m_w_up', 'new_m_w_down', 'new_m_ple_norm_pre', 'new_m_ple_norm_post', 'new_m_w_ple_gate', 'new_m_w_ple', 'new_v_mix_norm_pre', 'new_v_mix_norm_post', 'new_v_w_in', 'new_v_q_norm', 'new_v_w_uq', 'new_v_kv_norm', 'new_v_w_ukv', 'new_v_conv_w', 'new_v_conv_b', 'new_v_dt_bias', 'new_v_a_log', 'new_v_d_skip', 'new_v_ssm_norm', 'new_v_w_attn_o', 'new_v_w_ssm_o', 'new_v_w_out', 'new_v_ffn_norm_pre', 'new_v_ffn_norm_post', 'new_v_w_gate', 'new_v_w_up', 'new_v_w_down', 'new_v_ple_norm_pre', 'new_v_ple_norm_post', 'new_v_w_ple_gate', 'new_v_w_ple']
TWIN_LEAF_KINDS = {'loss': 'loss', 'grad_x': 'grad_x', 'grad_mix_norm_pre': 'grad_w', 'grad_mix_norm_post': 'grad_w', 'grad_w_in': 'grad_w', 'grad_q_norm': 'grad_w', 'grad_w_uq': 'grad_w', 'grad_kv_norm': 'grad_w', 'grad_w_ukv': 'grad_w', 'grad_conv_w': 'grad_w', 'grad_conv_b': 'grad_w', 'grad_dt_bias': 'grad_w', 'grad_a_log': 'grad_w', 'grad_d_skip': 'grad_w', 'grad_ssm_norm': 'grad_w', 'grad_w_attn_o': 'grad_w', 'grad_w_ssm_o': 'grad_w', 'grad_w_out': 'grad_w', 'grad_ffn_norm_pre': 'grad_w', 'grad_ffn_norm_post': 'grad_w', 'grad_w_gate': 'grad_w', 'grad_w_up': 'grad_w', 'grad_w_down': 'grad_w', 'grad_ple_norm_pre': 'grad_w', 'grad_ple_norm_post': 'grad_w', 'grad_w_ple_gate': 'grad_w', 'grad_w_ple': 'grad_w', 'delta_mix_norm_pre': 'delta_w', 'delta_mix_norm_post': 'delta_w', 'delta_w_in': 'delta_w', 'delta_q_norm': 'delta_w', 'delta_w_uq': 'delta_w', 'delta_kv_norm': 'delta_w', 'delta_w_ukv': 'delta_w', 'delta_conv_w': 'delta_w', 'delta_conv_b': 'delta_w', 'delta_dt_bias': 'delta_w', 'delta_a_log': 'delta_w', 'delta_d_skip': 'delta_w', 'delta_ssm_norm': 'delta_w', 'delta_w_attn_o': 'delta_w', 'delta_w_ssm_o': 'delta_w', 'delta_w_out': 'delta_w', 'delta_ffn_norm_pre': 'delta_w', 'delta_ffn_norm_post': 'delta_w', 'delta_w_gate': 'delta_w', 'delta_w_up': 'delta_w', 'delta_w_down': 'delta_w', 'delta_ple_norm_pre': 'delta_w', 'delta_ple_norm_post': 'delta_w', 'delta_w_ple_gate': 'delta_w', 'delta_w_ple': 'delta_w', 'new_m_mix_norm_pre': 'new_m', 'new_m_mix_norm_post': 'new_m', 'new_m_w_in': 'new_m', 'new_m_q_norm': 'new_m', 'new_m_w_uq': 'new_m', 'new_m_kv_norm': 'new_m', 'new_m_w_ukv': 'new_m', 'new_m_conv_w': 'new_m', 'new_m_conv_b': 'new_m', 'new_m_dt_bias': 'new_m', 'new_m_a_log': 'new_m', 'new_m_d_skip': 'new_m', 'new_m_ssm_norm': 'new_m', 'new_m_w_attn_o': 'new_m', 'new_m_w_ssm_o': 'new_m', 'new_m_w_out': 'new_m', 'new_m_ffn_norm_pre': 'new_m', 'new_m_ffn_norm_post': 'new_m', 'new_m_w_gate': 'new_m', 'new_m_w_up': 'new_m', 'new_m_w_down': 'new_m', 'new_m_ple_norm_pre': 'new_m', 'new_m_ple_norm_post': 'new_m', 'new_m_w_ple_gate': 'new_m', 'new_m_w_ple': 'new_m', 'new_v_mix_norm_pre': 'new_v', 'new_v_mix_norm_post': 'new_v', 'new_v_w_in': 'new_v', 'new_v_q_norm': 'new_v', 'new_v_w_uq': 'new_v', 'new_v_kv_norm': 'new_v', 'new_v_w_ukv': 'new_v', 'new_v_conv_w': 'new_v', 'new_v_conv_b': 'new_v', 'new_v_dt_bias': 'new_v', 'new_v_a_log': 'new_v', 'new_v_d_skip': 'new_v', 'new_v_ssm_norm': 'new_v', 'new_v_w_attn_o': 'new_v', 'new_v_w_ssm_o': 'new_v', 'new_v_w_out': 'new_v', 'new_v_ffn_norm_pre': 'new_v', 'new_v_ffn_norm_post': 'new_v', 'new_v_w_gate': 'new_v', 'new_v_w_up': 'new_v', 'new_v_w_down': 'new_v', 'new_v_ple_norm_pre': 'new_v', 'new_v_ple_norm_post': 'new_v', 'new_v_w_ple_gate': 'new_v', 'new_v_w_ple': 'new_v'}


def _forward(args):
    return _fwd_reference(*[args[k] for k in FWD_PARAMS])


def _output_shape():
    def fwd():
        inp = _fwd_setup_inputs(0)
        return _fwd_reference(*[inp[k] for k in FWD_PARAMS])
    out = _jax.eval_shape(fwd)
    return out.shape, out.dtype

N_MICROBATCH = 1
ADAM_LR = 0.001
ADAM_B1 = 0.9
ADAM_B2 = 0.999
ADAM_EPS = 1e-08
ADAM_WD = 0.01
ADAM_STEP = 10
PER_EXAMPLE_BATCH_AXIS = {'x': 0, 'p': 1, 'positions': 0, 'loss_target': 0}
SHARED_INPUTS = []
_WEIGHT_DTYPES = {'mix_norm_pre': _jnp.float32, 'mix_norm_post': _jnp.float32, 'w_in': _jnp.float32, 'q_norm': _jnp.float32, 'w_uq': _jnp.float32, 'kv_norm': _jnp.float32, 'w_ukv': _jnp.float32, 'conv_w': _jnp.float32, 'conv_b': _jnp.float32, 'dt_bias': _jnp.float32, 'a_log': _jnp.float32, 'd_skip': _jnp.float32, 'ssm_norm': _jnp.float32, 'w_attn_o': _jnp.float32, 'w_ssm_o': _jnp.float32, 'w_out': _jnp.float32, 'ffn_norm_pre': _jnp.float32, 'ffn_norm_post': _jnp.float32, 'w_gate': _jnp.float32, 'w_up': _jnp.float32, 'w_down': _jnp.float32, 'ple_norm_pre': _jnp.float32, 'ple_norm_post': _jnp.float32, 'w_ple_gate': _jnp.float32, 'w_ple': _jnp.float32}
MOMENT_SCALE = {'mix_norm_pre': 6.205157e-01, 'mix_norm_post': 3.199440e+01, 'w_in': 2.144063e-01, 'q_norm': 1.233515e-01, 'w_uq': 5.121376e-02, 'kv_norm': 1.840255e-01, 'w_ukv': 6.450968e-02, 'conv_w': 2.603692e-01, 'conv_b': 5.817460e-01, 'dt_bias': 5.760677e-01, 'a_log': 1.099379e+00, 'd_skip': 1.660726e+00, 'ssm_norm': 3.845523e-01, 'w_attn_o': 7.485330e-02, 'w_ssm_o': 5.173816e-01, 'w_out': 5.361518e-01, 'ffn_norm_pre': 5.091808e-01, 'ffn_norm_post': 3.198131e+01, 'w_gate': 1.785058e-01, 'w_up': 2.502319e-01, 'w_down': 4.157383e-01, 'ple_norm_pre': 1.074879e-01, 'ple_norm_post': 3.217916e+01, 'w_ple_gate': 1.013519e-01, 'w_ple': 2.535256e-01}


def _to_microbatches(a, axis):
    t = _jnp.moveaxis(a, axis, 0)
    t = t.reshape((N_MICROBATCH, t.shape[0] // N_MICROBATCH) + t.shape[1:])
    return _jnp.moveaxis(t, 1, axis + 1)


def setup_inputs(seed: int = 0) -> dict:
    inp = _fwd_setup_inputs(seed)
    key = _jax.random.fold_in(_jax.random.key(seed), 7919)
    shape, _ = _output_shape()
    out = dict(inp)
    out["loss_target"] = _jax.random.normal(_jax.random.fold_in(key, 0), shape, _jnp.float32)
    for i, name in enumerate(TWIN_WEIGHTS):
        w = inp[name].astype(_jnp.float32)
        if MOMENT_SCALE is None:
            s = _jnp.sqrt(_jnp.mean(_jnp.square(w)) + 1e-30)
        else:
            s = MOMENT_SCALE[name]
        km, kv = _jax.random.split(_jax.random.fold_in(key, i + 1))
        out[name] = w
        out["m_" + name] = s * _jax.random.normal(km, w.shape, _jnp.float32)
        out["v_" + name] = (s * s) * _jax.random.uniform(kv, w.shape, _jnp.float32, 0.5, 1.5)
    if N_MICROBATCH > 1:
        for name, axis in PER_EXAMPLE_BATCH_AXIS.items():
            out[name] = _to_microbatches(out[name], axis)
    return {'x': out['x'], 'p': out['p'], 'positions': out['positions'], 'mix_norm_pre': out['mix_norm_pre'], 'mix_norm_post': out['mix_norm_post'], 'w_in': out['w_in'], 'q_norm': out['q_norm'], 'w_uq': out['w_uq'], 'kv_norm': out['kv_norm'], 'w_ukv': out['w_ukv'], 'conv_w': out['conv_w'], 'conv_b': out['conv_b'], 'dt_bias': out['dt_bias'], 'a_log': out['a_log'], 'd_skip': out['d_skip'], 'ssm_norm': out['ssm_norm'], 'w_attn_o': out['w_attn_o'], 'w_ssm_o': out['w_ssm_o'], 'w_out': out['w_out'], 'ffn_norm_pre': out['ffn_norm_pre'], 'ffn_norm_post': out['ffn_norm_post'], 'w_gate': out['w_gate'], 'w_up': out['w_up'], 'w_down': out['w_down'], 'ple_norm_pre': out['ple_norm_pre'], 'ple_norm_post': out['ple_norm_post'], 'w_ple_gate': out['w_ple_gate'], 'w_ple': out['w_ple'], 'loss_target': out['loss_target'], 'm_mix_norm_pre': out['m_mix_norm_pre'], 'm_mix_norm_post': out['m_mix_norm_post'], 'm_w_in': out['m_w_in'], 'm_q_norm': out['m_q_norm'], 'm_w_uq': out['m_w_uq'], 'm_kv_norm': out['m_kv_norm'], 'm_w_ukv': out['m_w_ukv'], 'm_conv_w': out['m_conv_w'], 'm_conv_b': out['m_conv_b'], 'm_dt_bias': out['m_dt_bias'], 'm_a_log': out['m_a_log'], 'm_d_skip': out['m_d_skip'], 'm_ssm_norm': out['m_ssm_norm'], 'm_w_attn_o': out['m_w_attn_o'], 'm_w_ssm_o': out['m_w_ssm_o'], 'm_w_out': out['m_w_out'], 'm_ffn_norm_pre': out['m_ffn_norm_pre'], 'm_ffn_norm_post': out['m_ffn_norm_post'], 'm_w_gate': out['m_w_gate'], 'm_w_up': out['m_w_up'], 'm_w_down': out['m_w_down'], 'm_ple_norm_pre': out['m_ple_norm_pre'], 'm_ple_norm_post': out['m_ple_norm_post'], 'm_w_ple_gate': out['m_w_ple_gate'], 'm_w_ple': out['m_w_ple'], 'v_mix_norm_pre': out['v_mix_norm_pre'], 'v_mix_norm_post': out['v_mix_norm_post'], 'v_w_in': out['v_w_in'], 'v_q_norm': out['v_q_norm'], 'v_w_uq': out['v_w_uq'], 'v_kv_norm': out['v_kv_norm'], 'v_w_ukv': out['v_w_ukv'], 'v_conv_w': out['v_conv_w'], 'v_conv_b': out['v_conv_b'], 'v_dt_bias': out['v_dt_bias'], 'v_a_log': out['v_a_log'], 'v_d_skip': out['v_d_skip'], 'v_ssm_norm': out['v_ssm_norm'], 'v_w_attn_o': out['v_w_attn_o'], 'v_w_ssm_o': out['v_w_ssm_o'], 'v_w_out': out['v_w_out'], 'v_ffn_norm_pre': out['v_ffn_norm_pre'], 'v_ffn_norm_post': out['v_ffn_norm_post'], 'v_w_gate': out['v_w_gate'], 'v_w_up': out['v_w_up'], 'v_w_down': out['v_w_down'], 'v_ple_norm_pre': out['v_ple_norm_pre'], 'v_ple_norm_post': out['v_ple_norm_post'], 'v_w_ple_gate': out['v_w_ple_gate'], 'v_w_ple': out['v_w_ple']}


def _loss(weights, diff, rest, loss_target):
    with _jax.named_scope("forward"):
        args = {**rest, TWIN_DIFF_INPUT: diff, **{k: w.astype(_WEIGHT_DTYPES[k]) for k, w in weights.items()}}
        y = _forward(args)
    with _jax.named_scope("loss_head"):
        err = _jnp.square(y.astype(_jnp.float32) - loss_target)
        return 0.5 * _jnp.sum(_jnp.mean(err, axis=-1)) if err.ndim else 0.5 * err


def _adamw(w, g, m, v):
    m = ADAM_B1 * m + (1.0 - ADAM_B1) * g
    v = ADAM_B2 * v + (1.0 - ADAM_B2) * _jnp.square(g)
    m_hat = m / (1.0 - ADAM_B1 ** ADAM_STEP)
    v_hat = v / (1.0 - ADAM_B2 ** ADAM_STEP)
    delta = -ADAM_LR * (m_hat / (_jnp.sqrt(v_hat) + ADAM_EPS) + ADAM_WD * w)
    return delta, m, v


def reference(x, p, positions, mix_norm_pre, mix_norm_post, w_in, q_norm, w_uq, kv_norm, w_ukv, conv_w, conv_b, dt_bias, a_log, d_skip, ssm_norm, w_attn_o, w_ssm_o, w_out, ffn_norm_pre, ffn_norm_post, w_gate, w_up, w_down, ple_norm_pre, ple_norm_post, w_ple_gate, w_ple, loss_target, m_mix_norm_pre, m_mix_norm_post, m_w_in, m_q_norm, m_w_uq, m_kv_norm, m_w_ukv, m_conv_w, m_conv_b, m_dt_bias, m_a_log, m_d_skip, m_ssm_norm, m_w_attn_o, m_w_ssm_o, m_w_out, m_ffn_norm_pre, m_ffn_norm_post, m_w_gate, m_w_up, m_w_down, m_ple_norm_pre, m_ple_norm_post, m_w_ple_gate, m_w_ple, v_mix_norm_pre, v_mix_norm_post, v_w_in, v_q_norm, v_w_uq, v_kv_norm, v_w_ukv, v_conv_w, v_conv_b, v_dt_bias, v_a_log, v_d_skip, v_ssm_norm, v_w_attn_o, v_w_ssm_o, v_w_out, v_ffn_norm_pre, v_ffn_norm_post, v_w_gate, v_w_up, v_w_down, v_ple_norm_pre, v_ple_norm_post, v_w_ple_gate, v_w_ple):
    given = dict(x=x, p=p, positions=positions, mix_norm_pre=mix_norm_pre, mix_norm_post=mix_norm_post, w_in=w_in, q_norm=q_norm, w_uq=w_uq, kv_norm=kv_norm, w_ukv=w_ukv, conv_w=conv_w, conv_b=conv_b, dt_bias=dt_bias, a_log=a_log, d_skip=d_skip, ssm_norm=ssm_norm, w_attn_o=w_attn_o, w_ssm_o=w_ssm_o, w_out=w_out, ffn_norm_pre=ffn_norm_pre, ffn_norm_post=ffn_norm_post, w_gate=w_gate, w_up=w_up, w_down=w_down, ple_norm_pre=ple_norm_pre, ple_norm_post=ple_norm_post, w_ple_gate=w_ple_gate, w_ple=w_ple, loss_target=loss_target, m_mix_norm_pre=m_mix_norm_pre, m_mix_norm_post=m_mix_norm_post, m_w_in=m_w_in, m_q_norm=m_q_norm, m_w_uq=m_w_uq, m_kv_norm=m_kv_norm, m_w_ukv=m_w_ukv, m_conv_w=m_conv_w, m_conv_b=m_conv_b, m_dt_bias=m_dt_bias, m_a_log=m_a_log, m_d_skip=m_d_skip, m_ssm_norm=m_ssm_norm, m_w_attn_o=m_w_attn_o, m_w_ssm_o=m_w_ssm_o, m_w_out=m_w_out, m_ffn_norm_pre=m_ffn_norm_pre, m_ffn_norm_post=m_ffn_norm_post, m_w_gate=m_w_gate, m_w_up=m_w_up, m_w_down=m_w_down, m_ple_norm_pre=m_ple_norm_pre, m_ple_norm_post=m_ple_norm_post, m_w_ple_gate=m_w_ple_gate, m_w_ple=m_w_ple, v_mix_norm_pre=v_mix_norm_pre, v_mix_norm_post=v_mix_norm_post, v_w_in=v_w_in, v_q_norm=v_q_norm, v_w_uq=v_w_uq, v_kv_norm=v_kv_norm, v_w_ukv=v_w_ukv, v_conv_w=v_conv_w, v_conv_b=v_conv_b, v_dt_bias=v_dt_bias, v_a_log=v_a_log, v_d_skip=v_d_skip, v_ssm_norm=v_ssm_norm, v_w_attn_o=v_w_attn_o, v_w_ssm_o=v_w_ssm_o, v_w_out=v_w_out, v_ffn_norm_pre=v_ffn_norm_pre, v_ffn_norm_post=v_ffn_norm_post, v_w_gate=v_w_gate, v_w_up=v_w_up, v_w_down=v_w_down, v_ple_norm_pre=v_ple_norm_pre, v_ple_norm_post=v_ple_norm_post, v_w_ple_gate=v_w_ple_gate, v_w_ple=v_w_ple)
    weights = {n: given[n] for n in TWIN_WEIGHTS}
    shared = {n: given[n] for n in SHARED_INPUTS}
    per_example = {n: given[n] for n in ['x', 'p', 'positions']}
    grad_fn = _jax.value_and_grad(_loss, argnums=(0, 1))

    def one_microbatch(ex, loss_target):
        ex = dict(ex)
        diff = ex.pop(TWIN_DIFF_INPUT)
        return grad_fn(weights, diff, {**shared, **ex}, loss_target)

    if N_MICROBATCH == 1:
        loss, (grad_w, grad_x) = one_microbatch(per_example, given["loss_target"])
    else:
        def body(carry, xs):
            loss_sum, grad_sum = carry
            l_k, (gw_k, gx_k) = one_microbatch(xs[0], xs[1])
            with _jax.named_scope("update"):
                return (loss_sum + l_k, _jax.tree.map(_jnp.add, grad_sum, gw_k)), gx_k

        init = (_jnp.zeros((), _jnp.float32), _jax.tree.map(_jnp.zeros_like, weights))
        (loss, grad_w), grad_x = _jax.lax.scan(body, init, (per_example, given["loss_target"]))
    with _jax.named_scope("update"):
        delta_w, new_m, new_v = {}, {}, {}
        for n in TWIN_WEIGHTS:
            delta_w[n], new_m[n], new_v[n] = _adamw(weights[n], grad_w[n], given["m_" + n], given["v_" + n])
    return (loss, grad_x, *[grad_w[n] for n in TWIN_WEIGHTS], *[delta_w[n] for n in TWIN_WEIGHTS],
            *[new_m[n] for n in TWIN_WEIGHTS], *[new_v[n] for n in TWIN_WEIGHTS])
```

```python
import functools

import jax
import jax.numpy as jnp
from jax import lax
from jax.experimental import pallas as pl
from jax.experimental.pallas import tpu as pltpu

F32 = jnp.float32
BF16 = jnp.bfloat16

EPS = 1e-6
QK_NOPE = 128
QK_ROPE = 64
V_DIM = 128
QK_PAD = 256
ROPE_THETA = 10000.0
SSM_HEADDIM = 64
D_STATE = 128
CONV_WIDTH = 4
CHUNK = 256
ADAM_LR = 0.001
ADAM_B1 = 0.9
ADAM_B2 = 0.999
ADAM_EPS = 1e-08
ADAM_WD = 0.01
ADAM_STEP = 10

N_DEV = 8
LANES = 128
SUBLANES = 8
PACK_W = 1024
VMEM_LIMIT = 56 * 1024 * 1024
ROW_TILE_BYTES = 6 * 1024 * 1024
FLASH_TQ = 512
FLASH_TK = 512
MESH = pl.DeviceIdType.MESH


def _pick(dim, prefs):
    if dim <= prefs[0]:
        return dim
    for p in prefs:
        if dim % p == 0:
            return p
    return dim


def _params(sem):
    return pltpu.CompilerParams(dimension_semantics=sem, vmem_limit_bytes=VMEM_LIMIT)


def _dot(a, b):
    return lax.dot_general(a, b, (((1,), (0,)), ((), ())), preferred_element_type=F32)


def _dot_nt(a, b):
    return lax.dot_general(a, b, (((1,), (1,)), ((), ())), preferred_element_type=F32)


def _dot_tn(a, b):
    return lax.dot_general(a, b, (((0,), (0,)), ((), ())), preferred_element_type=F32)


def _mm(name, a, b, mode, out_dtype=F32, acc_in=None):
    if mode == "nn":
        (M, K), (K2, N) = a.shape, b.shape
    elif mode == "nt":
        (M, K), (N, K2) = a.shape, b.shape
    else:
        (K, M), (K2, N) = a.shape, b.shape
    assert K == K2, (name, a.shape, b.shape, mode)
    tm = _pick(M, (1024, 512, 256, 128))
    tn = _pick(N, (1024, 512, 256, 128))
    tk = _pick(K, (512, 256, 128))
    nk = K // tk
    dot = {"nn": _dot, "nt": _dot_nt, "tn": _dot_tn}[mode]
    has_acc = acc_in is not None

    def body(*refs):
        if has_acc:
            a_ref, b_ref, c_ref, o_ref, acc = refs
        else:
            a_ref, b_ref, o_ref, acc = refs
        k = pl.program_id(2)

        @pl.when(k == 0)
        def _():
            acc[...] = jnp.zeros_like(acc)

        acc[...] += dot(a_ref[...].astype(BF16), b_ref[...].astype(BF16))

        @pl.when(k == nk - 1)
        def _():
            r = acc[...]
            if has_acc:
                r = r + c_ref[...]
            o_ref[...] = r.astype(o_ref.dtype)

    if mode == "tn":
        a_spec = pl.BlockSpec((tk, tm), lambda i, j, k: (k, i))
    else:
        a_spec = pl.BlockSpec((tm, tk), lambda i, j, k: (i, k))
    if mode == "nt":
        b_spec = pl.BlockSpec((tn, tk), lambda i, j, k: (j, k))
    else:
        b_spec = pl.BlockSpec((tk, tn), lambda i, j, k: (k, j))
    o_spec = pl.BlockSpec((tm, tn), lambda i, j, k: (i, j))
    in_specs = [a_spec, b_spec] + ([o_spec] if has_acc else [])
    args = (a, b) + ((acc_in,) if has_acc else ())
    return pl.pallas_call(
        body, name=name, grid=(M // tm, N // tn, nk), in_specs=in_specs, out_specs=o_spec,
        out_shape=jax.ShapeDtypeStruct((M, N), out_dtype), scratch_shapes=[pltpu.VMEM((tm, tn), F32)],
        input_output_aliases=({2: 0} if has_acc and out_dtype == F32 else {}),
        compiler_params=_params(("parallel", "parallel", "arbitrary")),
    )(*args)


def _row_tile(n_rows, bytes_per_row):
    tr = 512
    while tr > SUBLANES and tr * bytes_per_row > ROW_TILE_BYTES:
        tr //= 2
    while n_rows % tr:
        tr //= 2
    return tr


def _acc_add(a_ref, v):
    if v.shape[0] == 1:
        a_ref[0:1, :] += v
    else:
        a_ref[...] += v.reshape(v.shape[0] // SUBLANES, SUBLANES, v.shape[1]).sum(axis=0)


def _rowwise(name, fn, rows, bcs, outs, accs=(), tr=None):
    n_rows = rows[0].shape[0]
    if tr is None:
        per_row = sum(r.shape[1] * r.dtype.itemsize for r in rows) + sum(w * jnp.dtype(d).itemsize for w, d in outs)
        tr = _row_tile(n_rows, per_row)
    n_r, n_b, n_o, n_a = len(rows), len(bcs), len(outs), len(accs)

    def body(*refs):
        ins = [r[...].astype(F32) for r in refs[: n_r + n_b]]
        res = fn(*ins)
        res = res if isinstance(res, (tuple, list)) else (res,)
        o_refs = refs[n_r + n_b: n_r + n_b + n_o]
        a_refs = refs[n_r + n_b + n_o:]
        for o, v in zip(o_refs, res[:n_o]):
            o[...] = v.astype(o.dtype)
        if n_a:
            @pl.when(pl.program_id(0) == 0)
            def _():
                for a in a_refs:
                    a[...] = jnp.zeros_like(a)

            for a, v in zip(a_refs, res[n_o:]):
                _acc_add(a, v)

    in_specs = [pl.BlockSpec((tr, r.shape[1]), lambda i: (i, 0)) for r in rows]
    in_specs += [pl.BlockSpec((1, b.shape[1]), lambda i: (0, 0)) for b in bcs]
    out_specs = [pl.BlockSpec((tr, w), lambda i: (i, 0)) for w, _ in outs]
    out_specs += [pl.BlockSpec((SUBLANES, w), lambda i: (0, 0)) for w in accs]
    out_shape = [jax.ShapeDtypeStruct((n_rows, w), d) for w, d in outs]
    out_shape += [jax.ShapeDtypeStruct((SUBLANES, w), F32) for w in accs]
    res = pl.pallas_call(
        body, name=name, grid=(n_rows // tr,), in_specs=in_specs, out_specs=out_specs, out_shape=out_shape,
        compiler_params=_params(("arbitrary",) if n_a else ("parallel",)),
    )(*rows, *bcs)
    return tuple(res)


def _rowwise_bwd(name, fn, rows, bcs, cts, need_rows, row_dtypes, need_bcs=None, adds=None, fwd_sums=(), tr=None):
    n_rows = rows[0].shape[0]
    adds = adds or {}
    need_bcs = list(range(len(bcs))) if need_bcs is None else list(need_bcs)
    ct_arrays = [c for c in cts if not isinstance(c, float)]
    add_keys = sorted(adds)
    add_arrays = [adds[k] for k in add_keys]
    if tr is None:
        per_row = sum(r.shape[1] * r.dtype.itemsize for r in list(rows) + ct_arrays + add_arrays)
        per_row += sum(rows[i].shape[1] * jnp.dtype(d).itemsize for i, d in zip(need_rows, row_dtypes))
        tr = _row_tile(n_rows, 2 * per_row)
    n_r, n_b, n_c, n_ad = len(rows), len(bcs), len(ct_arrays), len(add_arrays)
    n_go, n_gb, n_fs = len(need_rows), len(need_bcs), len(fwd_sums)

    def body(*refs):
        pos = 0
        r_t = [r[...].astype(F32) for r in refs[pos: pos + n_r]]
        pos += n_r
        b_t = [r[...].astype(F32) for r in refs[pos: pos + n_b]]
        pos += n_b
        c_t = [r[...].astype(F32) for r in refs[pos: pos + n_c]]
        pos += n_c
        ad_t = [r[...].astype(F32) for r in refs[pos: pos + n_ad]]
        pos += n_ad
        go_refs = refs[pos: pos + n_go]
        pos += n_go
        acc_refs = refs[pos:]

        def wrapped(*a):
            r = fn(*a)
            return tuple(r) if isinstance(r, (tuple, list)) else (r,)

        outs, vjp = jax.vjp(wrapped, *r_t, *b_t)
        it = iter(c_t)
        full = tuple(jnp.full(o.shape, c, F32) if isinstance(c, float) else next(it) for o, c in zip(outs, cts))
        grads = vjp(full)
        for o_ref, i in zip(go_refs, need_rows):
            g = grads[i]
            if i in adds:
                g = g + ad_t[add_keys.index(i)]
            o_ref[...] = g.astype(o_ref.dtype)

        @pl.when(pl.program_id(0) == 0)
        def _():
            for a in acc_refs:
                a[...] = jnp.zeros_like(a)

        for a, j in zip(acc_refs[:n_gb], need_bcs):
            _acc_add(a, grads[n_r + j])
        for a, j in zip(acc_refs[n_gb:], fwd_sums):
            a[0:1, :] += jnp.full((1, LANES), jnp.sum(outs[j]), F32)

    def row_spec(w):
        return pl.BlockSpec((tr, w), lambda i: (i, 0))

    in_specs = [row_spec(r.shape[1]) for r in rows]
    in_specs += [pl.BlockSpec((1, b.shape[1]), lambda i: (0, 0)) for b in bcs]
    in_specs += [row_spec(c.shape[1]) for c in ct_arrays] + [row_spec(a.shape[1]) for a in add_arrays]
    out_specs = [row_spec(rows[i].shape[1]) for i in need_rows]
    out_specs += [pl.BlockSpec((SUBLANES, bcs[j].shape[1]), lambda i: (0, 0)) for j in need_bcs]
    out_specs += [pl.BlockSpec((SUBLANES, LANES), lambda i: (0, 0)) for _ in fwd_sums]
    out_shape = [jax.ShapeDtypeStruct((n_rows, rows[i].shape[1]), d) for i, d in zip(need_rows, row_dtypes)]
    out_shape += [jax.ShapeDtypeStruct((SUBLANES, bcs[j].shape[1]), F32) for j in need_bcs]
    out_shape += [jax.ShapeDtypeStruct((SUBLANES, LANES), F32) for _ in fwd_sums]
    res = pl.pallas_call(
        body, name=name, grid=(n_rows // tr,), in_specs=in_specs, out_specs=out_specs, out_shape=out_shape,
        compiler_params=_params(("arbitrary",)),
    )(*rows, *bcs, *ct_arrays, *add_arrays)
    return tuple(res)


def _fold(acc):
    return jnp.sum(acc, axis=0, keepdims=True)


def _rms(x, w):
    return x * lax.rsqrt(jnp.mean(x * x, axis=-1, keepdims=True) + EPS) * w


def _sigmoid(x):
    return jax.nn.sigmoid(x)


def _silu(x):
    return x * _sigmoid(x)


def _log1p(u):
    series = u * (1.0 - u * (0.5 - u * (1.0 / 3.0 - u * 0.25)))
    return jnp.where(u < 0.01, series, jnp.log(1.0 + u))


def _softplus(x):
    return jnp.maximum(x, 0.0) + _log1p(jnp.exp(-jnp.abs(x)))


def _st_pre(x, w):
    return _rms(x, w)


def _st_lora_norms(dq):
    def fn(cqkv, qn, kvn):
        return _rms(cqkv[:, :dq], qn), _rms(cqkv[:, dq:], kvn)
    return fn


def _st_gated_norm(n_groups):
    def fn(y, xs, z, dskip, wn):
        yz = (y + dskip * xs) * _silu(z)
        gw = yz.shape[1] // n_groups
        parts = [_rms(yz[:, g * gw:(g + 1) * gw], wn[:, g * gw:(g + 1) * gw]) for g in range(n_groups)]
        return jnp.concatenate(parts, axis=1)
    return fn


def _st_mix(d):
    def fn(g, ao, so):
        return _sigmoid(g[:, :d]) * ao + _sigmoid(g[:, d:]) * so
    return fn


def _st_res_norm(h, y, w_post, w_pre):
    h2 = h + _rms(y, w_post)
    return h2, _rms(h2, w_pre)


def _st_swiglu(gt, up):
    return _silu(gt) * up


def _st_loss(pe, gl, h2, tgt, w_post):
    e = pe * _sigmoid(gl)
    diff = h2 + _rms(e, w_post) - tgt
    return 0.5 * jnp.mean(diff * diff, axis=-1, keepdims=True)


def _rope_tables(positions):
    half = QK_ROPE // 2
    inv_freq = ROPE_THETA ** (-jnp.arange(0, QK_ROPE, 2, dtype=F32) / QK_ROPE)
    ang = positions.astype(F32).reshape(-1, 1) * inv_freq
    cos, sin = jnp.cos(ang), jnp.sin(ang)
    n = ang.shape[0]
    z = lambda w: jnp.zeros((n, w), F32)
    c_tab = jnp.concatenate([jnp.ones((n, QK_NOPE), F32), cos, cos, z(QK_PAD - QK_NOPE - QK_ROPE)], axis=1)
    a_tab = jnp.concatenate([z(QK_NOPE), -sin, z(half), z(QK_PAD - QK_NOPE - QK_ROPE)], axis=1)
    b_tab = jnp.concatenate([z(QK_NOPE), z(half), sin, z(QK_PAD - QK_NOPE - QK_ROPE)], axis=1)
    return c_tab, a_tab, b_tab


def _rot(x, c, a, b):
    half = QK_ROPE // 2
    return x * c + pltpu.roll(x, QK_PAD - half, axis=1) * a + pltpu.roll(x, half, axis=1) * b


def _rot_t(g, c, a, b):
    half = QK_ROPE // 2
    return g * c + pltpu.roll(g * a, half, axis=1) + pltpu.roll(g * b, QK_PAD - half, axis=1)


def _st_rope(n_heads):
    def fn(qraw, kraw, sm, c, a, b):
        kpe = _rot(sm[:, :QK_PAD], c, a, b)
        q = [_rot(qraw[:, h * QK_PAD:(h + 1) * QK_PAD], c, a, b) for h in range(n_heads)]
        k = [kraw[:, h * QK_PAD:(h + 1) * QK_PAD] + kpe for h in range(n_heads)]
        return jnp.concatenate(q, axis=1), jnp.concatenate(k, axis=1)
    return fn


def _st_rope_bwd(n_heads):
    def fn(dq, dk, c, a, b):
        dqraw = [_rot_t(dq[:, h * QK_PAD:(h + 1) * QK_PAD], c, a, b) for h in range(n_heads)]
        dks = dk[:, :QK_PAD]
        for h in range(1, n_heads):
            dks = dks + dk[:, h * QK_PAD:(h + 1) * QK_PAD]
        return jnp.concatenate(dqraw, axis=1), _rot_t(dks, c, a, b)
    return fn


def _split3(x):
    h1 = x.astype(BF16)
    r1 = x - h1.astype(F32)
    h2 = r1.astype(BF16)
    h3 = (r1 - h2.astype(F32)).astype(BF16)
    return h1, h2, h3


def _tri_dot(tri, x):
    h1, h2, h3 = _split3(x)
    return (_dot(tri, h3) + _dot(tri, h2)) + _dot(tri, h1)


def _dot_tri(x, tri):
    h1, h2, h3 = _split3(x)
    return (_dot(h3, tri) + _dot(h2, tri)) + _dot(h1, tri)


def _st_dt(sm, bias, alog):
    x = sm[:, QK_PAD:] + bias
    dt = _softplus(x)
    return dt, dt * (-jnp.exp(alog))


def _st_dt_bwd(sm, ddt, dcum, bias, alog):
    n = sm.shape[0]
    i = lax.broadcasted_iota(jnp.int32, (n, n), 0)
    j = lax.broadcasted_iota(jnp.int32, (n, n), 1)
    upper = (j >= i).astype(BF16)
    dda = _tri_dot(upper, dcum)
    x = sm[:, QK_PAD:] + bias
    dt = _softplus(x)
    a = -jnp.exp(alog)
    draw = (ddt + dda * a) * _sigmoid(x)
    return draw, draw, dda * dt * a


def _conv_fwd(xbc, w, b):
    S, C = xbc.shape
    tr = _pick(S, (512, 256))
    tc = _pick(C, (1024, 512, 256, 128))
    hb = tr // SUBLANES

    def body(x_ref, halo_ref, w_ref, b_ref, c_ref, a_ref, ext):
        i = pl.program_id(1)
        halo = jnp.where(i == 0, 0.0, halo_ref[...])
        ext[0:SUBLANES, :] = halo
        ext[SUBLANES:, :] = x_ref[...]
        wv = w_ref[...]
        acc = b_ref[...] + wv[CONV_WIDTH - 1:CONV_WIDTH, :] * x_ref[...]
        for k in range(CONV_WIDTH - 1):
            off = SUBLANES - (CONV_WIDTH - 1) + k
            acc = acc + wv[k:k + 1, :] * ext[pl.ds(off, tr), :]
        c_ref[...] = acc
        a_ref[...] = _silu(acc)

    return pl.pallas_call(
        body, name="conv_fwd", grid=(C // tc, S // tr),
        in_specs=[pl.BlockSpec((tr, tc), lambda j, i: (i, j)),
                  pl.BlockSpec((SUBLANES, tc), lambda j, i: (jnp.maximum(i * hb - 1, 0), j)),
                  pl.BlockSpec((CONV_WIDTH, tc), lambda j, i: (0, j)),
                  pl.BlockSpec((1, tc), lambda j, i: (0, j))],
        out_specs=[pl.BlockSpec((tr, tc), lambda j, i: (i, j))] * 2,
        out_shape=[jax.ShapeDtypeStruct((S, C), F32)] * 2,
        scratch_shapes=[pltpu.VMEM((tr + SUBLANES, tc), F32)],
        compiler_params=_params(("parallel", "arbitrary")),
    )(xbc, xbc, w, b)


def _conv_bwd(xbc, dconv, w):
    S, C = xbc.shape
    tr = _pick(S, (512, 256))
    tc = _pick(C, (1024, 512, 256, 128))
    hb = tr // SUBLANES
    n_i = S // tr

    def body(x_ref, halo_ref, d_ref, dnext_ref, w_ref, dx_ref, dw_ref, ext, dext):
        i = pl.program_id(1)
        ext[0:SUBLANES, :] = jnp.where(i == 0, 0.0, halo_ref[...])
        ext[SUBLANES:, :] = x_ref[...]
        dext[0:tr, :] = d_ref[...]
        dext[tr:, :] = jnp.where(i == n_i - 1, 0.0, dnext_ref[...])
        wv = w_ref[...]
        d = d_ref[...]

        @pl.when(i == 0)
        def _():
            dw_ref[...] = jnp.zeros_like(dw_ref)

        dx = wv[CONV_WIDTH - 1:CONV_WIDTH, :] * d
        for k in range(CONV_WIDTH):
            if k < CONV_WIDTH - 1:
                dx = dx + wv[k:k + 1, :] * dext[pl.ds(CONV_WIDTH - 1 - k, tr), :]
                xs = ext[pl.ds(SUBLANES - (CONV_WIDTH - 1) + k, tr), :]
            else:
                xs = x_ref[...]
            prod = d * xs
            dw_ref[k * SUBLANES:(k + 1) * SUBLANES, :] += prod.reshape(tr // SUBLANES, SUBLANES, tc).sum(axis=0)
        dx_ref[...] = dx

    return pl.pallas_call(
        body, name="conv_bwd", grid=(C // tc, n_i),
        in_specs=[pl.BlockSpec((tr, tc), lambda j, i: (i, j)),
                  pl.BlockSpec((SUBLANES, tc), lambda j, i: (jnp.maximum(i * hb - 1, 0), j)),
                  pl.BlockSpec((tr, tc), lambda j, i: (i, j)),
                  pl.BlockSpec((SUBLANES, tc), lambda j, i: (jnp.minimum((i + 1) * hb, S // SUBLANES - 1), j)),
                  pl.BlockSpec((CONV_WIDTH, tc), lambda j, i: (0, j))],
        out_specs=[pl.BlockSpec((tr, tc), lambda j, i: (i, j)),
                   pl.BlockSpec((CONV_WIDTH * SUBLANES, tc), lambda j, i: (0, j))],
        out_shape=[jax.ShapeDtypeStruct((S, C), F32), jax.ShapeDtypeStruct((CONV_WIDTH * SUBLANES, C), F32)],
        scratch_shapes=[pltpu.VMEM((tr + SUBLANES, tc), F32), pltpu.VMEM((tr + SUBLANES, tc), F32)],
        compiler_params=_params(("parallel", "arbitrary")),
    )(xbc, xbc, dconv, dconv, w)


def _st_dconv(d_inner):
    def fn(xc, dxa, dxb, db_, dc_):
        s = _sigmoid(xc)
        g = jnp.concatenate([dxa + dxb, db_, dc_], axis=1) * (s * (1.0 + xc * (1.0 - s)))
        return g, g
    return fn


def _chunk_setup(b_ref, c_ref, dac_ref, dar_ref, L):
    ii = lax.broadcasted_iota(jnp.int32, (L, L), 0)
    jj = lax.broadcasted_iota(jnp.int32, (L, L), 1)
    tri = ii >= jj
    cum_c = _tri_dot(tri.astype(BF16), dac_ref[0])
    cum_r = _dot_tri(dar_ref[0], (ii <= jj).astype(BF16))
    bm = b_ref[...].astype(BF16)
    cm = c_ref[...].astype(BF16)
    return tri, cum_c, cum_r, bm, cm, _dot_nt(cm, bm)


def _ssd_specs(d_inner, n_groups, gw, L, rp, chunk_of):
    bb0 = d_inner // D_STATE
    cb0 = bb0 + n_groups
    return [pl.BlockSpec((L, gw), lambda g, c: (chunk_of(c), g)),
            pl.BlockSpec((L, D_STATE), lambda g, c: (chunk_of(c), bb0 + g)),
            pl.BlockSpec((L, D_STATE), lambda g, c: (chunk_of(c), cb0 + g)),
            pl.BlockSpec((1, L, LANES), lambda g, c: (g, chunk_of(c), 0)),
            pl.BlockSpec((1, L, LANES), lambda g, c: (g, chunk_of(c), 0)),
            pl.BlockSpec((1, rp, L), lambda g, c: (g, 0, chunk_of(c)))]


def _ssd_fwd(xbc_a, dt_col, da_col, da_row, d_inner, n_groups, R):
    S = xbc_a.shape[0]
    L = min(CHUNK, S)
    NC = S // L
    P, N = SSM_HEADDIM, D_STATE
    gw = R * P
    rp = da_row.shape[1]

    def body(x_ref, b_ref, c_ref, dt_ref, dac_ref, dar_ref, y_ref, st_ref, state):
        @pl.when(pl.program_id(1) == 0)
        def _():
            state[...] = jnp.zeros_like(state)

        st_ref[0, 0] = state[...]
        tri, cum_c, cum_r, bm, cm, gm = _chunk_setup(b_ref, c_ref, dac_ref, dar_ref, L)
        dt = dt_ref[0]
        for r in range(R):
            cc = cum_c[:, r:r + 1]
            cr = cum_r[r:r + 1, :]
            lam = jnp.exp(jnp.where(tri, cc - cr, -jnp.inf))
            m = (gm * lam).astype(BF16)
            x = x_ref[:, r * P:(r + 1) * P] * dt[:, r:r + 1]
            s_r = state[r * P:(r + 1) * P, :]
            y_off = _dot_nt(cm, s_r.astype(BF16)) * jnp.exp(cc)
            y_ref[:, r * P:(r + 1) * P] = _dot(m, x.astype(BF16)) + y_off
            last = cr[:, L - 1:L]
            xw = (x * jnp.exp(last - cc)).astype(BF16)
            state[r * P:(r + 1) * P, :] = s_r * jnp.exp(last) + _dot_tn(xw, bm)

    return pl.pallas_call(
        body, name="ssd_fwd", grid=(n_groups, NC),
        in_specs=_ssd_specs(d_inner, n_groups, gw, L, rp, lambda c: c),
        out_specs=[pl.BlockSpec((L, gw), lambda g, c: (c, g)),
                   pl.BlockSpec((1, 1, gw, N), lambda g, c: (g, c, 0, 0))],
        out_shape=[jax.ShapeDtypeStruct((S, d_inner), F32), jax.ShapeDtypeStruct((n_groups, NC, gw, N), F32)],
        scratch_shapes=[pltpu.VMEM((gw, N), F32)],
        compiler_params=_params(("parallel", "arbitrary")),
    )(xbc_a, xbc_a, xbc_a, dt_col, da_col, da_row)


def _ssd_bwd(xbc_a, dt_col, da_col, da_row, states, dy, d_inner, n_groups, R):
    S = xbc_a.shape[0]
    L = min(CHUNK, S)
    NC = S // L
    P, N = SSM_HEADDIM, D_STATE
    gw = R * P
    rp = da_row.shape[1]
    rev = lambda c: NC - 1 - c

    def body(x_ref, b_ref, c_ref, dt_ref, dac_ref, dar_ref, st_ref, dy_ref,
             dx_ref, db_ref, dc_ref, ddt_ref, dcc_ref, dcr_ref, dstate):
        @pl.when(pl.program_id(1) == 0)
        def _():
            dstate[...] = jnp.zeros_like(dstate)

        tri, cum_c, cum_r, bm, cm, gm = _chunk_setup(b_ref, c_ref, dac_ref, dar_ref, L)
        dt = dt_ref[0]
        lane = lax.broadcasted_iota(jnp.int32, (L, LANES), 1)
        sub = lax.broadcasted_iota(jnp.int32, (rp, L), 0)
        is_last = lax.broadcasted_iota(jnp.int32, (L, 1), 0) == L - 1
        d_g = jnp.zeros((L, L), F32)
        dc_acc = jnp.zeros((L, N), F32)
        db_acc = jnp.zeros((L, N), F32)
        ddt_out = jnp.zeros((L, LANES), F32)
        dcc_out = jnp.zeros((L, LANES), F32)
        dcr_out = jnp.zeros((rp, L), F32)
        for r in range(R):
            cc = cum_c[:, r:r + 1]
            cr = cum_r[r:r + 1, :]
            lam = jnp.exp(jnp.where(tri, cc - cr, -jnp.inf))
            m = gm * lam
            dtc = dt[:, r:r + 1]
            xh = x_ref[:, r * P:(r + 1) * P]
            x = xh * dtc
            xb = x.astype(BF16)
            d_y = dy_ref[:, r * P:(r + 1) * P]
            d_yb = d_y.astype(BF16)
            s_r = st_ref[0, 0, r * P:(r + 1) * P, :]
            s_rb = s_r.astype(BF16)
            ds_n = dstate[r * P:(r + 1) * P, :]
            ds_nb = ds_n.astype(BF16)
            e = jnp.exp(cc)
            last = cr[:, L - 1:L]
            e_last = jnp.exp(last)
            w = jnp.exp(last - cc)
            d_m = _dot_nt(d_yb, xb)
            d_x = _dot_tn(m.astype(BF16), d_yb)
            d_ye = (d_y * e).astype(BF16)
            dc_acc = dc_acc + _dot(d_ye, s_rb)
            ds_part = _dot_tn(d_ye, cm)
            y_off = _dot_nt(cm, s_rb) * e
            dcum = jnp.sum(d_y * y_off, axis=1, keepdims=True)
            d_xw = _dot_nt(bm, ds_nb)
            d_x = d_x + d_xw * w
            dw_w = jnp.sum(d_xw * x, axis=1, keepdims=True) * w
            db_acc = db_acc + _dot((x * w).astype(BF16), ds_nb)
            d_last = jnp.sum(ds_n * s_r, keepdims=True) * e_last + jnp.sum(dw_w, keepdims=True)
            dcum = dcum - dw_w
            dstate[r * P:(r + 1) * P, :] = e_last * ds_n + ds_part
            d_g = d_g + d_m * lam
            q = d_m * m
            dcum = dcum + jnp.sum(q, axis=1, keepdims=True) + jnp.where(is_last, d_last, 0.0)
            dcum_row = -jnp.sum(q, axis=0, keepdims=True)
            dx_ref[:, r * P:(r + 1) * P] = d_x * dtc
            ddt = jnp.sum(d_x * xh, axis=1, keepdims=True)
            ddt_out = ddt_out + jnp.where(lane == r, ddt, 0.0)
            dcc_out = dcc_out + jnp.where(lane == r, dcum, 0.0)
            dcr_out = dcr_out + jnp.where(sub == r, dcum_row, 0.0)
        d_gb = d_g.astype(BF16)
        dc_ref[...] = dc_acc + _dot(d_gb, bm)
        db_ref[...] = db_acc + _dot_tn(d_gb, cm)
        ddt_ref[0] = ddt_out
        dcc_ref[0] = dcc_out
        dcr_ref[0] = dcr_out

    gn = n_groups * N
    return pl.pallas_call(
        body, name="ssd_bwd", grid=(n_groups, NC),
        in_specs=_ssd_specs(d_inner, n_groups, gw, L, rp, rev) + [
            pl.BlockSpec((1, 1, gw, N), lambda g, c: (g, rev(c), 0, 0)),
            pl.BlockSpec((L, gw), lambda g, c: (rev(c), g))],
        out_specs=[pl.BlockSpec((L, gw), lambda g, c: (rev(c), g)),
                   pl.BlockSpec((L, N), lambda g, c: (rev(c), g)),
                   pl.BlockSpec((L, N), lambda g, c: (rev(c), g)),
                   pl.BlockSpec((1, L, LANES), lambda g, c: (g, rev(c), 0)),
                   pl.BlockSpec((1, L, LANES), lambda g, c: (g, rev(c), 0)),
                   pl.BlockSpec((1, rp, L), lambda g, c: (g, 0, rev(c)))],
        out_shape=[jax.ShapeDtypeStruct((S, d_inner), F32), jax.ShapeDtypeStruct((S, gn), F32),
                   jax.ShapeDtypeStruct((S, gn), F32), jax.ShapeDtypeStruct((n_groups, S, LANES), F32),
                   jax.ShapeDtypeStruct((n_groups, S, LANES), F32), jax.ShapeDtypeStruct((n_groups, rp, S), F32)],
        scratch_shapes=[pltpu.VMEM((gw, N), F32)],
        compiler_params=_params(("parallel", "arbitrary")),
    )(xbc_a, xbc_a, xbc_a, dt_col, da_col, da_row, states, dy)


def _attn_scale():
    return float(QK_NOPE + QK_ROPE) ** -0.5


def _masked_scores(q_ref, k_ref, i, j, tq, tk):
    s = _dot_nt(q_ref[...], k_ref[...]) * _attn_scale()
    rows = i * tq + lax.broadcasted_iota(jnp.int32, (tq, tk), 0)
    cols = j * tk + lax.broadcasted_iota(jnp.int32, (tq, tk), 1)
    return jnp.where(cols <= rows, s, -jnp.inf)


def _flash_fwd(q, k, v, n_heads):
    S = q.shape[0]
    tq, tk = min(FLASH_TQ, S), min(FLASH_TK, S)
    nk = S // tk
    last = lambda i: ((i + 1) * tq - 1) // tk

    def body(q_ref, k_ref, v_ref, o_ref, lse_ref, m_s, l_s, acc):
        i, j = pl.program_id(1), pl.program_id(2)

        @pl.when(j == 0)
        def _():
            m_s[...] = jnp.full_like(m_s, -jnp.inf)
            l_s[...] = jnp.zeros_like(l_s)
            acc[...] = jnp.zeros_like(acc)

        @pl.when(j <= last(i))
        def _():
            s = _masked_scores(q_ref, k_ref, i, j, tq, tk)
            m_new = jnp.maximum(m_s[...], jnp.max(s, axis=1, keepdims=True))
            alpha = jnp.exp(m_s[...] - m_new)
            p = jnp.exp(s - m_new)
            l_s[...] = alpha * l_s[...] + jnp.sum(p, axis=1, keepdims=True)
            acc[...] = alpha * acc[...] + _dot(p.astype(BF16), v_ref[...])
            m_s[...] = m_new

        @pl.when(j == nk - 1)
        def _():
            o_ref[...] = acc[...] / l_s[...]
            lse_ref[0] = m_s[...] + jnp.log(l_s[...])

    return pl.pallas_call(
        body, name="flash_fwd", grid=(n_heads, S // tq, nk),
        in_specs=[pl.BlockSpec((tq, QK_PAD), lambda h, i, j: (i, h)),
                  pl.BlockSpec((tk, QK_PAD), lambda h, i, j: (jnp.minimum(j, last(i)), h)),
                  pl.BlockSpec((tk, V_DIM), lambda h, i, j: (jnp.minimum(j, last(i)), h))],
        out_specs=[pl.BlockSpec((tq, V_DIM), lambda h, i, j: (i, h)),
                   pl.BlockSpec((1, tq, 1), lambda h, i, j: (h, i, 0))],
        out_shape=[jax.ShapeDtypeStruct((S, n_heads * V_DIM), F32), jax.ShapeDtypeStruct((n_heads, S, 1), F32)],
        scratch_shapes=[pltpu.VMEM((tq, 1), F32), pltpu.VMEM((tq, 1), F32), pltpu.VMEM((tq, V_DIM), F32)],
        compiler_params=_params(("parallel", "parallel", "arbitrary")),
    )(q, k, v)


def _flash_bwd_dq(q, k, v, do, lse, delta, n_heads):
    S = q.shape[0]
    tq, tk = min(FLASH_TQ, S), min(FLASH_TK, S)
    nk = S // tk
    last = lambda i: ((i + 1) * tq - 1) // tk

    def body(q_ref, k_ref, v_ref, do_ref, lse_ref, dl_ref, dq_ref, acc):
        i, j = pl.program_id(1), pl.program_id(2)

        @pl.when(j == 0)
        def _():
            acc[...] = jnp.zeros_like(acc)

        @pl.when(j <= last(i))
        def _():
            p = jnp.exp(_masked_scores(q_ref, k_ref, i, j, tq, tk) - lse_ref[0])
            dp = _dot_nt(do_ref[...].astype(BF16), v_ref[...])
            ds = p * (dp - dl_ref[0])
            acc[...] += _dot(ds.astype(BF16), k_ref[...])

        @pl.when(j == nk - 1)
        def _():
            dq_ref[...] = acc[...] * _attn_scale()

    kv = lambda h, i, j: (jnp.minimum(j, last(i)), h)
    return pl.pallas_call(
        body, name="flash_bwd_dq", grid=(n_heads, S // tq, nk),
        in_specs=[pl.BlockSpec((tq, QK_PAD), lambda h, i, j: (i, h)),
                  pl.BlockSpec((tk, QK_PAD), kv), pl.BlockSpec((tk, V_DIM), kv),
                  pl.BlockSpec((tq, V_DIM), lambda h, i, j: (i, h)),
                  pl.BlockSpec((1, tq, 1), lambda h, i, j: (h, i, 0)),
                  pl.BlockSpec((1, tq, 1), lambda h, i, j: (h, i, 0))],
        out_specs=pl.BlockSpec((tq, QK_PAD), lambda h, i, j: (i, h)),
        out_shape=jax.ShapeDtypeStruct((S, n_heads * QK_PAD), F32),
        scratch_shapes=[pltpu.VMEM((tq, QK_PAD), F32)],
        compiler_params=_params(("parallel", "parallel", "arbitrary")),
    )(q, k, v, do, lse, delta)


def _flash_bwd_dkv(q, k, v, do, lse, delta, n_heads):
    S = q.shape[0]
    tq, tk = min(FLASH_TQ, S), min(FLASH_TK, S)
    nq = S // tq
    first = lambda j: (j * tk) // tq

    def body(q_ref, k_ref, v_ref, do_ref, lse_ref, dl_ref, dk_ref, dv_ref, dk_acc, dv_acc):
        j, i = pl.program_id(1), pl.program_id(2)

        @pl.when(i == 0)
        def _():
            dk_acc[...] = jnp.zeros_like(dk_acc)
            dv_acc[...] = jnp.zeros_like(dv_acc)

        @pl.when(i >= first(j))
        def _():
            p = jnp.exp(_masked_scores(q_ref, k_ref, i, j, tq, tk) - lse_ref[0])
            dob = do_ref[...].astype(BF16)
            dv_acc[...] += _dot_tn(p.astype(BF16), dob)
            dp = _dot_nt(dob, v_ref[...])
            ds = p * (dp - dl_ref[0])
            dk_acc[...] += _dot_tn(ds.astype(BF16), q_ref[...])

        @pl.when(i == nq - 1)
        def _():
            dk_ref[...] = dk_acc[...] * _attn_scale()
            dv_ref[...] = dv_acc[...]

    qi = lambda h, j, i: (jnp.maximum(i, first(j)), h)
    qi3 = lambda h, j, i: (h, jnp.maximum(i, first(j)), 0)
    return pl.pallas_call(
        body, name="flash_bwd_dkv", grid=(n_heads, S // tk, nq),
        in_specs=[pl.BlockSpec((tq, QK_PAD), qi),
                  pl.BlockSpec((tk, QK_PAD), lambda h, j, i: (j, h)),
                  pl.BlockSpec((tk, V_DIM), lambda h, j, i: (j, h)),
                  pl.BlockSpec((tq, V_DIM), qi),
                  pl.BlockSpec((1, tq, 1), qi3), pl.BlockSpec((1, tq, 1), qi3)],
        out_specs=[pl.BlockSpec((tk, QK_PAD), lambda h, j, i: (j, h)),
                   pl.BlockSpec((tk, V_DIM), lambda h, j, i: (j, h))],
        out_shape=[jax.ShapeDtypeStruct((S, n_heads * QK_PAD), F32), jax.ShapeDtypeStruct((S, n_heads * V_DIM), F32)],
        scratch_shapes=[pltpu.VMEM((tk, QK_PAD), F32), pltpu.VMEM((tk, V_DIM), F32)],
        compiler_params=_params(("parallel", "parallel", "arbitrary")),
    )(q, k, v, do, lse, delta)


def _st_delta(n_heads):
    def fn(do, o):
        prod = do * o
        lane = lax.broadcasted_iota(jnp.int32, (do.shape[0], LANES), 1)
        out = jnp.zeros((do.shape[0], LANES), F32)
        for h in range(n_heads):
            out = out + jnp.where(lane == h, jnp.sum(prod[:, h * V_DIM:(h + 1) * V_DIM], axis=1, keepdims=True), 0.0)
        return out
    return fn


def _pad_cols(w, width):
    return jnp.pad(w, ((0, 0), (0, width - w.shape[1])))


def _dims(x, p, q_norm, kv_norm, w_uq, dt_bias, ssm_norm, conv_b, w_gate):
    d = dict(S=x.shape[0], D=x.shape[1], PLE=p.shape[1], DQ=q_norm.shape[1], DKV=kv_norm.shape[1],
             NH=w_uq.shape[1] // (QK_NOPE + QK_ROPE), NHS=dt_bias.shape[1], DI=ssm_norm.shape[1],
             CONV=conv_b.shape[1], FF=w_gate.shape[1])
    d["G"] = (d["CONV"] - d["DI"]) // (2 * D_STATE)
    d["R"] = d["NHS"] // d["G"]
    return d


def _local_step(x, p, positions, target, small, conv_w, wts):
    dm = _dims(x, p, small["q_norm"], small["kv_norm"], wts["w_uq"], small["dt_bias"], small["ssm_norm"],
               small["conv_b"], wts["w_gate"])
    S, D, DQ, DKV, NH, NHS, DI, CONV, G, R = (dm[k] for k in ("S", "D", "DQ", "DKV", "NH", "NHS", "DI", "CONV", "G", "R"))
    rp = -(-R // SUBLANES) * SUBLANES
    L = min(CHUNK, S)

    w_in = wts["w_in"]
    o = [0]
    for n in (DQ, DKV, QK_ROPE, DI, CONV, NHS, D, D):
        o.append(o[-1] + n)
    w_cqkv, w_kr, w_z, w_xbc, w_dt, w_g = (w_in[:, o[0]:o[2]], w_in[:, o[2]:o[3]], w_in[:, o[3]:o[4]],
                                           w_in[:, o[4]:o[5]], w_in[:, o[5]:o[6]], w_in[:, o[6]:o[8]])
    zc = lambda n: jnp.zeros((D, n), BF16)
    w_sm = jnp.concatenate([zc(QK_NOPE), w_kr, zc(QK_PAD - QK_NOPE - QK_ROPE), w_dt, zc(LANES - NHS)], axis=1)
    w_q = jnp.pad(wts["w_uq"].reshape(DQ, NH, QK_NOPE + QK_ROPE),
                  ((0, 0), (0, 0), (0, QK_PAD - QK_NOPE - QK_ROPE))).reshape(DQ, NH * QK_PAD)
    ukv = wts["w_ukv"].reshape(DKV, NH, QK_NOPE + V_DIM)
    w_k = jnp.pad(ukv[:, :, :QK_NOPE], ((0, 0), (0, 0), (0, QK_PAD - QK_NOPE))).reshape(DKV, NH * QK_PAD)
    w_v = ukv[:, :, QK_NOPE:].reshape(DKV, NH * V_DIM)
    dt_bias_p, a_log_p = _pad_cols(small["dt_bias"], LANES), _pad_cols(small["a_log"], LANES)
    dskip_rep = jnp.repeat(small["d_skip"], SSM_HEADDIM, axis=1)
    c_tab, a_tab, b_tab = _rope_tables(positions)

    (u,) = _rowwise("pre_norm", _st_pre, [x], [small["mix_norm_pre"]], [(D, BF16)])
    cqkv = _mm("in_cqkv", u, w_cqkv, "nn")
    z = _mm("in_z", u, w_z, "nn")
    xbc = _mm("in_xbc", u, w_xbc, "nn")
    g = _mm("in_gates", u, w_g, "nn")
    sm = _mm("in_small", u, w_sm, "nn")

    lora_fn = _st_lora_norms(DQ)
    cq_n, ckv_n = _rowwise("lora_norms", lora_fn, [cqkv], [small["q_norm"], small["kv_norm"]], [(DQ, BF16), (DKV, BF16)])
    qraw = _mm("up_q", cq_n, w_q, "nn")
    kraw = _mm("up_k", ckv_n, w_k, "nn")
    v = _mm("up_v", ckv_n, w_v, "nn", out_dtype=BF16)
    q, k = _rowwise("rope", _st_rope(NH), [qraw, kraw, sm, c_tab, a_tab, b_tab], [],
                    [(NH * QK_PAD, BF16), (NH * QK_PAD, BF16)])
    attn, lse = _flash_fwd(q, k, v, NH)

    xbc_c, xbc_a = _conv_fwd(xbc, conv_w, small["conv_b"])
    dt, da = _rowwise("dt", _st_dt, [sm], [dt_bias_p, a_log_p], [(LANES, F32), (LANES, F32)])

    def col_layout(t):
        return _pad_cols(t[:, :NHS].reshape(S, G, R).transpose(1, 0, 2).reshape(G * S, R), LANES).reshape(G, S, LANES)

    def row_layout(t):
        return jnp.pad(t[:, :NHS].reshape(S, G, R).transpose(1, 2, 0), ((0, 0), (0, rp - R), (0, 0)))

    dt_col, da_col, da_row = col_layout(dt), col_layout(da), row_layout(da)
    y, states = _ssd_fwd(xbc_a, dt_col, da_col, da_row, DI, G, R)
    xs = xbc_a[:, :DI]
    gn_fn = _st_gated_norm(G)
    (ssm,) = _rowwise("gated_norm", gn_fn, [y, xs, z], [dskip_rep, small["ssm_norm"]], [(DI, BF16)])

    ao = _mm("attn_o", attn, wts["w_attn_o"], "nn")
    so = _mm("ssm_o", ssm, wts["w_ssm_o"], "nn")
    mix_fn = _st_mix(D)
    (mixed,) = _rowwise("mix", mix_fn, [g, ao, so], [], [(D, BF16)])
    mo = _mm("out_proj", mixed, wts["w_out"], "nn")
    h1, f = _rowwise("res1", _st_res_norm, [x, mo], [small["mix_norm_post"], small["ffn_norm_pre"]], [(D, F32), (D, BF16)])
    gt = _mm("ffn_gate", f, wts["w_gate"], "nn")
    up = _mm("ffn_up", f, wts["w_up"], "nn")
    (act,) = _rowwise("swiglu", _st_swiglu, [gt, up], [], [(gt.shape[1], BF16)])
    dn = _mm("ffn_down", act, wts["w_down"], "nn")
    h2, a3 = _rowwise("res2", _st_res_norm, [h1, dn], [small["ffn_norm_post"], small["ple_norm_pre"]], [(D, F32), (D, BF16)])
    gl = _mm("ple_gate", a3, wts["w_ple_gate"], "nn")
    pe = _mm("ple_proj", p, wts["w_ple"], "nn")

    sg = {}
    bg = {}
    dpe, dgl, dh2, d_w, loss_acc = _rowwise_bwd(
        "loss", _st_loss, [pe, gl, h2, target], [small["ple_norm_post"]], [1.0], [0, 1, 2], [BF16, BF16, F32], fwd_sums=(0,))
    sg["ple_norm_post"] = _fold(d_w)
    loss = loss_acc[0, 0]
    bg["w_ple"] = _mm("d_w_ple", p, dpe, "tn")
    bg["w_ple_gate"] = _mm("d_w_ple_gate", a3, dgl, "tn")
    da3 = _mm("d_a3", dgl, wts["w_ple_gate"], "nt")

    dh1, ddn, d_post, d_pre = _rowwise_bwd(
        "res2_bwd", _st_res_norm, [h1, dn], [small["ffn_norm_post"], small["ple_norm_pre"]], [dh2, da3], [0, 1], [F32, BF16])
    sg["ffn_norm_post"], sg["ple_norm_pre"] = _fold(d_post), _fold(d_pre)
    bg["w_down"] = _mm("d_w_down", act, ddn, "tn")
    dact = _mm("d_act", ddn, wts["w_down"], "nt")
    dgt, dup = _rowwise_bwd("swiglu_bwd", _st_swiglu, [gt, up], [], [dact], [0, 1], [BF16, BF16])
    bg["w_gate"] = _mm("d_w_gate", f, dgt, "tn")
    bg["w_up"] = _mm("d_w_up", f, dup, "tn")
    df = _mm("d_f_gate", dgt, wts["w_gate"], "nt")
    df = _mm("d_f_up", dup, wts["w_up"], "nt", acc_in=df)

    dx_res, dmo, d_post, d_pre = _rowwise_bwd(
        "res1_bwd", _st_res_norm, [x, mo], [small["mix_norm_post"], small["ffn_norm_pre"]], [dh1, df], [0, 1], [F32, BF16])
    sg["mix_norm_post"], sg["ffn_norm_pre"] = _fold(d_post), _fold(d_pre)
    bg["w_out"] = _mm("d_w_out", mixed, dmo, "tn")
    dmixed = _mm("d_mixed", dmo, wts["w_out"], "nt")
    dg, dao, dso = _rowwise_bwd("mix_bwd", mix_fn, [g, ao, so], [], [dmixed], [0, 1, 2], [BF16, BF16, BF16])
    bg["w_attn_o"] = _mm("d_w_attn_o", attn, dao, "tn")
    bg["w_ssm_o"] = _mm("d_w_ssm_o", ssm, dso, "tn")
    dattn = _mm("d_attn", dao, wts["w_attn_o"], "nt")
    dssm = _mm("d_ssm", dso, wts["w_ssm_o"], "nt")

    dy, dxs_a, dz, d_dskip, d_ssmn = _rowwise_bwd(
        "gated_norm_bwd", gn_fn, [y, xs, z], [dskip_rep, small["ssm_norm"]], [dssm], [0, 1, 2], [F32, F32, BF16])
    sg["d_skip"] = _fold(d_dskip).reshape(NHS, SSM_HEADDIM).sum(axis=1).reshape(1, NHS)
    sg["ssm_norm"] = _fold(d_ssmn)
    dxs_b, d_b, d_c, ddt_col, dcum_col, dcum_row = _ssd_bwd(xbc_a, dt_col, da_col, da_row, states, dy, DI, G, R)

    def from_col(t):
        return _pad_cols(t[:, :, :R].transpose(1, 0, 2).reshape(S, NHS), LANES)

    dcum = from_col(dcum_col) + _pad_cols(dcum_row[:, :R, :].transpose(2, 0, 1).reshape(S, NHS), LANES)
    ddtraw, d_bias, d_alog = _rowwise("dt_bwd", _st_dt_bwd, [sm, from_col(ddt_col), dcum], [dt_bias_p, a_log_p],
                                      [(LANES, F32)], accs=(LANES, LANES), tr=L)
    sg["dt_bias"], sg["a_log"] = _fold(d_bias)[:, :NHS], _fold(d_alog)[:, :NHS]
    dconv, d_cb = _rowwise("dconv", _st_dconv(DI), [xbc_c, dxs_a, dxs_b, d_b, d_c], [], [(CONV, F32)], accs=(CONV,))
    sg["conv_b"] = _fold(d_cb)
    dxbc, d_cw = _conv_bwd(xbc, dconv, conv_w)
    d_conv_w = d_cw.reshape(CONV_WIDTH, SUBLANES, CONV).sum(axis=1)

    (delta,) = _rowwise("attn_delta", _st_delta(NH), [dattn, attn], [], [(LANES, F32)])
    delta = delta[:, :NH].T.reshape(NH, S, 1)
    dq = _flash_bwd_dq(q, k, v, dattn, lse, delta, NH)
    dk, dv = _flash_bwd_dkv(q, k, v, dattn, lse, delta, NH)
    dqraw, dkr = _rowwise("rope_bwd", _st_rope_bwd(NH), [dq, dk, c_tab, a_tab, b_tab], [], [(NH * QK_PAD, BF16), (QK_PAD, F32)])
    d_w_q = _mm("d_w_q", cq_n, dqraw, "tn")
    d_w_k = _mm("d_w_k", ckv_n, dk, "tn")
    d_w_v = _mm("d_w_v", ckv_n, dv, "tn")
    dcq_n = _mm("d_cq_n", dqraw, w_q, "nt")
    dckv_n = _mm("d_ckv_n_k", dk, w_k, "nt")
    dckv_n = _mm("d_ckv_n_v", dv, w_v, "nt", acc_in=dckv_n)
    bg["w_uq"] = d_w_q.reshape(DQ, NH, QK_PAD)[:, :, :QK_NOPE + QK_ROPE].reshape(DQ, NH * (QK_NOPE + QK_ROPE))
    bg["w_ukv"] = jnp.concatenate([d_w_k.reshape(DKV, NH, QK_PAD)[:, :, :QK_NOPE], d_w_v.reshape(DKV, NH, V_DIM)],
                                  axis=2).reshape(DKV, NH * (QK_NOPE + V_DIM))
    dcqkv, d_qn, d_kvn = _rowwise_bwd("lora_norms_bwd", lora_fn, [cqkv], [small["q_norm"], small["kv_norm"]],
                                      [dcq_n, dckv_n], [0], [BF16])
    sg["q_norm"], sg["kv_norm"] = _fold(d_qn), _fold(d_kvn)

    dsm = jnp.concatenate([dkr, ddtraw], axis=1)
    d_w_cqkv = _mm("d_w_cqkv", u, dcqkv, "tn")
    d_w_z = _mm("d_w_z", u, dz, "tn")
    d_w_xbc = _mm("d_w_xbc", u, dxbc, "tn")
    d_w_g = _mm("d_w_gates", u, dg, "tn")
    d_w_sm = _mm("d_w_small", u, dsm, "tn")
    bg["w_in"] = jnp.concatenate(
        [d_w_cqkv, d_w_sm[:, QK_NOPE:QK_NOPE + QK_ROPE], d_w_z, d_w_xbc, d_w_sm[:, QK_PAD:QK_PAD + NHS], d_w_g], axis=1)
    du = _mm("d_u_cqkv", dcqkv, w_cqkv, "nt")
    du = _mm("d_u_z", dz, w_z, "nt", acc_in=du)
    du = _mm("d_u_xbc", dxbc, w_xbc, "nt", acc_in=du)
    du = _mm("d_u_gates", dg, w_g, "nt", acc_in=du)
    du = _mm("d_u_small", dsm, w_sm, "nt", acc_in=du)
    grad_x, d_pre = _rowwise_bwd("pre_norm_bwd", _st_pre, [x], [small["mix_norm_pre"]], [du], [0], [F32], adds={0: dx_res})
    sg["mix_norm_pre"] = _fold(d_pre)
    return loss, grad_x, sg, d_conv_w, bg


_HBM = pl.BlockSpec(memory_space=pltpu.HBM)


def _place():
    return lax.axis_index("x"), lax.axis_index("y"), lax.axis_index("c")


def _all_gather(name, block):
    def body(x_ref, out_ref, send_sems, recv_sems, local_sem):
        x, y, c = _place()
        me, sibling = (x, y, c), (x, y, 1 - c)
        chips = [(1 - x, y), (x, 1 - y), (1 - x, 1 - y)]

        def slot(px, py, pc):
            return out_ref.at[4 * px + 2 * py + pc]

        def copy(k, blk, to, src=None):
            return pltpu.make_async_remote_copy(
                src_ref=slot(*blk) if src is None else src, dst_ref=slot(*blk),
                send_sem=send_sems.at[k], recv_sem=recv_sems.at[k], device_id=to, device_id_type=MESH)

        mine = pltpu.make_async_copy(x_ref, slot(*me), local_sem)
        mine.start()
        first = [copy(0, me, sibling, src=x_ref)]
        first += [copy(1 + j, me, (*chip, c), src=x_ref) for j, chip in enumerate(chips)]
        for cp in first:
            cp.start()
        passed = [copy(4 + j, (*chip, c), sibling) for j, chip in enumerate(chips)]
        for j, chip in enumerate(chips):
            copy(1 + j, (*chip, c), me).wait_recv()
            passed[j].start()
        copy(0, sibling, me).wait_recv()
        for j, chip in enumerate(chips):
            copy(4 + j, (*chip, 1 - c), me).wait_recv()
        for cp in first + passed:
            cp.wait_send()
        mine.wait()

    return pl.pallas_call(
        body, name=name, out_shape=jax.ShapeDtypeStruct((N_DEV,) + block.shape, block.dtype),
        in_specs=[_HBM], out_specs=_HBM,
        scratch_shapes=[pltpu.SemaphoreType.DMA((7,)), pltpu.SemaphoreType.DMA((7,)), pltpu.SemaphoreType.DMA(())],
    )(block)


def _swap_with_sibling(name, give):
    def body(g_ref, got_ref, send_sem, recv_sem):
        x, y, c = _place()
        cp = pltpu.make_async_remote_copy(src_ref=g_ref, dst_ref=got_ref, send_sem=send_sem, recv_sem=recv_sem,
                                          device_id=(x, y, 1 - c), device_id_type=MESH)
        cp.start()
        cp.wait()

    return pl.pallas_call(
        body, name=name, out_shape=jax.ShapeDtypeStruct(give.shape, give.dtype), in_specs=[_HBM], out_specs=_HBM,
        scratch_shapes=[pltpu.SemaphoreType.DMA(()), pltpu.SemaphoreType.DMA(())],
    )(give)


def _swap_with_chips(name, send):
    def body(s_ref, r_ref, send_sems, recv_sems):
        x, y, c = _place()
        chips = [(1 - x, y), (x, 1 - y), (1 - x, 1 - y)]
        cps = [pltpu.make_async_remote_copy(src_ref=s_ref.at[2 * px + py], dst_ref=r_ref.at[k],
                                            send_sem=send_sems.at[k], recv_sem=recv_sems.at[k],
                                            device_id=(px, py, c), device_id_type=MESH)
               for k, (px, py) in enumerate(chips)]
        for cp in cps:
            cp.start()
        for cp in cps:
            cp.wait()

    return pl.pallas_call(
        body, name=name, out_shape=jax.ShapeDtypeStruct((3,) + send.shape[1:], send.dtype), in_specs=[_HBM], out_specs=_HBM,
        scratch_shapes=[pltpu.SemaphoreType.DMA((3,)), pltpu.SemaphoreType.DMA((3,))],
    )(send)


def _pack(arrays, dtype, row_mult):
    flat = jnp.concatenate([a.astype(dtype).reshape(-1) for a in arrays])
    unit = PACK_W * row_mult
    n = -(-flat.shape[0] // unit) * unit
    return jnp.pad(flat, (0, n - flat.shape[0])).reshape(n // PACK_W, PACK_W)


def _unpack(packed, shapes, lead=()):
    flat = packed.reshape(lead + (-1,))
    out, off = [], 0
    for s in shapes:
        n = 1
        for d in s:
            n *= d
        out.append(flat[..., off:off + n].reshape(lead + tuple(s)))
        off += n
    return out


def _adamw(w, g, m, v):
    m = ADAM_B1 * m + (1.0 - ADAM_B1) * g
    v = ADAM_B2 * v + (1.0 - ADAM_B2) * (g * g)
    m_hat = m / (1.0 - ADAM_B1 ** ADAM_STEP)
    v_hat = v / (1.0 - ADAM_B2 ** ADAM_STEP)
    delta = -ADAM_LR * (m_hat / (jnp.sqrt(v_hat) + ADAM_EPS) + ADAM_WD * w)
    return delta, m, v


BIG = ("w_in", "w_uq", "w_ukv", "w_attn_o", "w_ssm_o", "w_out", "w_gate", "w_up", "w_down", "w_ple_gate", "w_ple")
COL_SHARDED = ("w_in", "w_uq", "w_ukv", "w_gate", "w_up", "w_ple")
SMALL = ("mix_norm_pre", "mix_norm_post", "q_norm", "kv_norm", "conv_b", "dt_bias", "a_log", "d_skip", "ssm_norm",
         "ffn_norm_pre", "ffn_norm_post", "ple_norm_pre", "ple_norm_post")
WEIGHTS = ("mix_norm_pre", "mix_norm_post", "w_in", "q_norm", "w_uq", "kv_norm", "w_ukv", "conv_w", "conv_b", "dt_bias",
           "a_log", "d_skip", "ssm_norm", "w_attn_o", "w_ssm_o", "w_out", "ffn_norm_pre", "ffn_norm_post", "w_gate",
           "w_up", "w_down", "ple_norm_pre", "ple_norm_post", "w_ple_gate", "w_ple")


def _step(x, p, positions, target, w, m, v):
    xi, yi, ci = _place()
    me = 4 * xi + 2 * yi + ci
    chip = 2 * xi + yi

    shard_shapes = [w[n].shape for n in BIG]
    gathered = _all_gather("gather_weights", _pack([w[n] for n in BIG], BF16, 2 * SUBLANES))
    wts = {}
    for n, blocks in zip(BIG, _unpack(gathered, shard_shapes, lead=(N_DEV,))):
        rows, cols = blocks.shape[1:]
        if n in COL_SHARDED:
            wts[n] = blocks.transpose(1, 0, 2).reshape(rows, N_DEV * cols)
        else:
            wts[n] = blocks.reshape(N_DEV * rows, cols)
    cw_shape = w["conv_w"].shape
    cw_all = _all_gather("gather_conv_w", _pack([w["conv_w"]], F32, SUBLANES))
    (cw_blocks,) = _unpack(cw_all, [cw_shape], lead=(N_DEV,))
    conv_w = cw_blocks.transpose(1, 0, 2).reshape(cw_shape[0], N_DEV * cw_shape[1])

    small = {n: w[n] for n in SMALL}
    loss, grad_x, sg, d_conv_w, bg = _local_step(x, p, positions, target, small, conv_w, wts)

    small_shapes = [sg[n].shape for n in SMALL] + [d_conv_w.shape]
    sg_all = _all_gather("gather_small_grads", _pack([sg[n] for n in SMALL] + [d_conv_w], F32, SUBLANES))
    (sg_sum,) = _rowwise("sum_small_grads", lambda *a: functools.reduce(lambda s, t: s + t, a),
                         [sg_all[k] for k in range(N_DEV)], [], [(PACK_W, F32)])
    sg_list = _unpack(sg_sum, small_shapes)
    grads = dict(zip(SMALL, sg_list[:-1]))
    grads["conv_w"] = lax.dynamic_slice_in_dim(sg_list[-1], me * cw_shape[1], cw_shape[1], axis=1)

    blocks = []
    for n in BIG:
        g = bg[n]
        rows, cols = w[n].shape
        if n in COL_SHARDED:
            blocks.append(g.reshape(rows, N_DEV, cols).transpose(1, 0, 2).reshape(N_DEV, rows * cols))
        else:
            blocks.append(g.reshape(N_DEV, rows * cols))
    flat = jnp.concatenate([b.astype(BF16) for b in blocks], axis=1)
    unit = PACK_W * 2 * SUBLANES
    n_loc = -(-flat.shape[1] // unit) * unit
    pr = n_loc // PACK_W
    by_chip = jnp.pad(flat, ((0, 0), (0, n_loc - flat.shape[1]))).reshape(4, 2, pr, PACK_W)
    keep = lax.dynamic_index_in_dim(by_chip, ci, axis=1, keepdims=False)
    give = lax.dynamic_index_in_dim(by_chip, 1 - ci, axis=1, keepdims=False)
    got = _swap_with_sibling("rs_within_chip", give)
    chip_f32, chip_bf16 = _rowwise("rs_add_sibling", lambda a, b: (a + b, a + b),
                                   [keep.reshape(4 * pr, PACK_W), got.reshape(4 * pr, PACK_W)], [],
                                   [(PACK_W, F32), (PACK_W, BF16)])
    recv = _swap_with_chips("rs_between_chips", chip_bf16.reshape(4, pr, PACK_W))
    own = lax.dynamic_index_in_dim(chip_f32.reshape(4, pr, PACK_W), chip, axis=0, keepdims=False)
    (g_sum,) = _rowwise("rs_add_chips", lambda o, a, b, c: ((o + a) + b) + c, [own, recv[0], recv[1], recv[2]], [],
                        [(PACK_W, F32)])
    grads.update(zip(BIG, _unpack(g_sum, shard_shapes)))

    delta, new_m, new_v = {}, {}, {}
    for n in BIG:
        cols = w[n].shape[1]
        delta[n], new_m[n], new_v[n] = _rowwise("adamw_" + n, _adamw, [w[n], grads[n], m[n], v[n]], [],
                                                [(cols, F32)] * 3)
    rest = SMALL + ("conv_w",)
    rest_shapes = [w[n].shape for n in rest]
    packed = [_pack([d[n] for n in rest], F32, SUBLANES) for d in (w, grads, m, v)]
    outs = _rowwise("adamw_small", _adamw, packed, [], [(PACK_W, F32)] * 3)
    for d, o in zip((delta, new_m, new_v), outs):
        d.update(zip(rest, _unpack(o, rest_shapes)))
    return loss, grad_x, grads, delta, new_m, new_v


def kernel(x, p, positions, mix_norm_pre, mix_norm_post, w_in, q_norm, w_uq, kv_norm, w_ukv, conv_w, conv_b, dt_bias, a_log, d_skip, ssm_norm, w_attn_o, w_ssm_o, w_out, ffn_norm_pre, ffn_norm_post, w_gate, w_up, w_down, ple_norm_pre, ple_norm_post, w_ple_gate, w_ple, loss_target, m_mix_norm_pre, m_mix_norm_post, m_w_in, m_q_norm, m_w_uq, m_kv_norm, m_w_ukv, m_conv_w, m_conv_b, m_dt_bias, m_a_log, m_d_skip, m_ssm_norm, m_w_attn_o, m_w_ssm_o, m_w_out, m_ffn_norm_pre, m_ffn_norm_post, m_w_gate, m_w_up, m_w_down, m_ple_norm_pre, m_ple_norm_post, m_w_ple_gate, m_w_ple, v_mix_norm_pre, v_mix_norm_post, v_w_in, v_q_norm, v_w_uq, v_kv_norm, v_w_ukv, v_conv_w, v_conv_b, v_dt_bias, v_a_log, v_d_skip, v_ssm_norm, v_w_attn_o, v_w_ssm_o, v_w_out, v_ffn_norm_pre, v_ffn_norm_post, v_w_gate, v_w_up, v_w_down, v_ple_norm_pre, v_ple_norm_post, v_w_ple_gate, v_w_ple):
    w_args = (mix_norm_pre, mix_norm_post, w_in, q_norm, w_uq, kv_norm, w_ukv, conv_w, conv_b, dt_bias, a_log, d_skip, ssm_norm, w_attn_o, w_ssm_o, w_out, ffn_norm_pre, ffn_norm_post, w_gate, w_up, w_down, ple_norm_pre, ple_norm_post, w_ple_gate, w_ple)
    m_args = (m_mix_norm_pre, m_mix_norm_post, m_w_in, m_q_norm, m_w_uq, m_kv_norm, m_w_ukv, m_conv_w, m_conv_b, m_dt_bias, m_a_log, m_d_skip, m_ssm_norm, m_w_attn_o, m_w_ssm_o, m_w_out, m_ffn_norm_pre, m_ffn_norm_post, m_w_gate, m_w_up, m_w_down, m_ple_norm_pre, m_ple_norm_post, m_w_ple_gate, m_w_ple)
    v_args = (v_mix_norm_pre, v_mix_norm_post, v_w_in, v_q_norm, v_w_uq, v_kv_norm, v_w_ukv, v_conv_w, v_conv_b, v_dt_bias, v_a_log, v_d_skip, v_ssm_norm, v_w_attn_o, v_w_ssm_o, v_w_out, v_ffn_norm_pre, v_ffn_norm_post, v_w_gate, v_w_up, v_w_down, v_ple_norm_pre, v_ple_norm_post, v_w_ple_gate, v_w_ple)

    def drop_layer(a):
        return a if a.ndim == 2 else a[0]

    w = {n: drop_layer(a) for n, a in zip(WEIGHTS, w_args)}
    m = {n: drop_layer(a) for n, a in zip(WEIGHTS, m_args)}
    v = {n: drop_layer(a) for n, a in zip(WEIGHTS, v_args)}
    loss, grad_x, grads, delta, new_m, new_v = _step(x[0], p[0, 0], positions[0], loss_target[0], w, m, v)
    loss = lax.psum(loss, ("x", "y", "c"))
    like = lambda d: [d[n].reshape(a.shape) for n, a in zip(WEIGHTS, w_args)]
    return (loss, grad_x[None], *like(grads), *like(delta), *like(new_m), *like(new_v))
```

```python
import functools

import jax
import jax.numpy as jnp
from jax import lax
from jax.experimental import pallas as pl
from jax.experimental.pallas import tpu as pltpu

F32 = jnp.float32
BF16 = jnp.bfloat16

EPS = 1e-6
QK_NOPE = 128
QK_ROPE = 64
V_DIM = 128
QK_PAD = 256
ROPE_THETA = 10000.0
SSM_HEADDIM = 64
D_STATE = 128
CONV_WIDTH = 4
CHUNK = 256
ADAM_LR = 0.001
ADAM_B1 = 0.9
ADAM_B2 = 0.999
ADAM_EPS = 1e-08
ADAM_WD = 0.01
ADAM_STEP = 10

N_DEV = 8
LANES = 128
SUBLANES = 8
PACK_W = 1024
VMEM_LIMIT = 56 * 1024 * 1024
ROW_TILE_BYTES = 6 * 1024 * 1024
FLASH_T = 512
MM_TILE_BYTES = 20 * 1024 * 1024
MESH = pl.DeviceIdType.MESH


def _pick(dim, prefs):
    if dim <= prefs[0]:
        return dim
    for p in prefs:
        if dim % p == 0:
            return p
    return dim


def _tile(dim, cap):
    if dim <= cap:
        return dim
    best = None
    for t in range(LANES, cap + 1, LANES):
        if dim % t == 0:
            best = t
    return best if best is not None else dim


def _params(sem):
    return pltpu.CompilerParams(dimension_semantics=sem, vmem_limit_bytes=VMEM_LIMIT)


def _dot(a, b):
    return lax.dot_general(a, b, (((1,), (0,)), ((), ())), preferred_element_type=F32)


def _dot_nt(a, b):
    return lax.dot_general(a, b, (((1,), (1,)), ((), ())), preferred_element_type=F32)


def _dot_tn(a, b):
    return lax.dot_general(a, b, (((0,), (0,)), ((), ())), preferred_element_type=F32)


def _mm(name, a, b, mode, out_dtype=F32, acc_in=None):
    if mode == "nn":
        (M, K), (K2, N) = a.shape, b.shape
    elif mode == "nt":
        (M, K), (N, K2) = a.shape, b.shape
    else:
        (K, M), (K2, N) = a.shape, b.shape
    assert K == K2, (name, a.shape, b.shape, mode)
    tm = _tile(M, 1024)
    tn = _tile(N, 1024 if acc_in is not None else 1536)
    tk = _tile(K, 2048)
    while tk > 512 and 2 * (tm * tk * a.dtype.itemsize + tk * tn * b.dtype.itemsize) > MM_TILE_BYTES:
        tk = _tile(K, tk - LANES)
    nk = K // tk
    dot = {"nn": _dot, "nt": _dot_nt, "tn": _dot_tn}[mode]
    has_acc = acc_in is not None

    def body(*refs):
        if has_acc:
            a_ref, b_ref, c_ref, o_ref, acc = refs
        else:
            a_ref, b_ref, o_ref, acc = refs
        k = pl.program_id(2)

        @pl.when(k == 0)
        def _():
            acc[...] = jnp.zeros_like(acc)

        acc[...] += dot(a_ref[...].astype(BF16), b_ref[...].astype(BF16))

        @pl.when(k == nk - 1)
        def _():
            r = acc[...]
            if has_acc:
                r = r + c_ref[...]
            o_ref[...] = r.astype(o_ref.dtype)

    if mode == "tn":
        a_spec = pl.BlockSpec((tk, tm), lambda i, j, k: (k, i))
    else:
        a_spec = pl.BlockSpec((tm, tk), lambda i, j, k: (i, k))
    if mode == "nt":
        b_spec = pl.BlockSpec((tn, tk), lambda i, j, k: (j, k))
    else:
        b_spec = pl.BlockSpec((tk, tn), lambda i, j, k: (k, j))
    o_spec = pl.BlockSpec((tm, tn), lambda i, j, k: (i, j))
    in_specs = [a_spec, b_spec] + ([o_spec] if has_acc else [])
    args = (a, b) + ((acc_in,) if has_acc else ())
    return pl.pallas_call(
        body, name=name, grid=(M // tm, N // tn, nk), in_specs=in_specs, out_specs=o_spec,
        out_shape=jax.ShapeDtypeStruct((M, N), out_dtype), scratch_shapes=[pltpu.VMEM((tm, tn), F32)],
        input_output_aliases=({2: 0} if has_acc and out_dtype == F32 else {}),
        compiler_params=_params(("parallel", "parallel", "arbitrary")),
    )(*args)


def _row_tile(n_rows, bytes_per_row):
    tr = 512
    while tr > SUBLANES and tr * bytes_per_row > ROW_TILE_BYTES:
        tr //= 2
    while n_rows % tr:
        tr //= 2
    return tr


def _acc_add(a_ref, v):
    if v.shape[0] == 1:
        a_ref[0:1, :] += v
    else:
        a_ref[...] += v.reshape(v.shape[0] // SUBLANES, SUBLANES, v.shape[1]).sum(axis=0)


def _rowwise(name, fn, rows, bcs, outs, accs=(), tr=None):
    n_rows = rows[0].shape[0]
    if tr is None:
        per_row = sum(r.shape[1] * r.dtype.itemsize for r in rows) + sum(w * jnp.dtype(d).itemsize for w, d in outs)
        tr = _row_tile(n_rows, per_row)
    n_r, n_b, n_o, n_a = len(rows), len(bcs), len(outs), len(accs)

    def body(*refs):
        ins = [r[...].astype(F32) for r in refs[: n_r + n_b]]
        res = fn(*ins)
        res = res if isinstance(res, (tuple, list)) else (res,)
        o_refs = refs[n_r + n_b: n_r + n_b + n_o]
        a_refs = refs[n_r + n_b + n_o:]
        for o, v in zip(o_refs, res[:n_o]):
            o[...] = v.astype(o.dtype)
        if n_a:
            @pl.when(pl.program_id(0) == 0)
            def _():
                for a in a_refs:
                    a[...] = jnp.zeros_like(a)

            for a, v in zip(a_refs, res[n_o:]):
                _acc_add(a, v)

    in_specs = [pl.BlockSpec((tr, r.shape[1]), lambda i: (i, 0)) for r in rows]
    in_specs += [pl.BlockSpec((1, b.shape[1]), lambda i: (0, 0)) for b in bcs]
    out_specs = [pl.BlockSpec((tr, w), lambda i: (i, 0)) for w, _ in outs]
    out_specs += [pl.BlockSpec((SUBLANES, w), lambda i: (0, 0)) for w in accs]
    out_shape = [jax.ShapeDtypeStruct((n_rows, w), d) for w, d in outs]
    out_shape += [jax.ShapeDtypeStruct((SUBLANES, w), F32) for w in accs]
    res = pl.pallas_call(
        body, name=name, grid=(n_rows // tr,), in_specs=in_specs, out_specs=out_specs, out_shape=out_shape,
        compiler_params=_params(("arbitrary",) if n_a else ("parallel",)),
    )(*rows, *bcs)
    return tuple(res)


def _rowwise_bwd(name, fn, rows, bcs, cts, need_rows, row_dtypes, need_bcs=None, adds=None, fwd_sums=(), tr=None):
    n_rows = rows[0].shape[0]
    adds = adds or {}
    need_bcs = list(range(len(bcs))) if need_bcs is None else list(need_bcs)
    ct_arrays = [c for c in cts if not isinstance(c, float)]
    add_keys = sorted(adds)
    add_arrays = [adds[k] for k in add_keys]
    if tr is None:
        per_row = sum(r.shape[1] * r.dtype.itemsize for r in list(rows) + ct_arrays + add_arrays)
        per_row += sum(rows[i].shape[1] * jnp.dtype(d).itemsize for i, d in zip(need_rows, row_dtypes))
        tr = _row_tile(n_rows, 2 * per_row)
    n_r, n_b, n_c, n_ad = len(rows), len(bcs), len(ct_arrays), len(add_arrays)
    n_go, n_gb, n_fs = len(need_rows), len(need_bcs), len(fwd_sums)

    def body(*refs):
        pos = 0
        r_t = [r[...].astype(F32) for r in refs[pos: pos + n_r]]
        pos += n_r
        b_t = [r[...].astype(F32) for r in refs[pos: pos + n_b]]
        pos += n_b
        c_t = [r[...].astype(F32) for r in refs[pos: pos + n_c]]
        pos += n_c
        ad_t = [r[...].astype(F32) for r in refs[pos: pos + n_ad]]
        pos += n_ad
        go_refs = refs[pos: pos + n_go]
        pos += n_go
        acc_refs = refs[pos:]

        def wrapped(*a):
            r = fn(*a)
            return tuple(r) if isinstance(r, (tuple, list)) else (r,)

        outs, vjp = jax.vjp(wrapped, *r_t, *b_t)
        it = iter(c_t)
        full = tuple(jnp.full(o.shape, c, F32) if isinstance(c, float) else next(it) for o, c in zip(outs, cts))
        grads = vjp(full)
        for o_ref, i in zip(go_refs, need_rows):
            g = grads[i]
            if i in adds:
                g = g + ad_t[add_keys.index(i)]
            o_ref[...] = g.astype(o_ref.dtype)

        @pl.when(pl.program_id(0) == 0)
        def _():
            for a in acc_refs:
                a[...] = jnp.zeros_like(a)

        for a, j in zip(acc_refs[:n_gb], need_bcs):
            _acc_add(a, grads[n_r + j])
        for a, j in zip(acc_refs[n_gb:], fwd_sums):
            a[0:1, :] += jnp.full((1, LANES), jnp.sum(outs[j]), F32)

    def row_spec(w):
        return pl.BlockSpec((tr, w), lambda i: (i, 0))

    in_specs = [row_spec(r.shape[1]) for r in rows]
    in_specs += [pl.BlockSpec((1, b.shape[1]), lambda i: (0, 0)) for b in bcs]
    in_specs += [row_spec(c.shape[1]) for c in ct_arrays] + [row_spec(a.shape[1]) for a in add_arrays]
    out_specs = [row_spec(rows[i].shape[1]) for i in need_rows]
    out_specs += [pl.BlockSpec((SUBLANES, bcs[j].shape[1]), lambda i: (0, 0)) for j in need_bcs]
    out_specs += [pl.BlockSpec((SUBLANES, LANES), lambda i: (0, 0)) for _ in fwd_sums]
    out_shape = [jax.ShapeDtypeStruct((n_rows, rows[i].shape[1]), d) for i, d in zip(need_rows, row_dtypes)]
    out_shape += [jax.ShapeDtypeStruct((SUBLANES, bcs[j].shape[1]), F32) for j in need_bcs]
    out_shape += [jax.ShapeDtypeStruct((SUBLANES, LANES), F32) for _ in fwd_sums]
    res = pl.pallas_call(
        body, name=name, grid=(n_rows // tr,), in_specs=in_specs, out_specs=out_specs, out_shape=out_shape,
        compiler_params=_params(("arbitrary",)),
    )(*rows, *bcs, *ct_arrays, *add_arrays)
    return tuple(res)


def _fold(acc):
    return jnp.sum(acc, axis=0, keepdims=True)


def _rms(x, w):
    return x * lax.rsqrt(jnp.mean(x * x, axis=-1, keepdims=True) + EPS) * w


def _sigmoid(x):
    return jax.nn.sigmoid(x)


def _silu(x):
    return x * _sigmoid(x)


def _log1p(u):
    series = u * (1.0 - u * (0.5 - u * (1.0 / 3.0 - u * 0.25)))
    return jnp.where(u < 0.01, series, jnp.log(1.0 + u))


def _softplus(x):
    return jnp.maximum(x, 0.0) + _log1p(jnp.exp(-jnp.abs(x)))


def _st_pre(x, w):
    return _rms(x, w)


def _st_lora_norms(dq):
    def fn(cqkv, qn, kvn):
        return _rms(cqkv[:, :dq], qn), _rms(cqkv[:, dq:], kvn)
    return fn


def _st_gated_norm(n_groups):
    def fn(y, xs, z, dskip, wn):
        yz = (y + dskip * xs) * _silu(z)
        gw = yz.shape[1] // n_groups
        parts = [_rms(yz[:, g * gw:(g + 1) * gw], wn[:, g * gw:(g + 1) * gw]) for g in range(n_groups)]
        return jnp.concatenate(parts, axis=1)
    return fn


def _st_mix(d):
    def fn(g, ao, so):
        return _sigmoid(g[:, :d]) * ao + _sigmoid(g[:, d:]) * so
    return fn


def _st_res_norm(h, y, w_post, w_pre):
    h2 = h + _rms(y, w_post)
    return h2, _rms(h2, w_pre)


def _st_swiglu(gt, up):
    return _silu(gt) * up


def _st_loss(pe, gl, h2, tgt, w_post):
    e = pe * _sigmoid(gl)
    diff = h2 + _rms(e, w_post) - tgt
    return 0.5 * jnp.mean(diff * diff, axis=-1, keepdims=True)


def _rope_tables(positions):
    half = QK_ROPE // 2
    inv_freq = ROPE_THETA ** (-jnp.arange(0, QK_ROPE, 2, dtype=F32) / QK_ROPE)
    ang = positions.astype(F32).reshape(-1, 1) * inv_freq
    cos, sin = jnp.cos(ang), jnp.sin(ang)
    n = ang.shape[0]
    z = lambda w: jnp.zeros((n, w), F32)
    c_tab = jnp.concatenate([jnp.ones((n, QK_NOPE), F32), cos, cos, z(QK_PAD - QK_NOPE - QK_ROPE)], axis=1)
    a_tab = jnp.concatenate([z(QK_NOPE), -sin, z(half), z(QK_PAD - QK_NOPE - QK_ROPE)], axis=1)
    b_tab = jnp.concatenate([z(QK_NOPE), z(half), sin, z(QK_PAD - QK_NOPE - QK_ROPE)], axis=1)
    return c_tab, a_tab, b_tab


def _rot(x, c, a, b):
    half = QK_ROPE // 2
    return x * c + pltpu.roll(x, QK_PAD - half, axis=1) * a + pltpu.roll(x, half, axis=1) * b


def _rot_t(g, c, a, b):
    half = QK_ROPE // 2
    return g * c + pltpu.roll(g * a, half, axis=1) + pltpu.roll(g * b, QK_PAD - half, axis=1)


def _st_rope(n_heads):
    def fn(qraw, kraw, sm, c, a, b):
        kpe = _rot(sm[:, :QK_PAD], c, a, b)
        scale = float(QK_NOPE + QK_ROPE) ** -0.5
        q = [_rot(qraw[:, h * QK_PAD:(h + 1) * QK_PAD], c, a, b) * scale for h in range(n_heads)]
        k = [kraw[:, h * QK_PAD:(h + 1) * QK_PAD] + kpe for h in range(n_heads)]
        return jnp.concatenate(q, axis=1), jnp.concatenate(k, axis=1)
    return fn


def _st_rope_bwd(n_heads):
    def fn(dq, dk, c, a, b):
        dqraw = [_rot_t(dq[:, h * QK_PAD:(h + 1) * QK_PAD], c, a, b) for h in range(n_heads)]
        dks = dk[:, :QK_PAD]
        for h in range(1, n_heads):
            dks = dks + dk[:, h * QK_PAD:(h + 1) * QK_PAD]
        return jnp.concatenate(dqraw, axis=1), _rot_t(dks, c, a, b), dk
    return fn


def _split3(x):
    h1 = x.astype(BF16)
    r1 = x - h1.astype(F32)
    h2 = r1.astype(BF16)
    h3 = (r1 - h2.astype(F32)).astype(BF16)
    return h1, h2, h3


def _tri_dot(tri, x):
    h1, h2, h3 = _split3(x)
    return (_dot(tri, h3) + _dot(tri, h2)) + _dot(tri, h1)


def _dot_tri(x, tri):
    h1, h2, h3 = _split3(x)
    return (_dot(h3, tri) + _dot(h2, tri)) + _dot(h1, tri)


def _st_dt(sm, bias, alog):
    x = sm[:, QK_PAD:] + bias
    dt = _softplus(x)
    return dt, dt * (-jnp.exp(alog))


def _st_dt_bwd(sm, ddt, dcum, bias, alog):
    n = sm.shape[0]
    i = lax.broadcasted_iota(jnp.int32, (n, n), 0)
    j = lax.broadcasted_iota(jnp.int32, (n, n), 1)
    upper = (j >= i).astype(BF16)
    dda = _tri_dot(upper, dcum)
    x = sm[:, QK_PAD:] + bias
    dt = _softplus(x)
    a = -jnp.exp(alog)
    draw = (ddt + dda * a) * _sigmoid(x)
    return draw, draw, dda * dt * a


def _conv_fwd(xbc, w, b):
    S, C = xbc.shape
    tr = _pick(S, (512, 256))
    tc = _pick(C, (1024, 512, 256, 128))
    hb = tr // SUBLANES

    def body(x_ref, halo_ref, w_ref, b_ref, c_ref, a_ref, ext):
        i = pl.program_id(1)
        halo = jnp.where(i == 0, 0.0, halo_ref[...])
        ext[0:SUBLANES, :] = halo
        ext[SUBLANES:, :] = x_ref[...]
        wv = w_ref[...]
        acc = b_ref[...] + wv[CONV_WIDTH - 1:CONV_WIDTH, :] * x_ref[...]
        for k in range(CONV_WIDTH - 1):
            off = SUBLANES - (CONV_WIDTH - 1) + k
            acc = acc + wv[k:k + 1, :] * ext[pl.ds(off, tr), :]
        c_ref[...] = acc
        a_ref[...] = _silu(acc)

    return pl.pallas_call(
        body, name="conv_fwd", grid=(C // tc, S // tr),
        in_specs=[pl.BlockSpec((tr, tc), lambda j, i: (i, j)),
                  pl.BlockSpec((SUBLANES, tc), lambda j, i: (jnp.maximum(i * hb - 1, 0), j)),
                  pl.BlockSpec((CONV_WIDTH, tc), lambda j, i: (0, j)),
                  pl.BlockSpec((1, tc), lambda j, i: (0, j))],
        out_specs=[pl.BlockSpec((tr, tc), lambda j, i: (i, j))] * 2,
        out_shape=[jax.ShapeDtypeStruct((S, C), F32)] * 2,
        scratch_shapes=[pltpu.VMEM((tr + SUBLANES, tc), F32)],
        compiler_params=_params(("parallel", "arbitrary")),
    )(xbc, xbc, w, b)


def _conv_bwd(xbc, dconv, w):
    S, C = xbc.shape
    tr = _pick(S, (512, 256))
    tc = _pick(C, (1024, 512, 256, 128))
    hb = tr // SUBLANES
    n_i = S // tr

    def body(x_ref, halo_ref, d_ref, dnext_ref, w_ref, dx_ref, dw_ref, ext, dext):
        i = pl.program_id(1)
        ext[0:SUBLANES, :] = jnp.where(i == 0, 0.0, halo_ref[...])
        ext[SUBLANES:, :] = x_ref[...]
        dext[0:tr, :] = d_ref[...]
        dext[tr:, :] = jnp.where(i == n_i - 1, 0.0, dnext_ref[...])
        wv = w_ref[...]
        d = d_ref[...]

        @pl.when(i == 0)
        def _():
            dw_ref[...] = jnp.zeros_like(dw_ref)

        dx = wv[CONV_WIDTH - 1:CONV_WIDTH, :] * d
        for k in range(CONV_WIDTH):
            if k < CONV_WIDTH - 1:
                dx = dx + wv[k:k + 1, :] * dext[pl.ds(CONV_WIDTH - 1 - k, tr), :]
                xs = ext[pl.ds(SUBLANES - (CONV_WIDTH - 1) + k, tr), :]
            else:
                xs = x_ref[...]
            prod = d * xs
            dw_ref[k * SUBLANES:(k + 1) * SUBLANES, :] += prod.reshape(tr // SUBLANES, SUBLANES, tc).sum(axis=0)
        dx_ref[...] = dx.astype(dx_ref.dtype)

    return pl.pallas_call(
        body, name="conv_bwd", grid=(C // tc, n_i),
        in_specs=[pl.BlockSpec((tr, tc), lambda j, i: (i, j)),
                  pl.BlockSpec((SUBLANES, tc), lambda j, i: (jnp.maximum(i * hb - 1, 0), j)),
                  pl.BlockSpec((tr, tc), lambda j, i: (i, j)),
                  pl.BlockSpec((SUBLANES, tc), lambda j, i: (jnp.minimum((i + 1) * hb, S // SUBLANES - 1), j)),
                  pl.BlockSpec((CONV_WIDTH, tc), lambda j, i: (0, j))],
        out_specs=[pl.BlockSpec((tr, tc), lambda j, i: (i, j)),
                   pl.BlockSpec((CONV_WIDTH * SUBLANES, tc), lambda j, i: (0, j))],
        out_shape=[jax.ShapeDtypeStruct((S, C), BF16), jax.ShapeDtypeStruct((CONV_WIDTH * SUBLANES, C), F32)],
        scratch_shapes=[pltpu.VMEM((tr + SUBLANES, tc), F32), pltpu.VMEM((tr + SUBLANES, tc), F32)],
        compiler_params=_params(("parallel", "arbitrary")),
    )(xbc, xbc, dconv, dconv, w)


def _st_dconv(d_inner):
    def fn(xc, dxa, dxb, db_, dc_):
        s = _sigmoid(xc)
        g = jnp.concatenate([dxa + dxb, db_, dc_], axis=1) * (s * (1.0 + xc * (1.0 - s)))
        return g, g
    return fn


def _chunk_setup(b_ref, c_ref, dac_ref, dar_ref, L):
    ii = lax.broadcasted_iota(jnp.int32, (L, L), 0)
    jj = lax.broadcasted_iota(jnp.int32, (L, L), 1)
    tri = ii >= jj
    cum_c = _tri_dot(tri.astype(BF16), dac_ref[0])
    cum_r = _dot_tri(dar_ref[0], (ii <= jj).astype(BF16))
    bm = b_ref[...].astype(BF16)
    cm = c_ref[...].astype(BF16)
    return tri, cum_c, cum_r, bm, cm, _dot_nt(cm, bm)


def _ssd_specs(d_inner, n_groups, gw, L, rp, chunk_of):
    bb0 = d_inner // D_STATE
    cb0 = bb0 + n_groups
    return [pl.BlockSpec((L, gw), lambda g, c: (chunk_of(c), g)),
            pl.BlockSpec((L, D_STATE), lambda g, c: (chunk_of(c), bb0 + g)),
            pl.BlockSpec((L, D_STATE), lambda g, c: (chunk_of(c), cb0 + g)),
            pl.BlockSpec((1, L, LANES), lambda g, c: (g, chunk_of(c), 0)),
            pl.BlockSpec((1, L, LANES), lambda g, c: (g, chunk_of(c), 0)),
            pl.BlockSpec((1, rp, L), lambda g, c: (g, 0, chunk_of(c)))]


def _ssd_fwd(xbc_a, dt_col, da_col, da_row, d_inner, n_groups, R):
    S = xbc_a.shape[0]
    L = min(CHUNK, S)
    NC = S // L
    P, N = SSM_HEADDIM, D_STATE
    gw = R * P
    rp = da_row.shape[1]

    def body(x_ref, b_ref, c_ref, dt_ref, dac_ref, dar_ref, y_ref, st_ref, state):
        @pl.when(pl.program_id(1) == 0)
        def _():
            state[...] = jnp.zeros_like(state)

        st_ref[0, 0] = state[...]
        tri, cum_c, cum_r, bm, cm, gm = _chunk_setup(b_ref, c_ref, dac_ref, dar_ref, L)
        dt = dt_ref[0]
        for r in range(R):
            cc = cum_c[:, r:r + 1]
            cr = cum_r[r:r + 1, :]
            lam = jnp.exp(jnp.where(tri, cc - cr, -jnp.inf))
            m = (gm * lam).astype(BF16)
            x = x_ref[:, r * P:(r + 1) * P] * dt[:, r:r + 1]
            s_r = state[r * P:(r + 1) * P, :]
            y_off = _dot_nt(cm, s_r.astype(BF16)) * jnp.exp(cc)
            y_ref[:, r * P:(r + 1) * P] = _dot(m, x.astype(BF16)) + y_off
            last = cr[:, L - 1:L]
            xw = (x * jnp.exp(last - cc)).astype(BF16)
            state[r * P:(r + 1) * P, :] = s_r * jnp.exp(last) + _dot_tn(xw, bm)

    return pl.pallas_call(
        body, name="ssd_fwd", grid=(n_groups, NC),
        in_specs=_ssd_specs(d_inner, n_groups, gw, L, rp, lambda c: c),
        out_specs=[pl.BlockSpec((L, gw), lambda g, c: (c, g)),
                   pl.BlockSpec((1, 1, gw, N), lambda g, c: (g, c, 0, 0))],
        out_shape=[jax.ShapeDtypeStruct((S, d_inner), F32), jax.ShapeDtypeStruct((n_groups, NC, gw, N), F32)],
        scratch_shapes=[pltpu.VMEM((gw, N), F32)],
        compiler_params=_params(("parallel", "arbitrary")),
    )(xbc_a, xbc_a, xbc_a, dt_col, da_col, da_row)


def _ssd_bwd(xbc_a, dt_col, da_col, da_row, states, dy, d_inner, n_groups, R):
    S = xbc_a.shape[0]
    L = min(CHUNK, S)
    NC = S // L
    P, N = SSM_HEADDIM, D_STATE
    gw = R * P
    rp = da_row.shape[1]
    rev = lambda c: NC - 1 - c

    def body(x_ref, b_ref, c_ref, dt_ref, dac_ref, dar_ref, st_ref, dy_ref,
             dx_ref, db_ref, dc_ref, ddt_ref, dcc_ref, dcr_ref, dstate):
        @pl.when(pl.program_id(1) == 0)
        def _():
            dstate[...] = jnp.zeros_like(dstate)

        tri, cum_c, cum_r, bm, cm, gm = _chunk_setup(b_ref, c_ref, dac_ref, dar_ref, L)
        dt = dt_ref[0]
        lane = lax.broadcasted_iota(jnp.int32, (L, LANES), 1)
        sub = lax.broadcasted_iota(jnp.int32, (rp, L), 0)
        is_last = lax.broadcasted_iota(jnp.int32, (L, 1), 0) == L - 1
        d_g = jnp.zeros((L, L), F32)
        dc_acc = jnp.zeros((L, N), F32)
        db_acc = jnp.zeros((L, N), F32)
        ddt_out = jnp.zeros((L, LANES), F32)
        dcc_out = jnp.zeros((L, LANES), F32)
        dcr_out = jnp.zeros((rp, L), F32)
        for r in range(R):
            cc = cum_c[:, r:r + 1]
            cr = cum_r[r:r + 1, :]
            lam = jnp.exp(jnp.where(tri, cc - cr, -jnp.inf))
            m = gm * lam
            dtc = dt[:, r:r + 1]
            xh = x_ref[:, r * P:(r + 1) * P]
            x = xh * dtc
            xb = x.astype(BF16)
            d_y = dy_ref[:, r * P:(r + 1) * P]
            d_yb = d_y.astype(BF16)
            s_r = st_ref[0, 0, r * P:(r + 1) * P, :]
            s_rb = s_r.astype(BF16)
            ds_n = dstate[r * P:(r + 1) * P, :]
            ds_nb = ds_n.astype(BF16)
            e = jnp.exp(cc)
            last = cr[:, L - 1:L]
            e_last = jnp.exp(last)
            w = jnp.exp(last - cc)
            d_m = _dot_nt(d_yb, xb)
            d_x = _dot_tn(m.astype(BF16), d_yb)
            d_ye = (d_y * e).astype(BF16)
            dc_acc = dc_acc + _dot(d_ye, s_rb)
            ds_part = _dot_tn(d_ye, cm)
            y_off = _dot_nt(cm, s_rb) * e
            dcum = jnp.sum(d_y * y_off, axis=1, keepdims=True)
            d_xw = _dot_nt(bm, ds_nb)
            d_x = d_x + d_xw * w
            dw_w = jnp.sum(d_xw * x, axis=1, keepdims=True) * w
            db_acc = db_acc + _dot((x * w).astype(BF16), ds_nb)
            d_last = jnp.sum(ds_n * s_r, keepdims=True) * e_last + jnp.sum(dw_w, keepdims=True)
            dcum = dcum - dw_w
            dstate[r * P:(r + 1) * P, :] = e_last * ds_n + ds_part
            d_g = d_g + d_m * lam
            q = d_m * m
            dcum = dcum + jnp.sum(q, axis=1, keepdims=True) + jnp.where(is_last, d_last, 0.0)
            dcum_row = -jnp.sum(q, axis=0, keepdims=True)
            dx_ref[:, r * P:(r + 1) * P] = d_x * dtc
            ddt = jnp.sum(d_x * xh, axis=1, keepdims=True)
            ddt_out = ddt_out + jnp.where(lane == r, ddt, 0.0)
            dcc_out = dcc_out + jnp.where(lane == r, dcum, 0.0)
            dcr_out = dcr_out + jnp.where(sub == r, dcum_row, 0.0)
        d_gb = d_g.astype(BF16)
        dc_ref[...] = dc_acc + _dot(d_gb, bm)
        db_ref[...] = db_acc + _dot_tn(d_gb, cm)
        ddt_ref[0] = ddt_out
        dcc_ref[0] = dcc_out
        dcr_ref[0] = dcr_out

    gn = n_groups * N
    return pl.pallas_call(
        body, name="ssd_bwd", grid=(n_groups, NC),
        in_specs=_ssd_specs(d_inner, n_groups, gw, L, rp, rev) + [
            pl.BlockSpec((1, 1, gw, N), lambda g, c: (g, rev(c), 0, 0)),
            pl.BlockSpec((L, gw), lambda g, c: (rev(c), g))],
        out_specs=[pl.BlockSpec((L, gw), lambda g, c: (rev(c), g)),
                   pl.BlockSpec((L, N), lambda g, c: (rev(c), g)),
                   pl.BlockSpec((L, N), lambda g, c: (rev(c), g)),
                   pl.BlockSpec((1, L, LANES), lambda g, c: (g, rev(c), 0)),
                   pl.BlockSpec((1, L, LANES), lambda g, c: (g, rev(c), 0)),
                   pl.BlockSpec((1, rp, L), lambda g, c: (g, 0, rev(c)))],
        out_shape=[jax.ShapeDtypeStruct((S, d_inner), F32), jax.ShapeDtypeStruct((S, gn), F32),
                   jax.ShapeDtypeStruct((S, gn), F32), jax.ShapeDtypeStruct((n_groups, S, LANES), F32),
                   jax.ShapeDtypeStruct((n_groups, S, LANES), F32), jax.ShapeDtypeStruct((n_groups, rp, S), F32)],
        scratch_shapes=[pltpu.VMEM((gw, N), F32)],
        compiler_params=_params(("parallel", "arbitrary")),
    )(xbc_a, xbc_a, xbc_a, dt_col, da_col, da_row, states, dy)


def _attn_scale():
    return float(QK_NOPE + QK_ROPE) ** -0.5


def _diag_mask(t, keys_first=False):
    rows = lax.broadcasted_iota(jnp.int32, (t, t), 0)
    cols = lax.broadcasted_iota(jnp.int32, (t, t), 1)
    return rows <= cols if keys_first else cols <= rows


def _tile_rows(ref, j, t):
    return ref[pl.ds(pl.multiple_of(j * t, t), t), :]


def _flash_fwd(q, k, v, n_heads):
    S = q.shape[0]
    T = min(FLASH_T, S)

    def body(q_ref, k_ref, v_ref, o_ref, lse_ref, m_s, l_s, acc):
        i = pl.program_id(1)
        m_s[...] = jnp.full_like(m_s, -jnp.inf)
        l_s[...] = jnp.zeros_like(l_s)
        acc[...] = jnp.zeros_like(acc)
        qv = q_ref[...]

        def step(j, masked):
            s = _dot_nt(qv, _tile_rows(k_ref, j, T))
            if masked:
                s = jnp.where(_diag_mask(T), s, -jnp.inf)
            m_prev = m_s[...]
            m_new = jnp.maximum(m_prev, jnp.max(s, axis=1, keepdims=True))
            alpha = jnp.exp(m_prev - m_new)
            p = jnp.exp(s - m_new[:, :1])
            l_s[...] = alpha * l_s[...] + jnp.sum(p, axis=1, keepdims=True)
            acc[...] = alpha * acc[...] + _dot(p.astype(BF16), _tile_rows(v_ref, j, T))
            m_s[...] = m_new

        def loop_body(j, carry):
            step(j, False)
            return carry

        lax.fori_loop(0, i, loop_body, 0)
        step(i, True)
        o_ref[...] = acc[...] / l_s[...]
        lse_ref[0] = (m_s[...] + jnp.log(l_s[...]))[:, :1]

    return pl.pallas_call(
        body, name="flash_fwd", grid=(n_heads, S // T),
        in_specs=[pl.BlockSpec((T, QK_PAD), lambda h, i: (i, h)),
                  pl.BlockSpec((S, QK_PAD), lambda h, i: (0, h)),
                  pl.BlockSpec((S, V_DIM), lambda h, i: (0, h))],
        out_specs=[pl.BlockSpec((T, V_DIM), lambda h, i: (i, h)),
                   pl.BlockSpec((1, T, 1), lambda h, i: (h, i, 0))],
        out_shape=[jax.ShapeDtypeStruct((S, n_heads * V_DIM), F32), jax.ShapeDtypeStruct((n_heads, S, 1), F32)],
        scratch_shapes=[pltpu.VMEM((T, V_DIM), F32), pltpu.VMEM((T, V_DIM), F32), pltpu.VMEM((T, V_DIM), F32)],
        compiler_params=_params(("parallel", "arbitrary")),
    )(q, k, v)


def _flash_bwd_dq(q, k, v, do, lse, delta, n_heads):
    S = q.shape[0]
    T = min(FLASH_T, S)

    def body(q_ref, k_ref, v_ref, do_ref, lse_ref, dl_ref, dq_ref, acc):
        i = pl.program_id(1)
        acc[...] = jnp.zeros_like(acc)
        qv = q_ref[...]
        dov = do_ref[...]
        lse_c = lse_ref[0]
        dl_c = dl_ref[0]

        def step(j, masked):
            kt = _tile_rows(k_ref, j, T)
            s = _dot_nt(qv, kt)
            if masked:
                s = jnp.where(_diag_mask(T), s, -jnp.inf)
            p = jnp.exp(s - lse_c)
            ds = p * (_dot_nt(dov, _tile_rows(v_ref, j, T)) - dl_c)
            acc[...] += _dot(ds.astype(BF16), kt)

        def loop_body(j, carry):
            step(j, False)
            return carry

        lax.fori_loop(0, i, loop_body, 0)
        step(i, True)
        dq_ref[...] = acc[...] * _attn_scale()

    return pl.pallas_call(
        body, name="flash_bwd_dq", grid=(n_heads, S // T),
        in_specs=[pl.BlockSpec((T, QK_PAD), lambda h, i: (i, h)),
                  pl.BlockSpec((S, QK_PAD), lambda h, i: (0, h)),
                  pl.BlockSpec((S, V_DIM), lambda h, i: (0, h)),
                  pl.BlockSpec((T, V_DIM), lambda h, i: (i, h)),
                  pl.BlockSpec((1, T, 1), lambda h, i: (h, i, 0)),
                  pl.BlockSpec((1, T, 1), lambda h, i: (h, i, 0))],
        out_specs=pl.BlockSpec((T, QK_PAD), lambda h, i: (i, h)),
        out_shape=jax.ShapeDtypeStruct((S, n_heads * QK_PAD), F32),
        scratch_shapes=[pltpu.VMEM((T, QK_PAD), F32)],
        compiler_params=_params(("parallel", "arbitrary")),
    )(q, k, v, do, lse, delta)


def _flash_bwd_dkv(q, k, v, do, lse_row, delta_row, n_heads):
    S = q.shape[0]
    T = min(FLASH_T, S)
    nq = S // T

    def body(q_ref, k_ref, v_ref, do_ref, lse_ref, dl_ref, dk_ref, dv_ref, dk_acc, dv_acc):
        j = pl.program_id(1)
        dk_acc[...] = jnp.zeros_like(dk_acc)
        dv_acc[...] = jnp.zeros_like(dv_acc)
        kv = k_ref[...]
        vv = v_ref[...]

        def step(i, masked):
            qt = _tile_rows(q_ref, i, T)
            dot = _tile_rows(do_ref, i, T)
            cols = pl.ds(pl.multiple_of(i * T, T), T)
            s_t = _dot_nt(kv, qt)
            if masked:
                s_t = jnp.where(_diag_mask(T, keys_first=True), s_t, -jnp.inf)
            p_t = jnp.exp(s_t - lse_ref[0, :, cols])
            dv_acc[...] += _dot(p_t.astype(BF16), dot)
            ds_t = p_t * (_dot_nt(vv, dot) - dl_ref[0, :, cols])
            dk_acc[...] += _dot(ds_t.astype(BF16), qt)

        def loop_body(i, carry):
            step(i, False)
            return carry

        step(j, True)
        lax.fori_loop(j + 1, nq, loop_body, 0)
        dk_ref[...] = dk_acc[...]
        dv_ref[...] = dv_acc[...].astype(dv_ref.dtype)

    return pl.pallas_call(
        body, name="flash_bwd_dkv", grid=(n_heads, S // T),
        in_specs=[pl.BlockSpec((S, QK_PAD), lambda h, j: (0, h)),
                  pl.BlockSpec((T, QK_PAD), lambda h, j: (j, h)),
                  pl.BlockSpec((T, V_DIM), lambda h, j: (j, h)),
                  pl.BlockSpec((S, V_DIM), lambda h, j: (0, h)),
                  pl.BlockSpec((1, 1, S), lambda h, j: (h, 0, 0)),
                  pl.BlockSpec((1, 1, S), lambda h, j: (h, 0, 0))],
        out_specs=[pl.BlockSpec((T, QK_PAD), lambda h, j: (j, h)),
                   pl.BlockSpec((T, V_DIM), lambda h, j: (j, h))],
        out_shape=[jax.ShapeDtypeStruct((S, n_heads * QK_PAD), F32), jax.ShapeDtypeStruct((S, n_heads * V_DIM), BF16)],
        scratch_shapes=[pltpu.VMEM((T, QK_PAD), F32), pltpu.VMEM((T, V_DIM), F32)],
        compiler_params=_params(("parallel", "arbitrary")),
    )(q, k, v, do, lse_row, delta_row)


def _st_delta(n_heads):
    def fn(do, o):
        prod = do * o
        lane = lax.broadcasted_iota(jnp.int32, (do.shape[0], LANES), 1)
        out = jnp.zeros((do.shape[0], LANES), F32)
        for h in range(n_heads):
            out = out + jnp.where(lane == h, jnp.sum(prod[:, h * V_DIM:(h + 1) * V_DIM], axis=1, keepdims=True), 0.0)
        return out
    return fn


def _pad_cols(w, width):
    return jnp.pad(w, ((0, 0), (0, width - w.shape[1])))


def _dims(x, p, q_norm, kv_norm, w_uq, dt_bias, ssm_norm, conv_b, w_gate):
    d = dict(S=x.shape[0], D=x.shape[1], PLE=p.shape[1], DQ=q_norm.shape[1], DKV=kv_norm.shape[1],
             NH=w_uq.shape[1] // (QK_NOPE + QK_ROPE), NHS=dt_bias.shape[1], DI=ssm_norm.shape[1],
             CONV=conv_b.shape[1], FF=w_gate.shape[1])
    d["G"] = (d["CONV"] - d["DI"]) // (2 * D_STATE)
    d["R"] = d["NHS"] // d["G"]
    return d


def _local_step(x, p, positions, target, small, conv_w, wts):
    dm = _dims(x, p, small["q_norm"], small["kv_norm"], wts["w_uq"], small["dt_bias"], small["ssm_norm"],
               small["conv_b"], wts["w_gate"])
    S, D, DQ, DKV, NH, NHS, DI, CONV, G, R = (dm[k] for k in ("S", "D", "DQ", "DKV", "NH", "NHS", "DI", "CONV", "G", "R"))
    rp = -(-R // SUBLANES) * SUBLANES
    L = min(CHUNK, S)

    w_in = wts["w_in"]
    o = [0]
    for n in (DQ, DKV, QK_ROPE, DI, CONV, NHS, D, D):
        o.append(o[-1] + n)
    w_cqkv, w_kr, w_z, w_xbc, w_dt, w_g = (w_in[:, o[0]:o[2]], w_in[:, o[2]:o[3]], w_in[:, o[3]:o[4]],
                                           w_in[:, o[4]:o[5]], w_in[:, o[5]:o[6]], w_in[:, o[6]:o[8]])
    zc = lambda n: jnp.zeros((D, n), BF16)
    w_sm = jnp.concatenate([zc(QK_NOPE), w_kr, zc(QK_PAD - QK_NOPE - QK_ROPE), w_dt, zc(LANES - NHS)], axis=1)
    w_q = jnp.pad(wts["w_uq"].reshape(DQ, NH, QK_NOPE + QK_ROPE),
                  ((0, 0), (0, 0), (0, QK_PAD - QK_NOPE - QK_ROPE))).reshape(DQ, NH * QK_PAD)
    ukv = wts["w_ukv"].reshape(DKV, NH, QK_NOPE + V_DIM)
    w_k = jnp.pad(ukv[:, :, :QK_NOPE], ((0, 0), (0, 0), (0, QK_PAD - QK_NOPE))).reshape(DKV, NH * QK_PAD)
    w_v = ukv[:, :, QK_NOPE:].reshape(DKV, NH * V_DIM)
    dt_bias_p, a_log_p = _pad_cols(small["dt_bias"], LANES), _pad_cols(small["a_log"], LANES)
    dskip_rep = jnp.repeat(small["d_skip"], SSM_HEADDIM, axis=1)
    c_tab, a_tab, b_tab = _rope_tables(positions)

    (u,) = _rowwise("pre_norm", _st_pre, [x], [small["mix_norm_pre"]], [(D, BF16)])
    cqkv = _mm("in_cqkv", u, w_cqkv, "nn")
    z = _mm("in_z", u, w_z, "nn")
    xbc = _mm("in_xbc", u, w_xbc, "nn")
    g = _mm("in_gates", u, w_g, "nn")
    sm = _mm("in_small", u, w_sm, "nn")

    lora_fn = _st_lora_norms(DQ)
    cq_n, ckv_n = _rowwise("lora_norms", lora_fn, [cqkv], [small["q_norm"], small["kv_norm"]], [(DQ, BF16), (DKV, BF16)])
    qraw = _mm("up_q", cq_n, w_q, "nn")
    kraw = _mm("up_k", ckv_n, w_k, "nn")
    v = _mm("up_v", ckv_n, w_v, "nn", out_dtype=BF16)
    q, k = _rowwise("rope", _st_rope(NH), [qraw, kraw, sm, c_tab, a_tab, b_tab], [],
                    [(NH * QK_PAD, BF16), (NH * QK_PAD, BF16)])
    attn, lse = _flash_fwd(q, k, v, NH)

    xbc_c, xbc_a = _conv_fwd(xbc, conv_w, small["conv_b"])
    dt, da = _rowwise("dt", _st_dt, [sm], [dt_bias_p, a_log_p], [(LANES, F32), (LANES, F32)])

    def col_layout(t):
        return _pad_cols(t[:, :NHS].reshape(S, G, R).transpose(1, 0, 2).reshape(G * S, R), LANES).reshape(G, S, LANES)

    def row_layout(t):
        return jnp.pad(t[:, :NHS].reshape(S, G, R).transpose(1, 2, 0), ((0, 0), (0, rp - R), (0, 0)))

    dt_col, da_col, da_row = col_layout(dt), col_layout(da), row_layout(da)
    y, states = _ssd_fwd(xbc_a, dt_col, da_col, da_row, DI, G, R)
    xs = xbc_a[:, :DI]
    gn_fn = _st_gated_norm(G)
    (ssm,) = _rowwise("gated_norm", gn_fn, [y, xs, z], [dskip_rep, small["ssm_norm"]], [(DI, BF16)])

    ao = _mm("attn_o", attn, wts["w_attn_o"], "nn")
    so = _mm("ssm_o", ssm, wts["w_ssm_o"], "nn")
    mix_fn = _st_mix(D)
    (mixed,) = _rowwise("mix", mix_fn, [g, ao, so], [], [(D, BF16)])
    mo = _mm("out_proj", mixed, wts["w_out"], "nn")
    h1, f = _rowwise("res1", _st_res_norm, [x, mo], [small["mix_norm_post"], small["ffn_norm_pre"]], [(D, F32), (D, BF16)])
    gt = _mm("ffn_gate", f, wts["w_gate"], "nn")
    up = _mm("ffn_up", f, wts["w_up"], "nn")
    (act,) = _rowwise("swiglu", _st_swiglu, [gt, up], [], [(gt.shape[1], BF16)])
    dn = _mm("ffn_down", act, wts["w_down"], "nn")
    h2, a3 = _rowwise("res2", _st_res_norm, [h1, dn], [small["ffn_norm_post"], small["ple_norm_pre"]], [(D, F32), (D, BF16)])
    gl = _mm("ple_gate", a3, wts["w_ple_gate"], "nn")
    pe = _mm("ple_proj", p, wts["w_ple"], "nn")

    sg = {}
    bg = {}
    dpe, dgl, dh2, d_w, loss_acc = _rowwise_bwd(
        "loss", _st_loss, [pe, gl, h2, target], [small["ple_norm_post"]], [1.0], [0, 1, 2], [BF16, BF16, F32], fwd_sums=(0,))
    sg["ple_norm_post"] = _fold(d_w)
    loss = loss_acc[0, 0]
    bg["w_ple"] = _mm("d_w_ple", p, dpe, "tn", out_dtype=BF16)
    bg["w_ple_gate"] = _mm("d_w_ple_gate", a3, dgl, "tn", out_dtype=BF16)
    da3 = _mm("d_a3", dgl, wts["w_ple_gate"], "nt")

    dh1, ddn, d_post, d_pre = _rowwise_bwd(
        "res2_bwd", _st_res_norm, [h1, dn], [small["ffn_norm_post"], small["ple_norm_pre"]], [dh2, da3], [0, 1], [F32, BF16])
    sg["ffn_norm_post"], sg["ple_norm_pre"] = _fold(d_post), _fold(d_pre)
    bg["w_down"] = _mm("d_w_down", act, ddn, "tn", out_dtype=BF16)
    dact = _mm("d_act", ddn, wts["w_down"], "nt")
    dgt, dup = _rowwise_bwd("swiglu_bwd", _st_swiglu, [gt, up], [], [dact], [0, 1], [BF16, BF16])
    bg["w_gate"] = _mm("d_w_gate", f, dgt, "tn", out_dtype=BF16)
    bg["w_up"] = _mm("d_w_up", f, dup, "tn", out_dtype=BF16)
    df = _mm("d_f_gate", dgt, wts["w_gate"], "nt")
    df = _mm("d_f_up", dup, wts["w_up"], "nt", acc_in=df)

    dx_res, dmo, d_post, d_pre = _rowwise_bwd(
        "res1_bwd", _st_res_norm, [x, mo], [small["mix_norm_post"], small["ffn_norm_pre"]], [dh1, df], [0, 1], [F32, BF16])
    sg["mix_norm_post"], sg["ffn_norm_pre"] = _fold(d_post), _fold(d_pre)
    bg["w_out"] = _mm("d_w_out", mixed, dmo, "tn", out_dtype=BF16)
    dmixed = _mm("d_mixed", dmo, wts["w_out"], "nt")
    dg, dao, dso = _rowwise_bwd("mix_bwd", mix_fn, [g, ao, so], [], [dmixed], [0, 1, 2], [BF16, BF16, BF16])
    bg["w_attn_o"] = _mm("d_w_attn_o", attn, dao, "tn", out_dtype=BF16)
    bg["w_ssm_o"] = _mm("d_w_ssm_o", ssm, dso, "tn", out_dtype=BF16)
    dattn = _mm("d_attn", dao, wts["w_attn_o"], "nt", out_dtype=BF16)
    dssm = _mm("d_ssm", dso, wts["w_ssm_o"], "nt")

    dy, dxs_a, dz, d_dskip, d_ssmn = _rowwise_bwd(
        "gated_norm_bwd", gn_fn, [y, xs, z], [dskip_rep, small["ssm_norm"]], [dssm], [0, 1, 2], [F32, F32, BF16])
    sg["d_skip"] = _fold(d_dskip).reshape(NHS, SSM_HEADDIM).sum(axis=1).reshape(1, NHS)
    sg["ssm_norm"] = _fold(d_ssmn)
    dxs_b, d_b, d_c, ddt_col, dcum_col, dcum_row = _ssd_bwd(xbc_a, dt_col, da_col, da_row, states, dy, DI, G, R)

    def from_col(t):
        return _pad_cols(t[:, :, :R].transpose(1, 0, 2).reshape(S, NHS), LANES)

    dcum = from_col(dcum_col) + _pad_cols(dcum_row[:, :R, :].transpose(2, 0, 1).reshape(S, NHS), LANES)
    ddtraw, d_bias, d_alog = _rowwise("dt_bwd", _st_dt_bwd, [sm, from_col(ddt_col), dcum], [dt_bias_p, a_log_p],
                                      [(LANES, F32)], accs=(LANES, LANES), tr=L)
    sg["dt_bias"], sg["a_log"] = _fold(d_bias)[:, :NHS], _fold(d_alog)[:, :NHS]
    dconv, d_cb = _rowwise("dconv", _st_dconv(DI), [xbc_c, dxs_a, dxs_b, d_b, d_c], [], [(CONV, F32)], accs=(CONV,))
    sg["conv_b"] = _fold(d_cb)
    dxbc, d_cw = _conv_bwd(xbc, dconv, conv_w)
    d_conv_w = d_cw.reshape(CONV_WIDTH, SUBLANES, CONV).sum(axis=1)

    (delta,) = _rowwise("attn_delta", _st_delta(NH), [dattn, attn], [], [(LANES, F32)])
    delta = delta[:, :NH].T
    dq = _flash_bwd_dq(q, k, v, dattn, lse, delta.reshape(NH, S, 1), NH)
    dk, dv = _flash_bwd_dkv(q, k, v, dattn, lse.reshape(NH, 1, S), delta.reshape(NH, 1, S), NH)
    dqraw, dkr, dk = _rowwise("rope_bwd", _st_rope_bwd(NH), [dq, dk, c_tab, a_tab, b_tab], [],
                              [(NH * QK_PAD, BF16), (QK_PAD, F32), (NH * QK_PAD, BF16)])
    d_w_q = _mm("d_w_q", cq_n, dqraw, "tn", out_dtype=BF16)
    d_w_k = _mm("d_w_k", ckv_n, dk, "tn", out_dtype=BF16)
    d_w_v = _mm("d_w_v", ckv_n, dv, "tn", out_dtype=BF16)
    dcq_n = _mm("d_cq_n", dqraw, w_q, "nt")
    dckv_n = _mm("d_ckv_n_k", dk, w_k, "nt")
    dckv_n = _mm("d_ckv_n_v", dv, w_v, "nt", acc_in=dckv_n)
    bg["w_uq"] = d_w_q.reshape(DQ, NH, QK_PAD)[:, :, :QK_NOPE + QK_ROPE].reshape(DQ, NH * (QK_NOPE + QK_ROPE))
    bg["w_ukv"] = jnp.concatenate([d_w_k.reshape(DKV, NH, QK_PAD)[:, :, :QK_NOPE], d_w_v.reshape(DKV, NH, V_DIM)],
                                  axis=2).reshape(DKV, NH * (QK_NOPE + V_DIM))
    dcqkv, d_qn, d_kvn = _rowwise_bwd("lora_norms_bwd", lora_fn, [cqkv], [small["q_norm"], small["kv_norm"]],
                                      [dcq_n, dckv_n], [0], [BF16])
    sg["q_norm"], sg["kv_norm"] = _fold(d_qn), _fold(d_kvn)

    dsm = jnp.concatenate([dkr, ddtraw], axis=1)
    d_w_cqkv = _mm("d_w_cqkv", u, dcqkv, "tn", out_dtype=BF16)
    d_w_z = _mm("d_w_z", u, dz, "tn", out_dtype=BF16)
    d_w_xbc = _mm("d_w_xbc", u, dxbc, "tn", out_dtype=BF16)
    d_w_g = _mm("d_w_gates", u, dg, "tn", out_dtype=BF16)
    d_w_sm = _mm("d_w_small", u, dsm, "tn", out_dtype=BF16)
    bg["w_in"] = jnp.concatenate(
        [d_w_cqkv, d_w_sm[:, QK_NOPE:QK_NOPE + QK_ROPE], d_w_z, d_w_xbc, d_w_sm[:, QK_PAD:QK_PAD + NHS], d_w_g], axis=1)
    du = _mm("d_u_cqkv", dcqkv, w_cqkv, "nt")
    du = _mm("d_u_z", dz, w_z, "nt", acc_in=du)
    du = _mm("d_u_xbc", dxbc, w_xbc, "nt", acc_in=du)
    du = _mm("d_u_gates", dg, w_g, "nt", acc_in=du)
    du = _mm("d_u_small", dsm, w_sm, "nt", acc_in=du)
    grad_x, d_pre = _rowwise_bwd("pre_norm_bwd", _st_pre, [x], [small["mix_norm_pre"]], [du], [0], [F32], adds={0: dx_res})
    sg["mix_norm_pre"] = _fold(d_pre)
    return loss, grad_x, sg, d_conv_w, bg


_HBM = pl.BlockSpec(memory_space=pltpu.HBM)


def _place():
    return lax.axis_index("x"), lax.axis_index("y"), lax.axis_index("c")


def _all_gather(name, blocks):
    nw = len(blocks)

    def body(*refs):
        x_refs, out_refs = refs[:nw], refs[nw:2 * nw]
        send_sems, recv_sems, local_sems = refs[2 * nw:]
        x, y, c = _place()
        me, sibling = (x, y, c), (x, y, 1 - c)
        chips = [(1 - x, y), (x, 1 - y), (1 - x, 1 - y)]

        def slot(w, px, py, pc):
            return out_refs[w].at[4 * px + 2 * py + pc]

        def copy(w, k, blk, to, src=None):
            return pltpu.make_async_remote_copy(
                src_ref=slot(w, *blk) if src is None else src, dst_ref=slot(w, *blk),
                send_sem=send_sems.at[7 * w + k], recv_sem=recv_sems.at[7 * w + k], device_id=to, device_id_type=MESH)

        mine = [pltpu.make_async_copy(x_refs[w], slot(w, *me), local_sems.at[w]) for w in range(nw)]
        for cp in mine:
            cp.start()
        first = []
        for w in range(nw):
            first.append(copy(w, 0, me, sibling, src=x_refs[w]))
            first += [copy(w, 1 + j, me, (*chip, c), src=x_refs[w]) for j, chip in enumerate(chips)]
        for cp in first:
            cp.start()
        passed = []
        for j, chip in enumerate(chips):
            for w in range(nw):
                copy(w, 1 + j, (*chip, c), me).wait_recv()
                passed.append(copy(w, 4 + j, (*chip, c), sibling))
                passed[-1].start()
        for w in range(nw):
            copy(w, 0, sibling, me).wait_recv()
        for j, chip in enumerate(chips):
            for w in range(nw):
                copy(w, 4 + j, (*chip, 1 - c), me).wait_recv()
        for cp in first + passed:
            cp.wait_send()
        for cp in mine:
            cp.wait()

    return pl.pallas_call(
        body, name=name, out_shape=[jax.ShapeDtypeStruct((N_DEV,) + b.shape, b.dtype) for b in blocks],
        in_specs=[_HBM] * nw, out_specs=[_HBM] * nw,
        scratch_shapes=[pltpu.SemaphoreType.DMA((7 * nw,)), pltpu.SemaphoreType.DMA((7 * nw,)),
                        pltpu.SemaphoreType.DMA((nw,))],
    )(*blocks)


def _swap_with_sibling(name, by_dev):
    nw = len(by_dev)

    def body(*refs):
        g_refs, got_refs = refs[:nw], refs[nw:2 * nw]
        send_sems, recv_sems = refs[2 * nw:]
        x, y, c = _place()
        cps = [pltpu.make_async_remote_copy(src_ref=g_refs[w].at[2 * q + (1 - c)], dst_ref=got_refs[w].at[q],
                                            send_sem=send_sems.at[4 * w + q], recv_sem=recv_sems.at[4 * w + q],
                                            device_id=(x, y, 1 - c), device_id_type=MESH)
               for w in range(nw) for q in range(4)]
        for cp in cps:
            cp.start()
        for cp in cps:
            cp.wait()

    return pl.pallas_call(
        body, name=name, out_shape=[jax.ShapeDtypeStruct((4,) + b.shape[1:], b.dtype) for b in by_dev],
        in_specs=[_HBM] * nw, out_specs=[_HBM] * nw,
        scratch_shapes=[pltpu.SemaphoreType.DMA((4 * nw,)), pltpu.SemaphoreType.DMA((4 * nw,))],
    )(*by_dev)


def _swap_with_chips(name, sends):
    nw = len(sends)

    def body(*refs):
        s_refs, r_refs = refs[:nw], refs[nw:2 * nw]
        send_sems, recv_sems = refs[2 * nw:]
        x, y, c = _place()
        chips = [(1 - x, y), (x, 1 - y), (1 - x, 1 - y)]
        cps = [pltpu.make_async_remote_copy(src_ref=s_refs[w].at[2 * px + py], dst_ref=r_refs[w].at[k],
                                            send_sem=send_sems.at[3 * w + k], recv_sem=recv_sems.at[3 * w + k],
                                            device_id=(px, py, c), device_id_type=MESH)
               for w in range(nw) for k, (px, py) in enumerate(chips)]
        for cp in cps:
            cp.start()
        for cp in cps:
            cp.wait()

    return pl.pallas_call(
        body, name=name, out_shape=[jax.ShapeDtypeStruct((3,) + s.shape[1:], s.dtype) for s in sends],
        in_specs=[_HBM] * nw, out_specs=[_HBM] * nw,
        scratch_shapes=[pltpu.SemaphoreType.DMA((3 * nw,)), pltpu.SemaphoreType.DMA((3 * nw,))],
    )(*sends)


def _lane_pad(n):
    return -(-n // LANES) * LANES


def _pack_small(vecs, mat):
    width = max(sum(_lane_pad(v.shape[1]) for v in vecs), _lane_pad(mat.shape[1]))
    row0 = jnp.concatenate([_pad_cols(v, _lane_pad(v.shape[1])) for v in vecs], axis=1)
    rows = jnp.concatenate([_pad_cols(row0, width), _pad_cols(mat, width)], axis=0)
    return jnp.pad(rows, ((0, SUBLANES - rows.shape[0]), (0, 0)))


def _unpack_small(packed, sizes, mat_cols):
    vecs, off = [], 0
    for n in sizes:
        vecs.append(packed[0:1, off:off + n])
        off += _lane_pad(n)
    return vecs, packed[1:1 + CONV_WIDTH, :mat_cols]


def _adamw(w, g, m, v):
    m = ADAM_B1 * m + (1.0 - ADAM_B1) * g
    v = ADAM_B2 * v + (1.0 - ADAM_B2) * (g * g)
    m_hat = m / (1.0 - ADAM_B1 ** ADAM_STEP)
    v_hat = v / (1.0 - ADAM_B2 ** ADAM_STEP)
    delta = -ADAM_LR * (m_hat / (jnp.sqrt(v_hat) + ADAM_EPS) + ADAM_WD * w)
    return delta, m, v


BIG = ("w_in", "w_uq", "w_ukv", "w_attn_o", "w_ssm_o", "w_out", "w_gate", "w_up", "w_down", "w_ple_gate", "w_ple")
COL_SHARDED = ("w_in", "w_uq", "w_ukv", "w_gate", "w_up", "w_ple")
SMALL = ("mix_norm_pre", "mix_norm_post", "q_norm", "kv_norm", "conv_b", "dt_bias", "a_log", "d_skip", "ssm_norm",
         "ffn_norm_pre", "ffn_norm_post", "ple_norm_pre", "ple_norm_post")
WEIGHTS = ("mix_norm_pre", "mix_norm_post", "w_in", "q_norm", "w_uq", "kv_norm", "w_ukv", "conv_w", "conv_b", "dt_bias",
           "a_log", "d_skip", "ssm_norm", "w_attn_o", "w_ssm_o", "w_out", "ffn_norm_pre", "ffn_norm_post", "w_gate",
           "w_up", "w_down", "ple_norm_pre", "ple_norm_post", "w_ple_gate", "w_ple")


def _step(x, p, positions, target, w, m, v):
    xi, yi, ci = _place()
    me = 4 * xi + 2 * yi + ci
    chip = 2 * xi + yi

    gathered = _all_gather("gather_weights", [w[n].astype(BF16) for n in BIG])
    wts = {}
    for n, blocks in zip(BIG, gathered):
        rows, cols = blocks.shape[1:]
        if n in COL_SHARDED:
            wts[n] = blocks.transpose(1, 0, 2).reshape(rows, N_DEV * cols)
        else:
            wts[n] = blocks.reshape(N_DEV * rows, cols)
    cw_rows, cw_cols = w["conv_w"].shape
    (cw_all,) = _all_gather("gather_conv_w", [jnp.pad(w["conv_w"], ((0, SUBLANES - cw_rows), (0, 0)))])
    conv_w = cw_all[:, :cw_rows, :].transpose(1, 0, 2).reshape(cw_rows, N_DEV * cw_cols)

    small = {n: w[n] for n in SMALL}
    loss, grad_x, sg, d_conv_w, bg = _local_step(x, p, positions, target, small, conv_w, wts)

    sizes = [w[n].shape[1] for n in SMALL]
    sg_pack = _pack_small([sg[n] for n in SMALL], d_conv_w)
    (sg_all,) = _all_gather("gather_small_grads", [sg_pack])
    (sg_sum,) = _rowwise("sum_small_grads", lambda *a: functools.reduce(lambda s, t: s + t, a),
                         [sg_all[k] for k in range(N_DEV)], [], [(sg_pack.shape[1], F32)])
    sg_vecs, d_conv_w_sum = _unpack_small(sg_sum, sizes, d_conv_w.shape[1])
    grads = dict(zip(SMALL, sg_vecs))
    grads["conv_w"] = lax.dynamic_slice_in_dim(d_conv_w_sum, me * cw_cols, cw_cols, axis=1)

    by_dev = []
    for n in BIG:
        rows, cols = w[n].shape
        if n in COL_SHARDED:
            by_dev.append(bg[n].reshape(rows, N_DEV, cols).transpose(1, 0, 2))
        else:
            by_dev.append(bg[n].reshape(N_DEV, rows, cols))
    gots = _swap_with_sibling("rs_within_chip", by_dev)
    chip_f32, chip_bf16 = [], []
    for n, b, got in zip(BIG, by_dev, gots):
        rows, cols = w[n].shape
        keep = lax.dynamic_index_in_dim(b.reshape(4, 2, rows, cols), ci, axis=1, keepdims=False)
        s32, s16 = _rowwise("rs_add_sibling_" + n, lambda a, b_: (a + b_, a + b_),
                            [keep.reshape(4 * rows, cols), got.reshape(4 * rows, cols)], [], [(cols, F32), (cols, BF16)])
        chip_f32.append(s32.reshape(4, rows, cols))
        chip_bf16.append(s16.reshape(4, rows, cols))
    recvs = _swap_with_chips("rs_between_chips", chip_bf16)

    def sum_then_adamw(wv, mv, vv, own, r0, r1, r2):
        g = ((own + r0) + r1) + r2
        return (g,) + _adamw(wv, g, mv, vv)

    delta, new_m, new_v = {}, {}, {}
    for n, c32, recv in zip(BIG, chip_f32, recvs):
        cols = w[n].shape[1]
        own = lax.dynamic_index_in_dim(c32, chip, axis=0, keepdims=False)
        grads[n], delta[n], new_m[n], new_v[n] = _rowwise(
            "adamw_" + n, sum_then_adamw, [w[n], m[n], v[n], own, recv[0], recv[1], recv[2]], [], [(cols, F32)] * 4)
    packed = [_pack_small([d[n] for n in SMALL], d["conv_w"]) for d in (w, grads, m, v)]
    outs = _rowwise("adamw_small", _adamw, packed, [], [(packed[0].shape[1], F32)] * 3)
    for d, o in zip((delta, new_m, new_v), outs):
        vecs, mat = _unpack_small(o, sizes, cw_cols)
        d.update(zip(SMALL, vecs))
        d["conv_w"] = mat
    return loss, grad_x, grads, delta, new_m, new_v


def kernel(x, p, positions, mix_norm_pre, mix_norm_post, w_in, q_norm, w_uq, kv_norm, w_ukv, conv_w, conv_b, dt_bias, a_log, d_skip, ssm_norm, w_attn_o, w_ssm_o, w_out, ffn_norm_pre, ffn_norm_post, w_gate, w_up, w_down, ple_norm_pre, ple_norm_post, w_ple_gate, w_ple, loss_target, m_mix_norm_pre, m_mix_norm_post, m_w_in, m_q_norm, m_w_uq, m_kv_norm, m_w_ukv, m_conv_w, m_conv_b, m_dt_bias, m_a_log, m_d_skip, m_ssm_norm, m_w_attn_o, m_w_ssm_o, m_w_out, m_ffn_norm_pre, m_ffn_norm_post, m_w_gate, m_w_up, m_w_down, m_ple_norm_pre, m_ple_norm_post, m_w_ple_gate, m_w_ple, v_mix_norm_pre, v_mix_norm_post, v_w_in, v_q_norm, v_w_uq, v_kv_norm, v_w_ukv, v_conv_w, v_conv_b, v_dt_bias, v_a_log, v_d_skip, v_ssm_norm, v_w_attn_o, v_w_ssm_o, v_w_out, v_ffn_norm_pre, v_ffn_norm_post, v_w_gate, v_w_up, v_w_down, v_ple_norm_pre, v_ple_norm_post, v_w_ple_gate, v_w_ple):
    w_args = (mix_norm_pre, mix_norm_post, w_in, q_norm, w_uq, kv_norm, w_ukv, conv_w, conv_b, dt_bias, a_log, d_skip, ssm_norm, w_attn_o, w_ssm_o, w_out, ffn_norm_pre, ffn_norm_post, w_gate, w_up, w_down, ple_norm_pre, ple_norm_post, w_ple_gate, w_ple)
    m_args = (m_mix_norm_pre, m_mix_norm_post, m_w_in, m_q_norm, m_w_uq, m_kv_norm, m_w_ukv, m_conv_w, m_conv_b, m_dt_bias, m_a_log, m_d_skip, m_ssm_norm, m_w_attn_o, m_w_ssm_o, m_w_out, m_ffn_norm_pre, m_ffn_norm_post, m_w_gate, m_w_up, m_w_down, m_ple_norm_pre, m_ple_norm_post, m_w_ple_gate, m_w_ple)
    v_args = (v_mix_norm_pre, v_mix_norm_post, v_w_in, v_q_norm, v_w_uq, v_kv_norm, v_w_ukv, v_conv_w, v_conv_b, v_dt_bias, v_a_log, v_d_skip, v_ssm_norm, v_w_attn_o, v_w_ssm_o, v_w_out, v_ffn_norm_pre, v_ffn_norm_post, v_w_gate, v_w_up, v_w_down, v_ple_norm_pre, v_ple_norm_post, v_w_ple_gate, v_w_ple)

    def drop_layer(a):
        return a if a.ndim == 2 else a[0]

    w = {n: drop_layer(a) for n, a in zip(WEIGHTS, w_args)}
    m = {n: drop_layer(a) for n, a in zip(WEIGHTS, m_args)}
    v = {n: drop_layer(a) for n, a in zip(WEIGHTS, v_args)}
    loss, grad_x, grads, delta, new_m, new_v = _step(x[0], p[0, 0], positions[0], loss_target[0], w, m, v)
    loss = lax.psum(loss, ("x", "y", "c"))
    like = lambda d: [d[n].reshape(a.shape) for n, a in zip(WEIGHTS, w_args)]
    return (loss, grad_x[None], *like(grads), *like(delta), *like(new_m), *like(new_v))
```

```python
import functools

import jax
import jax.numpy as jnp
from jax import lax
from jax.experimental import pallas as pl
from jax.experimental.pallas import tpu as pltpu

F32 = jnp.float32
BF16 = jnp.bfloat16

EPS = 1e-6
QK_NOPE = 128
QK_ROPE = 64
V_DIM = 128
QK_PAD = 256
ROPE_THETA = 10000.0
SSM_HEADDIM = 64
D_STATE = 128
CONV_WIDTH = 4
CHUNK = 256
ADAM_LR = 0.001
ADAM_B1 = 0.9
ADAM_B2 = 0.999
ADAM_EPS = 1e-08
ADAM_WD = 0.01
ADAM_STEP = 10

N_DEV = 8
LANES = 128
SUBLANES = 8
PACK_W = 1024
VMEM_LIMIT = 56 * 1024 * 1024
ROW_TILE_BYTES = 6 * 1024 * 1024
FLASH_T = 512
MM_TILE_BYTES = 20 * 1024 * 1024
MESH = pl.DeviceIdType.MESH


def _pick(dim, prefs):
    if dim <= prefs[0]:
        return dim
    for p in prefs:
        if dim % p == 0:
            return p
    return dim


def _tile(dim, cap):
    if dim <= cap:
        return dim
    best = None
    for t in range(LANES, cap + 1, LANES):
        if dim % t == 0:
            best = t
    return best if best is not None else dim


def _params(sem):
    return pltpu.CompilerParams(dimension_semantics=sem, vmem_limit_bytes=VMEM_LIMIT)


def _dot(a, b):
    return lax.dot_general(a, b, (((1,), (0,)), ((), ())), preferred_element_type=F32)


def _dot_nt(a, b):
    return lax.dot_general(a, b, (((1,), (1,)), ((), ())), preferred_element_type=F32)


def _dot_tn(a, b):
    return lax.dot_general(a, b, (((0,), (0,)), ((), ())), preferred_element_type=F32)


def _mm(name, a, b, mode, out_dtype=F32, acc_in=None):
    if mode == "nn":
        (M, K), (K2, N) = a.shape, b.shape
    elif mode == "nt":
        (M, K), (N, K2) = a.shape, b.shape
    else:
        (K, M), (K2, N) = a.shape, b.shape
    assert K == K2, (name, a.shape, b.shape, mode)
    tm = _tile(M, 1024)
    tn = _tile(N, 1024 if acc_in is not None else 1536)
    tk = _tile(K, 2048)
    while tk > 512 and 2 * (tm * tk * a.dtype.itemsize + tk * tn * b.dtype.itemsize) > MM_TILE_BYTES:
        tk = _tile(K, tk - LANES)
    nk = K // tk
    dot = {"nn": _dot, "nt": _dot_nt, "tn": _dot_tn}[mode]
    has_acc = acc_in is not None

    def body(*refs):
        if has_acc:
            a_ref, b_ref, c_ref, o_ref, acc = refs
        else:
            a_ref, b_ref, o_ref, acc = refs
        k = pl.program_id(2)

        @pl.when(k == 0)
        def _():
            acc[...] = jnp.zeros_like(acc)

        acc[...] += dot(a_ref[...].astype(BF16), b_ref[...].astype(BF16))

        @pl.when(k == nk - 1)
        def _():
            r = acc[...]
            if has_acc:
                r = r + c_ref[...]
            o_ref[...] = r.astype(o_ref.dtype)

    if mode == "tn":
        a_spec = pl.BlockSpec((tk, tm), lambda i, j, k: (k, i))
    else:
        a_spec = pl.BlockSpec((tm, tk), lambda i, j, k: (i, k))
    if mode == "nt":
        b_spec = pl.BlockSpec((tn, tk), lambda i, j, k: (j, k))
    else:
        b_spec = pl.BlockSpec((tk, tn), lambda i, j, k: (k, j))
    o_spec = pl.BlockSpec((tm, tn), lambda i, j, k: (i, j))
    in_specs = [a_spec, b_spec] + ([o_spec] if has_acc else [])
    args = (a, b) + ((acc_in,) if has_acc else ())
    return pl.pallas_call(
        body, name=name, grid=(M // tm, N // tn, nk), in_specs=in_specs, out_specs=o_spec,
        out_shape=jax.ShapeDtypeStruct((M, N), out_dtype), scratch_shapes=[pltpu.VMEM((tm, tn), F32)],
        input_output_aliases=({2: 0} if has_acc and out_dtype == F32 else {}),
        compiler_params=_params(("parallel", "parallel", "arbitrary")),
    )(*args)


def _row_tile(n_rows, bytes_per_row):
    tr = 512
    while tr > SUBLANES and tr * bytes_per_row > ROW_TILE_BYTES:
        tr //= 2
    while n_rows % tr:
        tr //= 2
    return tr


def _acc_add(a_ref, v):
    if v.shape[0] == 1:
        a_ref[0:1, :] += v
    else:
        a_ref[...] += v.reshape(v.shape[0] // SUBLANES, SUBLANES, v.shape[1]).sum(axis=0)


def _rowwise(name, fn, rows, bcs, outs, accs=(), tr=None):
    n_rows = rows[0].shape[0]
    if tr is None:
        per_row = sum(r.shape[1] * r.dtype.itemsize for r in rows) + sum(w * jnp.dtype(d).itemsize for w, d in outs)
        tr = _row_tile(n_rows, per_row)
    n_r, n_b, n_o, n_a = len(rows), len(bcs), len(outs), len(accs)

    def body(*refs):
        ins = [r[...].astype(F32) for r in refs[: n_r + n_b]]
        res = fn(*ins)
        res = res if isinstance(res, (tuple, list)) else (res,)
        o_refs = refs[n_r + n_b: n_r + n_b + n_o]
        a_refs = refs[n_r + n_b + n_o:]
        for o, v in zip(o_refs, res[:n_o]):
            o[...] = v.astype(o.dtype)
        if n_a:
            @pl.when(pl.program_id(0) == 0)
            def _():
                for a in a_refs:
                    a[...] = jnp.zeros_like(a)

            for a, v in zip(a_refs, res[n_o:]):
                _acc_add(a, v)

    in_specs = [pl.BlockSpec((tr, r.shape[1]), lambda i: (i, 0)) for r in rows]
    in_specs += [pl.BlockSpec((1, b.shape[1]), lambda i: (0, 0)) for b in bcs]
    out_specs = [pl.BlockSpec((tr, w), lambda i: (i, 0)) for w, _ in outs]
    out_specs += [pl.BlockSpec((SUBLANES, w), lambda i: (0, 0)) for w in accs]
    out_shape = [jax.ShapeDtypeStruct((n_rows, w), d) for w, d in outs]
    out_shape += [jax.ShapeDtypeStruct((SUBLANES, w), F32) for w in accs]
    res = pl.pallas_call(
        body, name=name, grid=(n_rows // tr,), in_specs=in_specs, out_specs=out_specs, out_shape=out_shape,
        compiler_params=_params(("arbitrary",) if n_a else ("parallel",)),
    )(*rows, *bcs)
    return tuple(res)


def _rowwise_bwd(name, fn, rows, bcs, cts, need_rows, row_dtypes, need_bcs=None, adds=None, fwd_sums=(), tr=None):
    n_rows = rows[0].shape[0]
    adds = adds or {}
    need_bcs = list(range(len(bcs))) if need_bcs is None else list(need_bcs)
    ct_arrays = [c for c in cts if not isinstance(c, float)]
    add_keys = sorted(adds)
    add_arrays = [adds[k] for k in add_keys]
    if tr is None:
        per_row = sum(r.shape[1] * r.dtype.itemsize for r in list(rows) + ct_arrays + add_arrays)
        per_row += sum(rows[i].shape[1] * jnp.dtype(d).itemsize for i, d in zip(need_rows, row_dtypes))
        tr = _row_tile(n_rows, 2 * per_row)
    n_r, n_b, n_c, n_ad = len(rows), len(bcs), len(ct_arrays), len(add_arrays)
    n_go, n_gb, n_fs = len(need_rows), len(need_bcs), len(fwd_sums)

    def body(*refs):
        pos = 0
        r_t = [r[...].astype(F32) for r in refs[pos: pos + n_r]]
        pos += n_r
        b_t = [r[...].astype(F32) for r in refs[pos: pos + n_b]]
        pos += n_b
        c_t = [r[...].astype(F32) for r in refs[pos: pos + n_c]]
        pos += n_c
        ad_t = [r[...].astype(F32) for r in refs[pos: pos + n_ad]]
        pos += n_ad
        go_refs = refs[pos: pos + n_go]
        pos += n_go
        acc_refs = refs[pos:]

        def wrapped(*a):
            r = fn(*a)
            return tuple(r) if isinstance(r, (tuple, list)) else (r,)

        outs, vjp = jax.vjp(wrapped, *r_t, *b_t)
        it = iter(c_t)
        full = tuple(jnp.full(o.shape, c, F32) if isinstance(c, float) else next(it) for o, c in zip(outs, cts))
        grads = vjp(full)
        for o_ref, i in zip(go_refs, need_rows):
            g = grads[i]
            if i in adds:
                g = g + ad_t[add_keys.index(i)]
            o_ref[...] = g.astype(o_ref.dtype)

        @pl.when(pl.program_id(0) == 0)
        def _():
            for a in acc_refs:
                a[...] = jnp.zeros_like(a)

        for a, j in zip(acc_refs[:n_gb], need_bcs):
            _acc_add(a, grads[n_r + j])
        for a, j in zip(acc_refs[n_gb:], fwd_sums):
            a[0:1, :] += jnp.full((1, LANES), jnp.sum(outs[j]), F32)

    def row_spec(w):
        return pl.BlockSpec((tr, w), lambda i: (i, 0))

    in_specs = [row_spec(r.shape[1]) for r in rows]
    in_specs += [pl.BlockSpec((1, b.shape[1]), lambda i: (0, 0)) for b in bcs]
    in_specs += [row_spec(c.shape[1]) for c in ct_arrays] + [row_spec(a.shape[1]) for a in add_arrays]
    out_specs = [row_spec(rows[i].shape[1]) for i in need_rows]
    out_specs += [pl.BlockSpec((SUBLANES, bcs[j].shape[1]), lambda i: (0, 0)) for j in need_bcs]
    out_specs += [pl.BlockSpec((SUBLANES, LANES), lambda i: (0, 0)) for _ in fwd_sums]
    out_shape = [jax.ShapeDtypeStruct((n_rows, rows[i].shape[1]), d) for i, d in zip(need_rows, row_dtypes)]
    out_shape += [jax.ShapeDtypeStruct((SUBLANES, bcs[j].shape[1]), F32) for j in need_bcs]
    out_shape += [jax.ShapeDtypeStruct((SUBLANES, LANES), F32) for _ in fwd_sums]
    res = pl.pallas_call(
        body, name=name, grid=(n_rows // tr,), in_specs=in_specs, out_specs=out_specs, out_shape=out_shape,
        compiler_params=_params(("arbitrary",)),
    )(*rows, *bcs, *ct_arrays, *add_arrays)
    return tuple(res)


def _fold(acc):
    return jnp.sum(acc, axis=0, keepdims=True)


def _rms(x, w):
    return x * lax.rsqrt(jnp.mean(x * x, axis=-1, keepdims=True) + EPS) * w


def _sigmoid(x):
    return jax.nn.sigmoid(x)


def _silu(x):
    return x * _sigmoid(x)


def _log1p(u):
    series = u * (1.0 - u * (0.5 - u * (1.0 / 3.0 - u * 0.25)))
    return jnp.where(u < 0.01, series, jnp.log(1.0 + u))


def _softplus(x):
    return jnp.maximum(x, 0.0) + _log1p(jnp.exp(-jnp.abs(x)))


def _st_pre(x, w):
    return _rms(x, w)


def _st_lora_norms(dq):
    def fn(cqkv, qn, kvn):
        return _rms(cqkv[:, :dq], qn), _rms(cqkv[:, dq:], kvn)
    return fn


def _st_gated_norm(n_groups):
    def fn(y, xs, z, dskip, wn):
        yz = (y + dskip * xs) * _silu(z)
        gw = yz.shape[1] // n_groups
        parts = [_rms(yz[:, g * gw:(g + 1) * gw], wn[:, g * gw:(g + 1) * gw]) for g in range(n_groups)]
        return jnp.concatenate(parts, axis=1)
    return fn


def _st_mix(d):
    def fn(g, ao, so):
        return _sigmoid(g[:, :d]) * ao + _sigmoid(g[:, d:]) * so
    return fn


def _st_res_norm(h, y, w_post, w_pre):
    h2 = h + _rms(y, w_post)
    return h2, _rms(h2, w_pre)


def _st_swiglu(gt, up):
    return _silu(gt) * up


def _st_loss(pe, gl, h2, tgt, w_post):
    e = pe * _sigmoid(gl)
    diff = h2 + _rms(e, w_post) - tgt
    return 0.5 * jnp.mean(diff * diff, axis=-1, keepdims=True)


def _rope_tables(positions):
    half = QK_ROPE // 2
    inv_freq = ROPE_THETA ** (-jnp.arange(0, QK_ROPE, 2, dtype=F32) / QK_ROPE)
    ang = positions.astype(F32).reshape(-1, 1) * inv_freq
    cos, sin = jnp.cos(ang), jnp.sin(ang)
    n = ang.shape[0]
    z = lambda w: jnp.zeros((n, w), F32)
    c_tab = jnp.concatenate([jnp.ones((n, QK_NOPE), F32), cos, cos, z(QK_PAD - QK_NOPE - QK_ROPE)], axis=1)
    a_tab = jnp.concatenate([z(QK_NOPE), -sin, z(half), z(QK_PAD - QK_NOPE - QK_ROPE)], axis=1)
    b_tab = jnp.concatenate([z(QK_NOPE), z(half), sin, z(QK_PAD - QK_NOPE - QK_ROPE)], axis=1)
    return c_tab, a_tab, b_tab


def _rot(x, c, a, b):
    half = QK_ROPE // 2
    return x * c + pltpu.roll(x, QK_PAD - half, axis=1) * a + pltpu.roll(x, half, axis=1) * b


def _rot_t(g, c, a, b):
    half = QK_ROPE // 2
    return g * c + pltpu.roll(g * a, half, axis=1) + pltpu.roll(g * b, QK_PAD - half, axis=1)


def _st_rope(n_heads):
    def fn(qraw, kraw, sm, c, a, b):
        kpe = _rot(sm[:, :QK_PAD], c, a, b)
        scale = float(QK_NOPE + QK_ROPE) ** -0.5
        q = [_rot(qraw[:, h * QK_PAD:(h + 1) * QK_PAD], c, a, b) * scale for h in range(n_heads)]
        k = [kraw[:, h * QK_PAD:(h + 1) * QK_PAD] + kpe for h in range(n_heads)]
        return jnp.concatenate(q, axis=1), jnp.concatenate(k, axis=1)
    return fn


def _st_rope_bwd(n_heads):
    def fn(dq, dk, c, a, b):
        dqraw = [_rot_t(dq[:, h * QK_PAD:(h + 1) * QK_PAD], c, a, b) for h in range(n_heads)]
        dks = dk[:, :QK_PAD]
        for h in range(1, n_heads):
            dks = dks + dk[:, h * QK_PAD:(h + 1) * QK_PAD]
        return jnp.concatenate(dqraw, axis=1), _rot_t(dks, c, a, b), dk
    return fn


def _split3(x):
    h1 = x.astype(BF16)
    r1 = x - h1.astype(F32)
    h2 = r1.astype(BF16)
    h3 = (r1 - h2.astype(F32)).astype(BF16)
    return h1, h2, h3


def _tri_dot(tri, x):
    h1, h2, h3 = _split3(x)
    return (_dot(tri, h3) + _dot(tri, h2)) + _dot(tri, h1)


def _dot_tri(x, tri):
    h1, h2, h3 = _split3(x)
    return (_dot(h3, tri) + _dot(h2, tri)) + _dot(h1, tri)


def _st_dt(sm, bias, alog):
    x = sm[:, QK_PAD:] + bias
    dt = _softplus(x)
    return dt, dt * (-jnp.exp(alog))


def _st_dt_bwd(sm, ddt, dcum, bias, alog):
    n = sm.shape[0]
    i = lax.broadcasted_iota(jnp.int32, (n, n), 0)
    j = lax.broadcasted_iota(jnp.int32, (n, n), 1)
    upper = (j >= i).astype(BF16)
    dda = _tri_dot(upper, dcum)
    x = sm[:, QK_PAD:] + bias
    dt = _softplus(x)
    a = -jnp.exp(alog)
    draw = (ddt + dda * a) * _sigmoid(x)
    return draw, draw, dda * dt * a


def _conv_fwd(xbc, w, b):
    S, C = xbc.shape
    tr = _pick(S, (512, 256))
    tc = _pick(C, (1024, 512, 256, 128))
    hb = tr // SUBLANES

    def body(x_ref, halo_ref, w_ref, b_ref, c_ref, a_ref, ext):
        i = pl.program_id(1)
        halo = jnp.where(i == 0, 0.0, halo_ref[...])
        ext[0:SUBLANES, :] = halo
        ext[SUBLANES:, :] = x_ref[...]
        wv = w_ref[...]
        acc = b_ref[...] + wv[CONV_WIDTH - 1:CONV_WIDTH, :] * x_ref[...]
        for k in range(CONV_WIDTH - 1):
            off = SUBLANES - (CONV_WIDTH - 1) + k
            acc = acc + wv[k:k + 1, :] * ext[pl.ds(off, tr), :]
        c_ref[...] = acc
        a_ref[...] = _silu(acc)

    return pl.pallas_call(
        body, name="conv_fwd", grid=(C // tc, S // tr),
        in_specs=[pl.BlockSpec((tr, tc), lambda j, i: (i, j)),
                  pl.BlockSpec((SUBLANES, tc), lambda j, i: (jnp.maximum(i * hb - 1, 0), j)),
                  pl.BlockSpec((CONV_WIDTH, tc), lambda j, i: (0, j)),
                  pl.BlockSpec((1, tc), lambda j, i: (0, j))],
        out_specs=[pl.BlockSpec((tr, tc), lambda j, i: (i, j))] * 2,
        out_shape=[jax.ShapeDtypeStruct((S, C), F32)] * 2,
        scratch_shapes=[pltpu.VMEM((tr + SUBLANES, tc), F32)],
        compiler_params=_params(("parallel", "arbitrary")),
    )(xbc, xbc, w, b)


def _conv_bwd(xbc, dconv, w):
    S, C = xbc.shape
    tr = _pick(S, (512, 256))
    tc = _pick(C, (1024, 512, 256, 128))
    hb = tr // SUBLANES
    n_i = S // tr

    def body(x_ref, halo_ref, d_ref, dnext_ref, w_ref, dx_ref, dw_ref, ext, dext):
        i = pl.program_id(1)
        ext[0:SUBLANES, :] = jnp.where(i == 0, 0.0, halo_ref[...])
        ext[SUBLANES:, :] = x_ref[...]
        dext[0:tr, :] = d_ref[...]
        dext[tr:, :] = jnp.where(i == n_i - 1, 0.0, dnext_ref[...])
        wv = w_ref[...]
        d = d_ref[...]

        @pl.when(i == 0)
        def _():
            dw_ref[...] = jnp.zeros_like(dw_ref)

        dx = wv[CONV_WIDTH - 1:CONV_WIDTH, :] * d
        for k in range(CONV_WIDTH):
            if k < CONV_WIDTH - 1:
                dx = dx + wv[k:k + 1, :] * dext[pl.ds(CONV_WIDTH - 1 - k, tr), :]
                xs = ext[pl.ds(SUBLANES - (CONV_WIDTH - 1) + k, tr), :]
            else:
                xs = x_ref[...]
            prod = d * xs
            dw_ref[k * SUBLANES:(k + 1) * SUBLANES, :] += prod.reshape(tr // SUBLANES, SUBLANES, tc).sum(axis=0)
        dx_ref[...] = dx.astype(dx_ref.dtype)

    return pl.pallas_call(
        body, name="conv_bwd", grid=(C // tc, n_i),
        in_specs=[pl.BlockSpec((tr, tc), lambda j, i: (i, j)),
                  pl.BlockSpec((SUBLANES, tc), lambda j, i: (jnp.maximum(i * hb - 1, 0), j)),
                  pl.BlockSpec((tr, tc), lambda j, i: (i, j)),
                  pl.BlockSpec((SUBLANES, tc), lambda j, i: (jnp.minimum((i + 1) * hb, S // SUBLANES - 1), j)),
                  pl.BlockSpec((CONV_WIDTH, tc), lambda j, i: (0, j))],
        out_specs=[pl.BlockSpec((tr, tc), lambda j, i: (i, j)),
                   pl.BlockSpec((CONV_WIDTH * SUBLANES, tc), lambda j, i: (0, j))],
        out_shape=[jax.ShapeDtypeStruct((S, C), BF16), jax.ShapeDtypeStruct((CONV_WIDTH * SUBLANES, C), F32)],
        scratch_shapes=[pltpu.VMEM((tr + SUBLANES, tc), F32), pltpu.VMEM((tr + SUBLANES, tc), F32)],
        compiler_params=_params(("parallel", "arbitrary")),
    )(xbc, xbc, dconv, dconv, w)


def _st_dconv(d_inner):
    def fn(xc, dxa, dxb, db_, dc_):
        s = _sigmoid(xc)
        g = jnp.concatenate([dxa + dxb, db_, dc_], axis=1) * (s * (1.0 + xc * (1.0 - s)))
        return g, g
    return fn


def _chunk_setup(b_ref, c_ref, dac_ref, dar_ref, L):
    ii = lax.broadcasted_iota(jnp.int32, (L, L), 0)
    jj = lax.broadcasted_iota(jnp.int32, (L, L), 1)
    tri = ii >= jj
    cum_c = _tri_dot(tri.astype(BF16), dac_ref[0])
    cum_r = _dot_tri(dar_ref[0], (ii <= jj).astype(BF16))
    bm = b_ref[...].astype(BF16)
    cm = c_ref[...].astype(BF16)
    return tri, cum_c, cum_r, bm, cm, _dot_nt(cm, bm)


def _ssd_specs(d_inner, n_groups, gw, L, rp, chunk_of):
    bb0 = d_inner // D_STATE
    cb0 = bb0 + n_groups
    return [pl.BlockSpec((L, gw), lambda g, c: (chunk_of(c), g)),
            pl.BlockSpec((L, D_STATE), lambda g, c: (chunk_of(c), bb0 + g)),
            pl.BlockSpec((L, D_STATE), lambda g, c: (chunk_of(c), cb0 + g)),
            pl.BlockSpec((1, L, LANES), lambda g, c: (g, chunk_of(c), 0)),
            pl.BlockSpec((1, L, LANES), lambda g, c: (g, chunk_of(c), 0)),
            pl.BlockSpec((1, rp, L), lambda g, c: (g, 0, chunk_of(c)))]


def _ssd_fwd(xbc_a, dt_col, da_col, da_row, d_inner, n_groups, R):
    S = xbc_a.shape[0]
    L = min(CHUNK, S)
    NC = S // L
    P, N = SSM_HEADDIM, D_STATE
    gw = R * P
    rp = da_row.shape[1]

    def body(x_ref, b_ref, c_ref, dt_ref, dac_ref, dar_ref, y_ref, st_ref, state):
        @pl.when(pl.program_id(1) == 0)
        def _():
            state[...] = jnp.zeros_like(state)

        st_ref[0, 0] = state[...]
        tri, cum_c, cum_r, bm, cm, gm = _chunk_setup(b_ref, c_ref, dac_ref, dar_ref, L)
        dt = dt_ref[0]
        for r in range(R):
            cc = cum_c[:, r:r + 1]
            cr = cum_r[r:r + 1, :]
            lam = jnp.exp(jnp.where(tri, cc - cr, -jnp.inf))
            m = (gm * lam).astype(BF16)
            x = x_ref[:, r * P:(r + 1) * P] * dt[:, r:r + 1]
            s_r = state[r * P:(r + 1) * P, :]
            y_off = _dot_nt(cm, s_r.astype(BF16)) * jnp.exp(cc)
            y_ref[:, r * P:(r + 1) * P] = _dot(m, x.astype(BF16)) + y_off
            last = cr[:, L - 1:L]
            xw = (x * jnp.exp(last - cc)).astype(BF16)
            state[r * P:(r + 1) * P, :] = s_r * jnp.exp(last) + _dot_tn(xw, bm)

    return pl.pallas_call(
        body, name="ssd_fwd", grid=(n_groups, NC),
        in_specs=_ssd_specs(d_inner, n_groups, gw, L, rp, lambda c: c),
        out_specs=[pl.BlockSpec((L, gw), lambda g, c: (c, g)),
                   pl.BlockSpec((1, 1, gw, N), lambda g, c: (g, c, 0, 0))],
        out_shape=[jax.ShapeDtypeStruct((S, d_inner), F32), jax.ShapeDtypeStruct((n_groups, NC, gw, N), F32)],
        scratch_shapes=[pltpu.VMEM((gw, N), F32)],
        compiler_params=_params(("parallel", "arbitrary")),
    )(xbc_a, xbc_a, xbc_a, dt_col, da_col, da_row)


def _ssd_bwd(xbc_a, dt_col, da_col, da_row, states, dy, d_inner, n_groups, R):
    S = xbc_a.shape[0]
    L = min(CHUNK, S)
    NC = S // L
    P, N = SSM_HEADDIM, D_STATE
    gw = R * P
    rp = da_row.shape[1]
    rev = lambda c: NC - 1 - c

    def body(x_ref, b_ref, c_ref, dt_ref, dac_ref, dar_ref, st_ref, dy_ref,
             dx_ref, db_ref, dc_ref, ddt_ref, dcc_ref, dcr_ref, dstate):
        @pl.when(pl.program_id(1) == 0)
        def _():
            dstate[...] = jnp.zeros_like(dstate)

        tri, cum_c, cum_r, bm, cm, gm = _chunk_setup(b_ref, c_ref, dac_ref, dar_ref, L)
        dt = dt_ref[0]
        lane = lax.broadcasted_iota(jnp.int32, (L, LANES), 1)
        sub = lax.broadcasted_iota(jnp.int32, (rp, L), 0)
        is_last = lax.broadcasted_iota(jnp.int32, (L, 1), 0) == L - 1
        d_g = jnp.zeros((L, L), F32)
        dc_acc = jnp.zeros((L, N), F32)
        db_acc = jnp.zeros((L, N), F32)
        ddt_out = jnp.zeros((L, LANES), F32)
        dcc_out = jnp.zeros((L, LANES), F32)
        dcr_out = jnp.zeros((rp, L), F32)
        for r in range(R):
            cc = cum_c[:, r:r + 1]
            cr = cum_r[r:r + 1, :]
            lam = jnp.exp(jnp.where(tri, cc - cr, -jnp.inf))
            m = gm * lam
            dtc = dt[:, r:r + 1]
            xh = x_ref[:, r * P:(r + 1) * P]
            x = xh * dtc
            xb = x.astype(BF16)
            d_y = dy_ref[:, r * P:(r + 1) * P]
            d_yb = d_y.astype(BF16)
            s_r = st_ref[0, 0, r * P:(r + 1) * P, :]
            s_rb = s_r.astype(BF16)
            ds_n = dstate[r * P:(r + 1) * P, :]
            ds_nb = ds_n.astype(BF16)
            e = jnp.exp(cc)
            last = cr[:, L - 1:L]
            e_last = jnp.exp(last)
            w = jnp.exp(last - cc)
            d_m = _dot_nt(d_yb, xb)
            d_x = _dot_tn(m.astype(BF16), d_yb)
            d_ye = (d_y * e).astype(BF16)
            dc_acc = dc_acc + _dot(d_ye, s_rb)
            ds_part = _dot_tn(d_ye, cm)
            y_off = _dot_nt(cm, s_rb) * e
            dcum = jnp.sum(d_y * y_off, axis=1, keepdims=True)
            d_xw = _dot_nt(bm, ds_nb)
            d_x = d_x + d_xw * w
            dw_w = jnp.sum(d_xw * x, axis=1, keepdims=True) * w
            db_acc = db_acc + _dot((x * w).astype(BF16), ds_nb)
            d_last = jnp.sum(ds_n * s_r, keepdims=True) * e_last + jnp.sum(dw_w, keepdims=True)
            dcum = dcum - dw_w
            dstate[r * P:(r + 1) * P, :] = e_last * ds_n + ds_part
            d_g = d_g + d_m * lam
            q = d_m * m
            dcum = dcum + jnp.sum(q, axis=1, keepdims=True) + jnp.where(is_last, d_last, 0.0)
            dcum_row = -jnp.sum(q, axis=0, keepdims=True)
            dx_ref[:, r * P:(r + 1) * P] = d_x * dtc
            ddt = jnp.sum(d_x * xh, axis=1, keepdims=True)
            ddt_out = ddt_out + jnp.where(lane == r, ddt, 0.0)
            dcc_out = dcc_out + jnp.where(lane == r, dcum, 0.0)
            dcr_out = dcr_out + jnp.where(sub == r, dcum_row, 0.0)
        d_gb = d_g.astype(BF16)
        dc_ref[...] = dc_acc + _dot(d_gb, bm)
        db_ref[...] = db_acc + _dot_tn(d_gb, cm)
        ddt_ref[0] = ddt_out
        dcc_ref[0] = dcc_out
        dcr_ref[0] = dcr_out

    gn = n_groups * N
    return pl.pallas_call(
        body, name="ssd_bwd", grid=(n_groups, NC),
        in_specs=_ssd_specs(d_inner, n_groups, gw, L, rp, rev) + [
            pl.BlockSpec((1, 1, gw, N), lambda g, c: (g, rev(c), 0, 0)),
            pl.BlockSpec((L, gw), lambda g, c: (rev(c), g))],
        out_specs=[pl.BlockSpec((L, gw), lambda g, c: (rev(c), g)),
                   pl.BlockSpec((L, N), lambda g, c: (rev(c), g)),
                   pl.BlockSpec((L, N), lambda g, c: (rev(c), g)),
                   pl.BlockSpec((1, L, LANES), lambda g, c: (g, rev(c), 0)),
                   pl.BlockSpec((1, L, LANES), lambda g, c: (g, rev(c), 0)),
                   pl.BlockSpec((1, rp, L), lambda g, c: (g, 0, rev(c)))],
        out_shape=[jax.ShapeDtypeStruct((S, d_inner), F32), jax.ShapeDtypeStruct((S, gn), F32),
                   jax.ShapeDtypeStruct((S, gn), F32), jax.ShapeDtypeStruct((n_groups, S, LANES), F32),
                   jax.ShapeDtypeStruct((n_groups, S, LANES), F32), jax.ShapeDtypeStruct((n_groups, rp, S), F32)],
        scratch_shapes=[pltpu.VMEM((gw, N), F32)],
        compiler_params=_params(("parallel", "arbitrary")),
    )(xbc_a, xbc_a, xbc_a, dt_col, da_col, da_row, states, dy)


def _attn_scale():
    return float(QK_NOPE + QK_ROPE) ** -0.5


def _diag_mask(t, keys_first=False):
    rows = lax.broadcasted_iota(jnp.int32, (t, t), 0)
    cols = lax.broadcasted_iota(jnp.int32, (t, t), 1)
    return rows <= cols if keys_first else cols <= rows


def _tile_rows(ref, j, t):
    return ref[pl.ds(pl.multiple_of(j * t, t), t), :]


def _walk_pairs(lo, hi, body):
    def pair(t, carry):
        body(lo + 2 * t, carry)
        body(lo + 2 * t + 1, carry)
        return carry

    lax.fori_loop(0, (hi - lo) // 2, pair, 0)

    @pl.when((hi - lo) % 2 == 1)
    def _():
        body(hi - 1, 0)


def _flash_fwd(q, k, v, n_heads):
    S = q.shape[0]
    T = min(FLASH_T, S)

    def body(q_ref, k_ref, v_ref, o_ref, lse_ref, m_s, l_s, acc):
        i = pl.program_id(1)
        m_s[...] = jnp.full_like(m_s, -jnp.inf)
        l_s[...] = jnp.zeros_like(l_s)
        acc[...] = jnp.zeros_like(acc)
        qv = q_ref[...]

        def step(j, masked):
            s = _dot_nt(qv, _tile_rows(k_ref, j, T))
            if masked:
                s = jnp.where(_diag_mask(T), s, -jnp.inf)
            m_prev = m_s[...]
            m_new = jnp.maximum(m_prev, jnp.max(s, axis=1, keepdims=True))
            alpha = jnp.exp(m_prev - m_new)
            p = jnp.exp(s - m_new[:, :1])
            l_s[...] = alpha * l_s[...] + jnp.sum(p, axis=1, keepdims=True)
            acc[...] = alpha * acc[...] + _dot(p.astype(BF16), _tile_rows(v_ref, j, T))
            m_s[...] = m_new

        def loop_body(j, carry):
            step(j, False)
            return carry

        _walk_pairs(0, i, loop_body)
        step(i, True)
        o_ref[...] = acc[...] / l_s[...]
        lse_ref[0] = (m_s[...] + jnp.log(l_s[...]))[:, :1]

    return pl.pallas_call(
        body, name="flash_fwd", grid=(n_heads, S // T),
        in_specs=[pl.BlockSpec((T, QK_PAD), lambda h, i: (i, h)),
                  pl.BlockSpec((S, QK_PAD), lambda h, i: (0, h)),
                  pl.BlockSpec((S, V_DIM), lambda h, i: (0, h))],
        out_specs=[pl.BlockSpec((T, V_DIM), lambda h, i: (i, h)),
                   pl.BlockSpec((1, T, 1), lambda h, i: (h, i, 0))],
        out_shape=[jax.ShapeDtypeStruct((S, n_heads * V_DIM), F32), jax.ShapeDtypeStruct((n_heads, S, 1), F32)],
        scratch_shapes=[pltpu.VMEM((T, V_DIM), F32), pltpu.VMEM((T, V_DIM), F32), pltpu.VMEM((T, V_DIM), F32)],
        compiler_params=_params(("parallel", "arbitrary")),
    )(q, k, v)


def _flash_bwd_dq(q, k, v, do, lse, delta, n_heads):
    S = q.shape[0]
    T = min(FLASH_T, S)

    def body(q_ref, k_ref, v_ref, do_ref, lse_ref, dl_ref, dq_ref, acc):
        i = pl.program_id(1)
        acc[...] = jnp.zeros_like(acc)
        qv = q_ref[...]
        dov = do_ref[...]
        lse_c = lse_ref[0]
        dl_c = dl_ref[0]

        def step(j, masked):
            kt = _tile_rows(k_ref, j, T)
            s = _dot_nt(qv, kt)
            if masked:
                s = jnp.where(_diag_mask(T), s, -jnp.inf)
            p = jnp.exp(s - lse_c)
            ds = p * (_dot_nt(dov, _tile_rows(v_ref, j, T)) - dl_c)
            acc[...] += _dot(ds.astype(BF16), kt)

        def loop_body(j, carry):
            step(j, False)
            return carry

        _walk_pairs(0, i, loop_body)
        step(i, True)
        dq_ref[...] = acc[...] * _attn_scale()

    return pl.pallas_call(
        body, name="flash_bwd_dq", grid=(n_heads, S // T),
        in_specs=[pl.BlockSpec((T, QK_PAD), lambda h, i: (i, h)),
                  pl.BlockSpec((S, QK_PAD), lambda h, i: (0, h)),
                  pl.BlockSpec((S, V_DIM), lambda h, i: (0, h)),
                  pl.BlockSpec((T, V_DIM), lambda h, i: (i, h)),
                  pl.BlockSpec((1, T, 1), lambda h, i: (h, i, 0)),
                  pl.BlockSpec((1, T, 1), lambda h, i: (h, i, 0))],
        out_specs=pl.BlockSpec((T, QK_PAD), lambda h, i: (i, h)),
        out_shape=jax.ShapeDtypeStruct((S, n_heads * QK_PAD), F32),
        scratch_shapes=[pltpu.VMEM((T, QK_PAD), F32)],
        compiler_params=_params(("parallel", "arbitrary")),
    )(q, k, v, do, lse, delta)


def _flash_bwd_dkv(q, k, v, do, lse_row, delta_row, n_heads):
    S = q.shape[0]
    T = min(FLASH_T, S)
    nq = S // T

    def body(q_ref, k_ref, v_ref, do_ref, lse_ref, dl_ref, dk_ref, dv_ref, dk_acc, dv_acc):
        j = pl.program_id(1)
        dk_acc[...] = jnp.zeros_like(dk_acc)
        dv_acc[...] = jnp.zeros_like(dv_acc)
        kv = k_ref[...]
        vv = v_ref[...]

        def step(i, masked):
            qt = _tile_rows(q_ref, i, T)
            dot = _tile_rows(do_ref, i, T)
            cols = pl.ds(pl.multiple_of(i * T, T), T)
            s_t = _dot_nt(kv, qt)
            if masked:
                s_t = jnp.where(_diag_mask(T, keys_first=True), s_t, -jnp.inf)
            p_t = jnp.exp(s_t - lse_ref[0, :, cols])
            dv_acc[...] += _dot(p_t.astype(BF16), dot)
            ds_t = p_t * (_dot_nt(vv, dot) - dl_ref[0, :, cols])
            dk_acc[...] += _dot(ds_t.astype(BF16), qt)

        def loop_body(i, carry):
            step(i, False)
            return carry

        step(j, True)
        _walk_pairs(j + 1, nq, loop_body)
        dk_ref[...] = dk_acc[...]
        dv_ref[...] = dv_acc[...].astype(dv_ref.dtype)

    return pl.pallas_call(
        body, name="flash_bwd_dkv", grid=(n_heads, S // T),
        in_specs=[pl.BlockSpec((S, QK_PAD), lambda h, j: (0, h)),
                  pl.BlockSpec((T, QK_PAD), lambda h, j: (j, h)),
                  pl.BlockSpec((T, V_DIM), lambda h, j: (j, h)),
                  pl.BlockSpec((S, V_DIM), lambda h, j: (0, h)),
                  pl.BlockSpec((1, 1, S), lambda h, j: (h, 0, 0)),
                  pl.BlockSpec((1, 1, S), lambda h, j: (h, 0, 0))],
        out_specs=[pl.BlockSpec((T, QK_PAD), lambda h, j: (j, h)),
                   pl.BlockSpec((T, V_DIM), lambda h, j: (j, h))],
        out_shape=[jax.ShapeDtypeStruct((S, n_heads * QK_PAD), F32), jax.ShapeDtypeStruct((S, n_heads * V_DIM), BF16)],
        scratch_shapes=[pltpu.VMEM((T, QK_PAD), F32), pltpu.VMEM((T, V_DIM), F32)],
        compiler_params=_params(("parallel", "arbitrary")),
    )(q, k, v, do, lse_row, delta_row)


def _st_delta(n_heads):
    def fn(do, o):
        prod = do * o
        lane = lax.broadcasted_iota(jnp.int32, (do.shape[0], LANES), 1)
        out = jnp.zeros((do.shape[0], LANES), F32)
        for h in range(n_heads):
            out = out + jnp.where(lane == h, jnp.sum(prod[:, h * V_DIM:(h + 1) * V_DIM], axis=1, keepdims=True), 0.0)
        return out
    return fn


def _pad_cols(w, width):
    return jnp.pad(w, ((0, 0), (0, width - w.shape[1])))


def _dims(x, p, q_norm, kv_norm, w_uq, dt_bias, ssm_norm, conv_b, w_gate):
    d = dict(S=x.shape[0], D=x.shape[1], PLE=p.shape[1], DQ=q_norm.shape[1], DKV=kv_norm.shape[1],
             NH=w_uq.shape[1] // (QK_NOPE + QK_ROPE), NHS=dt_bias.shape[1], DI=ssm_norm.shape[1],
             CONV=conv_b.shape[1], FF=w_gate.shape[1])
    d["G"] = (d["CONV"] - d["DI"]) // (2 * D_STATE)
    d["R"] = d["NHS"] // d["G"]
    return d


def _local_step(x, p, positions, target, small, conv_w, wts):
    dm = _dims(x, p, small["q_norm"], small["kv_norm"], wts["w_uq"], small["dt_bias"], small["ssm_norm"],
               small["conv_b"], wts["w_gate"])
    S, D, DQ, DKV, NH, NHS, DI, CONV, G, R = (dm[k] for k in ("S", "D", "DQ", "DKV", "NH", "NHS", "DI", "CONV", "G", "R"))
    rp = -(-R // SUBLANES) * SUBLANES
    L = min(CHUNK, S)

    w_in = wts["w_in"]
    o = [0]
    for n in (DQ, DKV, QK_ROPE, DI, CONV, NHS, D, D):
        o.append(o[-1] + n)
    w_cqkv, w_kr, w_z, w_xbc, w_dt, w_g = (w_in[:, o[0]:o[2]], w_in[:, o[2]:o[3]], w_in[:, o[3]:o[4]],
                                           w_in[:, o[4]:o[5]], w_in[:, o[5]:o[6]], w_in[:, o[6]:o[8]])
    zc = lambda n: jnp.zeros((D, n), BF16)
    w_sm = jnp.concatenate([zc(QK_NOPE), w_kr, zc(QK_PAD - QK_NOPE - QK_ROPE), w_dt, zc(LANES - NHS)], axis=1)
    w_q = jnp.pad(wts["w_uq"].reshape(DQ, NH, QK_NOPE + QK_ROPE),
                  ((0, 0), (0, 0), (0, QK_PAD - QK_NOPE - QK_ROPE))).reshape(DQ, NH * QK_PAD)
    ukv = wts["w_ukv"].reshape(DKV, NH, QK_NOPE + V_DIM)
    w_k = jnp.pad(ukv[:, :, :QK_NOPE], ((0, 0), (0, 0), (0, QK_PAD - QK_NOPE))).reshape(DKV, NH * QK_PAD)
    w_v = ukv[:, :, QK_NOPE:].reshape(DKV, NH * V_DIM)
    dt_bias_p, a_log_p = _pad_cols(small["dt_bias"], LANES), _pad_cols(small["a_log"], LANES)
    dskip_rep = jnp.repeat(small["d_skip"], SSM_HEADDIM, axis=1)
    c_tab, a_tab, b_tab = _rope_tables(positions)

    (u,) = _rowwise("pre_norm", _st_pre, [x], [small["mix_norm_pre"]], [(D, BF16)])
    cqkv = _mm("in_cqkv", u, w_cqkv, "nn")
    z = _mm("in_z", u, w_z, "nn")
    xbc = _mm("in_xbc", u, w_xbc, "nn")
    g = _mm("in_gates", u, w_g, "nn")
    sm = _mm("in_small", u, w_sm, "nn")

    lora_fn = _st_lora_norms(DQ)
    cq_n, ckv_n = _rowwise("lora_norms", lora_fn, [cqkv], [small["q_norm"], small["kv_norm"]], [(DQ, BF16), (DKV, BF16)])
    qraw = _mm("up_q", cq_n, w_q, "nn")
    kraw = _mm("up_k", ckv_n, w_k, "nn")
    v = _mm("up_v", ckv_n, w_v, "nn", out_dtype=BF16)
    q, k = _rowwise("rope", _st_rope(NH), [qraw, kraw, sm, c_tab, a_tab, b_tab], [],
                    [(NH * QK_PAD, BF16), (NH * QK_PAD, BF16)])
    attn, lse = _flash_fwd(q, k, v, NH)

    xbc_c, xbc_a = _conv_fwd(xbc, conv_w, small["conv_b"])
    dt, da = _rowwise("dt", _st_dt, [sm], [dt_bias_p, a_log_p], [(LANES, F32), (LANES, F32)])

    def col_layout(t):
        return _pad_cols(t[:, :NHS].reshape(S, G, R).transpose(1, 0, 2).reshape(G * S, R), LANES).reshape(G, S, LANES)

    def row_layout(t):
        return jnp.pad(t[:, :NHS].reshape(S, G, R).transpose(1, 2, 0), ((0, 0), (0, rp - R), (0, 0)))

    dt_col, da_col, da_row = col_layout(dt), col_layout(da), row_layout(da)
    y, states = _ssd_fwd(xbc_a, dt_col, da_col, da_row, DI, G, R)
    xs = xbc_a[:, :DI]
    gn_fn = _st_gated_norm(G)
    (ssm,) = _rowwise("gated_norm", gn_fn, [y, xs, z], [dskip_rep, small["ssm_norm"]], [(DI, BF16)])

    ao = _mm("attn_o", attn, wts["w_attn_o"], "nn")
    so = _mm("ssm_o", ssm, wts["w_ssm_o"], "nn")
    mix_fn = _st_mix(D)
    (mixed,) = _rowwise("mix", mix_fn, [g, ao, so], [], [(D, BF16)])
    mo = _mm("out_proj", mixed, wts["w_out"], "nn")
    h1, f = _rowwise("res1", _st_res_norm, [x, mo], [small["mix_norm_post"], small["ffn_norm_pre"]], [(D, F32), (D, BF16)])
    gt = _mm("ffn_gate", f, wts["w_gate"], "nn")
    up = _mm("ffn_up", f, wts["w_up"], "nn")
    (act,) = _rowwise("swiglu", _st_swiglu, [gt, up], [], [(gt.shape[1], BF16)])
    dn = _mm("ffn_down", act, wts["w_down"], "nn")
    h2, a3 = _rowwise("res2", _st_res_norm, [h1, dn], [small["ffn_norm_post"], small["ple_norm_pre"]], [(D, F32), (D, BF16)])
    gl = _mm("ple_gate", a3, wts["w_ple_gate"], "nn")
    pe = _mm("ple_proj", p, wts["w_ple"], "nn")

    sg = {}
    bg = {}
    dpe, dgl, dh2, d_w, loss_acc = _rowwise_bwd(
        "loss", _st_loss, [pe, gl, h2, target], [small["ple_norm_post"]], [1.0], [0, 1, 2], [BF16, BF16, F32], fwd_sums=(0,))
    sg["ple_norm_post"] = _fold(d_w)
    loss = loss_acc[0, 0]
    bg["w_ple"] = _mm("d_w_ple", p, dpe, "tn", out_dtype=BF16)
    bg["w_ple_gate"] = _mm("d_w_ple_gate", a3, dgl, "tn", out_dtype=BF16)
    da3 = _mm("d_a3", dgl, wts["w_ple_gate"], "nt")

    dh1, ddn, d_post, d_pre = _rowwise_bwd(
        "res2_bwd", _st_res_norm, [h1, dn], [small["ffn_norm_post"], small["ple_norm_pre"]], [dh2, da3], [0, 1], [F32, BF16])
    sg["ffn_norm_post"], sg["ple_norm_pre"] = _fold(d_post), _fold(d_pre)
    bg["w_down"] = _mm("d_w_down", act, ddn, "tn", out_dtype=BF16)
    dact = _mm("d_act", ddn, wts["w_down"], "nt")
    dgt, dup = _rowwise_bwd("swiglu_bwd", _st_swiglu, [gt, up], [], [dact], [0, 1], [BF16, BF16])
    bg["w_gate"] = _mm("d_w_gate", f, dgt, "tn", out_dtype=BF16)
    bg["w_up"] = _mm("d_w_up", f, dup, "tn", out_dtype=BF16)
    df = _mm("d_f_gate", dgt, wts["w_gate"], "nt")
    df = _mm("d_f_up", dup, wts["w_up"], "nt", acc_in=df)

    dx_res, dmo, d_post, d_pre = _rowwise_bwd(
        "res1_bwd", _st_res_norm, [x, mo], [small["mix_norm_post"], small["ffn_norm_pre"]], [dh1, df], [0, 1], [F32, BF16])
    sg["mix_norm_post"], sg["ffn_norm_pre"] = _fold(d_post), _fold(d_pre)
    bg["w_out"] = _mm("d_w_out", mixed, dmo, "tn", out_dtype=BF16)
    dmixed = _mm("d_mixed", dmo, wts["w_out"], "nt")
    dg, dao, dso = _rowwise_bwd("mix_bwd", mix_fn, [g, ao, so], [], [dmixed], [0, 1, 2], [BF16, BF16, BF16])
    bg["w_attn_o"] = _mm("d_w_attn_o", attn, dao, "tn", out_dtype=BF16)
    bg["w_ssm_o"] = _mm("d_w_ssm_o", ssm, dso, "tn", out_dtype=BF16)
    dattn = _mm("d_attn", dao, wts["w_attn_o"], "nt", out_dtype=BF16)
    dssm = _mm("d_ssm", dso, wts["w_ssm_o"], "nt")

    dy, dxs_a, dz, d_dskip, d_ssmn = _rowwise_bwd(
        "gated_norm_bwd", gn_fn, [y, xs, z], [dskip_rep, small["ssm_norm"]], [dssm], [0, 1, 2], [F32, F32, BF16])
    sg["d_skip"] = _fold(d_dskip).reshape(NHS, SSM_HEADDIM).sum(axis=1).reshape(1, NHS)
    sg["ssm_norm"] = _fold(d_ssmn)
    dxs_b, d_b, d_c, ddt_col, dcum_col, dcum_row = _ssd_bwd(xbc_a, dt_col, da_col, da_row, states, dy, DI, G, R)

    def from_col(t):
        return _pad_cols(t[:, :, :R].transpose(1, 0, 2).reshape(S, NHS), LANES)

    dcum = from_col(dcum_col) + _pad_cols(dcum_row[:, :R, :].transpose(2, 0, 1).reshape(S, NHS), LANES)
    ddtraw, d_bias, d_alog = _rowwise("dt_bwd", _st_dt_bwd, [sm, from_col(ddt_col), dcum], [dt_bias_p, a_log_p],
                                      [(LANES, F32)], accs=(LANES, LANES), tr=L)
    sg["dt_bias"], sg["a_log"] = _fold(d_bias)[:, :NHS], _fold(d_alog)[:, :NHS]
    dconv, d_cb = _rowwise("dconv", _st_dconv(DI), [xbc_c, dxs_a, dxs_b, d_b, d_c], [], [(CONV, F32)], accs=(CONV,))
    sg["conv_b"] = _fold(d_cb)
    dxbc, d_cw = _conv_bwd(xbc, dconv, conv_w)
    d_conv_w = d_cw.reshape(CONV_WIDTH, SUBLANES, CONV).sum(axis=1)

    (delta,) = _rowwise("attn_delta", _st_delta(NH), [dattn, attn], [], [(LANES, F32)])
    delta = delta[:, :NH].T
    dq = _flash_bwd_dq(q, k, v, dattn, lse, delta.reshape(NH, S, 1), NH)
    dk, dv = _flash_bwd_dkv(q, k, v, dattn, lse.reshape(NH, 1, S), delta.reshape(NH, 1, S), NH)
    dqraw, dkr, dk = _rowwise("rope_bwd", _st_rope_bwd(NH), [dq, dk, c_tab, a_tab, b_tab], [],
                              [(NH * QK_PAD, BF16), (QK_PAD, F32), (NH * QK_PAD, BF16)])
    d_w_q = _mm("d_w_q", cq_n, dqraw, "tn", out_dtype=BF16)
    d_w_k = _mm("d_w_k", ckv_n, dk, "tn", out_dtype=BF16)
    d_w_v = _mm("d_w_v", ckv_n, dv, "tn", out_dtype=BF16)
    dcq_n = _mm("d_cq_n", dqraw, w_q, "nt")
    dckv_n = _mm("d_ckv_n_k", dk, w_k, "nt")
    dckv_n = _mm("d_ckv_n_v", dv, w_v, "nt", acc_in=dckv_n)
    bg["w_uq"] = d_w_q.reshape(DQ, NH, QK_PAD)[:, :, :QK_NOPE + QK_ROPE].reshape(DQ, NH * (QK_NOPE + QK_ROPE))
    bg["w_ukv"] = jnp.concatenate([d_w_k.reshape(DKV, NH, QK_PAD)[:, :, :QK_NOPE], d_w_v.reshape(DKV, NH, V_DIM)],
                                  axis=2).reshape(DKV, NH * (QK_NOPE + V_DIM))
    dcqkv, d_qn, d_kvn = _rowwise_bwd("lora_norms_bwd", lora_fn, [cqkv], [small["q_norm"], small["kv_norm"]],
                                      [dcq_n, dckv_n], [0], [BF16])
    sg["q_norm"], sg["kv_norm"] = _fold(d_qn), _fold(d_kvn)

    dsm = jnp.concatenate([dkr, ddtraw], axis=1)
    d_w_cqkv = _mm("d_w_cqkv", u, dcqkv, "tn", out_dtype=BF16)
    d_w_z = _mm("d_w_z", u, dz, "tn", out_dtype=BF16)
    d_w_xbc = _mm("d_w_xbc", u, dxbc, "tn", out_dtype=BF16)
    d_w_g = _mm("d_w_gates", u, dg, "tn", out_dtype=BF16)
    d_w_sm = _mm("d_w_small", u, dsm, "tn", out_dtype=BF16)
    bg["w_in"] = jnp.concatenate(
        [d_w_cqkv, d_w_sm[:, QK_NOPE:QK_NOPE + QK_ROPE], d_w_z, d_w_xbc, d_w_sm[:, QK_PAD:QK_PAD + NHS], d_w_g], axis=1)
    du = _mm("d_u_cqkv", dcqkv, w_cqkv, "nt")
    du = _mm("d_u_z", dz, w_z, "nt", acc_in=du)
    du = _mm("d_u_xbc", dxbc, w_xbc, "nt", acc_in=du)
    du = _mm("d_u_gates", dg, w_g, "nt", acc_in=du)
    du = _mm("d_u_small", dsm, w_sm, "nt", acc_in=du)
    grad_x, d_pre = _rowwise_bwd("pre_norm_bwd", _st_pre, [x], [small["mix_norm_pre"]], [du], [0], [F32], adds={0: dx_res})
    sg["mix_norm_pre"] = _fold(d_pre)
    return loss, grad_x, sg, d_conv_w, bg


_HBM = pl.BlockSpec(memory_space=pltpu.HBM)


def _place():
    return lax.axis_index("x"), lax.axis_index("y"), lax.axis_index("c")


def _all_gather(name, blocks):
    nw = len(blocks)

    def body(*refs):
        x_refs, out_refs = refs[:nw], refs[nw:2 * nw]
        send_sems, recv_sems, local_sems = refs[2 * nw:]
        x, y, c = _place()
        me, sibling = (x, y, c), (x, y, 1 - c)
        chips = [(1 - x, y), (x, 1 - y), (1 - x, 1 - y)]

        def slot(w, px, py, pc):
            return out_refs[w].at[4 * px + 2 * py + pc]

        def copy(w, k, blk, to, src=None):
            return pltpu.make_async_remote_copy(
                src_ref=slot(w, *blk) if src is None else src, dst_ref=slot(w, *blk),
                send_sem=send_sems.at[7 * w + k], recv_sem=recv_sems.at[7 * w + k], device_id=to, device_id_type=MESH)

        mine = [pltpu.make_async_copy(x_refs[w], slot(w, *me), local_sems.at[w]) for w in range(nw)]
        for cp in mine:
            cp.start()
        first = []
        for w in range(nw):
            first.append(copy(w, 0, me, sibling, src=x_refs[w]))
            first += [copy(w, 1 + j, me, (*chip, c), src=x_refs[w]) for j, chip in enumerate(chips)]
        for cp in first:
            cp.start()
        passed = []
        for j, chip in enumerate(chips):
            for w in range(nw):
                copy(w, 1 + j, (*chip, c), me).wait_recv()
                passed.append(copy(w, 4 + j, (*chip, c), sibling))
                passed[-1].start()
        for w in range(nw):
            copy(w, 0, sibling, me).wait_recv()
        for j, chip in enumerate(chips):
            for w in range(nw):
                copy(w, 4 + j, (*chip, 1 - c), me).wait_recv()
        for cp in first + passed:
            cp.wait_send()
        for cp in mine:
            cp.wait()

    return pl.pallas_call(
        body, name=name, out_shape=[jax.ShapeDtypeStruct((N_DEV,) + b.shape, b.dtype) for b in blocks],
        in_specs=[_HBM] * nw, out_specs=[_HBM] * nw,
        scratch_shapes=[pltpu.SemaphoreType.DMA((7 * nw,)), pltpu.SemaphoreType.DMA((7 * nw,)),
                        pltpu.SemaphoreType.DMA((nw,))],
    )(*blocks)


def _swap_with_sibling(name, by_dev):
    nw = len(by_dev)

    def body(*refs):
        g_refs, got_refs = refs[:nw], refs[nw:2 * nw]
        send_sems, recv_sems = refs[2 * nw:]
        x, y, c = _place()
        cps = [pltpu.make_async_remote_copy(src_ref=g_refs[w].at[2 * q + (1 - c)], dst_ref=got_refs[w].at[q],
                                            send_sem=send_sems.at[4 * w + q], recv_sem=recv_sems.at[4 * w + q],
                                            device_id=(x, y, 1 - c), device_id_type=MESH)
               for w in range(nw) for q in range(4)]
        for cp in cps:
            cp.start()
        for cp in cps:
            cp.wait()

    return pl.pallas_call(
        body, name=name, out_shape=[jax.ShapeDtypeStruct((4,) + b.shape[1:], b.dtype) for b in by_dev],
        in_specs=[_HBM] * nw, out_specs=[_HBM] * nw,
        scratch_shapes=[pltpu.SemaphoreType.DMA((4 * nw,)), pltpu.SemaphoreType.DMA((4 * nw,))],
    )(*by_dev)


def _swap_with_chips(name, sends):
    nw = len(sends)

    def body(*refs):
        s_refs, r_refs = refs[:nw], refs[nw:2 * nw]
        send_sems, recv_sems = refs[2 * nw:]
        x, y, c = _place()
        chips = [(1 - x, y), (x, 1 - y), (1 - x, 1 - y)]
        cps = [pltpu.make_async_remote_copy(src_ref=s_refs[w].at[2 * px + py], dst_ref=r_refs[w].at[k],
                                            send_sem=send_sems.at[3 * w + k], recv_sem=recv_sems.at[3 * w + k],
                                            device_id=(px, py, c), device_id_type=MESH)
               for w in range(nw) for k, (px, py) in enumerate(chips)]
        for cp in cps:
            cp.start()
        for cp in cps:
            cp.wait()

    return pl.pallas_call(
        body, name=name, out_shape=[jax.ShapeDtypeStruct((3,) + s.shape[1:], s.dtype) for s in sends],
        in_specs=[_HBM] * nw, out_specs=[_HBM] * nw,
        scratch_shapes=[pltpu.SemaphoreType.DMA((3 * nw,)), pltpu.SemaphoreType.DMA((3 * nw,))],
    )(*sends)


def _lane_pad(n):
    return -(-n // LANES) * LANES


def _pack_small(vecs, mat):
    width = max(sum(_lane_pad(v.shape[1]) for v in vecs), _lane_pad(mat.shape[1]))
    row0 = jnp.concatenate([_pad_cols(v, _lane_pad(v.shape[1])) for v in vecs], axis=1)
    rows = jnp.concatenate([_pad_cols(row0, width), _pad_cols(mat, width)], axis=0)
    return jnp.pad(rows, ((0, SUBLANES - rows.shape[0]), (0, 0)))


def _unpack_small(packed, sizes, mat_cols):
    vecs, off = [], 0
    for n in sizes:
        vecs.append(packed[0:1, off:off + n])
        off += _lane_pad(n)
    return vecs, packed[1:1 + CONV_WIDTH, :mat_cols]


def _adamw(w, g, m, v):
    m = ADAM_B1 * m + (1.0 - ADAM_B1) * g
    v = ADAM_B2 * v + (1.0 - ADAM_B2) * (g * g)
    m_hat = m / (1.0 - ADAM_B1 ** ADAM_STEP)
    v_hat = v / (1.0 - ADAM_B2 ** ADAM_STEP)
    delta = -ADAM_LR * (m_hat / (jnp.sqrt(v_hat) + ADAM_EPS) + ADAM_WD * w)
    return delta, m, v


BIG = ("w_in", "w_uq", "w_ukv", "w_attn_o", "w_ssm_o", "w_out", "w_gate", "w_up", "w_down", "w_ple_gate", "w_ple")
COL_SHARDED = ("w_in", "w_uq", "w_ukv", "w_gate", "w_up", "w_ple")
SMALL = ("mix_norm_pre", "mix_norm_post", "q_norm", "kv_norm", "conv_b", "dt_bias", "a_log", "d_skip", "ssm_norm",
         "ffn_norm_pre", "ffn_norm_post", "ple_norm_pre", "ple_norm_post")
WEIGHTS = ("mix_norm_pre", "mix_norm_post", "w_in", "q_norm", "w_uq", "kv_norm", "w_ukv", "conv_w", "conv_b", "dt_bias",
           "a_log", "d_skip", "ssm_norm", "w_attn_o", "w_ssm_o", "w_out", "ffn_norm_pre", "ffn_norm_post", "w_gate",
           "w_up", "w_down", "ple_norm_pre", "ple_norm_post", "w_ple_gate", "w_ple")


def _step(x, p, positions, target, w, m, v):
    xi, yi, ci = _place()
    me = 4 * xi + 2 * yi + ci
    chip = 2 * xi + yi

    gathered = _all_gather("gather_weights", [w[n].astype(BF16) for n in BIG])
    wts = {}
    for n, blocks in zip(BIG, gathered):
        rows, cols = blocks.shape[1:]
        if n in COL_SHARDED:
            wts[n] = blocks.transpose(1, 0, 2).reshape(rows, N_DEV * cols)
        else:
            wts[n] = blocks.reshape(N_DEV * rows, cols)
    cw_rows, cw_cols = w["conv_w"].shape
    (cw_all,) = _all_gather("gather_conv_w", [jnp.pad(w["conv_w"], ((0, SUBLANES - cw_rows), (0, 0)))])
    conv_w = cw_all[:, :cw_rows, :].transpose(1, 0, 2).reshape(cw_rows, N_DEV * cw_cols)

    small = {n: w[n] for n in SMALL}
    loss, grad_x, sg, d_conv_w, bg = _local_step(x, p, positions, target, small, conv_w, wts)

    sizes = [w[n].shape[1] for n in SMALL]
    sg_pack = _pack_small([sg[n] for n in SMALL], d_conv_w)
    (sg_all,) = _all_gather("gather_small_grads", [sg_pack])
    (sg_sum,) = _rowwise("sum_small_grads", lambda *a: functools.reduce(lambda s, t: s + t, a),
                         [sg_all[k] for k in range(N_DEV)], [], [(sg_pack.shape[1], F32)])
    sg_vecs, d_conv_w_sum = _unpack_small(sg_sum, sizes, d_conv_w.shape[1])
    grads = dict(zip(SMALL, sg_vecs))
    grads["conv_w"] = lax.dynamic_slice_in_dim(d_conv_w_sum, me * cw_cols, cw_cols, axis=1)

    by_dev = []
    for n in BIG:
        rows, cols = w[n].shape
        if n in COL_SHARDED:
            by_dev.append(bg[n].reshape(rows, N_DEV, cols).transpose(1, 0, 2))
        else:
            by_dev.append(bg[n].reshape(N_DEV, rows, cols))
    gots = _swap_with_sibling("rs_within_chip", by_dev)
    chip_f32, chip_bf16 = [], []
    for n, b, got in zip(BIG, by_dev, gots):
        rows, cols = w[n].shape
        keep = lax.dynamic_index_in_dim(b.reshape(4, 2, rows, cols), ci, axis=1, keepdims=False)
        s32, s16 = _rowwise("rs_add_sibling_" + n, lambda a, b_: (a + b_, a + b_),
                            [keep.reshape(4 * rows, cols), got.reshape(4 * rows, cols)], [], [(cols, F32), (cols, BF16)])
        chip_f32.append(s32.reshape(4, rows, cols))
        chip_bf16.append(s16.reshape(4, rows, cols))
    recvs = _swap_with_chips("rs_between_chips", chip_bf16)

    def sum_then_adamw(wv, mv, vv, own, r0, r1, r2):
        g = ((own + r0) + r1) + r2
        return (g,) + _adamw(wv, g, mv, vv)

    delta, new_m, new_v = {}, {}, {}
    for n, c32, recv in zip(BIG, chip_f32, recvs):
        cols = w[n].shape[1]
        own = lax.dynamic_index_in_dim(c32, chip, axis=0, keepdims=False)
        grads[n], delta[n], new_m[n], new_v[n] = _rowwise(
            "adamw_" + n, sum_then_adamw, [w[n], m[n], v[n], own, recv[0], recv[1], recv[2]], [], [(cols, F32)] * 4)
    packed = [_pack_small([d[n] for n in SMALL], d["conv_w"]) for d in (w, grads, m, v)]
    outs = _rowwise("adamw_small", _adamw, packed, [], [(packed[0].shape[1], F32)] * 3)
    for d, o in zip((delta, new_m, new_v), outs):
        vecs, mat = _unpack_small(o, sizes, cw_cols)
        d.update(zip(SMALL, vecs))
        d["conv_w"] = mat
    return loss, grad_x, grads, delta, new_m, new_v


def kernel(x, p, positions, mix_norm_pre, mix_norm_post, w_in, q_norm, w_uq, kv_norm, w_ukv, conv_w, conv_b, dt_bias, a_log, d_skip, ssm_norm, w_attn_o, w_ssm_o, w_out, ffn_norm_pre, ffn_norm_post, w_gate, w_up, w_down, ple_norm_pre, ple_norm_post, w_ple_gate, w_ple, loss_target, m_mix_norm_pre, m_mix_norm_post, m_w_in, m_q_norm, m_w_uq, m_kv_norm, m_w_ukv, m_conv_w, m_conv_b, m_dt_bias, m_a_log, m_d_skip, m_ssm_norm, m_w_attn_o, m_w_ssm_o, m_w_out, m_ffn_norm_pre, m_ffn_norm_post, m_w_gate, m_w_up, m_w_down, m_ple_norm_pre, m_ple_norm_post, m_w_ple_gate, m_w_ple, v_mix_norm_pre, v_mix_norm_post, v_w_in, v_q_norm, v_w_uq, v_kv_norm, v_w_ukv, v_conv_w, v_conv_b, v_dt_bias, v_a_log, v_d_skip, v_ssm_norm, v_w_attn_o, v_w_ssm_o, v_w_out, v_ffn_norm_pre, v_ffn_norm_post, v_w_gate, v_w_up, v_w_down, v_ple_norm_pre, v_ple_norm_post, v_w_ple_gate, v_w_ple):
    w_args = (mix_norm_pre, mix_norm_post, w_in, q_norm, w_uq, kv_norm, w_ukv, conv_w, conv_b, dt_bias, a_log, d_skip, ssm_norm, w_attn_o, w_ssm_o, w_out, ffn_norm_pre, ffn_norm_post, w_gate, w_up, w_down, ple_norm_pre, ple_norm_post, w_ple_gate, w_ple)
    m_args = (m_mix_norm_pre, m_mix_norm_post, m_w_in, m_q_norm, m_w_uq, m_kv_norm, m_w_ukv, m_conv_w, m_conv_b, m_dt_bias, m_a_log, m_d_skip, m_ssm_norm, m_w_attn_o, m_w_ssm_o, m_w_out, m_ffn_norm_pre, m_ffn_norm_post, m_w_gate, m_w_up, m_w_down, m_ple_norm_pre, m_ple_norm_post, m_w_ple_gate, m_w_ple)
    v_args = (v_mix_norm_pre, v_mix_norm_post, v_w_in, v_q_norm, v_w_uq, v_kv_norm, v_w_ukv, v_conv_w, v_conv_b, v_dt_bias, v_a_log, v_d_skip, v_ssm_norm, v_w_attn_o, v_w_ssm_o, v_w_out, v_ffn_norm_pre, v_ffn_norm_post, v_w_gate, v_w_up, v_w_down, v_ple_norm_pre, v_ple_norm_post, v_w_ple_gate, v_w_ple)

    def drop_layer(a):
        return a if a.ndim == 2 else a[0]

    w = {n: drop_layer(a) for n, a in zip(WEIGHTS, w_args)}
    m = {n: drop_layer(a) for n, a in zip(WEIGHTS, m_args)}
    v = {n: drop_layer(a) for n, a in zip(WEIGHTS, v_args)}
    loss, grad_x, grads, delta, new_m, new_v = _step(x[0], p[0, 0], positions[0], loss_target[0], w, m, v)
    loss = lax.psum(loss, ("x", "y", "c"))
    like = lambda d: [d[n].reshape(a.shape) for n, a in zip(WEIGHTS, w_args)]
    return (loss, grad_x[None], *like(grads), *like(delta), *like(new_m), *like(new_v))
```

```python
import functools

import jax
import jax.numpy as jnp
from jax import lax
from jax.experimental import pallas as pl
from jax.experimental.pallas import tpu as pltpu

F32 = jnp.float32
BF16 = jnp.bfloat16

EPS = 1e-6
QK_NOPE = 128
QK_ROPE = 64
V_DIM = 128
QK_PAD = 256
ROPE_THETA = 10000.0
SSM_HEADDIM = 64
D_STATE = 128
CONV_WIDTH = 4
CHUNK = 256
ADAM_LR = 0.001
ADAM_B1 = 0.9
ADAM_B2 = 0.999
ADAM_EPS = 1e-08
ADAM_WD = 0.01
ADAM_STEP = 10

N_DEV = 8
LANES = 128
SUBLANES = 8
PACK_W = 1024
VMEM_LIMIT = 56 * 1024 * 1024
ROW_TILE_BYTES = 6 * 1024 * 1024
FLASH_T = 512
MM_TILE_BYTES = 20 * 1024 * 1024
MESH = pl.DeviceIdType.MESH


def _pick(dim, prefs):
    if dim <= prefs[0]:
        return dim
    for p in prefs:
        if dim % p == 0:
            return p
    return dim


def _tile(dim, cap):
    if dim <= cap:
        return dim
    best = None
    for t in range(LANES, cap + 1, LANES):
        if dim % t == 0:
            best = t
    return best if best is not None else dim


def _params(sem):
    return pltpu.CompilerParams(dimension_semantics=sem, vmem_limit_bytes=VMEM_LIMIT)


def _dot(a, b):
    return lax.dot_general(a, b, (((1,), (0,)), ((), ())), preferred_element_type=F32)


def _dot_nt(a, b):
    return lax.dot_general(a, b, (((1,), (1,)), ((), ())), preferred_element_type=F32)


def _dot_tn(a, b):
    return lax.dot_general(a, b, (((0,), (0,)), ((), ())), preferred_element_type=F32)


def _mm(name, a, b, mode, out_dtype=F32, acc_in=None):
    if mode == "nn":
        (M, K), (K2, N) = a.shape, b.shape
    elif mode == "nt":
        (M, K), (N, K2) = a.shape, b.shape
    else:
        (K, M), (K2, N) = a.shape, b.shape
    assert K == K2, (name, a.shape, b.shape, mode)
    tm = _tile(M, 1024)
    tn = _tile(N, 1024 if acc_in is not None else 1536)
    tk = _tile(K, 2048)
    while tk > 512 and 2 * (tm * tk * a.dtype.itemsize + tk * tn * b.dtype.itemsize) > MM_TILE_BYTES:
        tk = _tile(K, tk - LANES)
    nk = K // tk
    dot = {"nn": _dot, "nt": _dot_nt, "tn": _dot_tn}[mode]
    has_acc = acc_in is not None

    def body(*refs):
        if has_acc:
            a_ref, b_ref, c_ref, o_ref, acc = refs
        else:
            a_ref, b_ref, o_ref, acc = refs
        k = pl.program_id(2)

        @pl.when(k == 0)
        def _():
            acc[...] = jnp.zeros_like(acc)

        acc[...] += dot(a_ref[...].astype(BF16), b_ref[...].astype(BF16))

        @pl.when(k == nk - 1)
        def _():
            r = acc[...]
            if has_acc:
                r = r + c_ref[...]
            o_ref[...] = r.astype(o_ref.dtype)

    if mode == "tn":
        a_spec = pl.BlockSpec((tk, tm), lambda i, j, k: (k, i))
    else:
        a_spec = pl.BlockSpec((tm, tk), lambda i, j, k: (i, k))
    if mode == "nt":
        b_spec = pl.BlockSpec((tn, tk), lambda i, j, k: (j, k))
    else:
        b_spec = pl.BlockSpec((tk, tn), lambda i, j, k: (k, j))
    o_spec = pl.BlockSpec((tm, tn), lambda i, j, k: (i, j))
    in_specs = [a_spec, b_spec] + ([o_spec] if has_acc else [])
    args = (a, b) + ((acc_in,) if has_acc else ())
    return pl.pallas_call(
        body, name=name, grid=(M // tm, N // tn, nk), in_specs=in_specs, out_specs=o_spec,
        out_shape=jax.ShapeDtypeStruct((M, N), out_dtype), scratch_shapes=[pltpu.VMEM((tm, tn), F32)],
        input_output_aliases=({2: 0} if has_acc and out_dtype == F32 else {}),
        compiler_params=_params(("parallel", "parallel", "arbitrary")),
    )(*args)


def _row_tile(n_rows, bytes_per_row):
    tr = 512
    while tr > SUBLANES and tr * bytes_per_row > ROW_TILE_BYTES:
        tr //= 2
    while n_rows % tr:
        tr //= 2
    return tr


def _acc_add(a_ref, v):
    if v.shape[0] == 1:
        a_ref[0:1, :] += v
    else:
        a_ref[...] += v.reshape(v.shape[0] // SUBLANES, SUBLANES, v.shape[1]).sum(axis=0)


def _rowwise(name, fn, rows, bcs, outs, accs=(), tr=None):
    n_rows = rows[0].shape[0]
    if tr is None:
        per_row = sum(r.shape[1] * r.dtype.itemsize for r in rows) + sum(w * jnp.dtype(d).itemsize for w, d in outs)
        tr = _row_tile(n_rows, per_row)
    n_r, n_b, n_o, n_a = len(rows), len(bcs), len(outs), len(accs)

    def body(*refs):
        ins = [r[...].astype(F32) for r in refs[: n_r + n_b]]
        res = fn(*ins)
        res = res if isinstance(res, (tuple, list)) else (res,)
        o_refs = refs[n_r + n_b: n_r + n_b + n_o]
        a_refs = refs[n_r + n_b + n_o:]
        for o, v in zip(o_refs, res[:n_o]):
            o[...] = v.astype(o.dtype)
        if n_a:
            @pl.when(pl.program_id(0) == 0)
            def _():
                for a in a_refs:
                    a[...] = jnp.zeros_like(a)

            for a, v in zip(a_refs, res[n_o:]):
                _acc_add(a, v)

    in_specs = [pl.BlockSpec((tr, r.shape[1]), lambda i: (i, 0)) for r in rows]
    in_specs += [pl.BlockSpec((1, b.shape[1]), lambda i: (0, 0)) for b in bcs]
    out_specs = [pl.BlockSpec((tr, w), lambda i: (i, 0)) for w, _ in outs]
    out_specs += [pl.BlockSpec((SUBLANES, w), lambda i: (0, 0)) for w in accs]
    out_shape = [jax.ShapeDtypeStruct((n_rows, w), d) for w, d in outs]
    out_shape += [jax.ShapeDtypeStruct((SUBLANES, w), F32) for w in accs]
    res = pl.pallas_call(
        body, name=name, grid=(n_rows // tr,), in_specs=in_specs, out_specs=out_specs, out_shape=out_shape,
        compiler_params=_params(("arbitrary",) if n_a else ("parallel",)),
    )(*rows, *bcs)
    return tuple(res)


def _rowwise_bwd(name, fn, rows, bcs, cts, need_rows, row_dtypes, need_bcs=None, adds=None, fwd_sums=(), tr=None):
    n_rows = rows[0].shape[0]
    adds = adds or {}
    need_bcs = list(range(len(bcs))) if need_bcs is None else list(need_bcs)
    ct_arrays = [c for c in cts if not isinstance(c, float)]
    add_keys = sorted(adds)
    add_arrays = [adds[k] for k in add_keys]
    if tr is None:
        per_row = sum(r.shape[1] * r.dtype.itemsize for r in list(rows) + ct_arrays + add_arrays)
        per_row += sum(rows[i].shape[1] * jnp.dtype(d).itemsize for i, d in zip(need_rows, row_dtypes))
        tr = _row_tile(n_rows, 2 * per_row)
    n_r, n_b, n_c, n_ad = len(rows), len(bcs), len(ct_arrays), len(add_arrays)
    n_go, n_gb, n_fs = len(need_rows), len(need_bcs), len(fwd_sums)

    def body(*refs):
        pos = 0
        r_t = [r[...].astype(F32) for r in refs[pos: pos + n_r]]
        pos += n_r
        b_t = [r[...].astype(F32) for r in refs[pos: pos + n_b]]
        pos += n_b
        c_t = [r[...].astype(F32) for r in refs[pos: pos + n_c]]
        pos += n_c
        ad_t = [r[...].astype(F32) for r in refs[pos: pos + n_ad]]
        pos += n_ad
        go_refs = refs[pos: pos + n_go]
        pos += n_go
        acc_refs = refs[pos:]

        def wrapped(*a):
            r = fn(*a)
            return tuple(r) if isinstance(r, (tuple, list)) else (r,)

        outs, vjp = jax.vjp(wrapped, *r_t, *b_t)
        it = iter(c_t)
        full = tuple(jnp.full(o.shape, c, F32) if isinstance(c, float) else next(it) for o, c in zip(outs, cts))
        grads = vjp(full)
        for o_ref, i in zip(go_refs, need_rows):
            g = grads[i]
            if i in adds:
                g = g + ad_t[add_keys.index(i)]
            o_ref[...] = g.astype(o_ref.dtype)

        @pl.when(pl.program_id(0) == 0)
        def _():
            for a in acc_refs:
                a[...] = jnp.zeros_like(a)

        for a, j in zip(acc_refs[:n_gb], need_bcs):
            _acc_add(a, grads[n_r + j])
        for a, j in zip(acc_refs[n_gb:], fwd_sums):
            a[0:1, :] += jnp.full((1, LANES), jnp.sum(outs[j]), F32)

    def row_spec(w):
        return pl.BlockSpec((tr, w), lambda i: (i, 0))

    in_specs = [row_spec(r.shape[1]) for r in rows]
    in_specs += [pl.BlockSpec((1, b.shape[1]), lambda i: (0, 0)) for b in bcs]
    in_specs += [row_spec(c.shape[1]) for c in ct_arrays] + [row_spec(a.shape[1]) for a in add_arrays]
    out_specs = [row_spec(rows[i].shape[1]) for i in need_rows]
    out_specs += [pl.BlockSpec((SUBLANES, bcs[j].shape[1]), lambda i: (0, 0)) for j in need_bcs]
    out_specs += [pl.BlockSpec((SUBLANES, LANES), lambda i: (0, 0)) for _ in fwd_sums]
    out_shape = [jax.ShapeDtypeStruct((n_rows, rows[i].shape[1]), d) for i, d in zip(need_rows, row_dtypes)]
    out_shape += [jax.ShapeDtypeStruct((SUBLANES, bcs[j].shape[1]), F32) for j in need_bcs]
    out_shape += [jax.ShapeDtypeStruct((SUBLANES, LANES), F32) for _ in fwd_sums]
    res = pl.pallas_call(
        body, name=name, grid=(n_rows // tr,), in_specs=in_specs, out_specs=out_specs, out_shape=out_shape,
        compiler_params=_params(("arbitrary",)),
    )(*rows, *bcs, *ct_arrays, *add_arrays)
    return tuple(res)


def _fold(acc):
    return jnp.sum(acc, axis=0, keepdims=True)


def _rms(x, w):
    return x * lax.rsqrt(jnp.mean(x * x, axis=-1, keepdims=True) + EPS) * w


def _sigmoid(x):
    return jax.nn.sigmoid(x)


def _silu(x):
    return x * _sigmoid(x)


def _log1p(u):
    series = u * (1.0 - u * (0.5 - u * (1.0 / 3.0 - u * 0.25)))
    return jnp.where(u < 0.01, series, jnp.log(1.0 + u))


def _softplus(x):
    return jnp.maximum(x, 0.0) + _log1p(jnp.exp(-jnp.abs(x)))


def _st_pre(x, w):
    return _rms(x, w)


def _st_lora_norms(dq):
    def fn(cqkv, qn, kvn):
        return _rms(cqkv[:, :dq], qn), _rms(cqkv[:, dq:], kvn)
    return fn


def _st_gated_norm(n_groups):
    def fn(y, xs, z, dskip, wn):
        yz = (y + dskip * xs) * _silu(z)
        gw = yz.shape[1] // n_groups
        parts = [_rms(yz[:, g * gw:(g + 1) * gw], wn[:, g * gw:(g + 1) * gw]) for g in range(n_groups)]
        return jnp.concatenate(parts, axis=1)
    return fn


def _st_mix(d):
    def fn(g, ao, so):
        return _sigmoid(g[:, :d]) * ao + _sigmoid(g[:, d:]) * so
    return fn


def _st_res_norm(h, y, w_post, w_pre):
    h2 = h + _rms(y, w_post)
    return h2, _rms(h2, w_pre)


def _st_swiglu(gt, up):
    return _silu(gt) * up


def _st_loss(pe, gl, h2, tgt, w_post):
    e = pe * _sigmoid(gl)
    diff = h2 + _rms(e, w_post) - tgt
    return 0.5 * jnp.mean(diff * diff, axis=-1, keepdims=True)


def _rope_tables(positions):
    half = QK_ROPE // 2
    inv_freq = ROPE_THETA ** (-jnp.arange(0, QK_ROPE, 2, dtype=F32) / QK_ROPE)
    ang = positions.astype(F32).reshape(-1, 1) * inv_freq
    cos, sin = jnp.cos(ang), jnp.sin(ang)
    n = ang.shape[0]
    z = lambda w: jnp.zeros((n, w), F32)
    c_tab = jnp.concatenate([jnp.ones((n, QK_NOPE), F32), cos, cos, z(QK_PAD - QK_NOPE - QK_ROPE)], axis=1)
    a_tab = jnp.concatenate([z(QK_NOPE), -sin, z(half), z(QK_PAD - QK_NOPE - QK_ROPE)], axis=1)
    b_tab = jnp.concatenate([z(QK_NOPE), z(half), sin, z(QK_PAD - QK_NOPE - QK_ROPE)], axis=1)
    return c_tab, a_tab, b_tab


def _rot(x, c, a, b):
    half = QK_ROPE // 2
    return x * c + pltpu.roll(x, QK_PAD - half, axis=1) * a + pltpu.roll(x, half, axis=1) * b


def _rot_t(g, c, a, b):
    half = QK_ROPE // 2
    return g * c + pltpu.roll(g * a, half, axis=1) + pltpu.roll(g * b, QK_PAD - half, axis=1)


def _st_rope(n_heads):
    def fn(qraw, kraw, sm, c, a, b):
        kpe = _rot(sm[:, :QK_PAD], c, a, b)
        scale = float(QK_NOPE + QK_ROPE) ** -0.5
        q = [_rot(qraw[:, h * QK_PAD:(h + 1) * QK_PAD], c, a, b) * scale for h in range(n_heads)]
        k = [kraw[:, h * QK_PAD:(h + 1) * QK_PAD] + kpe for h in range(n_heads)]
        return jnp.concatenate(q, axis=1), jnp.concatenate(k, axis=1)
    return fn


def _st_rope_bwd(n_heads):
    def fn(dq, dk, c, a, b):
        dqraw = [_rot_t(dq[:, h * QK_PAD:(h + 1) * QK_PAD], c, a, b) for h in range(n_heads)]
        dks = dk[:, :QK_PAD]
        for h in range(1, n_heads):
            dks = dks + dk[:, h * QK_PAD:(h + 1) * QK_PAD]
        return jnp.concatenate(dqraw, axis=1), _rot_t(dks, c, a, b), dk
    return fn


def _split3(x):
    h1 = x.astype(BF16)
    r1 = x - h1.astype(F32)
    h2 = r1.astype(BF16)
    h3 = (r1 - h2.astype(F32)).astype(BF16)
    return h1, h2, h3


def _tri_dot(tri, x):
    h1, h2, h3 = _split3(x)
    return (_dot(tri, h3) + _dot(tri, h2)) + _dot(tri, h1)


def _dot_tri(x, tri):
    h1, h2, h3 = _split3(x)
    return (_dot(h3, tri) + _dot(h2, tri)) + _dot(h1, tri)


def _st_dt(sm, bias, alog):
    x = sm[:, QK_PAD:] + bias
    dt = _softplus(x)
    return dt, dt * (-jnp.exp(alog))


def _st_dt_bwd(sm, ddt, dcum, bias, alog):
    n = sm.shape[0]
    i = lax.broadcasted_iota(jnp.int32, (n, n), 0)
    j = lax.broadcasted_iota(jnp.int32, (n, n), 1)
    upper = (j >= i).astype(BF16)
    dda = _tri_dot(upper, dcum)
    x = sm[:, QK_PAD:] + bias
    dt = _softplus(x)
    a = -jnp.exp(alog)
    draw = (ddt + dda * a) * _sigmoid(x)
    return draw, draw, dda * dt * a


def _conv_fwd(xbc, w, b):
    S, C = xbc.shape
    tr = _pick(S, (512, 256))
    tc = _pick(C, (1024, 512, 256, 128))
    hb = tr // SUBLANES

    def body(x_ref, halo_ref, w_ref, b_ref, c_ref, a_ref, ext):
        i = pl.program_id(1)
        halo = jnp.where(i == 0, 0.0, halo_ref[...])
        ext[0:SUBLANES, :] = halo
        ext[SUBLANES:, :] = x_ref[...]
        wv = w_ref[...]
        acc = b_ref[...] + wv[CONV_WIDTH - 1:CONV_WIDTH, :] * x_ref[...]
        for k in range(CONV_WIDTH - 1):
            off = SUBLANES - (CONV_WIDTH - 1) + k
            acc = acc + wv[k:k + 1, :] * ext[pl.ds(off, tr), :]
        c_ref[...] = acc
        a_ref[...] = _silu(acc)

    return pl.pallas_call(
        body, name="conv_fwd", grid=(C // tc, S // tr),
        in_specs=[pl.BlockSpec((tr, tc), lambda j, i: (i, j)),
                  pl.BlockSpec((SUBLANES, tc), lambda j, i: (jnp.maximum(i * hb - 1, 0), j)),
                  pl.BlockSpec((CONV_WIDTH, tc), lambda j, i: (0, j)),
                  pl.BlockSpec((1, tc), lambda j, i: (0, j))],
        out_specs=[pl.BlockSpec((tr, tc), lambda j, i: (i, j))] * 2,
        out_shape=[jax.ShapeDtypeStruct((S, C), F32)] * 2,
        scratch_shapes=[pltpu.VMEM((tr + SUBLANES, tc), F32)],
        compiler_params=_params(("parallel", "arbitrary")),
    )(xbc, xbc, w, b)


def _conv_bwd(xbc, dconv, w):
    S, C = xbc.shape
    tr = _pick(S, (512, 256))
    tc = _pick(C, (1024, 512, 256, 128))
    hb = tr // SUBLANES
    n_i = S // tr

    def body(x_ref, halo_ref, d_ref, dnext_ref, w_ref, dx_ref, dw_ref, ext, dext):
        i = pl.program_id(1)
        ext[0:SUBLANES, :] = jnp.where(i == 0, 0.0, halo_ref[...])
        ext[SUBLANES:, :] = x_ref[...]
        dext[0:tr, :] = d_ref[...]
        dext[tr:, :] = jnp.where(i == n_i - 1, 0.0, dnext_ref[...])
        wv = w_ref[...]
        d = d_ref[...]

        @pl.when(i == 0)
        def _():
            dw_ref[...] = jnp.zeros_like(dw_ref)

        dx = wv[CONV_WIDTH - 1:CONV_WIDTH, :] * d
        for k in range(CONV_WIDTH):
            if k < CONV_WIDTH - 1:
                dx = dx + wv[k:k + 1, :] * dext[pl.ds(CONV_WIDTH - 1 - k, tr), :]
                xs = ext[pl.ds(SUBLANES - (CONV_WIDTH - 1) + k, tr), :]
            else:
                xs = x_ref[...]
            prod = d * xs
            dw_ref[k * SUBLANES:(k + 1) * SUBLANES, :] += prod.reshape(tr // SUBLANES, SUBLANES, tc).sum(axis=0)
        dx_ref[...] = dx.astype(dx_ref.dtype)

    return pl.pallas_call(
        body, name="conv_bwd", grid=(C // tc, n_i),
        in_specs=[pl.BlockSpec((tr, tc), lambda j, i: (i, j)),
                  pl.BlockSpec((SUBLANES, tc), lambda j, i: (jnp.maximum(i * hb - 1, 0), j)),
                  pl.BlockSpec((tr, tc), lambda j, i: (i, j)),
                  pl.BlockSpec((SUBLANES, tc), lambda j, i: (jnp.minimum((i + 1) * hb, S // SUBLANES - 1), j)),
                  pl.BlockSpec((CONV_WIDTH, tc), lambda j, i: (0, j))],
        out_specs=[pl.BlockSpec((tr, tc), lambda j, i: (i, j)),
                   pl.BlockSpec((CONV_WIDTH * SUBLANES, tc), lambda j, i: (0, j))],
        out_shape=[jax.ShapeDtypeStruct((S, C), BF16), jax.ShapeDtypeStruct((CONV_WIDTH * SUBLANES, C), F32)],
        scratch_shapes=[pltpu.VMEM((tr + SUBLANES, tc), F32), pltpu.VMEM((tr + SUBLANES, tc), F32)],
        compiler_params=_params(("parallel", "arbitrary")),
    )(xbc, xbc, dconv, dconv, w)


def _st_dconv(d_inner):
    def fn(xc, dxa, dxb, db_, dc_):
        s = _sigmoid(xc)
        g = jnp.concatenate([dxa + dxb, db_, dc_], axis=1) * (s * (1.0 + xc * (1.0 - s)))
        return g, g
    return fn


def _chunk_setup(b_ref, c_ref, dac_ref, dar_ref, L):
    ii = lax.broadcasted_iota(jnp.int32, (L, L), 0)
    jj = lax.broadcasted_iota(jnp.int32, (L, L), 1)
    tri = ii >= jj
    cum_c = _tri_dot(tri.astype(BF16), dac_ref[0])
    cum_r = _dot_tri(dar_ref[0], (ii <= jj).astype(BF16))
    bm = b_ref[...].astype(BF16)
    cm = c_ref[...].astype(BF16)
    return tri, cum_c, cum_r, bm, cm, _dot_nt(cm, bm)


def _ssd_specs(d_inner, n_groups, gw, L, rp, chunk_of):
    bb0 = d_inner // D_STATE
    cb0 = bb0 + n_groups
    return [pl.BlockSpec((L, gw), lambda g, c: (chunk_of(c), g)),
            pl.BlockSpec((L, D_STATE), lambda g, c: (chunk_of(c), bb0 + g)),
            pl.BlockSpec((L, D_STATE), lambda g, c: (chunk_of(c), cb0 + g)),
            pl.BlockSpec((1, L, LANES), lambda g, c: (g, chunk_of(c), 0)),
            pl.BlockSpec((1, L, LANES), lambda g, c: (g, chunk_of(c), 0)),
            pl.BlockSpec((1, rp, L), lambda g, c: (g, 0, chunk_of(c)))]


def _ssd_fwd(xbc_a, dt_col, da_col, da_row, d_inner, n_groups, R):
    S = xbc_a.shape[0]
    L = min(CHUNK, S)
    NC = S // L
    P, N = SSM_HEADDIM, D_STATE
    gw = R * P
    rp = da_row.shape[1]

    def body(x_ref, b_ref, c_ref, dt_ref, dac_ref, dar_ref, y_ref, st_ref, state):
        @pl.when(pl.program_id(1) == 0)
        def _():
            state[...] = jnp.zeros_like(state)

        st_ref[0, 0] = state[...]
        tri, cum_c, cum_r, bm, cm, gm = _chunk_setup(b_ref, c_ref, dac_ref, dar_ref, L)
        dt = dt_ref[0]
        for r in range(R):
            cc = cum_c[:, r:r + 1]
            cr = cum_r[r:r + 1, :]
            lam = jnp.exp(jnp.where(tri, cc - cr, -jnp.inf))
            m = (gm * lam).astype(BF16)
            x = x_ref[:, r * P:(r + 1) * P] * dt[:, r:r + 1]
            s_r = state[r * P:(r + 1) * P, :]
            y_off = _dot_nt(cm, s_r.astype(BF16)) * jnp.exp(cc)
            y_ref[:, r * P:(r + 1) * P] = _dot(m, x.astype(BF16)) + y_off
            last = cr[:, L - 1:L]
            xw = (x * jnp.exp(last - cc)).astype(BF16)
            state[r * P:(r + 1) * P, :] = s_r * jnp.exp(last) + _dot_tn(xw, bm)

    return pl.pallas_call(
        body, name="ssd_fwd", grid=(n_groups, NC),
        in_specs=_ssd_specs(d_inner, n_groups, gw, L, rp, lambda c: c),
        out_specs=[pl.BlockSpec((L, gw), lambda g, c: (c, g)),
                   pl.BlockSpec((1, 1, gw, N), lambda g, c: (g, c, 0, 0))],
        out_shape=[jax.ShapeDtypeStruct((S, d_inner), F32), jax.ShapeDtypeStruct((n_groups, NC, gw, N), F32)],
        scratch_shapes=[pltpu.VMEM((gw, N), F32)],
        compiler_params=_params(("parallel", "arbitrary")),
    )(xbc_a, xbc_a, xbc_a, dt_col, da_col, da_row)


def _ssd_bwd(xbc_a, dt_col, da_col, da_row, states, dy, d_inner, n_groups, R):
    S = xbc_a.shape[0]
    L = min(CHUNK, S)
    NC = S // L
    P, N = SSM_HEADDIM, D_STATE
    gw = R * P
    rp = da_row.shape[1]
    rev = lambda c: NC - 1 - c

    def body(x_ref, b_ref, c_ref, dt_ref, dac_ref, dar_ref, st_ref, dy_ref,
             dx_ref, db_ref, dc_ref, ddt_ref, dcc_ref, dcr_ref, dstate):
        @pl.when(pl.program_id(1) == 0)
        def _():
            dstate[...] = jnp.zeros_like(dstate)

        tri, cum_c, cum_r, bm, cm, gm = _chunk_setup(b_ref, c_ref, dac_ref, dar_ref, L)
        dt = dt_ref[0]
        lane = lax.broadcasted_iota(jnp.int32, (L, LANES), 1)
        sub = lax.broadcasted_iota(jnp.int32, (rp, L), 0)
        is_last = lax.broadcasted_iota(jnp.int32, (L, 1), 0) == L - 1
        d_g = jnp.zeros((L, L), F32)
        dc_acc = jnp.zeros((L, N), F32)
        db_acc = jnp.zeros((L, N), F32)
        ddt_out = jnp.zeros((L, LANES), F32)
        dcc_out = jnp.zeros((L, LANES), F32)
        dcr_out = jnp.zeros((rp, L), F32)
        for r in range(R):
            cc = cum_c[:, r:r + 1]
            cr = cum_r[r:r + 1, :]
            lam = jnp.exp(jnp.where(tri, cc - cr, -jnp.inf))
            m = gm * lam
            dtc = dt[:, r:r + 1]
            xh = x_ref[:, r * P:(r + 1) * P]
            x = xh * dtc
            xb = x.astype(BF16)
            d_y = dy_ref[:, r * P:(r + 1) * P]
            d_yb = d_y.astype(BF16)
            s_r = st_ref[0, 0, r * P:(r + 1) * P, :]
            s_rb = s_r.astype(BF16)
            ds_n = dstate[r * P:(r + 1) * P, :]
            ds_nb = ds_n.astype(BF16)
            e = jnp.exp(cc)
            last = cr[:, L - 1:L]
            e_last = jnp.exp(last)
            w = jnp.exp(last - cc)
            d_m = _dot_nt(d_yb, xb)
            d_x = _dot_tn(m.astype(BF16), d_yb)
            d_ye = (d_y * e).astype(BF16)
            dc_acc = dc_acc + _dot(d_ye, s_rb)
            ds_part = _dot_tn(d_ye, cm)
            y_off = _dot_nt(cm, s_rb) * e
            dcum = jnp.sum(d_y * y_off, axis=1, keepdims=True)
            d_xw = _dot_nt(bm, ds_nb)
            d_x = d_x + d_xw * w
            dw_w = jnp.sum(d_xw * x, axis=1, keepdims=True) * w
            db_acc = db_acc + _dot((x * w).astype(BF16), ds_nb)
            d_last = jnp.sum(ds_n * s_r, keepdims=True) * e_last + jnp.sum(dw_w, keepdims=True)
            dcum = dcum - dw_w
            dstate[r * P:(r + 1) * P, :] = e_last * ds_n + ds_part
            d_g = d_g + d_m * lam
            q = d_m * m
            dcum = dcum + jnp.sum(q, axis=1, keepdims=True) + jnp.where(is_last, d_last, 0.0)
            dcum_row = -jnp.sum(q, axis=0, keepdims=True)
            dx_ref[:, r * P:(r + 1) * P] = d_x * dtc
            ddt = jnp.sum(d_x * xh, axis=1, keepdims=True)
            ddt_out = ddt_out + jnp.where(lane == r, ddt, 0.0)
            dcc_out = dcc_out + jnp.where(lane == r, dcum, 0.0)
            dcr_out = dcr_out + jnp.where(sub == r, dcum_row, 0.0)
        d_gb = d_g.astype(BF16)
        dc_ref[...] = dc_acc + _dot(d_gb, bm)
        db_ref[...] = db_acc + _dot_tn(d_gb, cm)
        ddt_ref[0] = ddt_out
        dcc_ref[0] = dcc_out
        dcr_ref[0] = dcr_out

    gn = n_groups * N
    return pl.pallas_call(
        body, name="ssd_bwd", grid=(n_groups, NC),
        in_specs=_ssd_specs(d_inner, n_groups, gw, L, rp, rev) + [
            pl.BlockSpec((1, 1, gw, N), lambda g, c: (g, rev(c), 0, 0)),
            pl.BlockSpec((L, gw), lambda g, c: (rev(c), g))],
        out_specs=[pl.BlockSpec((L, gw), lambda g, c: (rev(c), g)),
                   pl.BlockSpec((L, N), lambda g, c: (rev(c), g)),
                   pl.BlockSpec((L, N), lambda g, c: (rev(c), g)),
                   pl.BlockSpec((1, L, LANES), lambda g, c: (g, rev(c), 0)),
                   pl.BlockSpec((1, L, LANES), lambda g, c: (g, rev(c), 0)),
                   pl.BlockSpec((1, rp, L), lambda g, c: (g, 0, rev(c)))],
        out_shape=[jax.ShapeDtypeStruct((S, d_inner), F32), jax.ShapeDtypeStruct((S, gn), F32),
                   jax.ShapeDtypeStruct((S, gn), F32), jax.ShapeDtypeStruct((n_groups, S, LANES), F32),
                   jax.ShapeDtypeStruct((n_groups, S, LANES), F32), jax.ShapeDtypeStruct((n_groups, rp, S), F32)],
        scratch_shapes=[pltpu.VMEM((gw, N), F32)],
        compiler_params=_params(("parallel", "arbitrary")),
    )(xbc_a, xbc_a, xbc_a, dt_col, da_col, da_row, states, dy)


def _chunk_setup_t(b_ref, c_ref, dac_ref, dar_ref, L):
    ii = lax.broadcasted_iota(jnp.int32, (L, L), 0)
    jj = lax.broadcasted_iota(jnp.int32, (L, L), 1)
    lower = ii >= jj
    upper = ii <= jj
    cum_c = _tri_dot(lower.astype(BF16), dac_ref[0])
    cum_r = _dot_tri(dar_ref[0], upper.astype(BF16))
    bm = b_ref[...].astype(BF16)
    cm = c_ref[...].astype(BF16)
    return lower, upper, cum_c, cum_r, bm, cm


def _ssd_specs_t(d_inner, n_groups, gw, L, rp, chunk_of):
    bb0 = d_inner // D_STATE
    cb0 = bb0 + n_groups
    return [pl.BlockSpec((gw, L), lambda g, c: (g, chunk_of(c))),
            pl.BlockSpec((L, D_STATE), lambda g, c: (chunk_of(c), bb0 + g)),
            pl.BlockSpec((L, D_STATE), lambda g, c: (chunk_of(c), cb0 + g)),
            pl.BlockSpec((1, rp, L), lambda g, c: (g, 0, chunk_of(c))),
            pl.BlockSpec((1, L, LANES), lambda g, c: (g, chunk_of(c), 0)),
            pl.BlockSpec((1, rp, L), lambda g, c: (g, 0, chunk_of(c)))]


def _ssd_fwd_t(x_t, xbc_a, dt_row, da_col, da_row, d_inner, n_groups, R):
    S = xbc_a.shape[0]
    L = min(CHUNK, S)
    NC = S // L
    P, N = SSM_HEADDIM, D_STATE
    gw = R * P
    rp = da_row.shape[1]

    def body(x_ref, b_ref, c_ref, dt_ref, dac_ref, dar_ref, y_ref, st_ref, state):
        @pl.when(pl.program_id(1) == 0)
        def _():
            state[...] = jnp.zeros_like(state)

        st_ref[0, 0] = state[...]
        lower, upper, cum_c, cum_r, bm, cm = _chunk_setup_t(b_ref, c_ref, dac_ref, dar_ref, L)
        gm_t = _dot_nt(bm, cm)
        dt = dt_ref[0]
        for r in range(R):
            cc = cum_c[:, r:r + 1]
            cr = cum_r[r:r + 1, :]
            m_t = (gm_t * jnp.exp(jnp.where(upper, cr - cc, -jnp.inf))).astype(BF16)
            x = x_ref[r * P:(r + 1) * P, :] * dt[r:r + 1, :]
            s_r = state[r * P:(r + 1) * P, :]
            y_off = _dot_nt(s_r.astype(BF16), cm) * jnp.exp(cr)
            y_ref[r * P:(r + 1) * P, :] = _dot(x.astype(BF16), m_t) + y_off
            last = cr[:, L - 1:L]
            xw = (x * jnp.exp(last - cr)).astype(BF16)
            state[r * P:(r + 1) * P, :] = s_r * jnp.exp(last) + _dot(xw, bm)

    return pl.pallas_call(
        body, name="ssd_fwd", grid=(n_groups, NC),
        in_specs=_ssd_specs_t(d_inner, n_groups, gw, L, rp, lambda c: c),
        out_specs=[pl.BlockSpec((gw, L), lambda g, c: (g, c)),
                   pl.BlockSpec((1, 1, gw, N), lambda g, c: (g, c, 0, 0))],
        out_shape=[jax.ShapeDtypeStruct((d_inner, S), F32), jax.ShapeDtypeStruct((n_groups, NC, gw, N), F32)],
        scratch_shapes=[pltpu.VMEM((gw, N), F32)],
        compiler_params=_params(("parallel", "arbitrary")),
    )(x_t, xbc_a, xbc_a, dt_row, da_col, da_row)


def _ssd_bwd_t(x_t, xbc_a, dt_row, da_col, da_row, states, dy_t, d_inner, n_groups, R):
    S = xbc_a.shape[0]
    L = min(CHUNK, S)
    NC = S // L
    P, N = SSM_HEADDIM, D_STATE
    gw = R * P
    rp = da_row.shape[1]
    rev = lambda c: NC - 1 - c

    def body(x_ref, b_ref, c_ref, dt_ref, dac_ref, dar_ref, st_ref, dy_ref,
             dx_ref, db_ref, dc_ref, ddt_ref, dcum_ref, dstate):
        @pl.when(pl.program_id(1) == 0)
        def _():
            dstate[...] = jnp.zeros_like(dstate)

        lower, upper, cum_c, cum_r, bm, cm = _chunk_setup_t(b_ref, c_ref, dac_ref, dar_ref, L)
        gm = _dot_nt(cm, bm)
        gm_t = _dot_nt(bm, cm)
        dt = dt_ref[0]
        sub = lax.broadcasted_iota(jnp.int32, (rp, L), 0)
        is_last = lax.broadcasted_iota(jnp.int32, (1, L), 1) == L - 1
        d_g = jnp.zeros((L, L), F32)
        d_g_t = jnp.zeros((L, L), F32)
        dc_acc = jnp.zeros((L, N), F32)
        db_acc = jnp.zeros((L, N), F32)
        ddt_out = jnp.zeros((rp, L), F32)
        dcum_out = jnp.zeros((rp, L), F32)
        for r in range(R):
            cc = jnp.broadcast_to(cum_c[:, r:r + 1], (L, L))
            cr = cum_r[r:r + 1, :]
            lam = jnp.exp(jnp.where(lower, cc - cum_r[r:r + 1, :], -jnp.inf))
            lam_t = jnp.exp(jnp.where(upper, cr - cc, -jnp.inf))
            m = gm * lam
            m_t = gm_t * lam_t
            dtr = dt[r:r + 1, :]
            xh = x_ref[r * P:(r + 1) * P, :]
            x = xh * dtr
            xb = x.astype(BF16)
            d_y = dy_ref[r * P:(r + 1) * P, :]
            d_yb = d_y.astype(BF16)
            s_r = st_ref[0, 0, r * P:(r + 1) * P, :]
            s_rb = s_r.astype(BF16)
            ds_n = dstate[r * P:(r + 1) * P, :]
            ds_nb = ds_n.astype(BF16)
            e = jnp.exp(cr)
            last = cr[:, L - 1:L]
            e_last = jnp.exp(last)
            w = jnp.exp(last - cr)
            d_x = _dot(d_yb, m.astype(BF16))
            d_m = _dot_tn(d_yb, xb)
            d_m_t = _dot_tn(xb, d_yb)
            d_ye = (d_y * e).astype(BF16)
            dc_acc = dc_acc + _dot_tn(d_ye, s_rb)
            ds_part = _dot(d_ye, cm)
            y_off = _dot_nt(s_rb, cm) * e
            dcum = jnp.sum(d_y * y_off, axis=0, keepdims=True)
            d_xw = _dot_nt(ds_nb, bm)
            d_x = d_x + d_xw * w
            dw_w = jnp.sum(d_xw * x, axis=0, keepdims=True) * w
            db_acc = db_acc + _dot_tn((x * w).astype(BF16), ds_nb)
            d_last = jnp.sum(ds_n * s_r, keepdims=True) * e_last + jnp.sum(dw_w, keepdims=True)
            dstate[r * P:(r + 1) * P, :] = e_last * ds_n + ds_part
            d_g = d_g + d_m * lam
            d_g_t = d_g_t + d_m_t * lam_t
            dcum = (dcum - dw_w + jnp.sum(d_m_t * m_t, axis=0, keepdims=True)
                    - jnp.sum(d_m * m, axis=0, keepdims=True) + jnp.where(is_last, d_last, 0.0))
            dx_ref[r * P:(r + 1) * P, :] = d_x * dtr
            ddt = jnp.sum(d_x * xh, axis=0, keepdims=True)
            ddt_out = ddt_out + jnp.where(sub == r, ddt, 0.0)
            dcum_out = dcum_out + jnp.where(sub == r, dcum, 0.0)
        dc_ref[...] = dc_acc + _dot(d_g.astype(BF16), bm)
        db_ref[...] = db_acc + _dot(d_g_t.astype(BF16), cm)
        ddt_ref[0] = ddt_out
        dcum_ref[0] = dcum_out

    gn = n_groups * N
    return pl.pallas_call(
        body, name="ssd_bwd", grid=(n_groups, NC),
        in_specs=_ssd_specs_t(d_inner, n_groups, gw, L, rp, rev) + [
            pl.BlockSpec((1, 1, gw, N), lambda g, c: (g, rev(c), 0, 0)),
            pl.BlockSpec((gw, L), lambda g, c: (g, rev(c)))],
        out_specs=[pl.BlockSpec((gw, L), lambda g, c: (g, rev(c))),
                   pl.BlockSpec((L, N), lambda g, c: (rev(c), g)),
                   pl.BlockSpec((L, N), lambda g, c: (rev(c), g)),
                   pl.BlockSpec((1, rp, L), lambda g, c: (g, 0, rev(c))),
                   pl.BlockSpec((1, rp, L), lambda g, c: (g, 0, rev(c)))],
        out_shape=[jax.ShapeDtypeStruct((d_inner, S), F32), jax.ShapeDtypeStruct((S, gn), F32),
                   jax.ShapeDtypeStruct((S, gn), F32), jax.ShapeDtypeStruct((n_groups, rp, S), F32),
                   jax.ShapeDtypeStruct((n_groups, rp, S), F32)],
        scratch_shapes=[pltpu.VMEM((gw, N), F32)],
        compiler_params=_params(("parallel", "arbitrary")),
    )(x_t, xbc_a, xbc_a, dt_row, da_col, da_row, states, dy_t)


def _attn_scale():
    return float(QK_NOPE + QK_ROPE) ** -0.5


def _diag_mask(t, keys_first=False):
    rows = lax.broadcasted_iota(jnp.int32, (t, t), 0)
    cols = lax.broadcasted_iota(jnp.int32, (t, t), 1)
    return rows <= cols if keys_first else cols <= rows


def _tile_rows(ref, j, t):
    return ref[pl.ds(pl.multiple_of(j * t, t), t), :]


def _walk_pairs(lo, hi, body):
    def pair(t, carry):
        body(lo + 2 * t, carry)
        body(lo + 2 * t + 1, carry)
        return carry

    lax.fori_loop(0, (hi - lo) // 2, pair, 0)

    @pl.when((hi - lo) % 2 == 1)
    def _():
        body(hi - 1, 0)


def _carried(main_body, n_in, n_out, n_scratch, carry, grid):
    if carry is None:
        return main_body
    n_ci, n_co = len(carry["ins"]), len(carry["outs"])

    def body(*refs):
        pos = [0]

        def take(n):
            pos[0] += n
            return refs[pos[0] - n: pos[0]]

        ins, c_ins, outs, c_outs, scratch, sems = take(n_in), take(n_ci), take(n_out), take(n_co), take(n_scratch), take(3)
        steps = [pl.program_id(a) for a in range(len(grid))]

        @pl.when(functools.reduce(jnp.logical_and, [s == 0 for s in steps]))
        def _():
            for cp in carry["copies"](c_ins, c_outs, *sems):
                cp.start()

        main_body(*ins, *outs, *scratch)

        @pl.when(functools.reduce(jnp.logical_and, [s == n - 1 for s, n in zip(steps, grid)]))
        def _():
            for cp in carry["copies"](c_ins, c_outs, *sems):
                cp.wait()

    return body


def _carry_call(carry):
    if carry is None:
        return [], [], [], [], []
    sems = [pltpu.SemaphoreType.DMA((carry["n_remote"],)), pltpu.SemaphoreType.DMA((carry["n_remote"],)),
            pltpu.SemaphoreType.DMA((max(carry["n_local"], 1),))]
    return (list(carry["ins"]), [_HBM] * len(carry["ins"]), [_HBM] * len(carry["outs"]), list(carry["outs"]), sems)


def _flash_fwd(q, k, v, n_heads, carry=None):
    S = q.shape[0]
    T = min(FLASH_T, S)
    grid = (n_heads, S // T)
    c_args, c_in_specs, c_out_specs, c_out_shapes, c_sems = _carry_call(carry)

    def body(q_ref, k_ref, v_ref, o_ref, lse_ref, m_s, l_s, acc):
        i = pl.program_id(1)
        m_s[...] = jnp.full_like(m_s, -jnp.inf)
        l_s[...] = jnp.zeros_like(l_s)
        acc[...] = jnp.zeros_like(acc)
        qv = q_ref[...]

        def step(j, masked):
            s = _dot_nt(qv, _tile_rows(k_ref, j, T))
            if masked:
                s = jnp.where(_diag_mask(T), s, -jnp.inf)
            m_prev = m_s[...]
            m_new = jnp.maximum(m_prev, jnp.max(s, axis=1, keepdims=True))
            alpha = jnp.exp(m_prev - m_new)
            p = jnp.exp(s - m_new[:, :1])
            l_s[...] = alpha * l_s[...] + jnp.sum(p, axis=1, keepdims=True)
            acc[...] = alpha * acc[...] + _dot(p.astype(BF16), _tile_rows(v_ref, j, T))
            m_s[...] = m_new

        def loop_body(j, carry):
            step(j, False)
            return carry

        _walk_pairs(0, i, loop_body)
        step(i, True)
        o_ref[...] = acc[...] / l_s[...]
        lse_ref[0] = (m_s[...] + jnp.log(l_s[...]))[:, :1]

    res = pl.pallas_call(
        _carried(body, 3, 2, 3, carry, grid), name="flash_fwd", grid=grid,
        in_specs=[pl.BlockSpec((T, QK_PAD), lambda h, i: (i, h)),
                  pl.BlockSpec((S, QK_PAD), lambda h, i: (0, h)),
                  pl.BlockSpec((S, V_DIM), lambda h, i: (0, h))] + c_in_specs,
        out_specs=[pl.BlockSpec((T, V_DIM), lambda h, i: (i, h)),
                   pl.BlockSpec((1, T, 1), lambda h, i: (h, i, 0))] + c_out_specs,
        out_shape=[jax.ShapeDtypeStruct((S, n_heads * V_DIM), F32),
                   jax.ShapeDtypeStruct((n_heads, S, 1), F32)] + c_out_shapes,
        scratch_shapes=[pltpu.VMEM((T, V_DIM), F32), pltpu.VMEM((T, V_DIM), F32), pltpu.VMEM((T, V_DIM), F32)] + c_sems,
        compiler_params=_params(("arbitrary", "arbitrary")),
    )(q, k, v, *c_args)
    return res[0], res[1], list(res[2:])


def _flash_bwd_dq(q, k, v, do, lse, delta, n_heads):
    S = q.shape[0]
    T = min(FLASH_T, S)

    def body(q_ref, k_ref, v_ref, do_ref, lse_ref, dl_ref, dq_ref, acc):
        i = pl.program_id(1)
        acc[...] = jnp.zeros_like(acc)
        qv = q_ref[...]
        dov = do_ref[...]
        lse_c = lse_ref[0]
        dl_c = dl_ref[0]

        def step(j, masked):
            kt = _tile_rows(k_ref, j, T)
            s = _dot_nt(qv, kt)
            if masked:
                s = jnp.where(_diag_mask(T), s, -jnp.inf)
            p = jnp.exp(s - lse_c)
            ds = p * (_dot_nt(dov, _tile_rows(v_ref, j, T)) - dl_c)
            acc[...] += _dot(ds.astype(BF16), kt)

        def loop_body(j, carry):
            step(j, False)
            return carry

        _walk_pairs(0, i, loop_body)
        step(i, True)
        dq_ref[...] = acc[...] * _attn_scale()

    return pl.pallas_call(
        body, name="flash_bwd_dq", grid=(n_heads, S // T),
        in_specs=[pl.BlockSpec((T, QK_PAD), lambda h, i: (i, h)),
                  pl.BlockSpec((S, QK_PAD), lambda h, i: (0, h)),
                  pl.BlockSpec((S, V_DIM), lambda h, i: (0, h)),
                  pl.BlockSpec((T, V_DIM), lambda h, i: (i, h)),
                  pl.BlockSpec((1, T, 1), lambda h, i: (h, i, 0)),
                  pl.BlockSpec((1, T, 1), lambda h, i: (h, i, 0))],
        out_specs=pl.BlockSpec((T, QK_PAD), lambda h, i: (i, h)),
        out_shape=jax.ShapeDtypeStruct((S, n_heads * QK_PAD), F32),
        scratch_shapes=[pltpu.VMEM((T, QK_PAD), F32)],
        compiler_params=_params(("parallel", "arbitrary")),
    )(q, k, v, do, lse, delta)


def _flash_bwd_dkv(q, k, v, do, lse_row, delta_row, n_heads, carry=None):
    S = q.shape[0]
    T = min(FLASH_T, S)
    nq = S // T
    grid = (n_heads, S // T)
    c_args, c_in_specs, c_out_specs, c_out_shapes, c_sems = _carry_call(carry)

    def body(q_ref, k_ref, v_ref, do_ref, lse_ref, dl_ref, dk_ref, dv_ref, dk_acc, dv_acc):
        j = pl.program_id(1)
        dk_acc[...] = jnp.zeros_like(dk_acc)
        dv_acc[...] = jnp.zeros_like(dv_acc)
        kv = k_ref[...]
        vv = v_ref[...]

        def step(i, masked):
            qt = _tile_rows(q_ref, i, T)
            dot = _tile_rows(do_ref, i, T)
            cols = pl.ds(pl.multiple_of(i * T, T), T)
            s_t = _dot_nt(kv, qt)
            if masked:
                s_t = jnp.where(_diag_mask(T, keys_first=True), s_t, -jnp.inf)
            p_t = jnp.exp(s_t - lse_ref[0, :, cols])
            dv_acc[...] += _dot(p_t.astype(BF16), dot)
            ds_t = p_t * (_dot_nt(vv, dot) - dl_ref[0, :, cols])
            dk_acc[...] += _dot(ds_t.astype(BF16), qt)

        def loop_body(i, carry):
            step(i, False)
            return carry

        step(j, True)
        _walk_pairs(j + 1, nq, loop_body)
        dk_ref[...] = dk_acc[...]
        dv_ref[...] = dv_acc[...].astype(dv_ref.dtype)

    res = pl.pallas_call(
        _carried(body, 6, 2, 2, carry, grid), name="flash_bwd_dkv", grid=grid,
        in_specs=[pl.BlockSpec((S, QK_PAD), lambda h, j: (0, h)),
                  pl.BlockSpec((T, QK_PAD), lambda h, j: (j, h)),
                  pl.BlockSpec((T, V_DIM), lambda h, j: (j, h)),
                  pl.BlockSpec((S, V_DIM), lambda h, j: (0, h)),
                  pl.BlockSpec((1, 1, S), lambda h, j: (h, 0, 0)),
                  pl.BlockSpec((1, 1, S), lambda h, j: (h, 0, 0))] + c_in_specs,
        out_specs=[pl.BlockSpec((T, QK_PAD), lambda h, j: (j, h)),
                   pl.BlockSpec((T, V_DIM), lambda h, j: (j, h))] + c_out_specs,
        out_shape=[jax.ShapeDtypeStruct((S, n_heads * QK_PAD), F32),
                   jax.ShapeDtypeStruct((S, n_heads * V_DIM), BF16)] + c_out_shapes,
        scratch_shapes=[pltpu.VMEM((T, QK_PAD), F32), pltpu.VMEM((T, V_DIM), F32)] + c_sems,
        compiler_params=_params(("arbitrary", "arbitrary")),
    )(q, k, v, do, lse_row, delta_row, *c_args)
    return res[0], res[1], list(res[2:])


def _st_delta(n_heads):
    def fn(do, o):
        prod = do * o
        lane = lax.broadcasted_iota(jnp.int32, (do.shape[0], LANES), 1)
        out = jnp.zeros((do.shape[0], LANES), F32)
        for h in range(n_heads):
            out = out + jnp.where(lane == h, jnp.sum(prod[:, h * V_DIM:(h + 1) * V_DIM], axis=1, keepdims=True), 0.0)
        return out
    return fn


def _pad_cols(w, width):
    return jnp.pad(w, ((0, 0), (0, width - w.shape[1])))


def _dims(x, p, q_norm, kv_norm, w_uq, dt_bias, ssm_norm, conv_b):
    d = dict(S=x.shape[0], D=x.shape[1], PLE=p.shape[1], DQ=q_norm.shape[1], DKV=kv_norm.shape[1],
             NH=w_uq.shape[1] // (QK_NOPE + QK_ROPE), NHS=dt_bias.shape[1], DI=ssm_norm.shape[1],
             CONV=conv_b.shape[1])
    d["G"] = (d["CONV"] - d["DI"]) // (2 * D_STATE)
    d["R"] = d["NHS"] // d["G"]
    return d


def _assemble(name, blocks):
    rows, cols = blocks.shape[1:]
    if name in COL_SHARDED:
        return blocks.transpose(1, 0, 2).reshape(rows, N_DEV * cols)
    return blocks.reshape(N_DEV * rows, cols)


def _by_device(name, g):
    if name in COL_SHARDED:
        return g.reshape(g.shape[0], N_DEV, g.shape[1] // N_DEV).transpose(1, 0, 2)
    return g.reshape(N_DEV, g.shape[0] // N_DEV, g.shape[1])


def _local_step(x, p, positions, target, small, conv_w, wts, late_names=(), late_shards=()):
    wts = dict(wts)
    dm = _dims(x, p, small["q_norm"], small["kv_norm"], wts["w_uq"], small["dt_bias"], small["ssm_norm"],
               small["conv_b"])
    S, D, DQ, DKV, NH, NHS, DI, CONV, G, R = (dm[k] for k in ("S", "D", "DQ", "DKV", "NH", "NHS", "DI", "CONV", "G", "R"))
    rp = -(-R // SUBLANES) * SUBLANES
    L = min(CHUNK, S)

    w_in = wts["w_in"]
    o = [0]
    for n in (DQ, DKV, QK_ROPE, DI, CONV, NHS, D, D):
        o.append(o[-1] + n)
    w_cqkv, w_kr, w_z, w_xbc, w_dt, w_g = (w_in[:, o[0]:o[2]], w_in[:, o[2]:o[3]], w_in[:, o[3]:o[4]],
                                           w_in[:, o[4]:o[5]], w_in[:, o[5]:o[6]], w_in[:, o[6]:o[8]])
    zc = lambda n: jnp.zeros((D, n), BF16)
    w_sm = jnp.concatenate([zc(QK_NOPE), w_kr, zc(QK_PAD - QK_NOPE - QK_ROPE), w_dt, zc(LANES - NHS)], axis=1)
    w_q = jnp.pad(wts["w_uq"].reshape(DQ, NH, QK_NOPE + QK_ROPE),
                  ((0, 0), (0, 0), (0, QK_PAD - QK_NOPE - QK_ROPE))).reshape(DQ, NH * QK_PAD)
    ukv = wts["w_ukv"].reshape(DKV, NH, QK_NOPE + V_DIM)
    w_k = jnp.pad(ukv[:, :, :QK_NOPE], ((0, 0), (0, 0), (0, QK_PAD - QK_NOPE))).reshape(DKV, NH * QK_PAD)
    w_v = ukv[:, :, QK_NOPE:].reshape(DKV, NH * V_DIM)
    dt_bias_p, a_log_p = _pad_cols(small["dt_bias"], LANES), _pad_cols(small["a_log"], LANES)
    dskip_rep = jnp.repeat(small["d_skip"], SSM_HEADDIM, axis=1)
    c_tab, a_tab, b_tab = _rope_tables(positions)

    (u,) = _rowwise("pre_norm", _st_pre, [x], [small["mix_norm_pre"]], [(D, BF16)])
    cqkv = _mm("in_cqkv", u, w_cqkv, "nn")
    z = _mm("in_z", u, w_z, "nn")
    xbc = _mm("in_xbc", u, w_xbc, "nn")
    g = _mm("in_gates", u, w_g, "nn")
    sm = _mm("in_small", u, w_sm, "nn")

    lora_fn = _st_lora_norms(DQ)
    cq_n, ckv_n = _rowwise("lora_norms", lora_fn, [cqkv], [small["q_norm"], small["kv_norm"]], [(DQ, BF16), (DKV, BF16)])
    qraw = _mm("up_q", cq_n, w_q, "nn")
    kraw = _mm("up_k", ckv_n, w_k, "nn")
    v = _mm("up_v", ckv_n, w_v, "nn", out_dtype=BF16)
    q, k = _rowwise("rope", _st_rope(NH), [qraw, kraw, sm, c_tab, a_tab, b_tab], [],
                    [(NH * QK_PAD, BF16), (NH * QK_PAD, BF16)])
    attn, lse, late_blocks = _flash_fwd(q, k, v, NH, carry=_gather_carry(list(late_shards)) if late_names else None)
    wts.update({n: _assemble(n, b) for n, b in zip(late_names, late_blocks)})

    xbc_c, xbc_a = _conv_fwd(xbc, conv_w, small["conv_b"])
    dt, da = _rowwise("dt", _st_dt, [sm], [dt_bias_p, a_log_p], [(LANES, F32), (LANES, F32)])

    def col_layout(t):
        return _pad_cols(t[:, :NHS].reshape(S, G, R).transpose(1, 0, 2).reshape(G * S, R), LANES).reshape(G, S, LANES)

    def row_layout(t):
        return jnp.pad(t[:, :NHS].reshape(S, G, R).transpose(1, 2, 0), ((0, 0), (0, rp - R), (0, 0)))

    dt_row, da_col, da_row = row_layout(dt), col_layout(da), row_layout(da)
    xs = xbc_a[:, :DI]
    xs_t = xs.T
    y_t, states = _ssd_fwd_t(xs_t, xbc_a, dt_row, da_col, da_row, DI, G, R)
    y = y_t.T
    gn_fn = _st_gated_norm(G)
    (ssm,) = _rowwise("gated_norm", gn_fn, [y, xs, z], [dskip_rep, small["ssm_norm"]], [(DI, BF16)])

    ao = _mm("attn_o", attn, wts["w_attn_o"], "nn")
    so = _mm("ssm_o", ssm, wts["w_ssm_o"], "nn")
    mix_fn = _st_mix(D)
    (mixed,) = _rowwise("mix", mix_fn, [g, ao, so], [], [(D, BF16)])
    mo = _mm("out_proj", mixed, wts["w_out"], "nn")
    h1, f = _rowwise("res1", _st_res_norm, [x, mo], [small["mix_norm_post"], small["ffn_norm_pre"]], [(D, F32), (D, BF16)])
    gt = _mm("ffn_gate", f, wts["w_gate"], "nn")
    up = _mm("ffn_up", f, wts["w_up"], "nn")
    (act,) = _rowwise("swiglu", _st_swiglu, [gt, up], [], [(gt.shape[1], BF16)])
    dn = _mm("ffn_down", act, wts["w_down"], "nn")
    h2, a3 = _rowwise("res2", _st_res_norm, [h1, dn], [small["ffn_norm_post"], small["ple_norm_pre"]], [(D, F32), (D, BF16)])
    gl = _mm("ple_gate", a3, wts["w_ple_gate"], "nn")
    pe = _mm("ple_proj", p, wts["w_ple"], "nn")

    sg = {}
    bg = {}
    dpe, dgl, dh2, d_w, loss_acc = _rowwise_bwd(
        "loss", _st_loss, [pe, gl, h2, target], [small["ple_norm_post"]], [1.0], [0, 1, 2], [BF16, BF16, F32], fwd_sums=(0,))
    sg["ple_norm_post"] = _fold(d_w)
    loss = loss_acc[0, 0]
    bg["w_ple"] = _mm("d_w_ple", p, dpe, "tn", out_dtype=BF16)
    bg["w_ple_gate"] = _mm("d_w_ple_gate", a3, dgl, "tn", out_dtype=BF16)
    da3 = _mm("d_a3", dgl, wts["w_ple_gate"], "nt")

    dh1, ddn, d_post, d_pre = _rowwise_bwd(
        "res2_bwd", _st_res_norm, [h1, dn], [small["ffn_norm_post"], small["ple_norm_pre"]], [dh2, da3], [0, 1], [F32, BF16])
    sg["ffn_norm_post"], sg["ple_norm_pre"] = _fold(d_post), _fold(d_pre)
    bg["w_down"] = _mm("d_w_down", act, ddn, "tn", out_dtype=BF16)
    dact = _mm("d_act", ddn, wts["w_down"], "nt")
    dgt, dup = _rowwise_bwd("swiglu_bwd", _st_swiglu, [gt, up], [], [dact], [0, 1], [BF16, BF16])
    bg["w_gate"] = _mm("d_w_gate", f, dgt, "tn", out_dtype=BF16)
    bg["w_up"] = _mm("d_w_up", f, dup, "tn", out_dtype=BF16)
    df = _mm("d_f_gate", dgt, wts["w_gate"], "nt")
    df = _mm("d_f_up", dup, wts["w_up"], "nt", acc_in=df)

    dx_res, dmo, d_post, d_pre = _rowwise_bwd(
        "res1_bwd", _st_res_norm, [x, mo], [small["mix_norm_post"], small["ffn_norm_pre"]], [dh1, df], [0, 1], [F32, BF16])
    sg["mix_norm_post"], sg["ffn_norm_pre"] = _fold(d_post), _fold(d_pre)
    bg["w_out"] = _mm("d_w_out", mixed, dmo, "tn", out_dtype=BF16)
    dmixed = _mm("d_mixed", dmo, wts["w_out"], "nt")
    dg, dao, dso = _rowwise_bwd("mix_bwd", mix_fn, [g, ao, so], [], [dmixed], [0, 1, 2], [BF16, BF16, BF16])
    bg["w_attn_o"] = _mm("d_w_attn_o", attn, dao, "tn", out_dtype=BF16)
    bg["w_ssm_o"] = _mm("d_w_ssm_o", ssm, dso, "tn", out_dtype=BF16)
    dattn = _mm("d_attn", dao, wts["w_attn_o"], "nt", out_dtype=BF16)
    dssm = _mm("d_ssm", dso, wts["w_ssm_o"], "nt")

    dy, dxs_a, dz, d_dskip, d_ssmn = _rowwise_bwd(
        "gated_norm_bwd", gn_fn, [y, xs, z], [dskip_rep, small["ssm_norm"]], [dssm], [0, 1, 2], [F32, F32, BF16])
    sg["d_skip"] = _fold(d_dskip).reshape(NHS, SSM_HEADDIM).sum(axis=1).reshape(1, NHS)
    sg["ssm_norm"] = _fold(d_ssmn)
    dxs_b_t, d_b, d_c, ddt_row, dcum_row = _ssd_bwd_t(xs_t, xbc_a, dt_row, da_col, da_row, states, dy.T, DI, G, R)
    dxs_b = dxs_b_t.T

    def from_row(t):
        return _pad_cols(t[:, :R, :].transpose(2, 0, 1).reshape(S, NHS), LANES)

    ddtraw, d_bias, d_alog = _rowwise("dt_bwd", _st_dt_bwd, [sm, from_row(ddt_row), from_row(dcum_row)],
                                      [dt_bias_p, a_log_p], [(LANES, F32)], accs=(LANES, LANES), tr=L)
    sg["dt_bias"], sg["a_log"] = _fold(d_bias)[:, :NHS], _fold(d_alog)[:, :NHS]
    dconv, d_cb = _rowwise("dconv", _st_dconv(DI), [xbc_c, dxs_a, dxs_b, d_b, d_c], [], [(CONV, F32)], accs=(CONV,))
    sg["conv_b"] = _fold(d_cb)
    dxbc, d_cw = _conv_bwd(xbc, dconv, conv_w)
    d_conv_w = d_cw.reshape(CONV_WIDTH, SUBLANES, CONV).sum(axis=1)

    (delta,) = _rowwise("attn_delta", _st_delta(NH), [dattn, attn], [], [(LANES, F32)])
    delta = delta[:, :NH].T
    dq = _flash_bwd_dq(q, k, v, dattn, lse, delta.reshape(NH, S, 1), NH)
    late_sent = [_by_device(n, bg.pop(n)) for n in late_names]
    dk, dv, late_recv = _flash_bwd_dkv(q, k, v, dattn, lse.reshape(NH, 1, S), delta.reshape(NH, 1, S), NH,
                                       carry=_scatter_carry(late_sent) if late_names else None)
    dqraw, dkr, dk = _rowwise("rope_bwd", _st_rope_bwd(NH), [dq, dk, c_tab, a_tab, b_tab], [],
                              [(NH * QK_PAD, BF16), (QK_PAD, F32), (NH * QK_PAD, BF16)])
    d_w_q = _mm("d_w_q", cq_n, dqraw, "tn", out_dtype=BF16)
    d_w_k = _mm("d_w_k", ckv_n, dk, "tn", out_dtype=BF16)
    d_w_v = _mm("d_w_v", ckv_n, dv, "tn", out_dtype=BF16)
    dcq_n = _mm("d_cq_n", dqraw, w_q, "nt")
    dckv_n = _mm("d_ckv_n_k", dk, w_k, "nt")
    dckv_n = _mm("d_ckv_n_v", dv, w_v, "nt", acc_in=dckv_n)
    bg["w_uq"] = d_w_q.reshape(DQ, NH, QK_PAD)[:, :, :QK_NOPE + QK_ROPE].reshape(DQ, NH * (QK_NOPE + QK_ROPE))
    bg["w_ukv"] = jnp.concatenate([d_w_k.reshape(DKV, NH, QK_PAD)[:, :, :QK_NOPE], d_w_v.reshape(DKV, NH, V_DIM)],
                                  axis=2).reshape(DKV, NH * (QK_NOPE + V_DIM))
    dcqkv, d_qn, d_kvn = _rowwise_bwd("lora_norms_bwd", lora_fn, [cqkv], [small["q_norm"], small["kv_norm"]],
                                      [dcq_n, dckv_n], [0], [BF16])
    sg["q_norm"], sg["kv_norm"] = _fold(d_qn), _fold(d_kvn)

    dsm = jnp.concatenate([dkr, ddtraw], axis=1)
    d_w_cqkv = _mm("d_w_cqkv", u, dcqkv, "tn", out_dtype=BF16)
    d_w_z = _mm("d_w_z", u, dz, "tn", out_dtype=BF16)
    d_w_xbc = _mm("d_w_xbc", u, dxbc, "tn", out_dtype=BF16)
    d_w_g = _mm("d_w_gates", u, dg, "tn", out_dtype=BF16)
    d_w_sm = _mm("d_w_small", u, dsm, "tn", out_dtype=BF16)
    bg["w_in"] = jnp.concatenate(
        [d_w_cqkv, d_w_sm[:, QK_NOPE:QK_NOPE + QK_ROPE], d_w_z, d_w_xbc, d_w_sm[:, QK_PAD:QK_PAD + NHS], d_w_g], axis=1)
    du = _mm("d_u_cqkv", dcqkv, w_cqkv, "nt")
    du = _mm("d_u_z", dz, w_z, "nt", acc_in=du)
    du = _mm("d_u_xbc", dxbc, w_xbc, "nt", acc_in=du)
    du = _mm("d_u_gates", dg, w_g, "nt", acc_in=du)
    du = _mm("d_u_small", dsm, w_sm, "nt", acc_in=du)
    grad_x, d_pre = _rowwise_bwd("pre_norm_bwd", _st_pre, [x], [small["mix_norm_pre"]], [du], [0], [F32], adds={0: dx_res})
    sg["mix_norm_pre"] = _fold(d_pre)
    return loss, grad_x, sg, d_conv_w, bg, late_sent, late_recv


_HBM = pl.BlockSpec(memory_space=pltpu.HBM)
_FLIPS = ((0, 0, 1), (1, 0, 0), (0, 1, 0), (1, 1, 0), (1, 0, 1), (0, 1, 1), (1, 1, 1))


def _place():
    return lax.axis_index("x"), lax.axis_index("y"), lax.axis_index("c")


def _flipped(place, flip):
    return tuple(1 - v if f else v for v, f in zip(place, flip))


def _gather_carry(blocks):
    nw = len(blocks)

    def copies(ins, outs, send_sems, recv_sems, local_sems):
        x, y, c = _place()
        me = 4 * x + 2 * y + c
        cps = []
        for w in range(nw):
            cps.append(pltpu.make_async_copy(ins[w], outs[w].at[me], local_sems.at[w]))
            for k, flip in enumerate(_FLIPS):
                cps.append(pltpu.make_async_remote_copy(
                    src_ref=ins[w], dst_ref=outs[w].at[me], send_sem=send_sems.at[7 * w + k],
                    recv_sem=recv_sems.at[7 * w + k], device_id=_flipped((x, y, c), flip), device_id_type=MESH))
        return cps

    return dict(ins=blocks, outs=[jax.ShapeDtypeStruct((N_DEV,) + b.shape, b.dtype) for b in blocks],
                n_remote=7 * nw, n_local=nw, copies=copies)


def _scatter_carry(by_dev):
    nw = len(by_dev)

    def copies(ins, outs, send_sems, recv_sems, local_sems):
        x, y, c = _place()
        cps = []
        for w in range(nw):
            for k, flip in enumerate(_FLIPS):
                px, py, pc = _flipped((x, y, c), flip)
                cps.append(pltpu.make_async_remote_copy(
                    src_ref=ins[w].at[4 * px + 2 * py + pc], dst_ref=outs[w].at[k], send_sem=send_sems.at[7 * w + k],
                    recv_sem=recv_sems.at[7 * w + k], device_id=(px, py, pc), device_id_type=MESH))
        return cps

    return dict(ins=by_dev, outs=[jax.ShapeDtypeStruct((7,) + b.shape[1:], b.dtype) for b in by_dev],
                n_remote=7 * nw, n_local=0, copies=copies)


def _all_gather(name, blocks):
    nw = len(blocks)

    def body(*refs):
        x_refs, out_refs = refs[:nw], refs[nw:2 * nw]
        send_sems, recv_sems, local_sems = refs[2 * nw:]
        x, y, c = _place()
        me, sibling = (x, y, c), (x, y, 1 - c)
        chips = [(1 - x, y), (x, 1 - y), (1 - x, 1 - y)]

        def slot(w, px, py, pc):
            return out_refs[w].at[4 * px + 2 * py + pc]

        def copy(w, k, blk, to, src=None):
            return pltpu.make_async_remote_copy(
                src_ref=slot(w, *blk) if src is None else src, dst_ref=slot(w, *blk),
                send_sem=send_sems.at[7 * w + k], recv_sem=recv_sems.at[7 * w + k], device_id=to, device_id_type=MESH)

        mine = [pltpu.make_async_copy(x_refs[w], slot(w, *me), local_sems.at[w]) for w in range(nw)]
        for cp in mine:
            cp.start()
        first = []
        for w in range(nw):
            first.append(copy(w, 0, me, sibling, src=x_refs[w]))
            first += [copy(w, 1 + j, me, (*chip, c), src=x_refs[w]) for j, chip in enumerate(chips)]
        for cp in first:
            cp.start()
        passed = []
        for j, chip in enumerate(chips):
            for w in range(nw):
                copy(w, 1 + j, (*chip, c), me).wait_recv()
                passed.append(copy(w, 4 + j, (*chip, c), sibling))
                passed[-1].start()
        for w in range(nw):
            copy(w, 0, sibling, me).wait_recv()
        for j, chip in enumerate(chips):
            for w in range(nw):
                copy(w, 4 + j, (*chip, 1 - c), me).wait_recv()
        for cp in first + passed:
            cp.wait_send()
        for cp in mine:
            cp.wait()

    return pl.pallas_call(
        body, name=name, out_shape=[jax.ShapeDtypeStruct((N_DEV,) + b.shape, b.dtype) for b in blocks],
        in_specs=[_HBM] * nw, out_specs=[_HBM] * nw,
        scratch_shapes=[pltpu.SemaphoreType.DMA((7 * nw,)), pltpu.SemaphoreType.DMA((7 * nw,)),
                        pltpu.SemaphoreType.DMA((nw,))],
    )(*blocks)


def _swap_with_sibling(name, by_dev):
    nw = len(by_dev)

    def body(*refs):
        g_refs, got_refs = refs[:nw], refs[nw:2 * nw]
        send_sems, recv_sems = refs[2 * nw:]
        x, y, c = _place()
        cps = [pltpu.make_async_remote_copy(src_ref=g_refs[w].at[2 * q + (1 - c)], dst_ref=got_refs[w].at[q],
                                            send_sem=send_sems.at[4 * w + q], recv_sem=recv_sems.at[4 * w + q],
                                            device_id=(x, y, 1 - c), device_id_type=MESH)
               for w in range(nw) for q in range(4)]
        for cp in cps:
            cp.start()
        for cp in cps:
            cp.wait()

    return pl.pallas_call(
        body, name=name, out_shape=[jax.ShapeDtypeStruct((4,) + b.shape[1:], b.dtype) for b in by_dev],
        in_specs=[_HBM] * nw, out_specs=[_HBM] * nw,
        scratch_shapes=[pltpu.SemaphoreType.DMA((4 * nw,)), pltpu.SemaphoreType.DMA((4 * nw,))],
    )(*by_dev)


def _swap_with_chips(name, sends):
    nw = len(sends)

    def body(*refs):
        s_refs, r_refs = refs[:nw], refs[nw:2 * nw]
        send_sems, recv_sems = refs[2 * nw:]
        x, y, c = _place()
        chips = [(1 - x, y), (x, 1 - y), (1 - x, 1 - y)]
        cps = [pltpu.make_async_remote_copy(src_ref=s_refs[w].at[2 * px + py], dst_ref=r_refs[w].at[k],
                                            send_sem=send_sems.at[3 * w + k], recv_sem=recv_sems.at[3 * w + k],
                                            device_id=(px, py, c), device_id_type=MESH)
               for w in range(nw) for k, (px, py) in enumerate(chips)]
        for cp in cps:
            cp.start()
        for cp in cps:
            cp.wait()

    return pl.pallas_call(
        body, name=name, out_shape=[jax.ShapeDtypeStruct((3,) + s.shape[1:], s.dtype) for s in sends],
        in_specs=[_HBM] * nw, out_specs=[_HBM] * nw,
        scratch_shapes=[pltpu.SemaphoreType.DMA((3 * nw,)), pltpu.SemaphoreType.DMA((3 * nw,))],
    )(*sends)


def _lane_pad(n):
    return -(-n // LANES) * LANES


def _pack_small(vecs, mat):
    width = max(sum(_lane_pad(v.shape[1]) for v in vecs), _lane_pad(mat.shape[1]))
    row0 = jnp.concatenate([_pad_cols(v, _lane_pad(v.shape[1])) for v in vecs], axis=1)
    rows = jnp.concatenate([_pad_cols(row0, width), _pad_cols(mat, width)], axis=0)
    return jnp.pad(rows, ((0, SUBLANES - rows.shape[0]), (0, 0)))


def _unpack_small(packed, sizes, mat_cols):
    vecs, off = [], 0
    for n in sizes:
        vecs.append(packed[0:1, off:off + n])
        off += _lane_pad(n)
    return vecs, packed[1:1 + CONV_WIDTH, :mat_cols]


def _adamw(w, g, m, v):
    m = ADAM_B1 * m + (1.0 - ADAM_B1) * g
    v = ADAM_B2 * v + (1.0 - ADAM_B2) * (g * g)
    m_hat = m / (1.0 - ADAM_B1 ** ADAM_STEP)
    v_hat = v / (1.0 - ADAM_B2 ** ADAM_STEP)
    delta = -ADAM_LR * (m_hat / (jnp.sqrt(v_hat) + ADAM_EPS) + ADAM_WD * w)
    return delta, m, v


BIG = ("w_in", "w_uq", "w_ukv", "w_attn_o", "w_ssm_o", "w_out", "w_gate", "w_up", "w_down", "w_ple_gate", "w_ple")
FIRST = ("w_in", "w_uq", "w_ukv")
LATE = ("w_attn_o", "w_ssm_o", "w_out", "w_gate", "w_up", "w_down", "w_ple_gate", "w_ple")
COL_SHARDED = ("w_in", "w_uq", "w_ukv", "w_gate", "w_up", "w_ple")
SMALL = ("mix_norm_pre", "mix_norm_post", "q_norm", "kv_norm", "conv_b", "dt_bias", "a_log", "d_skip", "ssm_norm",
         "ffn_norm_pre", "ffn_norm_post", "ple_norm_pre", "ple_norm_post")
WEIGHTS = ("mix_norm_pre", "mix_norm_post", "w_in", "q_norm", "w_uq", "kv_norm", "w_ukv", "conv_w", "conv_b", "dt_bias",
           "a_log", "d_skip", "ssm_norm", "w_attn_o", "w_ssm_o", "w_out", "ffn_norm_pre", "ffn_norm_post", "w_gate",
           "w_up", "w_down", "ple_norm_pre", "ple_norm_post", "w_ple_gate", "w_ple")


def _step(x, p, positions, target, w, m, v):
    xi, yi, ci = _place()
    me = 4 * xi + 2 * yi + ci
    chip = 2 * xi + yi

    gathered = _all_gather("gather_weights", [w[n].astype(BF16) for n in FIRST])
    wts = {n: _assemble(n, blocks) for n, blocks in zip(FIRST, gathered)}
    cw_rows, cw_cols = w["conv_w"].shape
    (cw_all,) = _all_gather("gather_conv_w", [jnp.pad(w["conv_w"], ((0, SUBLANES - cw_rows), (0, 0)))])
    conv_w = cw_all[:, :cw_rows, :].transpose(1, 0, 2).reshape(cw_rows, N_DEV * cw_cols)

    small = {n: w[n] for n in SMALL}
    loss, grad_x, sg, d_conv_w, bg, late_sent, late_recv = _local_step(
        x, p, positions, target, small, conv_w, wts, LATE, [w[n].astype(BF16) for n in LATE])

    sizes = [w[n].shape[1] for n in SMALL]
    sg_pack = _pack_small([sg[n] for n in SMALL], d_conv_w)
    (sg_all,) = _all_gather("gather_small_grads", [sg_pack])
    (sg_sum,) = _rowwise("sum_small_grads", lambda *a: functools.reduce(lambda s, t: s + t, a),
                         [sg_all[k] for k in range(N_DEV)], [], [(sg_pack.shape[1], F32)])
    sg_vecs, d_conv_w_sum = _unpack_small(sg_sum, sizes, d_conv_w.shape[1])
    grads = dict(zip(SMALL, sg_vecs))
    grads["conv_w"] = lax.dynamic_slice_in_dim(d_conv_w_sum, me * cw_cols, cw_cols, axis=1)

    by_dev = [_by_device(n, bg[n]) for n in FIRST]
    gots = _swap_with_sibling("rs_within_chip", by_dev)
    chip_f32, chip_bf16 = [], []
    for n, b, got in zip(FIRST, by_dev, gots):
        rows, cols = w[n].shape
        keep = lax.dynamic_index_in_dim(b.reshape(4, 2, rows, cols), ci, axis=1, keepdims=False)
        s32, s16 = _rowwise("rs_add_sibling_" + n, lambda a, b_: (a + b_, a + b_),
                            [keep.reshape(4 * rows, cols), got.reshape(4 * rows, cols)], [], [(cols, F32), (cols, BF16)])
        chip_f32.append(s32.reshape(4, rows, cols))
        chip_bf16.append(s16.reshape(4, rows, cols))
    recvs = _swap_with_chips("rs_between_chips", chip_bf16)

    def sum_then_adamw(wv, mv, vv, own, r0, r1, r2):
        g = ((own + r0) + r1) + r2
        return (g,) + _adamw(wv, g, mv, vv)

    def sum8_then_adamw(wv, mv, vv, own, *others):
        g = functools.reduce(lambda s, t: s + t, others, own)
        return (g,) + _adamw(wv, g, mv, vv)

    delta, new_m, new_v = {}, {}, {}
    for n, c32, recv in zip(FIRST, chip_f32, recvs):
        cols = w[n].shape[1]
        own = lax.dynamic_index_in_dim(c32, chip, axis=0, keepdims=False)
        grads[n], delta[n], new_m[n], new_v[n] = _rowwise(
            "adamw_" + n, sum_then_adamw, [w[n], m[n], v[n], own, recv[0], recv[1], recv[2]], [], [(cols, F32)] * 4)
    for n, sent, recv in zip(LATE, late_sent, late_recv):
        cols = w[n].shape[1]
        own = lax.dynamic_index_in_dim(sent, me, axis=0, keepdims=False)
        grads[n], delta[n], new_m[n], new_v[n] = _rowwise(
            "adamw_" + n, sum8_then_adamw, [w[n], m[n], v[n], own] + [recv[k] for k in range(N_DEV - 1)], [],
            [(cols, F32)] * 4)
    packed = [_pack_small([d[n] for n in SMALL], d["conv_w"]) for d in (w, grads, m, v)]
    outs = _rowwise("adamw_small", _adamw, packed, [], [(packed[0].shape[1], F32)] * 3)
    for d, o in zip((delta, new_m, new_v), outs):
        vecs, mat = _unpack_small(o, sizes, cw_cols)
        d.update(zip(SMALL, vecs))
        d["conv_w"] = mat
    return loss, grad_x, grads, delta, new_m, new_v


def kernel(x, p, positions, mix_norm_pre, mix_norm_post, w_in, q_norm, w_uq, kv_norm, w_ukv, conv_w, conv_b, dt_bias, a_log, d_skip, ssm_norm, w_attn_o, w_ssm_o, w_out, ffn_norm_pre, ffn_norm_post, w_gate, w_up, w_down, ple_norm_pre, ple_norm_post, w_ple_gate, w_ple, loss_target, m_mix_norm_pre, m_mix_norm_post, m_w_in, m_q_norm, m_w_uq, m_kv_norm, m_w_ukv, m_conv_w, m_conv_b, m_dt_bias, m_a_log, m_d_skip, m_ssm_norm, m_w_attn_o, m_w_ssm_o, m_w_out, m_ffn_norm_pre, m_ffn_norm_post, m_w_gate, m_w_up, m_w_down, m_ple_norm_pre, m_ple_norm_post, m_w_ple_gate, m_w_ple, v_mix_norm_pre, v_mix_norm_post, v_w_in, v_q_norm, v_w_uq, v_kv_norm, v_w_ukv, v_conv_w, v_conv_b, v_dt_bias, v_a_log, v_d_skip, v_ssm_norm, v_w_attn_o, v_w_ssm_o, v_w_out, v_ffn_norm_pre, v_ffn_norm_post, v_w_gate, v_w_up, v_w_down, v_ple_norm_pre, v_ple_norm_post, v_w_ple_gate, v_w_ple):
    w_args = (mix_norm_pre, mix_norm_post, w_in, q_norm, w_uq, kv_norm, w_ukv, conv_w, conv_b, dt_bias, a_log, d_skip, ssm_norm, w_attn_o, w_ssm_o, w_out, ffn_norm_pre, ffn_norm_post, w_gate, w_up, w_down, ple_norm_pre, ple_norm_post, w_ple_gate, w_ple)
    m_args = (m_mix_norm_pre, m_mix_norm_post, m_w_in, m_q_norm, m_w_uq, m_kv_norm, m_w_ukv, m_conv_w, m_conv_b, m_dt_bias, m_a_log, m_d_skip, m_ssm_norm, m_w_attn_o, m_w_ssm_o, m_w_out, m_ffn_norm_pre, m_ffn_norm_post, m_w_gate, m_w_up, m_w_down, m_ple_norm_pre, m_ple_norm_post, m_w_ple_gate, m_w_ple)
    v_args = (v_mix_norm_pre, v_mix_norm_post, v_w_in, v_q_norm, v_w_uq, v_kv_norm, v_w_ukv, v_conv_w, v_conv_b, v_dt_bias, v_a_log, v_d_skip, v_ssm_norm, v_w_attn_o, v_w_ssm_o, v_w_out, v_ffn_norm_pre, v_ffn_norm_post, v_w_gate, v_w_up, v_w_down, v_ple_norm_pre, v_ple_norm_post, v_w_ple_gate, v_w_ple)

    def drop_layer(a):
        return a if a.ndim == 2 else a[0]

    w = {n: drop_layer(a) for n, a in zip(WEIGHTS, w_args)}
    m = {n: drop_layer(a) for n, a in zip(WEIGHTS, m_args)}
    v = {n: drop_layer(a) for n, a in zip(WEIGHTS, v_args)}
    loss, grad_x, grads, delta, new_m, new_v = _step(x[0], p[0, 0], positions[0], loss_target[0], w, m, v)
    loss = lax.psum(loss, ("x", "y", "c"))
    like = lambda d: [d[n].reshape(a.shape) for n, a in zip(WEIGHTS, w_args)]
    return (loss, grad_x[None], *like(grads), *like(delta), *like(new_m), *like(new_v))
```

```python
import functools

import jax
import jax.numpy as jnp
from jax import lax
from jax.experimental import pallas as pl
from jax.experimental.pallas import tpu as pltpu

F32 = jnp.float32
BF16 = jnp.bfloat16

EPS = 1e-6
QK_NOPE = 128
QK_ROPE = 64
V_DIM = 128
QK_PAD = 256
ROPE_THETA = 10000.0
SSM_HEADDIM = 64
D_STATE = 128
CONV_WIDTH = 4
CHUNK = 256
ADAM_LR = 0.001
ADAM_B1 = 0.9
ADAM_B2 = 0.999
ADAM_EPS = 1e-08
ADAM_WD = 0.01
ADAM_STEP = 10

N_DEV = 8
LANES = 128
SUBLANES = 8
PACK_W = 1024
VMEM_LIMIT = 56 * 1024 * 1024
ROW_TILE_BYTES = 6 * 1024 * 1024
FLASH_T = 512
MM_TILE_BYTES = 20 * 1024 * 1024
MESH = pl.DeviceIdType.MESH


def _pick(dim, prefs):
    if dim <= prefs[0]:
        return dim
    for p in prefs:
        if dim % p == 0:
            return p
    return dim


def _tile(dim, cap):
    if dim <= cap:
        return dim
    best = None
    for t in range(LANES, cap + 1, LANES):
        if dim % t == 0:
            best = t
    return best if best is not None else dim


def _params(sem):
    return pltpu.CompilerParams(dimension_semantics=sem, vmem_limit_bytes=VMEM_LIMIT)


def _dot(a, b):
    return lax.dot_general(a, b, (((1,), (0,)), ((), ())), preferred_element_type=F32)


def _dot_nt(a, b):
    return lax.dot_general(a, b, (((1,), (1,)), ((), ())), preferred_element_type=F32)


def _dot_tn(a, b):
    return lax.dot_general(a, b, (((0,), (0,)), ((), ())), preferred_element_type=F32)


def _mm(name, a, b, mode, out_dtype=F32, acc_in=None):
    if mode == "nn":
        (M, K), (K2, N) = a.shape, b.shape
    elif mode == "nt":
        (M, K), (N, K2) = a.shape, b.shape
    else:
        (K, M), (K2, N) = a.shape, b.shape
    assert K == K2, (name, a.shape, b.shape, mode)
    tm = _tile(M, 1024)
    tn = _tile(N, 1024 if acc_in is not None else 1536)
    tk = _tile(K, 2048)
    while tk > 512 and 2 * (tm * tk * a.dtype.itemsize + tk * tn * b.dtype.itemsize) > MM_TILE_BYTES:
        tk = _tile(K, tk - LANES)
    nk = K // tk
    dot = {"nn": _dot, "nt": _dot_nt, "tn": _dot_tn}[mode]
    has_acc = acc_in is not None

    def body(*refs):
        if has_acc:
            a_ref, b_ref, c_ref, o_ref, acc = refs
        else:
            a_ref, b_ref, o_ref, acc = refs
        k = pl.program_id(2)

        @pl.when(k == 0)
        def _():
            acc[...] = jnp.zeros_like(acc)

        acc[...] += dot(a_ref[...].astype(BF16), b_ref[...].astype(BF16))

        @pl.when(k == nk - 1)
        def _():
            r = acc[...]
            if has_acc:
                r = r + c_ref[...]
            o_ref[...] = r.astype(o_ref.dtype)

    if mode == "tn":
        a_spec = pl.BlockSpec((tk, tm), lambda i, j, k: (k, i))
    else:
        a_spec = pl.BlockSpec((tm, tk), lambda i, j, k: (i, k))
    if mode == "nt":
        b_spec = pl.BlockSpec((tn, tk), lambda i, j, k: (j, k))
    else:
        b_spec = pl.BlockSpec((tk, tn), lambda i, j, k: (k, j))
    o_spec = pl.BlockSpec((tm, tn), lambda i, j, k: (i, j))
    in_specs = [a_spec, b_spec] + ([o_spec] if has_acc else [])
    args = (a, b) + ((acc_in,) if has_acc else ())
    return pl.pallas_call(
        body, name=name, grid=(M // tm, N // tn, nk), in_specs=in_specs, out_specs=o_spec,
        out_shape=jax.ShapeDtypeStruct((M, N), out_dtype), scratch_shapes=[pltpu.VMEM((tm, tn), F32)],
        input_output_aliases=({2: 0} if has_acc and out_dtype == F32 else {}),
        compiler_params=_params(("parallel", "parallel", "arbitrary")),
    )(*args)


def _row_tile(n_rows, bytes_per_row):
    tr = 512
    while tr > SUBLANES and tr * bytes_per_row > ROW_TILE_BYTES:
        tr //= 2
    while n_rows % tr:
        tr //= 2
    return tr


def _acc_add(a_ref, v):
    if v.shape[0] == 1:
        a_ref[0:1, :] += v
    else:
        a_ref[...] += v.reshape(v.shape[0] // SUBLANES, SUBLANES, v.shape[1]).sum(axis=0)


def _rowwise(name, fn, rows, bcs, outs, accs=(), tr=None):
    n_rows = rows[0].shape[0]
    if tr is None:
        per_row = sum(r.shape[1] * r.dtype.itemsize for r in rows) + sum(w * jnp.dtype(d).itemsize for w, d in outs)
        tr = _row_tile(n_rows, per_row)
    n_r, n_b, n_o, n_a = len(rows), len(bcs), len(outs), len(accs)

    def body(*refs):
        ins = [r[...].astype(F32) for r in refs[: n_r + n_b]]
        res = fn(*ins)
        res = res if isinstance(res, (tuple, list)) else (res,)
        o_refs = refs[n_r + n_b: n_r + n_b + n_o]
        a_refs = refs[n_r + n_b + n_o:]
        for o, v in zip(o_refs, res[:n_o]):
            o[...] = v.astype(o.dtype)
        if n_a:
            @pl.when(pl.program_id(0) == 0)
            def _():
                for a in a_refs:
                    a[...] = jnp.zeros_like(a)

            for a, v in zip(a_refs, res[n_o:]):
                _acc_add(a, v)

    in_specs = [pl.BlockSpec((tr, r.shape[1]), lambda i: (i, 0)) for r in rows]
    in_specs += [pl.BlockSpec((1, b.shape[1]), lambda i: (0, 0)) for b in bcs]
    out_specs = [pl.BlockSpec((tr, w), lambda i: (i, 0)) for w, _ in outs]
    out_specs += [pl.BlockSpec((SUBLANES, w), lambda i: (0, 0)) for w in accs]
    out_shape = [jax.ShapeDtypeStruct((n_rows, w), d) for w, d in outs]
    out_shape += [jax.ShapeDtypeStruct((SUBLANES, w), F32) for w in accs]
    res = pl.pallas_call(
        body, name=name, grid=(n_rows // tr,), in_specs=in_specs, out_specs=out_specs, out_shape=out_shape,
        compiler_params=_params(("arbitrary",) if n_a else ("parallel",)),
    )(*rows, *bcs)
    return tuple(res)


def _rowwise_bwd(name, fn, rows, bcs, cts, need_rows, row_dtypes, need_bcs=None, adds=None, fwd_sums=(), tr=None):
    n_rows = rows[0].shape[0]
    adds = adds or {}
    need_bcs = list(range(len(bcs))) if need_bcs is None else list(need_bcs)
    ct_arrays = [c for c in cts if not isinstance(c, float)]
    add_keys = sorted(adds)
    add_arrays = [adds[k] for k in add_keys]
    if tr is None:
        per_row = sum(r.shape[1] * r.dtype.itemsize for r in list(rows) + ct_arrays + add_arrays)
        per_row += sum(rows[i].shape[1] * jnp.dtype(d).itemsize for i, d in zip(need_rows, row_dtypes))
        tr = _row_tile(n_rows, 2 * per_row)
    n_r, n_b, n_c, n_ad = len(rows), len(bcs), len(ct_arrays), len(add_arrays)
    n_go, n_gb, n_fs = len(need_rows), len(need_bcs), len(fwd_sums)

    def body(*refs):
        pos = 0
        r_t = [r[...].astype(F32) for r in refs[pos: pos + n_r]]
        pos += n_r
        b_t = [r[...].astype(F32) for r in refs[pos: pos + n_b]]
        pos += n_b
        c_t = [r[...].astype(F32) for r in refs[pos: pos + n_c]]
        pos += n_c
        ad_t = [r[...].astype(F32) for r in refs[pos: pos + n_ad]]
        pos += n_ad
        go_refs = refs[pos: pos + n_go]
        pos += n_go
        acc_refs = refs[pos:]

        def wrapped(*a):
            r = fn(*a)
            return tuple(r) if isinstance(r, (tuple, list)) else (r,)

        outs, vjp = jax.vjp(wrapped, *r_t, *b_t)
        it = iter(c_t)
        full = tuple(jnp.full(o.shape, c, F32) if isinstance(c, float) else next(it) for o, c in zip(outs, cts))
        grads = vjp(full)
        for o_ref, i in zip(go_refs, need_rows):
            g = grads[i]
            if i in adds:
                g = g + ad_t[add_keys.index(i)]
            o_ref[...] = g.astype(o_ref.dtype)

        @pl.when(pl.program_id(0) == 0)
        def _():
            for a in acc_refs:
                a[...] = jnp.zeros_like(a)

        for a, j in zip(acc_refs[:n_gb], need_bcs):
            _acc_add(a, grads[n_r + j])
        for a, j in zip(acc_refs[n_gb:], fwd_sums):
            a[0:1, :] += jnp.full((1, LANES), jnp.sum(outs[j]), F32)

    def row_spec(w):
        return pl.BlockSpec((tr, w), lambda i: (i, 0))

    in_specs = [row_spec(r.shape[1]) for r in rows]
    in_specs += [pl.BlockSpec((1, b.shape[1]), lambda i: (0, 0)) for b in bcs]
    in_specs += [row_spec(c.shape[1]) for c in ct_arrays] + [row_spec(a.shape[1]) for a in add_arrays]
    out_specs = [row_spec(rows[i].shape[1]) for i in need_rows]
    out_specs += [pl.BlockSpec((SUBLANES, bcs[j].shape[1]), lambda i: (0, 0)) for j in need_bcs]
    out_specs += [pl.BlockSpec((SUBLANES, LANES), lambda i: (0, 0)) for _ in fwd_sums]
    out_shape = [jax.ShapeDtypeStruct((n_rows, rows[i].shape[1]), d) for i, d in zip(need_rows, row_dtypes)]
    out_shape += [jax.ShapeDtypeStruct((SUBLANES, bcs[j].shape[1]), F32) for j in need_bcs]
    out_shape += [jax.ShapeDtypeStruct((SUBLANES, LANES), F32) for _ in fwd_sums]
    res = pl.pallas_call(
        body, name=name, grid=(n_rows // tr,), in_specs=in_specs, out_specs=out_specs, out_shape=out_shape,
        compiler_params=_params(("arbitrary",)),
    )(*rows, *bcs, *ct_arrays, *add_arrays)
    return tuple(res)


def _fold(acc):
    return jnp.sum(acc, axis=0, keepdims=True)


def _rms(x, w):
    return x * lax.rsqrt(jnp.mean(x * x, axis=-1, keepdims=True) + EPS) * w


def _sigmoid(x):
    return jax.nn.sigmoid(x)


def _silu(x):
    return x * _sigmoid(x)


def _log1p(u):
    series = u * (1.0 - u * (0.5 - u * (1.0 / 3.0 - u * 0.25)))
    return jnp.where(u < 0.01, series, jnp.log(1.0 + u))


def _softplus(x):
    return jnp.maximum(x, 0.0) + _log1p(jnp.exp(-jnp.abs(x)))


def _st_pre(x, w):
    return _rms(x, w)


def _st_lora_norms(dq):
    def fn(cqkv, qn, kvn):
        return _rms(cqkv[:, :dq], qn), _rms(cqkv[:, dq:], kvn)
    return fn


def _st_gated_norm(n_groups):
    def fn(y, xs, z, dskip, wn):
        yz = (y + dskip * xs) * _silu(z)
        gw = yz.shape[1] // n_groups
        parts = [_rms(yz[:, g * gw:(g + 1) * gw], wn[:, g * gw:(g + 1) * gw]) for g in range(n_groups)]
        return jnp.concatenate(parts, axis=1)
    return fn


def _st_mix(d):
    def fn(g, ao, so):
        return _sigmoid(g[:, :d]) * ao + _sigmoid(g[:, d:]) * so
    return fn


def _st_res_norm(h, y, w_post, w_pre):
    h2 = h + _rms(y, w_post)
    return h2, _rms(h2, w_pre)


def _st_swiglu(gt, up):
    return _silu(gt) * up


def _st_loss(pe, gl, h2, tgt, w_post):
    e = pe * _sigmoid(gl)
    diff = h2 + _rms(e, w_post) - tgt
    return 0.5 * jnp.mean(diff * diff, axis=-1, keepdims=True)


def _rope_tables(positions):
    half = QK_ROPE // 2
    inv_freq = ROPE_THETA ** (-jnp.arange(0, QK_ROPE, 2, dtype=F32) / QK_ROPE)
    ang = positions.astype(F32).reshape(-1, 1) * inv_freq
    cos, sin = jnp.cos(ang), jnp.sin(ang)
    n = ang.shape[0]
    z = lambda w: jnp.zeros((n, w), F32)
    c_tab = jnp.concatenate([jnp.ones((n, QK_NOPE), F32), cos, cos, z(QK_PAD - QK_NOPE - QK_ROPE)], axis=1)
    a_tab = jnp.concatenate([z(QK_NOPE), -sin, z(half), z(QK_PAD - QK_NOPE - QK_ROPE)], axis=1)
    b_tab = jnp.concatenate([z(QK_NOPE), z(half), sin, z(QK_PAD - QK_NOPE - QK_ROPE)], axis=1)
    return c_tab, a_tab, b_tab


def _rot(x, c, a, b):
    half = QK_ROPE // 2
    return x * c + pltpu.roll(x, QK_PAD - half, axis=1) * a + pltpu.roll(x, half, axis=1) * b


def _rot_t(g, c, a, b):
    half = QK_ROPE // 2
    return g * c + pltpu.roll(g * a, half, axis=1) + pltpu.roll(g * b, QK_PAD - half, axis=1)


def _st_rope(n_heads):
    def fn(qraw, kraw, sm, c, a, b):
        kpe = _rot(sm[:, :QK_PAD], c, a, b)
        scale = float(QK_NOPE + QK_ROPE) ** -0.5
        q = [_rot(qraw[:, h * QK_PAD:(h + 1) * QK_PAD], c, a, b) * scale for h in range(n_heads)]
        k = [kraw[:, h * QK_PAD:(h + 1) * QK_PAD] + kpe for h in range(n_heads)]
        return jnp.concatenate(q, axis=1), jnp.concatenate(k, axis=1)
    return fn


def _st_rope_bwd(n_heads):
    def fn(dq, dk, c, a, b):
        dqraw = [_rot_t(dq[:, h * QK_PAD:(h + 1) * QK_PAD], c, a, b) for h in range(n_heads)]
        dks = dk[:, :QK_PAD]
        for h in range(1, n_heads):
            dks = dks + dk[:, h * QK_PAD:(h + 1) * QK_PAD]
        return jnp.concatenate(dqraw, axis=1), _rot_t(dks, c, a, b), dk
    return fn


def _split3(x):
    h1 = x.astype(BF16)
    r1 = x - h1.astype(F32)
    h2 = r1.astype(BF16)
    h3 = (r1 - h2.astype(F32)).astype(BF16)
    return h1, h2, h3


def _tri_dot(tri, x):
    h1, h2, h3 = _split3(x)
    return (_dot(tri, h3) + _dot(tri, h2)) + _dot(tri, h1)


def _dot_tri(x, tri):
    h1, h2, h3 = _split3(x)
    return (_dot(h3, tri) + _dot(h2, tri)) + _dot(h1, tri)


def _st_dt(sm, bias, alog):
    x = sm[:, QK_PAD:] + bias
    dt = _softplus(x)
    return dt, dt * (-jnp.exp(alog))


def _st_dt_bwd(sm, ddt, dcum, bias, alog):
    n = sm.shape[0]
    i = lax.broadcasted_iota(jnp.int32, (n, n), 0)
    j = lax.broadcasted_iota(jnp.int32, (n, n), 1)
    upper = (j >= i).astype(BF16)
    dda = _tri_dot(upper, dcum)
    x = sm[:, QK_PAD:] + bias
    dt = _softplus(x)
    a = -jnp.exp(alog)
    draw = (ddt + dda * a) * _sigmoid(x)
    return draw, draw, dda * dt * a


def _conv_fwd(xbc, w, b):
    S, C = xbc.shape
    tr = _pick(S, (512, 256))
    tc = _pick(C, (1024, 512, 256, 128))
    hb = tr // SUBLANES

    def body(x_ref, halo_ref, w_ref, b_ref, c_ref, a_ref, ext):
        i = pl.program_id(1)
        halo = jnp.where(i == 0, 0.0, halo_ref[...])
        ext[0:SUBLANES, :] = halo
        ext[SUBLANES:, :] = x_ref[...]
        wv = w_ref[...]
        acc = b_ref[...] + wv[CONV_WIDTH - 1:CONV_WIDTH, :] * x_ref[...]
        for k in range(CONV_WIDTH - 1):
            off = SUBLANES - (CONV_WIDTH - 1) + k
            acc = acc + wv[k:k + 1, :] * ext[pl.ds(off, tr), :]
        c_ref[...] = acc
        a_ref[...] = _silu(acc)

    return pl.pallas_call(
        body, name="conv_fwd", grid=(C // tc, S // tr),
        in_specs=[pl.BlockSpec((tr, tc), lambda j, i: (i, j)),
                  pl.BlockSpec((SUBLANES, tc), lambda j, i: (jnp.maximum(i * hb - 1, 0), j)),
                  pl.BlockSpec((CONV_WIDTH, tc), lambda j, i: (0, j)),
                  pl.BlockSpec((1, tc), lambda j, i: (0, j))],
        out_specs=[pl.BlockSpec((tr, tc), lambda j, i: (i, j))] * 2,
        out_shape=[jax.ShapeDtypeStruct((S, C), F32)] * 2,
        scratch_shapes=[pltpu.VMEM((tr + SUBLANES, tc), F32)],
        compiler_params=_params(("parallel", "arbitrary")),
    )(xbc, xbc, w, b)


def _conv_bwd(xbc, dconv, w):
    S, C = xbc.shape
    tr = _pick(S, (512, 256))
    tc = _pick(C, (1024, 512, 256, 128))
    hb = tr // SUBLANES
    n_i = S // tr

    def body(x_ref, halo_ref, d_ref, dnext_ref, w_ref, dx_ref, dw_ref, ext, dext):
        i = pl.program_id(1)
        ext[0:SUBLANES, :] = jnp.where(i == 0, 0.0, halo_ref[...])
        ext[SUBLANES:, :] = x_ref[...]
        dext[0:tr, :] = d_ref[...]
        dext[tr:, :] = jnp.where(i == n_i - 1, 0.0, dnext_ref[...])
        wv = w_ref[...]
        d = d_ref[...]

        @pl.when(i == 0)
        def _():
            dw_ref[...] = jnp.zeros_like(dw_ref)

        dx = wv[CONV_WIDTH - 1:CONV_WIDTH, :] * d
        for k in range(CONV_WIDTH):
            if k < CONV_WIDTH - 1:
                dx = dx + wv[k:k + 1, :] * dext[pl.ds(CONV_WIDTH - 1 - k, tr), :]
                xs = ext[pl.ds(SUBLANES - (CONV_WIDTH - 1) + k, tr), :]
            else:
                xs = x_ref[...]
            prod = d * xs
            dw_ref[k * SUBLANES:(k + 1) * SUBLANES, :] += prod.reshape(tr // SUBLANES, SUBLANES, tc).sum(axis=0)
        dx_ref[...] = dx.astype(dx_ref.dtype)

    return pl.pallas_call(
        body, name="conv_bwd", grid=(C // tc, n_i),
        in_specs=[pl.BlockSpec((tr, tc), lambda j, i: (i, j)),
                  pl.BlockSpec((SUBLANES, tc), lambda j, i: (jnp.maximum(i * hb - 1, 0), j)),
                  pl.BlockSpec((tr, tc), lambda j, i: (i, j)),
                  pl.BlockSpec((SUBLANES, tc), lambda j, i: (jnp.minimum((i + 1) * hb, S // SUBLANES - 1), j)),
                  pl.BlockSpec((CONV_WIDTH, tc), lambda j, i: (0, j))],
        out_specs=[pl.BlockSpec((tr, tc), lambda j, i: (i, j)),
                   pl.BlockSpec((CONV_WIDTH * SUBLANES, tc), lambda j, i: (0, j))],
        out_shape=[jax.ShapeDtypeStruct((S, C), BF16), jax.ShapeDtypeStruct((CONV_WIDTH * SUBLANES, C), F32)],
        scratch_shapes=[pltpu.VMEM((tr + SUBLANES, tc), F32), pltpu.VMEM((tr + SUBLANES, tc), F32)],
        compiler_params=_params(("parallel", "arbitrary")),
    )(xbc, xbc, dconv, dconv, w)


def _st_dconv(d_inner):
    def fn(xc, dxa, dxb, db_, dc_):
        s = _sigmoid(xc)
        g = jnp.concatenate([dxa + dxb, db_, dc_], axis=1) * (s * (1.0 + xc * (1.0 - s)))
        return g, g
    return fn


def _chunk_setup(b_ref, c_ref, dac_ref, dar_ref, L):
    ii = lax.broadcasted_iota(jnp.int32, (L, L), 0)
    jj = lax.broadcasted_iota(jnp.int32, (L, L), 1)
    tri = ii >= jj
    cum_c = _tri_dot(tri.astype(BF16), dac_ref[0])
    cum_r = _dot_tri(dar_ref[0], (ii <= jj).astype(BF16))
    bm = b_ref[...].astype(BF16)
    cm = c_ref[...].astype(BF16)
    return tri, cum_c, cum_r, bm, cm, _dot_nt(cm, bm)


def _ssd_specs(d_inner, n_groups, gw, L, rp, chunk_of):
    bb0 = d_inner // D_STATE
    cb0 = bb0 + n_groups
    return [pl.BlockSpec((L, gw), lambda g, c: (chunk_of(c), g)),
            pl.BlockSpec((L, D_STATE), lambda g, c: (chunk_of(c), bb0 + g)),
            pl.BlockSpec((L, D_STATE), lambda g, c: (chunk_of(c), cb0 + g)),
            pl.BlockSpec((1, L, LANES), lambda g, c: (g, chunk_of(c), 0)),
            pl.BlockSpec((1, L, LANES), lambda g, c: (g, chunk_of(c), 0)),
            pl.BlockSpec((1, rp, L), lambda g, c: (g, 0, chunk_of(c)))]


def _ssd_fwd(xbc_a, dt_col, da_col, da_row, d_inner, n_groups, R):
    S = xbc_a.shape[0]
    L = min(CHUNK, S)
    NC = S // L
    P, N = SSM_HEADDIM, D_STATE
    gw = R * P
    rp = da_row.shape[1]

    def body(x_ref, b_ref, c_ref, dt_ref, dac_ref, dar_ref, y_ref, st_ref, state):
        @pl.when(pl.program_id(1) == 0)
        def _():
            state[...] = jnp.zeros_like(state)

        st_ref[0, 0] = state[...]
        tri, cum_c, cum_r, bm, cm, gm = _chunk_setup(b_ref, c_ref, dac_ref, dar_ref, L)
        dt = dt_ref[0]
        for r in range(R):
            cc = cum_c[:, r:r + 1]
            cr = cum_r[r:r + 1, :]
            lam = jnp.exp(jnp.where(tri, cc - cr, -jnp.inf))
            m = (gm * lam).astype(BF16)
            x = x_ref[:, r * P:(r + 1) * P] * dt[:, r:r + 1]
            s_r = state[r * P:(r + 1) * P, :]
            y_off = _dot_nt(cm, s_r.astype(BF16)) * jnp.exp(cc)
            y_ref[:, r * P:(r + 1) * P] = _dot(m, x.astype(BF16)) + y_off
            last = cr[:, L - 1:L]
            xw = (x * jnp.exp(last - cc)).astype(BF16)
            state[r * P:(r + 1) * P, :] = s_r * jnp.exp(last) + _dot_tn(xw, bm)

    return pl.pallas_call(
        body, name="ssd_fwd", grid=(n_groups, NC),
        in_specs=_ssd_specs(d_inner, n_groups, gw, L, rp, lambda c: c),
        out_specs=[pl.BlockSpec((L, gw), lambda g, c: (c, g)),
                   pl.BlockSpec((1, 1, gw, N), lambda g, c: (g, c, 0, 0))],
        out_shape=[jax.ShapeDtypeStruct((S, d_inner), F32), jax.ShapeDtypeStruct((n_groups, NC, gw, N), F32)],
        scratch_shapes=[pltpu.VMEM((gw, N), F32)],
        compiler_params=_params(("parallel", "arbitrary")),
    )(xbc_a, xbc_a, xbc_a, dt_col, da_col, da_row)


def _ssd_bwd(xbc_a, dt_col, da_col, da_row, states, dy, d_inner, n_groups, R):
    S = xbc_a.shape[0]
    L = min(CHUNK, S)
    NC = S // L
    P, N = SSM_HEADDIM, D_STATE
    gw = R * P
    rp = da_row.shape[1]
    rev = lambda c: NC - 1 - c

    def body(x_ref, b_ref, c_ref, dt_ref, dac_ref, dar_ref, st_ref, dy_ref,
             dx_ref, db_ref, dc_ref, ddt_ref, dcc_ref, dcr_ref, dstate):
        @pl.when(pl.program_id(1) == 0)
        def _():
            dstate[...] = jnp.zeros_like(dstate)

        tri, cum_c, cum_r, bm, cm, gm = _chunk_setup(b_ref, c_ref, dac_ref, dar_ref, L)
        dt = dt_ref[0]
        lane = lax.broadcasted_iota(jnp.int32, (L, LANES), 1)
        sub = lax.broadcasted_iota(jnp.int32, (rp, L), 0)
        is_last = lax.broadcasted_iota(jnp.int32, (L, 1), 0) == L - 1
        d_g = jnp.zeros((L, L), F32)
        dc_acc = jnp.zeros((L, N), F32)
        db_acc = jnp.zeros((L, N), F32)
        ddt_out = jnp.zeros((L, LANES), F32)
        dcc_out = jnp.zeros((L, LANES), F32)
        dcr_out = jnp.zeros((rp, L), F32)
        for r in range(R):
            cc = cum_c[:, r:r + 1]
            cr = cum_r[r:r + 1, :]
            lam = jnp.exp(jnp.where(tri, cc - cr, -jnp.inf))
            m = gm * lam
            dtc = dt[:, r:r + 1]
            xh = x_ref[:, r * P:(r + 1) * P]
            x = xh * dtc
            xb = x.astype(BF16)
            d_y = dy_ref[:, r * P:(r + 1) * P]
            d_yb = d_y.astype(BF16)
            s_r = st_ref[0, 0, r * P:(r + 1) * P, :]
            s_rb = s_r.astype(BF16)
            ds_n = dstate[r * P:(r + 1) * P, :]
            ds_nb = ds_n.astype(BF16)
            e = jnp.exp(cc)
            last = cr[:, L - 1:L]
            e_last = jnp.exp(last)
            w = jnp.exp(last - cc)
            d_m = _dot_nt(d_yb, xb)
            d_x = _dot_tn(m.astype(BF16), d_yb)
            d_ye = (d_y * e).astype(BF16)
            dc_acc = dc_acc + _dot(d_ye, s_rb)
            ds_part = _dot_tn(d_ye, cm)
            y_off = _dot_nt(cm, s_rb) * e
            dcum = jnp.sum(d_y * y_off, axis=1, keepdims=True)
            d_xw = _dot_nt(bm, ds_nb)
            d_x = d_x + d_xw * w
            dw_w = jnp.sum(d_xw * x, axis=1, keepdims=True) * w
            db_acc = db_acc + _dot((x * w).astype(BF16), ds_nb)
            d_last = jnp.sum(ds_n * s_r, keepdims=True) * e_last + jnp.sum(dw_w, keepdims=True)
            dcum = dcum - dw_w
            dstate[r * P:(r + 1) * P, :] = e_last * ds_n + ds_part
            d_g = d_g + d_m * lam
            q = d_m * m
            dcum = dcum + jnp.sum(q, axis=1, keepdims=True) + jnp.where(is_last, d_last, 0.0)
            dcum_row = -jnp.sum(q, axis=0, keepdims=True)
            dx_ref[:, r * P:(r + 1) * P] = d_x * dtc
            ddt = jnp.sum(d_x * xh, axis=1, keepdims=True)
            ddt_out = ddt_out + jnp.where(lane == r, ddt, 0.0)
            dcc_out = dcc_out + jnp.where(lane == r, dcum, 0.0)
            dcr_out = dcr_out + jnp.where(sub == r, dcum_row, 0.0)
        d_gb = d_g.astype(BF16)
        dc_ref[...] = dc_acc + _dot(d_gb, bm)
        db_ref[...] = db_acc + _dot_tn(d_gb, cm)
        ddt_ref[0] = ddt_out
        dcc_ref[0] = dcc_out
        dcr_ref[0] = dcr_out

    gn = n_groups * N
    return pl.pallas_call(
        body, name="ssd_bwd", grid=(n_groups, NC),
        in_specs=_ssd_specs(d_inner, n_groups, gw, L, rp, rev) + [
            pl.BlockSpec((1, 1, gw, N), lambda g, c: (g, rev(c), 0, 0)),
            pl.BlockSpec((L, gw), lambda g, c: (rev(c), g))],
        out_specs=[pl.BlockSpec((L, gw), lambda g, c: (rev(c), g)),
                   pl.BlockSpec((L, N), lambda g, c: (rev(c), g)),
                   pl.BlockSpec((L, N), lambda g, c: (rev(c), g)),
                   pl.BlockSpec((1, L, LANES), lambda g, c: (g, rev(c), 0)),
                   pl.BlockSpec((1, L, LANES), lambda g, c: (g, rev(c), 0)),
                   pl.BlockSpec((1, rp, L), lambda g, c: (g, 0, rev(c)))],
        out_shape=[jax.ShapeDtypeStruct((S, d_inner), F32), jax.ShapeDtypeStruct((S, gn), F32),
                   jax.ShapeDtypeStruct((S, gn), F32), jax.ShapeDtypeStruct((n_groups, S, LANES), F32),
                   jax.ShapeDtypeStruct((n_groups, S, LANES), F32), jax.ShapeDtypeStruct((n_groups, rp, S), F32)],
        scratch_shapes=[pltpu.VMEM((gw, N), F32)],
        compiler_params=_params(("parallel", "arbitrary")),
    )(xbc_a, xbc_a, xbc_a, dt_col, da_col, da_row, states, dy)


def _chunk_setup_t(b_ref, c_ref, dac_ref, dar_ref, L):
    ii = lax.broadcasted_iota(jnp.int32, (L, L), 0)
    jj = lax.broadcasted_iota(jnp.int32, (L, L), 1)
    lower = ii >= jj
    upper = ii <= jj
    cum_c = _tri_dot(lower.astype(BF16), dac_ref[0])
    cum_r = _dot_tri(dar_ref[0], upper.astype(BF16))
    bm = b_ref[...].astype(BF16)
    cm = c_ref[...].astype(BF16)
    return lower, upper, cum_c, cum_r, bm, cm


def _ssd_specs_t(d_inner, n_groups, gw, L, rp, chunk_of):
    bb0 = d_inner // D_STATE
    cb0 = bb0 + n_groups
    return [pl.BlockSpec((L, gw), lambda g, c: (chunk_of(c), g)),
            pl.BlockSpec((L, D_STATE), lambda g, c: (chunk_of(c), bb0 + g)),
            pl.BlockSpec((L, D_STATE), lambda g, c: (chunk_of(c), cb0 + g)),
            pl.BlockSpec((1, rp, L), lambda g, c: (g, 0, chunk_of(c))),
            pl.BlockSpec((1, L, LANES), lambda g, c: (g, chunk_of(c), 0)),
            pl.BlockSpec((1, rp, L), lambda g, c: (g, 0, chunk_of(c)))]


def _ssd_fwd_t(xbc_a, dt_row, da_col, da_row, d_inner, n_groups, R):
    S = xbc_a.shape[0]
    L = min(CHUNK, S)
    NC = S // L
    P, N = SSM_HEADDIM, D_STATE
    gw = R * P
    rp = da_row.shape[1]

    def body(x_ref, b_ref, c_ref, dt_ref, dac_ref, dar_ref, y_ref, st_ref, state, y_t):
        @pl.when(pl.program_id(1) == 0)
        def _():
            state[...] = jnp.zeros_like(state)

        st_ref[0, 0] = state[...]
        lower, upper, cum_c, cum_r, bm, cm = _chunk_setup_t(b_ref, c_ref, dac_ref, dar_ref, L)
        gm_t = _dot_nt(bm, cm)
        dt = dt_ref[0]
        x_t = x_ref[...].T
        for r in range(R):
            cc = cum_c[:, r:r + 1]
            cr = cum_r[r:r + 1, :]
            m_t = (gm_t * jnp.exp(jnp.where(upper, cr - cc, -jnp.inf))).astype(BF16)
            x = x_t[r * P:(r + 1) * P, :] * dt[r:r + 1, :]
            s_r = state[r * P:(r + 1) * P, :]
            y_off = _dot_nt(s_r.astype(BF16), cm) * jnp.exp(cr)
            y_t[r * P:(r + 1) * P, :] = _dot(x.astype(BF16), m_t) + y_off
            last = cr[:, L - 1:L]
            xw = (x * jnp.exp(last - cr)).astype(BF16)
            state[r * P:(r + 1) * P, :] = s_r * jnp.exp(last) + _dot(xw, bm)
        y_ref[...] = y_t[...].T

    return pl.pallas_call(
        body, name="ssd_fwd", grid=(n_groups, NC),
        in_specs=_ssd_specs_t(d_inner, n_groups, gw, L, rp, lambda c: c),
        out_specs=[pl.BlockSpec((L, gw), lambda g, c: (c, g)),
                   pl.BlockSpec((1, 1, gw, N), lambda g, c: (g, c, 0, 0))],
        out_shape=[jax.ShapeDtypeStruct((S, d_inner), F32), jax.ShapeDtypeStruct((n_groups, NC, gw, N), F32)],
        scratch_shapes=[pltpu.VMEM((gw, N), F32), pltpu.VMEM((gw, L), F32)],
        compiler_params=_params(("parallel", "arbitrary")),
    )(xbc_a, xbc_a, xbc_a, dt_row, da_col, da_row)


def _ssd_bwd_t(xbc_a, dt_row, da_col, da_row, states, dy, d_inner, n_groups, R):
    S = xbc_a.shape[0]
    L = min(CHUNK, S)
    NC = S // L
    P, N = SSM_HEADDIM, D_STATE
    gw = R * P
    rp = da_row.shape[1]
    rev = lambda c: NC - 1 - c

    def body(x_ref, b_ref, c_ref, dt_ref, dac_ref, dar_ref, st_ref, dy_ref,
             dx_ref, db_ref, dc_ref, ddt_ref, dcum_ref, dstate, dx_t):
        @pl.when(pl.program_id(1) == 0)
        def _():
            dstate[...] = jnp.zeros_like(dstate)

        lower, upper, cum_c, cum_r, bm, cm = _chunk_setup_t(b_ref, c_ref, dac_ref, dar_ref, L)
        gm = _dot_nt(cm, bm)
        gm_t = _dot_nt(bm, cm)
        dt = dt_ref[0]
        x_t = x_ref[...].T
        dy_t = dy_ref[...].T
        sub = lax.broadcasted_iota(jnp.int32, (rp, L), 0)
        is_last = lax.broadcasted_iota(jnp.int32, (1, L), 1) == L - 1
        d_g = jnp.zeros((L, L), F32)
        d_g_t = jnp.zeros((L, L), F32)
        dc_acc = jnp.zeros((L, N), F32)
        db_acc = jnp.zeros((L, N), F32)
        ddt_out = jnp.zeros((rp, L), F32)
        dcum_out = jnp.zeros((rp, L), F32)
        for r in range(R):
            cc = jnp.broadcast_to(cum_c[:, r:r + 1], (L, L))
            cr = cum_r[r:r + 1, :]
            lam = jnp.exp(jnp.where(lower, cc - cum_r[r:r + 1, :], -jnp.inf))
            lam_t = jnp.exp(jnp.where(upper, cr - cc, -jnp.inf))
            m = gm * lam
            m_t = gm_t * lam_t
            dtr = dt[r:r + 1, :]
            xh = x_t[r * P:(r + 1) * P, :]
            x = xh * dtr
            xb = x.astype(BF16)
            d_y = dy_t[r * P:(r + 1) * P, :]
            d_yb = d_y.astype(BF16)
            s_r = st_ref[0, 0, r * P:(r + 1) * P, :]
            s_rb = s_r.astype(BF16)
            ds_n = dstate[r * P:(r + 1) * P, :]
            ds_nb = ds_n.astype(BF16)
            e = jnp.exp(cr)
            last = cr[:, L - 1:L]
            e_last = jnp.exp(last)
            w = jnp.exp(last - cr)
            d_x = _dot(d_yb, m.astype(BF16))
            d_m = _dot_tn(d_yb, xb)
            d_m_t = _dot_tn(xb, d_yb)
            d_ye = (d_y * e).astype(BF16)
            dc_acc = dc_acc + _dot_tn(d_ye, s_rb)
            ds_part = _dot(d_ye, cm)
            y_off = _dot_nt(s_rb, cm) * e
            dcum = jnp.sum(d_y * y_off, axis=0, keepdims=True)
            d_xw = _dot_nt(ds_nb, bm)
            d_x = d_x + d_xw * w
            dw_w = jnp.sum(d_xw * x, axis=0, keepdims=True) * w
            db_acc = db_acc + _dot_tn((x * w).astype(BF16), ds_nb)
            d_last = jnp.sum(ds_n * s_r, keepdims=True) * e_last + jnp.sum(dw_w, keepdims=True)
            dstate[r * P:(r + 1) * P, :] = e_last * ds_n + ds_part
            d_g = d_g + d_m * lam
            d_g_t = d_g_t + d_m_t * lam_t
            dcum = (dcum - dw_w + jnp.sum(d_m_t * m_t, axis=0, keepdims=True)
                    - jnp.sum(d_m * m, axis=0, keepdims=True) + jnp.where(is_last, d_last, 0.0))
            dx_t[r * P:(r + 1) * P, :] = d_x * dtr
            ddt = jnp.sum(d_x * xh, axis=0, keepdims=True)
            ddt_out = ddt_out + jnp.where(sub == r, ddt, 0.0)
            dcum_out = dcum_out + jnp.where(sub == r, dcum, 0.0)
        dc_ref[...] = dc_acc + _dot(d_g.astype(BF16), bm)
        db_ref[...] = db_acc + _dot(d_g_t.astype(BF16), cm)
        dx_ref[...] = dx_t[...].T
        ddt_ref[0] = ddt_out
        dcum_ref[0] = dcum_out

    gn = n_groups * N
    return pl.pallas_call(
        body, name="ssd_bwd", grid=(n_groups, NC),
        in_specs=_ssd_specs_t(d_inner, n_groups, gw, L, rp, rev) + [
            pl.BlockSpec((1, 1, gw, N), lambda g, c: (g, rev(c), 0, 0)),
            pl.BlockSpec((L, gw), lambda g, c: (rev(c), g))],
        out_specs=[pl.BlockSpec((L, gw), lambda g, c: (rev(c), g)),
                   pl.BlockSpec((L, N), lambda g, c: (rev(c), g)),
                   pl.BlockSpec((L, N), lambda g, c: (rev(c), g)),
                   pl.BlockSpec((1, rp, L), lambda g, c: (g, 0, rev(c))),
                   pl.BlockSpec((1, rp, L), lambda g, c: (g, 0, rev(c)))],
        out_shape=[jax.ShapeDtypeStruct((S, d_inner), F32), jax.ShapeDtypeStruct((S, gn), F32),
                   jax.ShapeDtypeStruct((S, gn), F32), jax.ShapeDtypeStruct((n_groups, rp, S), F32),
                   jax.ShapeDtypeStruct((n_groups, rp, S), F32)],
        scratch_shapes=[pltpu.VMEM((gw, N), F32), pltpu.VMEM((gw, L), F32)],
        compiler_params=_params(("parallel", "arbitrary")),
    )(xbc_a, xbc_a, xbc_a, dt_row, da_col, da_row, states, dy)


def _attn_scale():
    return float(QK_NOPE + QK_ROPE) ** -0.5


def _diag_mask(t, keys_first=False):
    rows = lax.broadcasted_iota(jnp.int32, (t, t), 0)
    cols = lax.broadcasted_iota(jnp.int32, (t, t), 1)
    return rows <= cols if keys_first else cols <= rows


def _tile_rows(ref, j, t):
    return ref[pl.ds(pl.multiple_of(j * t, t), t), :]


def _walk_wide(lo, hi, tile_step, joint=True):
    n = hi - lo

    def step(j, width):
        if joint:
            tile_step(j, width)
        else:
            for u in range(width):
                tile_step(j + u, 1)

    def quad(t, carry):
        step(lo + 4 * t, 4)
        return carry

    lax.fori_loop(0, n // 4, quad, 0)

    @pl.when(n % 4 >= 2)
    def _():
        step(hi - n % 4, 2)

    @pl.when(n % 2 == 1)
    def _():
        step(hi - 1, 1)


def _carried(main_body, n_in, n_out, n_scratch, carry, grid):
    if carry is None:
        return main_body
    n_ci, n_co = len(carry["ins"]), len(carry["outs"])

    def body(*refs):
        pos = [0]

        def take(n):
            pos[0] += n
            return refs[pos[0] - n: pos[0]]

        ins, c_ins, outs, c_outs, scratch, sems = take(n_in), take(n_ci), take(n_out), take(n_co), take(n_scratch), take(3)
        steps = [pl.program_id(a) for a in range(len(grid))]

        @pl.when(functools.reduce(jnp.logical_and, [s == 0 for s in steps]))
        def _():
            for cp in carry["copies"](c_ins, c_outs, *sems):
                cp.start()

        main_body(*ins, *outs, *scratch)

        @pl.when(functools.reduce(jnp.logical_and, [s == n - 1 for s, n in zip(steps, grid)]))
        def _():
            for cp in carry["copies"](c_ins, c_outs, *sems):
                cp.wait()

    return body


def _carry_call(carry):
    if carry is None:
        return [], [], [], [], []
    sems = [pltpu.SemaphoreType.DMA((carry["n_remote"],)), pltpu.SemaphoreType.DMA((carry["n_remote"],)),
            pltpu.SemaphoreType.DMA((max(carry["n_local"], 1),))]
    return (list(carry["ins"]), [_HBM] * len(carry["ins"]), [_HBM] * len(carry["outs"]), list(carry["outs"]), sems)


def _flash_fwd(q, k, v, n_heads, carry=None):
    S = q.shape[0]
    T = min(FLASH_T, S)
    grid = (n_heads, S // T)
    c_args, c_in_specs, c_out_specs, c_out_shapes, c_sems = _carry_call(carry)

    def body(q_ref, k_ref, v_ref, o_ref, lse_ref, m_s, l_s, acc):
        i = pl.program_id(1)
        m_s[...] = jnp.full_like(m_s, -jnp.inf)
        l_s[...] = jnp.zeros_like(l_s)
        acc[...] = jnp.zeros_like(acc)
        qv = q_ref[...]

        def step(j, width, masked):
            keys = pl.ds(pl.multiple_of(j * T, T), width * T)
            s = _dot_nt(qv, k_ref[keys, :])
            if masked:
                s = jnp.where(_diag_mask(T), s, -jnp.inf)
            m_prev = m_s[...]
            m_new = jnp.maximum(m_prev, jnp.max(s, axis=1, keepdims=True))
            alpha = jnp.exp(m_prev - m_new)
            p = jnp.exp(s - m_new[:, :1])
            l_s[...] = alpha * l_s[...] + jnp.sum(p, axis=1, keepdims=True)
            acc[...] = alpha * acc[...] + _dot(p.astype(BF16), v_ref[keys, :])
            m_s[...] = m_new

        _walk_wide(0, i, lambda j, width: step(j, width, False))
        step(i, 1, True)
        o_ref[...] = acc[...] / l_s[...]
        lse_ref[0] = (m_s[...] + jnp.log(l_s[...]))[:, :1]

    res = pl.pallas_call(
        _carried(body, 3, 2, 3, carry, grid), name="flash_fwd", grid=grid,
        in_specs=[pl.BlockSpec((T, QK_PAD), lambda h, i: (i, h)),
                  pl.BlockSpec((S, QK_PAD), lambda h, i: (0, h)),
                  pl.BlockSpec((S, V_DIM), lambda h, i: (0, h))] + c_in_specs,
        out_specs=[pl.BlockSpec((T, V_DIM), lambda h, i: (i, h)),
                   pl.BlockSpec((1, T, 1), lambda h, i: (h, i, 0))] + c_out_specs,
        out_shape=[jax.ShapeDtypeStruct((S, n_heads * V_DIM), F32),
                   jax.ShapeDtypeStruct((n_heads, S, 1), F32)] + c_out_shapes,
        scratch_shapes=[pltpu.VMEM((T, V_DIM), F32), pltpu.VMEM((T, V_DIM), F32), pltpu.VMEM((T, V_DIM), F32)] + c_sems,
        compiler_params=_params(("arbitrary", "arbitrary")),
    )(q, k, v, *c_args)
    return res[0], res[1], list(res[2:])


def _flash_bwd_dq(q, k, v, do, lse, delta, n_heads):
    S = q.shape[0]
    T = min(FLASH_T, S)

    def body(q_ref, k_ref, v_ref, do_ref, lse_ref, dl_ref, dq_ref, acc):
        i = pl.program_id(1)
        acc[...] = jnp.zeros_like(acc)
        qv = q_ref[...]
        dov = do_ref[...]
        lse_c = lse_ref[0]
        dl_c = dl_ref[0]

        def step(j, width, masked):
            keys = pl.ds(pl.multiple_of(j * T, T), width * T)
            kt = k_ref[keys, :]
            s = _dot_nt(qv, kt)
            if masked:
                s = jnp.where(_diag_mask(T), s, -jnp.inf)
            p = jnp.exp(s - lse_c)
            ds = p * (_dot_nt(dov, v_ref[keys, :]) - dl_c)
            acc[...] += _dot(ds.astype(BF16), kt)

        _walk_wide(0, i, lambda j, width: step(j, width, False), joint=False)
        step(i, 1, True)
        dq_ref[...] = acc[...] * _attn_scale()

    return pl.pallas_call(
        body, name="flash_bwd_dq", grid=(n_heads, S // T),
        in_specs=[pl.BlockSpec((T, QK_PAD), lambda h, i: (i, h)),
                  pl.BlockSpec((S, QK_PAD), lambda h, i: (0, h)),
                  pl.BlockSpec((S, V_DIM), lambda h, i: (0, h)),
                  pl.BlockSpec((T, V_DIM), lambda h, i: (i, h)),
                  pl.BlockSpec((1, T, 1), lambda h, i: (h, i, 0)),
                  pl.BlockSpec((1, T, 1), lambda h, i: (h, i, 0))],
        out_specs=pl.BlockSpec((T, QK_PAD), lambda h, i: (i, h)),
        out_shape=jax.ShapeDtypeStruct((S, n_heads * QK_PAD), F32),
        scratch_shapes=[pltpu.VMEM((T, QK_PAD), F32)],
        compiler_params=_params(("parallel", "arbitrary")),
    )(q, k, v, do, lse, delta)


def _flash_bwd_dkv(q, k, v, do, lse_row, delta_row, n_heads, carry=None):
    S = q.shape[0]
    T = min(FLASH_T, S)
    nq = S // T
    grid = (n_heads, S // T)
    c_args, c_in_specs, c_out_specs, c_out_shapes, c_sems = _carry_call(carry)

    def body(q_ref, k_ref, v_ref, do_ref, lse_ref, dl_ref, dk_ref, dv_ref, dk_acc, dv_acc):
        j = pl.program_id(1)
        dk_acc[...] = jnp.zeros_like(dk_acc)
        dv_acc[...] = jnp.zeros_like(dv_acc)
        kv = k_ref[...]
        vv = v_ref[...]

        def step(i, width, masked):
            cols = pl.ds(pl.multiple_of(i * T, T), width * T)
            qt = q_ref[cols, :]
            dot = do_ref[cols, :]
            s_t = _dot_nt(kv, qt)
            if masked:
                s_t = jnp.where(_diag_mask(T, keys_first=True), s_t, -jnp.inf)
            p_t = jnp.exp(s_t - lse_ref[0, :, cols])
            dv_acc[...] += _dot(p_t.astype(BF16), dot)
            ds_t = p_t * (_dot_nt(vv, dot) - dl_ref[0, :, cols])
            dk_acc[...] += _dot(ds_t.astype(BF16), qt)

        step(j, 1, True)
        _walk_wide(j + 1, nq, lambda i, width: step(i, width, False), joint=False)
        dk_ref[...] = dk_acc[...]
        dv_ref[...] = dv_acc[...].astype(dv_ref.dtype)

    res = pl.pallas_call(
        _carried(body, 6, 2, 2, carry, grid), name="flash_bwd_dkv", grid=grid,
        in_specs=[pl.BlockSpec((S, QK_PAD), lambda h, j: (0, h)),
                  pl.BlockSpec((T, QK_PAD), lambda h, j: (j, h)),
                  pl.BlockSpec((T, V_DIM), lambda h, j: (j, h)),
                  pl.BlockSpec((S, V_DIM), lambda h, j: (0, h)),
                  pl.BlockSpec((1, 1, S), lambda h, j: (h, 0, 0)),
                  pl.BlockSpec((1, 1, S), lambda h, j: (h, 0, 0))] + c_in_specs,
        out_specs=[pl.BlockSpec((T, QK_PAD), lambda h, j: (j, h)),
                   pl.BlockSpec((T, V_DIM), lambda h, j: (j, h))] + c_out_specs,
        out_shape=[jax.ShapeDtypeStruct((S, n_heads * QK_PAD), F32),
                   jax.ShapeDtypeStruct((S, n_heads * V_DIM), BF16)] + c_out_shapes,
        scratch_shapes=[pltpu.VMEM((T, QK_PAD), F32), pltpu.VMEM((T, V_DIM), F32)] + c_sems,
        compiler_params=_params(("arbitrary", "arbitrary")),
    )(q, k, v, do, lse_row, delta_row, *c_args)
    return res[0], res[1], list(res[2:])


def _st_delta(n_heads):
    def fn(do, o):
        prod = do * o
        lane = lax.broadcasted_iota(jnp.int32, (do.shape[0], LANES), 1)
        out = jnp.zeros((do.shape[0], LANES), F32)
        for h in range(n_heads):
            out = out + jnp.where(lane == h, jnp.sum(prod[:, h * V_DIM:(h + 1) * V_DIM], axis=1, keepdims=True), 0.0)
        return out
    return fn


def _pad_cols(w, width):
    return jnp.pad(w, ((0, 0), (0, width - w.shape[1])))


def _dims(x, p, q_norm, kv_norm, w_uq, dt_bias, ssm_norm, conv_b):
    d = dict(S=x.shape[0], D=x.shape[1], PLE=p.shape[1], DQ=q_norm.shape[1], DKV=kv_norm.shape[1],
             NH=w_uq.shape[1] // (QK_NOPE + QK_ROPE), NHS=dt_bias.shape[1], DI=ssm_norm.shape[1],
             CONV=conv_b.shape[1])
    d["G"] = (d["CONV"] - d["DI"]) // (2 * D_STATE)
    d["R"] = d["NHS"] // d["G"]
    return d


def _assemble(name, blocks):
    rows, cols = blocks.shape[1:]
    if name in COL_SHARDED:
        return blocks.transpose(1, 0, 2).reshape(rows, N_DEV * cols)
    return blocks.reshape(N_DEV * rows, cols)


def _by_device(name, g):
    if name in COL_SHARDED:
        return g.reshape(g.shape[0], N_DEV, g.shape[1] // N_DEV).transpose(1, 0, 2)
    return g.reshape(N_DEV, g.shape[0] // N_DEV, g.shape[1])


def _local_step(x, p, positions, target, small, conv_w, wts, late_names=(), late_shards=()):
    wts = dict(wts)
    dm = _dims(x, p, small["q_norm"], small["kv_norm"], wts["w_uq"], small["dt_bias"], small["ssm_norm"],
               small["conv_b"])
    S, D, DQ, DKV, NH, NHS, DI, CONV, G, R = (dm[k] for k in ("S", "D", "DQ", "DKV", "NH", "NHS", "DI", "CONV", "G", "R"))
    rp = -(-R // SUBLANES) * SUBLANES
    L = min(CHUNK, S)

    w_in = wts["w_in"]
    o = [0]
    for n in (DQ, DKV, QK_ROPE, DI, CONV, NHS, D, D):
        o.append(o[-1] + n)
    w_cqkv, w_kr, w_z, w_xbc, w_dt, w_g = (w_in[:, o[0]:o[2]], w_in[:, o[2]:o[3]], w_in[:, o[3]:o[4]],
                                           w_in[:, o[4]:o[5]], w_in[:, o[5]:o[6]], w_in[:, o[6]:o[8]])
    zc = lambda n: jnp.zeros((D, n), BF16)
    w_sm = jnp.concatenate([zc(QK_NOPE), w_kr, zc(QK_PAD - QK_NOPE - QK_ROPE), w_dt, zc(LANES - NHS)], axis=1)
    w_q = jnp.pad(wts["w_uq"].reshape(DQ, NH, QK_NOPE + QK_ROPE),
                  ((0, 0), (0, 0), (0, QK_PAD - QK_NOPE - QK_ROPE))).reshape(DQ, NH * QK_PAD)
    ukv = wts["w_ukv"].reshape(DKV, NH, QK_NOPE + V_DIM)
    w_k = jnp.pad(ukv[:, :, :QK_NOPE], ((0, 0), (0, 0), (0, QK_PAD - QK_NOPE))).reshape(DKV, NH * QK_PAD)
    w_v = ukv[:, :, QK_NOPE:].reshape(DKV, NH * V_DIM)
    dt_bias_p, a_log_p = _pad_cols(small["dt_bias"], LANES), _pad_cols(small["a_log"], LANES)
    dskip_rep = jnp.repeat(small["d_skip"], SSM_HEADDIM, axis=1)
    c_tab, a_tab, b_tab = _rope_tables(positions)

    (u,) = _rowwise("pre_norm", _st_pre, [x], [small["mix_norm_pre"]], [(D, BF16)])
    cqkv = _mm("in_cqkv", u, w_cqkv, "nn")
    z = _mm("in_z", u, w_z, "nn")
    xbc = _mm("in_xbc", u, w_xbc, "nn")
    g = _mm("in_gates", u, w_g, "nn")
    sm = _mm("in_small", u, w_sm, "nn")

    lora_fn = _st_lora_norms(DQ)
    cq_n, ckv_n = _rowwise("lora_norms", lora_fn, [cqkv], [small["q_norm"], small["kv_norm"]], [(DQ, BF16), (DKV, BF16)])
    qraw = _mm("up_q", cq_n, w_q, "nn")
    kraw = _mm("up_k", ckv_n, w_k, "nn")
    v = _mm("up_v", ckv_n, w_v, "nn", out_dtype=BF16)
    q, k = _rowwise("rope", _st_rope(NH), [qraw, kraw, sm, c_tab, a_tab, b_tab], [],
                    [(NH * QK_PAD, BF16), (NH * QK_PAD, BF16)])
    attn, lse, late_blocks = _flash_fwd(q, k, v, NH, carry=_gather_carry(list(late_shards)) if late_names else None)
    wts.update({n: _assemble(n, b) for n, b in zip(late_names, late_blocks)})

    xbc_c, xbc_a = _conv_fwd(xbc, conv_w, small["conv_b"])
    dt, da = _rowwise("dt", _st_dt, [sm], [dt_bias_p, a_log_p], [(LANES, F32), (LANES, F32)])

    def col_layout(t):
        return _pad_cols(t[:, :NHS].reshape(S, G, R).transpose(1, 0, 2).reshape(G * S, R), LANES).reshape(G, S, LANES)

    def row_layout(t):
        return jnp.pad(t[:, :NHS].reshape(S, G, R).transpose(1, 2, 0), ((0, 0), (0, rp - R), (0, 0)))

    dt_row, da_col, da_row = row_layout(dt), col_layout(da), row_layout(da)
    xs = xbc_a[:, :DI]
    y, states = _ssd_fwd_t(xbc_a, dt_row, da_col, da_row, DI, G, R)
    gn_fn = _st_gated_norm(G)
    (ssm,) = _rowwise("gated_norm", gn_fn, [y, xs, z], [dskip_rep, small["ssm_norm"]], [(DI, BF16)])

    ao = _mm("attn_o", attn, wts["w_attn_o"], "nn")
    so = _mm("ssm_o", ssm, wts["w_ssm_o"], "nn")
    mix_fn = _st_mix(D)
    (mixed,) = _rowwise("mix", mix_fn, [g, ao, so], [], [(D, BF16)])
    mo = _mm("out_proj", mixed, wts["w_out"], "nn")
    h1, f = _rowwise("res1", _st_res_norm, [x, mo], [small["mix_norm_post"], small["ffn_norm_pre"]], [(D, F32), (D, BF16)])
    gt = _mm("ffn_gate", f, wts["w_gate"], "nn")
    up = _mm("ffn_up", f, wts["w_up"], "nn")
    (act,) = _rowwise("swiglu", _st_swiglu, [gt, up], [], [(gt.shape[1], BF16)])
    dn = _mm("ffn_down", act, wts["w_down"], "nn")
    h2, a3 = _rowwise("res2", _st_res_norm, [h1, dn], [small["ffn_norm_post"], small["ple_norm_pre"]], [(D, F32), (D, BF16)])
    gl = _mm("ple_gate", a3, wts["w_ple_gate"], "nn")
    pe = _mm("ple_proj", p, wts["w_ple"], "nn")

    sg = {}
    bg = {}
    dpe, dgl, dh2, d_w, loss_acc = _rowwise_bwd(
        "loss", _st_loss, [pe, gl, h2, target], [small["ple_norm_post"]], [1.0], [0, 1, 2], [BF16, BF16, F32], fwd_sums=(0,))
    sg["ple_norm_post"] = _fold(d_w)
    loss = loss_acc[0:1, :]
    bg["w_ple"] = _mm("d_w_ple", p, dpe, "tn", out_dtype=BF16)
    bg["w_ple_gate"] = _mm("d_w_ple_gate", a3, dgl, "tn", out_dtype=BF16)
    da3 = _mm("d_a3", dgl, wts["w_ple_gate"], "nt")

    dh1, ddn, d_post, d_pre = _rowwise_bwd(
        "res2_bwd", _st_res_norm, [h1, dn], [small["ffn_norm_post"], small["ple_norm_pre"]], [dh2, da3], [0, 1], [F32, BF16])
    sg["ffn_norm_post"], sg["ple_norm_pre"] = _fold(d_post), _fold(d_pre)
    bg["w_down"] = _mm("d_w_down", act, ddn, "tn", out_dtype=BF16)
    dact = _mm("d_act", ddn, wts["w_down"], "nt")
    dgt, dup = _rowwise_bwd("swiglu_bwd", _st_swiglu, [gt, up], [], [dact], [0, 1], [BF16, BF16])
    bg["w_gate"] = _mm("d_w_gate", f, dgt, "tn", out_dtype=BF16)
    bg["w_up"] = _mm("d_w_up", f, dup, "tn", out_dtype=BF16)
    df = _mm("d_f_gate", dgt, wts["w_gate"], "nt")
    df = _mm("d_f_up", dup, wts["w_up"], "nt", acc_in=df)

    dx_res, dmo, d_post, d_pre = _rowwise_bwd(
        "res1_bwd", _st_res_norm, [x, mo], [small["mix_norm_post"], small["ffn_norm_pre"]], [dh1, df], [0, 1], [F32, BF16])
    sg["mix_norm_post"], sg["ffn_norm_pre"] = _fold(d_post), _fold(d_pre)
    bg["w_out"] = _mm("d_w_out", mixed, dmo, "tn", out_dtype=BF16)
    dmixed = _mm("d_mixed", dmo, wts["w_out"], "nt")
    dg, dao, dso = _rowwise_bwd("mix_bwd", mix_fn, [g, ao, so], [], [dmixed], [0, 1, 2], [BF16, BF16, BF16])
    bg["w_attn_o"] = _mm("d_w_attn_o", attn, dao, "tn", out_dtype=BF16)
    bg["w_ssm_o"] = _mm("d_w_ssm_o", ssm, dso, "tn", out_dtype=BF16)
    dattn = _mm("d_attn", dao, wts["w_attn_o"], "nt", out_dtype=BF16)
    dssm = _mm("d_ssm", dso, wts["w_ssm_o"], "nt")

    dy, dxs_a, dz, d_dskip, d_ssmn = _rowwise_bwd(
        "gated_norm_bwd", gn_fn, [y, xs, z], [dskip_rep, small["ssm_norm"]], [dssm], [0, 1, 2], [F32, F32, BF16])
    sg["d_skip"] = _fold(d_dskip).reshape(NHS, SSM_HEADDIM).sum(axis=1).reshape(1, NHS)
    sg["ssm_norm"] = _fold(d_ssmn)
    dxs_b, d_b, d_c, ddt_row, dcum_row = _ssd_bwd_t(xbc_a, dt_row, da_col, da_row, states, dy, DI, G, R)

    def from_row(t):
        return _pad_cols(t[:, :R, :].transpose(2, 0, 1).reshape(S, NHS), LANES)

    ddtraw, d_bias, d_alog = _rowwise("dt_bwd", _st_dt_bwd, [sm, from_row(ddt_row), from_row(dcum_row)],
                                      [dt_bias_p, a_log_p], [(LANES, F32)], accs=(LANES, LANES), tr=L)
    sg["dt_bias"], sg["a_log"] = _fold(d_bias)[:, :NHS], _fold(d_alog)[:, :NHS]
    dconv, d_cb = _rowwise("dconv", _st_dconv(DI), [xbc_c, dxs_a, dxs_b, d_b, d_c], [], [(CONV, F32)], accs=(CONV,))
    sg["conv_b"] = _fold(d_cb)
    dxbc, d_cw = _conv_bwd(xbc, dconv, conv_w)
    d_conv_w = d_cw.reshape(CONV_WIDTH, SUBLANES, CONV).sum(axis=1)

    (delta,) = _rowwise("attn_delta", _st_delta(NH), [dattn, attn], [], [(LANES, F32)])
    delta = delta[:, :NH].T
    dq = _flash_bwd_dq(q, k, v, dattn, lse, delta.reshape(NH, S, 1), NH)
    late_sent = [_by_device(n, bg.pop(n)) for n in late_names]
    dk, dv, late_recv = _flash_bwd_dkv(q, k, v, dattn, lse.reshape(NH, 1, S), delta.reshape(NH, 1, S), NH,
                                       carry=_scatter_carry(late_sent) if late_names else None)
    dqraw, dkr, dk = _rowwise("rope_bwd", _st_rope_bwd(NH), [dq, dk, c_tab, a_tab, b_tab], [],
                              [(NH * QK_PAD, BF16), (QK_PAD, F32), (NH * QK_PAD, BF16)])
    d_w_q = _mm("d_w_q", cq_n, dqraw, "tn", out_dtype=BF16)
    d_w_k = _mm("d_w_k", ckv_n, dk, "tn", out_dtype=BF16)
    d_w_v = _mm("d_w_v", ckv_n, dv, "tn", out_dtype=BF16)
    dcq_n = _mm("d_cq_n", dqraw, w_q, "nt")
    dckv_n = _mm("d_ckv_n_k", dk, w_k, "nt")
    dckv_n = _mm("d_ckv_n_v", dv, w_v, "nt", acc_in=dckv_n)
    bg["w_uq"] = d_w_q.reshape(DQ, NH, QK_PAD)[:, :, :QK_NOPE + QK_ROPE].reshape(DQ, NH * (QK_NOPE + QK_ROPE))
    bg["w_ukv"] = jnp.concatenate([d_w_k.reshape(DKV, NH, QK_PAD)[:, :, :QK_NOPE], d_w_v.reshape(DKV, NH, V_DIM)],
                                  axis=2).reshape(DKV, NH * (QK_NOPE + V_DIM))
    dcqkv, d_qn, d_kvn = _rowwise_bwd("lora_norms_bwd", lora_fn, [cqkv], [small["q_norm"], small["kv_norm"]],
                                      [dcq_n, dckv_n], [0], [BF16])
    sg["q_norm"], sg["kv_norm"] = _fold(d_qn), _fold(d_kvn)

    dsm = jnp.concatenate([dkr, ddtraw], axis=1)
    d_w_cqkv = _mm("d_w_cqkv", u, dcqkv, "tn", out_dtype=BF16)
    d_w_z = _mm("d_w_z", u, dz, "tn", out_dtype=BF16)
    d_w_xbc = _mm("d_w_xbc", u, dxbc, "tn", out_dtype=BF16)
    d_w_g = _mm("d_w_gates", u, dg, "tn", out_dtype=BF16)
    d_w_sm = _mm("d_w_small", u, dsm, "tn", out_dtype=BF16)
    bg["w_in"] = jnp.concatenate(
        [d_w_cqkv, d_w_sm[:, QK_NOPE:QK_NOPE + QK_ROPE], d_w_z, d_w_xbc, d_w_sm[:, QK_PAD:QK_PAD + NHS], d_w_g], axis=1)
    du = _mm("d_u_cqkv", dcqkv, w_cqkv, "nt")
    du = _mm("d_u_z", dz, w_z, "nt", acc_in=du)
    du = _mm("d_u_xbc", dxbc, w_xbc, "nt", acc_in=du)
    du = _mm("d_u_gates", dg, w_g, "nt", acc_in=du)
    du = _mm("d_u_small", dsm, w_sm, "nt", acc_in=du)
    grad_x, d_pre = _rowwise_bwd("pre_norm_bwd", _st_pre, [x], [small["mix_norm_pre"]], [du], [0], [F32], adds={0: dx_res})
    sg["mix_norm_pre"] = _fold(d_pre)
    return loss, grad_x, sg, d_conv_w, bg, late_sent, late_recv


_HBM = pl.BlockSpec(memory_space=pltpu.HBM)
_FLIPS = ((0, 0, 1), (1, 0, 0), (0, 1, 0), (1, 1, 0), (1, 0, 1), (0, 1, 1), (1, 1, 1))


def _place():
    return lax.axis_index("x"), lax.axis_index("y"), lax.axis_index("c")


def _flipped(place, flip):
    return tuple(1 - v if f else v for v, f in zip(place, flip))


def _gather_carry(blocks):
    nw = len(blocks)

    def copies(ins, outs, send_sems, recv_sems, local_sems):
        x, y, c = _place()
        me = 4 * x + 2 * y + c
        cps = []
        for w in range(nw):
            cps.append(pltpu.make_async_copy(ins[w], outs[w].at[me], local_sems.at[w]))
            for k, flip in enumerate(_FLIPS):
                cps.append(pltpu.make_async_remote_copy(
                    src_ref=ins[w], dst_ref=outs[w].at[me], send_sem=send_sems.at[7 * w + k],
                    recv_sem=recv_sems.at[7 * w + k], device_id=_flipped((x, y, c), flip), device_id_type=MESH))
        return cps

    return dict(ins=blocks, outs=[jax.ShapeDtypeStruct((N_DEV,) + b.shape, b.dtype) for b in blocks],
                n_remote=7 * nw, n_local=nw, copies=copies)


def _scatter_carry(by_dev):
    nw = len(by_dev)

    def copies(ins, outs, send_sems, recv_sems, local_sems):
        x, y, c = _place()
        cps = []
        for w in range(nw):
            for k, flip in enumerate(_FLIPS):
                px, py, pc = _flipped((x, y, c), flip)
                cps.append(pltpu.make_async_remote_copy(
                    src_ref=ins[w].at[4 * px + 2 * py + pc], dst_ref=outs[w].at[k], send_sem=send_sems.at[7 * w + k],
                    recv_sem=recv_sems.at[7 * w + k], device_id=(px, py, pc), device_id_type=MESH))
        return cps

    return dict(ins=by_dev, outs=[jax.ShapeDtypeStruct((7,) + b.shape[1:], b.dtype) for b in by_dev],
                n_remote=7 * nw, n_local=0, copies=copies)


def _all_gather(name, blocks):
    nw = len(blocks)

    def body(*refs):
        x_refs, out_refs = refs[:nw], refs[nw:2 * nw]
        send_sems, recv_sems, local_sems = refs[2 * nw:]
        x, y, c = _place()
        me, sibling = (x, y, c), (x, y, 1 - c)
        chips = [(1 - x, y), (x, 1 - y), (1 - x, 1 - y)]

        def slot(w, px, py, pc):
            return out_refs[w].at[4 * px + 2 * py + pc]

        def copy(w, k, blk, to, src=None):
            return pltpu.make_async_remote_copy(
                src_ref=slot(w, *blk) if src is None else src, dst_ref=slot(w, *blk),
                send_sem=send_sems.at[7 * w + k], recv_sem=recv_sems.at[7 * w + k], device_id=to, device_id_type=MESH)

        mine = [pltpu.make_async_copy(x_refs[w], slot(w, *me), local_sems.at[w]) for w in range(nw)]
        for cp in mine:
            cp.start()
        first = []
        for w in range(nw):
            first.append(copy(w, 0, me, sibling, src=x_refs[w]))
            first += [copy(w, 1 + j, me, (*chip, c), src=x_refs[w]) for j, chip in enumerate(chips)]
        for cp in first:
            cp.start()
        passed = []
        for j, chip in enumerate(chips):
            for w in range(nw):
                copy(w, 1 + j, (*chip, c), me).wait_recv()
                passed.append(copy(w, 4 + j, (*chip, c), sibling))
                passed[-1].start()
        for w in range(nw):
            copy(w, 0, sibling, me).wait_recv()
        for j, chip in enumerate(chips):
            for w in range(nw):
                copy(w, 4 + j, (*chip, 1 - c), me).wait_recv()
        for cp in first + passed:
            cp.wait_send()
        for cp in mine:
            cp.wait()

    return pl.pallas_call(
        body, name=name, out_shape=[jax.ShapeDtypeStruct((N_DEV,) + b.shape, b.dtype) for b in blocks],
        in_specs=[_HBM] * nw, out_specs=[_HBM] * nw,
        scratch_shapes=[pltpu.SemaphoreType.DMA((7 * nw,)), pltpu.SemaphoreType.DMA((7 * nw,)),
                        pltpu.SemaphoreType.DMA((nw,))],
    )(*blocks)


def _swap_with_sibling(name, by_dev):
    nw = len(by_dev)

    def body(*refs):
        g_refs, got_refs = refs[:nw], refs[nw:2 * nw]
        send_sems, recv_sems = refs[2 * nw:]
        x, y, c = _place()
        cps = [pltpu.make_async_remote_copy(src_ref=g_refs[w].at[2 * q + (1 - c)], dst_ref=got_refs[w].at[q],
                                            send_sem=send_sems.at[4 * w + q], recv_sem=recv_sems.at[4 * w + q],
                                            device_id=(x, y, 1 - c), device_id_type=MESH)
               for w in range(nw) for q in range(4)]
        for cp in cps:
            cp.start()
        for cp in cps:
            cp.wait()

    return pl.pallas_call(
        body, name=name, out_shape=[jax.ShapeDtypeStruct((4,) + b.shape[1:], b.dtype) for b in by_dev],
        in_specs=[_HBM] * nw, out_specs=[_HBM] * nw,
        scratch_shapes=[pltpu.SemaphoreType.DMA((4 * nw,)), pltpu.SemaphoreType.DMA((4 * nw,))],
    )(*by_dev)


def _swap_with_chips(name, sends):
    nw = len(sends)

    def body(*refs):
        s_refs, r_refs = refs[:nw], refs[nw:2 * nw]
        send_sems, recv_sems = refs[2 * nw:]
        x, y, c = _place()
        chips = [(1 - x, y), (x, 1 - y), (1 - x, 1 - y)]
        cps = [pltpu.make_async_remote_copy(src_ref=s_refs[w].at[2 * px + py], dst_ref=r_refs[w].at[k],
                                            send_sem=send_sems.at[3 * w + k], recv_sem=recv_sems.at[3 * w + k],
                                            device_id=(px, py, c), device_id_type=MESH)
               for w in range(nw) for k, (px, py) in enumerate(chips)]
        for cp in cps:
            cp.start()
        for cp in cps:
            cp.wait()

    return pl.pallas_call(
        body, name=name, out_shape=[jax.ShapeDtypeStruct((3,) + s.shape[1:], s.dtype) for s in sends],
        in_specs=[_HBM] * nw, out_specs=[_HBM] * nw,
        scratch_shapes=[pltpu.SemaphoreType.DMA((3 * nw,)), pltpu.SemaphoreType.DMA((3 * nw,))],
    )(*sends)


def _lane_pad(n):
    return -(-n // LANES) * LANES


def _pack_small(vecs, mat):
    width = max(sum(_lane_pad(v.shape[1]) for v in vecs), _lane_pad(mat.shape[1]))
    row0 = jnp.concatenate([_pad_cols(v, _lane_pad(v.shape[1])) for v in vecs], axis=1)
    rows = jnp.concatenate([_pad_cols(row0, width), _pad_cols(mat, width)], axis=0)
    return jnp.pad(rows, ((0, SUBLANES - rows.shape[0]), (0, 0)))


def _unpack_small(packed, sizes, mat_cols):
    vecs, off = [], 0
    for n in sizes:
        vecs.append(packed[0:1, off:off + n])
        off += _lane_pad(n)
    return vecs, packed[1:1 + CONV_WIDTH, :mat_cols]


def _adamw(w, g, m, v):
    m = ADAM_B1 * m + (1.0 - ADAM_B1) * g
    v = ADAM_B2 * v + (1.0 - ADAM_B2) * (g * g)
    m_hat = m / (1.0 - ADAM_B1 ** ADAM_STEP)
    v_hat = v / (1.0 - ADAM_B2 ** ADAM_STEP)
    delta = -ADAM_LR * (m_hat / (jnp.sqrt(v_hat) + ADAM_EPS) + ADAM_WD * w)
    return delta, m, v


BIG = ("w_in", "w_uq", "w_ukv", "w_attn_o", "w_ssm_o", "w_out", "w_gate", "w_up", "w_down", "w_ple_gate", "w_ple")
FIRST = ("w_in", "w_uq", "w_ukv")
LATE = ("w_attn_o", "w_ssm_o", "w_out", "w_gate", "w_up", "w_down", "w_ple_gate", "w_ple")
COL_SHARDED = ("w_in", "w_uq", "w_ukv", "w_gate", "w_up", "w_ple")
SMALL = ("mix_norm_pre", "mix_norm_post", "q_norm", "kv_norm", "conv_b", "dt_bias", "a_log", "d_skip", "ssm_norm",
         "ffn_norm_pre", "ffn_norm_post", "ple_norm_pre", "ple_norm_post")
WEIGHTS = ("mix_norm_pre", "mix_norm_post", "w_in", "q_norm", "w_uq", "kv_norm", "w_ukv", "conv_w", "conv_b", "dt_bias",
           "a_log", "d_skip", "ssm_norm", "w_attn_o", "w_ssm_o", "w_out", "ffn_norm_pre", "ffn_norm_post", "w_gate",
           "w_up", "w_down", "ple_norm_pre", "ple_norm_post", "w_ple_gate", "w_ple")


def _step(x, p, positions, target, w, m, v):
    xi, yi, ci = _place()
    me = 4 * xi + 2 * yi + ci
    chip = 2 * xi + yi

    gathered = _all_gather("gather_weights", [w[n].astype(BF16) for n in FIRST])
    wts = {n: _assemble(n, blocks) for n, blocks in zip(FIRST, gathered)}
    cw_rows, cw_cols = w["conv_w"].shape
    (cw_all,) = _all_gather("gather_conv_w", [jnp.pad(w["conv_w"], ((0, SUBLANES - cw_rows), (0, 0)))])
    conv_w = cw_all[:, :cw_rows, :].transpose(1, 0, 2).reshape(cw_rows, N_DEV * cw_cols)

    small = {n: w[n] for n in SMALL}
    loss, grad_x, sg, d_conv_w, bg, late_sent, late_recv = _local_step(
        x, p, positions, target, small, conv_w, wts, LATE, [w[n].astype(BF16) for n in LATE])

    sizes = [w[n].shape[1] for n in SMALL]
    sg_pack = _pack_small([sg[n] for n in SMALL] + [loss], d_conv_w)
    (sg_all,) = _all_gather("gather_small_grads", [sg_pack])
    (sg_sum,) = _rowwise("sum_small_grads", lambda *a: functools.reduce(lambda s, t: s + t, a),
                         [sg_all[k] for k in range(N_DEV)], [], [(sg_pack.shape[1], F32)])
    sg_vecs, d_conv_w_sum = _unpack_small(sg_sum, sizes + [LANES], d_conv_w.shape[1])
    loss = sg_vecs[-1][0, 0]
    grads = dict(zip(SMALL, sg_vecs[:-1]))
    grads["conv_w"] = lax.dynamic_slice_in_dim(d_conv_w_sum, me * cw_cols, cw_cols, axis=1)

    by_dev = [_by_device(n, bg[n]) for n in FIRST]
    gots = _swap_with_sibling("rs_within_chip", by_dev)
    chip_f32, chip_bf16 = [], []
    for n, b, got in zip(FIRST, by_dev, gots):
        rows, cols = w[n].shape
        keep = lax.dynamic_index_in_dim(b.reshape(4, 2, rows, cols), ci, axis=1, keepdims=False)
        s32, s16 = _rowwise("rs_add_sibling_" + n, lambda a, b_: (a + b_, a + b_),
                            [keep.reshape(4 * rows, cols), got.reshape(4 * rows, cols)], [], [(cols, F32), (cols, BF16)])
        chip_f32.append(s32.reshape(4, rows, cols))
        chip_bf16.append(s16.reshape(4, rows, cols))
    recvs = _swap_with_chips("rs_between_chips", chip_bf16)

    def sum_then_adamw(wv, mv, vv, own, r0, r1, r2):
        g = ((own + r0) + r1) + r2
        return (g,) + _adamw(wv, g, mv, vv)

    def sum8_then_adamw(wv, mv, vv, own, *others):
        g = functools.reduce(lambda s, t: s + t, others, own)
        return (g,) + _adamw(wv, g, mv, vv)

    delta, new_m, new_v = {}, {}, {}
    for n, c32, recv in zip(FIRST, chip_f32, recvs):
        cols = w[n].shape[1]
        own = lax.dynamic_index_in_dim(c32, chip, axis=0, keepdims=False)
        grads[n], delta[n], new_m[n], new_v[n] = _rowwise(
            "adamw_" + n, sum_then_adamw, [w[n], m[n], v[n], own, recv[0], recv[1], recv[2]], [], [(cols, F32)] * 4)
    for n, sent, recv in zip(LATE, late_sent, late_recv):
        cols = w[n].shape[1]
        own = lax.dynamic_index_in_dim(sent, me, axis=0, keepdims=False)
        grads[n], delta[n], new_m[n], new_v[n] = _rowwise(
            "adamw_" + n, sum8_then_adamw, [w[n], m[n], v[n], own] + [recv[k] for k in range(N_DEV - 1)], [],
            [(cols, F32)] * 4)
    packed = [_pack_small([d[n] for n in SMALL], d["conv_w"]) for d in (w, grads, m, v)]
    outs = _rowwise("adamw_small", _adamw, packed, [], [(packed[0].shape[1], F32)] * 3)
    for d, o in zip((delta, new_m, new_v), outs):
        vecs, mat = _unpack_small(o, sizes, cw_cols)
        d.update(zip(SMALL, vecs))
        d["conv_w"] = mat
    return loss, grad_x, grads, delta, new_m, new_v


def kernel(x, p, positions, mix_norm_pre, mix_norm_post, w_in, q_norm, w_uq, kv_norm, w_ukv, conv_w, conv_b, dt_bias, a_log, d_skip, ssm_norm, w_attn_o, w_ssm_o, w_out, ffn_norm_pre, ffn_norm_post, w_gate, w_up, w_down, ple_norm_pre, ple_norm_post, w_ple_gate, w_ple, loss_target, m_mix_norm_pre, m_mix_norm_post, m_w_in, m_q_norm, m_w_uq, m_kv_norm, m_w_ukv, m_conv_w, m_conv_b, m_dt_bias, m_a_log, m_d_skip, m_ssm_norm, m_w_attn_o, m_w_ssm_o, m_w_out, m_ffn_norm_pre, m_ffn_norm_post, m_w_gate, m_w_up, m_w_down, m_ple_norm_pre, m_ple_norm_post, m_w_ple_gate, m_w_ple, v_mix_norm_pre, v_mix_norm_post, v_w_in, v_q_norm, v_w_uq, v_kv_norm, v_w_ukv, v_conv_w, v_conv_b, v_dt_bias, v_a_log, v_d_skip, v_ssm_norm, v_w_attn_o, v_w_ssm_o, v_w_out, v_ffn_norm_pre, v_ffn_norm_post, v_w_gate, v_w_up, v_w_down, v_ple_norm_pre, v_ple_norm_post, v_w_ple_gate, v_w_ple):
    w_args = (mix_norm_pre, mix_norm_post, w_in, q_norm, w_uq, kv_norm, w_ukv, conv_w, conv_b, dt_bias, a_log, d_skip, ssm_norm, w_attn_o, w_ssm_o, w_out, ffn_norm_pre, ffn_norm_post, w_gate, w_up, w_down, ple_norm_pre, ple_norm_post, w_ple_gate, w_ple)
    m_args = (m_mix_norm_pre, m_mix_norm_post, m_w_in, m_q_norm, m_w_uq, m_kv_norm, m_w_ukv, m_conv_w, m_conv_b, m_dt_bias, m_a_log, m_d_skip, m_ssm_norm, m_w_attn_o, m_w_ssm_o, m_w_out, m_ffn_norm_pre, m_ffn_norm_post, m_w_gate, m_w_up, m_w_down, m_ple_norm_pre, m_ple_norm_post, m_w_ple_gate, m_w_ple)
    v_args = (v_mix_norm_pre, v_mix_norm_post, v_w_in, v_q_norm, v_w_uq, v_kv_norm, v_w_ukv, v_conv_w, v_conv_b, v_dt_bias, v_a_log, v_d_skip, v_ssm_norm, v_w_attn_o, v_w_ssm_o, v_w_out, v_ffn_norm_pre, v_ffn_norm_post, v_w_gate, v_w_up, v_w_down, v_ple_norm_pre, v_ple_norm_post, v_w_ple_gate, v_w_ple)

    def drop_layer(a):
        return a if a.ndim == 2 else a[0]

    w = {n: drop_layer(a) for n, a in zip(WEIGHTS, w_args)}
    m = {n: drop_layer(a) for n, a in zip(WEIGHTS, m_args)}
    v = {n: drop_layer(a) for n, a in zip(WEIGHTS, v_args)}
    loss, grad_x, grads, delta, new_m, new_v = _step(x[0], p[0, 0], positions[0], loss_target[0], w, m, v)
    like = lambda d: [d[n].reshape(a.shape) for n, a in zip(WEIGHTS, w_args)]
    return (loss, grad_x[None], *like(grads), *like(delta), *like(new_m), *like(new_v))
```

```python
import functools

import jax
import jax.numpy as jnp
from jax import lax
from jax.experimental import pallas as pl
from jax.experimental.pallas import tpu as pltpu

F32 = jnp.float32
BF16 = jnp.bfloat16

EPS = 1e-6
QK_NOPE = 128
QK_ROPE = 64
V_DIM = 128
QK_PAD = 256
ROPE_THETA = 10000.0
SSM_HEADDIM = 64
D_STATE = 128
CONV_WIDTH = 4
CHUNK = 256
ADAM_LR = 0.001
ADAM_B1 = 0.9
ADAM_B2 = 0.999
ADAM_EPS = 1e-08
ADAM_WD = 0.01
ADAM_STEP = 10

N_DEV = 8
LANES = 128
SUBLANES = 8
PACK_W = 1024
VMEM_LIMIT = 56 * 1024 * 1024
ROW_TILE_BYTES = 6 * 1024 * 1024
FLASH_T = 512
MM_TILE_BYTES = 20 * 1024 * 1024
MESH = pl.DeviceIdType.MESH


def _pick(dim, prefs):
    if dim <= prefs[0]:
        return dim
    for p in prefs:
        if dim % p == 0:
            return p
    return dim


def _tile(dim, cap):
    if dim <= cap:
        return dim
    best = None
    for t in range(LANES, cap + 1, LANES):
        if dim % t == 0:
            best = t
    return best if best is not None else dim


def _params(sem):
    return pltpu.CompilerParams(dimension_semantics=sem, vmem_limit_bytes=VMEM_LIMIT)


def _dot(a, b):
    return lax.dot_general(a, b, (((1,), (0,)), ((), ())), preferred_element_type=F32)


def _dot_nt(a, b):
    return lax.dot_general(a, b, (((1,), (1,)), ((), ())), preferred_element_type=F32)


def _dot_tn(a, b):
    return lax.dot_general(a, b, (((0,), (0,)), ((), ())), preferred_element_type=F32)


def _mm(name, a, b, mode, out_dtype=F32, acc_in=None, carry=None):
    if mode == "nn":
        (M, K), (K2, N) = a.shape, b.shape
    elif mode == "nt":
        (M, K), (N, K2) = a.shape, b.shape
    else:
        (K, M), (K2, N) = a.shape, b.shape
    assert K == K2, (name, a.shape, b.shape, mode)
    tm = _tile(M, 1024)
    tn = _tile(N, 1024 if acc_in is not None else 1536)
    tk = _tile(K, 2048)
    while tk > 512 and 2 * (tm * tk * a.dtype.itemsize + tk * tn * b.dtype.itemsize) > MM_TILE_BYTES:
        tk = _tile(K, tk - LANES)
    nk = K // tk
    dot = {"nn": _dot, "nt": _dot_nt, "tn": _dot_tn}[mode]
    has_acc = acc_in is not None

    def body(*refs):
        if has_acc:
            a_ref, b_ref, c_ref, o_ref, acc = refs
        else:
            a_ref, b_ref, o_ref, acc = refs
        k = pl.program_id(2)

        @pl.when(k == 0)
        def _():
            acc[...] = jnp.zeros_like(acc)

        acc[...] += dot(a_ref[...].astype(BF16), b_ref[...].astype(BF16))

        @pl.when(k == nk - 1)
        def _():
            r = acc[...]
            if has_acc:
                r = r + c_ref[...]
            o_ref[...] = r.astype(o_ref.dtype)

    if mode == "tn":
        a_spec = pl.BlockSpec((tk, tm), lambda i, j, k: (k, i))
    else:
        a_spec = pl.BlockSpec((tm, tk), lambda i, j, k: (i, k))
    if mode == "nt":
        b_spec = pl.BlockSpec((tn, tk), lambda i, j, k: (j, k))
    else:
        b_spec = pl.BlockSpec((tk, tn), lambda i, j, k: (k, j))
    o_spec = pl.BlockSpec((tm, tn), lambda i, j, k: (i, j))
    in_specs = [a_spec, b_spec] + ([o_spec] if has_acc else [])
    args = (a, b) + ((acc_in,) if has_acc else ())
    if carry is None:
        return pl.pallas_call(
            body, name=name, grid=(M // tm, N // tn, nk), in_specs=in_specs, out_specs=o_spec,
            out_shape=jax.ShapeDtypeStruct((M, N), out_dtype), scratch_shapes=[pltpu.VMEM((tm, tn), F32)],
            input_output_aliases=({2: 0} if has_acc and out_dtype == F32 else {}),
            compiler_params=_params(("parallel", "parallel", "arbitrary")),
        )(*args)
    grid = (M // tm, N // tn, nk)
    c_args, c_in_specs, c_out_specs, c_out_shapes, c_sems = _carry_call(carry)
    res = pl.pallas_call(
        _carried(body, len(args), 1, 1, carry, grid), name=name, grid=grid, in_specs=in_specs + c_in_specs,
        out_specs=[o_spec] + c_out_specs, out_shape=[jax.ShapeDtypeStruct((M, N), out_dtype)] + c_out_shapes,
        scratch_shapes=[pltpu.VMEM((tm, tn), F32)] + c_sems,
        compiler_params=_params(("arbitrary", "arbitrary", "arbitrary")),
    )(*args, *c_args)
    return res[0], list(res[1:])


def _row_tile(n_rows, bytes_per_row):
    tr = 512
    while tr > SUBLANES and tr * bytes_per_row > ROW_TILE_BYTES:
        tr //= 2
    while n_rows % tr:
        tr //= 2
    return tr


def _acc_add(a_ref, v):
    if v.shape[0] == 1:
        a_ref[0:1, :] += v
    else:
        a_ref[...] += v.reshape(v.shape[0] // SUBLANES, SUBLANES, v.shape[1]).sum(axis=0)


def _rowwise(name, fn, rows, bcs, outs, accs=(), tr=None):
    n_rows = rows[0].shape[0]
    if tr is None:
        per_row = sum(r.shape[1] * r.dtype.itemsize for r in rows) + sum(w * jnp.dtype(d).itemsize for w, d in outs)
        tr = _row_tile(n_rows, per_row)
    n_r, n_b, n_o, n_a = len(rows), len(bcs), len(outs), len(accs)

    def body(*refs):
        ins = [r[...].astype(F32) for r in refs[: n_r + n_b]]
        res = fn(*ins)
        res = res if isinstance(res, (tuple, list)) else (res,)
        o_refs = refs[n_r + n_b: n_r + n_b + n_o]
        a_refs = refs[n_r + n_b + n_o:]
        for o, v in zip(o_refs, res[:n_o]):
            o[...] = v.astype(o.dtype)
        if n_a:
            @pl.when(pl.program_id(0) == 0)
            def _():
                for a in a_refs:
                    a[...] = jnp.zeros_like(a)

            for a, v in zip(a_refs, res[n_o:]):
                _acc_add(a, v)

    in_specs = [pl.BlockSpec((tr, r.shape[1]), lambda i: (i, 0)) for r in rows]
    in_specs += [pl.BlockSpec((1, b.shape[1]), lambda i: (0, 0)) for b in bcs]
    out_specs = [pl.BlockSpec((tr, w), lambda i: (i, 0)) for w, _ in outs]
    out_specs += [pl.BlockSpec((SUBLANES, w), lambda i: (0, 0)) for w in accs]
    out_shape = [jax.ShapeDtypeStruct((n_rows, w), d) for w, d in outs]
    out_shape += [jax.ShapeDtypeStruct((SUBLANES, w), F32) for w in accs]
    res = pl.pallas_call(
        body, name=name, grid=(n_rows // tr,), in_specs=in_specs, out_specs=out_specs, out_shape=out_shape,
        compiler_params=_params(("arbitrary",) if n_a else ("parallel",)),
    )(*rows, *bcs)
    return tuple(res)


def _rowwise_bwd(name, fn, rows, bcs, cts, need_rows, row_dtypes, need_bcs=None, adds=None, fwd_sums=(), tr=None):
    n_rows = rows[0].shape[0]
    adds = adds or {}
    need_bcs = list(range(len(bcs))) if need_bcs is None else list(need_bcs)
    ct_arrays = [c for c in cts if not isinstance(c, float)]
    add_keys = sorted(adds)
    add_arrays = [adds[k] for k in add_keys]
    if tr is None:
        per_row = sum(r.shape[1] * r.dtype.itemsize for r in list(rows) + ct_arrays + add_arrays)
        per_row += sum(rows[i].shape[1] * jnp.dtype(d).itemsize for i, d in zip(need_rows, row_dtypes))
        tr = _row_tile(n_rows, 2 * per_row)
    n_r, n_b, n_c, n_ad = len(rows), len(bcs), len(ct_arrays), len(add_arrays)
    n_go, n_gb, n_fs = len(need_rows), len(need_bcs), len(fwd_sums)

    def body(*refs):
        pos = 0
        r_t = [r[...].astype(F32) for r in refs[pos: pos + n_r]]
        pos += n_r
        b_t = [r[...].astype(F32) for r in refs[pos: pos + n_b]]
        pos += n_b
        c_t = [r[...].astype(F32) for r in refs[pos: pos + n_c]]
        pos += n_c
        ad_t = [r[...].astype(F32) for r in refs[pos: pos + n_ad]]
        pos += n_ad
        go_refs = refs[pos: pos + n_go]
        pos += n_go
        acc_refs = refs[pos:]

        def wrapped(*a):
            r = fn(*a)
            return tuple(r) if isinstance(r, (tuple, list)) else (r,)

        outs, vjp = jax.vjp(wrapped, *r_t, *b_t)
        it = iter(c_t)
        full = tuple(jnp.full(o.shape, c, F32) if isinstance(c, float) else next(it) for o, c in zip(outs, cts))
        grads = vjp(full)
        for o_ref, i in zip(go_refs, need_rows):
            g = grads[i]
            if i in adds:
                g = g + ad_t[add_keys.index(i)]
            o_ref[...] = g.astype(o_ref.dtype)

        @pl.when(pl.program_id(0) == 0)
        def _():
            for a in acc_refs:
                a[...] = jnp.zeros_like(a)

        for a, j in zip(acc_refs[:n_gb], need_bcs):
            _acc_add(a, grads[n_r + j])
        for a, j in zip(acc_refs[n_gb:], fwd_sums):
            a[0:1, :] += jnp.full((1, LANES), jnp.sum(outs[j]), F32)

    def row_spec(w):
        return pl.BlockSpec((tr, w), lambda i: (i, 0))

    in_specs = [row_spec(r.shape[1]) for r in rows]
    in_specs += [pl.BlockSpec((1, b.shape[1]), lambda i: (0, 0)) for b in bcs]
    in_specs += [row_spec(c.shape[1]) for c in ct_arrays] + [row_spec(a.shape[1]) for a in add_arrays]
    out_specs = [row_spec(rows[i].shape[1]) for i in need_rows]
    out_specs += [pl.BlockSpec((SUBLANES, bcs[j].shape[1]), lambda i: (0, 0)) for j in need_bcs]
    out_specs += [pl.BlockSpec((SUBLANES, LANES), lambda i: (0, 0)) for _ in fwd_sums]
    out_shape = [jax.ShapeDtypeStruct((n_rows, rows[i].shape[1]), d) for i, d in zip(need_rows, row_dtypes)]
    out_shape += [jax.ShapeDtypeStruct((SUBLANES, bcs[j].shape[1]), F32) for j in need_bcs]
    out_shape += [jax.ShapeDtypeStruct((SUBLANES, LANES), F32) for _ in fwd_sums]
    res = pl.pallas_call(
        body, name=name, grid=(n_rows // tr,), in_specs=in_specs, out_specs=out_specs, out_shape=out_shape,
        compiler_params=_params(("arbitrary",)),
    )(*rows, *bcs, *ct_arrays, *add_arrays)
    return tuple(res)


def _fold(acc):
    return jnp.sum(acc, axis=0, keepdims=True)


def _rms(x, w):
    return x * lax.rsqrt(jnp.mean(x * x, axis=-1, keepdims=True) + EPS) * w


def _sigmoid(x):
    return jax.nn.sigmoid(x)


def _silu(x):
    return x * _sigmoid(x)


def _log1p(u):
    series = u * (1.0 - u * (0.5 - u * (1.0 / 3.0 - u * 0.25)))
    return jnp.where(u < 0.01, series, jnp.log(1.0 + u))


def _softplus(x):
    return jnp.maximum(x, 0.0) + _log1p(jnp.exp(-jnp.abs(x)))


def _st_pre(x, w):
    return _rms(x, w)


def _st_lora_norms(dq):
    def fn(cqkv, qn, kvn):
        return _rms(cqkv[:, :dq], qn), _rms(cqkv[:, dq:], kvn)
    return fn


def _st_gated_norm(n_groups):
    def fn(y, xs, z, dskip, wn):
        yz = (y + dskip * xs) * _silu(z)
        gw = yz.shape[1] // n_groups
        parts = [_rms(yz[:, g * gw:(g + 1) * gw], wn[:, g * gw:(g + 1) * gw]) for g in range(n_groups)]
        return jnp.concatenate(parts, axis=1)
    return fn


def _st_mix(d):
    def fn(g, ao, so):
        return _sigmoid(g[:, :d]) * ao + _sigmoid(g[:, d:]) * so
    return fn


def _st_res_norm(h, y, w_post, w_pre):
    h2 = h + _rms(y, w_post)
    return h2, _rms(h2, w_pre)


def _st_swiglu(gt, up):
    return _silu(gt) * up


def _st_loss(pe, gl, h2, tgt, w_post):
    e = pe * _sigmoid(gl)
    diff = h2 + _rms(e, w_post) - tgt
    return 0.5 * jnp.mean(diff * diff, axis=-1, keepdims=True)


def _rope_tables(positions):
    half = QK_ROPE // 2
    inv_freq = ROPE_THETA ** (-jnp.arange(0, QK_ROPE, 2, dtype=F32) / QK_ROPE)
    ang = positions.astype(F32).reshape(-1, 1) * inv_freq
    cos, sin = jnp.cos(ang), jnp.sin(ang)
    n = ang.shape[0]
    z = lambda w: jnp.zeros((n, w), F32)
    c_tab = jnp.concatenate([jnp.ones((n, QK_NOPE), F32), cos, cos, z(QK_PAD - QK_NOPE - QK_ROPE)], axis=1)
    a_tab = jnp.concatenate([z(QK_NOPE), -sin, z(half), z(QK_PAD - QK_NOPE - QK_ROPE)], axis=1)
    b_tab = jnp.concatenate([z(QK_NOPE), z(half), sin, z(QK_PAD - QK_NOPE - QK_ROPE)], axis=1)
    return c_tab, a_tab, b_tab


def _rot(x, c, a, b):
    half = QK_ROPE // 2
    return x * c + pltpu.roll(x, QK_PAD - half, axis=1) * a + pltpu.roll(x, half, axis=1) * b


def _rot_t(g, c, a, b):
    half = QK_ROPE // 2
    return g * c + pltpu.roll(g * a, half, axis=1) + pltpu.roll(g * b, QK_PAD - half, axis=1)


def _st_rope(n_heads):
    def fn(qraw, kraw, sm, c, a, b):
        kpe = _rot(sm[:, :QK_PAD], c, a, b)
        scale = float(QK_NOPE + QK_ROPE) ** -0.5
        q = [_rot(qraw[:, h * QK_PAD:(h + 1) * QK_PAD], c, a, b) * scale for h in range(n_heads)]
        k = [kraw[:, h * QK_PAD:(h + 1) * QK_PAD] + kpe for h in range(n_heads)]
        return jnp.concatenate(q, axis=1), jnp.concatenate(k, axis=1)
    return fn


def _st_rope_bwd(n_heads):
    def fn(dq, dk, c, a, b):
        dqraw = [_rot_t(dq[:, h * QK_PAD:(h + 1) * QK_PAD], c, a, b) for h in range(n_heads)]
        dks = dk[:, :QK_PAD]
        for h in range(1, n_heads):
            dks = dks + dk[:, h * QK_PAD:(h + 1) * QK_PAD]
        return jnp.concatenate(dqraw, axis=1), _rot_t(dks, c, a, b), dk
    return fn


def _split3(x):
    h1 = x.astype(BF16)
    r1 = x - h1.astype(F32)
    h2 = r1.astype(BF16)
    h3 = (r1 - h2.astype(F32)).astype(BF16)
    return h1, h2, h3


def _tri_dot(tri, x):
    h1, h2, h3 = _split3(x)
    return (_dot(tri, h3) + _dot(tri, h2)) + _dot(tri, h1)


def _dot_tri(x, tri):
    h1, h2, h3 = _split3(x)
    return (_dot(h3, tri) + _dot(h2, tri)) + _dot(h1, tri)


def _st_dt(sm, bias, alog):
    x = sm[:, QK_PAD:] + bias
    dt = _softplus(x)
    return dt, dt * (-jnp.exp(alog))


def _st_dt_bwd(sm, ddt, dcum, bias, alog):
    n = sm.shape[0]
    i = lax.broadcasted_iota(jnp.int32, (n, n), 0)
    j = lax.broadcasted_iota(jnp.int32, (n, n), 1)
    upper = (j >= i).astype(BF16)
    dda = _tri_dot(upper, dcum)
    x = sm[:, QK_PAD:] + bias
    dt = _softplus(x)
    a = -jnp.exp(alog)
    draw = (ddt + dda * a) * _sigmoid(x)
    return draw, draw, dda * dt * a


def _conv_fwd(xbc, w, b):
    S, C = xbc.shape
    tr = _pick(S, (512, 256))
    tc = _pick(C, (1024, 512, 256, 128))
    hb = tr // SUBLANES

    def body(x_ref, halo_ref, w_ref, b_ref, c_ref, a_ref, ext):
        i = pl.program_id(1)
        halo = jnp.where(i == 0, 0.0, halo_ref[...])
        ext[0:SUBLANES, :] = halo
        ext[SUBLANES:, :] = x_ref[...]
        wv = w_ref[...]
        acc = b_ref[...] + wv[CONV_WIDTH - 1:CONV_WIDTH, :] * x_ref[...]
        for k in range(CONV_WIDTH - 1):
            off = SUBLANES - (CONV_WIDTH - 1) + k
            acc = acc + wv[k:k + 1, :] * ext[pl.ds(off, tr), :]
        c_ref[...] = acc
        a_ref[...] = _silu(acc)

    return pl.pallas_call(
        body, name="conv_fwd", grid=(C // tc, S // tr),
        in_specs=[pl.BlockSpec((tr, tc), lambda j, i: (i, j)),
                  pl.BlockSpec((SUBLANES, tc), lambda j, i: (jnp.maximum(i * hb - 1, 0), j)),
                  pl.BlockSpec((CONV_WIDTH, tc), lambda j, i: (0, j)),
                  pl.BlockSpec((1, tc), lambda j, i: (0, j))],
        out_specs=[pl.BlockSpec((tr, tc), lambda j, i: (i, j))] * 2,
        out_shape=[jax.ShapeDtypeStruct((S, C), F32)] * 2,
        scratch_shapes=[pltpu.VMEM((tr + SUBLANES, tc), F32)],
        compiler_params=_params(("parallel", "arbitrary")),
    )(xbc, xbc, w, b)


def _conv_bwd(xbc, dconv, w):
    S, C = xbc.shape
    tr = _pick(S, (512, 256))
    tc = _pick(C, (1024, 512, 256, 128))
    hb = tr // SUBLANES
    n_i = S // tr

    def body(x_ref, halo_ref, d_ref, dnext_ref, w_ref, dx_ref, dw_ref, ext, dext):
        i = pl.program_id(1)
        ext[0:SUBLANES, :] = jnp.where(i == 0, 0.0, halo_ref[...])
        ext[SUBLANES:, :] = x_ref[...]
        dext[0:tr, :] = d_ref[...]
        dext[tr:, :] = jnp.where(i == n_i - 1, 0.0, dnext_ref[...])
        wv = w_ref[...]
        d = d_ref[...]

        @pl.when(i == 0)
        def _():
            dw_ref[...] = jnp.zeros_like(dw_ref)

        dx = wv[CONV_WIDTH - 1:CONV_WIDTH, :] * d
        for k in range(CONV_WIDTH):
            if k < CONV_WIDTH - 1:
                dx = dx + wv[k:k + 1, :] * dext[pl.ds(CONV_WIDTH - 1 - k, tr), :]
                xs = ext[pl.ds(SUBLANES - (CONV_WIDTH - 1) + k, tr), :]
            else:
                xs = x_ref[...]
            prod = d * xs
            dw_ref[k * SUBLANES:(k + 1) * SUBLANES, :] += prod.reshape(tr // SUBLANES, SUBLANES, tc).sum(axis=0)
        dx_ref[...] = dx.astype(dx_ref.dtype)

    return pl.pallas_call(
        body, name="conv_bwd", grid=(C // tc, n_i),
        in_specs=[pl.BlockSpec((tr, tc), lambda j, i: (i, j)),
                  pl.BlockSpec((SUBLANES, tc), lambda j, i: (jnp.maximum(i * hb - 1, 0), j)),
                  pl.BlockSpec((tr, tc), lambda j, i: (i, j)),
                  pl.BlockSpec((SUBLANES, tc), lambda j, i: (jnp.minimum((i + 1) * hb, S // SUBLANES - 1), j)),
                  pl.BlockSpec((CONV_WIDTH, tc), lambda j, i: (0, j))],
        out_specs=[pl.BlockSpec((tr, tc), lambda j, i: (i, j)),
                   pl.BlockSpec((CONV_WIDTH * SUBLANES, tc), lambda j, i: (0, j))],
        out_shape=[jax.ShapeDtypeStruct((S, C), BF16), jax.ShapeDtypeStruct((CONV_WIDTH * SUBLANES, C), F32)],
        scratch_shapes=[pltpu.VMEM((tr + SUBLANES, tc), F32), pltpu.VMEM((tr + SUBLANES, tc), F32)],
        compiler_params=_params(("parallel", "arbitrary")),
    )(xbc, xbc, dconv, dconv, w)


def _st_dconv(d_inner):
    def fn(xc, dxa, dxb, db_, dc_):
        s = _sigmoid(xc)
        g = jnp.concatenate([dxa + dxb, db_, dc_], axis=1) * (s * (1.0 + xc * (1.0 - s)))
        return g, g
    return fn


def _chunk_setup(b_ref, c_ref, dac_ref, dar_ref, L):
    ii = lax.broadcasted_iota(jnp.int32, (L, L), 0)
    jj = lax.broadcasted_iota(jnp.int32, (L, L), 1)
    tri = ii >= jj
    cum_c = _tri_dot(tri.astype(BF16), dac_ref[0])
    cum_r = _dot_tri(dar_ref[0], (ii <= jj).astype(BF16))
    bm = b_ref[...].astype(BF16)
    cm = c_ref[...].astype(BF16)
    return tri, cum_c, cum_r, bm, cm, _dot_nt(cm, bm)


def _ssd_specs(d_inner, n_groups, gw, L, rp, chunk_of):
    bb0 = d_inner // D_STATE
    cb0 = bb0 + n_groups
    return [pl.BlockSpec((L, gw), lambda g, c: (chunk_of(c), g)),
            pl.BlockSpec((L, D_STATE), lambda g, c: (chunk_of(c), bb0 + g)),
            pl.BlockSpec((L, D_STATE), lambda g, c: (chunk_of(c), cb0 + g)),
            pl.BlockSpec((1, L, LANES), lambda g, c: (g, chunk_of(c), 0)),
            pl.BlockSpec((1, L, LANES), lambda g, c: (g, chunk_of(c), 0)),
            pl.BlockSpec((1, rp, L), lambda g, c: (g, 0, chunk_of(c)))]


def _ssd_fwd(xbc_a, dt_col, da_col, da_row, d_inner, n_groups, R):
    S = xbc_a.shape[0]
    L = min(CHUNK, S)
    NC = S // L
    P, N = SSM_HEADDIM, D_STATE
    gw = R * P
    rp = da_row.shape[1]

    def body(x_ref, b_ref, c_ref, dt_ref, dac_ref, dar_ref, y_ref, st_ref, state):
        @pl.when(pl.program_id(1) == 0)
        def _():
            state[...] = jnp.zeros_like(state)

        st_ref[0, 0] = state[...]
        tri, cum_c, cum_r, bm, cm, gm = _chunk_setup(b_ref, c_ref, dac_ref, dar_ref, L)
        dt = dt_ref[0]
        for r in range(R):
            cc = cum_c[:, r:r + 1]
            cr = cum_r[r:r + 1, :]
            lam = jnp.exp(jnp.where(tri, cc - cr, -jnp.inf))
            m = (gm * lam).astype(BF16)
            x = x_ref[:, r * P:(r + 1) * P] * dt[:, r:r + 1]
            s_r = state[r * P:(r + 1) * P, :]
            y_off = _dot_nt(cm, s_r.astype(BF16)) * jnp.exp(cc)
            y_ref[:, r * P:(r + 1) * P] = _dot(m, x.astype(BF16)) + y_off
            last = cr[:, L - 1:L]
            xw = (x * jnp.exp(last - cc)).astype(BF16)
            state[r * P:(r + 1) * P, :] = s_r * jnp.exp(last) + _dot_tn(xw, bm)

    return pl.pallas_call(
        body, name="ssd_fwd", grid=(n_groups, NC),
        in_specs=_ssd_specs(d_inner, n_groups, gw, L, rp, lambda c: c),
        out_specs=[pl.BlockSpec((L, gw), lambda g, c: (c, g)),
                   pl.BlockSpec((1, 1, gw, N), lambda g, c: (g, c, 0, 0))],
        out_shape=[jax.ShapeDtypeStruct((S, d_inner), F32), jax.ShapeDtypeStruct((n_groups, NC, gw, N), F32)],
        scratch_shapes=[pltpu.VMEM((gw, N), F32)],
        compiler_params=_params(("parallel", "arbitrary")),
    )(xbc_a, xbc_a, xbc_a, dt_col, da_col, da_row)


def _ssd_bwd(xbc_a, dt_col, da_col, da_row, states, dy, d_inner, n_groups, R):
    S = xbc_a.shape[0]
    L = min(CHUNK, S)
    NC = S // L
    P, N = SSM_HEADDIM, D_STATE
    gw = R * P
    rp = da_row.shape[1]
    rev = lambda c: NC - 1 - c

    def body(x_ref, b_ref, c_ref, dt_ref, dac_ref, dar_ref, st_ref, dy_ref,
             dx_ref, db_ref, dc_ref, ddt_ref, dcc_ref, dcr_ref, dstate):
        @pl.when(pl.program_id(1) == 0)
        def _():
            dstate[...] = jnp.zeros_like(dstate)

        tri, cum_c, cum_r, bm, cm, gm = _chunk_setup(b_ref, c_ref, dac_ref, dar_ref, L)
        dt = dt_ref[0]
        lane = lax.broadcasted_iota(jnp.int32, (L, LANES), 1)
        sub = lax.broadcasted_iota(jnp.int32, (rp, L), 0)
        is_last = lax.broadcasted_iota(jnp.int32, (L, 1), 0) == L - 1
        d_g = jnp.zeros((L, L), F32)
        dc_acc = jnp.zeros((L, N), F32)
        db_acc = jnp.zeros((L, N), F32)
        ddt_out = jnp.zeros((L, LANES), F32)
        dcc_out = jnp.zeros((L, LANES), F32)
        dcr_out = jnp.zeros((rp, L), F32)
        for r in range(R):
            cc = cum_c[:, r:r + 1]
            cr = cum_r[r:r + 1, :]
            lam = jnp.exp(jnp.where(tri, cc - cr, -jnp.inf))
            m = gm * lam
            dtc = dt[:, r:r + 1]
            xh = x_ref[:, r * P:(r + 1) * P]
            x = xh * dtc
            xb = x.astype(BF16)
            d_y = dy_ref[:, r * P:(r + 1) * P]
            d_yb = d_y.astype(BF16)
            s_r = st_ref[0, 0, r * P:(r + 1) * P, :]
            s_rb = s_r.astype(BF16)
            ds_n = dstate[r * P:(r + 1) * P, :]
            ds_nb = ds_n.astype(BF16)
            e = jnp.exp(cc)
            last = cr[:, L - 1:L]
            e_last = jnp.exp(last)
            w = jnp.exp(last - cc)
            d_m = _dot_nt(d_yb, xb)
            d_x = _dot_tn(m.astype(BF16), d_yb)
            d_ye = (d_y * e).astype(BF16)
            dc_acc = dc_acc + _dot(d_ye, s_rb)
            ds_part = _dot_tn(d_ye, cm)
            y_off = _dot_nt(cm, s_rb) * e
            dcum = jnp.sum(d_y * y_off, axis=1, keepdims=True)
            d_xw = _dot_nt(bm, ds_nb)
            d_x = d_x + d_xw * w
            dw_w = jnp.sum(d_xw * x, axis=1, keepdims=True) * w
            db_acc = db_acc + _dot((x * w).astype(BF16), ds_nb)
            d_last = jnp.sum(ds_n * s_r, keepdims=True) * e_last + jnp.sum(dw_w, keepdims=True)
            dcum = dcum - dw_w
            dstate[r * P:(r + 1) * P, :] = e_last * ds_n + ds_part
            d_g = d_g + d_m * lam
            q = d_m * m
            dcum = dcum + jnp.sum(q, axis=1, keepdims=True) + jnp.where(is_last, d_last, 0.0)
            dcum_row = -jnp.sum(q, axis=0, keepdims=True)
            dx_ref[:, r * P:(r + 1) * P] = d_x * dtc
            ddt = jnp.sum(d_x * xh, axis=1, keepdims=True)
            ddt_out = ddt_out + jnp.where(lane == r, ddt, 0.0)
            dcc_out = dcc_out + jnp.where(lane == r, dcum, 0.0)
            dcr_out = dcr_out + jnp.where(sub == r, dcum_row, 0.0)
        d_gb = d_g.astype(BF16)
        dc_ref[...] = dc_acc + _dot(d_gb, bm)
        db_ref[...] = db_acc + _dot_tn(d_gb, cm)
        ddt_ref[0] = ddt_out
        dcc_ref[0] = dcc_out
        dcr_ref[0] = dcr_out

    gn = n_groups * N
    return pl.pallas_call(
        body, name="ssd_bwd", grid=(n_groups, NC),
        in_specs=_ssd_specs(d_inner, n_groups, gw, L, rp, rev) + [
            pl.BlockSpec((1, 1, gw, N), lambda g, c: (g, rev(c), 0, 0)),
            pl.BlockSpec((L, gw), lambda g, c: (rev(c), g))],
        out_specs=[pl.BlockSpec((L, gw), lambda g, c: (rev(c), g)),
                   pl.BlockSpec((L, N), lambda g, c: (rev(c), g)),
                   pl.BlockSpec((L, N), lambda g, c: (rev(c), g)),
                   pl.BlockSpec((1, L, LANES), lambda g, c: (g, rev(c), 0)),
                   pl.BlockSpec((1, L, LANES), lambda g, c: (g, rev(c), 0)),
                   pl.BlockSpec((1, rp, L), lambda g, c: (g, 0, rev(c)))],
        out_shape=[jax.ShapeDtypeStruct((S, d_inner), F32), jax.ShapeDtypeStruct((S, gn), F32),
                   jax.ShapeDtypeStruct((S, gn), F32), jax.ShapeDtypeStruct((n_groups, S, LANES), F32),
                   jax.ShapeDtypeStruct((n_groups, S, LANES), F32), jax.ShapeDtypeStruct((n_groups, rp, S), F32)],
        scratch_shapes=[pltpu.VMEM((gw, N), F32)],
        compiler_params=_params(("parallel", "arbitrary")),
    )(xbc_a, xbc_a, xbc_a, dt_col, da_col, da_row, states, dy)


def _chunk_setup_t(b_ref, c_ref, dac_ref, dar_ref, L):
    ii = lax.broadcasted_iota(jnp.int32, (L, L), 0)
    jj = lax.broadcasted_iota(jnp.int32, (L, L), 1)
    lower = ii >= jj
    upper = ii <= jj
    cum_c = _tri_dot(lower.astype(BF16), dac_ref[0])
    cum_r = _dot_tri(dar_ref[0], upper.astype(BF16))
    bm = b_ref[...].astype(BF16)
    cm = c_ref[...].astype(BF16)
    return lower, upper, cum_c, cum_r, bm, cm


def _ssd_specs_t(d_inner, n_groups, gw, L, rp, chunk_of):
    bb0 = d_inner // D_STATE
    cb0 = bb0 + n_groups
    return [pl.BlockSpec((L, gw), lambda g, c: (chunk_of(c), g)),
            pl.BlockSpec((L, D_STATE), lambda g, c: (chunk_of(c), bb0 + g)),
            pl.BlockSpec((L, D_STATE), lambda g, c: (chunk_of(c), cb0 + g)),
            pl.BlockSpec((1, rp, L), lambda g, c: (g, 0, chunk_of(c))),
            pl.BlockSpec((1, L, LANES), lambda g, c: (g, chunk_of(c), 0)),
            pl.BlockSpec((1, rp, L), lambda g, c: (g, 0, chunk_of(c)))]


def _ssd_fwd_t(xbc_a, dt_row, da_col, da_row, d_inner, n_groups, R):
    S = xbc_a.shape[0]
    L = min(CHUNK, S)
    NC = S // L
    P, N = SSM_HEADDIM, D_STATE
    gw = R * P
    rp = da_row.shape[1]

    def body(x_ref, b_ref, c_ref, dt_ref, dac_ref, dar_ref, y_ref, st_ref, state, y_t):
        @pl.when(pl.program_id(1) == 0)
        def _():
            state[...] = jnp.zeros_like(state)

        st_ref[0, 0] = state[...]
        lower, upper, cum_c, cum_r, bm, cm = _chunk_setup_t(b_ref, c_ref, dac_ref, dar_ref, L)
        gm_t = _dot_nt(bm, cm)
        dt = dt_ref[0]
        x_t = x_ref[...].T
        for r in range(R):
            cc = cum_c[:, r:r + 1]
            cr = cum_r[r:r + 1, :]
            m_t = (gm_t * jnp.exp(jnp.where(upper, cr - cc, -jnp.inf))).astype(BF16)
            x = x_t[r * P:(r + 1) * P, :] * dt[r:r + 1, :]
            s_r = state[r * P:(r + 1) * P, :]
            y_off = _dot_nt(s_r.astype(BF16), cm) * jnp.exp(cr)
            y_t[r * P:(r + 1) * P, :] = _dot(x.astype(BF16), m_t) + y_off
            last = cr[:, L - 1:L]
            xw = (x * jnp.exp(last - cr)).astype(BF16)
            state[r * P:(r + 1) * P, :] = s_r * jnp.exp(last) + _dot(xw, bm)
        y_ref[...] = y_t[...].T

    return pl.pallas_call(
        body, name="ssd_fwd", grid=(n_groups, NC),
        in_specs=_ssd_specs_t(d_inner, n_groups, gw, L, rp, lambda c: c),
        out_specs=[pl.BlockSpec((L, gw), lambda g, c: (c, g)),
                   pl.BlockSpec((1, 1, gw, N), lambda g, c: (g, c, 0, 0))],
        out_shape=[jax.ShapeDtypeStruct((S, d_inner), F32), jax.ShapeDtypeStruct((n_groups, NC, gw, N), F32)],
        scratch_shapes=[pltpu.VMEM((gw, N), F32), pltpu.VMEM((gw, L), F32)],
        compiler_params=_params(("parallel", "arbitrary")),
    )(xbc_a, xbc_a, xbc_a, dt_row, da_col, da_row)


def _ssd_bwd_t(xbc_a, dt_row, da_col, da_row, states, dy, d_inner, n_groups, R):
    S = xbc_a.shape[0]
    L = min(CHUNK, S)
    NC = S // L
    P, N = SSM_HEADDIM, D_STATE
    gw = R * P
    rp = da_row.shape[1]
    rev = lambda c: NC - 1 - c

    def body(x_ref, b_ref, c_ref, dt_ref, dac_ref, dar_ref, st_ref, dy_ref,
             dx_ref, db_ref, dc_ref, ddt_ref, dcum_ref, dstate, dx_t):
        @pl.when(pl.program_id(1) == 0)
        def _():
            dstate[...] = jnp.zeros_like(dstate)

        lower, upper, cum_c, cum_r, bm, cm = _chunk_setup_t(b_ref, c_ref, dac_ref, dar_ref, L)
        gm = _dot_nt(cm, bm)
        gm_t = _dot_nt(bm, cm)
        dt = dt_ref[0]
        x_t = x_ref[...].T
        dy_t = dy_ref[...].T
        sub = lax.broadcasted_iota(jnp.int32, (rp, L), 0)
        is_last = lax.broadcasted_iota(jnp.int32, (1, L), 1) == L - 1
        d_g = jnp.zeros((L, L), F32)
        d_g_t = jnp.zeros((L, L), F32)
        dc_acc = jnp.zeros((L, N), F32)
        db_acc = jnp.zeros((L, N), F32)
        ddt_out = jnp.zeros((rp, L), F32)
        dcum_out = jnp.zeros((rp, L), F32)
        for r in range(R):
            cc = jnp.broadcast_to(cum_c[:, r:r + 1], (L, L))
            cr = cum_r[r:r + 1, :]
            lam = jnp.exp(jnp.where(lower, cc - cum_r[r:r + 1, :], -jnp.inf))
            lam_t = jnp.exp(jnp.where(upper, cr - cc, -jnp.inf))
            m = gm * lam
            m_t = gm_t * lam_t
            dtr = dt[r:r + 1, :]
            xh = x_t[r * P:(r + 1) * P, :]
            x = xh * dtr
            xb = x.astype(BF16)
            d_y = dy_t[r * P:(r + 1) * P, :]
            d_yb = d_y.astype(BF16)
            s_r = st_ref[0, 0, r * P:(r + 1) * P, :]
            s_rb = s_r.astype(BF16)
            ds_n = dstate[r * P:(r + 1) * P, :]
            ds_nb = ds_n.astype(BF16)
            e = jnp.exp(cr)
            last = cr[:, L - 1:L]
            e_last = jnp.exp(last)
            w = jnp.exp(last - cr)
            d_x = _dot(d_yb, m.astype(BF16))
            d_m = _dot_tn(d_yb, xb)
            d_m_t = _dot_tn(xb, d_yb)
            d_ye = (d_y * e).astype(BF16)
            dc_acc = dc_acc + _dot_tn(d_ye, s_rb)
            ds_part = _dot(d_ye, cm)
            y_off = _dot_nt(s_rb, cm) * e
            dcum = jnp.sum(d_y * y_off, axis=0, keepdims=True)
            d_xw = _dot_nt(ds_nb, bm)
            d_x = d_x + d_xw * w
            dw_w = jnp.sum(d_xw * x, axis=0, keepdims=True) * w
            db_acc = db_acc + _dot_tn((x * w).astype(BF16), ds_nb)
            d_last = jnp.sum(ds_n * s_r, keepdims=True) * e_last + jnp.sum(dw_w, keepdims=True)
            dstate[r * P:(r + 1) * P, :] = e_last * ds_n + ds_part
            d_g = d_g + d_m * lam
            d_g_t = d_g_t + d_m_t * lam_t
            dcum = (dcum - dw_w + jnp.sum(d_m_t * m_t, axis=0, keepdims=True)
                    - jnp.sum(d_m * m, axis=0, keepdims=True) + jnp.where(is_last, d_last, 0.0))
            dx_t[r * P:(r + 1) * P, :] = d_x * dtr
            ddt = jnp.sum(d_x * xh, axis=0, keepdims=True)
            ddt_out = ddt_out + jnp.where(sub == r, ddt, 0.0)
            dcum_out = dcum_out + jnp.where(sub == r, dcum, 0.0)
        dc_ref[...] = dc_acc + _dot(d_g.astype(BF16), bm)
        db_ref[...] = db_acc + _dot(d_g_t.astype(BF16), cm)
        dx_ref[...] = dx_t[...].T
        ddt_ref[0] = ddt_out
        dcum_ref[0] = dcum_out

    gn = n_groups * N
    return pl.pallas_call(
        body, name="ssd_bwd", grid=(n_groups, NC),
        in_specs=_ssd_specs_t(d_inner, n_groups, gw, L, rp, rev) + [
            pl.BlockSpec((1, 1, gw, N), lambda g, c: (g, rev(c), 0, 0)),
            pl.BlockSpec((L, gw), lambda g, c: (rev(c), g))],
        out_specs=[pl.BlockSpec((L, gw), lambda g, c: (rev(c), g)),
                   pl.BlockSpec((L, N), lambda g, c: (rev(c), g)),
                   pl.BlockSpec((L, N), lambda g, c: (rev(c), g)),
                   pl.BlockSpec((1, rp, L), lambda g, c: (g, 0, rev(c))),
                   pl.BlockSpec((1, rp, L), lambda g, c: (g, 0, rev(c)))],
        out_shape=[jax.ShapeDtypeStruct((S, d_inner), F32), jax.ShapeDtypeStruct((S, gn), F32),
                   jax.ShapeDtypeStruct((S, gn), F32), jax.ShapeDtypeStruct((n_groups, rp, S), F32),
                   jax.ShapeDtypeStruct((n_groups, rp, S), F32)],
        scratch_shapes=[pltpu.VMEM((gw, N), F32), pltpu.VMEM((gw, L), F32)],
        compiler_params=_params(("parallel", "arbitrary")),
    )(xbc_a, xbc_a, xbc_a, dt_row, da_col, da_row, states, dy)


def _attn_scale():
    return float(QK_NOPE + QK_ROPE) ** -0.5


def _diag_mask(t, keys_first=False):
    rows = lax.broadcasted_iota(jnp.int32, (t, t), 0)
    cols = lax.broadcasted_iota(jnp.int32, (t, t), 1)
    return rows <= cols if keys_first else cols <= rows


def _tile_rows(ref, j, t):
    return ref[pl.ds(pl.multiple_of(j * t, t), t), :]


def _walk_wide(lo, hi, tile_step, joint=True):
    n = hi - lo

    def step(j, width):
        if joint:
            tile_step(j, width)
        else:
            for u in range(width):
                tile_step(j + u, 1)

    def quad(t, carry):
        step(lo + 4 * t, 4)
        return carry

    lax.fori_loop(0, n // 4, quad, 0)

    @pl.when(n % 4 >= 2)
    def _():
        step(hi - n % 4, 2)

    @pl.when(n % 2 == 1)
    def _():
        step(hi - 1, 1)


def _carried(main_body, n_in, n_out, n_scratch, carry, grid):
    if carry is None:
        return main_body
    n_ci, n_co = len(carry["ins"]), len(carry["outs"])

    def body(*refs):
        pos = [0]

        def take(n):
            pos[0] += n
            return refs[pos[0] - n: pos[0]]

        ins, c_ins, outs, c_outs, scratch, sems = take(n_in), take(n_ci), take(n_out), take(n_co), take(n_scratch), take(3)
        steps = [pl.program_id(a) for a in range(len(grid))]

        @pl.when(functools.reduce(jnp.logical_and, [s == 0 for s in steps]))
        def _():
            for cp in carry["copies"](c_ins, c_outs, *sems):
                cp.start()

        main_body(*ins, *outs, *scratch)

        @pl.when(functools.reduce(jnp.logical_and, [s == n - 1 for s, n in zip(steps, grid)]))
        def _():
            for cp in carry["copies"](c_ins, c_outs, *sems):
                cp.wait()

    return body


def _carry_call(carry):
    if carry is None:
        return [], [], [], [], []
    sems = [pltpu.SemaphoreType.DMA((carry["n_remote"],)), pltpu.SemaphoreType.DMA((carry["n_remote"],)),
            pltpu.SemaphoreType.DMA((max(carry["n_local"], 1),))]
    return (list(carry["ins"]), [_HBM] * len(carry["ins"]), [_HBM] * len(carry["outs"]), list(carry["outs"]), sems)


def _flash_fwd(q, k, v, n_heads, carry=None):
    S = q.shape[0]
    T = min(FLASH_T, S)
    grid = (n_heads, S // T)
    c_args, c_in_specs, c_out_specs, c_out_shapes, c_sems = _carry_call(carry)

    def body(q_ref, k_ref, v_ref, o_ref, lse_ref, m_s, l_s, acc):
        i = pl.program_id(1)
        m_s[...] = jnp.full_like(m_s, -jnp.inf)
        l_s[...] = jnp.zeros_like(l_s)
        acc[...] = jnp.zeros_like(acc)
        qv = q_ref[...]

        def step(j, width, masked):
            keys = pl.ds(pl.multiple_of(j * T, T), width * T)
            s = _dot_nt(qv, k_ref[keys, :])
            if masked:
                s = jnp.where(_diag_mask(T), s, -jnp.inf)
            m_prev = m_s[...]
            m_new = jnp.maximum(m_prev, jnp.max(s, axis=1, keepdims=True))
            alpha = jnp.exp(m_prev - m_new)
            p = jnp.exp(s - m_new[:, :1])
            l_s[...] = alpha * l_s[...] + jnp.sum(p, axis=1, keepdims=True)
            acc[...] = alpha * acc[...] + _dot(p.astype(BF16), v_ref[keys, :])
            m_s[...] = m_new

        _walk_wide(0, i, lambda j, width: step(j, width, False))
        step(i, 1, True)
        o_ref[...] = acc[...] / l_s[...]
        lse_ref[0] = (m_s[...] + jnp.log(l_s[...]))[:, :1]

    res = pl.pallas_call(
        _carried(body, 3, 2, 3, carry, grid), name="flash_fwd", grid=grid,
        in_specs=[pl.BlockSpec((T, QK_PAD), lambda h, i: (i, h)),
                  pl.BlockSpec((S, QK_PAD), lambda h, i: (0, h)),
                  pl.BlockSpec((S, V_DIM), lambda h, i: (0, h))] + c_in_specs,
        out_specs=[pl.BlockSpec((T, V_DIM), lambda h, i: (i, h)),
                   pl.BlockSpec((1, T, 1), lambda h, i: (h, i, 0))] + c_out_specs,
        out_shape=[jax.ShapeDtypeStruct((S, n_heads * V_DIM), F32),
                   jax.ShapeDtypeStruct((n_heads, S, 1), F32)] + c_out_shapes,
        scratch_shapes=[pltpu.VMEM((T, V_DIM), F32), pltpu.VMEM((T, V_DIM), F32), pltpu.VMEM((T, V_DIM), F32)] + c_sems,
        compiler_params=_params(("arbitrary", "arbitrary")),
    )(q, k, v, *c_args)
    return res[0], res[1], list(res[2:])


def _flash_bwd_dq(q, k, v, do, lse, delta, n_heads):
    S = q.shape[0]
    T = min(FLASH_T, S)

    def body(q_ref, k_ref, v_ref, do_ref, lse_ref, dl_ref, dq_ref, acc):
        i = pl.program_id(1)
        acc[...] = jnp.zeros_like(acc)
        qv = q_ref[...]
        dov = do_ref[...]
        lse_c = lse_ref[0]
        dl_c = dl_ref[0]

        def step(j, width, masked):
            keys = pl.ds(pl.multiple_of(j * T, T), width * T)
            kt = k_ref[keys, :]
            s = _dot_nt(qv, kt)
            if masked:
                s = jnp.where(_diag_mask(T), s, -jnp.inf)
            p = jnp.exp(s - lse_c)
            ds = p * (_dot_nt(dov, v_ref[keys, :]) - dl_c)
            acc[...] += _dot(ds.astype(BF16), kt)

        _walk_wide(0, i, lambda j, width: step(j, width, False), joint=False)
        step(i, 1, True)
        dq_ref[...] = acc[...] * _attn_scale()

    return pl.pallas_call(
        body, name="flash_bwd_dq", grid=(n_heads, S // T),
        in_specs=[pl.BlockSpec((T, QK_PAD), lambda h, i: (i, h)),
                  pl.BlockSpec((S, QK_PAD), lambda h, i: (0, h)),
                  pl.BlockSpec((S, V_DIM), lambda h, i: (0, h)),
                  pl.BlockSpec((T, V_DIM), lambda h, i: (i, h)),
                  pl.BlockSpec((1, T, 1), lambda h, i: (h, i, 0)),
                  pl.BlockSpec((1, T, 1), lambda h, i: (h, i, 0))],
        out_specs=pl.BlockSpec((T, QK_PAD), lambda h, i: (i, h)),
        out_shape=jax.ShapeDtypeStruct((S, n_heads * QK_PAD), F32),
        scratch_shapes=[pltpu.VMEM((T, QK_PAD), F32)],
        compiler_params=_params(("parallel", "arbitrary")),
    )(q, k, v, do, lse, delta)


def _flash_bwd_dkv(q, k, v, do, lse_row, delta_row, n_heads, carry=None):
    S = q.shape[0]
    T = min(FLASH_T, S)
    nq = S // T
    grid = (n_heads, S // T)
    c_args, c_in_specs, c_out_specs, c_out_shapes, c_sems = _carry_call(carry)

    def body(q_ref, k_ref, v_ref, do_ref, lse_ref, dl_ref, dk_ref, dv_ref, dk_acc, dv_acc):
        j = pl.program_id(1)
        dk_acc[...] = jnp.zeros_like(dk_acc)
        dv_acc[...] = jnp.zeros_like(dv_acc)
        kv = k_ref[...]
        vv = v_ref[...]

        def step(i, width, masked):
            cols = pl.ds(pl.multiple_of(i * T, T), width * T)
            qt = q_ref[cols, :]
            dot = do_ref[cols, :]
            s_t = _dot_nt(kv, qt)
            if masked:
                s_t = jnp.where(_diag_mask(T, keys_first=True), s_t, -jnp.inf)
            p_t = jnp.exp(s_t - lse_ref[0, :, cols])
            dv_acc[...] += _dot(p_t.astype(BF16), dot)
            ds_t = p_t * (_dot_nt(vv, dot) - dl_ref[0, :, cols])
            dk_acc[...] += _dot(ds_t.astype(BF16), qt)

        step(j, 1, True)
        _walk_wide(j + 1, nq, lambda i, width: step(i, width, False), joint=False)
        dk_ref[...] = dk_acc[...]
        dv_ref[...] = dv_acc[...].astype(dv_ref.dtype)

    res = pl.pallas_call(
        _carried(body, 6, 2, 2, carry, grid), name="flash_bwd_dkv", grid=grid,
        in_specs=[pl.BlockSpec((S, QK_PAD), lambda h, j: (0, h)),
                  pl.BlockSpec((T, QK_PAD), lambda h, j: (j, h)),
                  pl.BlockSpec((T, V_DIM), lambda h, j: (j, h)),
                  pl.BlockSpec((S, V_DIM), lambda h, j: (0, h)),
                  pl.BlockSpec((1, 1, S), lambda h, j: (h, 0, 0)),
                  pl.BlockSpec((1, 1, S), lambda h, j: (h, 0, 0))] + c_in_specs,
        out_specs=[pl.BlockSpec((T, QK_PAD), lambda h, j: (j, h)),
                   pl.BlockSpec((T, V_DIM), lambda h, j: (j, h))] + c_out_specs,
        out_shape=[jax.ShapeDtypeStruct((S, n_heads * QK_PAD), F32),
                   jax.ShapeDtypeStruct((S, n_heads * V_DIM), BF16)] + c_out_shapes,
        scratch_shapes=[pltpu.VMEM((T, QK_PAD), F32), pltpu.VMEM((T, V_DIM), F32)] + c_sems,
        compiler_params=_params(("arbitrary", "arbitrary")),
    )(q, k, v, do, lse_row, delta_row, *c_args)
    return res[0], res[1], list(res[2:])


def _st_delta(n_heads):
    def fn(do, o):
        prod = do * o
        lane = lax.broadcasted_iota(jnp.int32, (do.shape[0], LANES), 1)
        out = jnp.zeros((do.shape[0], LANES), F32)
        for h in range(n_heads):
            out = out + jnp.where(lane == h, jnp.sum(prod[:, h * V_DIM:(h + 1) * V_DIM], axis=1, keepdims=True), 0.0)
        return out
    return fn


def _pad_cols(w, width):
    return jnp.pad(w, ((0, 0), (0, width - w.shape[1])))


def _dims(x, p, q_norm, kv_norm, w_uq, dt_bias, ssm_norm, conv_b):
    d = dict(S=x.shape[0], D=x.shape[1], PLE=p.shape[1], DQ=q_norm.shape[1], DKV=kv_norm.shape[1],
             NH=w_uq.shape[1] // (QK_NOPE + QK_ROPE), NHS=dt_bias.shape[1], DI=ssm_norm.shape[1],
             CONV=conv_b.shape[1])
    d["G"] = (d["CONV"] - d["DI"]) // (2 * D_STATE)
    d["R"] = d["NHS"] // d["G"]
    return d


def _assemble(name, blocks):
    rows, cols = blocks.shape[1:]
    if name in COL_SHARDED:
        return blocks.transpose(1, 0, 2).reshape(rows, N_DEV * cols)
    return blocks.reshape(N_DEV * rows, cols)


def _by_device(name, g):
    if name in COL_SHARDED:
        return g.reshape(g.shape[0], N_DEV, g.shape[1] // N_DEV).transpose(1, 0, 2)
    return g.reshape(N_DEV, g.shape[0] // N_DEV, g.shape[1])


def _local_step(x, p, positions, target, small, conv_w, wts, late_names=(), late_shards=(), exchange=True):
    wts = dict(wts)
    dm = _dims(x, p, small["q_norm"], small["kv_norm"], wts["w_uq"], small["dt_bias"], small["ssm_norm"],
               small["conv_b"])
    S, D, DQ, DKV, NH, NHS, DI, CONV, G, R = (dm[k] for k in ("S", "D", "DQ", "DKV", "NH", "NHS", "DI", "CONV", "G", "R"))
    rp = -(-R // SUBLANES) * SUBLANES
    L = min(CHUNK, S)

    w_in = wts["w_in"]
    o = [0]
    for n in (DQ, DKV, QK_ROPE, DI, CONV, NHS, D, D):
        o.append(o[-1] + n)
    w_cqkv, w_kr, w_z, w_xbc, w_dt, w_g = (w_in[:, o[0]:o[2]], w_in[:, o[2]:o[3]], w_in[:, o[3]:o[4]],
                                           w_in[:, o[4]:o[5]], w_in[:, o[5]:o[6]], w_in[:, o[6]:o[8]])
    zc = lambda n: jnp.zeros((D, n), BF16)
    w_sm = jnp.concatenate([zc(QK_NOPE), w_kr, zc(QK_PAD - QK_NOPE - QK_ROPE), w_dt, zc(LANES - NHS)], axis=1)
    w_q = jnp.pad(wts["w_uq"].reshape(DQ, NH, QK_NOPE + QK_ROPE),
                  ((0, 0), (0, 0), (0, QK_PAD - QK_NOPE - QK_ROPE))).reshape(DQ, NH * QK_PAD)
    ukv = wts["w_ukv"].reshape(DKV, NH, QK_NOPE + V_DIM)
    w_k = jnp.pad(ukv[:, :, :QK_NOPE], ((0, 0), (0, 0), (0, QK_PAD - QK_NOPE))).reshape(DKV, NH * QK_PAD)
    w_v = ukv[:, :, QK_NOPE:].reshape(DKV, NH * V_DIM)
    dt_bias_p, a_log_p = _pad_cols(small["dt_bias"], LANES), _pad_cols(small["a_log"], LANES)
    dskip_rep = jnp.repeat(small["d_skip"], SSM_HEADDIM, axis=1)
    c_tab, a_tab, b_tab = _rope_tables(positions)

    (u,) = _rowwise("pre_norm", _st_pre, [x], [small["mix_norm_pre"]], [(D, BF16)])
    cqkv = _mm("in_cqkv", u, w_cqkv, "nn")
    z = _mm("in_z", u, w_z, "nn")
    xbc = _mm("in_xbc", u, w_xbc, "nn")
    g = _mm("in_gates", u, w_g, "nn")
    sm = _mm("in_small", u, w_sm, "nn")

    lora_fn = _st_lora_norms(DQ)
    cq_n, ckv_n = _rowwise("lora_norms", lora_fn, [cqkv], [small["q_norm"], small["kv_norm"]], [(DQ, BF16), (DKV, BF16)])
    qraw = _mm("up_q", cq_n, w_q, "nn")
    kraw = _mm("up_k", ckv_n, w_k, "nn")
    v = _mm("up_v", ckv_n, w_v, "nn", out_dtype=BF16)
    q, k = _rowwise("rope", _st_rope(NH), [qraw, kraw, sm, c_tab, a_tab, b_tab], [],
                    [(NH * QK_PAD, BF16), (NH * QK_PAD, BF16)])
    attn, lse, late_blocks = _flash_fwd(q, k, v, NH, carry=_gather_carry(list(late_shards)) if late_names else None)
    wts.update({n: _assemble(n, b) for n, b in zip(late_names, late_blocks)})

    xbc_c, xbc_a = _conv_fwd(xbc, conv_w, small["conv_b"])
    dt, da = _rowwise("dt", _st_dt, [sm], [dt_bias_p, a_log_p], [(LANES, F32), (LANES, F32)])

    def col_layout(t):
        return _pad_cols(t[:, :NHS].reshape(S, G, R).transpose(1, 0, 2).reshape(G * S, R), LANES).reshape(G, S, LANES)

    def row_layout(t):
        return jnp.pad(t[:, :NHS].reshape(S, G, R).transpose(1, 2, 0), ((0, 0), (0, rp - R), (0, 0)))

    dt_row, da_col, da_row = row_layout(dt), col_layout(da), row_layout(da)
    xs = xbc_a[:, :DI]
    y, states = _ssd_fwd_t(xbc_a, dt_row, da_col, da_row, DI, G, R)
    gn_fn = _st_gated_norm(G)
    (ssm,) = _rowwise("gated_norm", gn_fn, [y, xs, z], [dskip_rep, small["ssm_norm"]], [(DI, BF16)])

    ao = _mm("attn_o", attn, wts["w_attn_o"], "nn")
    so = _mm("ssm_o", ssm, wts["w_ssm_o"], "nn")
    mix_fn = _st_mix(D)
    (mixed,) = _rowwise("mix", mix_fn, [g, ao, so], [], [(D, BF16)])
    mo = _mm("out_proj", mixed, wts["w_out"], "nn")
    h1, f = _rowwise("res1", _st_res_norm, [x, mo], [small["mix_norm_post"], small["ffn_norm_pre"]], [(D, F32), (D, BF16)])
    gt = _mm("ffn_gate", f, wts["w_gate"], "nn")
    up = _mm("ffn_up", f, wts["w_up"], "nn")
    (act,) = _rowwise("swiglu", _st_swiglu, [gt, up], [], [(gt.shape[1], BF16)])
    dn = _mm("ffn_down", act, wts["w_down"], "nn")
    h2, a3 = _rowwise("res2", _st_res_norm, [h1, dn], [small["ffn_norm_post"], small["ple_norm_pre"]], [(D, F32), (D, BF16)])
    gl = _mm("ple_gate", a3, wts["w_ple_gate"], "nn")
    pe = _mm("ple_proj", p, wts["w_ple"], "nn")

    sg = {}
    bg = {}
    dpe, dgl, dh2, d_w, loss_acc = _rowwise_bwd(
        "loss", _st_loss, [pe, gl, h2, target], [small["ple_norm_post"]], [1.0], [0, 1, 2], [BF16, BF16, F32], fwd_sums=(0,))
    sg["ple_norm_post"] = _fold(d_w)
    loss = loss_acc[0:1, :]
    bg["w_ple"] = _mm("d_w_ple", p, dpe, "tn", out_dtype=BF16)
    bg["w_ple_gate"] = _mm("d_w_ple_gate", a3, dgl, "tn", out_dtype=BF16)
    da3 = _mm("d_a3", dgl, wts["w_ple_gate"], "nt")

    dh1, ddn, d_post, d_pre = _rowwise_bwd(
        "res2_bwd", _st_res_norm, [h1, dn], [small["ffn_norm_post"], small["ple_norm_pre"]], [dh2, da3], [0, 1], [F32, BF16])
    sg["ffn_norm_post"], sg["ple_norm_pre"] = _fold(d_post), _fold(d_pre)
    bg["w_down"] = _mm("d_w_down", act, ddn, "tn", out_dtype=BF16)
    dact = _mm("d_act", ddn, wts["w_down"], "nt")
    dgt, dup = _rowwise_bwd("swiglu_bwd", _st_swiglu, [gt, up], [], [dact], [0, 1], [BF16, BF16])
    bg["w_gate"] = _mm("d_w_gate", f, dgt, "tn", out_dtype=BF16)
    bg["w_up"] = _mm("d_w_up", f, dup, "tn", out_dtype=BF16)
    df = _mm("d_f_gate", dgt, wts["w_gate"], "nt")
    df = _mm("d_f_up", dup, wts["w_up"], "nt", acc_in=df)

    dx_res, dmo, d_post, d_pre = _rowwise_bwd(
        "res1_bwd", _st_res_norm, [x, mo], [small["mix_norm_post"], small["ffn_norm_pre"]], [dh1, df], [0, 1], [F32, BF16])
    sg["mix_norm_post"], sg["ffn_norm_pre"] = _fold(d_post), _fold(d_pre)
    bg["w_out"] = _mm("d_w_out", mixed, dmo, "tn", out_dtype=BF16)
    dmixed = _mm("d_mixed", dmo, wts["w_out"], "nt")
    dg, dao, dso = _rowwise_bwd("mix_bwd", mix_fn, [g, ao, so], [], [dmixed], [0, 1, 2], [BF16, BF16, BF16])
    bg["w_attn_o"] = _mm("d_w_attn_o", attn, dao, "tn", out_dtype=BF16)
    bg["w_ssm_o"] = _mm("d_w_ssm_o", ssm, dso, "tn", out_dtype=BF16)
    dattn = _mm("d_attn", dao, wts["w_attn_o"], "nt", out_dtype=BF16)
    dssm = _mm("d_ssm", dso, wts["w_ssm_o"], "nt")

    dy, dxs_a, dz, d_dskip, d_ssmn = _rowwise_bwd(
        "gated_norm_bwd", gn_fn, [y, xs, z], [dskip_rep, small["ssm_norm"]], [dssm], [0, 1, 2], [F32, F32, BF16])
    sg["d_skip"] = _fold(d_dskip).reshape(NHS, SSM_HEADDIM).sum(axis=1).reshape(1, NHS)
    sg["ssm_norm"] = _fold(d_ssmn)
    dxs_b, d_b, d_c, ddt_row, dcum_row = _ssd_bwd_t(xbc_a, dt_row, da_col, da_row, states, dy, DI, G, R)

    def from_row(t):
        return _pad_cols(t[:, :R, :].transpose(2, 0, 1).reshape(S, NHS), LANES)

    ddtraw, d_bias, d_alog = _rowwise("dt_bwd", _st_dt_bwd, [sm, from_row(ddt_row), from_row(dcum_row)],
                                      [dt_bias_p, a_log_p], [(LANES, F32)], accs=(LANES, LANES), tr=L)
    sg["dt_bias"], sg["a_log"] = _fold(d_bias)[:, :NHS], _fold(d_alog)[:, :NHS]
    dconv, d_cb = _rowwise("dconv", _st_dconv(DI), [xbc_c, dxs_a, dxs_b, d_b, d_c], [], [(CONV, F32)], accs=(CONV,))
    sg["conv_b"] = _fold(d_cb)
    dxbc, d_cw = _conv_bwd(xbc, dconv, conv_w)
    d_conv_w = d_cw.reshape(CONV_WIDTH, SUBLANES, CONV).sum(axis=1)

    (delta,) = _rowwise("attn_delta", _st_delta(NH), [dattn, attn], [], [(LANES, F32)])
    delta = delta[:, :NH].T
    dq = _flash_bwd_dq(q, k, v, dattn, lse, delta.reshape(NH, S, 1), NH)
    late_sent = [_by_device(n, bg.pop(n)) for n in late_names]
    dk, dv, late_recv = _flash_bwd_dkv(q, k, v, dattn, lse.reshape(NH, 1, S), delta.reshape(NH, 1, S), NH,
                                       carry=_scatter_carry(late_sent) if late_names else None)
    dqraw, dkr, dk = _rowwise("rope_bwd", _st_rope_bwd(NH), [dq, dk, c_tab, a_tab, b_tab], [],
                              [(NH * QK_PAD, BF16), (QK_PAD, F32), (NH * QK_PAD, BF16)])
    d_w_q = _mm("d_w_q", cq_n, dqraw, "tn", out_dtype=BF16)
    d_w_k = _mm("d_w_k", ckv_n, dk, "tn", out_dtype=BF16)
    d_w_v = _mm("d_w_v", ckv_n, dv, "tn", out_dtype=BF16)
    dcq_n = _mm("d_cq_n", dqraw, w_q, "nt")
    dckv_n = _mm("d_ckv_n_k", dk, w_k, "nt")
    dckv_n = _mm("d_ckv_n_v", dv, w_v, "nt", acc_in=dckv_n)
    bg["w_uq"] = d_w_q.reshape(DQ, NH, QK_PAD)[:, :, :QK_NOPE + QK_ROPE].reshape(DQ, NH * (QK_NOPE + QK_ROPE))
    bg["w_ukv"] = jnp.concatenate([d_w_k.reshape(DKV, NH, QK_PAD)[:, :, :QK_NOPE], d_w_v.reshape(DKV, NH, V_DIM)],
                                  axis=2).reshape(DKV, NH * (QK_NOPE + V_DIM))
    dcqkv, d_qn, d_kvn = _rowwise_bwd("lora_norms_bwd", lora_fn, [cqkv], [small["q_norm"], small["kv_norm"]],
                                      [dcq_n, dckv_n], [0], [BF16])
    sg["q_norm"], sg["kv_norm"] = _fold(d_qn), _fold(d_kvn)

    dproj = jnp.concatenate([dcqkv, dkr[:, QK_NOPE:QK_NOPE + QK_ROPE].astype(BF16), dz, dxbc,
                             ddtraw[:, :NHS].astype(BF16), dg], axis=1)
    bg["w_in"] = _mm("d_w_in", u, dproj, "tn", out_dtype=BF16)
    rest_names = tuple(n for n in BIG if n in bg)
    if not exchange:
        du = _mm("d_u", dproj, w_in, "nt")
        grad_x, d_pre = _rowwise_bwd("pre_norm_bwd", _st_pre, [x], [small["mix_norm_pre"]], [du], [0], [F32],
                                     adds={0: dx_res})
        sg["mix_norm_pre"] = _fold(d_pre)
        return loss, grad_x, sg, d_conv_w, bg, {}
    rest_sent = [_by_device(n, bg.pop(n)) for n in rest_names]
    du, rest_recv = _mm("d_u", dproj, w_in, "nt", carry=_scatter_carry(rest_sent))
    grad_x, d_pre = _rowwise_bwd("pre_norm_bwd", _st_pre, [x], [small["mix_norm_pre"]], [du], [0], [F32], adds={0: dx_res})
    sg["mix_norm_pre"] = _fold(d_pre)
    sent = dict(zip(late_names, late_sent), **dict(zip(rest_names, rest_sent)))
    recv = dict(zip(late_names, late_recv), **dict(zip(rest_names, rest_recv)))
    return loss, grad_x, sg, d_conv_w, sent, recv


_HBM = pl.BlockSpec(memory_space=pltpu.HBM)
_FLIPS = ((0, 0, 1), (1, 0, 0), (0, 1, 0), (1, 1, 0), (1, 0, 1), (0, 1, 1), (1, 1, 1))


def _place():
    return lax.axis_index("x"), lax.axis_index("y"), lax.axis_index("c")


def _flipped(place, flip):
    return tuple(1 - v if f else v for v, f in zip(place, flip))


def _gather_carry(blocks):
    nw = len(blocks)

    def copies(ins, outs, send_sems, recv_sems, local_sems):
        x, y, c = _place()
        me = 4 * x + 2 * y + c
        cps = []
        for w in range(nw):
            cps.append(pltpu.make_async_copy(ins[w], outs[w].at[me], local_sems.at[w]))
            for k, flip in enumerate(_FLIPS):
                cps.append(pltpu.make_async_remote_copy(
                    src_ref=ins[w], dst_ref=outs[w].at[me], send_sem=send_sems.at[7 * w + k],
                    recv_sem=recv_sems.at[7 * w + k], device_id=_flipped((x, y, c), flip), device_id_type=MESH))
        return cps

    return dict(ins=blocks, outs=[jax.ShapeDtypeStruct((N_DEV,) + b.shape, b.dtype) for b in blocks],
                n_remote=7 * nw, n_local=nw, copies=copies)


def _scatter_carry(by_dev):
    nw = len(by_dev)

    def copies(ins, outs, send_sems, recv_sems, local_sems):
        x, y, c = _place()
        cps = []
        for w in range(nw):
            for k, flip in enumerate(_FLIPS):
                px, py, pc = _flipped((x, y, c), flip)
                cps.append(pltpu.make_async_remote_copy(
                    src_ref=ins[w].at[4 * px + 2 * py + pc], dst_ref=outs[w].at[k], send_sem=send_sems.at[7 * w + k],
                    recv_sem=recv_sems.at[7 * w + k], device_id=(px, py, pc), device_id_type=MESH))
        return cps

    return dict(ins=by_dev, outs=[jax.ShapeDtypeStruct((7,) + b.shape[1:], b.dtype) for b in by_dev],
                n_remote=7 * nw, n_local=0, copies=copies)


def _all_gather(name, blocks):
    nw = len(blocks)

    def body(*refs):
        x_refs, out_refs = refs[:nw], refs[nw:2 * nw]
        send_sems, recv_sems, local_sems = refs[2 * nw:]
        x, y, c = _place()
        me, sibling = (x, y, c), (x, y, 1 - c)
        chips = [(1 - x, y), (x, 1 - y), (1 - x, 1 - y)]

        def slot(w, px, py, pc):
            return out_refs[w].at[4 * px + 2 * py + pc]

        def copy(w, k, blk, to, src=None):
            return pltpu.make_async_remote_copy(
                src_ref=slot(w, *blk) if src is None else src, dst_ref=slot(w, *blk),
                send_sem=send_sems.at[7 * w + k], recv_sem=recv_sems.at[7 * w + k], device_id=to, device_id_type=MESH)

        mine = [pltpu.make_async_copy(x_refs[w], slot(w, *me), local_sems.at[w]) for w in range(nw)]
        for cp in mine:
            cp.start()
        first = []
        for w in range(nw):
            first.append(copy(w, 0, me, sibling, src=x_refs[w]))
            first += [copy(w, 1 + j, me, (*chip, c), src=x_refs[w]) for j, chip in enumerate(chips)]
        for cp in first:
            cp.start()
        passed = []
        for j, chip in enumerate(chips):
            for w in range(nw):
                copy(w, 1 + j, (*chip, c), me).wait_recv()
                passed.append(copy(w, 4 + j, (*chip, c), sibling))
                passed[-1].start()
        for w in range(nw):
            copy(w, 0, sibling, me).wait_recv()
        for j, chip in enumerate(chips):
            for w in range(nw):
                copy(w, 4 + j, (*chip, 1 - c), me).wait_recv()
        for cp in first + passed:
            cp.wait_send()
        for cp in mine:
            cp.wait()

    return pl.pallas_call(
        body, name=name, out_shape=[jax.ShapeDtypeStruct((N_DEV,) + b.shape, b.dtype) for b in blocks],
        in_specs=[_HBM] * nw, out_specs=[_HBM] * nw,
        scratch_shapes=[pltpu.SemaphoreType.DMA((7 * nw,)), pltpu.SemaphoreType.DMA((7 * nw,)),
                        pltpu.SemaphoreType.DMA((nw,))],
    )(*blocks)


def _swap_with_sibling(name, by_dev):
    nw = len(by_dev)

    def body(*refs):
        g_refs, got_refs = refs[:nw], refs[nw:2 * nw]
        send_sems, recv_sems = refs[2 * nw:]
        x, y, c = _place()
        cps = [pltpu.make_async_remote_copy(src_ref=g_refs[w].at[2 * q + (1 - c)], dst_ref=got_refs[w].at[q],
                                            send_sem=send_sems.at[4 * w + q], recv_sem=recv_sems.at[4 * w + q],
                                            device_id=(x, y, 1 - c), device_id_type=MESH)
               for w in range(nw) for q in range(4)]
        for cp in cps:
            cp.start()
        for cp in cps:
            cp.wait()

    return pl.pallas_call(
        body, name=name, out_shape=[jax.ShapeDtypeStruct((4,) + b.shape[1:], b.dtype) for b in by_dev],
        in_specs=[_HBM] * nw, out_specs=[_HBM] * nw,
        scratch_shapes=[pltpu.SemaphoreType.DMA((4 * nw,)), pltpu.SemaphoreType.DMA((4 * nw,))],
    )(*by_dev)


def _swap_with_chips(name, sends):
    nw = len(sends)

    def body(*refs):
        s_refs, r_refs = refs[:nw], refs[nw:2 * nw]
        send_sems, recv_sems = refs[2 * nw:]
        x, y, c = _place()
        chips = [(1 - x, y), (x, 1 - y), (1 - x, 1 - y)]
        cps = [pltpu.make_async_remote_copy(src_ref=s_refs[w].at[2 * px + py], dst_ref=r_refs[w].at[k],
                                            send_sem=send_sems.at[3 * w + k], recv_sem=recv_sems.at[3 * w + k],
                                            device_id=(px, py, c), device_id_type=MESH)
               for w in range(nw) for k, (px, py) in enumerate(chips)]
        for cp in cps:
            cp.start()
        for cp in cps:
            cp.wait()

    return pl.pallas_call(
        body, name=name, out_shape=[jax.ShapeDtypeStruct((3,) + s.shape[1:], s.dtype) for s in sends],
        in_specs=[_HBM] * nw, out_specs=[_HBM] * nw,
        scratch_shapes=[pltpu.SemaphoreType.DMA((3 * nw,)), pltpu.SemaphoreType.DMA((3 * nw,))],
    )(*sends)


def _lane_pad(n):
    return -(-n // LANES) * LANES


def _pack_small(vecs, mat):
    width = max(sum(_lane_pad(v.shape[1]) for v in vecs), _lane_pad(mat.shape[1]))
    row0 = jnp.concatenate([_pad_cols(v, _lane_pad(v.shape[1])) for v in vecs], axis=1)
    rows = jnp.concatenate([_pad_cols(row0, width), _pad_cols(mat, width)], axis=0)
    return jnp.pad(rows, ((0, SUBLANES - rows.shape[0]), (0, 0)))


def _unpack_small(packed, sizes, mat_cols):
    vecs, off = [], 0
    for n in sizes:
        vecs.append(packed[0:1, off:off + n])
        off += _lane_pad(n)
    return vecs, packed[1:1 + CONV_WIDTH, :mat_cols]


def _adamw(w, g, m, v):
    m = ADAM_B1 * m + (1.0 - ADAM_B1) * g
    v = ADAM_B2 * v + (1.0 - ADAM_B2) * (g * g)
    m_hat = m / (1.0 - ADAM_B1 ** ADAM_STEP)
    v_hat = v / (1.0 - ADAM_B2 ** ADAM_STEP)
    delta = -ADAM_LR * (m_hat / (jnp.sqrt(v_hat) + ADAM_EPS) + ADAM_WD * w)
    return delta, m, v


BIG = ("w_in", "w_uq", "w_ukv", "w_attn_o", "w_ssm_o", "w_out", "w_gate", "w_up", "w_down", "w_ple_gate", "w_ple")
FIRST = ("w_in", "w_uq", "w_ukv")
LATE = ("w_attn_o", "w_ssm_o", "w_out", "w_gate", "w_up", "w_down", "w_ple_gate", "w_ple")
COL_SHARDED = ("w_in", "w_uq", "w_ukv", "w_gate", "w_up", "w_ple")
SMALL = ("mix_norm_pre", "mix_norm_post", "q_norm", "kv_norm", "conv_b", "dt_bias", "a_log", "d_skip", "ssm_norm",
         "ffn_norm_pre", "ffn_norm_post", "ple_norm_pre", "ple_norm_post")
WEIGHTS = ("mix_norm_pre", "mix_norm_post", "w_in", "q_norm", "w_uq", "kv_norm", "w_ukv", "conv_w", "conv_b", "dt_bias",
           "a_log", "d_skip", "ssm_norm", "w_attn_o", "w_ssm_o", "w_out", "ffn_norm_pre", "ffn_norm_post", "w_gate",
           "w_up", "w_down", "ple_norm_pre", "ple_norm_post", "w_ple_gate", "w_ple")


def _step(x, p, positions, target, w, m, v):
    xi, yi, ci = _place()
    me = 4 * xi + 2 * yi + ci
    chip = 2 * xi + yi

    gathered = _all_gather("gather_weights", [w[n].astype(BF16) for n in FIRST])
    wts = {n: _assemble(n, blocks) for n, blocks in zip(FIRST, gathered)}
    cw_rows, cw_cols = w["conv_w"].shape
    (cw_all,) = _all_gather("gather_conv_w", [jnp.pad(w["conv_w"], ((0, SUBLANES - cw_rows), (0, 0)))])
    conv_w = cw_all[:, :cw_rows, :].transpose(1, 0, 2).reshape(cw_rows, N_DEV * cw_cols)

    small = {n: w[n] for n in SMALL}
    loss, grad_x, sg, d_conv_w, sent, recv = _local_step(
        x, p, positions, target, small, conv_w, wts, LATE, [w[n].astype(BF16) for n in LATE])

    sizes = [w[n].shape[1] for n in SMALL]
    sg_pack = _pack_small([sg[n] for n in SMALL] + [loss], d_conv_w)
    (sg_all,) = _all_gather("gather_small_grads", [sg_pack])
    (sg_sum,) = _rowwise("sum_small_grads", lambda *a: functools.reduce(lambda s, t: s + t, a),
                         [sg_all[k] for k in range(N_DEV)], [], [(sg_pack.shape[1], F32)])
    sg_vecs, d_conv_w_sum = _unpack_small(sg_sum, sizes + [LANES], d_conv_w.shape[1])
    loss = sg_vecs[-1][0, 0]
    grads = dict(zip(SMALL, sg_vecs[:-1]))
    grads["conv_w"] = lax.dynamic_slice_in_dim(d_conv_w_sum, me * cw_cols, cw_cols, axis=1)

    def sum8_then_adamw(wv, mv, vv, own, *others):
        g = functools.reduce(lambda s, t: s + t, others, own)
        return (g,) + _adamw(wv, g, mv, vv)

    delta, new_m, new_v = {}, {}, {}
    for n in BIG:
        cols = w[n].shape[1]
        own = lax.dynamic_index_in_dim(sent[n], me, axis=0, keepdims=False)
        grads[n], delta[n], new_m[n], new_v[n] = _rowwise(
            "adamw_" + n, sum8_then_adamw, [w[n], m[n], v[n], own] + [recv[n][k] for k in range(N_DEV - 1)], [],
            [(cols, F32)] * 4)
    packed = [_pack_small([d[n] for n in SMALL], d["conv_w"]) for d in (w, grads, m, v)]
    outs = _rowwise("adamw_small", _adamw, packed, [], [(packed[0].shape[1], F32)] * 3)
    for d, o in zip((delta, new_m, new_v), outs):
        vecs, mat = _unpack_small(o, sizes, cw_cols)
        d.update(zip(SMALL, vecs))
        d["conv_w"] = mat
    return loss, grad_x, grads, delta, new_m, new_v


def kernel(x, p, positions, mix_norm_pre, mix_norm_post, w_in, q_norm, w_uq, kv_norm, w_ukv, conv_w, conv_b, dt_bias, a_log, d_skip, ssm_norm, w_attn_o, w_ssm_o, w_out, ffn_norm_pre, ffn_norm_post, w_gate, w_up, w_down, ple_norm_pre, ple_norm_post, w_ple_gate, w_ple, loss_target, m_mix_norm_pre, m_mix_norm_post, m_w_in, m_q_norm, m_w_uq, m_kv_norm, m_w_ukv, m_conv_w, m_conv_b, m_dt_bias, m_a_log, m_d_skip, m_ssm_norm, m_w_attn_o, m_w_ssm_o, m_w_out, m_ffn_norm_pre, m_ffn_norm_post, m_w_gate, m_w_up, m_w_down, m_ple_norm_pre, m_ple_norm_post, m_w_ple_gate, m_w_ple, v_mix_norm_pre, v_mix_norm_post, v_w_in, v_q_norm, v_w_uq, v_kv_norm, v_w_ukv, v_conv_w, v_conv_b, v_dt_bias, v_a_log, v_d_skip, v_ssm_norm, v_w_attn_o, v_w_ssm_o, v_w_out, v_ffn_norm_pre, v_ffn_norm_post, v_w_gate, v_w_up, v_w_down, v_ple_norm_pre, v_ple_norm_post, v_w_ple_gate, v_w_ple):
    w_args = (mix_norm_pre, mix_norm_post, w_in, q_norm, w_uq, kv_norm, w_ukv, conv_w, conv_b, dt_bias, a_log, d_skip, ssm_norm, w_attn_o, w_ssm_o, w_out, ffn_norm_pre, ffn_norm_post, w_gate, w_up, w_down, ple_norm_pre, ple_norm_post, w_ple_gate, w_ple)
    m_args = (m_mix_norm_pre, m_mix_norm_post, m_w_in, m_q_norm, m_w_uq, m_kv_norm, m_w_ukv, m_conv_w, m_conv_b, m_dt_bias, m_a_log, m_d_skip, m_ssm_norm, m_w_attn_o, m_w_ssm_o, m_w_out, m_ffn_norm_pre, m_ffn_norm_post, m_w_gate, m_w_up, m_w_down, m_ple_norm_pre, m_ple_norm_post, m_w_ple_gate, m_w_ple)
    v_args = (v_mix_norm_pre, v_mix_norm_post, v_w_in, v_q_norm, v_w_uq, v_kv_norm, v_w_ukv, v_conv_w, v_conv_b, v_dt_bias, v_a_log, v_d_skip, v_ssm_norm, v_w_attn_o, v_w_ssm_o, v_w_out, v_ffn_norm_pre, v_ffn_norm_post, v_w_gate, v_w_up, v_w_down, v_ple_norm_pre, v_ple_norm_post, v_w_ple_gate, v_w_ple)

    def drop_layer(a):
        return a if a.ndim == 2 else a[0]

    w = {n: drop_layer(a) for n, a in zip(WEIGHTS, w_args)}
    m = {n: drop_layer(a) for n, a in zip(WEIGHTS, m_args)}
    v = {n: drop_layer(a) for n, a in zip(WEIGHTS, v_args)}
    loss, grad_x, grads, delta, new_m, new_v = _step(x[0], p[0, 0], positions[0], loss_target[0], w, m, v)
    like = lambda d: [d[n].reshape(a.shape) for n, a in zip(WEIGHTS, w_args)]
    return (loss, grad_x[None], *like(grads), *like(delta), *like(new_m), *like(new_v))
```

```python
import functools

import jax
import jax.numpy as jnp
from jax import lax
from jax.experimental import pallas as pl
from jax.experimental.pallas import tpu as pltpu

F32 = jnp.float32
BF16 = jnp.bfloat16

EPS = 1e-6
QK_NOPE = 128
QK_ROPE = 64
V_DIM = 128
QK_PAD = 256
ROPE_THETA = 10000.0
SSM_HEADDIM = 64
D_STATE = 128
CONV_WIDTH = 4
CHUNK = 256
ADAM_LR = 0.001
ADAM_B1 = 0.9
ADAM_B2 = 0.999
ADAM_EPS = 1e-08
ADAM_WD = 0.01
ADAM_STEP = 10

N_DEV = 8
LANES = 128
SUBLANES = 8
PACK_W = 1024
VMEM_LIMIT = 56 * 1024 * 1024
ROW_TILE_BYTES = 16 * 1024 * 1024
STRIP_ROWS = 16
FLASH_T = 512
MM_TILE_BYTES = 20 * 1024 * 1024
MESH = pl.DeviceIdType.MESH


def _pick(dim, prefs):
    if dim <= prefs[0]:
        return dim
    for p in prefs:
        if dim % p == 0:
            return p
    return dim


def _tile(dim, cap):
    if dim <= cap:
        return dim
    best = None
    for t in range(LANES, cap + 1, LANES):
        if dim % t == 0:
            best = t
    return best if best is not None else dim


def _params(sem):
    return pltpu.CompilerParams(dimension_semantics=sem, vmem_limit_bytes=VMEM_LIMIT)


def _dot(a, b):
    return lax.dot_general(a, b, (((1,), (0,)), ((), ())), preferred_element_type=F32)


def _dot_nt(a, b):
    return lax.dot_general(a, b, (((1,), (1,)), ((), ())), preferred_element_type=F32)


def _dot_tn(a, b):
    return lax.dot_general(a, b, (((0,), (0,)), ((), ())), preferred_element_type=F32)


def _mm(name, a, b, mode, out_dtype=F32, acc_in=None, carry=None):
    if mode == "nn":
        (M, K), (K2, N) = a.shape, b.shape
    elif mode == "nt":
        (M, K), (N, K2) = a.shape, b.shape
    else:
        (K, M), (K2, N) = a.shape, b.shape
    assert K == K2, (name, a.shape, b.shape, mode)
    tm = _tile(M, 1024)
    tn = _tile(N, 1024 if acc_in is not None else 1536)
    tk = _tile(K, 2048)
    while tk > 512 and 2 * (tm * tk * a.dtype.itemsize + tk * tn * b.dtype.itemsize) > MM_TILE_BYTES:
        tk = _tile(K, tk - LANES)
    nk = K // tk
    dot = {"nn": _dot, "nt": _dot_nt, "tn": _dot_tn}[mode]
    has_acc = acc_in is not None

    def body(*refs):
        if has_acc:
            a_ref, b_ref, c_ref, o_ref, acc = refs
        else:
            a_ref, b_ref, o_ref, acc = refs
        k = pl.program_id(2)

        @pl.when(k == 0)
        def _():
            acc[...] = jnp.zeros_like(acc)

        acc[...] += dot(a_ref[...].astype(BF16), b_ref[...].astype(BF16))

        @pl.when(k == nk - 1)
        def _():
            r = acc[...]
            if has_acc:
                r = r + c_ref[...]
            o_ref[...] = r.astype(o_ref.dtype)

    if mode == "tn":
        a_spec = pl.BlockSpec((tk, tm), lambda i, j, k: (k, i))
    else:
        a_spec = pl.BlockSpec((tm, tk), lambda i, j, k: (i, k))
    if mode == "nt":
        b_spec = pl.BlockSpec((tn, tk), lambda i, j, k: (j, k))
    else:
        b_spec = pl.BlockSpec((tk, tn), lambda i, j, k: (k, j))
    o_spec = pl.BlockSpec((tm, tn), lambda i, j, k: (i, j))
    in_specs = [a_spec, b_spec] + ([o_spec] if has_acc else [])
    args = (a, b) + ((acc_in,) if has_acc else ())
    if carry is None:
        return pl.pallas_call(
            body, name=name, grid=(M // tm, N // tn, nk), in_specs=in_specs, out_specs=o_spec,
            out_shape=jax.ShapeDtypeStruct((M, N), out_dtype), scratch_shapes=[pltpu.VMEM((tm, tn), F32)],
            input_output_aliases=({2: 0} if has_acc and out_dtype == F32 else {}),
            compiler_params=_params(("parallel", "parallel", "arbitrary")),
        )(*args)
    grid = (M // tm, N // tn, nk)
    c_args, c_in_specs, c_out_specs, c_out_shapes, c_sems = _carry_call(carry)
    res = pl.pallas_call(
        _carried(body, len(args), 1, 1, carry, grid), name=name, grid=grid, in_specs=in_specs + c_in_specs,
        out_specs=[o_spec] + c_out_specs, out_shape=[jax.ShapeDtypeStruct((M, N), out_dtype)] + c_out_shapes,
        scratch_shapes=[pltpu.VMEM((tm, tn), F32)] + c_sems,
        compiler_params=_params(("arbitrary", "arbitrary", "arbitrary")),
    )(*args, *c_args)
    return res[0], list(res[1:])


def _row_tile(n_rows, bytes_per_row):
    tr = 1024
    while tr > SUBLANES and tr * bytes_per_row > ROW_TILE_BYTES:
        tr //= 2
    while n_rows % tr:
        tr //= 2
    return tr


def _over_strips(tr, strip, work):
    if strip is None or tr <= strip or tr % strip:
        work(slice(None))
        return

    def one(s, carry):
        work(pl.ds(pl.multiple_of(s * strip, strip), strip))
        return carry

    lax.fori_loop(0, tr // strip, one, 0)


def _acc_add(a_ref, v):
    if v.shape[0] == 1:
        a_ref[0:1, :] += v
    else:
        a_ref[...] += v.reshape(v.shape[0] // SUBLANES, SUBLANES, v.shape[1]).sum(axis=0)


def _rowwise(name, fn, rows, bcs, outs, accs=(), tr=None):
    n_rows = rows[0].shape[0]
    strip = STRIP_ROWS if tr is None and n_rows % STRIP_ROWS == 0 else None
    if tr is None:
        per_row = sum(r.shape[1] * r.dtype.itemsize for r in rows) + sum(w * jnp.dtype(d).itemsize for w, d in outs)
        tr = _row_tile(n_rows, per_row)
    n_r, n_b, n_o, n_a = len(rows), len(bcs), len(outs), len(accs)

    def body(*refs):
        r_refs, b_refs = refs[:n_r], refs[n_r: n_r + n_b]
        o_refs = refs[n_r + n_b: n_r + n_b + n_o]
        a_refs = refs[n_r + n_b + n_o:]
        if n_a:
            @pl.when(pl.program_id(0) == 0)
            def _():
                for a in a_refs:
                    a[...] = jnp.zeros_like(a)

        def work(rws):
            ins = [r[rws, :].astype(F32) for r in r_refs] + [b[...].astype(F32) for b in b_refs]
            res = fn(*ins)
            res = res if isinstance(res, (tuple, list)) else (res,)
            for o, v in zip(o_refs, res[:n_o]):
                o[rws, :] = v.astype(o.dtype)
            for a, v in zip(a_refs, res[n_o:]):
                _acc_add(a, v)

        _over_strips(tr, strip, work)

    in_specs = [pl.BlockSpec((tr, r.shape[1]), lambda i: (i, 0)) for r in rows]
    in_specs += [pl.BlockSpec((1, b.shape[1]), lambda i: (0, 0)) for b in bcs]
    out_specs = [pl.BlockSpec((tr, w), lambda i: (i, 0)) for w, _ in outs]
    out_specs += [pl.BlockSpec((SUBLANES, w), lambda i: (0, 0)) for w in accs]
    out_shape = [jax.ShapeDtypeStruct((n_rows, w), d) for w, d in outs]
    out_shape += [jax.ShapeDtypeStruct((SUBLANES, w), F32) for w in accs]
    res = pl.pallas_call(
        body, name=name, grid=(n_rows // tr,), in_specs=in_specs, out_specs=out_specs, out_shape=out_shape,
        compiler_params=_params(("arbitrary",) if n_a else ("parallel",)),
    )(*rows, *bcs)
    return tuple(res)


def _rowwise_bwd(name, fn, rows, bcs, cts, need_rows, row_dtypes, need_bcs=None, adds=None, fwd_sums=(), tr=None):
    n_rows = rows[0].shape[0]
    adds = adds or {}
    need_bcs = list(range(len(bcs))) if need_bcs is None else list(need_bcs)
    ct_arrays = [c for c in cts if not isinstance(c, float)]
    add_keys = sorted(adds)
    add_arrays = [adds[k] for k in add_keys]
    strip = STRIP_ROWS if tr is None and n_rows % STRIP_ROWS == 0 else None
    if tr is None:
        per_row = sum(r.shape[1] * r.dtype.itemsize for r in list(rows) + ct_arrays + add_arrays)
        per_row += sum(rows[i].shape[1] * jnp.dtype(d).itemsize for i, d in zip(need_rows, row_dtypes))
        tr = _row_tile(n_rows, per_row)
    n_r, n_b, n_c, n_ad = len(rows), len(bcs), len(ct_arrays), len(add_arrays)
    n_go, n_gb, n_fs = len(need_rows), len(need_bcs), len(fwd_sums)

    def body(*refs):
        pos = [0]

        def take(n):
            pos[0] += n
            return refs[pos[0] - n: pos[0]]

        r_refs, b_refs, c_refs, ad_refs, go_refs = take(n_r), take(n_b), take(n_c), take(n_ad), take(n_go)
        acc_refs = refs[pos[0]:]

        @pl.when(pl.program_id(0) == 0)
        def _():
            for a in acc_refs:
                a[...] = jnp.zeros_like(a)

        def wrapped(*a):
            r = fn(*a)
            return tuple(r) if isinstance(r, (tuple, list)) else (r,)

        def work(rws):
            r_t = [r[rws, :].astype(F32) for r in r_refs]
            b_t = [r[...].astype(F32) for r in b_refs]
            outs, vjp = jax.vjp(wrapped, *r_t, *b_t)
            it = iter(c_refs)
            full = tuple(jnp.full(o.shape, c, F32) if isinstance(c, float) else next(it)[rws, :].astype(F32)
                         for o, c in zip(outs, cts))
            grads = vjp(full)
            for o_ref, i in zip(go_refs, need_rows):
                g = grads[i]
                if i in adds:
                    g = g + ad_refs[add_keys.index(i)][rws, :].astype(F32)
                o_ref[rws, :] = g.astype(o_ref.dtype)
            for a, j in zip(acc_refs[:n_gb], need_bcs):
                _acc_add(a, grads[n_r + j])
            for a, j in zip(acc_refs[n_gb:], fwd_sums):
                a[0:1, :] += jnp.full((1, LANES), jnp.sum(outs[j]), F32)

        _over_strips(tr, strip, work)

    def row_spec(w):
        return pl.BlockSpec((tr, w), lambda i: (i, 0))

    in_specs = [row_spec(r.shape[1]) for r in rows]
    in_specs += [pl.BlockSpec((1, b.shape[1]), lambda i: (0, 0)) for b in bcs]
    in_specs += [row_spec(c.shape[1]) for c in ct_arrays] + [row_spec(a.shape[1]) for a in add_arrays]
    out_specs = [row_spec(rows[i].shape[1]) for i in need_rows]
    out_specs += [pl.BlockSpec((SUBLANES, bcs[j].shape[1]), lambda i: (0, 0)) for j in need_bcs]
    out_specs += [pl.BlockSpec((SUBLANES, LANES), lambda i: (0, 0)) for _ in fwd_sums]
    out_shape = [jax.ShapeDtypeStruct((n_rows, rows[i].shape[1]), d) for i, d in zip(need_rows, row_dtypes)]
    out_shape += [jax.ShapeDtypeStruct((SUBLANES, bcs[j].shape[1]), F32) for j in need_bcs]
    out_shape += [jax.ShapeDtypeStruct((SUBLANES, LANES), F32) for _ in fwd_sums]
    res = pl.pallas_call(
        body, name=name, grid=(n_rows // tr,), in_specs=in_specs, out_specs=out_specs, out_shape=out_shape,
        compiler_params=_params(("arbitrary",)),
    )(*rows, *bcs, *ct_arrays, *add_arrays)
    return tuple(res)


def _fold(acc):
    return jnp.sum(acc, axis=0, keepdims=True)


def _rms(x, w):
    return x * lax.rsqrt(jnp.mean(x * x, axis=-1, keepdims=True) + EPS) * w


def _sigmoid(x):
    return jax.nn.sigmoid(x)


def _silu(x):
    return x * _sigmoid(x)


def _log1p(u):
    series = u * (1.0 - u * (0.5 - u * (1.0 / 3.0 - u * 0.25)))
    return jnp.where(u < 0.01, series, jnp.log(1.0 + u))


def _softplus(x):
    return jnp.maximum(x, 0.0) + _log1p(jnp.exp(-jnp.abs(x)))


def _st_pre(x, w):
    return _rms(x, w)


def _st_lora_norms(dq):
    def fn(cqkv, qn, kvn):
        return _rms(cqkv[:, :dq], qn), _rms(cqkv[:, dq:], kvn)
    return fn


def _st_gated_norm(n_groups):
    def fn(y, xs, z, dskip, wn):
        yz = (y + dskip * xs) * _silu(z)
        gw = yz.shape[1] // n_groups
        parts = [_rms(yz[:, g * gw:(g + 1) * gw], wn[:, g * gw:(g + 1) * gw]) for g in range(n_groups)]
        return jnp.concatenate(parts, axis=1)
    return fn


def _st_mix(d):
    def fn(g, ao, so):
        return _sigmoid(g[:, :d]) * ao + _sigmoid(g[:, d:]) * so
    return fn


def _st_res_norm(h, y, w_post, w_pre):
    h2 = h + _rms(y, w_post)
    return h2, _rms(h2, w_pre)


def _st_swiglu(gt, up):
    return _silu(gt) * up


def _st_loss(pe, gl, h2, tgt, w_post):
    e = pe * _sigmoid(gl)
    diff = h2 + _rms(e, w_post) - tgt
    return 0.5 * jnp.mean(diff * diff, axis=-1, keepdims=True)


def _rope_tables(positions):
    half = QK_ROPE // 2
    inv_freq = ROPE_THETA ** (-jnp.arange(0, QK_ROPE, 2, dtype=F32) / QK_ROPE)
    ang = positions.astype(F32).reshape(-1, 1) * inv_freq
    cos, sin = jnp.cos(ang), jnp.sin(ang)
    n = ang.shape[0]
    z = lambda w: jnp.zeros((n, w), F32)
    c_tab = jnp.concatenate([jnp.ones((n, QK_NOPE), F32), cos, cos, z(QK_PAD - QK_NOPE - QK_ROPE)], axis=1)
    a_tab = jnp.concatenate([z(QK_NOPE), -sin, z(half), z(QK_PAD - QK_NOPE - QK_ROPE)], axis=1)
    b_tab = jnp.concatenate([z(QK_NOPE), z(half), sin, z(QK_PAD - QK_NOPE - QK_ROPE)], axis=1)
    return c_tab, a_tab, b_tab


def _rot(x, c, a, b):
    half = QK_ROPE // 2
    return x * c + pltpu.roll(x, QK_PAD - half, axis=1) * a + pltpu.roll(x, half, axis=1) * b


def _rot_t(g, c, a, b):
    half = QK_ROPE // 2
    return g * c + pltpu.roll(g * a, half, axis=1) + pltpu.roll(g * b, QK_PAD - half, axis=1)


def _st_rope(n_heads):
    def fn(qraw, kraw, sm, c, a, b):
        kpe = _rot(sm[:, :QK_PAD], c, a, b)
        scale = float(QK_NOPE + QK_ROPE) ** -0.5
        q = [_rot(qraw[:, h * QK_PAD:(h + 1) * QK_PAD], c, a, b) * scale for h in range(n_heads)]
        k = [kraw[:, h * QK_PAD:(h + 1) * QK_PAD] + kpe for h in range(n_heads)]
        return jnp.concatenate(q, axis=1), jnp.concatenate(k, axis=1)
    return fn


def _st_rope_bwd(n_heads):
    def fn(dq, dk, c, a, b):
        dqraw = [_rot_t(dq[:, h * QK_PAD:(h + 1) * QK_PAD], c, a, b) for h in range(n_heads)]
        dks = dk[:, :QK_PAD]
        for h in range(1, n_heads):
            dks = dks + dk[:, h * QK_PAD:(h + 1) * QK_PAD]
        return jnp.concatenate(dqraw, axis=1), _rot_t(dks, c, a, b), dk
    return fn


def _split3(x):
    h1 = x.astype(BF16)
    r1 = x - h1.astype(F32)
    h2 = r1.astype(BF16)
    h3 = (r1 - h2.astype(F32)).astype(BF16)
    return h1, h2, h3


def _tri_dot(tri, x):
    h1, h2, h3 = _split3(x)
    return (_dot(tri, h3) + _dot(tri, h2)) + _dot(tri, h1)


def _dot_tri(x, tri):
    h1, h2, h3 = _split3(x)
    return (_dot(h3, tri) + _dot(h2, tri)) + _dot(h1, tri)


def _st_dt(sm, bias, alog):
    x = sm[:, QK_PAD:] + bias
    dt = _softplus(x)
    return dt, dt * (-jnp.exp(alog))


def _st_dt_bwd(sm, ddt, dcum, bias, alog):
    n = sm.shape[0]
    i = lax.broadcasted_iota(jnp.int32, (n, n), 0)
    j = lax.broadcasted_iota(jnp.int32, (n, n), 1)
    upper = (j >= i).astype(BF16)
    dda = _tri_dot(upper, dcum)
    x = sm[:, QK_PAD:] + bias
    dt = _softplus(x)
    a = -jnp.exp(alog)
    draw = (ddt + dda * a) * _sigmoid(x)
    return draw, draw, dda * dt * a


def _conv_fwd(xbc, w, b):
    S, C = xbc.shape
    tr = _pick(S, (512, 256))
    tc = _pick(C, (1024, 512, 256, 128))
    hb = tr // SUBLANES

    def body(x_ref, halo_ref, w_ref, b_ref, c_ref, a_ref, ext):
        i = pl.program_id(1)
        halo = jnp.where(i == 0, 0.0, halo_ref[...])
        ext[0:SUBLANES, :] = halo
        ext[SUBLANES:, :] = x_ref[...]
        wv = w_ref[...]
        acc = b_ref[...] + wv[CONV_WIDTH - 1:CONV_WIDTH, :] * x_ref[...]
        for k in range(CONV_WIDTH - 1):
            off = SUBLANES - (CONV_WIDTH - 1) + k
            acc = acc + wv[k:k + 1, :] * ext[pl.ds(off, tr), :]
        c_ref[...] = acc
        a_ref[...] = _silu(acc)

    return pl.pallas_call(
        body, name="conv_fwd", grid=(C // tc, S // tr),
        in_specs=[pl.BlockSpec((tr, tc), lambda j, i: (i, j)),
                  pl.BlockSpec((SUBLANES, tc), lambda j, i: (jnp.maximum(i * hb - 1, 0), j)),
                  pl.BlockSpec((CONV_WIDTH, tc), lambda j, i: (0, j)),
                  pl.BlockSpec((1, tc), lambda j, i: (0, j))],
        out_specs=[pl.BlockSpec((tr, tc), lambda j, i: (i, j))] * 2,
        out_shape=[jax.ShapeDtypeStruct((S, C), F32)] * 2,
        scratch_shapes=[pltpu.VMEM((tr + SUBLANES, tc), F32)],
        compiler_params=_params(("parallel", "arbitrary")),
    )(xbc, xbc, w, b)


def _conv_bwd(xbc, dconv, w):
    S, C = xbc.shape
    tr = _pick(S, (512, 256))
    tc = _pick(C, (1024, 512, 256, 128))
    hb = tr // SUBLANES
    n_i = S // tr

    def body(x_ref, halo_ref, d_ref, dnext_ref, w_ref, dx_ref, dw_ref, ext, dext):
        i = pl.program_id(1)
        ext[0:SUBLANES, :] = jnp.where(i == 0, 0.0, halo_ref[...])
        ext[SUBLANES:, :] = x_ref[...]
        dext[0:tr, :] = d_ref[...]
        dext[tr:, :] = jnp.where(i == n_i - 1, 0.0, dnext_ref[...])
        wv = w_ref[...]
        d = d_ref[...]

        @pl.when(i == 0)
        def _():
            dw_ref[...] = jnp.zeros_like(dw_ref)

        dx = wv[CONV_WIDTH - 1:CONV_WIDTH, :] * d
        for k in range(CONV_WIDTH):
            if k < CONV_WIDTH - 1:
                dx = dx + wv[k:k + 1, :] * dext[pl.ds(CONV_WIDTH - 1 - k, tr), :]
                xs = ext[pl.ds(SUBLANES - (CONV_WIDTH - 1) + k, tr), :]
            else:
                xs = x_ref[...]
            prod = d * xs
            dw_ref[k * SUBLANES:(k + 1) * SUBLANES, :] += prod.reshape(tr // SUBLANES, SUBLANES, tc).sum(axis=0)
        dx_ref[...] = dx.astype(dx_ref.dtype)

    return pl.pallas_call(
        body, name="conv_bwd", grid=(C // tc, n_i),
        in_specs=[pl.BlockSpec((tr, tc), lambda j, i: (i, j)),
                  pl.BlockSpec((SUBLANES, tc), lambda j, i: (jnp.maximum(i * hb - 1, 0), j)),
                  pl.BlockSpec((tr, tc), lambda j, i: (i, j)),
                  pl.BlockSpec((SUBLANES, tc), lambda j, i: (jnp.minimum((i + 1) * hb, S // SUBLANES - 1), j)),
                  pl.BlockSpec((CONV_WIDTH, tc), lambda j, i: (0, j))],
        out_specs=[pl.BlockSpec((tr, tc), lambda j, i: (i, j)),
                   pl.BlockSpec((CONV_WIDTH * SUBLANES, tc), lambda j, i: (0, j))],
        out_shape=[jax.ShapeDtypeStruct((S, C), BF16), jax.ShapeDtypeStruct((CONV_WIDTH * SUBLANES, C), F32)],
        scratch_shapes=[pltpu.VMEM((tr + SUBLANES, tc), F32), pltpu.VMEM((tr + SUBLANES, tc), F32)],
        compiler_params=_params(("parallel", "arbitrary")),
    )(xbc, xbc, dconv, dconv, w)


def _st_dconv(d_inner):
    def fn(xc, dxa, dxb, db_, dc_):
        s = _sigmoid(xc)
        g = jnp.concatenate([dxa + dxb, db_, dc_], axis=1) * (s * (1.0 + xc * (1.0 - s)))
        return g, g
    return fn


def _chunk_setup(b_ref, c_ref, dac_ref, dar_ref, L):
    ii = lax.broadcasted_iota(jnp.int32, (L, L), 0)
    jj = lax.broadcasted_iota(jnp.int32, (L, L), 1)
    tri = ii >= jj
    cum_c = _tri_dot(tri.astype(BF16), dac_ref[0])
    cum_r = _dot_tri(dar_ref[0], (ii <= jj).astype(BF16))
    bm = b_ref[...].astype(BF16)
    cm = c_ref[...].astype(BF16)
    return tri, cum_c, cum_r, bm, cm, _dot_nt(cm, bm)


def _ssd_specs(d_inner, n_groups, gw, L, rp, chunk_of):
    bb0 = d_inner // D_STATE
    cb0 = bb0 + n_groups
    return [pl.BlockSpec((L, gw), lambda g, c: (chunk_of(c), g)),
            pl.BlockSpec((L, D_STATE), lambda g, c: (chunk_of(c), bb0 + g)),
            pl.BlockSpec((L, D_STATE), lambda g, c: (chunk_of(c), cb0 + g)),
            pl.BlockSpec((1, L, LANES), lambda g, c: (g, chunk_of(c), 0)),
            pl.BlockSpec((1, L, LANES), lambda g, c: (g, chunk_of(c), 0)),
            pl.BlockSpec((1, rp, L), lambda g, c: (g, 0, chunk_of(c)))]


def _ssd_fwd(xbc_a, dt_col, da_col, da_row, d_inner, n_groups, R):
    S = xbc_a.shape[0]
    L = min(CHUNK, S)
    NC = S // L
    P, N = SSM_HEADDIM, D_STATE
    gw = R * P
    rp = da_row.shape[1]

    def body(x_ref, b_ref, c_ref, dt_ref, dac_ref, dar_ref, y_ref, st_ref, state):
        @pl.when(pl.program_id(1) == 0)
        def _():
            state[...] = jnp.zeros_like(state)

        st_ref[0, 0] = state[...]
        tri, cum_c, cum_r, bm, cm, gm = _chunk_setup(b_ref, c_ref, dac_ref, dar_ref, L)
        dt = dt_ref[0]
        for r in range(R):
            cc = cum_c[:, r:r + 1]
            cr = cum_r[r:r + 1, :]
            lam = jnp.exp(jnp.where(tri, cc - cr, -jnp.inf))
            m = (gm * lam).astype(BF16)
            x = x_ref[:, r * P:(r + 1) * P] * dt[:, r:r + 1]
            s_r = state[r * P:(r + 1) * P, :]
            y_off = _dot_nt(cm, s_r.astype(BF16)) * jnp.exp(cc)
            y_ref[:, r * P:(r + 1) * P] = _dot(m, x.astype(BF16)) + y_off
            last = cr[:, L - 1:L]
            xw = (x * jnp.exp(last - cc)).astype(BF16)
            state[r * P:(r + 1) * P, :] = s_r * jnp.exp(last) + _dot_tn(xw, bm)

    return pl.pallas_call(
        body, name="ssd_fwd", grid=(n_groups, NC),
        in_specs=_ssd_specs(d_inner, n_groups, gw, L, rp, lambda c: c),
        out_specs=[pl.BlockSpec((L, gw), lambda g, c: (c, g)),
                   pl.BlockSpec((1, 1, gw, N), lambda g, c: (g, c, 0, 0))],
        out_shape=[jax.ShapeDtypeStruct((S, d_inner), F32), jax.ShapeDtypeStruct((n_groups, NC, gw, N), F32)],
        scratch_shapes=[pltpu.VMEM((gw, N), F32)],
        compiler_params=_params(("parallel", "arbitrary")),
    )(xbc_a, xbc_a, xbc_a, dt_col, da_col, da_row)


def _ssd_bwd(xbc_a, dt_col, da_col, da_row, states, dy, d_inner, n_groups, R):
    S = xbc_a.shape[0]
    L = min(CHUNK, S)
    NC = S // L
    P, N = SSM_HEADDIM, D_STATE
    gw = R * P
    rp = da_row.shape[1]
    rev = lambda c: NC - 1 - c

    def body(x_ref, b_ref, c_ref, dt_ref, dac_ref, dar_ref, st_ref, dy_ref,
             dx_ref, db_ref, dc_ref, ddt_ref, dcc_ref, dcr_ref, dstate):
        @pl.when(pl.program_id(1) == 0)
        def _():
            dstate[...] = jnp.zeros_like(dstate)

        tri, cum_c, cum_r, bm, cm, gm = _chunk_setup(b_ref, c_ref, dac_ref, dar_ref, L)
        dt = dt_ref[0]
        lane = lax.broadcasted_iota(jnp.int32, (L, LANES), 1)
        sub = lax.broadcasted_iota(jnp.int32, (rp, L), 0)
        is_last = lax.broadcasted_iota(jnp.int32, (L, 1), 0) == L - 1
        d_g = jnp.zeros((L, L), F32)
        dc_acc = jnp.zeros((L, N), F32)
        db_acc = jnp.zeros((L, N), F32)
        ddt_out = jnp.zeros((L, LANES), F32)
        dcc_out = jnp.zeros((L, LANES), F32)
        dcr_out = jnp.zeros((rp, L), F32)
        for r in range(R):
            cc = cum_c[:, r:r + 1]
            cr = cum_r[r:r + 1, :]
            lam = jnp.exp(jnp.where(tri, cc - cr, -jnp.inf))
            m = gm * lam
            dtc = dt[:, r:r + 1]
            xh = x_ref[:, r * P:(r + 1) * P]
            x = xh * dtc
            xb = x.astype(BF16)
            d_y = dy_ref[:, r * P:(r + 1) * P]
            d_yb = d_y.astype(BF16)
            s_r = st_ref[0, 0, r * P:(r + 1) * P, :]
            s_rb = s_r.astype(BF16)
            ds_n = dstate[r * P:(r + 1) * P, :]
            ds_nb = ds_n.astype(BF16)
            e = jnp.exp(cc)
            last = cr[:, L - 1:L]
            e_last = jnp.exp(last)
            w = jnp.exp(last - cc)
            d_m = _dot_nt(d_yb, xb)
            d_x = _dot_tn(m.astype(BF16), d_yb)
            d_ye = (d_y * e).astype(BF16)
            dc_acc = dc_acc + _dot(d_ye, s_rb)
            ds_part = _dot_tn(d_ye, cm)
            y_off = _dot_nt(cm, s_rb) * e
            dcum = jnp.sum(d_y * y_off, axis=1, keepdims=True)
            d_xw = _dot_nt(bm, ds_nb)
            d_x = d_x + d_xw * w
            dw_w = jnp.sum(d_xw * x, axis=1, keepdims=True) * w
            db_acc = db_acc + _dot((x * w).astype(BF16), ds_nb)
            d_last = jnp.sum(ds_n * s_r, keepdims=True) * e_last + jnp.sum(dw_w, keepdims=True)
            dcum = dcum - dw_w
            dstate[r * P:(r + 1) * P, :] = e_last * ds_n + ds_part
            d_g = d_g + d_m * lam
            q = d_m * m
            dcum = dcum + jnp.sum(q, axis=1, keepdims=True) + jnp.where(is_last, d_last, 0.0)
            dcum_row = -jnp.sum(q, axis=0, keepdims=True)
            dx_ref[:, r * P:(r + 1) * P] = d_x * dtc
            ddt = jnp.sum(d_x * xh, axis=1, keepdims=True)
            ddt_out = ddt_out + jnp.where(lane == r, ddt, 0.0)
            dcc_out = dcc_out + jnp.where(lane == r, dcum, 0.0)
            dcr_out = dcr_out + jnp.where(sub == r, dcum_row, 0.0)
        d_gb = d_g.astype(BF16)
        dc_ref[...] = dc_acc + _dot(d_gb, bm)
        db_ref[...] = db_acc + _dot_tn(d_gb, cm)
        ddt_ref[0] = ddt_out
        dcc_ref[0] = dcc_out
        dcr_ref[0] = dcr_out

    gn = n_groups * N
    return pl.pallas_call(
        body, name="ssd_bwd", grid=(n_groups, NC),
        in_specs=_ssd_specs(d_inner, n_groups, gw, L, rp, rev) + [
            pl.BlockSpec((1, 1, gw, N), lambda g, c: (g, rev(c), 0, 0)),
            pl.BlockSpec((L, gw), lambda g, c: (rev(c), g))],
        out_specs=[pl.BlockSpec((L, gw), lambda g, c: (rev(c), g)),
                   pl.BlockSpec((L, N), lambda g, c: (rev(c), g)),
                   pl.BlockSpec((L, N), lambda g, c: (rev(c), g)),
                   pl.BlockSpec((1, L, LANES), lambda g, c: (g, rev(c), 0)),
                   pl.BlockSpec((1, L, LANES), lambda g, c: (g, rev(c), 0)),
                   pl.BlockSpec((1, rp, L), lambda g, c: (g, 0, rev(c)))],
        out_shape=[jax.ShapeDtypeStruct((S, d_inner), F32), jax.ShapeDtypeStruct((S, gn), F32),
                   jax.ShapeDtypeStruct((S, gn), F32), jax.ShapeDtypeStruct((n_groups, S, LANES), F32),
                   jax.ShapeDtypeStruct((n_groups, S, LANES), F32), jax.ShapeDtypeStruct((n_groups, rp, S), F32)],
        scratch_shapes=[pltpu.VMEM((gw, N), F32)],
        compiler_params=_params(("parallel", "arbitrary")),
    )(xbc_a, xbc_a, xbc_a, dt_col, da_col, da_row, states, dy)


def _chunk_setup_t(b_ref, c_ref, dac_ref, dar_ref, L):
    ii = lax.broadcasted_iota(jnp.int32, (L, L), 0)
    jj = lax.broadcasted_iota(jnp.int32, (L, L), 1)
    lower = ii >= jj
    upper = ii <= jj
    cum_c = _tri_dot(lower.astype(BF16), dac_ref[0])
    cum_r = _dot_tri(dar_ref[0], upper.astype(BF16))
    bm = b_ref[...].astype(BF16)
    cm = c_ref[...].astype(BF16)
    return lower, upper, cum_c, cum_r, bm, cm


def _ssd_specs_t(d_inner, n_groups, gw, L, rp, chunk_of):
    bb0 = d_inner // D_STATE
    cb0 = bb0 + n_groups
    return [pl.BlockSpec((L, gw), lambda g, c: (chunk_of(c), g)),
            pl.BlockSpec((L, D_STATE), lambda g, c: (chunk_of(c), bb0 + g)),
            pl.BlockSpec((L, D_STATE), lambda g, c: (chunk_of(c), cb0 + g)),
            pl.BlockSpec((1, rp, L), lambda g, c: (g, 0, chunk_of(c))),
            pl.BlockSpec((1, L, LANES), lambda g, c: (g, chunk_of(c), 0)),
            pl.BlockSpec((1, rp, L), lambda g, c: (g, 0, chunk_of(c)))]


def _ssd_fwd_t(xbc_a, dt_row, da_col, da_row, d_inner, n_groups, R):
    S = xbc_a.shape[0]
    L = min(CHUNK, S)
    NC = S // L
    P, N = SSM_HEADDIM, D_STATE
    gw = R * P
    rp = da_row.shape[1]

    def body(x_ref, b_ref, c_ref, dt_ref, dac_ref, dar_ref, y_ref, st_ref, state, y_t):
        @pl.when(pl.program_id(1) == 0)
        def _():
            state[...] = jnp.zeros_like(state)

        st_ref[0, 0] = state[...]
        lower, upper, cum_c, cum_r, bm, cm = _chunk_setup_t(b_ref, c_ref, dac_ref, dar_ref, L)
        gm_t = _dot_nt(bm, cm)
        dt = dt_ref[0]
        x_t = x_ref[...].T
        for r in range(R):
            cc = cum_c[:, r:r + 1]
            cr = cum_r[r:r + 1, :]
            m_t = (gm_t * jnp.exp(jnp.where(upper, cr - cc, -jnp.inf))).astype(BF16)
            x = x_t[r * P:(r + 1) * P, :] * dt[r:r + 1, :]
            s_r = state[r * P:(r + 1) * P, :]
            y_off = _dot_nt(s_r.astype(BF16), cm) * jnp.exp(cr)
            y_t[r * P:(r + 1) * P, :] = _dot(x.astype(BF16), m_t) + y_off
            last = cr[:, L - 1:L]
            xw = (x * jnp.exp(last - cr)).astype(BF16)
            state[r * P:(r + 1) * P, :] = s_r * jnp.exp(last) + _dot(xw, bm)
        y_ref[...] = y_t[...].T

    return pl.pallas_call(
        body, name="ssd_fwd", grid=(n_groups, NC),
        in_specs=_ssd_specs_t(d_inner, n_groups, gw, L, rp, lambda c: c),
        out_specs=[pl.BlockSpec((L, gw), lambda g, c: (c, g)),
                   pl.BlockSpec((1, 1, gw, N), lambda g, c: (g, c, 0, 0))],
        out_shape=[jax.ShapeDtypeStruct((S, d_inner), F32), jax.ShapeDtypeStruct((n_groups, NC, gw, N), F32)],
        scratch_shapes=[pltpu.VMEM((gw, N), F32), pltpu.VMEM((gw, L), F32)],
        compiler_params=_params(("parallel", "arbitrary")),
    )(xbc_a, xbc_a, xbc_a, dt_row, da_col, da_row)


def _ssd_bwd_t(xbc_a, dt_row, da_col, da_row, states, dy, d_inner, n_groups, R):
    S = xbc_a.shape[0]
    L = min(CHUNK, S)
    NC = S // L
    P, N = SSM_HEADDIM, D_STATE
    gw = R * P
    rp = da_row.shape[1]
    rev = lambda c: NC - 1 - c

    def body(x_ref, b_ref, c_ref, dt_ref, dac_ref, dar_ref, st_ref, dy_ref,
             dx_ref, db_ref, dc_ref, ddt_ref, dcum_ref, dstate, dx_t):
        @pl.when(pl.program_id(1) == 0)
        def _():
            dstate[...] = jnp.zeros_like(dstate)

        lower, upper, cum_c, cum_r, bm, cm = _chunk_setup_t(b_ref, c_ref, dac_ref, dar_ref, L)
        gm = _dot_nt(cm, bm)
        gm_t = _dot_nt(bm, cm)
        dt = dt_ref[0]
        x_t = x_ref[...].T
        dy_t = dy_ref[...].T
        sub = lax.broadcasted_iota(jnp.int32, (rp, L), 0)
        is_last = lax.broadcasted_iota(jnp.int32, (1, L), 1) == L - 1
        d_g = jnp.zeros((L, L), F32)
        d_g_t = jnp.zeros((L, L), F32)
        dc_acc = jnp.zeros((L, N), F32)
        db_acc = jnp.zeros((L, N), F32)
        ddt_out = jnp.zeros((rp, L), F32)
        dcum_out = jnp.zeros((rp, L), F32)
        for r in range(R):
            cc = jnp.broadcast_to(cum_c[:, r:r + 1], (L, L))
            cr = cum_r[r:r + 1, :]
            lam = jnp.exp(jnp.where(lower, cc - cum_r[r:r + 1, :], -jnp.inf))
            lam_t = jnp.exp(jnp.where(upper, cr - cc, -jnp.inf))
            m = gm * lam
            m_t = gm_t * lam_t
            dtr = dt[r:r + 1, :]
            xh = x_t[r * P:(r + 1) * P, :]
            x = xh * dtr
            xb = x.astype(BF16)
            d_y = dy_t[r * P:(r + 1) * P, :]
            d_yb = d_y.astype(BF16)
            s_r = st_ref[0, 0, r * P:(r + 1) * P, :]
            s_rb = s_r.astype(BF16)
            ds_n = dstate[r * P:(r + 1) * P, :]
            ds_nb = ds_n.astype(BF16)
            e = jnp.exp(cr)
            last = cr[:, L - 1:L]
            e_last = jnp.exp(last)
            w = jnp.exp(last - cr)
            d_x = _dot(d_yb, m.astype(BF16))
            d_m = _dot_tn(d_yb, xb)
            d_m_t = _dot_tn(xb, d_yb)
            d_ye = (d_y * e).astype(BF16)
            dc_acc = dc_acc + _dot_tn(d_ye, s_rb)
            ds_part = _dot(d_ye, cm)
            y_off = _dot_nt(s_rb, cm) * e
            dcum = jnp.sum(d_y * y_off, axis=0, keepdims=True)
            d_xw = _dot_nt(ds_nb, bm)
            d_x = d_x + d_xw * w
            dw_w = jnp.sum(d_xw * x, axis=0, keepdims=True) * w
            db_acc = db_acc + _dot_tn((x * w).astype(BF16), ds_nb)
            d_last = jnp.sum(ds_n * s_r, keepdims=True) * e_last + jnp.sum(dw_w, keepdims=True)
            dstate[r * P:(r + 1) * P, :] = e_last * ds_n + ds_part
            d_g = d_g + d_m * lam
            d_g_t = d_g_t + d_m_t * lam_t
            dcum = (dcum - dw_w + jnp.sum(d_m_t * m_t, axis=0, keepdims=True)
                    - jnp.sum(d_m * m, axis=0, keepdims=True) + jnp.where(is_last, d_last, 0.0))
            dx_t[r * P:(r + 1) * P, :] = d_x * dtr
            ddt = jnp.sum(d_x * xh, axis=0, keepdims=True)
            ddt_out = ddt_out + jnp.where(sub == r, ddt, 0.0)
            dcum_out = dcum_out + jnp.where(sub == r, dcum, 0.0)
        dc_ref[...] = dc_acc + _dot(d_g.astype(BF16), bm)
        db_ref[...] = db_acc + _dot(d_g_t.astype(BF16), cm)
        dx_ref[...] = dx_t[...].T
        ddt_ref[0] = ddt_out
        dcum_ref[0] = dcum_out

    gn = n_groups * N
    return pl.pallas_call(
        body, name="ssd_bwd", grid=(n_groups, NC),
        in_specs=_ssd_specs_t(d_inner, n_groups, gw, L, rp, rev) + [
            pl.BlockSpec((1, 1, gw, N), lambda g, c: (g, rev(c), 0, 0)),
            pl.BlockSpec((L, gw), lambda g, c: (rev(c), g))],
        out_specs=[pl.BlockSpec((L, gw), lambda g, c: (rev(c), g)),
                   pl.BlockSpec((L, N), lambda g, c: (rev(c), g)),
                   pl.BlockSpec((L, N), lambda g, c: (rev(c), g)),
                   pl.BlockSpec((1, rp, L), lambda g, c: (g, 0, rev(c))),
                   pl.BlockSpec((1, rp, L), lambda g, c: (g, 0, rev(c)))],
        out_shape=[jax.ShapeDtypeStruct((S, d_inner), F32), jax.ShapeDtypeStruct((S, gn), F32),
                   jax.ShapeDtypeStruct((S, gn), F32), jax.ShapeDtypeStruct((n_groups, rp, S), F32),
                   jax.ShapeDtypeStruct((n_groups, rp, S), F32)],
        scratch_shapes=[pltpu.VMEM((gw, N), F32), pltpu.VMEM((gw, L), F32)],
        compiler_params=_params(("parallel", "arbitrary")),
    )(xbc_a, xbc_a, xbc_a, dt_row, da_col, da_row, states, dy)


def _attn_scale():
    return float(QK_NOPE + QK_ROPE) ** -0.5


def _diag_mask(t, keys_first=False):
    rows = lax.broadcasted_iota(jnp.int32, (t, t), 0)
    cols = lax.broadcasted_iota(jnp.int32, (t, t), 1)
    return rows <= cols if keys_first else cols <= rows


def _tile_rows(ref, j, t):
    return ref[pl.ds(pl.multiple_of(j * t, t), t), :]


def _walk_wide(lo, hi, tile_step, joint=True):
    n = hi - lo

    def step(j, width):
        if joint:
            tile_step(j, width)
        else:
            for u in range(width):
                tile_step(j + u, 1)

    def quad(t, carry):
        step(lo + 4 * t, 4)
        return carry

    lax.fori_loop(0, n // 4, quad, 0)

    @pl.when(n % 4 >= 2)
    def _():
        step(hi - n % 4, 2)

    @pl.when(n % 2 == 1)
    def _():
        step(hi - 1, 1)


def _carried(main_body, n_in, n_out, n_scratch, carry, grid):
    if carry is None:
        return main_body
    n_ci, n_co = len(carry["ins"]), len(carry["outs"])

    def body(*refs):
        pos = [0]

        def take(n):
            pos[0] += n
            return refs[pos[0] - n: pos[0]]

        ins, c_ins, outs, c_outs, scratch, sems = take(n_in), take(n_ci), take(n_out), take(n_co), take(n_scratch), take(3)
        steps = [pl.program_id(a) for a in range(len(grid))]

        @pl.when(functools.reduce(jnp.logical_and, [s == 0 for s in steps]))
        def _():
            for cp in carry["copies"](c_ins, c_outs, *sems):
                cp.start()

        main_body(*ins, *outs, *scratch)

        @pl.when(functools.reduce(jnp.logical_and, [s == n - 1 for s, n in zip(steps, grid)]))
        def _():
            for cp in carry["copies"](c_ins, c_outs, *sems):
                cp.wait()

    return body


def _carry_call(carry):
    if carry is None:
        return [], [], [], [], []
    sems = [pltpu.SemaphoreType.DMA((carry["n_remote"],)), pltpu.SemaphoreType.DMA((carry["n_remote"],)),
            pltpu.SemaphoreType.DMA((max(carry["n_local"], 1),))]
    return (list(carry["ins"]), [_HBM] * len(carry["ins"]), [_HBM] * len(carry["outs"]), list(carry["outs"]), sems)


def _flash_fwd(q, k, v_t, n_heads, carry=None):
    S = q.shape[0]
    T = min(FLASH_T, S)
    grid = (n_heads, S // T)
    c_args, c_in_specs, c_out_specs, c_out_shapes, c_sems = _carry_call(carry)

    def body(q_ref, k_ref, vt_ref, o_ref, lse_ref, m_s, l_s, acc_t):
        i = pl.program_id(1)
        m_s[...] = jnp.full_like(m_s, -jnp.inf)
        l_s[...] = jnp.zeros_like(l_s)
        acc_t[...] = jnp.zeros_like(acc_t)
        qv = q_ref[...]

        def step(j, width, masked):
            keys = pl.ds(pl.multiple_of(j * T, T), width * T)
            s_t = _dot_nt(k_ref[keys, :], qv)
            if masked:
                s_t = jnp.where(_diag_mask(T, keys_first=True), s_t, -jnp.inf)
            m_prev = m_s[...]
            m_new = jnp.maximum(m_prev, jnp.max(s_t, axis=0, keepdims=True))
            alpha = jnp.exp(m_prev - m_new)
            p_t = jnp.exp(s_t - m_new)
            l_s[...] = alpha * l_s[...] + jnp.sum(p_t, axis=0, keepdims=True)
            acc_t[...] = alpha * acc_t[...] + _dot(vt_ref[:, keys], p_t.astype(BF16))
            m_s[...] = m_new

        _walk_wide(0, i, lambda j, width: step(j, width, False))
        step(i, 1, True)
        o_ref[...] = (acc_t[...] / l_s[...]).T
        lse_ref[0] = m_s[...] + jnp.log(l_s[...])

    res = pl.pallas_call(
        _carried(body, 3, 2, 3, carry, grid), name="flash_fwd", grid=grid,
        in_specs=[pl.BlockSpec((T, QK_PAD), lambda h, i: (i, h)),
                  pl.BlockSpec((S, QK_PAD), lambda h, i: (0, h)),
                  pl.BlockSpec((V_DIM, S), lambda h, i: (h, 0))] + c_in_specs,
        out_specs=[pl.BlockSpec((T, V_DIM), lambda h, i: (i, h)),
                   pl.BlockSpec((1, 1, T), lambda h, i: (h, 0, i))] + c_out_specs,
        out_shape=[jax.ShapeDtypeStruct((S, n_heads * V_DIM), F32),
                   jax.ShapeDtypeStruct((n_heads, 1, S), F32)] + c_out_shapes,
        scratch_shapes=[pltpu.VMEM((1, T), F32), pltpu.VMEM((1, T), F32), pltpu.VMEM((V_DIM, T), F32)] + c_sems,
        compiler_params=_params(("arbitrary", "arbitrary")),
    )(q, k, v_t, *c_args)
    return res[0], res[1], list(res[2:])


def _flash_bwd_dq(q, k, v, do, lse, delta, n_heads):
    S = q.shape[0]
    T = min(FLASH_T, S)

    def body(q_ref, k_ref, v_ref, do_ref, lse_ref, dl_ref, dq_ref, acc):
        i = pl.program_id(1)
        acc[...] = jnp.zeros_like(acc)
        qv = q_ref[...]
        dov = do_ref[...]
        lse_c = lse_ref[0]
        dl_c = dl_ref[0]

        def step(j, width, masked):
            keys = pl.ds(pl.multiple_of(j * T, T), width * T)
            kt = k_ref[keys, :]
            s = _dot_nt(qv, kt)
            if masked:
                s = jnp.where(_diag_mask(T), s, -jnp.inf)
            p = jnp.exp(s - lse_c)
            ds = p * (_dot_nt(dov, v_ref[keys, :]) - dl_c)
            acc[...] += _dot(ds.astype(BF16), kt)

        _walk_wide(0, i, lambda j, width: step(j, width, False), joint=False)
        step(i, 1, True)
        dq_ref[...] = acc[...] * _attn_scale()

    return pl.pallas_call(
        body, name="flash_bwd_dq", grid=(n_heads, S // T),
        in_specs=[pl.BlockSpec((T, QK_PAD), lambda h, i: (i, h)),
                  pl.BlockSpec((S, QK_PAD), lambda h, i: (0, h)),
                  pl.BlockSpec((S, V_DIM), lambda h, i: (0, h)),
                  pl.BlockSpec((T, V_DIM), lambda h, i: (i, h)),
                  pl.BlockSpec((1, T, 1), lambda h, i: (h, i, 0)),
                  pl.BlockSpec((1, T, 1), lambda h, i: (h, i, 0))],
        out_specs=pl.BlockSpec((T, QK_PAD), lambda h, i: (i, h)),
        out_shape=jax.ShapeDtypeStruct((S, n_heads * QK_PAD), F32),
        scratch_shapes=[pltpu.VMEM((T, QK_PAD), F32)],
        compiler_params=_params(("parallel", "arbitrary")),
    )(q, k, v, do, lse, delta)


def _flash_bwd_dkv(q, k, v, do, lse_row, delta_row, n_heads, carry=None):
    S = q.shape[0]
    T = min(FLASH_T, S)
    nq = S // T
    grid = (n_heads, S // T)
    c_args, c_in_specs, c_out_specs, c_out_shapes, c_sems = _carry_call(carry)

    def body(q_ref, k_ref, v_ref, do_ref, lse_ref, dl_ref, dk_ref, dv_ref, dk_acc, dv_acc):
        j = pl.program_id(1)
        dk_acc[...] = jnp.zeros_like(dk_acc)
        dv_acc[...] = jnp.zeros_like(dv_acc)
        kv = k_ref[...]
        vv = v_ref[...]

        def step(i, width, masked):
            cols = pl.ds(pl.multiple_of(i * T, T), width * T)
            qt = q_ref[cols, :]
            dot = do_ref[cols, :]
            s_t = _dot_nt(kv, qt)
            if masked:
                s_t = jnp.where(_diag_mask(T, keys_first=True), s_t, -jnp.inf)
            p_t = jnp.exp(s_t - lse_ref[0, :, cols])
            dv_acc[...] += _dot(p_t.astype(BF16), dot)
            ds_t = p_t * (_dot_nt(vv, dot) - dl_ref[0, :, cols])
            dk_acc[...] += _dot(ds_t.astype(BF16), qt)

        step(j, 1, True)
        _walk_wide(j + 1, nq, lambda i, width: step(i, width, False), joint=False)
        dk_ref[...] = dk_acc[...]
        dv_ref[...] = dv_acc[...].astype(dv_ref.dtype)

    res = pl.pallas_call(
        _carried(body, 6, 2, 2, carry, grid), name="flash_bwd_dkv", grid=grid,
        in_specs=[pl.BlockSpec((S, QK_PAD), lambda h, j: (0, h)),
                  pl.BlockSpec((T, QK_PAD), lambda h, j: (j, h)),
                  pl.BlockSpec((T, V_DIM), lambda h, j: (j, h)),
                  pl.BlockSpec((S, V_DIM), lambda h, j: (0, h)),
                  pl.BlockSpec((1, 1, S), lambda h, j: (h, 0, 0)),
                  pl.BlockSpec((1, 1, S), lambda h, j: (h, 0, 0))] + c_in_specs,
        out_specs=[pl.BlockSpec((T, QK_PAD), lambda h, j: (j, h)),
                   pl.BlockSpec((T, V_DIM), lambda h, j: (j, h))] + c_out_specs,
        out_shape=[jax.ShapeDtypeStruct((S, n_heads * QK_PAD), F32),
                   jax.ShapeDtypeStruct((S, n_heads * V_DIM), BF16)] + c_out_shapes,
        scratch_shapes=[pltpu.VMEM((T, QK_PAD), F32), pltpu.VMEM((T, V_DIM), F32)] + c_sems,
        compiler_params=_params(("arbitrary", "arbitrary")),
    )(q, k, v, do, lse_row, delta_row, *c_args)
    return res[0], res[1], list(res[2:])


def _st_delta(n_heads):
    def fn(do, o):
        prod = do * o
        lane = lax.broadcasted_iota(jnp.int32, (do.shape[0], LANES), 1)
        out = jnp.zeros((do.shape[0], LANES), F32)
        for h in range(n_heads):
            out = out + jnp.where(lane == h, jnp.sum(prod[:, h * V_DIM:(h + 1) * V_DIM], axis=1, keepdims=True), 0.0)
        return out
    return fn


def _pad_cols(w, width):
    return jnp.pad(w, ((0, 0), (0, width - w.shape[1])))


def _dims(x, p, q_norm, kv_norm, w_uq, dt_bias, ssm_norm, conv_b):
    d = dict(S=x.shape[0], D=x.shape[1], PLE=p.shape[1], DQ=q_norm.shape[1], DKV=kv_norm.shape[1],
             NH=w_uq.shape[1] // (QK_NOPE + QK_ROPE), NHS=dt_bias.shape[1], DI=ssm_norm.shape[1],
             CONV=conv_b.shape[1])
    d["G"] = (d["CONV"] - d["DI"]) // (2 * D_STATE)
    d["R"] = d["NHS"] // d["G"]
    return d


def _assemble(name, blocks):
    rows, cols = blocks.shape[1:]
    if name in COL_SHARDED:
        return blocks.transpose(1, 0, 2).reshape(rows, N_DEV * cols)
    return blocks.reshape(N_DEV * rows, cols)


def _by_device(name, g):
    if name in COL_SHARDED:
        return g.reshape(g.shape[0], N_DEV, g.shape[1] // N_DEV).transpose(1, 0, 2)
    return g.reshape(N_DEV, g.shape[0] // N_DEV, g.shape[1])


def _local_step(x, p, positions, target, small, conv_w, wts, late_names=(), late_shards=(), exchange=True):
    wts = dict(wts)
    dm = _dims(x, p, small["q_norm"], small["kv_norm"], wts["w_uq"], small["dt_bias"], small["ssm_norm"],
               small["conv_b"])
    S, D, DQ, DKV, NH, NHS, DI, CONV, G, R = (dm[k] for k in ("S", "D", "DQ", "DKV", "NH", "NHS", "DI", "CONV", "G", "R"))
    rp = -(-R // SUBLANES) * SUBLANES
    L = min(CHUNK, S)

    w_in = wts["w_in"]
    o = [0]
    for n in (DQ, DKV, QK_ROPE, DI, CONV, NHS, D, D):
        o.append(o[-1] + n)
    w_cqkv, w_kr, w_z, w_xbc, w_dt, w_g = (w_in[:, o[0]:o[2]], w_in[:, o[2]:o[3]], w_in[:, o[3]:o[4]],
                                           w_in[:, o[4]:o[5]], w_in[:, o[5]:o[6]], w_in[:, o[6]:o[8]])
    zc = lambda n: jnp.zeros((D, n), BF16)
    w_sm = jnp.concatenate([zc(QK_NOPE), w_kr, zc(QK_PAD - QK_NOPE - QK_ROPE), w_dt, zc(LANES - NHS)], axis=1)
    w_q = jnp.pad(wts["w_uq"].reshape(DQ, NH, QK_NOPE + QK_ROPE),
                  ((0, 0), (0, 0), (0, QK_PAD - QK_NOPE - QK_ROPE))).reshape(DQ, NH * QK_PAD)
    ukv = wts["w_ukv"].reshape(DKV, NH, QK_NOPE + V_DIM)
    w_k = jnp.pad(ukv[:, :, :QK_NOPE], ((0, 0), (0, 0), (0, QK_PAD - QK_NOPE))).reshape(DKV, NH * QK_PAD)
    w_v = ukv[:, :, QK_NOPE:].reshape(DKV, NH * V_DIM)
    dt_bias_p, a_log_p = _pad_cols(small["dt_bias"], LANES), _pad_cols(small["a_log"], LANES)
    dskip_rep = jnp.repeat(small["d_skip"], SSM_HEADDIM, axis=1)
    c_tab, a_tab, b_tab = _rope_tables(positions)

    (u,) = _rowwise("pre_norm", _st_pre, [x], [small["mix_norm_pre"]], [(D, BF16)])
    cqkv = _mm("in_cqkv", u, w_cqkv, "nn")
    z = _mm("in_z", u, w_z, "nn")
    xbc = _mm("in_xbc", u, w_xbc, "nn")
    g = _mm("in_gates", u, w_g, "nn")
    sm = _mm("in_small", u, w_sm, "nn")

    lora_fn = _st_lora_norms(DQ)
    cq_n, ckv_n = _rowwise("lora_norms", lora_fn, [cqkv], [small["q_norm"], small["kv_norm"]], [(DQ, BF16), (DKV, BF16)])
    qraw = _mm("up_q", cq_n, w_q, "nn")
    kraw = _mm("up_k", ckv_n, w_k, "nn")
    v = _mm("up_v", ckv_n, w_v, "nn", out_dtype=BF16)
    v_t = _mm("up_v_t", w_v.T, ckv_n, "nt", out_dtype=BF16)
    q, k = _rowwise("rope", _st_rope(NH), [qraw, kraw, sm, c_tab, a_tab, b_tab], [],
                    [(NH * QK_PAD, BF16), (NH * QK_PAD, BF16)])
    attn, lse, late_blocks = _flash_fwd(q, k, v_t, NH, carry=_gather_carry(list(late_shards)) if late_names else None)
    wts.update({n: _assemble(n, b) for n, b in zip(late_names, late_blocks)})

    xbc_c, xbc_a = _conv_fwd(xbc, conv_w, small["conv_b"])
    dt, da = _rowwise("dt", _st_dt, [sm], [dt_bias_p, a_log_p], [(LANES, F32), (LANES, F32)])

    def col_layout(t):
        return _pad_cols(t[:, :NHS].reshape(S, G, R).transpose(1, 0, 2).reshape(G * S, R), LANES).reshape(G, S, LANES)

    def row_layout(t):
        return jnp.pad(t[:, :NHS].reshape(S, G, R).transpose(1, 2, 0), ((0, 0), (0, rp - R), (0, 0)))

    dt_row, da_col, da_row = row_layout(dt), col_layout(da), row_layout(da)
    xs = xbc_a[:, :DI]
    y, states = _ssd_fwd_t(xbc_a, dt_row, da_col, da_row, DI, G, R)
    gn_fn = _st_gated_norm(G)
    (ssm,) = _rowwise("gated_norm", gn_fn, [y, xs, z], [dskip_rep, small["ssm_norm"]], [(DI, BF16)])

    ao = _mm("attn_o", attn, wts["w_attn_o"], "nn")
    so = _mm("ssm_o", ssm, wts["w_ssm_o"], "nn")
    mix_fn = _st_mix(D)
    (mixed,) = _rowwise("mix", mix_fn, [g, ao, so], [], [(D, BF16)])
    mo = _mm("out_proj", mixed, wts["w_out"], "nn")
    h1, f = _rowwise("res1", _st_res_norm, [x, mo], [small["mix_norm_post"], small["ffn_norm_pre"]], [(D, F32), (D, BF16)])
    gt = _mm("ffn_gate", f, wts["w_gate"], "nn")
    up = _mm("ffn_up", f, wts["w_up"], "nn")
    (act,) = _rowwise("swiglu", _st_swiglu, [gt, up], [], [(gt.shape[1], BF16)])
    dn = _mm("ffn_down", act, wts["w_down"], "nn")
    h2, a3 = _rowwise("res2", _st_res_norm, [h1, dn], [small["ffn_norm_post"], small["ple_norm_pre"]], [(D, F32), (D, BF16)])
    gl = _mm("ple_gate", a3, wts["w_ple_gate"], "nn")
    pe = _mm("ple_proj", p, wts["w_ple"], "nn")

    sg = {}
    bg = {}
    dpe, dgl, dh2, d_w, loss_acc = _rowwise_bwd(
        "loss", _st_loss, [pe, gl, h2, target], [small["ple_norm_post"]], [1.0], [0, 1, 2], [BF16, BF16, F32], fwd_sums=(0,))
    sg["ple_norm_post"] = _fold(d_w)
    loss = loss_acc[0:1, :]
    bg["w_ple"] = _mm("d_w_ple", p, dpe, "tn", out_dtype=BF16)
    bg["w_ple_gate"] = _mm("d_w_ple_gate", a3, dgl, "tn", out_dtype=BF16)
    da3 = _mm("d_a3", dgl, wts["w_ple_gate"], "nt")

    dh1, ddn, d_post, d_pre = _rowwise_bwd(
        "res2_bwd", _st_res_norm, [h1, dn], [small["ffn_norm_post"], small["ple_norm_pre"]], [dh2, da3], [0, 1], [F32, BF16])
    sg["ffn_norm_post"], sg["ple_norm_pre"] = _fold(d_post), _fold(d_pre)
    bg["w_down"] = _mm("d_w_down", act, ddn, "tn", out_dtype=BF16)
    dact = _mm("d_act", ddn, wts["w_down"], "nt")
    dgt, dup = _rowwise_bwd("swiglu_bwd", _st_swiglu, [gt, up], [], [dact], [0, 1], [BF16, BF16])
    bg["w_gate"] = _mm("d_w_gate", f, dgt, "tn", out_dtype=BF16)
    bg["w_up"] = _mm("d_w_up", f, dup, "tn", out_dtype=BF16)
    df = _mm("d_f_gate", dgt, wts["w_gate"], "nt")
    df = _mm("d_f_up", dup, wts["w_up"], "nt", acc_in=df)

    dx_res, dmo, d_post, d_pre = _rowwise_bwd(
        "res1_bwd", _st_res_norm, [x, mo], [small["mix_norm_post"], small["ffn_norm_pre"]], [dh1, df], [0, 1], [F32, BF16])
    sg["mix_norm_post"], sg["ffn_norm_pre"] = _fold(d_post), _fold(d_pre)
    bg["w_out"] = _mm("d_w_out", mixed, dmo, "tn", out_dtype=BF16)
    dmixed = _mm("d_mixed", dmo, wts["w_out"], "nt")
    dg, dao, dso = _rowwise_bwd("mix_bwd", mix_fn, [g, ao, so], [], [dmixed], [0, 1, 2], [BF16, BF16, BF16])
    bg["w_attn_o"] = _mm("d_w_attn_o", attn, dao, "tn", out_dtype=BF16)
    bg["w_ssm_o"] = _mm("d_w_ssm_o", ssm, dso, "tn", out_dtype=BF16)
    dattn = _mm("d_attn", dao, wts["w_attn_o"], "nt", out_dtype=BF16)
    dssm = _mm("d_ssm", dso, wts["w_ssm_o"], "nt")

    dy, dxs_a, dz, d_dskip, d_ssmn = _rowwise_bwd(
        "gated_norm_bwd", gn_fn, [y, xs, z], [dskip_rep, small["ssm_norm"]], [dssm], [0, 1, 2], [F32, F32, BF16])
    sg["d_skip"] = _fold(d_dskip).reshape(NHS, SSM_HEADDIM).sum(axis=1).reshape(1, NHS)
    sg["ssm_norm"] = _fold(d_ssmn)
    dxs_b, d_b, d_c, ddt_row, dcum_row = _ssd_bwd_t(xbc_a, dt_row, da_col, da_row, states, dy, DI, G, R)

    def from_row(t):
        return _pad_cols(t[:, :R, :].transpose(2, 0, 1).reshape(S, NHS), LANES)

    ddtraw, d_bias, d_alog = _rowwise("dt_bwd", _st_dt_bwd, [sm, from_row(ddt_row), from_row(dcum_row)],
                                      [dt_bias_p, a_log_p], [(LANES, F32)], accs=(LANES, LANES), tr=L)
    sg["dt_bias"], sg["a_log"] = _fold(d_bias)[:, :NHS], _fold(d_alog)[:, :NHS]
    dconv, d_cb = _rowwise("dconv", _st_dconv(DI), [xbc_c, dxs_a, dxs_b, d_b, d_c], [], [(CONV, F32)], accs=(CONV,))
    sg["conv_b"] = _fold(d_cb)
    dxbc, d_cw = _conv_bwd(xbc, dconv, conv_w)
    d_conv_w = d_cw.reshape(CONV_WIDTH, SUBLANES, CONV).sum(axis=1)

    (delta,) = _rowwise("attn_delta", _st_delta(NH), [dattn, attn], [], [(LANES, F32)])
    delta = delta[:, :NH].T
    dq = _flash_bwd_dq(q, k, v, dattn, lse.reshape(NH, S, 1), delta.reshape(NH, S, 1), NH)
    late_sent = [_by_device(n, bg.pop(n)) for n in late_names]
    dk, dv, late_recv = _flash_bwd_dkv(q, k, v, dattn, lse, delta.reshape(NH, 1, S), NH,
                                       carry=_scatter_carry(late_sent) if late_names else None)
    dqraw, dkr, dk = _rowwise("rope_bwd", _st_rope_bwd(NH), [dq, dk, c_tab, a_tab, b_tab], [],
                              [(NH * QK_PAD, BF16), (QK_PAD, F32), (NH * QK_PAD, BF16)])
    d_w_q = _mm("d_w_q", cq_n, dqraw, "tn", out_dtype=BF16)
    d_w_k = _mm("d_w_k", ckv_n, dk, "tn", out_dtype=BF16)
    d_w_v = _mm("d_w_v", ckv_n, dv, "tn", out_dtype=BF16)
    dcq_n = _mm("d_cq_n", dqraw, w_q, "nt")
    dckv_n = _mm("d_ckv_n_k", dk, w_k, "nt")
    dckv_n = _mm("d_ckv_n_v", dv, w_v, "nt", acc_in=dckv_n)
    bg["w_uq"] = d_w_q.reshape(DQ, NH, QK_PAD)[:, :, :QK_NOPE + QK_ROPE].reshape(DQ, NH * (QK_NOPE + QK_ROPE))
    bg["w_ukv"] = jnp.concatenate([d_w_k.reshape(DKV, NH, QK_PAD)[:, :, :QK_NOPE], d_w_v.reshape(DKV, NH, V_DIM)],
                                  axis=2).reshape(DKV, NH * (QK_NOPE + V_DIM))
    dcqkv, d_qn, d_kvn = _rowwise_bwd("lora_norms_bwd", lora_fn, [cqkv], [small["q_norm"], small["kv_norm"]],
                                      [dcq_n, dckv_n], [0], [BF16])
    sg["q_norm"], sg["kv_norm"] = _fold(d_qn), _fold(d_kvn)

    dproj = jnp.concatenate([dcqkv, dz, dxbc, dg, dkr[:, QK_NOPE:QK_NOPE + QK_ROPE].astype(BF16),
                             ddtraw[:, :NHS].astype(BF16)], axis=1)
    w_in = jnp.concatenate([w_cqkv, w_z, w_xbc, w_g, w_kr, w_dt], axis=1)
    d_w = _mm("d_w_in", u, dproj, "tn", out_dtype=BF16)
    e = [0]
    for n in (DQ + DKV, DI, CONV, 2 * D, QK_ROPE, NHS):
        e.append(e[-1] + n)
    bg["w_in"] = jnp.concatenate([d_w[:, e[0]:e[1]], d_w[:, e[4]:e[5]], d_w[:, e[1]:e[2]], d_w[:, e[2]:e[3]],
                                  d_w[:, e[5]:e[6]], d_w[:, e[3]:e[4]]], axis=1)
    rest_names = tuple(n for n in BIG if n in bg)
    if not exchange:
        du = _mm("d_u", dproj, w_in, "nt")
        grad_x, d_pre = _rowwise_bwd("pre_norm_bwd", _st_pre, [x], [small["mix_norm_pre"]], [du], [0], [F32],
                                     adds={0: dx_res})
        sg["mix_norm_pre"] = _fold(d_pre)
        return loss, grad_x, sg, d_conv_w, bg, {}
    rest_sent = [_by_device(n, bg.pop(n)) for n in rest_names]
    du, rest_recv = _mm("d_u", dproj, w_in, "nt", carry=_scatter_carry(rest_sent))
    grad_x, d_pre = _rowwise_bwd("pre_norm_bwd", _st_pre, [x], [small["mix_norm_pre"]], [du], [0], [F32], adds={0: dx_res})
    sg["mix_norm_pre"] = _fold(d_pre)
    sent = dict(zip(late_names, late_sent), **dict(zip(rest_names, rest_sent)))
    recv = dict(zip(late_names, late_recv), **dict(zip(rest_names, rest_recv)))
    return loss, grad_x, sg, d_conv_w, sent, recv


_HBM = pl.BlockSpec(memory_space=pltpu.HBM)
_FLIPS = ((0, 0, 1), (1, 0, 0), (0, 1, 0), (1, 1, 0), (1, 0, 1), (0, 1, 1), (1, 1, 1))


def _place():
    return lax.axis_index("x"), lax.axis_index("y"), lax.axis_index("c")


def _flipped(place, flip):
    return tuple(1 - v if f else v for v, f in zip(place, flip))


def _gather_carry(blocks):
    nw = len(blocks)

    def copies(ins, outs, send_sems, recv_sems, local_sems):
        x, y, c = _place()
        me = 4 * x + 2 * y + c
        cps = []
        for w in range(nw):
            cps.append(pltpu.make_async_copy(ins[w], outs[w].at[me], local_sems.at[w]))
            for k, flip in enumerate(_FLIPS):
                cps.append(pltpu.make_async_remote_copy(
                    src_ref=ins[w], dst_ref=outs[w].at[me], send_sem=send_sems.at[7 * w + k],
                    recv_sem=recv_sems.at[7 * w + k], device_id=_flipped((x, y, c), flip), device_id_type=MESH))
        return cps

    return dict(ins=blocks, outs=[jax.ShapeDtypeStruct((N_DEV,) + b.shape, b.dtype) for b in blocks],
                n_remote=7 * nw, n_local=nw, copies=copies)


def _scatter_carry(by_dev):
    nw = len(by_dev)

    def copies(ins, outs, send_sems, recv_sems, local_sems):
        x, y, c = _place()
        cps = []
        for w in range(nw):
            for k, flip in enumerate(_FLIPS):
                px, py, pc = _flipped((x, y, c), flip)
                cps.append(pltpu.make_async_remote_copy(
                    src_ref=ins[w].at[4 * px + 2 * py + pc], dst_ref=outs[w].at[k], send_sem=send_sems.at[7 * w + k],
                    recv_sem=recv_sems.at[7 * w + k], device_id=(px, py, pc), device_id_type=MESH))
        return cps

    return dict(ins=by_dev, outs=[jax.ShapeDtypeStruct((7,) + b.shape[1:], b.dtype) for b in by_dev],
                n_remote=7 * nw, n_local=0, copies=copies)


def _all_gather(name, blocks):
    nw = len(blocks)

    def body(*refs):
        x_refs, out_refs = refs[:nw], refs[nw:2 * nw]
        send_sems, recv_sems, local_sems = refs[2 * nw:]
        x, y, c = _place()
        me, sibling = (x, y, c), (x, y, 1 - c)
        chips = [(1 - x, y), (x, 1 - y), (1 - x, 1 - y)]

        def slot(w, px, py, pc):
            return out_refs[w].at[4 * px + 2 * py + pc]

        def copy(w, k, blk, to, src=None):
            return pltpu.make_async_remote_copy(
                src_ref=slot(w, *blk) if src is None else src, dst_ref=slot(w, *blk),
                send_sem=send_sems.at[7 * w + k], recv_sem=recv_sems.at[7 * w + k], device_id=to, device_id_type=MESH)

        mine = [pltpu.make_async_copy(x_refs[w], slot(w, *me), local_sems.at[w]) for w in range(nw)]
        for cp in mine:
            cp.start()
        first = []
        for w in range(nw):
            first.append(copy(w, 0, me, sibling, src=x_refs[w]))
            first += [copy(w, 1 + j, me, (*chip, c), src=x_refs[w]) for j, chip in enumerate(chips)]
        for cp in first:
            cp.start()
        passed = []
        for j, chip in enumerate(chips):
            for w in range(nw):
                copy(w, 1 + j, (*chip, c), me).wait_recv()
                passed.append(copy(w, 4 + j, (*chip, c), sibling))
                passed[-1].start()
        for w in range(nw):
            copy(w, 0, sibling, me).wait_recv()
        for j, chip in enumerate(chips):
            for w in range(nw):
                copy(w, 4 + j, (*chip, 1 - c), me).wait_recv()
        for cp in first + passed:
            cp.wait_send()
        for cp in mine:
            cp.wait()

    return pl.pallas_call(
        body, name=name, out_shape=[jax.ShapeDtypeStruct((N_DEV,) + b.shape, b.dtype) for b in blocks],
        in_specs=[_HBM] * nw, out_specs=[_HBM] * nw,
        scratch_shapes=[pltpu.SemaphoreType.DMA((7 * nw,)), pltpu.SemaphoreType.DMA((7 * nw,)),
                        pltpu.SemaphoreType.DMA((nw,))],
    )(*blocks)


def _swap_with_sibling(name, by_dev):
    nw = len(by_dev)

    def body(*refs):
        g_refs, got_refs = refs[:nw], refs[nw:2 * nw]
        send_sems, recv_sems = refs[2 * nw:]
        x, y, c = _place()
        cps = [pltpu.make_async_remote_copy(src_ref=g_refs[w].at[2 * q + (1 - c)], dst_ref=got_refs[w].at[q],
                                            send_sem=send_sems.at[4 * w + q], recv_sem=recv_sems.at[4 * w + q],
                                            device_id=(x, y, 1 - c), device_id_type=MESH)
               for w in range(nw) for q in range(4)]
        for cp in cps:
            cp.start()
        for cp in cps:
            cp.wait()

    return pl.pallas_call(
        body, name=name, out_shape=[jax.ShapeDtypeStruct((4,) + b.shape[1:], b.dtype) for b in by_dev],
        in_specs=[_HBM] * nw, out_specs=[_HBM] * nw,
        scratch_shapes=[pltpu.SemaphoreType.DMA((4 * nw,)), pltpu.SemaphoreType.DMA((4 * nw,))],
    )(*by_dev)


def _swap_with_chips(name, sends):
    nw = len(sends)

    def body(*refs):
        s_refs, r_refs = refs[:nw], refs[nw:2 * nw]
        send_sems, recv_sems = refs[2 * nw:]
        x, y, c = _place()
        chips = [(1 - x, y), (x, 1 - y), (1 - x, 1 - y)]
        cps = [pltpu.make_async_remote_copy(src_ref=s_refs[w].at[2 * px + py], dst_ref=r_refs[w].at[k],
                                            send_sem=send_sems.at[3 * w + k], recv_sem=recv_sems.at[3 * w + k],
                                            device_id=(px, py, c), device_id_type=MESH)
               for w in range(nw) for k, (px, py) in enumerate(chips)]
        for cp in cps:
            cp.start()
        for cp in cps:
            cp.wait()

    return pl.pallas_call(
        body, name=name, out_shape=[jax.ShapeDtypeStruct((3,) + s.shape[1:], s.dtype) for s in sends],
        in_specs=[_HBM] * nw, out_specs=[_HBM] * nw,
        scratch_shapes=[pltpu.SemaphoreType.DMA((3 * nw,)), pltpu.SemaphoreType.DMA((3 * nw,))],
    )(*sends)


def _lane_pad(n):
    return -(-n // LANES) * LANES


def _pack_small(vecs, mat):
    width = max(sum(_lane_pad(v.shape[1]) for v in vecs), _lane_pad(mat.shape[1]))
    row0 = jnp.concatenate([_pad_cols(v, _lane_pad(v.shape[1])) for v in vecs], axis=1)
    rows = jnp.concatenate([_pad_cols(row0, width), _pad_cols(mat, width)], axis=0)
    return jnp.pad(rows, ((0, SUBLANES - rows.shape[0]), (0, 0)))


def _unpack_small(packed, sizes, mat_cols):
    vecs, off = [], 0
    for n in sizes:
        vecs.append(packed[0:1, off:off + n])
        off += _lane_pad(n)
    return vecs, packed[1:1 + CONV_WIDTH, :mat_cols]


def _adamw(w, g, m, v):
    m = ADAM_B1 * m + (1.0 - ADAM_B1) * g
    v = ADAM_B2 * v + (1.0 - ADAM_B2) * (g * g)
    m_hat = m / (1.0 - ADAM_B1 ** ADAM_STEP)
    v_hat = v / (1.0 - ADAM_B2 ** ADAM_STEP)
    delta = -ADAM_LR * (m_hat / (jnp.sqrt(v_hat) + ADAM_EPS) + ADAM_WD * w)
    return delta, m, v


BIG = ("w_in", "w_uq", "w_ukv", "w_attn_o", "w_ssm_o", "w_out", "w_gate", "w_up", "w_down", "w_ple_gate", "w_ple")
FIRST = ("w_in", "w_uq", "w_ukv")
LATE = ("w_attn_o", "w_ssm_o", "w_out", "w_gate", "w_up", "w_down", "w_ple_gate", "w_ple")
COL_SHARDED = ("w_in", "w_uq", "w_ukv", "w_gate", "w_up", "w_ple")
SMALL = ("mix_norm_pre", "mix_norm_post", "q_norm", "kv_norm", "conv_b", "dt_bias", "a_log", "d_skip", "ssm_norm",
         "ffn_norm_pre", "ffn_norm_post", "ple_norm_pre", "ple_norm_post")
WEIGHTS = ("mix_norm_pre", "mix_norm_post", "w_in", "q_norm", "w_uq", "kv_norm", "w_ukv", "conv_w", "conv_b", "dt_bias",
           "a_log", "d_skip", "ssm_norm", "w_attn_o", "w_ssm_o", "w_out", "ffn_norm_pre", "ffn_norm_post", "w_gate",
           "w_up", "w_down", "ple_norm_pre", "ple_norm_post", "w_ple_gate", "w_ple")


def _step(x, p, positions, target, w, m, v):
    xi, yi, ci = _place()
    me = 4 * xi + 2 * yi + ci
    chip = 2 * xi + yi

    gathered = _all_gather("gather_weights", [w[n].astype(BF16) for n in FIRST])
    wts = {n: _assemble(n, blocks) for n, blocks in zip(FIRST, gathered)}
    cw_rows, cw_cols = w["conv_w"].shape
    (cw_all,) = _all_gather("gather_conv_w", [jnp.pad(w["conv_w"], ((0, SUBLANES - cw_rows), (0, 0)))])
    conv_w = cw_all[:, :cw_rows, :].transpose(1, 0, 2).reshape(cw_rows, N_DEV * cw_cols)

    small = {n: w[n] for n in SMALL}
    loss, grad_x, sg, d_conv_w, sent, recv = _local_step(
        x, p, positions, target, small, conv_w, wts, LATE, [w[n].astype(BF16) for n in LATE])

    sizes = [w[n].shape[1] for n in SMALL]
    sg_pack = _pack_small([sg[n] for n in SMALL] + [loss], d_conv_w)
    (sg_all,) = _all_gather("gather_small_grads", [sg_pack])
    (sg_sum,) = _rowwise("sum_small_grads", lambda *a: functools.reduce(lambda s, t: s + t, a),
                         [sg_all[k] for k in range(N_DEV)], [], [(sg_pack.shape[1], F32)])
    sg_vecs, d_conv_w_sum = _unpack_small(sg_sum, sizes + [LANES], d_conv_w.shape[1])
    loss = sg_vecs[-1][0, 0]
    grads = dict(zip(SMALL, sg_vecs[:-1]))
    grads["conv_w"] = lax.dynamic_slice_in_dim(d_conv_w_sum, me * cw_cols, cw_cols, axis=1)

    def sum8_then_adamw(wv, mv, vv, own, *others):
        g = functools.reduce(lambda s, t: s + t, others, own)
        return (g,) + _adamw(wv, g, mv, vv)

    delta, new_m, new_v = {}, {}, {}
    for n in BIG:
        cols = w[n].shape[1]
        own = lax.dynamic_index_in_dim(sent[n], me, axis=0, keepdims=False)
        grads[n], delta[n], new_m[n], new_v[n] = _rowwise(
            "adamw_" + n, sum8_then_adamw, [w[n], m[n], v[n], own] + [recv[n][k] for k in range(N_DEV - 1)], [],
            [(cols, F32)] * 4)
    packed = [_pack_small([d[n] for n in SMALL], d["conv_w"]) for d in (w, grads, m, v)]
    outs = _rowwise("adamw_small", _adamw, packed, [], [(packed[0].shape[1], F32)] * 3)
    for d, o in zip((delta, new_m, new_v), outs):
        vecs, mat = _unpack_small(o, sizes, cw_cols)
        d.update(zip(SMALL, vecs))
        d["conv_w"] = mat
    return loss, grad_x, grads, delta, new_m, new_v


def kernel(x, p, positions, mix_norm_pre, mix_norm_post, w_in, q_norm, w_uq, kv_norm, w_ukv, conv_w, conv_b, dt_bias, a_log, d_skip, ssm_norm, w_attn_o, w_ssm_o, w_out, ffn_norm_pre, ffn_norm_post, w_gate, w_up, w_down, ple_norm_pre, ple_norm_post, w_ple_gate, w_ple, loss_target, m_mix_norm_pre, m_mix_norm_post, m_w_in, m_q_norm, m_w_uq, m_kv_norm, m_w_ukv, m_conv_w, m_conv_b, m_dt_bias, m_a_log, m_d_skip, m_ssm_norm, m_w_attn_o, m_w_ssm_o, m_w_out, m_ffn_norm_pre, m_ffn_norm_post, m_w_gate, m_w_up, m_w_down, m_ple_norm_pre, m_ple_norm_post, m_w_ple_gate, m_w_ple, v_mix_norm_pre, v_mix_norm_post, v_w_in, v_q_norm, v_w_uq, v_kv_norm, v_w_ukv, v_conv_w, v_conv_b, v_dt_bias, v_a_log, v_d_skip, v_ssm_norm, v_w_attn_o, v_w_ssm_o, v_w_out, v_ffn_norm_pre, v_ffn_norm_post, v_w_gate, v_w_up, v_w_down, v_ple_norm_pre, v_ple_norm_post, v_w_ple_gate, v_w_ple):
    w_args = (mix_norm_pre, mix_norm_post, w_in, q_norm, w_uq, kv_norm, w_ukv, conv_w, conv_b, dt_bias, a_log, d_skip, ssm_norm, w_attn_o, w_ssm_o, w_out, ffn_norm_pre, ffn_norm_post, w_gate, w_up, w_down, ple_norm_pre, ple_norm_post, w_ple_gate, w_ple)
    m_args = (m_mix_norm_pre, m_mix_norm_post, m_w_in, m_q_norm, m_w_uq, m_kv_norm, m_w_ukv, m_conv_w, m_conv_b, m_dt_bias, m_a_log, m_d_skip, m_ssm_norm, m_w_attn_o, m_w_ssm_o, m_w_out, m_ffn_norm_pre, m_ffn_norm_post, m_w_gate, m_w_up, m_w_down, m_ple_norm_pre, m_ple_norm_post, m_w_ple_gate, m_w_ple)
    v_args = (v_mix_norm_pre, v_mix_norm_post, v_w_in, v_q_norm, v_w_uq, v_kv_norm, v_w_ukv, v_conv_w, v_conv_b, v_dt_bias, v_a_log, v_d_skip, v_ssm_norm, v_w_attn_o, v_w_ssm_o, v_w_out, v_ffn_norm_pre, v_ffn_norm_post, v_w_gate, v_w_up, v_w_down, v_ple_norm_pre, v_ple_norm_post, v_w_ple_gate, v_w_ple)

    def drop_layer(a):
        return a if a.ndim == 2 else a[0]

    w = {n: drop_layer(a) for n, a in zip(WEIGHTS, w_args)}
    m = {n: drop_layer(a) for n, a in zip(WEIGHTS, m_args)}
    v = {n: drop_layer(a) for n, a in zip(WEIGHTS, v_args)}
    loss, grad_x, grads, delta, new_m, new_v = _step(x[0], p[0, 0], positions[0], loss_target[0], w, m, v)
    like = lambda d: [d[n].reshape(a.shape) for n, a in zip(WEIGHTS, w_args)]
    return (loss, grad_x[None], *like(grads), *like(delta), *like(new_m), *like(new_v))
```

```python
import functools

import jax
import jax.numpy as jnp
from jax import lax
from jax.experimental import pallas as pl
from jax.experimental.pallas import tpu as pltpu

F32 = jnp.float32
BF16 = jnp.bfloat16

EPS = 1e-6
QK_NOPE = 128
QK_ROPE = 64
V_DIM = 128
QK_PAD = 256
ROPE_THETA = 10000.0
SSM_HEADDIM = 64
D_STATE = 128
CONV_WIDTH = 4
CHUNK = 256
ADAM_LR = 0.001
ADAM_B1 = 0.9
ADAM_B2 = 0.999
ADAM_EPS = 1e-08
ADAM_WD = 0.01
ADAM_STEP = 10

N_DEV = 8
LANES = 128
SUBLANES = 8
PACK_W = 1024
VMEM_LIMIT = 56 * 1024 * 1024
ROW_TILE_BYTES = 16 * 1024 * 1024
STRIP_ROWS = 16
STRIP_ELEMS = 64 * 1024
FLASH_T = 512
MM_TILE_BYTES = 20 * 1024 * 1024
MESH = pl.DeviceIdType.MESH


def _pick(dim, prefs):
    if dim <= prefs[0]:
        return dim
    for p in prefs:
        if dim % p == 0:
            return p
    return dim


def _tile(dim, cap):
    if dim <= cap:
        return dim
    best = None
    for t in range(LANES, cap + 1, LANES):
        if dim % t == 0:
            best = t
    return best if best is not None else dim


def _params(sem):
    return pltpu.CompilerParams(dimension_semantics=sem, vmem_limit_bytes=VMEM_LIMIT)


def _dot(a, b):
    return lax.dot_general(a, b, (((1,), (0,)), ((), ())), preferred_element_type=F32)


def _dot_nt(a, b):
    return lax.dot_general(a, b, (((1,), (1,)), ((), ())), preferred_element_type=F32)


def _dot_tn(a, b):
    return lax.dot_general(a, b, (((0,), (0,)), ((), ())), preferred_element_type=F32)


def _mm(name, a, b, mode, out_dtype=F32, acc_in=None, carry=None):
    if mode == "nn":
        (M, K), (K2, N) = a.shape, b.shape
    elif mode == "nt":
        (M, K), (N, K2) = a.shape, b.shape
    else:
        (K, M), (K2, N) = a.shape, b.shape
    assert K == K2, (name, a.shape, b.shape, mode)
    tm = _tile(M, 1024)
    tn = _tile(N, 1024 if acc_in is not None else 1536)
    tk = _tile(K, 2048)
    while tk > 512 and 2 * (tm * tk * a.dtype.itemsize + tk * tn * b.dtype.itemsize) > MM_TILE_BYTES:
        tk = _tile(K, tk - LANES)
    nk = K // tk
    dot = {"nn": _dot, "nt": _dot_nt, "tn": _dot_tn}[mode]
    has_acc = acc_in is not None

    def body(*refs):
        if has_acc:
            a_ref, b_ref, c_ref, o_ref, acc = refs
        else:
            a_ref, b_ref, o_ref, acc = refs
        k = pl.program_id(2)

        @pl.when(k == 0)
        def _():
            acc[...] = jnp.zeros_like(acc)

        acc[...] += dot(a_ref[...].astype(BF16), b_ref[...].astype(BF16))

        @pl.when(k == nk - 1)
        def _():
            r = acc[...]
            if has_acc:
                r = r + c_ref[...]
            o_ref[...] = r.astype(o_ref.dtype)

    if mode == "tn":
        a_spec = pl.BlockSpec((tk, tm), lambda i, j, k: (k, i))
    else:
        a_spec = pl.BlockSpec((tm, tk), lambda i, j, k: (i, k))
    if mode == "nt":
        b_spec = pl.BlockSpec((tn, tk), lambda i, j, k: (j, k))
    else:
        b_spec = pl.BlockSpec((tk, tn), lambda i, j, k: (k, j))
    o_spec = pl.BlockSpec((tm, tn), lambda i, j, k: (i, j))
    in_specs = [a_spec, b_spec] + ([o_spec] if has_acc else [])
    args = (a, b) + ((acc_in,) if has_acc else ())
    if carry is None:
        return pl.pallas_call(
            body, name=name, grid=(M // tm, N // tn, nk), in_specs=in_specs, out_specs=o_spec,
            out_shape=jax.ShapeDtypeStruct((M, N), out_dtype), scratch_shapes=[pltpu.VMEM((tm, tn), F32)],
            input_output_aliases=({2: 0} if has_acc and out_dtype == F32 else {}),
            compiler_params=_params(("parallel", "parallel", "arbitrary")),
        )(*args)
    grid = (M // tm, N // tn, nk)
    c_args, c_in_specs, c_out_specs, c_out_shapes, c_sems = _carry_call(carry)
    res = pl.pallas_call(
        _carried(body, len(args), 1, 1, carry, grid), name=name, grid=grid, in_specs=in_specs + c_in_specs,
        out_specs=[o_spec] + c_out_specs, out_shape=[jax.ShapeDtypeStruct((M, N), out_dtype)] + c_out_shapes,
        scratch_shapes=[pltpu.VMEM((tm, tn), F32)] + c_sems,
        compiler_params=_params(("arbitrary", "arbitrary", "arbitrary")),
    )(*args, *c_args)
    return res[0], list(res[1:])


def _row_tile(n_rows, bytes_per_row):
    tr = 1024
    while tr > SUBLANES and tr * bytes_per_row > ROW_TILE_BYTES:
        tr //= 2
    while n_rows % tr:
        tr //= 2
    return tr


def _strip_rows(width):
    return max(STRIP_ROWS, min(4 * STRIP_ROWS, STRIP_ELEMS // width // STRIP_ROWS * STRIP_ROWS))


def _over_strips(tr, strip, work):
    if strip is None or tr <= strip or tr % strip:
        work(slice(None))
        return

    def one(s, carry):
        work(pl.ds(pl.multiple_of(s * strip, strip), strip))
        return carry

    lax.fori_loop(0, tr // strip, one, 0)


def _acc_add(a_ref, v):
    if v.shape[0] == 1:
        a_ref[0:1, :] += v
    else:
        a_ref[...] += v.reshape(v.shape[0] // SUBLANES, SUBLANES, v.shape[1]).sum(axis=0)


def _rowwise(name, fn, rows, bcs, outs, accs=(), tr=None):
    n_rows = rows[0].shape[0]
    strip = _strip_rows(max(r.shape[1] for r in rows)) if tr is None else None
    if tr is None:
        per_row = sum(r.shape[1] * r.dtype.itemsize for r in rows) + sum(w * jnp.dtype(d).itemsize for w, d in outs)
        tr = _row_tile(n_rows, per_row)
    n_r, n_b, n_o, n_a = len(rows), len(bcs), len(outs), len(accs)

    def body(*refs):
        r_refs, b_refs = refs[:n_r], refs[n_r: n_r + n_b]
        o_refs = refs[n_r + n_b: n_r + n_b + n_o]
        a_refs = refs[n_r + n_b + n_o:]
        if n_a:
            @pl.when(pl.program_id(0) == 0)
            def _():
                for a in a_refs:
                    a[...] = jnp.zeros_like(a)

        def work(rws):
            ins = [r[rws, :].astype(F32) for r in r_refs] + [b[...].astype(F32) for b in b_refs]
            res = fn(*ins)
            res = res if isinstance(res, (tuple, list)) else (res,)
            for o, v in zip(o_refs, res[:n_o]):
                o[rws, :] = v.astype(o.dtype)
            for a, v in zip(a_refs, res[n_o:]):
                _acc_add(a, v)

        _over_strips(tr, strip, work)

    in_specs = [pl.BlockSpec((tr, r.shape[1]), lambda i: (i, 0)) for r in rows]
    in_specs += [pl.BlockSpec((1, b.shape[1]), lambda i: (0, 0)) for b in bcs]
    out_specs = [pl.BlockSpec((tr, w), lambda i: (i, 0)) for w, _ in outs]
    out_specs += [pl.BlockSpec((SUBLANES, w), lambda i: (0, 0)) for w in accs]
    out_shape = [jax.ShapeDtypeStruct((n_rows, w), d) for w, d in outs]
    out_shape += [jax.ShapeDtypeStruct((SUBLANES, w), F32) for w in accs]
    res = pl.pallas_call(
        body, name=name, grid=(n_rows // tr,), in_specs=in_specs, out_specs=out_specs, out_shape=out_shape,
        compiler_params=_params(("arbitrary",) if n_a else ("parallel",)),
    )(*rows, *bcs)
    return tuple(res)


def _rowwise_bwd(name, fn, rows, bcs, cts, need_rows, row_dtypes, need_bcs=None, adds=None, fwd_sums=(), tr=None,
                 into=None):
    n_rows = rows[0].shape[0]
    adds = adds or {}
    into = into or {}
    into_keys = sorted(into)
    need_bcs = list(range(len(bcs))) if need_bcs is None else list(need_bcs)
    ct_arrays = [c for c in cts if not isinstance(c, float)]
    add_keys = sorted(adds)
    add_arrays = [adds[k] for k in add_keys]
    strip = _strip_rows(max(r.shape[1] for r in rows)) if tr is None else None
    if tr is None:
        per_row = sum(r.shape[1] * r.dtype.itemsize for r in list(rows) + ct_arrays + add_arrays)
        per_row += sum(rows[i].shape[1] * jnp.dtype(d).itemsize for i, d in zip(need_rows, row_dtypes))
        tr = _row_tile(n_rows, per_row)
    n_r, n_b, n_c, n_ad = len(rows), len(bcs), len(ct_arrays), len(add_arrays)
    n_go, n_gb, n_fs = len(need_rows), len(need_bcs), len(fwd_sums)

    def body(*refs):
        pos = [0]

        def take(n):
            pos[0] += n
            return refs[pos[0] - n: pos[0]]

        r_refs, b_refs, c_refs, ad_refs = take(n_r), take(n_b), take(n_c), take(n_ad)
        take(len(into_keys))
        go_refs = take(n_go)
        acc_refs = refs[pos[0]:]

        @pl.when(pl.program_id(0) == 0)
        def _():
            for a in acc_refs:
                a[...] = jnp.zeros_like(a)

        def wrapped(*a):
            r = fn(*a)
            return tuple(r) if isinstance(r, (tuple, list)) else (r,)

        def work(rws):
            r_t = [r[rws, :].astype(F32) for r in r_refs]
            b_t = [r[...].astype(F32) for r in b_refs]
            outs, vjp = jax.vjp(wrapped, *r_t, *b_t)
            it = iter(c_refs)
            full = tuple(jnp.full(o.shape, c, F32) if isinstance(c, float) else next(it)[rws, :].astype(F32)
                         for o, c in zip(outs, cts))
            grads = vjp(full)
            for o_ref, i in zip(go_refs, need_rows):
                g = grads[i]
                if i in adds:
                    g = g + ad_refs[add_keys.index(i)][rws, :].astype(F32)
                o_ref[rws, :] = g.astype(o_ref.dtype)
            for a, j in zip(acc_refs[:n_gb], need_bcs):
                _acc_add(a, grads[n_r + j])
            for a, j in zip(acc_refs[n_gb:], fwd_sums):
                a[0:1, :] += jnp.full((1, LANES), jnp.sum(outs[j]), F32)

        _over_strips(tr, strip, work)

    def row_spec(w):
        return pl.BlockSpec((tr, w), lambda i: (i, 0))

    in_specs = [row_spec(r.shape[1]) for r in rows]
    in_specs += [pl.BlockSpec((1, b.shape[1]), lambda i: (0, 0)) for b in bcs]
    in_specs += [row_spec(c.shape[1]) for c in ct_arrays] + [row_spec(a.shape[1]) for a in add_arrays]
    in_specs += [pl.BlockSpec(memory_space=pl.ANY) for _ in into_keys]
    out_specs = [row_spec(rows[i].shape[1]) for i in need_rows]
    out_shape = [jax.ShapeDtypeStruct((n_rows, rows[i].shape[1]), d) for i, d in zip(need_rows, row_dtypes)]
    aliases = {}
    for pos_in, k in enumerate(into_keys):
        buf, col_block = into[k]
        out_specs[k] = pl.BlockSpec((tr, rows[need_rows[k]].shape[1]), lambda i, cb=col_block: (i, cb))
        out_shape[k] = jax.ShapeDtypeStruct(buf.shape, buf.dtype)
        aliases[len(in_specs) - len(into_keys) + pos_in] = k
    out_specs += [pl.BlockSpec((SUBLANES, bcs[j].shape[1]), lambda i: (0, 0)) for j in need_bcs]
    out_specs += [pl.BlockSpec((SUBLANES, LANES), lambda i: (0, 0)) for _ in fwd_sums]
    out_shape += [jax.ShapeDtypeStruct((SUBLANES, bcs[j].shape[1]), F32) for j in need_bcs]
    out_shape += [jax.ShapeDtypeStruct((SUBLANES, LANES), F32) for _ in fwd_sums]
    res = pl.pallas_call(
        body, name=name, grid=(n_rows // tr,), in_specs=in_specs, out_specs=out_specs, out_shape=out_shape,
        input_output_aliases=aliases, compiler_params=_params(("arbitrary",)),
    )(*rows, *bcs, *ct_arrays, *add_arrays, *[into[k][0] for k in into_keys])
    return tuple(res)


def _fold(acc):
    return jnp.sum(acc, axis=0, keepdims=True)


def _rms(x, w):
    return x * lax.rsqrt(jnp.mean(x * x, axis=-1, keepdims=True) + EPS) * w


def _sigmoid(x):
    return jax.nn.sigmoid(x)


def _silu(x):
    return x * _sigmoid(x)


def _log1p(u):
    series = u * (1.0 - u * (0.5 - u * (1.0 / 3.0 - u * 0.25)))
    return jnp.where(u < 0.01, series, jnp.log(1.0 + u))


def _softplus(x):
    return jnp.maximum(x, 0.0) + _log1p(jnp.exp(-jnp.abs(x)))


def _st_pre(x, w):
    return _rms(x, w)


def _st_lora_norms(dq):
    def fn(cqkv, qn, kvn):
        return _rms(cqkv[:, :dq], qn), _rms(cqkv[:, dq:], kvn)
    return fn


def _st_gated_norm(n_groups):
    def fn(y, xs, z, dskip, wn):
        yz = (y + dskip * xs) * _silu(z)
        gw = yz.shape[1] // n_groups
        parts = [_rms(yz[:, g * gw:(g + 1) * gw], wn[:, g * gw:(g + 1) * gw]) for g in range(n_groups)]
        return jnp.concatenate(parts, axis=1)
    return fn


def _st_mix(d):
    def fn(g, ao, so):
        return _sigmoid(g[:, :d]) * ao + _sigmoid(g[:, d:]) * so
    return fn


def _st_res_norm(h, y, w_post, w_pre):
    h2 = h + _rms(y, w_post)
    return h2, _rms(h2, w_pre)


def _st_swiglu(gt, up):
    return _silu(gt) * up


def _st_loss(pe, gl, h2, tgt, w_post):
    e = pe * _sigmoid(gl)
    diff = h2 + _rms(e, w_post) - tgt
    return 0.5 * jnp.mean(diff * diff, axis=-1, keepdims=True)


def _rope_tables(positions):
    half = QK_ROPE // 2
    inv_freq = ROPE_THETA ** (-jnp.arange(0, QK_ROPE, 2, dtype=F32) / QK_ROPE)
    ang = positions.astype(F32).reshape(-1, 1) * inv_freq
    cos, sin = jnp.cos(ang), jnp.sin(ang)
    n = ang.shape[0]
    z = lambda w: jnp.zeros((n, w), F32)
    c_tab = jnp.concatenate([jnp.ones((n, QK_NOPE), F32), cos, cos, z(QK_PAD - QK_NOPE - QK_ROPE)], axis=1)
    a_tab = jnp.concatenate([z(QK_NOPE), -sin, z(half), z(QK_PAD - QK_NOPE - QK_ROPE)], axis=1)
    b_tab = jnp.concatenate([z(QK_NOPE), z(half), sin, z(QK_PAD - QK_NOPE - QK_ROPE)], axis=1)
    return c_tab, a_tab, b_tab


def _rot(x, c, a, b):
    half = QK_ROPE // 2
    return x * c + pltpu.roll(x, QK_PAD - half, axis=1) * a + pltpu.roll(x, half, axis=1) * b


def _rot_t(g, c, a, b):
    half = QK_ROPE // 2
    return g * c + pltpu.roll(g * a, half, axis=1) + pltpu.roll(g * b, QK_PAD - half, axis=1)


def _st_rope(n_heads):
    def fn(qraw, kraw, sm, c, a, b):
        kpe = _rot(sm[:, :QK_PAD], c, a, b)
        scale = float(QK_NOPE + QK_ROPE) ** -0.5
        q = [_rot(qraw[:, h * QK_PAD:(h + 1) * QK_PAD], c, a, b) * scale for h in range(n_heads)]
        k = [kraw[:, h * QK_PAD:(h + 1) * QK_PAD] + kpe for h in range(n_heads)]
        return jnp.concatenate(q, axis=1), jnp.concatenate(k, axis=1)
    return fn


def _st_rope_bwd(n_heads):
    def fn(dq, dk, c, a, b):
        dqraw = [_rot_t(dq[:, h * QK_PAD:(h + 1) * QK_PAD], c, a, b) for h in range(n_heads)]
        dks = dk[:, :QK_PAD]
        for h in range(1, n_heads):
            dks = dks + dk[:, h * QK_PAD:(h + 1) * QK_PAD]
        return jnp.concatenate(dqraw, axis=1), _rot_t(dks, c, a, b), dk
    return fn


def _split3(x):
    h1 = x.astype(BF16)
    r1 = x - h1.astype(F32)
    h2 = r1.astype(BF16)
    h3 = (r1 - h2.astype(F32)).astype(BF16)
    return h1, h2, h3


def _tri_dot(tri, x):
    h1, h2, h3 = _split3(x)
    return (_dot(tri, h3) + _dot(tri, h2)) + _dot(tri, h1)


def _dot_tri(x, tri):
    h1, h2, h3 = _split3(x)
    return (_dot(h3, tri) + _dot(h2, tri)) + _dot(h1, tri)


def _st_dt(sm, bias, alog):
    x = sm[:, QK_PAD:] + bias
    dt = _softplus(x)
    return dt, dt * (-jnp.exp(alog))


def _st_dt_bwd(sm, ddt, dcum, bias, alog):
    n = sm.shape[0]
    i = lax.broadcasted_iota(jnp.int32, (n, n), 0)
    j = lax.broadcasted_iota(jnp.int32, (n, n), 1)
    upper = (j >= i).astype(BF16)
    dda = _tri_dot(upper, dcum)
    x = sm[:, QK_PAD:] + bias
    dt = _softplus(x)
    a = -jnp.exp(alog)
    draw = (ddt + dda * a) * _sigmoid(x)
    return draw, draw, dda * dt * a


def _conv_fwd(xbc, w, b):
    S, C = xbc.shape
    tr = _pick(S, (512, 256))
    tc = _pick(C, (1024, 512, 256, 128))
    hb = tr // SUBLANES

    def body(x_ref, halo_ref, w_ref, b_ref, c_ref, a_ref, ext):
        i = pl.program_id(1)
        halo = jnp.where(i == 0, 0.0, halo_ref[...])
        ext[0:SUBLANES, :] = halo
        ext[SUBLANES:, :] = x_ref[...]
        wv = w_ref[...]
        acc = b_ref[...] + wv[CONV_WIDTH - 1:CONV_WIDTH, :] * x_ref[...]
        for k in range(CONV_WIDTH - 1):
            off = SUBLANES - (CONV_WIDTH - 1) + k
            acc = acc + wv[k:k + 1, :] * ext[pl.ds(off, tr), :]
        c_ref[...] = acc
        a_ref[...] = _silu(acc)

    return pl.pallas_call(
        body, name="conv_fwd", grid=(C // tc, S // tr),
        in_specs=[pl.BlockSpec((tr, tc), lambda j, i: (i, j)),
                  pl.BlockSpec((SUBLANES, tc), lambda j, i: (jnp.maximum(i * hb - 1, 0), j)),
                  pl.BlockSpec((CONV_WIDTH, tc), lambda j, i: (0, j)),
                  pl.BlockSpec((1, tc), lambda j, i: (0, j))],
        out_specs=[pl.BlockSpec((tr, tc), lambda j, i: (i, j))] * 2,
        out_shape=[jax.ShapeDtypeStruct((S, C), F32)] * 2,
        scratch_shapes=[pltpu.VMEM((tr + SUBLANES, tc), F32)],
        compiler_params=_params(("parallel", "arbitrary")),
    )(xbc, xbc, w, b)


def _conv_bwd(xbc, dconv, w, buf, col0):
    S, C = xbc.shape
    tr = _pick(S, (512, 256))
    tc = _pick(C, (1024, 512, 256, 128))
    hb = tr // SUBLANES
    n_i = S // tr
    assert col0 % tc == 0, (col0, tc)
    cb0 = col0 // tc

    def body(x_ref, halo_ref, d_ref, dnext_ref, w_ref, buf_ref, dx_ref, dw_ref, ext, dext):
        i = pl.program_id(1)
        ext[0:SUBLANES, :] = jnp.where(i == 0, 0.0, halo_ref[...])
        ext[SUBLANES:, :] = x_ref[...]
        dext[0:tr, :] = d_ref[...]
        dext[tr:, :] = jnp.where(i == n_i - 1, 0.0, dnext_ref[...])
        wv = w_ref[...]
        d = d_ref[...]

        @pl.when(i == 0)
        def _():
            dw_ref[...] = jnp.zeros_like(dw_ref)

        dx = wv[CONV_WIDTH - 1:CONV_WIDTH, :] * d
        for k in range(CONV_WIDTH):
            if k < CONV_WIDTH - 1:
                dx = dx + wv[k:k + 1, :] * dext[pl.ds(CONV_WIDTH - 1 - k, tr), :]
                xs = ext[pl.ds(SUBLANES - (CONV_WIDTH - 1) + k, tr), :]
            else:
                xs = x_ref[...]
            prod = d * xs
            dw_ref[k * SUBLANES:(k + 1) * SUBLANES, :] += prod.reshape(tr // SUBLANES, SUBLANES, tc).sum(axis=0)
        dx_ref[...] = dx.astype(dx_ref.dtype)

    return pl.pallas_call(
        body, name="conv_bwd", grid=(C // tc, n_i),
        in_specs=[pl.BlockSpec((tr, tc), lambda j, i: (i, j)),
                  pl.BlockSpec((SUBLANES, tc), lambda j, i: (jnp.maximum(i * hb - 1, 0), j)),
                  pl.BlockSpec((tr, tc), lambda j, i: (i, j)),
                  pl.BlockSpec((SUBLANES, tc), lambda j, i: (jnp.minimum((i + 1) * hb, S // SUBLANES - 1), j)),
                  pl.BlockSpec((CONV_WIDTH, tc), lambda j, i: (0, j)),
                  pl.BlockSpec(memory_space=pl.ANY)],
        out_specs=[pl.BlockSpec((tr, tc), lambda j, i: (i, cb0 + j)),
                   pl.BlockSpec((CONV_WIDTH * SUBLANES, tc), lambda j, i: (0, j))],
        out_shape=[jax.ShapeDtypeStruct(buf.shape, buf.dtype), jax.ShapeDtypeStruct((CONV_WIDTH * SUBLANES, C), F32)],
        scratch_shapes=[pltpu.VMEM((tr + SUBLANES, tc), F32), pltpu.VMEM((tr + SUBLANES, tc), F32)],
        input_output_aliases={5: 0},
        compiler_params=_params(("parallel", "arbitrary")),
    )(xbc, xbc, dconv, dconv, w, buf)


def _st_dconv(d_inner):
    def fn(xc, dxa, dxb, db_, dc_):
        s = _sigmoid(xc)
        g = jnp.concatenate([dxa + dxb, db_, dc_], axis=1) * (s * (1.0 + xc * (1.0 - s)))
        return g, g
    return fn


def _chunk_setup(b_ref, c_ref, dac_ref, dar_ref, L):
    ii = lax.broadcasted_iota(jnp.int32, (L, L), 0)
    jj = lax.broadcasted_iota(jnp.int32, (L, L), 1)
    tri = ii >= jj
    cum_c = _tri_dot(tri.astype(BF16), dac_ref[0])
    cum_r = _dot_tri(dar_ref[0], (ii <= jj).astype(BF16))
    bm = b_ref[...].astype(BF16)
    cm = c_ref[...].astype(BF16)
    return tri, cum_c, cum_r, bm, cm, _dot_nt(cm, bm)


def _ssd_specs(d_inner, n_groups, gw, L, rp, chunk_of):
    bb0 = d_inner // D_STATE
    cb0 = bb0 + n_groups
    return [pl.BlockSpec((L, gw), lambda g, c: (chunk_of(c), g)),
            pl.BlockSpec((L, D_STATE), lambda g, c: (chunk_of(c), bb0 + g)),
            pl.BlockSpec((L, D_STATE), lambda g, c: (chunk_of(c), cb0 + g)),
            pl.BlockSpec((1, L, LANES), lambda g, c: (g, chunk_of(c), 0)),
            pl.BlockSpec((1, L, LANES), lambda g, c: (g, chunk_of(c), 0)),
            pl.BlockSpec((1, rp, L), lambda g, c: (g, 0, chunk_of(c)))]


def _ssd_fwd(xbc_a, dt_col, da_col, da_row, d_inner, n_groups, R):
    S = xbc_a.shape[0]
    L = min(CHUNK, S)
    NC = S // L
    P, N = SSM_HEADDIM, D_STATE
    gw = R * P
    rp = da_row.shape[1]

    def body(x_ref, b_ref, c_ref, dt_ref, dac_ref, dar_ref, y_ref, st_ref, state):
        @pl.when(pl.program_id(1) == 0)
        def _():
            state[...] = jnp.zeros_like(state)

        st_ref[0, 0] = state[...]
        tri, cum_c, cum_r, bm, cm, gm = _chunk_setup(b_ref, c_ref, dac_ref, dar_ref, L)
        dt = dt_ref[0]
        for r in range(R):
            cc = cum_c[:, r:r + 1]
            cr = cum_r[r:r + 1, :]
            lam = jnp.exp(jnp.where(tri, cc - cr, -jnp.inf))
            m = (gm * lam).astype(BF16)
            x = x_ref[:, r * P:(r + 1) * P] * dt[:, r:r + 1]
            s_r = state[r * P:(r + 1) * P, :]
            y_off = _dot_nt(cm, s_r.astype(BF16)) * jnp.exp(cc)
            y_ref[:, r * P:(r + 1) * P] = _dot(m, x.astype(BF16)) + y_off
            last = cr[:, L - 1:L]
            xw = (x * jnp.exp(last - cc)).astype(BF16)
            state[r * P:(r + 1) * P, :] = s_r * jnp.exp(last) + _dot_tn(xw, bm)

    return pl.pallas_call(
        body, name="ssd_fwd", grid=(n_groups, NC),
        in_specs=_ssd_specs(d_inner, n_groups, gw, L, rp, lambda c: c),
        out_specs=[pl.BlockSpec((L, gw), lambda g, c: (c, g)),
                   pl.BlockSpec((1, 1, gw, N), lambda g, c: (g, c, 0, 0))],
        out_shape=[jax.ShapeDtypeStruct((S, d_inner), F32), jax.ShapeDtypeStruct((n_groups, NC, gw, N), F32)],
        scratch_shapes=[pltpu.VMEM((gw, N), F32)],
        compiler_params=_params(("parallel", "arbitrary")),
    )(xbc_a, xbc_a, xbc_a, dt_col, da_col, da_row)


def _ssd_bwd(xbc_a, dt_col, da_col, da_row, states, dy, d_inner, n_groups, R):
    S = xbc_a.shape[0]
    L = min(CHUNK, S)
    NC = S // L
    P, N = SSM_HEADDIM, D_STATE
    gw = R * P
    rp = da_row.shape[1]
    rev = lambda c: NC - 1 - c

    def body(x_ref, b_ref, c_ref, dt_ref, dac_ref, dar_ref, st_ref, dy_ref,
             dx_ref, db_ref, dc_ref, ddt_ref, dcc_ref, dcr_ref, dstate):
        @pl.when(pl.program_id(1) == 0)
        def _():
            dstate[...] = jnp.zeros_like(dstate)

        tri, cum_c, cum_r, bm, cm, gm = _chunk_setup(b_ref, c_ref, dac_ref, dar_ref, L)
        dt = dt_ref[0]
        lane = lax.broadcasted_iota(jnp.int32, (L, LANES), 1)
        sub = lax.broadcasted_iota(jnp.int32, (rp, L), 0)
        is_last = lax.broadcasted_iota(jnp.int32, (L, 1), 0) == L - 1
        d_g = jnp.zeros((L, L), F32)
        dc_acc = jnp.zeros((L, N), F32)
        db_acc = jnp.zeros((L, N), F32)
        ddt_out = jnp.zeros((L, LANES), F32)
        dcc_out = jnp.zeros((L, LANES), F32)
        dcr_out = jnp.zeros((rp, L), F32)
        for r in range(R):
            cc = cum_c[:, r:r + 1]
            cr = cum_r[r:r + 1, :]
            lam = jnp.exp(jnp.where(tri, cc - cr, -jnp.inf))
            m = gm * lam
            dtc = dt[:, r:r + 1]
            xh = x_ref[:, r * P:(r + 1) * P]
            x = xh * dtc
            xb = x.astype(BF16)
            d_y = dy_ref[:, r * P:(r + 1) * P]
            d_yb = d_y.astype(BF16)
            s_r = st_ref[0, 0, r * P:(r + 1) * P, :]
            s_rb = s_r.astype(BF16)
            ds_n = dstate[r * P:(r + 1) * P, :]
            ds_nb = ds_n.astype(BF16)
            e = jnp.exp(cc)
            last = cr[:, L - 1:L]
            e_last = jnp.exp(last)
            w = jnp.exp(last - cc)
            d_m = _dot_nt(d_yb, xb)
            d_x = _dot_tn(m.astype(BF16), d_yb)
            d_ye = (d_y * e).astype(BF16)
            dc_acc = dc_acc + _dot(d_ye, s_rb)
            ds_part = _dot_tn(d_ye, cm)
            y_off = _dot_nt(cm, s_rb) * e
            dcum = jnp.sum(d_y * y_off, axis=1, keepdims=True)
            d_xw = _dot_nt(bm, ds_nb)
            d_x = d_x + d_xw * w
            dw_w = jnp.sum(d_xw * x, axis=1, keepdims=True) * w
            db_acc = db_acc + _dot((x * w).astype(BF16), ds_nb)
            d_last = jnp.sum(ds_n * s_r, keepdims=True) * e_last + jnp.sum(dw_w, keepdims=True)
            dcum = dcum - dw_w
            dstate[r * P:(r + 1) * P, :] = e_last * ds_n + ds_part
            d_g = d_g + d_m * lam
            q = d_m * m
            dcum = dcum + jnp.sum(q, axis=1, keepdims=True) + jnp.where(is_last, d_last, 0.0)
            dcum_row = -jnp.sum(q, axis=0, keepdims=True)
            dx_ref[:, r * P:(r + 1) * P] = d_x * dtc
            ddt = jnp.sum(d_x * xh, axis=1, keepdims=True)
            ddt_out = ddt_out + jnp.where(lane == r, ddt, 0.0)
            dcc_out = dcc_out + jnp.where(lane == r, dcum, 0.0)
            dcr_out = dcr_out + jnp.where(sub == r, dcum_row, 0.0)
        d_gb = d_g.astype(BF16)
        dc_ref[...] = dc_acc + _dot(d_gb, bm)
        db_ref[...] = db_acc + _dot_tn(d_gb, cm)
        ddt_ref[0] = ddt_out
        dcc_ref[0] = dcc_out
        dcr_ref[0] = dcr_out

    gn = n_groups * N
    return pl.pallas_call(
        body, name="ssd_bwd", grid=(n_groups, NC),
        in_specs=_ssd_specs(d_inner, n_groups, gw, L, rp, rev) + [
            pl.BlockSpec((1, 1, gw, N), lambda g, c: (g, rev(c), 0, 0)),
            pl.BlockSpec((L, gw), lambda g, c: (rev(c), g))],
        out_specs=[pl.BlockSpec((L, gw), lambda g, c: (rev(c), g)),
                   pl.BlockSpec((L, N), lambda g, c: (rev(c), g)),
                   pl.BlockSpec((L, N), lambda g, c: (rev(c), g)),
                   pl.BlockSpec((1, L, LANES), lambda g, c: (g, rev(c), 0)),
                   pl.BlockSpec((1, L, LANES), lambda g, c: (g, rev(c), 0)),
                   pl.BlockSpec((1, rp, L), lambda g, c: (g, 0, rev(c)))],
        out_shape=[jax.ShapeDtypeStruct((S, d_inner), F32), jax.ShapeDtypeStruct((S, gn), F32),
                   jax.ShapeDtypeStruct((S, gn), F32), jax.ShapeDtypeStruct((n_groups, S, LANES), F32),
                   jax.ShapeDtypeStruct((n_groups, S, LANES), F32), jax.ShapeDtypeStruct((n_groups, rp, S), F32)],
        scratch_shapes=[pltpu.VMEM((gw, N), F32)],
        compiler_params=_params(("parallel", "arbitrary")),
    )(xbc_a, xbc_a, xbc_a, dt_col, da_col, da_row, states, dy)


def _chunk_setup_t(b_ref, c_ref, dac_ref, dar_ref, L):
    ii = lax.broadcasted_iota(jnp.int32, (L, L), 0)
    jj = lax.broadcasted_iota(jnp.int32, (L, L), 1)
    lower = ii >= jj
    upper = ii <= jj
    cum_c = _tri_dot(lower.astype(BF16), dac_ref[0])
    cum_r = _dot_tri(dar_ref[0], upper.astype(BF16))
    bm = b_ref[...].astype(BF16)
    cm = c_ref[...].astype(BF16)
    return lower, upper, cum_c, cum_r, bm, cm


def _ssd_specs_t(d_inner, n_groups, gw, L, rp, chunk_of):
    bb0 = d_inner // D_STATE
    cb0 = bb0 + n_groups
    return [pl.BlockSpec((L, gw), lambda g, c: (chunk_of(c), g)),
            pl.BlockSpec((L, D_STATE), lambda g, c: (chunk_of(c), bb0 + g)),
            pl.BlockSpec((L, D_STATE), lambda g, c: (chunk_of(c), cb0 + g)),
            pl.BlockSpec((1, rp, L), lambda g, c: (g, 0, chunk_of(c))),
            pl.BlockSpec((1, L, LANES), lambda g, c: (g, chunk_of(c), 0)),
            pl.BlockSpec((1, rp, L), lambda g, c: (g, 0, chunk_of(c)))]


def _ssd_fwd_t(xbc_a, dt_row, da_col, da_row, d_inner, n_groups, R):
    S = xbc_a.shape[0]
    L = min(CHUNK, S)
    NC = S // L
    P, N = SSM_HEADDIM, D_STATE
    gw = R * P
    rp = da_row.shape[1]

    def body(x_ref, b_ref, c_ref, dt_ref, dac_ref, dar_ref, y_ref, st_ref, state, y_t):
        @pl.when(pl.program_id(1) == 0)
        def _():
            state[...] = jnp.zeros_like(state)

        st_ref[0, 0] = state[...]
        lower, upper, cum_c, cum_r, bm, cm = _chunk_setup_t(b_ref, c_ref, dac_ref, dar_ref, L)
        gm_t = _dot_nt(bm, cm)
        dt = dt_ref[0]
        x_t = x_ref[...].T
        for r in range(R):
            cc = cum_c[:, r:r + 1]
            cr = cum_r[r:r + 1, :]
            m_t = (gm_t * jnp.exp(jnp.where(upper, cr - cc, -jnp.inf))).astype(BF16)
            x = x_t[r * P:(r + 1) * P, :] * dt[r:r + 1, :]
            s_r = state[r * P:(r + 1) * P, :]
            y_off = _dot_nt(s_r.astype(BF16), cm) * jnp.exp(cr)
            y_t[r * P:(r + 1) * P, :] = _dot(x.astype(BF16), m_t) + y_off
            last = cr[:, L - 1:L]
            xw = (x * jnp.exp(last - cr)).astype(BF16)
            state[r * P:(r + 1) * P, :] = s_r * jnp.exp(last) + _dot(xw, bm)
        y_ref[...] = y_t[...].T

    return pl.pallas_call(
        body, name="ssd_fwd", grid=(n_groups, NC),
        in_specs=_ssd_specs_t(d_inner, n_groups, gw, L, rp, lambda c: c),
        out_specs=[pl.BlockSpec((L, gw), lambda g, c: (c, g)),
                   pl.BlockSpec((1, 1, gw, N), lambda g, c: (g, c, 0, 0))],
        out_shape=[jax.ShapeDtypeStruct((S, d_inner), F32), jax.ShapeDtypeStruct((n_groups, NC, gw, N), F32)],
        scratch_shapes=[pltpu.VMEM((gw, N), F32), pltpu.VMEM((gw, L), F32)],
        compiler_params=_params(("parallel", "arbitrary")),
    )(xbc_a, xbc_a, xbc_a, dt_row, da_col, da_row)


def _ssd_bwd_t(xbc_a, dt_row, da_col, da_row, states, dy, d_inner, n_groups, R):
    S = xbc_a.shape[0]
    L = min(CHUNK, S)
    NC = S // L
    P, N = SSM_HEADDIM, D_STATE
    gw = R * P
    rp = da_row.shape[1]
    rev = lambda c: NC - 1 - c

    def body(x_ref, b_ref, c_ref, dt_ref, dac_ref, dar_ref, st_ref, dy_ref,
             dx_ref, db_ref, dc_ref, ddt_ref, dcum_ref, dstate, dx_t):
        @pl.when(pl.program_id(1) == 0)
        def _():
            dstate[...] = jnp.zeros_like(dstate)

        lower, upper, cum_c, cum_r, bm, cm = _chunk_setup_t(b_ref, c_ref, dac_ref, dar_ref, L)
        gm = _dot_nt(cm, bm)
        gm_t = _dot_nt(bm, cm)
        dt = dt_ref[0]
        x_t = x_ref[...].T
        dy_t = dy_ref[...].T
        sub = lax.broadcasted_iota(jnp.int32, (rp, L), 0)
        is_last = lax.broadcasted_iota(jnp.int32, (1, L), 1) == L - 1
        d_g = jnp.zeros((L, L), F32)
        d_g_t = jnp.zeros((L, L), F32)
        dc_acc = jnp.zeros((L, N), F32)
        db_acc = jnp.zeros((L, N), F32)
        ddt_out = jnp.zeros((rp, L), F32)
        dcum_out = jnp.zeros((rp, L), F32)
        for r in range(R):
            cc = jnp.broadcast_to(cum_c[:, r:r + 1], (L, L))
            cr = cum_r[r:r + 1, :]
            lam = jnp.exp(jnp.where(lower, cc - cum_r[r:r + 1, :], -jnp.inf))
            lam_t = jnp.exp(jnp.where(upper, cr - cc, -jnp.inf))
            m = gm * lam
            m_t = gm_t * lam_t
            dtr = dt[r:r + 1, :]
            xh = x_t[r * P:(r + 1) * P, :]
            x = xh * dtr
            xb = x.astype(BF16)
            d_y = dy_t[r * P:(r + 1) * P, :]
            d_yb = d_y.astype(BF16)
            s_r = st_ref[0, 0, r * P:(r + 1) * P, :]
            s_rb = s_r.astype(BF16)
            ds_n = dstate[r * P:(r + 1) * P, :]
            ds_nb = ds_n.astype(BF16)
            e = jnp.exp(cr)
            last = cr[:, L - 1:L]
            e_last = jnp.exp(last)
            w = jnp.exp(last - cr)
            d_x = _dot(d_yb, m.astype(BF16))
            d_m = _dot_tn(d_yb, xb)
            d_m_t = _dot_tn(xb, d_yb)
            d_ye = (d_y * e).astype(BF16)
            dc_acc = dc_acc + _dot_tn(d_ye, s_rb)
            ds_part = _dot(d_ye, cm)
            y_off = _dot_nt(s_rb, cm) * e
            dcum = jnp.sum(d_y * y_off, axis=0, keepdims=True)
            d_xw = _dot_nt(ds_nb, bm)
            d_x = d_x + d_xw * w
            dw_w = jnp.sum(d_xw * x, axis=0, keepdims=True) * w
            db_acc = db_acc + _dot_tn((x * w).astype(BF16), ds_nb)
            d_last = jnp.sum(ds_n * s_r, keepdims=True) * e_last + jnp.sum(dw_w, keepdims=True)
            dstate[r * P:(r + 1) * P, :] = e_last * ds_n + ds_part
            d_g = d_g + d_m * lam
            d_g_t = d_g_t + d_m_t * lam_t
            dcum = (dcum - dw_w + jnp.sum(d_m_t * m_t, axis=0, keepdims=True)
                    - jnp.sum(d_m * m, axis=0, keepdims=True) + jnp.where(is_last, d_last, 0.0))
            dx_t[r * P:(r + 1) * P, :] = d_x * dtr
            ddt = jnp.sum(d_x * xh, axis=0, keepdims=True)
            ddt_out = ddt_out + jnp.where(sub == r, ddt, 0.0)
            dcum_out = dcum_out + jnp.where(sub == r, dcum, 0.0)
        dc_ref[...] = dc_acc + _dot(d_g.astype(BF16), bm)
        db_ref[...] = db_acc + _dot(d_g_t.astype(BF16), cm)
        dx_ref[...] = dx_t[...].T
        ddt_ref[0] = ddt_out
        dcum_ref[0] = dcum_out

    gn = n_groups * N
    return pl.pallas_call(
        body, name="ssd_bwd", grid=(n_groups, NC),
        in_specs=_ssd_specs_t(d_inner, n_groups, gw, L, rp, rev) + [
            pl.BlockSpec((1, 1, gw, N), lambda g, c: (g, rev(c), 0, 0)),
            pl.BlockSpec((L, gw), lambda g, c: (rev(c), g))],
        out_specs=[pl.BlockSpec((L, gw), lambda g, c: (rev(c), g)),
                   pl.BlockSpec((L, N), lambda g, c: (rev(c), g)),
                   pl.BlockSpec((L, N), lambda g, c: (rev(c), g)),
                   pl.BlockSpec((1, rp, L), lambda g, c: (g, 0, rev(c))),
                   pl.BlockSpec((1, rp, L), lambda g, c: (g, 0, rev(c)))],
        out_shape=[jax.ShapeDtypeStruct((S, d_inner), F32), jax.ShapeDtypeStruct((S, gn), F32),
                   jax.ShapeDtypeStruct((S, gn), F32), jax.ShapeDtypeStruct((n_groups, rp, S), F32),
                   jax.ShapeDtypeStruct((n_groups, rp, S), F32)],
        scratch_shapes=[pltpu.VMEM((gw, N), F32), pltpu.VMEM((gw, L), F32)],
        compiler_params=_params(("parallel", "arbitrary")),
    )(xbc_a, xbc_a, xbc_a, dt_row, da_col, da_row, states, dy)


def _attn_scale():
    return float(QK_NOPE + QK_ROPE) ** -0.5


def _diag_mask(t, keys_first=False):
    rows = lax.broadcasted_iota(jnp.int32, (t, t), 0)
    cols = lax.broadcasted_iota(jnp.int32, (t, t), 1)
    return rows <= cols if keys_first else cols <= rows


def _tile_rows(ref, j, t):
    return ref[pl.ds(pl.multiple_of(j * t, t), t), :]


def _walk_wide(lo, hi, tile_step, joint=True):
    n = hi - lo

    def step(j, width):
        if joint:
            tile_step(j, width)
        else:
            for u in range(width):
                tile_step(j + u, 1)

    def quad(t, carry):
        step(lo + 4 * t, 4)
        return carry

    lax.fori_loop(0, n // 4, quad, 0)

    @pl.when(n % 4 >= 2)
    def _():
        step(hi - n % 4, 2)

    @pl.when(n % 2 == 1)
    def _():
        step(hi - 1, 1)


def _carried(main_body, n_in, n_out, n_scratch, carry, grid):
    if carry is None:
        return main_body
    n_ci, n_co = len(carry["ins"]), len(carry["outs"])

    def body(*refs):
        pos = [0]

        def take(n):
            pos[0] += n
            return refs[pos[0] - n: pos[0]]

        ins, c_ins, outs, c_outs, scratch, sems = take(n_in), take(n_ci), take(n_out), take(n_co), take(n_scratch), take(3)
        steps = [pl.program_id(a) for a in range(len(grid))]

        @pl.when(functools.reduce(jnp.logical_and, [s == 0 for s in steps]))
        def _():
            for cp in carry["copies"](c_ins, c_outs, *sems):
                cp.start()

        main_body(*ins, *outs, *scratch)

        @pl.when(functools.reduce(jnp.logical_and, [s == n - 1 for s, n in zip(steps, grid)]))
        def _():
            for cp in carry["copies"](c_ins, c_outs, *sems):
                cp.wait()

    return body


def _carry_call(carry):
    if carry is None:
        return [], [], [], [], []
    sems = [pltpu.SemaphoreType.DMA((carry["n_remote"],)), pltpu.SemaphoreType.DMA((carry["n_remote"],)),
            pltpu.SemaphoreType.DMA((max(carry["n_local"], 1),))]
    return (list(carry["ins"]), [_HBM] * len(carry["ins"]), [_HBM] * len(carry["outs"]), list(carry["outs"]), sems)


def _flash_fwd(q, k, v_t, n_heads, carry=None):
    S = q.shape[0]
    T = min(FLASH_T, S)
    grid = (n_heads, S // T)
    c_args, c_in_specs, c_out_specs, c_out_shapes, c_sems = _carry_call(carry)

    def body(q_ref, k_ref, vt_ref, o_ref, lse_ref, m_s, l_s, acc_t):
        i = pl.program_id(1)
        m_s[...] = jnp.full_like(m_s, -jnp.inf)
        l_s[...] = jnp.zeros_like(l_s)
        acc_t[...] = jnp.zeros_like(acc_t)
        qv = q_ref[...]

        def step(j, width, masked):
            keys = pl.ds(pl.multiple_of(j * T, T), width * T)
            s_t = _dot_nt(k_ref[keys, :], qv)
            if masked:
                s_t = jnp.where(_diag_mask(T, keys_first=True), s_t, -jnp.inf)
            m_prev = m_s[...]
            m_new = jnp.maximum(m_prev, jnp.max(s_t, axis=0, keepdims=True))
            alpha = jnp.exp(m_prev - m_new)
            p_t = jnp.exp(s_t - m_new)
            l_s[...] = alpha * l_s[...] + jnp.sum(p_t, axis=0, keepdims=True)
            acc_t[...] = alpha * acc_t[...] + _dot(vt_ref[:, keys], p_t.astype(BF16))
            m_s[...] = m_new

        _walk_wide(0, i, lambda j, width: step(j, width, False))
        step(i, 1, True)
        o_ref[...] = (acc_t[...] / l_s[...]).T
        lse_ref[0] = m_s[...] + jnp.log(l_s[...])

    res = pl.pallas_call(
        _carried(body, 3, 2, 3, carry, grid), name="flash_fwd", grid=grid,
        in_specs=[pl.BlockSpec((T, QK_PAD), lambda h, i: (i, h)),
                  pl.BlockSpec((S, QK_PAD), lambda h, i: (0, h)),
                  pl.BlockSpec((V_DIM, S), lambda h, i: (h, 0))] + c_in_specs,
        out_specs=[pl.BlockSpec((T, V_DIM), lambda h, i: (i, h)),
                   pl.BlockSpec((1, 1, T), lambda h, i: (h, 0, i))] + c_out_specs,
        out_shape=[jax.ShapeDtypeStruct((S, n_heads * V_DIM), F32),
                   jax.ShapeDtypeStruct((n_heads, 1, S), F32)] + c_out_shapes,
        scratch_shapes=[pltpu.VMEM((1, T), F32), pltpu.VMEM((1, T), F32), pltpu.VMEM((V_DIM, T), F32)] + c_sems,
        compiler_params=_params(("arbitrary", "arbitrary")),
    )(q, k, v_t, *c_args)
    return res[0], res[1], list(res[2:])


def _flash_bwd_dq(q, k, v, do, lse, delta, n_heads):
    S = q.shape[0]
    T = min(FLASH_T, S)

    def body(q_ref, k_ref, v_ref, do_ref, lse_ref, dl_ref, dq_ref, acc):
        i = pl.program_id(1)
        acc[...] = jnp.zeros_like(acc)
        qv = q_ref[...]
        dov = do_ref[...]
        lse_c = lse_ref[0]
        dl_c = dl_ref[0]

        def step(j, width, masked):
            keys = pl.ds(pl.multiple_of(j * T, T), width * T)
            kt = k_ref[keys, :]
            s = _dot_nt(qv, kt)
            if masked:
                s = jnp.where(_diag_mask(T), s, -jnp.inf)
            p = jnp.exp(s - lse_c)
            ds = p * (_dot_nt(dov, v_ref[keys, :]) - dl_c)
            acc[...] += _dot(ds.astype(BF16), kt)

        _walk_wide(0, i, lambda j, width: step(j, width, False), joint=False)
        step(i, 1, True)
        dq_ref[...] = acc[...] * _attn_scale()

    return pl.pallas_call(
        body, name="flash_bwd_dq", grid=(n_heads, S // T),
        in_specs=[pl.BlockSpec((T, QK_PAD), lambda h, i: (i, h)),
                  pl.BlockSpec((S, QK_PAD), lambda h, i: (0, h)),
                  pl.BlockSpec((S, V_DIM), lambda h, i: (0, h)),
                  pl.BlockSpec((T, V_DIM), lambda h, i: (i, h)),
                  pl.BlockSpec((1, T, 1), lambda h, i: (h, i, 0)),
                  pl.BlockSpec((1, T, 1), lambda h, i: (h, i, 0))],
        out_specs=pl.BlockSpec((T, QK_PAD), lambda h, i: (i, h)),
        out_shape=jax.ShapeDtypeStruct((S, n_heads * QK_PAD), F32),
        scratch_shapes=[pltpu.VMEM((T, QK_PAD), F32)],
        compiler_params=_params(("parallel", "arbitrary")),
    )(q, k, v, do, lse, delta)


def _flash_bwd_dkv(q, k, v, do, lse_row, delta_row, n_heads, carry=None):
    S = q.shape[0]
    T = min(FLASH_T, S)
    nq = S // T
    grid = (n_heads, S // T)
    c_args, c_in_specs, c_out_specs, c_out_shapes, c_sems = _carry_call(carry)

    def body(q_ref, k_ref, v_ref, do_ref, lse_ref, dl_ref, dk_ref, dv_ref, dk_acc, dv_acc):
        j = pl.program_id(1)
        dk_acc[...] = jnp.zeros_like(dk_acc)
        dv_acc[...] = jnp.zeros_like(dv_acc)
        kv = k_ref[...]
        vv = v_ref[...]

        def step(i, width, masked):
            cols = pl.ds(pl.multiple_of(i * T, T), width * T)
            qt = q_ref[cols, :]
            dot = do_ref[cols, :]
            s_t = _dot_nt(kv, qt)
            if masked:
                s_t = jnp.where(_diag_mask(T, keys_first=True), s_t, -jnp.inf)
            p_t = jnp.exp(s_t - lse_ref[0, :, cols])
            dv_acc[...] += _dot(p_t.astype(BF16), dot)
            ds_t = p_t * (_dot_nt(vv, dot) - dl_ref[0, :, cols])
            dk_acc[...] += _dot(ds_t.astype(BF16), qt)

        step(j, 1, True)
        _walk_wide(j + 1, nq, lambda i, width: step(i, width, False), joint=False)
        dk_ref[...] = dk_acc[...]
        dv_ref[...] = dv_acc[...].astype(dv_ref.dtype)

    res = pl.pallas_call(
        _carried(body, 6, 2, 2, carry, grid), name="flash_bwd_dkv", grid=grid,
        in_specs=[pl.BlockSpec((S, QK_PAD), lambda h, j: (0, h)),
                  pl.BlockSpec((T, QK_PAD), lambda h, j: (j, h)),
                  pl.BlockSpec((T, V_DIM), lambda h, j: (j, h)),
                  pl.BlockSpec((S, V_DIM), lambda h, j: (0, h)),
                  pl.BlockSpec((1, 1, S), lambda h, j: (h, 0, 0)),
                  pl.BlockSpec((1, 1, S), lambda h, j: (h, 0, 0))] + c_in_specs,
        out_specs=[pl.BlockSpec((T, QK_PAD), lambda h, j: (j, h)),
                   pl.BlockSpec((T, V_DIM), lambda h, j: (j, h))] + c_out_specs,
        out_shape=[jax.ShapeDtypeStruct((S, n_heads * QK_PAD), F32),
                   jax.ShapeDtypeStruct((S, n_heads * V_DIM), BF16)] + c_out_shapes,
        scratch_shapes=[pltpu.VMEM((T, QK_PAD), F32), pltpu.VMEM((T, V_DIM), F32)] + c_sems,
        compiler_params=_params(("arbitrary", "arbitrary")),
    )(q, k, v, do, lse_row, delta_row, *c_args)
    return res[0], res[1], list(res[2:])


def _st_delta(n_heads):
    def fn(do, o):
        prod = do * o
        lane = lax.broadcasted_iota(jnp.int32, (do.shape[0], LANES), 1)
        out = jnp.zeros((do.shape[0], LANES), F32)
        for h in range(n_heads):
            out = out + jnp.where(lane == h, jnp.sum(prod[:, h * V_DIM:(h + 1) * V_DIM], axis=1, keepdims=True), 0.0)
        return out
    return fn


def _pad_cols(w, width):
    return jnp.pad(w, ((0, 0), (0, width - w.shape[1])))


def _dims(x, p, q_norm, kv_norm, w_uq, dt_bias, ssm_norm, conv_b):
    d = dict(S=x.shape[0], D=x.shape[1], PLE=p.shape[1], DQ=q_norm.shape[1], DKV=kv_norm.shape[1],
             NH=w_uq.shape[1] // (QK_NOPE + QK_ROPE), NHS=dt_bias.shape[1], DI=ssm_norm.shape[1],
             CONV=conv_b.shape[1])
    d["G"] = (d["CONV"] - d["DI"]) // (2 * D_STATE)
    d["R"] = d["NHS"] // d["G"]
    return d


def _assemble(name, blocks):
    rows, cols = blocks.shape[1:]
    if name in COL_SHARDED:
        return blocks.transpose(1, 0, 2).reshape(rows, N_DEV * cols)
    return blocks.reshape(N_DEV * rows, cols)


def _by_device(name, g):
    if name in COL_SHARDED:
        return g.reshape(g.shape[0], N_DEV, g.shape[1] // N_DEV).transpose(1, 0, 2)
    return g.reshape(N_DEV, g.shape[0] // N_DEV, g.shape[1])


def _local_step(x, p, positions, target, small, conv_w, wts, late_names=(), late_shards=(), exchange=True):
    wts = dict(wts)
    dm = _dims(x, p, small["q_norm"], small["kv_norm"], wts["w_uq"], small["dt_bias"], small["ssm_norm"],
               small["conv_b"])
    S, D, DQ, DKV, NH, NHS, DI, CONV, G, R = (dm[k] for k in ("S", "D", "DQ", "DKV", "NH", "NHS", "DI", "CONV", "G", "R"))
    rp = -(-R // SUBLANES) * SUBLANES
    L = min(CHUNK, S)

    w_in = wts["w_in"]
    o = [0]
    for n in (DQ, DKV, QK_ROPE, DI, CONV, NHS, D, D):
        o.append(o[-1] + n)
    w_cqkv, w_kr, w_z, w_xbc, w_dt, w_g = (w_in[:, o[0]:o[2]], w_in[:, o[2]:o[3]], w_in[:, o[3]:o[4]],
                                           w_in[:, o[4]:o[5]], w_in[:, o[5]:o[6]], w_in[:, o[6]:o[8]])
    zc = lambda n: jnp.zeros((D, n), BF16)
    w_sm = jnp.concatenate([zc(QK_NOPE), w_kr, zc(QK_PAD - QK_NOPE - QK_ROPE), w_dt, zc(LANES - NHS)], axis=1)
    w_q = jnp.pad(wts["w_uq"].reshape(DQ, NH, QK_NOPE + QK_ROPE),
                  ((0, 0), (0, 0), (0, QK_PAD - QK_NOPE - QK_ROPE))).reshape(DQ, NH * QK_PAD)
    ukv = wts["w_ukv"].reshape(DKV, NH, QK_NOPE + V_DIM)
    w_k = jnp.pad(ukv[:, :, :QK_NOPE], ((0, 0), (0, 0), (0, QK_PAD - QK_NOPE))).reshape(DKV, NH * QK_PAD)
    w_v = ukv[:, :, QK_NOPE:].reshape(DKV, NH * V_DIM)
    dt_bias_p, a_log_p = _pad_cols(small["dt_bias"], LANES), _pad_cols(small["a_log"], LANES)
    dskip_rep = jnp.repeat(small["d_skip"], SSM_HEADDIM, axis=1)
    c_tab, a_tab, b_tab = _rope_tables(positions)

    (u,) = _rowwise("pre_norm", _st_pre, [x], [small["mix_norm_pre"]], [(D, BF16)])
    cqkv = _mm("in_cqkv", u, w_cqkv, "nn")
    z = _mm("in_z", u, w_z, "nn")
    xbc = _mm("in_xbc", u, w_xbc, "nn")
    g = _mm("in_gates", u, w_g, "nn")
    sm = _mm("in_small", u, w_sm, "nn")

    lora_fn = _st_lora_norms(DQ)
    cq_n, ckv_n = _rowwise("lora_norms", lora_fn, [cqkv], [small["q_norm"], small["kv_norm"]], [(DQ, BF16), (DKV, BF16)])
    qraw = _mm("up_q", cq_n, w_q, "nn")
    kraw = _mm("up_k", ckv_n, w_k, "nn")
    v = _mm("up_v", ckv_n, w_v, "nn", out_dtype=BF16)
    v_t = _mm("up_v_t", w_v.T, ckv_n, "nt", out_dtype=BF16)
    q, k = _rowwise("rope", _st_rope(NH), [qraw, kraw, sm, c_tab, a_tab, b_tab], [],
                    [(NH * QK_PAD, BF16), (NH * QK_PAD, BF16)])
    attn, lse, late_blocks = _flash_fwd(q, k, v_t, NH, carry=_gather_carry(list(late_shards)) if late_names else None)
    wts.update({n: _assemble(n, b) for n, b in zip(late_names, late_blocks)})

    xbc_c, xbc_a = _conv_fwd(xbc, conv_w, small["conv_b"])
    dt, da = _rowwise("dt", _st_dt, [sm], [dt_bias_p, a_log_p], [(LANES, F32), (LANES, F32)])

    def col_layout(t):
        return _pad_cols(t[:, :NHS].reshape(S, G, R).transpose(1, 0, 2).reshape(G * S, R), LANES).reshape(G, S, LANES)

    def row_layout(t):
        return jnp.pad(t[:, :NHS].reshape(S, G, R).transpose(1, 2, 0), ((0, 0), (0, rp - R), (0, 0)))

    dt_row, da_col, da_row = row_layout(dt), col_layout(da), row_layout(da)
    xs = xbc_a[:, :DI]
    y, states = _ssd_fwd_t(xbc_a, dt_row, da_col, da_row, DI, G, R)
    gn_fn = _st_gated_norm(G)
    (ssm,) = _rowwise("gated_norm", gn_fn, [y, xs, z], [dskip_rep, small["ssm_norm"]], [(DI, BF16)])

    ao = _mm("attn_o", attn, wts["w_attn_o"], "nn")
    so = _mm("ssm_o", ssm, wts["w_ssm_o"], "nn")
    mix_fn = _st_mix(D)
    (mixed,) = _rowwise("mix", mix_fn, [g, ao, so], [], [(D, BF16)])
    mo = _mm("out_proj", mixed, wts["w_out"], "nn")
    h1, f = _rowwise("res1", _st_res_norm, [x, mo], [small["mix_norm_post"], small["ffn_norm_pre"]], [(D, F32), (D, BF16)])
    gt = _mm("ffn_gate", f, wts["w_gate"], "nn")
    up = _mm("ffn_up", f, wts["w_up"], "nn")
    (act,) = _rowwise("swiglu", _st_swiglu, [gt, up], [], [(gt.shape[1], BF16)])
    dn = _mm("ffn_down", act, wts["w_down"], "nn")
    h2, a3 = _rowwise("res2", _st_res_norm, [h1, dn], [small["ffn_norm_post"], small["ple_norm_pre"]], [(D, F32), (D, BF16)])
    gl = _mm("ple_gate", a3, wts["w_ple_gate"], "nn")
    pe = _mm("ple_proj", p, wts["w_ple"], "nn")

    sg = {}
    bg = {}
    dpe, dgl, dh2, d_w, loss_acc = _rowwise_bwd(
        "loss", _st_loss, [pe, gl, h2, target], [small["ple_norm_post"]], [1.0], [0, 1, 2], [BF16, BF16, F32], fwd_sums=(0,))
    sg["ple_norm_post"] = _fold(d_w)
    loss = loss_acc[0:1, :]
    bg["w_ple"] = _mm("d_w_ple", p, dpe, "tn", out_dtype=BF16)
    bg["w_ple_gate"] = _mm("d_w_ple_gate", a3, dgl, "tn", out_dtype=BF16)
    da3 = _mm("d_a3", dgl, wts["w_ple_gate"], "nt")

    dh1, ddn, d_post, d_pre = _rowwise_bwd(
        "res2_bwd", _st_res_norm, [h1, dn], [small["ffn_norm_post"], small["ple_norm_pre"]], [dh2, da3], [0, 1], [F32, BF16])
    sg["ffn_norm_post"], sg["ple_norm_pre"] = _fold(d_post), _fold(d_pre)
    bg["w_down"] = _mm("d_w_down", act, ddn, "tn", out_dtype=BF16)
    dact = _mm("d_act", ddn, wts["w_down"], "nt")
    dgt, dup = _rowwise_bwd("swiglu_bwd", _st_swiglu, [gt, up], [], [dact], [0, 1], [BF16, BF16])
    bg["w_gate"] = _mm("d_w_gate", f, dgt, "tn", out_dtype=BF16)
    bg["w_up"] = _mm("d_w_up", f, dup, "tn", out_dtype=BF16)
    df = _mm("d_f_gate", dgt, wts["w_gate"], "nt")
    df = _mm("d_f_up", dup, wts["w_up"], "nt", acc_in=df)

    dx_res, dmo, d_post, d_pre = _rowwise_bwd(
        "res1_bwd", _st_res_norm, [x, mo], [small["mix_norm_post"], small["ffn_norm_pre"]], [dh1, df], [0, 1], [F32, BF16])
    sg["mix_norm_post"], sg["ffn_norm_pre"] = _fold(d_post), _fold(d_pre)
    bg["w_out"] = _mm("d_w_out", mixed, dmo, "tn", out_dtype=BF16)
    dmixed = _mm("d_mixed", dmo, wts["w_out"], "nt")
    pw = (DI, 2 * D, CONV, DQ + DKV, QK_ROPE + NHS)
    po = [sum(pw[:k]) for k in range(len(pw))]
    assert po[1] % pw[1] == 0 and po[3] % pw[3] == 0, (po, pw)
    dproj = lax.empty((S, sum(pw)), BF16)
    dproj, dao, dso = _rowwise_bwd("mix_bwd", mix_fn, [g, ao, so], [], [dmixed], [0, 1, 2], [BF16, BF16, BF16],
                                   into={0: (dproj, po[1] // pw[1])})
    bg["w_attn_o"] = _mm("d_w_attn_o", attn, dao, "tn", out_dtype=BF16)
    bg["w_ssm_o"] = _mm("d_w_ssm_o", ssm, dso, "tn", out_dtype=BF16)
    dattn = _mm("d_attn", dao, wts["w_attn_o"], "nt", out_dtype=BF16)
    dssm = _mm("d_ssm", dso, wts["w_ssm_o"], "nt")

    dy, dxs_a, dproj, d_dskip, d_ssmn = _rowwise_bwd(
        "gated_norm_bwd", gn_fn, [y, xs, z], [dskip_rep, small["ssm_norm"]], [dssm], [0, 1, 2], [F32, F32, BF16],
        into={2: (dproj, 0)})
    sg["d_skip"] = _fold(d_dskip).reshape(NHS, SSM_HEADDIM).sum(axis=1).reshape(1, NHS)
    sg["ssm_norm"] = _fold(d_ssmn)
    dxs_b, d_b, d_c, ddt_row, dcum_row = _ssd_bwd_t(xbc_a, dt_row, da_col, da_row, states, dy, DI, G, R)

    def from_row(t):
        return _pad_cols(t[:, :R, :].transpose(2, 0, 1).reshape(S, NHS), LANES)

    ddtraw, d_bias, d_alog = _rowwise("dt_bwd", _st_dt_bwd, [sm, from_row(ddt_row), from_row(dcum_row)],
                                      [dt_bias_p, a_log_p], [(LANES, F32)], accs=(LANES, LANES), tr=L)
    sg["dt_bias"], sg["a_log"] = _fold(d_bias)[:, :NHS], _fold(d_alog)[:, :NHS]
    dconv, d_cb = _rowwise("dconv", _st_dconv(DI), [xbc_c, dxs_a, dxs_b, d_b, d_c], [], [(CONV, F32)], accs=(CONV,))
    sg["conv_b"] = _fold(d_cb)
    dproj, d_cw = _conv_bwd(xbc, dconv, conv_w, dproj, po[2])
    d_conv_w = d_cw.reshape(CONV_WIDTH, SUBLANES, CONV).sum(axis=1)

    (delta,) = _rowwise("attn_delta", _st_delta(NH), [dattn, attn], [], [(LANES, F32)])
    delta = delta[:, :NH].T
    dq = _flash_bwd_dq(q, k, v, dattn, lse.reshape(NH, S, 1), delta.reshape(NH, S, 1), NH)
    late_sent = [_by_device(n, bg.pop(n)) for n in late_names]
    dk, dv, late_recv = _flash_bwd_dkv(q, k, v, dattn, lse, delta.reshape(NH, 1, S), NH,
                                       carry=_scatter_carry(late_sent) if late_names else None)
    dqraw, dkr, dk = _rowwise("rope_bwd", _st_rope_bwd(NH), [dq, dk, c_tab, a_tab, b_tab], [],
                              [(NH * QK_PAD, BF16), (QK_PAD, F32), (NH * QK_PAD, BF16)])
    d_w_q = _mm("d_w_q", cq_n, dqraw, "tn", out_dtype=BF16)
    d_w_k = _mm("d_w_k", ckv_n, dk, "tn", out_dtype=BF16)
    d_w_v = _mm("d_w_v", ckv_n, dv, "tn", out_dtype=BF16)
    dcq_n = _mm("d_cq_n", dqraw, w_q, "nt")
    dckv_n = _mm("d_ckv_n_k", dk, w_k, "nt")
    dckv_n = _mm("d_ckv_n_v", dv, w_v, "nt", acc_in=dckv_n)
    bg["w_uq"] = d_w_q.reshape(DQ, NH, QK_PAD)[:, :, :QK_NOPE + QK_ROPE].reshape(DQ, NH * (QK_NOPE + QK_ROPE))
    bg["w_ukv"] = jnp.concatenate([d_w_k.reshape(DKV, NH, QK_PAD)[:, :, :QK_NOPE], d_w_v.reshape(DKV, NH, V_DIM)],
                                  axis=2).reshape(DKV, NH * (QK_NOPE + V_DIM))
    dproj, d_qn, d_kvn = _rowwise_bwd("lora_norms_bwd", lora_fn, [cqkv], [small["q_norm"], small["kv_norm"]],
                                      [dcq_n, dckv_n], [0], [BF16], into={0: (dproj, po[3] // pw[3])})
    sg["q_norm"], sg["kv_norm"] = _fold(d_qn), _fold(d_kvn)

    d_small = jnp.concatenate([dkr[:, QK_NOPE:QK_NOPE + QK_ROPE], ddtraw[:, :NHS]], axis=1).astype(BF16)
    dproj = lax.dynamic_update_slice(dproj, d_small, (0, po[4]))
    w_in = jnp.concatenate([w_z, w_g, w_xbc, w_cqkv, w_kr, w_dt], axis=1)
    d_w = _mm("d_w_in", u, dproj, "tn", out_dtype=BF16)
    kr = po[4] + QK_ROPE
    bg["w_in"] = jnp.concatenate([d_w[:, po[3]:po[4]], d_w[:, po[4]:kr], d_w[:, po[0]:po[1]], d_w[:, po[2]:po[3]],
                                  d_w[:, kr:kr + NHS], d_w[:, po[1]:po[2]]], axis=1)
    rest_names = tuple(n for n in BIG if n in bg)
    if not exchange:
        du = _mm("d_u", dproj, w_in, "nt")
        grad_x, d_pre = _rowwise_bwd("pre_norm_bwd", _st_pre, [x], [small["mix_norm_pre"]], [du], [0], [F32],
                                     adds={0: dx_res})
        sg["mix_norm_pre"] = _fold(d_pre)
        return loss, grad_x, sg, d_conv_w, bg, {}
    rest_sent = [_by_device(n, bg.pop(n)) for n in rest_names]
    du, rest_recv = _mm("d_u", dproj, w_in, "nt", carry=_scatter_carry(rest_sent))
    grad_x, d_pre = _rowwise_bwd("pre_norm_bwd", _st_pre, [x], [small["mix_norm_pre"]], [du], [0], [F32], adds={0: dx_res})
    sg["mix_norm_pre"] = _fold(d_pre)
    sent = dict(zip(late_names, late_sent), **dict(zip(rest_names, rest_sent)))
    recv = dict(zip(late_names, late_recv), **dict(zip(rest_names, rest_recv)))
    return loss, grad_x, sg, d_conv_w, sent, recv


_HBM = pl.BlockSpec(memory_space=pltpu.HBM)
_FLIPS = ((0, 0, 1), (1, 0, 0), (0, 1, 0), (1, 1, 0), (1, 0, 1), (0, 1, 1), (1, 1, 1))


def _place():
    return lax.axis_index("x"), lax.axis_index("y"), lax.axis_index("c")


def _flipped(place, flip):
    return tuple(1 - v if f else v for v, f in zip(place, flip))


def _gather_carry(blocks):
    nw = len(blocks)

    def copies(ins, outs, send_sems, recv_sems, local_sems):
        x, y, c = _place()
        me = 4 * x + 2 * y + c
        cps = []
        for w in range(nw):
            cps.append(pltpu.make_async_copy(ins[w], outs[w].at[me], local_sems.at[w]))
            for k, flip in enumerate(_FLIPS):
                cps.append(pltpu.make_async_remote_copy(
                    src_ref=ins[w], dst_ref=outs[w].at[me], send_sem=send_sems.at[7 * w + k],
                    recv_sem=recv_sems.at[7 * w + k], device_id=_flipped((x, y, c), flip), device_id_type=MESH))
        return cps

    return dict(ins=blocks, outs=[jax.ShapeDtypeStruct((N_DEV,) + b.shape, b.dtype) for b in blocks],
                n_remote=7 * nw, n_local=nw, copies=copies)


def _scatter_carry(by_dev):
    nw = len(by_dev)

    def copies(ins, outs, send_sems, recv_sems, local_sems):
        x, y, c = _place()
        cps = []
        for w in range(nw):
            for k, flip in enumerate(_FLIPS):
                px, py, pc = _flipped((x, y, c), flip)
                cps.append(pltpu.make_async_remote_copy(
                    src_ref=ins[w].at[4 * px + 2 * py + pc], dst_ref=outs[w].at[k], send_sem=send_sems.at[7 * w + k],
                    recv_sem=recv_sems.at[7 * w + k], device_id=(px, py, pc), device_id_type=MESH))
        return cps

    return dict(ins=by_dev, outs=[jax.ShapeDtypeStruct((7,) + b.shape[1:], b.dtype) for b in by_dev],
                n_remote=7 * nw, n_local=0, copies=copies)


def _all_gather(name, blocks):
    nw = len(blocks)

    def body(*refs):
        x_refs, out_refs = refs[:nw], refs[nw:2 * nw]
        send_sems, recv_sems, local_sems = refs[2 * nw:]
        x, y, c = _place()
        me, sibling = (x, y, c), (x, y, 1 - c)
        chips = [(1 - x, y), (x, 1 - y), (1 - x, 1 - y)]

        def slot(w, px, py, pc):
            return out_refs[w].at[4 * px + 2 * py + pc]

        def copy(w, k, blk, to, src=None):
            return pltpu.make_async_remote_copy(
                src_ref=slot(w, *blk) if src is None else src, dst_ref=slot(w, *blk),
                send_sem=send_sems.at[7 * w + k], recv_sem=recv_sems.at[7 * w + k], device_id=to, device_id_type=MESH)

        mine = [pltpu.make_async_copy(x_refs[w], slot(w, *me), local_sems.at[w]) for w in range(nw)]
        for cp in mine:
            cp.start()
        first = []
        for w in range(nw):
            first.append(copy(w, 0, me, sibling, src=x_refs[w]))
            first += [copy(w, 1 + j, me, (*chip, c), src=x_refs[w]) for j, chip in enumerate(chips)]
        for cp in first:
            cp.start()
        passed = []
        for j, chip in enumerate(chips):
            for w in range(nw):
                copy(w, 1 + j, (*chip, c), me).wait_recv()
                passed.append(copy(w, 4 + j, (*chip, c), sibling))
                passed[-1].start()
        for w in range(nw):
            copy(w, 0, sibling, me).wait_recv()
        for j, chip in enumerate(chips):
            for w in range(nw):
                copy(w, 4 + j, (*chip, 1 - c), me).wait_recv()
        for cp in first + passed:
            cp.wait_send()
        for cp in mine:
            cp.wait()

    return pl.pallas_call(
        body, name=name, out_shape=[jax.ShapeDtypeStruct((N_DEV,) + b.shape, b.dtype) for b in blocks],
        in_specs=[_HBM] * nw, out_specs=[_HBM] * nw,
        scratch_shapes=[pltpu.SemaphoreType.DMA((7 * nw,)), pltpu.SemaphoreType.DMA((7 * nw,)),
                        pltpu.SemaphoreType.DMA((nw,))],
    )(*blocks)


def _swap_with_sibling(name, by_dev):
    nw = len(by_dev)

    def body(*refs):
        g_refs, got_refs = refs[:nw], refs[nw:2 * nw]
        send_sems, recv_sems = refs[2 * nw:]
        x, y, c = _place()
        cps = [pltpu.make_async_remote_copy(src_ref=g_refs[w].at[2 * q + (1 - c)], dst_ref=got_refs[w].at[q],
                                            send_sem=send_sems.at[4 * w + q], recv_sem=recv_sems.at[4 * w + q],
                                            device_id=(x, y, 1 - c), device_id_type=MESH)
               for w in range(nw) for q in range(4)]
        for cp in cps:
            cp.start()
        for cp in cps:
            cp.wait()

    return pl.pallas_call(
        body, name=name, out_shape=[jax.ShapeDtypeStruct((4,) + b.shape[1:], b.dtype) for b in by_dev],
        in_specs=[_HBM] * nw, out_specs=[_HBM] * nw,
        scratch_shapes=[pltpu.SemaphoreType.DMA((4 * nw,)), pltpu.SemaphoreType.DMA((4 * nw,))],
    )(*by_dev)


def _swap_with_chips(name, sends):
    nw = len(sends)

    def body(*refs):
        s_refs, r_refs = refs[:nw], refs[nw:2 * nw]
        send_sems, recv_sems = refs[2 * nw:]
        x, y, c = _place()
        chips = [(1 - x, y), (x, 1 - y), (1 - x, 1 - y)]
        cps = [pltpu.make_async_remote_copy(src_ref=s_refs[w].at[2 * px + py], dst_ref=r_refs[w].at[k],
                                            send_sem=send_sems.at[3 * w + k], recv_sem=recv_sems.at[3 * w + k],
                                            device_id=(px, py, c), device_id_type=MESH)
               for w in range(nw) for k, (px, py) in enumerate(chips)]
        for cp in cps:
            cp.start()
        for cp in cps:
            cp.wait()

    return pl.pallas_call(
        body, name=name, out_shape=[jax.ShapeDtypeStruct((3,) + s.shape[1:], s.dtype) for s in sends],
        in_specs=[_HBM] * nw, out_specs=[_HBM] * nw,
        scratch_shapes=[pltpu.SemaphoreType.DMA((3 * nw,)), pltpu.SemaphoreType.DMA((3 * nw,))],
    )(*sends)


def _lane_pad(n):
    return -(-n // LANES) * LANES


def _pack_small(vecs, mat):
    width = max(sum(_lane_pad(v.shape[1]) for v in vecs), _lane_pad(mat.shape[1]))
    row0 = jnp.concatenate([_pad_cols(v, _lane_pad(v.shape[1])) for v in vecs], axis=1)
    rows = jnp.concatenate([_pad_cols(row0, width), _pad_cols(mat, width)], axis=0)
    return jnp.pad(rows, ((0, SUBLANES - rows.shape[0]), (0, 0)))


def _unpack_small(packed, sizes, mat_cols):
    vecs, off = [], 0
    for n in sizes:
        vecs.append(packed[0:1, off:off + n])
        off += _lane_pad(n)
    return vecs, packed[1:1 + CONV_WIDTH, :mat_cols]


def _adamw(w, g, m, v):
    m = ADAM_B1 * m + (1.0 - ADAM_B1) * g
    v = ADAM_B2 * v + (1.0 - ADAM_B2) * (g * g)
    m_hat = m / (1.0 - ADAM_B1 ** ADAM_STEP)
    v_hat = v / (1.0 - ADAM_B2 ** ADAM_STEP)
    delta = -ADAM_LR * (m_hat / (jnp.sqrt(v_hat) + ADAM_EPS) + ADAM_WD * w)
    return delta, m, v


BIG = ("w_in", "w_uq", "w_ukv", "w_attn_o", "w_ssm_o", "w_out", "w_gate", "w_up", "w_down", "w_ple_gate", "w_ple")
FIRST = ("w_in", "w_uq", "w_ukv")
LATE = ("w_attn_o", "w_ssm_o", "w_out", "w_gate", "w_up", "w_down", "w_ple_gate", "w_ple")
COL_SHARDED = ("w_in", "w_uq", "w_ukv", "w_gate", "w_up", "w_ple")
SMALL = ("mix_norm_pre", "mix_norm_post", "q_norm", "kv_norm", "conv_b", "dt_bias", "a_log", "d_skip", "ssm_norm",
         "ffn_norm_pre", "ffn_norm_post", "ple_norm_pre", "ple_norm_post")
WEIGHTS = ("mix_norm_pre", "mix_norm_post", "w_in", "q_norm", "w_uq", "kv_norm", "w_ukv", "conv_w", "conv_b", "dt_bias",
           "a_log", "d_skip", "ssm_norm", "w_attn_o", "w_ssm_o", "w_out", "ffn_norm_pre", "ffn_norm_post", "w_gate",
           "w_up", "w_down", "ple_norm_pre", "ple_norm_post", "w_ple_gate", "w_ple")


def _step(x, p, positions, target, w, m, v):
    xi, yi, ci = _place()
    me = 4 * xi + 2 * yi + ci
    chip = 2 * xi + yi

    gathered = _all_gather("gather_weights", [w[n].astype(BF16) for n in FIRST])
    wts = {n: _assemble(n, blocks) for n, blocks in zip(FIRST, gathered)}
    cw_rows, cw_cols = w["conv_w"].shape
    (cw_all,) = _all_gather("gather_conv_w", [jnp.pad(w["conv_w"], ((0, SUBLANES - cw_rows), (0, 0)))])
    conv_w = cw_all[:, :cw_rows, :].transpose(1, 0, 2).reshape(cw_rows, N_DEV * cw_cols)

    small = {n: w[n] for n in SMALL}
    loss, grad_x, sg, d_conv_w, sent, recv = _local_step(
        x, p, positions, target, small, conv_w, wts, LATE, [w[n].astype(BF16) for n in LATE])

    sizes = [w[n].shape[1] for n in SMALL]
    sg_pack = _pack_small([sg[n] for n in SMALL] + [loss], d_conv_w)
    (sg_all,) = _all_gather("gather_small_grads", [sg_pack])
    (sg_sum,) = _rowwise("sum_small_grads", lambda *a: functools.reduce(lambda s, t: s + t, a),
                         [sg_all[k] for k in range(N_DEV)], [], [(sg_pack.shape[1], F32)])
    sg_vecs, d_conv_w_sum = _unpack_small(sg_sum, sizes + [LANES], d_conv_w.shape[1])
    loss = sg_vecs[-1][0, 0]
    grads = dict(zip(SMALL, sg_vecs[:-1]))
    grads["conv_w"] = lax.dynamic_slice_in_dim(d_conv_w_sum, me * cw_cols, cw_cols, axis=1)

    def sum8_then_adamw(wv, mv, vv, own, *others):
        g = functools.reduce(lambda s, t: s + t, others, own)
        return (g,) + _adamw(wv, g, mv, vv)

    delta, new_m, new_v = {}, {}, {}
    for n in BIG:
        cols = w[n].shape[1]
        own = lax.dynamic_index_in_dim(sent[n], me, axis=0, keepdims=False)
        grads[n], delta[n], new_m[n], new_v[n] = _rowwise(
            "adamw_" + n, sum8_then_adamw, [w[n], m[n], v[n], own] + [recv[n][k] for k in range(N_DEV - 1)], [],
            [(cols, F32)] * 4)
    packed = [_pack_small([d[n] for n in SMALL], d["conv_w"]) for d in (w, grads, m, v)]
    outs = _rowwise("adamw_small", _adamw, packed, [], [(packed[0].shape[1], F32)] * 3)
    for d, o in zip((delta, new_m, new_v), outs):
        vecs, mat = _unpack_small(o, sizes, cw_cols)
        d.update(zip(SMALL, vecs))
        d["conv_w"] = mat
    return loss, grad_x, grads, delta, new_m, new_v


def kernel(x, p, positions, mix_norm_pre, mix_norm_post, w_in, q_norm, w_uq, kv_norm, w_ukv, conv_w, conv_b, dt_bias, a_log, d_skip, ssm_norm, w_attn_o, w_ssm_o, w_out, ffn_norm_pre, ffn_norm_post, w_gate, w_up, w_down, ple_norm_pre, ple_norm_post, w_ple_gate, w_ple, loss_target, m_mix_norm_pre, m_mix_norm_post, m_w_in, m_q_norm, m_w_uq, m_kv_norm, m_w_ukv, m_conv_w, m_conv_b, m_dt_bias, m_a_log, m_d_skip, m_ssm_norm, m_w_attn_o, m_w_ssm_o, m_w_out, m_ffn_norm_pre, m_ffn_norm_post, m_w_gate, m_w_up, m_w_down, m_ple_norm_pre, m_ple_norm_post, m_w_ple_gate, m_w_ple, v_mix_norm_pre, v_mix_norm_post, v_w_in, v_q_norm, v_w_uq, v_kv_norm, v_w_ukv, v_conv_w, v_conv_b, v_dt_bias, v_a_log, v_d_skip, v_ssm_norm, v_w_attn_o, v_w_ssm_o, v_w_out, v_ffn_norm_pre, v_ffn_norm_post, v_w_gate, v_w_up, v_w_down, v_ple_norm_pre, v_ple_norm_post, v_w_ple_gate, v_w_ple):
    w_args = (mix_norm_pre, mix_norm_post, w_in, q_norm, w_uq, kv_norm, w_ukv, conv_w, conv_b, dt_bias, a_log, d_skip, ssm_norm, w_attn_o, w_ssm_o, w_out, ffn_norm_pre, ffn_norm_post, w_gate, w_up, w_down, ple_norm_pre, ple_norm_post, w_ple_gate, w_ple)
    m_args = (m_mix_norm_pre, m_mix_norm_post, m_w_in, m_q_norm, m_w_uq, m_kv_norm, m_w_ukv, m_conv_w, m_conv_b, m_dt_bias, m_a_log, m_d_skip, m_ssm_norm, m_w_attn_o, m_w_ssm_o, m_w_out, m_ffn_norm_pre, m_ffn_norm_post, m_w_gate, m_w_up, m_w_down, m_ple_norm_pre, m_ple_norm_post, m_w_ple_gate, m_w_ple)
    v_args = (v_mix_norm_pre, v_mix_norm_post, v_w_in, v_q_norm, v_w_uq, v_kv_norm, v_w_ukv, v_conv_w, v_conv_b, v_dt_bias, v_a_log, v_d_skip, v_ssm_norm, v_w_attn_o, v_w_ssm_o, v_w_out, v_ffn_norm_pre, v_ffn_norm_post, v_w_gate, v_w_up, v_w_down, v_ple_norm_pre, v_ple_norm_post, v_w_ple_gate, v_w_ple)

    def drop_layer(a):
        return a if a.ndim == 2 else a[0]

    w = {n: drop_layer(a) for n, a in zip(WEIGHTS, w_args)}
    m = {n: drop_layer(a) for n, a in zip(WEIGHTS, m_args)}
    v = {n: drop_layer(a) for n, a in zip(WEIGHTS, v_args)}
    loss, grad_x, grads, delta, new_m, new_v = _step(x[0], p[0, 0], positions[0], loss_target[0], w, m, v)
    like = lambda d: [d[n].reshape(a.shape) for n, a in zip(WEIGHTS, w_args)]
    return (loss, grad_x[None], *like(grads), *like(delta), *like(new_m), *like(new_v))
```

```python
import functools

import jax
import jax.numpy as jnp
from jax import lax
from jax.experimental import pallas as pl
from jax.experimental.pallas import tpu as pltpu

F32 = jnp.float32
BF16 = jnp.bfloat16

EPS = 1e-6
QK_NOPE = 128
QK_ROPE = 64
V_DIM = 128
QK_PAD = 256
ROPE_THETA = 10000.0
SSM_HEADDIM = 64
D_STATE = 128
CONV_WIDTH = 4
CHUNK = 256
ADAM_LR = 0.001
ADAM_B1 = 0.9
ADAM_B2 = 0.999
ADAM_EPS = 1e-08
ADAM_WD = 0.01
ADAM_STEP = 10

N_DEV = 8
LANES = 128
SUBLANES = 8
VMEM_LIMIT = 56 * 1024 * 1024
ROW_TILE_BYTES = 16 * 1024 * 1024
STRIP_ROWS = 16
STRIP_ELEMS = 64 * 1024
FLASH_T = 512
MM_TILE_BYTES = 20 * 1024 * 1024
MESH = pl.DeviceIdType.MESH


def _pick(dim, prefs):
    if dim <= prefs[0]:
        return dim
    for p in prefs:
        if dim % p == 0:
            return p
    return dim


def _tile(dim, cap):
    if dim <= cap:
        return dim
    best = None
    for t in range(LANES, cap + 1, LANES):
        if dim % t == 0:
            best = t
    return best if best is not None else dim


def _params(sem):
    return pltpu.CompilerParams(dimension_semantics=sem, vmem_limit_bytes=VMEM_LIMIT)


def _dot(a, b):
    return lax.dot_general(a, b, (((1,), (0,)), ((), ())), preferred_element_type=F32)


def _dot_nt(a, b):
    return lax.dot_general(a, b, (((1,), (1,)), ((), ())), preferred_element_type=F32)


def _dot_tn(a, b):
    return lax.dot_general(a, b, (((0,), (0,)), ((), ())), preferred_element_type=F32)


def _mm(name, a, b, mode, out_dtype=F32, acc_in=None, carry=None):
    if mode == "nn":
        (M, K), (K2, N) = a.shape, b.shape
    elif mode == "nt":
        (M, K), (N, K2) = a.shape, b.shape
    else:
        (K, M), (K2, N) = a.shape, b.shape
    assert K == K2, (name, a.shape, b.shape, mode)
    tm = _tile(M, 1024)
    tn = _tile(N, 1024 if acc_in is not None else 1536)
    tk = _tile(K, 2048)
    while tk > 512 and 2 * (tm * tk * a.dtype.itemsize + tk * tn * b.dtype.itemsize) > MM_TILE_BYTES:
        tk = _tile(K, tk - LANES)
    nk = K // tk
    dot = {"nn": _dot, "nt": _dot_nt, "tn": _dot_tn}[mode]
    has_acc = acc_in is not None

    def body(*refs):
        if has_acc:
            a_ref, b_ref, c_ref, o_ref, acc = refs
        else:
            a_ref, b_ref, o_ref, acc = refs
        k = pl.program_id(2)

        @pl.when(k == 0)
        def _():
            acc[...] = jnp.zeros_like(acc)

        acc[...] += dot(a_ref[...].astype(BF16), b_ref[...].astype(BF16))

        @pl.when(k == nk - 1)
        def _():
            r = acc[...]
            if has_acc:
                r = r + c_ref[...]
            o_ref[...] = r.astype(o_ref.dtype)

    if mode == "tn":
        a_spec = pl.BlockSpec((tk, tm), lambda i, j, k: (k, i))
    else:
        a_spec = pl.BlockSpec((tm, tk), lambda i, j, k: (i, k))
    if mode == "nt":
        b_spec = pl.BlockSpec((tn, tk), lambda i, j, k: (j, k))
    else:
        b_spec = pl.BlockSpec((tk, tn), lambda i, j, k: (k, j))
    o_spec = pl.BlockSpec((tm, tn), lambda i, j, k: (i, j))
    in_specs = [a_spec, b_spec] + ([o_spec] if has_acc else [])
    args = (a, b) + ((acc_in,) if has_acc else ())
    if carry is None:
        return pl.pallas_call(
            body, name=name, grid=(M // tm, N // tn, nk), in_specs=in_specs, out_specs=o_spec,
            out_shape=jax.ShapeDtypeStruct((M, N), out_dtype), scratch_shapes=[pltpu.VMEM((tm, tn), F32)],
            input_output_aliases=({2: 0} if has_acc and out_dtype == F32 else {}),
            compiler_params=_params(("parallel", "parallel", "arbitrary")),
        )(*args)
    grid = (M // tm, N // tn, nk)
    c_args, c_in_specs, c_out_specs, c_out_shapes, c_sems = _carry_call(carry)
    res = pl.pallas_call(
        _carried(body, len(args), 1, 1, carry, grid), name=name, grid=grid, in_specs=in_specs + c_in_specs,
        out_specs=[o_spec] + c_out_specs, out_shape=[jax.ShapeDtypeStruct((M, N), out_dtype)] + c_out_shapes,
        scratch_shapes=[pltpu.VMEM((tm, tn), F32)] + c_sems,
        compiler_params=_params(("arbitrary", "arbitrary", "arbitrary")),
    )(*args, *c_args)
    return res[0], list(res[1:])


def _row_tile(n_rows, bytes_per_row):
    tr = 1024
    while tr > SUBLANES and tr * bytes_per_row > ROW_TILE_BYTES:
        tr //= 2
    while n_rows % tr:
        tr //= 2
    return tr


def _strip_rows(width):
    return max(STRIP_ROWS, min(4 * STRIP_ROWS, STRIP_ELEMS // width // STRIP_ROWS * STRIP_ROWS))


def _over_strips(tr, strip, work):
    if strip is None or tr <= strip or tr % strip:
        work(slice(None))
        return

    def one(s, carry):
        work(pl.ds(pl.multiple_of(s * strip, strip), strip))
        return carry

    lax.fori_loop(0, tr // strip, one, 0)


def _acc_add(a_ref, v):
    if v.shape[0] == 1:
        a_ref[0:1, :] += v
    else:
        a_ref[...] += v.reshape(v.shape[0] // SUBLANES, SUBLANES, v.shape[1]).sum(axis=0)


def _rowwise(name, fn, rows, bcs, outs, accs=(), tr=None):
    n_rows = rows[0].shape[0]
    strip = _strip_rows(max(r.shape[1] for r in rows)) if tr is None else None
    if tr is None:
        per_row = sum(r.shape[1] * r.dtype.itemsize for r in rows) + sum(w * jnp.dtype(d).itemsize for w, d in outs)
        tr = _row_tile(n_rows, per_row)
    n_r, n_b, n_o, n_a = len(rows), len(bcs), len(outs), len(accs)

    def body(*refs):
        r_refs, b_refs = refs[:n_r], refs[n_r: n_r + n_b]
        o_refs = refs[n_r + n_b: n_r + n_b + n_o]
        a_refs = refs[n_r + n_b + n_o:]
        if n_a:
            @pl.when(pl.program_id(0) == 0)
            def _():
                for a in a_refs:
                    a[...] = jnp.zeros_like(a)

        def work(rws):
            ins = [r[rws, :].astype(F32) for r in r_refs] + [b[...].astype(F32) for b in b_refs]
            res = fn(*ins)
            res = res if isinstance(res, (tuple, list)) else (res,)
            for o, v in zip(o_refs, res[:n_o]):
                o[rws, :] = v.astype(o.dtype)
            for a, v in zip(a_refs, res[n_o:]):
                _acc_add(a, v)

        _over_strips(tr, strip, work)

    in_specs = [pl.BlockSpec((tr, r.shape[1]), lambda i: (i, 0)) for r in rows]
    in_specs += [pl.BlockSpec((1, b.shape[1]), lambda i: (0, 0)) for b in bcs]
    out_specs = [pl.BlockSpec((tr, w), lambda i: (i, 0)) for w, _ in outs]
    out_specs += [pl.BlockSpec((SUBLANES, w), lambda i: (0, 0)) for w in accs]
    out_shape = [jax.ShapeDtypeStruct((n_rows, w), d) for w, d in outs]
    out_shape += [jax.ShapeDtypeStruct((SUBLANES, w), F32) for w in accs]
    res = pl.pallas_call(
        body, name=name, grid=(n_rows // tr,), in_specs=in_specs, out_specs=out_specs, out_shape=out_shape,
        compiler_params=_params(("arbitrary",) if n_a else ("parallel",)),
    )(*rows, *bcs)
    return tuple(res)


def _rowwise_bwd(name, fn, rows, bcs, cts, need_rows, row_dtypes, need_bcs=None, adds=None, fwd_sums=(), tr=None,
                 into=None):
    n_rows = rows[0].shape[0]
    adds = adds or {}
    into = into or {}
    into_keys = sorted(into)
    need_bcs = list(range(len(bcs))) if need_bcs is None else list(need_bcs)
    ct_arrays = [c for c in cts if not isinstance(c, float)]
    add_keys = sorted(adds)
    add_arrays = [adds[k] for k in add_keys]
    strip = _strip_rows(max(r.shape[1] for r in rows)) if tr is None else None
    if tr is None:
        per_row = sum(r.shape[1] * r.dtype.itemsize for r in list(rows) + ct_arrays + add_arrays)
        per_row += sum(rows[i].shape[1] * jnp.dtype(d).itemsize for i, d in zip(need_rows, row_dtypes))
        tr = _row_tile(n_rows, per_row)
    n_r, n_b, n_c, n_ad = len(rows), len(bcs), len(ct_arrays), len(add_arrays)
    n_go, n_gb, n_fs = len(need_rows), len(need_bcs), len(fwd_sums)

    def body(*refs):
        pos = [0]

        def take(n):
            pos[0] += n
            return refs[pos[0] - n: pos[0]]

        r_refs, b_refs, c_refs, ad_refs = take(n_r), take(n_b), take(n_c), take(n_ad)
        take(len(into_keys))
        go_refs = take(n_go)
        acc_refs = refs[pos[0]:]

        @pl.when(pl.program_id(0) == 0)
        def _():
            for a in acc_refs:
                a[...] = jnp.zeros_like(a)

        def wrapped(*a):
            r = fn(*a)
            return tuple(r) if isinstance(r, (tuple, list)) else (r,)

        def work(rws):
            r_t = [r[rws, :].astype(F32) for r in r_refs]
            b_t = [r[...].astype(F32) for r in b_refs]
            outs, vjp = jax.vjp(wrapped, *r_t, *b_t)
            it = iter(c_refs)
            full = tuple(jnp.full(o.shape, c, F32) if isinstance(c, float) else next(it)[rws, :].astype(F32)
                         for o, c in zip(outs, cts))
            grads = vjp(full)
            for o_ref, i in zip(go_refs, need_rows):
                g = grads[i]
                if i in adds:
                    g = g + ad_refs[add_keys.index(i)][rws, :].astype(F32)
                o_ref[rws, :] = g.astype(o_ref.dtype)
            for a, j in zip(acc_refs[:n_gb], need_bcs):
                _acc_add(a, grads[n_r + j])
            for a, j in zip(acc_refs[n_gb:], fwd_sums):
                a[0:1, :] += jnp.full((1, LANES), jnp.sum(outs[j]), F32)

        _over_strips(tr, strip, work)

    def row_spec(w):
        return pl.BlockSpec((tr, w), lambda i: (i, 0))

    in_specs = [row_spec(r.shape[1]) for r in rows]
    in_specs += [pl.BlockSpec((1, b.shape[1]), lambda i: (0, 0)) for b in bcs]
    in_specs += [row_spec(c.shape[1]) for c in ct_arrays] + [row_spec(a.shape[1]) for a in add_arrays]
    in_specs += [pl.BlockSpec(memory_space=pl.ANY) for _ in into_keys]
    out_specs = [row_spec(rows[i].shape[1]) for i in need_rows]
    out_shape = [jax.ShapeDtypeStruct((n_rows, rows[i].shape[1]), d) for i, d in zip(need_rows, row_dtypes)]
    aliases = {}
    for pos_in, k in enumerate(into_keys):
        buf, col_block = into[k]
        out_specs[k] = pl.BlockSpec((tr, rows[need_rows[k]].shape[1]), lambda i, cb=col_block: (i, cb))
        out_shape[k] = jax.ShapeDtypeStruct(buf.shape, buf.dtype)
        aliases[len(in_specs) - len(into_keys) + pos_in] = k
    out_specs += [pl.BlockSpec((SUBLANES, bcs[j].shape[1]), lambda i: (0, 0)) for j in need_bcs]
    out_specs += [pl.BlockSpec((SUBLANES, LANES), lambda i: (0, 0)) for _ in fwd_sums]
    out_shape += [jax.ShapeDtypeStruct((SUBLANES, bcs[j].shape[1]), F32) for j in need_bcs]
    out_shape += [jax.ShapeDtypeStruct((SUBLANES, LANES), F32) for _ in fwd_sums]
    res = pl.pallas_call(
        body, name=name, grid=(n_rows // tr,), in_specs=in_specs, out_specs=out_specs, out_shape=out_shape,
        input_output_aliases=aliases, compiler_params=_params(("arbitrary",)),
    )(*rows, *bcs, *ct_arrays, *add_arrays, *[into[k][0] for k in into_keys])
    return tuple(res)


def _fold(acc):
    return jnp.sum(acc, axis=0, keepdims=True)


def _rms(x, w):
    return x * lax.rsqrt(jnp.mean(x * x, axis=-1, keepdims=True) + EPS) * w


def _sigmoid(x):
    return jax.nn.sigmoid(x)


def _silu(x):
    return x * _sigmoid(x)


def _log1p(u):
    series = u * (1.0 - u * (0.5 - u * (1.0 / 3.0 - u * 0.25)))
    return jnp.where(u < 0.01, series, jnp.log(1.0 + u))


def _softplus(x):
    return jnp.maximum(x, 0.0) + _log1p(jnp.exp(-jnp.abs(x)))


def _st_pre(x, w):
    return _rms(x, w)


def _st_lora_norms(dq):
    def fn(cqkv, qn, kvn):
        return _rms(cqkv[:, :dq], qn), _rms(cqkv[:, dq:], kvn)
    return fn


def _st_gated_norm(n_groups):
    def fn(y, xs, z, dskip, wn):
        yz = (y + dskip * xs) * _silu(z)
        gw = yz.shape[1] // n_groups
        parts = [_rms(yz[:, g * gw:(g + 1) * gw], wn[:, g * gw:(g + 1) * gw]) for g in range(n_groups)]
        return jnp.concatenate(parts, axis=1)
    return fn


def _st_mix(d):
    def fn(g, ao, so):
        return _sigmoid(g[:, :d]) * ao + _sigmoid(g[:, d:]) * so
    return fn


def _st_res_norm(h, y, w_post, w_pre):
    h2 = h + _rms(y, w_post)
    return h2, _rms(h2, w_pre)


def _st_swiglu(gt, up):
    return _silu(gt) * up


def _st_loss(pe, gl, h2, tgt, w_post):
    e = pe * _sigmoid(gl)
    diff = h2 + _rms(e, w_post) - tgt
    return 0.5 * jnp.mean(diff * diff, axis=-1, keepdims=True)


def _rope_tables(positions):
    half = QK_ROPE // 2
    inv_freq = ROPE_THETA ** (-jnp.arange(0, QK_ROPE, 2, dtype=F32) / QK_ROPE)
    ang = positions.astype(F32).reshape(-1, 1) * inv_freq
    cos, sin = jnp.cos(ang), jnp.sin(ang)
    n = ang.shape[0]
    z = lambda w: jnp.zeros((n, w), F32)
    c_tab = jnp.concatenate([jnp.ones((n, QK_NOPE), F32), cos, cos, z(QK_PAD - QK_NOPE - QK_ROPE)], axis=1)
    a_tab = jnp.concatenate([z(QK_NOPE), -sin, z(half), z(QK_PAD - QK_NOPE - QK_ROPE)], axis=1)
    b_tab = jnp.concatenate([z(QK_NOPE), z(half), sin, z(QK_PAD - QK_NOPE - QK_ROPE)], axis=1)
    return c_tab, a_tab, b_tab


def _rot(x, c, a, b):
    half = QK_ROPE // 2
    return x * c + pltpu.roll(x, QK_PAD - half, axis=1) * a + pltpu.roll(x, half, axis=1) * b


def _rot_t(g, c, a, b):
    half = QK_ROPE // 2
    return g * c + pltpu.roll(g * a, half, axis=1) + pltpu.roll(g * b, QK_PAD - half, axis=1)


def _st_rope(n_heads):
    def fn(qraw, kraw, sm, c, a, b):
        kpe = _rot(sm[:, :QK_PAD], c, a, b)
        scale = float(QK_NOPE + QK_ROPE) ** -0.5
        q = [_rot(qraw[:, h * QK_PAD:(h + 1) * QK_PAD], c, a, b) * scale for h in range(n_heads)]
        k = [kraw[:, h * QK_PAD:(h + 1) * QK_PAD] + kpe for h in range(n_heads)]
        return jnp.concatenate(q, axis=1), jnp.concatenate(k, axis=1)
    return fn


def _st_rope_bwd(n_heads):
    def fn(dq, dk, c, a, b):
        dqraw = [_rot_t(dq[:, h * QK_PAD:(h + 1) * QK_PAD], c, a, b) for h in range(n_heads)]
        dks = dk[:, :QK_PAD]
        for h in range(1, n_heads):
            dks = dks + dk[:, h * QK_PAD:(h + 1) * QK_PAD]
        return jnp.concatenate(dqraw, axis=1), _rot_t(dks, c, a, b), dk
    return fn


def _split3(x):
    h1 = x.astype(BF16)
    r1 = x - h1.astype(F32)
    h2 = r1.astype(BF16)
    h3 = (r1 - h2.astype(F32)).astype(BF16)
    return h1, h2, h3


def _tri_dot(tri, x):
    h1, h2, h3 = _split3(x)
    return (_dot(tri, h3) + _dot(tri, h2)) + _dot(tri, h1)


def _dot_tri(x, tri):
    h1, h2, h3 = _split3(x)
    return (_dot(h3, tri) + _dot(h2, tri)) + _dot(h1, tri)


def _st_dt(sm, bias, alog):
    x = sm[:, QK_PAD:] + bias
    dt = _softplus(x)
    return dt, dt * (-jnp.exp(alog))


def _st_dt_bwd(sm, ddt, dcum, bias, alog):
    n = sm.shape[0]
    i = lax.broadcasted_iota(jnp.int32, (n, n), 0)
    j = lax.broadcasted_iota(jnp.int32, (n, n), 1)
    upper = (j >= i).astype(BF16)
    dda = _tri_dot(upper, dcum)
    x = sm[:, QK_PAD:] + bias
    dt = _softplus(x)
    a = -jnp.exp(alog)
    draw = (ddt + dda * a) * _sigmoid(x)
    return draw, draw, dda * dt * a


def _conv_fwd(xbc, w, b):
    S, C = xbc.shape
    tr = _pick(S, (512, 256))
    tc = _pick(C, (1024, 512, 256, 128))
    hb = tr // SUBLANES

    def body(x_ref, halo_ref, w_ref, b_ref, c_ref, a_ref, ext):
        i = pl.program_id(1)
        halo = jnp.where(i == 0, 0.0, halo_ref[...])
        ext[0:SUBLANES, :] = halo
        ext[SUBLANES:, :] = x_ref[...]
        wv = w_ref[...]
        acc = b_ref[...] + wv[CONV_WIDTH - 1:CONV_WIDTH, :] * x_ref[...]
        for k in range(CONV_WIDTH - 1):
            off = SUBLANES - (CONV_WIDTH - 1) + k
            acc = acc + wv[k:k + 1, :] * ext[pl.ds(off, tr), :]
        c_ref[...] = acc
        a_ref[...] = _silu(acc)

    return pl.pallas_call(
        body, name="conv_fwd", grid=(C // tc, S // tr),
        in_specs=[pl.BlockSpec((tr, tc), lambda j, i: (i, j)),
                  pl.BlockSpec((SUBLANES, tc), lambda j, i: (jnp.maximum(i * hb - 1, 0), j)),
                  pl.BlockSpec((CONV_WIDTH, tc), lambda j, i: (0, j)),
                  pl.BlockSpec((1, tc), lambda j, i: (0, j))],
        out_specs=[pl.BlockSpec((tr, tc), lambda j, i: (i, j))] * 2,
        out_shape=[jax.ShapeDtypeStruct((S, C), F32)] * 2,
        scratch_shapes=[pltpu.VMEM((tr + SUBLANES, tc), F32)],
        compiler_params=_params(("parallel", "arbitrary")),
    )(xbc, xbc, w, b)


def _conv_bwd(xbc, dconv, w, buf, col0):
    S, C = xbc.shape
    tr = _pick(S, (512, 256))
    tc = _pick(C, (1024, 512, 256, 128))
    hb = tr // SUBLANES
    n_i = S // tr
    assert col0 % tc == 0, (col0, tc)
    cb0 = col0 // tc

    def body(x_ref, halo_ref, d_ref, dnext_ref, w_ref, buf_ref, dx_ref, dw_ref, ext, dext):
        i = pl.program_id(1)
        ext[0:SUBLANES, :] = jnp.where(i == 0, 0.0, halo_ref[...])
        ext[SUBLANES:, :] = x_ref[...]
        dext[0:tr, :] = d_ref[...]
        dext[tr:, :] = jnp.where(i == n_i - 1, 0.0, dnext_ref[...])
        wv = w_ref[...]
        d = d_ref[...]

        @pl.when(i == 0)
        def _():
            dw_ref[...] = jnp.zeros_like(dw_ref)

        dx = wv[CONV_WIDTH - 1:CONV_WIDTH, :] * d
        for k in range(CONV_WIDTH):
            if k < CONV_WIDTH - 1:
                dx = dx + wv[k:k + 1, :] * dext[pl.ds(CONV_WIDTH - 1 - k, tr), :]
                xs = ext[pl.ds(SUBLANES - (CONV_WIDTH - 1) + k, tr), :]
            else:
                xs = x_ref[...]
            prod = d * xs
            dw_ref[k * SUBLANES:(k + 1) * SUBLANES, :] += prod.reshape(tr // SUBLANES, SUBLANES, tc).sum(axis=0)
        dx_ref[...] = dx.astype(dx_ref.dtype)

    return pl.pallas_call(
        body, name="conv_bwd", grid=(C // tc, n_i),
        in_specs=[pl.BlockSpec((tr, tc), lambda j, i: (i, j)),
                  pl.BlockSpec((SUBLANES, tc), lambda j, i: (jnp.maximum(i * hb - 1, 0), j)),
                  pl.BlockSpec((tr, tc), lambda j, i: (i, j)),
                  pl.BlockSpec((SUBLANES, tc), lambda j, i: (jnp.minimum((i + 1) * hb, S // SUBLANES - 1), j)),
                  pl.BlockSpec((CONV_WIDTH, tc), lambda j, i: (0, j)),
                  pl.BlockSpec(memory_space=pl.ANY)],
        out_specs=[pl.BlockSpec((tr, tc), lambda j, i: (i, cb0 + j)),
                   pl.BlockSpec((CONV_WIDTH * SUBLANES, tc), lambda j, i: (0, j))],
        out_shape=[jax.ShapeDtypeStruct(buf.shape, buf.dtype), jax.ShapeDtypeStruct((CONV_WIDTH * SUBLANES, C), F32)],
        scratch_shapes=[pltpu.VMEM((tr + SUBLANES, tc), F32), pltpu.VMEM((tr + SUBLANES, tc), F32)],
        input_output_aliases={5: 0},
        compiler_params=_params(("parallel", "arbitrary")),
    )(xbc, xbc, dconv, dconv, w, buf)


def _st_dconv(d_inner):
    def fn(xc, dxa, dxb, db_, dc_):
        s = _sigmoid(xc)
        g = jnp.concatenate([dxa + dxb, db_, dc_], axis=1) * (s * (1.0 + xc * (1.0 - s)))
        return g, g
    return fn


def _chunk_setup_t(b_ref, c_ref, dac_ref, dar_ref, L):
    ii = lax.broadcasted_iota(jnp.int32, (L, L), 0)
    jj = lax.broadcasted_iota(jnp.int32, (L, L), 1)
    lower = ii >= jj
    upper = ii <= jj
    cum_c = _tri_dot(lower.astype(BF16), dac_ref[0])
    cum_r = _dot_tri(dar_ref[0], upper.astype(BF16))
    bm = b_ref[...].astype(BF16)
    cm = c_ref[...].astype(BF16)
    return lower, upper, cum_c, cum_r, bm, cm


def _ssd_specs_t(d_inner, n_groups, gw, L, rp, chunk_of):
    bb0 = d_inner // D_STATE
    cb0 = bb0 + n_groups
    return [pl.BlockSpec((L, gw), lambda g, c: (chunk_of(c), g)),
            pl.BlockSpec((L, D_STATE), lambda g, c: (chunk_of(c), bb0 + g)),
            pl.BlockSpec((L, D_STATE), lambda g, c: (chunk_of(c), cb0 + g)),
            pl.BlockSpec((1, rp, L), lambda g, c: (g, 0, chunk_of(c))),
            pl.BlockSpec((1, L, LANES), lambda g, c: (g, chunk_of(c), 0)),
            pl.BlockSpec((1, rp, L), lambda g, c: (g, 0, chunk_of(c)))]


def _ssd_fwd_t(xbc_a, dt_row, da_col, da_row, d_inner, n_groups, R):
    S = xbc_a.shape[0]
    L = min(CHUNK, S)
    NC = S // L
    P, N = SSM_HEADDIM, D_STATE
    gw = R * P
    rp = da_row.shape[1]

    def body(x_ref, b_ref, c_ref, dt_ref, dac_ref, dar_ref, y_ref, st_ref, state, y_t):
        @pl.when(pl.program_id(1) == 0)
        def _():
            state[...] = jnp.zeros_like(state)

        st_ref[0, 0] = state[...]
        lower, upper, cum_c, cum_r, bm, cm = _chunk_setup_t(b_ref, c_ref, dac_ref, dar_ref, L)
        gm_t = _dot_nt(bm, cm)
        dt = dt_ref[0]
        x_t = x_ref[...].T
        for r in range(R):
            cc = cum_c[:, r:r + 1]
            cr = cum_r[r:r + 1, :]
            m_t = (gm_t * jnp.exp(jnp.where(upper, cr - cc, -jnp.inf))).astype(BF16)
            x = x_t[r * P:(r + 1) * P, :] * dt[r:r + 1, :]
            s_r = state[r * P:(r + 1) * P, :]
            y_off = _dot_nt(s_r.astype(BF16), cm) * jnp.exp(cr)
            y_t[r * P:(r + 1) * P, :] = _dot(x.astype(BF16), m_t) + y_off
            last = cr[:, L - 1:L]
            xw = (x * jnp.exp(last - cr)).astype(BF16)
            state[r * P:(r + 1) * P, :] = s_r * jnp.exp(last) + _dot(xw, bm)
        y_ref[...] = y_t[...].T

    return pl.pallas_call(
        body, name="ssd_fwd", grid=(n_groups, NC),
        in_specs=_ssd_specs_t(d_inner, n_groups, gw, L, rp, lambda c: c),
        out_specs=[pl.BlockSpec((L, gw), lambda g, c: (c, g)),
                   pl.BlockSpec((1, 1, gw, N), lambda g, c: (g, c, 0, 0))],
        out_shape=[jax.ShapeDtypeStruct((S, d_inner), F32), jax.ShapeDtypeStruct((n_groups, NC, gw, N), F32)],
        scratch_shapes=[pltpu.VMEM((gw, N), F32), pltpu.VMEM((gw, L), F32)],
        compiler_params=_params(("parallel", "arbitrary")),
    )(xbc_a, xbc_a, xbc_a, dt_row, da_col, da_row)


def _ssd_bwd_t(xbc_a, dt_row, da_col, da_row, states, dy, d_inner, n_groups, R):
    S = xbc_a.shape[0]
    L = min(CHUNK, S)
    NC = S // L
    P, N = SSM_HEADDIM, D_STATE
    gw = R * P
    rp = da_row.shape[1]
    rev = lambda c: NC - 1 - c

    def body(x_ref, b_ref, c_ref, dt_ref, dac_ref, dar_ref, st_ref, dy_ref,
             dx_ref, db_ref, dc_ref, ddt_ref, dcum_ref, dstate, dx_t):
        @pl.when(pl.program_id(1) == 0)
        def _():
            dstate[...] = jnp.zeros_like(dstate)

        lower, upper, cum_c, cum_r, bm, cm = _chunk_setup_t(b_ref, c_ref, dac_ref, dar_ref, L)
        gm = _dot_nt(cm, bm)
        gm_t = _dot_nt(bm, cm)
        dt = dt_ref[0]
        x_t = x_ref[...].T
        dy_t = dy_ref[...].T
        sub = lax.broadcasted_iota(jnp.int32, (rp, L), 0)
        is_last = lax.broadcasted_iota(jnp.int32, (1, L), 1) == L - 1
        d_g = jnp.zeros((L, L), F32)
        d_g_t = jnp.zeros((L, L), F32)
        dc_acc = jnp.zeros((L, N), F32)
        db_acc = jnp.zeros((L, N), F32)
        ddt_out = jnp.zeros((rp, L), F32)
        dcum_out = jnp.zeros((rp, L), F32)
        for r in range(R):
            cc = jnp.broadcast_to(cum_c[:, r:r + 1], (L, L))
            cr = cum_r[r:r + 1, :]
            lam = jnp.exp(jnp.where(lower, cc - cum_r[r:r + 1, :], -jnp.inf))
            lam_t = jnp.exp(jnp.where(upper, cr - cc, -jnp.inf))
            m = gm * lam
            m_t = gm_t * lam_t
            dtr = dt[r:r + 1, :]
            xh = x_t[r * P:(r + 1) * P, :]
            x = xh * dtr
            xb = x.astype(BF16)
            d_y = dy_t[r * P:(r + 1) * P, :]
            d_yb = d_y.astype(BF16)
            s_r = st_ref[0, 0, r * P:(r + 1) * P, :]
            s_rb = s_r.astype(BF16)
            ds_n = dstate[r * P:(r + 1) * P, :]
            ds_nb = ds_n.astype(BF16)
            e = jnp.exp(cr)
            last = cr[:, L - 1:L]
            e_last = jnp.exp(last)
            w = jnp.exp(last - cr)
            d_x = _dot(d_yb, m.astype(BF16))
            d_m = _dot_tn(d_yb, xb)
            d_m_t = _dot_tn(xb, d_yb)
            d_ye = (d_y * e).astype(BF16)
            dc_acc = dc_acc + _dot_tn(d_ye, s_rb)
            ds_part = _dot(d_ye, cm)
            y_off = _dot_nt(s_rb, cm) * e
            dcum = jnp.sum(d_y * y_off, axis=0, keepdims=True)
            d_xw = _dot_nt(ds_nb, bm)
            d_x = d_x + d_xw * w
            dw_w = jnp.sum(d_xw * x, axis=0, keepdims=True) * w
            db_acc = db_acc + _dot_tn((x * w).astype(BF16), ds_nb)
            d_last = jnp.sum(ds_n * s_r, keepdims=True) * e_last + jnp.sum(dw_w, keepdims=True)
            dstate[r * P:(r + 1) * P, :] = e_last * ds_n + ds_part
            d_g = d_g + d_m * lam
            d_g_t = d_g_t + d_m_t * lam_t
            dcum = (dcum - dw_w + jnp.sum(d_m_t * m_t, axis=0, keepdims=True)
                    - jnp.sum(d_m * m, axis=0, keepdims=True) + jnp.where(is_last, d_last, 0.0))
            dx_t[r * P:(r + 1) * P, :] = d_x * dtr
            ddt = jnp.sum(d_x * xh, axis=0, keepdims=True)
            ddt_out = ddt_out + jnp.where(sub == r, ddt, 0.0)
            dcum_out = dcum_out + jnp.where(sub == r, dcum, 0.0)
        dc_ref[...] = dc_acc + _dot(d_g.astype(BF16), bm)
        db_ref[...] = db_acc + _dot(d_g_t.astype(BF16), cm)
        dx_ref[...] = dx_t[...].T
        ddt_ref[0] = ddt_out
        dcum_ref[0] = dcum_out

    gn = n_groups * N
    return pl.pallas_call(
        body, name="ssd_bwd", grid=(n_groups, NC),
        in_specs=_ssd_specs_t(d_inner, n_groups, gw, L, rp, rev) + [
            pl.BlockSpec((1, 1, gw, N), lambda g, c: (g, rev(c), 0, 0)),
            pl.BlockSpec((L, gw), lambda g, c: (rev(c), g))],
        out_specs=[pl.BlockSpec((L, gw), lambda g, c: (rev(c), g)),
                   pl.BlockSpec((L, N), lambda g, c: (rev(c), g)),
                   pl.BlockSpec((L, N), lambda g, c: (rev(c), g)),
                   pl.BlockSpec((1, rp, L), lambda g, c: (g, 0, rev(c))),
                   pl.BlockSpec((1, rp, L), lambda g, c: (g, 0, rev(c)))],
        out_shape=[jax.ShapeDtypeStruct((S, d_inner), F32), jax.ShapeDtypeStruct((S, gn), F32),
                   jax.ShapeDtypeStruct((S, gn), F32), jax.ShapeDtypeStruct((n_groups, rp, S), F32),
                   jax.ShapeDtypeStruct((n_groups, rp, S), F32)],
        scratch_shapes=[pltpu.VMEM((gw, N), F32), pltpu.VMEM((gw, L), F32)],
        compiler_params=_params(("parallel", "arbitrary")),
    )(xbc_a, xbc_a, xbc_a, dt_row, da_col, da_row, states, dy)


def _attn_scale():
    return float(QK_NOPE + QK_ROPE) ** -0.5


def _diag_mask(t):
    return lax.broadcasted_iota(jnp.int32, (t, t), 0) <= lax.broadcasted_iota(jnp.int32, (t, t), 1)


def _walk_wide(lo, hi, tile_step, joint=True):
    n = hi - lo

    def step(j, width):
        if joint:
            tile_step(j, width)
        else:
            for u in range(width):
                tile_step(j + u, 1)

    def quad(t, carry):
        step(lo + 4 * t, 4)
        return carry

    lax.fori_loop(0, n // 4, quad, 0)

    @pl.when(n % 4 >= 2)
    def _():
        step(hi - n % 4, 2)

    @pl.when(n % 2 == 1)
    def _():
        step(hi - 1, 1)


def _carried(main_body, n_in, n_out, n_scratch, carry, grid):
    if carry is None:
        return main_body
    n_ci, n_co = len(carry["ins"]), len(carry["outs"])

    def body(*refs):
        pos = [0]

        def take(n):
            pos[0] += n
            return refs[pos[0] - n: pos[0]]

        ins, c_ins, outs, c_outs, scratch, sems = take(n_in), take(n_ci), take(n_out), take(n_co), take(n_scratch), take(3)
        steps = [pl.program_id(a) for a in range(len(grid))]

        @pl.when(functools.reduce(jnp.logical_and, [s == 0 for s in steps]))
        def _():
            for cp in carry["copies"](c_ins, c_outs, *sems):
                cp.start()

        main_body(*ins, *outs, *scratch)

        @pl.when(functools.reduce(jnp.logical_and, [s == n - 1 for s, n in zip(steps, grid)]))
        def _():
            for cp in carry["copies"](c_ins, c_outs, *sems):
                cp.wait()

    return body


def _carry_call(carry):
    if carry is None:
        return [], [], [], [], []
    sems = [pltpu.SemaphoreType.DMA((carry["n_remote"],)), pltpu.SemaphoreType.DMA((carry["n_remote"],)),
            pltpu.SemaphoreType.DMA((max(carry["n_local"], 1),))]
    return (list(carry["ins"]), [_HBM] * len(carry["ins"]), [_HBM] * len(carry["outs"]), list(carry["outs"]), sems)


def _flash_fwd(q, k, v_t, n_heads, carry=None):
    S = q.shape[0]
    T = min(FLASH_T, S)
    grid = (n_heads, S // T)
    c_args, c_in_specs, c_out_specs, c_out_shapes, c_sems = _carry_call(carry)

    def body(q_ref, k_ref, vt_ref, o_ref, lse_ref, m_s, l_s, acc_t):
        i = pl.program_id(1)
        m_s[...] = jnp.full_like(m_s, -jnp.inf)
        l_s[...] = jnp.zeros_like(l_s)
        acc_t[...] = jnp.zeros_like(acc_t)
        qv = q_ref[...]

        def step(j, width, masked):
            keys = pl.ds(pl.multiple_of(j * T, T), width * T)
            s_t = _dot_nt(k_ref[keys, :], qv)
            if masked:
                s_t = jnp.where(_diag_mask(T), s_t, -jnp.inf)
            m_prev = m_s[...]
            m_new = jnp.maximum(m_prev, jnp.max(s_t, axis=0, keepdims=True))
            alpha = jnp.exp(m_prev - m_new)
            p_t = jnp.exp(s_t - m_new)
            l_s[...] = alpha * l_s[...] + jnp.sum(p_t, axis=0, keepdims=True)
            acc_t[...] = alpha * acc_t[...] + _dot(vt_ref[:, keys], p_t.astype(BF16))
            m_s[...] = m_new

        _walk_wide(0, i, lambda j, width: step(j, width, False))
        step(i, 1, True)
        o_ref[...] = (acc_t[...] / l_s[...]).T
        lse_ref[0] = m_s[...] + jnp.log(l_s[...])

    res = pl.pallas_call(
        _carried(body, 3, 2, 3, carry, grid), name="flash_fwd", grid=grid,
        in_specs=[pl.BlockSpec((T, QK_PAD), lambda h, i: (i, h)),
                  pl.BlockSpec((S, QK_PAD), lambda h, i: (0, h)),
                  pl.BlockSpec((V_DIM, S), lambda h, i: (h, 0))] + c_in_specs,
        out_specs=[pl.BlockSpec((T, V_DIM), lambda h, i: (i, h)),
                   pl.BlockSpec((1, 1, T), lambda h, i: (h, 0, i))] + c_out_specs,
        out_shape=[jax.ShapeDtypeStruct((S, n_heads * V_DIM), F32),
                   jax.ShapeDtypeStruct((n_heads, 1, S), F32)] + c_out_shapes,
        scratch_shapes=[pltpu.VMEM((1, T), F32), pltpu.VMEM((1, T), F32), pltpu.VMEM((V_DIM, T), F32)] + c_sems,
        compiler_params=_params(("arbitrary", "arbitrary")),
    )(q, k, v_t, *c_args)
    return res[0], res[1], list(res[2:])


def _flash_bwd(q, k, v, do, lse_row, delta_row, n_heads, carry=None):
    S = q.shape[0]
    T = min(FLASH_T, S)
    nq = S // T
    grid = (n_heads, S // T)
    c_args, c_in_specs, c_out_specs, c_out_shapes, c_sems = _carry_call(carry)

    def body(q_ref, k_ref, v_ref, do_ref, lse_ref, dl_ref, dq_ref, dk_ref, dv_ref, dk_acc, dv_acc):
        j = pl.program_id(1)

        @pl.when(j == 0)
        def _():
            dq_ref[...] = jnp.zeros_like(dq_ref)

        dk_acc[...] = jnp.zeros_like(dk_acc)
        dv_acc[...] = jnp.zeros_like(dv_acc)
        kv = k_ref[...]
        vv = v_ref[...]

        def step(i, width, masked):
            cols = pl.ds(pl.multiple_of(i * T, T), width * T)
            qt = q_ref[cols, :]
            dot = do_ref[cols, :]
            s_t = _dot_nt(kv, qt)
            if masked:
                s_t = jnp.where(_diag_mask(T), s_t, -jnp.inf)
            p_t = jnp.exp(s_t - lse_ref[0, :, cols])
            dv_acc[...] += _dot(p_t.astype(BF16), dot)
            ds_t = (p_t * (_dot_nt(vv, dot) - dl_ref[0, :, cols])).astype(BF16)
            dk_acc[...] += _dot(ds_t, qt)
            dq_ref[cols, :] += _dot_tn(ds_t, kv)

        step(j, 1, True)
        _walk_wide(j + 1, nq, lambda i, width: step(i, width, False), joint=False)
        dk_ref[...] = dk_acc[...]
        dv_ref[...] = dv_acc[...].astype(dv_ref.dtype)

        @pl.when(j == nq - 1)
        def _():
            dq_ref[...] = dq_ref[...] * _attn_scale()

    res = pl.pallas_call(
        _carried(body, 6, 3, 2, carry, grid), name="flash_bwd", grid=grid,
        in_specs=[pl.BlockSpec((S, QK_PAD), lambda h, j: (0, h)),
                  pl.BlockSpec((T, QK_PAD), lambda h, j: (j, h)),
                  pl.BlockSpec((T, V_DIM), lambda h, j: (j, h)),
                  pl.BlockSpec((S, V_DIM), lambda h, j: (0, h)),
                  pl.BlockSpec((1, 1, S), lambda h, j: (h, 0, 0)),
                  pl.BlockSpec((1, 1, S), lambda h, j: (h, 0, 0))] + c_in_specs,
        out_specs=[pl.BlockSpec((S, QK_PAD), lambda h, j: (0, h)),
                   pl.BlockSpec((T, QK_PAD), lambda h, j: (j, h)),
                   pl.BlockSpec((T, V_DIM), lambda h, j: (j, h))] + c_out_specs,
        out_shape=[jax.ShapeDtypeStruct((S, n_heads * QK_PAD), F32), jax.ShapeDtypeStruct((S, n_heads * QK_PAD), F32),
                   jax.ShapeDtypeStruct((S, n_heads * V_DIM), BF16)] + c_out_shapes,
        scratch_shapes=[pltpu.VMEM((T, QK_PAD), F32), pltpu.VMEM((T, V_DIM), F32)] + c_sems,
        compiler_params=_params(("arbitrary", "arbitrary")),
    )(q, k, v, do, lse_row, delta_row, *c_args)
    return res[0], res[1], res[2], list(res[3:])


def _st_delta(n_heads):
    def fn(do, o):
        prod = do * o
        lane = lax.broadcasted_iota(jnp.int32, (do.shape[0], LANES), 1)
        out = jnp.zeros((do.shape[0], LANES), F32)
        for h in range(n_heads):
            out = out + jnp.where(lane == h, jnp.sum(prod[:, h * V_DIM:(h + 1) * V_DIM], axis=1, keepdims=True), 0.0)
        return out
    return fn


def _pad_cols(w, width):
    return jnp.pad(w, ((0, 0), (0, width - w.shape[1])))


def _dims(x, p, q_norm, kv_norm, w_uq, dt_bias, ssm_norm, conv_b):
    d = dict(S=x.shape[0], D=x.shape[1], PLE=p.shape[1], DQ=q_norm.shape[1], DKV=kv_norm.shape[1],
             NH=w_uq.shape[1] // (QK_NOPE + QK_ROPE), NHS=dt_bias.shape[1], DI=ssm_norm.shape[1],
             CONV=conv_b.shape[1])
    d["G"] = (d["CONV"] - d["DI"]) // (2 * D_STATE)
    d["R"] = d["NHS"] // d["G"]
    return d


def _assemble(name, blocks):
    rows, cols = blocks.shape[1:]
    if name in COL_SHARDED:
        return blocks.transpose(1, 0, 2).reshape(rows, N_DEV * cols)
    return blocks.reshape(N_DEV * rows, cols)


def _by_device(name, g):
    if name in COL_SHARDED:
        return g.reshape(g.shape[0], N_DEV, g.shape[1] // N_DEV).transpose(1, 0, 2)
    return g.reshape(N_DEV, g.shape[0] // N_DEV, g.shape[1])


def _local_step(x, p, positions, target, small, conv_w, wts, late_names=(), late_shards=(), exchange=True):
    wts = dict(wts)
    dm = _dims(x, p, small["q_norm"], small["kv_norm"], wts["w_uq"], small["dt_bias"], small["ssm_norm"],
               small["conv_b"])
    S, D, DQ, DKV, NH, NHS, DI, CONV, G, R = (dm[k] for k in ("S", "D", "DQ", "DKV", "NH", "NHS", "DI", "CONV", "G", "R"))
    rp = -(-R // SUBLANES) * SUBLANES
    L = min(CHUNK, S)

    w_in = wts["w_in"]
    o = [0]
    for n in (DQ, DKV, QK_ROPE, DI, CONV, NHS, D, D):
        o.append(o[-1] + n)
    w_cqkv, w_kr, w_z, w_xbc, w_dt, w_g = (w_in[:, o[0]:o[2]], w_in[:, o[2]:o[3]], w_in[:, o[3]:o[4]],
                                           w_in[:, o[4]:o[5]], w_in[:, o[5]:o[6]], w_in[:, o[6]:o[8]])
    zc = lambda n: jnp.zeros((D, n), BF16)
    w_sm = jnp.concatenate([zc(QK_NOPE), w_kr, zc(QK_PAD - QK_NOPE - QK_ROPE), w_dt, zc(LANES - NHS)], axis=1)
    w_q = jnp.pad(wts["w_uq"].reshape(DQ, NH, QK_NOPE + QK_ROPE),
                  ((0, 0), (0, 0), (0, QK_PAD - QK_NOPE - QK_ROPE))).reshape(DQ, NH * QK_PAD)
    ukv = wts["w_ukv"].reshape(DKV, NH, QK_NOPE + V_DIM)
    w_k = jnp.pad(ukv[:, :, :QK_NOPE], ((0, 0), (0, 0), (0, QK_PAD - QK_NOPE))).reshape(DKV, NH * QK_PAD)
    w_v = ukv[:, :, QK_NOPE:].reshape(DKV, NH * V_DIM)
    dt_bias_p, a_log_p = _pad_cols(small["dt_bias"], LANES), _pad_cols(small["a_log"], LANES)
    dskip_rep = jnp.repeat(small["d_skip"], SSM_HEADDIM, axis=1)
    c_tab, a_tab, b_tab = _rope_tables(positions)

    (u,) = _rowwise("pre_norm", _st_pre, [x], [small["mix_norm_pre"]], [(D, BF16)])
    cqkv = _mm("in_cqkv", u, w_cqkv, "nn")
    z = _mm("in_z", u, w_z, "nn")
    xbc = _mm("in_xbc", u, w_xbc, "nn")
    g = _mm("in_gates", u, w_g, "nn")
    sm = _mm("in_small", u, w_sm, "nn")

    lora_fn = _st_lora_norms(DQ)
    cq_n, ckv_n = _rowwise("lora_norms", lora_fn, [cqkv], [small["q_norm"], small["kv_norm"]], [(DQ, BF16), (DKV, BF16)])
    qraw = _mm("up_q", cq_n, w_q, "nn")
    kraw = _mm("up_k", ckv_n, w_k, "nn")
    v = _mm("up_v", ckv_n, w_v, "nn", out_dtype=BF16)
    v_t = _mm("up_v_t", w_v.T, ckv_n, "nt", out_dtype=BF16)
    q, k = _rowwise("rope", _st_rope(NH), [qraw, kraw, sm, c_tab, a_tab, b_tab], [],
                    [(NH * QK_PAD, BF16), (NH * QK_PAD, BF16)])
    attn, lse, late_blocks = _flash_fwd(q, k, v_t, NH, carry=_gather_carry(list(late_shards)) if late_names else None)
    wts.update({n: _assemble(n, b) for n, b in zip(late_names, late_blocks)})

    xbc_c, xbc_a = _conv_fwd(xbc, conv_w, small["conv_b"])
    dt, da = _rowwise("dt", _st_dt, [sm], [dt_bias_p, a_log_p], [(LANES, F32), (LANES, F32)])

    def col_layout(t):
        return _pad_cols(t[:, :NHS].reshape(S, G, R).transpose(1, 0, 2).reshape(G * S, R), LANES).reshape(G, S, LANES)

    def row_layout(t):
        return jnp.pad(t[:, :NHS].reshape(S, G, R).transpose(1, 2, 0), ((0, 0), (0, rp - R), (0, 0)))

    dt_row, da_col, da_row = row_layout(dt), col_layout(da), row_layout(da)
    xs = xbc_a[:, :DI]
    y, states = _ssd_fwd_t(xbc_a, dt_row, da_col, da_row, DI, G, R)
    gn_fn = _st_gated_norm(G)
    (ssm,) = _rowwise("gated_norm", gn_fn, [y, xs, z], [dskip_rep, small["ssm_norm"]], [(DI, BF16)])

    ao = _mm("attn_o", attn, wts["w_attn_o"], "nn")
    so = _mm("ssm_o", ssm, wts["w_ssm_o"], "nn")
    mix_fn = _st_mix(D)
    (mixed,) = _rowwise("mix", mix_fn, [g, ao, so], [], [(D, BF16)])
    mo = _mm("out_proj", mixed, wts["w_out"], "nn")
    h1, f = _rowwise("res1", _st_res_norm, [x, mo], [small["mix_norm_post"], small["ffn_norm_pre"]], [(D, F32), (D, BF16)])
    gt = _mm("ffn_gate", f, wts["w_gate"], "nn")
    up = _mm("ffn_up", f, wts["w_up"], "nn")
    (act,) = _rowwise("swiglu", _st_swiglu, [gt, up], [], [(gt.shape[1], BF16)])
    dn = _mm("ffn_down", act, wts["w_down"], "nn")
    h2, a3 = _rowwise("res2", _st_res_norm, [h1, dn], [small["ffn_norm_post"], small["ple_norm_pre"]], [(D, F32), (D, BF16)])
    gl = _mm("ple_gate", a3, wts["w_ple_gate"], "nn")
    pe = _mm("ple_proj", p, wts["w_ple"], "nn")

    sg = {}
    bg = {}
    dpe, dgl, dh2, d_w, loss_acc = _rowwise_bwd(
        "loss", _st_loss, [pe, gl, h2, target], [small["ple_norm_post"]], [1.0], [0, 1, 2], [BF16, BF16, F32], fwd_sums=(0,))
    sg["ple_norm_post"] = _fold(d_w)
    loss = loss_acc[0:1, :]
    bg["w_ple"] = _mm("d_w_ple", p, dpe, "tn", out_dtype=BF16)
    bg["w_ple_gate"] = _mm("d_w_ple_gate", a3, dgl, "tn", out_dtype=BF16)
    da3 = _mm("d_a3", dgl, wts["w_ple_gate"], "nt")

    dh1, ddn, d_post, d_pre = _rowwise_bwd(
        "res2_bwd", _st_res_norm, [h1, dn], [small["ffn_norm_post"], small["ple_norm_pre"]], [dh2, da3], [0, 1], [F32, BF16])
    sg["ffn_norm_post"], sg["ple_norm_pre"] = _fold(d_post), _fold(d_pre)
    bg["w_down"] = _mm("d_w_down", act, ddn, "tn", out_dtype=BF16)
    dact = _mm("d_act", ddn, wts["w_down"], "nt")
    dgt, dup = _rowwise_bwd("swiglu_bwd", _st_swiglu, [gt, up], [], [dact], [0, 1], [BF16, BF16])
    bg["w_gate"] = _mm("d_w_gate", f, dgt, "tn", out_dtype=BF16)
    bg["w_up"] = _mm("d_w_up", f, dup, "tn", out_dtype=BF16)
    df = _mm("d_f_gate", dgt, wts["w_gate"], "nt")
    df = _mm("d_f_up", dup, wts["w_up"], "nt", acc_in=df)

    dx_res, dmo, d_post, d_pre = _rowwise_bwd(
        "res1_bwd", _st_res_norm, [x, mo], [small["mix_norm_post"], small["ffn_norm_pre"]], [dh1, df], [0, 1], [F32, BF16])
    sg["mix_norm_post"], sg["ffn_norm_pre"] = _fold(d_post), _fold(d_pre)
    bg["w_out"] = _mm("d_w_out", mixed, dmo, "tn", out_dtype=BF16)
    dmixed = _mm("d_mixed", dmo, wts["w_out"], "nt")
    pw = (DI, 2 * D, CONV, DQ + DKV, QK_ROPE + NHS)
    po = [sum(pw[:k]) for k in range(len(pw))]
    assert po[1] % pw[1] == 0 and po[3] % pw[3] == 0, (po, pw)
    dproj = lax.empty((S, sum(pw)), BF16)
    dproj, dao, dso = _rowwise_bwd("mix_bwd", mix_fn, [g, ao, so], [], [dmixed], [0, 1, 2], [BF16, BF16, BF16],
                                   into={0: (dproj, po[1] // pw[1])})
    bg["w_attn_o"] = _mm("d_w_attn_o", attn, dao, "tn", out_dtype=BF16)
    bg["w_ssm_o"] = _mm("d_w_ssm_o", ssm, dso, "tn", out_dtype=BF16)
    dattn = _mm("d_attn", dao, wts["w_attn_o"], "nt", out_dtype=BF16)
    dssm = _mm("d_ssm", dso, wts["w_ssm_o"], "nt")

    dy, dxs_a, dproj, d_dskip, d_ssmn = _rowwise_bwd(
        "gated_norm_bwd", gn_fn, [y, xs, z], [dskip_rep, small["ssm_norm"]], [dssm], [0, 1, 2], [F32, F32, BF16],
        into={2: (dproj, 0)})
    sg["d_skip"] = _fold(d_dskip).reshape(NHS, SSM_HEADDIM).sum(axis=1).reshape(1, NHS)
    sg["ssm_norm"] = _fold(d_ssmn)
    dxs_b, d_b, d_c, ddt_row, dcum_row = _ssd_bwd_t(xbc_a, dt_row, da_col, da_row, states, dy, DI, G, R)

    def from_row(t):
        return _pad_cols(t[:, :R, :].transpose(2, 0, 1).reshape(S, NHS), LANES)

    ddtraw, d_bias, d_alog = _rowwise("dt_bwd", _st_dt_bwd, [sm, from_row(ddt_row), from_row(dcum_row)],
                                      [dt_bias_p, a_log_p], [(LANES, F32)], accs=(LANES, LANES), tr=L)
    sg["dt_bias"], sg["a_log"] = _fold(d_bias)[:, :NHS], _fold(d_alog)[:, :NHS]
    dconv, d_cb = _rowwise("dconv", _st_dconv(DI), [xbc_c, dxs_a, dxs_b, d_b, d_c], [], [(CONV, F32)], accs=(CONV,))
    sg["conv_b"] = _fold(d_cb)
    dproj, d_cw = _conv_bwd(xbc, dconv, conv_w, dproj, po[2])
    d_conv_w = d_cw.reshape(CONV_WIDTH, SUBLANES, CONV).sum(axis=1)

    (delta,) = _rowwise("attn_delta", _st_delta(NH), [dattn, attn], [], [(LANES, F32)])
    delta = delta[:, :NH].T
    late_sent = [_by_device(n, bg.pop(n)) for n in late_names]
    dq, dk, dv, late_recv = _flash_bwd(q, k, v, dattn, lse, delta.reshape(NH, 1, S), NH,
                                       carry=_scatter_carry(late_sent) if late_names else None)
    dqraw, dkr, dk = _rowwise("rope_bwd", _st_rope_bwd(NH), [dq, dk, c_tab, a_tab, b_tab], [],
                              [(NH * QK_PAD, BF16), (QK_PAD, F32), (NH * QK_PAD, BF16)])
    d_w_q = _mm("d_w_q", cq_n, dqraw, "tn", out_dtype=BF16)
    d_w_k = _mm("d_w_k", ckv_n, dk, "tn", out_dtype=BF16)
    d_w_v = _mm("d_w_v", ckv_n, dv, "tn", out_dtype=BF16)
    dcq_n = _mm("d_cq_n", dqraw, w_q, "nt")
    dckv_n = _mm("d_ckv_n_k", dk, w_k, "nt")
    dckv_n = _mm("d_ckv_n_v", dv, w_v, "nt", acc_in=dckv_n)
    bg["w_uq"] = d_w_q.reshape(DQ, NH, QK_PAD)[:, :, :QK_NOPE + QK_ROPE].reshape(DQ, NH * (QK_NOPE + QK_ROPE))
    bg["w_ukv"] = jnp.concatenate([d_w_k.reshape(DKV, NH, QK_PAD)[:, :, :QK_NOPE], d_w_v.reshape(DKV, NH, V_DIM)],
                                  axis=2).reshape(DKV, NH * (QK_NOPE + V_DIM))
    dproj, d_qn, d_kvn = _rowwise_bwd("lora_norms_bwd", lora_fn, [cqkv], [small["q_norm"], small["kv_norm"]],
                                      [dcq_n, dckv_n], [0], [BF16], into={0: (dproj, po[3] // pw[3])})
    sg["q_norm"], sg["kv_norm"] = _fold(d_qn), _fold(d_kvn)

    d_small = jnp.concatenate([dkr[:, QK_NOPE:QK_NOPE + QK_ROPE], ddtraw[:, :NHS]], axis=1).astype(BF16)
    dproj = lax.dynamic_update_slice(dproj, d_small, (0, po[4]))
    w_in = jnp.concatenate([w_z, w_g, w_xbc, w_cqkv, w_kr, w_dt], axis=1)
    d_w = _mm("d_w_in", u, dproj, "tn", out_dtype=BF16)
    kr = po[4] + QK_ROPE
    bg["w_in"] = jnp.concatenate([d_w[:, po[3]:po[4]], d_w[:, po[4]:kr], d_w[:, po[0]:po[1]], d_w[:, po[2]:po[3]],
                                  d_w[:, kr:kr + NHS], d_w[:, po[1]:po[2]]], axis=1)
    rest_names = tuple(n for n in BIG if n in bg)
    if not exchange:
        du = _mm("d_u", dproj, w_in, "nt")
        grad_x, d_pre = _rowwise_bwd("pre_norm_bwd", _st_pre, [x], [small["mix_norm_pre"]], [du], [0], [F32],
                                     adds={0: dx_res})
        sg["mix_norm_pre"] = _fold(d_pre)
        return loss, grad_x, sg, d_conv_w, bg, {}
    rest_sent = [_by_device(n, bg.pop(n)) for n in rest_names]
    du, rest_recv = _mm("d_u", dproj, w_in, "nt", carry=_scatter_carry(rest_sent))
    grad_x, d_pre = _rowwise_bwd("pre_norm_bwd", _st_pre, [x], [small["mix_norm_pre"]], [du], [0], [F32], adds={0: dx_res})
    sg["mix_norm_pre"] = _fold(d_pre)
    sent = dict(zip(late_names, late_sent), **dict(zip(rest_names, rest_sent)))
    recv = dict(zip(late_names, late_recv), **dict(zip(rest_names, rest_recv)))
    return loss, grad_x, sg, d_conv_w, sent, recv


_HBM = pl.BlockSpec(memory_space=pltpu.HBM)
_FLIPS = ((0, 0, 1), (1, 0, 0), (0, 1, 0), (1, 1, 0), (1, 0, 1), (0, 1, 1), (1, 1, 1))


def _place():
    return lax.axis_index("x"), lax.axis_index("y"), lax.axis_index("c")


def _flipped(place, flip):
    return tuple(1 - v if f else v for v, f in zip(place, flip))


def _gather_carry(blocks):
    nw = len(blocks)

    def copies(ins, outs, send_sems, recv_sems, local_sems):
        x, y, c = _place()
        me = 4 * x + 2 * y + c
        cps = []
        for w in range(nw):
            cps.append(pltpu.make_async_copy(ins[w], outs[w].at[me], local_sems.at[w]))
            for k, flip in enumerate(_FLIPS):
                cps.append(pltpu.make_async_remote_copy(
                    src_ref=ins[w], dst_ref=outs[w].at[me], send_sem=send_sems.at[7 * w + k],
                    recv_sem=recv_sems.at[7 * w + k], device_id=_flipped((x, y, c), flip), device_id_type=MESH))
        return cps

    return dict(ins=blocks, outs=[jax.ShapeDtypeStruct((N_DEV,) + b.shape, b.dtype) for b in blocks],
                n_remote=7 * nw, n_local=nw, copies=copies)


def _scatter_carry(by_dev):
    nw = len(by_dev)

    def copies(ins, outs, send_sems, recv_sems, local_sems):
        x, y, c = _place()
        cps = []
        for w in range(nw):
            for k, flip in enumerate(_FLIPS):
                px, py, pc = _flipped((x, y, c), flip)
                cps.append(pltpu.make_async_remote_copy(
                    src_ref=ins[w].at[4 * px + 2 * py + pc], dst_ref=outs[w].at[k], send_sem=send_sems.at[7 * w + k],
                    recv_sem=recv_sems.at[7 * w + k], device_id=(px, py, pc), device_id_type=MESH))
        return cps

    return dict(ins=by_dev, outs=[jax.ShapeDtypeStruct((7,) + b.shape[1:], b.dtype) for b in by_dev],
                n_remote=7 * nw, n_local=0, copies=copies)


def _all_gather(name, blocks):
    nw = len(blocks)

    def body(*refs):
        x_refs, out_refs = refs[:nw], refs[nw:2 * nw]
        send_sems, recv_sems, local_sems = refs[2 * nw:]
        x, y, c = _place()
        me, sibling = (x, y, c), (x, y, 1 - c)
        chips = [(1 - x, y), (x, 1 - y), (1 - x, 1 - y)]

        def slot(w, px, py, pc):
            return out_refs[w].at[4 * px + 2 * py + pc]

        def copy(w, k, blk, to, src=None):
            return pltpu.make_async_remote_copy(
                src_ref=slot(w, *blk) if src is None else src, dst_ref=slot(w, *blk),
                send_sem=send_sems.at[7 * w + k], recv_sem=recv_sems.at[7 * w + k], device_id=to, device_id_type=MESH)

        mine = [pltpu.make_async_copy(x_refs[w], slot(w, *me), local_sems.at[w]) for w in range(nw)]
        for cp in mine:
            cp.start()
        first = []
        for w in range(nw):
            first.append(copy(w, 0, me, sibling, src=x_refs[w]))
            first += [copy(w, 1 + j, me, (*chip, c), src=x_refs[w]) for j, chip in enumerate(chips)]
        for cp in first:
            cp.start()
        passed = []
        for j, chip in enumerate(chips):
            for w in range(nw):
                copy(w, 1 + j, (*chip, c), me).wait_recv()
                passed.append(copy(w, 4 + j, (*chip, c), sibling))
                passed[-1].start()
        for w in range(nw):
            copy(w, 0, sibling, me).wait_recv()
        for j, chip in enumerate(chips):
            for w in range(nw):
                copy(w, 4 + j, (*chip, 1 - c), me).wait_recv()
        for cp in first + passed:
            cp.wait_send()
        for cp in mine:
            cp.wait()

    return pl.pallas_call(
        body, name=name, out_shape=[jax.ShapeDtypeStruct((N_DEV,) + b.shape, b.dtype) for b in blocks],
        in_specs=[_HBM] * nw, out_specs=[_HBM] * nw,
        scratch_shapes=[pltpu.SemaphoreType.DMA((7 * nw,)), pltpu.SemaphoreType.DMA((7 * nw,)),
                        pltpu.SemaphoreType.DMA((nw,))],
    )(*blocks)


def _lane_pad(n):
    return -(-n // LANES) * LANES


def _pack_small(vecs, mat):
    width = max(sum(_lane_pad(v.shape[1]) for v in vecs), _lane_pad(mat.shape[1]))
    row0 = jnp.concatenate([_pad_cols(v, _lane_pad(v.shape[1])) for v in vecs], axis=1)
    rows = jnp.concatenate([_pad_cols(row0, width), _pad_cols(mat, width)], axis=0)
    return jnp.pad(rows, ((0, SUBLANES - rows.shape[0]), (0, 0)))


def _unpack_small(packed, sizes, mat_cols):
    vecs, off = [], 0
    for n in sizes:
        vecs.append(packed[0:1, off:off + n])
        off += _lane_pad(n)
    return vecs, packed[1:1 + CONV_WIDTH, :mat_cols]


def _adamw(w, g, m, v):
    m = ADAM_B1 * m + (1.0 - ADAM_B1) * g
    v = ADAM_B2 * v + (1.0 - ADAM_B2) * (g * g)
    m_hat = m / (1.0 - ADAM_B1 ** ADAM_STEP)
    v_hat = v / (1.0 - ADAM_B2 ** ADAM_STEP)
    delta = -ADAM_LR * (m_hat / (jnp.sqrt(v_hat) + ADAM_EPS) + ADAM_WD * w)
    return delta, m, v


BIG = ("w_in", "w_uq", "w_ukv", "w_attn_o", "w_ssm_o", "w_out", "w_gate", "w_up", "w_down", "w_ple_gate", "w_ple")
FIRST = ("w_in", "w_uq", "w_ukv")
LATE = ("w_attn_o", "w_ssm_o", "w_out", "w_gate", "w_up", "w_down", "w_ple_gate", "w_ple")
COL_SHARDED = ("w_in", "w_uq", "w_ukv", "w_gate", "w_up", "w_ple")
SMALL = ("mix_norm_pre", "mix_norm_post", "q_norm", "kv_norm", "conv_b", "dt_bias", "a_log", "d_skip", "ssm_norm",
         "ffn_norm_pre", "ffn_norm_post", "ple_norm_pre", "ple_norm_post")
WEIGHTS = ("mix_norm_pre", "mix_norm_post", "w_in", "q_norm", "w_uq", "kv_norm", "w_ukv", "conv_w", "conv_b", "dt_bias",
           "a_log", "d_skip", "ssm_norm", "w_attn_o", "w_ssm_o", "w_out", "ffn_norm_pre", "ffn_norm_post", "w_gate",
           "w_up", "w_down", "ple_norm_pre", "ple_norm_post", "w_ple_gate", "w_ple")


def _step(x, p, positions, target, w, m, v):
    xi, yi, ci = _place()
    me = 4 * xi + 2 * yi + ci

    gathered = _all_gather("gather_weights", [w[n].astype(BF16) for n in FIRST])
    wts = {n: _assemble(n, blocks) for n, blocks in zip(FIRST, gathered)}
    cw_rows, cw_cols = w["conv_w"].shape
    (cw_all,) = _all_gather("gather_conv_w", [jnp.pad(w["conv_w"], ((0, SUBLANES - cw_rows), (0, 0)))])
    conv_w = cw_all[:, :cw_rows, :].transpose(1, 0, 2).reshape(cw_rows, N_DEV * cw_cols)

    small = {n: w[n] for n in SMALL}
    loss, grad_x, sg, d_conv_w, sent, recv = _local_step(
        x, p, positions, target, small, conv_w, wts, LATE, [w[n].astype(BF16) for n in LATE])

    sizes = [w[n].shape[1] for n in SMALL]
    sg_pack = _pack_small([sg[n] for n in SMALL] + [loss], d_conv_w)
    (sg_all,) = _all_gather("gather_small_grads", [sg_pack])
    (sg_sum,) = _rowwise("sum_small_grads", lambda *a: functools.reduce(lambda s, t: s + t, a),
                         [sg_all[k] for k in range(N_DEV)], [], [(sg_pack.shape[1], F32)])
    sg_vecs, d_conv_w_sum = _unpack_small(sg_sum, sizes + [LANES], d_conv_w.shape[1])
    loss = sg_vecs[-1][0, 0]
    grads = dict(zip(SMALL, sg_vecs[:-1]))
    grads["conv_w"] = lax.dynamic_slice_in_dim(d_conv_w_sum, me * cw_cols, cw_cols, axis=1)

    def sum8_then_adamw(wv, mv, vv, own, *others):
        g = functools.reduce(lambda s, t: s + t, others, own)
        return (g,) + _adamw(wv, g, mv, vv)

    delta, new_m, new_v = {}, {}, {}
    for n in BIG:
        cols = w[n].shape[1]
        own = lax.dynamic_index_in_dim(sent[n], me, axis=0, keepdims=False)
        grads[n], delta[n], new_m[n], new_v[n] = _rowwise(
            "adamw_" + n, sum8_then_adamw, [w[n], m[n], v[n], own] + [recv[n][k] for k in range(N_DEV - 1)], [],
            [(cols, F32)] * 4)
    packed = [_pack_small([d[n] for n in SMALL], d["conv_w"]) for d in (w, grads, m, v)]
    outs = _rowwise("adamw_small", _adamw, packed, [], [(packed[0].shape[1], F32)] * 3)
    for d, o in zip((delta, new_m, new_v), outs):
        vecs, mat = _unpack_small(o, sizes, cw_cols)
        d.update(zip(SMALL, vecs))
        d["conv_w"] = mat
    return loss, grad_x, grads, delta, new_m, new_v


def kernel(x, p, positions, mix_norm_pre, mix_norm_post, w_in, q_norm, w_uq, kv_norm, w_ukv, conv_w, conv_b, dt_bias, a_log, d_skip, ssm_norm, w_attn_o, w_ssm_o, w_out, ffn_norm_pre, ffn_norm_post, w_gate, w_up, w_down, ple_norm_pre, ple_norm_post, w_ple_gate, w_ple, loss_target, m_mix_norm_pre, m_mix_norm_post, m_w_in, m_q_norm, m_w_uq, m_kv_norm, m_w_ukv, m_conv_w, m_conv_b, m_dt_bias, m_a_log, m_d_skip, m_ssm_norm, m_w_attn_o, m_w_ssm_o, m_w_out, m_ffn_norm_pre, m_ffn_norm_post, m_w_gate, m_w_up, m_w_down, m_ple_norm_pre, m_ple_norm_post, m_w_ple_gate, m_w_ple, v_mix_norm_pre, v_mix_norm_post, v_w_in, v_q_norm, v_w_uq, v_kv_norm, v_w_ukv, v_conv_w, v_conv_b, v_dt_bias, v_a_log, v_d_skip, v_ssm_norm, v_w_attn_o, v_w_ssm_o, v_w_out, v_ffn_norm_pre, v_ffn_norm_post, v_w_gate, v_w_up, v_w_down, v_ple_norm_pre, v_ple_norm_post, v_w_ple_gate, v_w_ple):
    w_args = (mix_norm_pre, mix_norm_post, w_in, q_norm, w_uq, kv_norm, w_ukv, conv_w, conv_b, dt_bias, a_log, d_skip, ssm_norm, w_attn_o, w_ssm_o, w_out, ffn_norm_pre, ffn_norm_post, w_gate, w_up, w_down, ple_norm_pre, ple_norm_post, w_ple_gate, w_ple)
    m_args = (m_mix_norm_pre, m_mix_norm_post, m_w_in, m_q_norm, m_w_uq, m_kv_norm, m_w_ukv, m_conv_w, m_conv_b, m_dt_bias, m_a_log, m_d_skip, m_ssm_norm, m_w_attn_o, m_w_ssm_o, m_w_out, m_ffn_norm_pre, m_ffn_norm_post, m_w_gate, m_w_up, m_w_down, m_ple_norm_pre, m_ple_norm_post, m_w_ple_gate, m_w_ple)
    v_args = (v_mix_norm_pre, v_mix_norm_post, v_w_in, v_q_norm, v_w_uq, v_kv_norm, v_w_ukv, v_conv_w, v_conv_b, v_dt_bias, v_a_log, v_d_skip, v_ssm_norm, v_w_attn_o, v_w_ssm_o, v_w_out, v_ffn_norm_pre, v_ffn_norm_post, v_w_gate, v_w_up, v_w_down, v_ple_norm_pre, v_ple_norm_post, v_w_ple_gate, v_w_ple)

    def drop_layer(a):
        return a if a.ndim == 2 else a[0]

    w = {n: drop_layer(a) for n, a in zip(WEIGHTS, w_args)}
    m = {n: drop_layer(a) for n, a in zip(WEIGHTS, m_args)}
    v = {n: drop_layer(a) for n, a in zip(WEIGHTS, v_args)}
    loss, grad_x, grads, delta, new_m, new_v = _step(x[0], p[0, 0], positions[0], loss_target[0], w, m, v)
    like = lambda d: [d[n].reshape(a.shape) for n, a in zip(WEIGHTS, w_args)]
    return (loss, grad_x[None], *like(grads), *like(delta), *like(new_m), *like(new_v))
```

```python
import functools

import jax
import jax.numpy as jnp
from jax import lax
from jax.experimental import pallas as pl
from jax.experimental.pallas import tpu as pltpu

F32 = jnp.float32
BF16 = jnp.bfloat16

EPS = 1e-6
QK_NOPE = 128
QK_ROPE = 64
V_DIM = 128
QK_PAD = 256
ROPE_THETA = 10000.0
SSM_HEADDIM = 64
D_STATE = 128
CONV_WIDTH = 4
CHUNK = 256
ADAM_LR = 0.001
ADAM_B1 = 0.9
ADAM_B2 = 0.999
ADAM_EPS = 1e-08
ADAM_WD = 0.01
ADAM_STEP = 10

N_DEV = 8
LANES = 128
SUBLANES = 8
VMEM_LIMIT = 56 * 1024 * 1024
ROW_TILE_BYTES = 16 * 1024 * 1024
STRIP_ROWS = 16
STRIP_ELEMS = 64 * 1024
FLASH_T = 512
MM_TILE_BYTES = 20 * 1024 * 1024
MESH = pl.DeviceIdType.MESH


def _pick(dim, prefs):
    if dim <= prefs[0]:
        return dim
    for p in prefs:
        if dim % p == 0:
            return p
    return dim


def _tile(dim, cap):
    if dim <= cap:
        return dim
    best = None
    for t in range(LANES, cap + 1, LANES):
        if dim % t == 0:
            best = t
    return best if best is not None else dim


def _params(sem):
    return pltpu.CompilerParams(dimension_semantics=sem, vmem_limit_bytes=VMEM_LIMIT)


def _dot(a, b):
    return lax.dot_general(a, b, (((1,), (0,)), ((), ())), preferred_element_type=F32)


def _dot_nt(a, b):
    return lax.dot_general(a, b, (((1,), (1,)), ((), ())), preferred_element_type=F32)


def _dot_tn(a, b):
    return lax.dot_general(a, b, (((0,), (0,)), ((), ())), preferred_element_type=F32)


def _mm(name, a, b, mode, out_dtype=F32, acc_in=None, carry=None):
    if mode == "nn":
        (M, K), (K2, N) = a.shape, b.shape
    elif mode == "nt":
        (M, K), (N, K2) = a.shape, b.shape
    else:
        (K, M), (K2, N) = a.shape, b.shape
    assert K == K2, (name, a.shape, b.shape, mode)
    tm = _tile(M, 1024)
    tn = _tile(N, 1024 if acc_in is not None else 1536)
    tk = _tile(K, 2048)
    while tk > 512 and 2 * (tm * tk * a.dtype.itemsize + tk * tn * b.dtype.itemsize) > MM_TILE_BYTES:
        tk = _tile(K, tk - LANES)
    nk = K // tk
    dot = {"nn": _dot, "nt": _dot_nt, "tn": _dot_tn}[mode]
    has_acc = acc_in is not None

    def body(*refs):
        if has_acc:
            a_ref, b_ref, c_ref, o_ref, acc = refs
        else:
            a_ref, b_ref, o_ref, acc = refs
        k = pl.program_id(2)

        @pl.when(k == 0)
        def _():
            acc[...] = jnp.zeros_like(acc)

        acc[...] += dot(a_ref[...].astype(BF16), b_ref[...].astype(BF16))

        @pl.when(k == nk - 1)
        def _():
            r = acc[...]
            if has_acc:
                r = r + c_ref[...]
            o_ref[...] = r.astype(o_ref.dtype)

    if mode == "tn":
        a_spec = pl.BlockSpec((tk, tm), lambda i, j, k: (k, i))
    else:
        a_spec = pl.BlockSpec((tm, tk), lambda i, j, k: (i, k))
    if mode == "nt":
        b_spec = pl.BlockSpec((tn, tk), lambda i, j, k: (j, k))
    else:
        b_spec = pl.BlockSpec((tk, tn), lambda i, j, k: (k, j))
    o_spec = pl.BlockSpec((tm, tn), lambda i, j, k: (i, j))
    in_specs = [a_spec, b_spec] + ([o_spec] if has_acc else [])
    args = (a, b) + ((acc_in,) if has_acc else ())
    if carry is None:
        return pl.pallas_call(
            body, name=name, grid=(M // tm, N // tn, nk), in_specs=in_specs, out_specs=o_spec,
            out_shape=jax.ShapeDtypeStruct((M, N), out_dtype), scratch_shapes=[pltpu.VMEM((tm, tn), F32)],
            input_output_aliases=({2: 0} if has_acc and out_dtype == F32 else {}),
            compiler_params=_params(("parallel", "parallel", "arbitrary")),
        )(*args)
    grid = (M // tm, N // tn, nk)
    c_args, c_in_specs, c_out_specs, c_out_shapes, c_sems = _carry_call(carry)
    res = pl.pallas_call(
        _carried(body, len(args), 1, 1, carry, grid), name=name, grid=grid, in_specs=in_specs + c_in_specs,
        out_specs=[o_spec] + c_out_specs, out_shape=[jax.ShapeDtypeStruct((M, N), out_dtype)] + c_out_shapes,
        scratch_shapes=[pltpu.VMEM((tm, tn), F32)] + c_sems,
        compiler_params=_params(("arbitrary", "arbitrary", "arbitrary")),
    )(*args, *c_args)
    return res[0], list(res[1:])


def _row_tile(n_rows, bytes_per_row):
    tr = 1024
    while tr > SUBLANES and tr * bytes_per_row > ROW_TILE_BYTES:
        tr //= 2
    while n_rows % tr:
        tr //= 2
    return tr


def _strip_rows(width):
    return max(STRIP_ROWS, min(4 * STRIP_ROWS, STRIP_ELEMS // width // STRIP_ROWS * STRIP_ROWS))


def _over_strips(tr, strip, work):
    if strip is None or tr <= strip or tr % strip:
        work(slice(None))
        return

    def one(s, carry):
        work(pl.ds(pl.multiple_of(s * strip, strip), strip))
        return carry

    lax.fori_loop(0, tr // strip, one, 0)


def _acc_add(a_ref, v):
    if v.shape[0] == 1:
        a_ref[0:1, :] += v
    else:
        a_ref[...] += v.reshape(v.shape[0] // SUBLANES, SUBLANES, v.shape[1]).sum(axis=0)


def _rowwise(name, fn, rows, bcs, outs, accs=(), tr=None):
    n_rows = rows[0].shape[0]
    strip = _strip_rows(max(r.shape[1] for r in rows)) if tr is None else None
    if tr is None:
        per_row = sum(r.shape[1] * r.dtype.itemsize for r in rows) + sum(w * jnp.dtype(d).itemsize for w, d in outs)
        tr = _row_tile(n_rows, per_row)
    n_r, n_b, n_o, n_a = len(rows), len(bcs), len(outs), len(accs)

    def body(*refs):
        r_refs, b_refs = refs[:n_r], refs[n_r: n_r + n_b]
        o_refs = refs[n_r + n_b: n_r + n_b + n_o]
        a_refs = refs[n_r + n_b + n_o:]
        if n_a:
            @pl.when(pl.program_id(0) == 0)
            def _():
                for a in a_refs:
                    a[...] = jnp.zeros_like(a)

        def work(rws):
            ins = [r[rws, :].astype(F32) for r in r_refs] + [b[...].astype(F32) for b in b_refs]
            res = fn(*ins)
            res = res if isinstance(res, (tuple, list)) else (res,)
            for o, v in zip(o_refs, res[:n_o]):
                o[rws, :] = v.astype(o.dtype)
            for a, v in zip(a_refs, res[n_o:]):
                _acc_add(a, v)

        _over_strips(tr, strip, work)

    in_specs = [pl.BlockSpec((tr, r.shape[1]), lambda i: (i, 0)) for r in rows]
    in_specs += [pl.BlockSpec((1, b.shape[1]), lambda i: (0, 0)) for b in bcs]
    out_specs = [pl.BlockSpec((tr, w), lambda i: (i, 0)) for w, _ in outs]
    out_specs += [pl.BlockSpec((SUBLANES, w), lambda i: (0, 0)) for w in accs]
    out_shape = [jax.ShapeDtypeStruct((n_rows, w), d) for w, d in outs]
    out_shape += [jax.ShapeDtypeStruct((SUBLANES, w), F32) for w in accs]
    res = pl.pallas_call(
        body, name=name, grid=(n_rows // tr,), in_specs=in_specs, out_specs=out_specs, out_shape=out_shape,
        compiler_params=_params(("arbitrary",) if n_a else ("parallel",)),
    )(*rows, *bcs)
    return tuple(res)


def _rowwise_bwd(name, fn, rows, bcs, cts, need_rows, row_dtypes, need_bcs=None, adds=None, fwd_sums=(), tr=None,
                 into=None):
    n_rows = rows[0].shape[0]
    adds = adds or {}
    into = into or {}
    into_keys = sorted(into)
    need_bcs = list(range(len(bcs))) if need_bcs is None else list(need_bcs)
    ct_arrays = [c for c in cts if not isinstance(c, float)]
    add_keys = sorted(adds)
    add_arrays = [adds[k] for k in add_keys]
    strip = _strip_rows(max(r.shape[1] for r in rows)) if tr is None else None
    if tr is None:
        per_row = sum(r.shape[1] * r.dtype.itemsize for r in list(rows) + ct_arrays + add_arrays)
        per_row += sum(rows[i].shape[1] * jnp.dtype(d).itemsize for i, d in zip(need_rows, row_dtypes))
        tr = _row_tile(n_rows, per_row)
    n_r, n_b, n_c, n_ad = len(rows), len(bcs), len(ct_arrays), len(add_arrays)
    n_go, n_gb, n_fs = len(need_rows), len(need_bcs), len(fwd_sums)

    def body(*refs):
        pos = [0]

        def take(n):
            pos[0] += n
            return refs[pos[0] - n: pos[0]]

        r_refs, b_refs, c_refs, ad_refs = take(n_r), take(n_b), take(n_c), take(n_ad)
        take(len(into_keys))
        go_refs = take(n_go)
        acc_refs = refs[pos[0]:]

        @pl.when(pl.program_id(0) == 0)
        def _():
            for a in acc_refs:
                a[...] = jnp.zeros_like(a)

        def wrapped(*a):
            r = fn(*a)
            return tuple(r) if isinstance(r, (tuple, list)) else (r,)

        def work(rws):
            r_t = [r[rws, :].astype(F32) for r in r_refs]
            b_t = [r[...].astype(F32) for r in b_refs]
            outs, vjp = jax.vjp(wrapped, *r_t, *b_t)
            it = iter(c_refs)
            full = tuple(jnp.full(o.shape, c, F32) if isinstance(c, float) else next(it)[rws, :].astype(F32)
                         for o, c in zip(outs, cts))
            grads = vjp(full)
            for o_ref, i in zip(go_refs, need_rows):
                g = grads[i]
                if i in adds:
                    g = g + ad_refs[add_keys.index(i)][rws, :].astype(F32)
                o_ref[rws, :] = g.astype(o_ref.dtype)
            for a, j in zip(acc_refs[:n_gb], need_bcs):
                _acc_add(a, grads[n_r + j])
            for a, j in zip(acc_refs[n_gb:], fwd_sums):
                a[0:1, :] += jnp.full((1, LANES), jnp.sum(outs[j]), F32)

        _over_strips(tr, strip, work)

    def row_spec(w):
        return pl.BlockSpec((tr, w), lambda i: (i, 0))

    in_specs = [row_spec(r.shape[1]) for r in rows]
    in_specs += [pl.BlockSpec((1, b.shape[1]), lambda i: (0, 0)) for b in bcs]
    in_specs += [row_spec(c.shape[1]) for c in ct_arrays] + [row_spec(a.shape[1]) for a in add_arrays]
    in_specs += [pl.BlockSpec(memory_space=pl.ANY) for _ in into_keys]
    out_specs = [row_spec(rows[i].shape[1]) for i in need_rows]
    out_shape = [jax.ShapeDtypeStruct((n_rows, rows[i].shape[1]), d) for i, d in zip(need_rows, row_dtypes)]
    aliases = {}
    for pos_in, k in enumerate(into_keys):
        buf, col_block = into[k]
        out_specs[k] = pl.BlockSpec((tr, rows[need_rows[k]].shape[1]), lambda i, cb=col_block: (i, cb))
        out_shape[k] = jax.ShapeDtypeStruct(buf.shape, buf.dtype)
        aliases[len(in_specs) - len(into_keys) + pos_in] = k
    out_specs += [pl.BlockSpec((SUBLANES, bcs[j].shape[1]), lambda i: (0, 0)) for j in need_bcs]
    out_specs += [pl.BlockSpec((SUBLANES, LANES), lambda i: (0, 0)) for _ in fwd_sums]
    out_shape += [jax.ShapeDtypeStruct((SUBLANES, bcs[j].shape[1]), F32) for j in need_bcs]
    out_shape += [jax.ShapeDtypeStruct((SUBLANES, LANES), F32) for _ in fwd_sums]
    res = pl.pallas_call(
        body, name=name, grid=(n_rows // tr,), in_specs=in_specs, out_specs=out_specs, out_shape=out_shape,
        input_output_aliases=aliases, compiler_params=_params(("arbitrary",)),
    )(*rows, *bcs, *ct_arrays, *add_arrays, *[into[k][0] for k in into_keys])
    return tuple(res)


def _fold(acc):
    return jnp.sum(acc, axis=0, keepdims=True)


def _rms(x, w):
    return x * lax.rsqrt(jnp.mean(x * x, axis=-1, keepdims=True) + EPS) * w


def _sigmoid(x):
    return jax.nn.sigmoid(x)


def _silu(x):
    return x * _sigmoid(x)


def _log1p(u):
    series = u * (1.0 - u * (0.5 - u * (1.0 / 3.0 - u * 0.25)))
    return jnp.where(u < 0.01, series, jnp.log(1.0 + u))


def _softplus(x):
    return jnp.maximum(x, 0.0) + _log1p(jnp.exp(-jnp.abs(x)))


def _st_pre(x, w):
    return _rms(x, w)


def _st_lora_norms(dq):
    def fn(cqkv, qn, kvn):
        return _rms(cqkv[:, :dq], qn), _rms(cqkv[:, dq:], kvn)
    return fn


def _st_gated_norm(n_groups):
    def fn(y, xs, z, dskip, wn):
        yz = (y + dskip * xs) * _silu(z)
        gw = yz.shape[1] // n_groups
        parts = [_rms(yz[:, g * gw:(g + 1) * gw], wn[:, g * gw:(g + 1) * gw]) for g in range(n_groups)]
        return jnp.concatenate(parts, axis=1)
    return fn


def _st_mix(d):
    def fn(g, ao, so):
        return _sigmoid(g[:, :d]) * ao + _sigmoid(g[:, d:]) * so
    return fn


def _st_res_norm(h, y, w_post, w_pre):
    h2 = h + _rms(y, w_post)
    return h2, _rms(h2, w_pre)


def _st_swiglu(gt, up):
    return _silu(gt) * up


def _st_loss(pe, gl, h2, tgt, w_post):
    e = pe * _sigmoid(gl)
    diff = h2 + _rms(e, w_post) - tgt
    return 0.5 * jnp.mean(diff * diff, axis=-1, keepdims=True)


def _rope_tables(positions):
    half = QK_ROPE // 2
    inv_freq = ROPE_THETA ** (-jnp.arange(0, QK_ROPE, 2, dtype=F32) / QK_ROPE)
    ang = positions.astype(F32).reshape(-1, 1) * inv_freq
    cos, sin = jnp.cos(ang), jnp.sin(ang)
    n = ang.shape[0]
    z = lambda w: jnp.zeros((n, w), F32)
    c_tab = jnp.concatenate([jnp.ones((n, QK_NOPE), F32), cos, cos, z(QK_PAD - QK_NOPE - QK_ROPE)], axis=1)
    a_tab = jnp.concatenate([z(QK_NOPE), -sin, z(half), z(QK_PAD - QK_NOPE - QK_ROPE)], axis=1)
    b_tab = jnp.concatenate([z(QK_NOPE), z(half), sin, z(QK_PAD - QK_NOPE - QK_ROPE)], axis=1)
    return c_tab, a_tab, b_tab


def _rot(x, c, a, b):
    half = QK_ROPE // 2
    return x * c + pltpu.roll(x, QK_PAD - half, axis=1) * a + pltpu.roll(x, half, axis=1) * b


def _rot_t(g, c, a, b):
    half = QK_ROPE // 2
    return g * c + pltpu.roll(g * a, half, axis=1) + pltpu.roll(g * b, QK_PAD - half, axis=1)


def _st_rope(n_heads):
    def fn(qraw, kraw, sm, c, a, b):
        kpe = _rot(sm[:, :QK_PAD], c, a, b)
        scale = float(QK_NOPE + QK_ROPE) ** -0.5
        q = [_rot(qraw[:, h * QK_PAD:(h + 1) * QK_PAD], c, a, b) * scale for h in range(n_heads)]
        k = [kraw[:, h * QK_PAD:(h + 1) * QK_PAD] + kpe for h in range(n_heads)]
        return jnp.concatenate(q, axis=1), jnp.concatenate(k, axis=1)
    return fn


def _st_rope_bwd(n_heads):
    def fn(dq, dk, c, a, b):
        dqraw = [_rot_t(dq[:, h * QK_PAD:(h + 1) * QK_PAD], c, a, b) for h in range(n_heads)]
        dks = dk[:, :QK_PAD]
        for h in range(1, n_heads):
            dks = dks + dk[:, h * QK_PAD:(h + 1) * QK_PAD]
        return jnp.concatenate(dqraw, axis=1), _rot_t(dks, c, a, b), dk
    return fn


def _split3(x):
    h1 = x.astype(BF16)
    r1 = x - h1.astype(F32)
    h2 = r1.astype(BF16)
    h3 = (r1 - h2.astype(F32)).astype(BF16)
    return h1, h2, h3


def _tri_dot(tri, x):
    h1, h2, h3 = _split3(x)
    return (_dot(tri, h3) + _dot(tri, h2)) + _dot(tri, h1)


def _dot_tri(x, tri):
    h1, h2, h3 = _split3(x)
    return (_dot(h3, tri) + _dot(h2, tri)) + _dot(h1, tri)


def _st_dt(sm, bias, alog):
    x = sm[:, QK_PAD:] + bias
    dt = _softplus(x)
    return dt, dt * (-jnp.exp(alog))


def _st_dt_bwd(sm, ddt, dcum, bias, alog):
    n = sm.shape[0]
    i = lax.broadcasted_iota(jnp.int32, (n, n), 0)
    j = lax.broadcasted_iota(jnp.int32, (n, n), 1)
    upper = (j >= i).astype(BF16)
    dda = _tri_dot(upper, dcum)
    x = sm[:, QK_PAD:] + bias
    dt = _softplus(x)
    a = -jnp.exp(alog)
    draw = (ddt + dda * a) * _sigmoid(x)
    return draw, draw, dda * dt * a


def _conv_fwd(xbc, w, b):
    S, C = xbc.shape
    tr = _pick(S, (512, 256))
    tc = _pick(C, (1024, 512, 256, 128))
    hb = tr // SUBLANES

    def body(x_ref, halo_ref, w_ref, b_ref, c_ref, a_ref, ext):
        i = pl.program_id(1)
        halo = jnp.where(i == 0, 0.0, halo_ref[...])
        ext[0:SUBLANES, :] = halo
        ext[SUBLANES:, :] = x_ref[...]

        def lane_strip(c, carry):
            lanes = pl.ds(pl.multiple_of(c * LANES, LANES), LANES)
            wv = w_ref[:, lanes]
            acc = b_ref[:, lanes] + wv[CONV_WIDTH - 1:CONV_WIDTH, :] * ext[pl.ds(SUBLANES, tr), lanes]
            for k in range(CONV_WIDTH - 1):
                off = SUBLANES - (CONV_WIDTH - 1) + k
                acc = acc + wv[k:k + 1, :] * ext[pl.ds(off, tr), lanes]
            c_ref[:, lanes] = acc
            a_ref[:, lanes] = _silu(acc)
            return carry

        lax.fori_loop(0, tc // LANES, lane_strip, 0)

    return pl.pallas_call(
        body, name="conv_fwd", grid=(C // tc, S // tr),
        in_specs=[pl.BlockSpec((tr, tc), lambda j, i: (i, j)),
                  pl.BlockSpec((SUBLANES, tc), lambda j, i: (jnp.maximum(i * hb - 1, 0), j)),
                  pl.BlockSpec((CONV_WIDTH, tc), lambda j, i: (0, j)),
                  pl.BlockSpec((1, tc), lambda j, i: (0, j))],
        out_specs=[pl.BlockSpec((tr, tc), lambda j, i: (i, j))] * 2,
        out_shape=[jax.ShapeDtypeStruct((S, C), F32)] * 2,
        scratch_shapes=[pltpu.VMEM((tr + SUBLANES, tc), F32)],
        compiler_params=_params(("parallel", "arbitrary")),
    )(xbc, xbc, w, b)


def _conv_bwd(xbc, dconv, w, buf, col0):
    S, C = xbc.shape
    tr = _pick(S, (512, 256))
    tc = _pick(C, (1024, 512, 256, 128))
    hb = tr // SUBLANES
    n_i = S // tr
    assert col0 % tc == 0, (col0, tc)
    cb0 = col0 // tc

    def body(x_ref, halo_ref, d_ref, dnext_ref, w_ref, buf_ref, dx_ref, dw_ref, ext, dext):
        i = pl.program_id(1)
        ext[0:SUBLANES, :] = jnp.where(i == 0, 0.0, halo_ref[...])
        ext[SUBLANES:, :] = x_ref[...]
        dext[0:tr, :] = d_ref[...]
        dext[tr:, :] = jnp.where(i == n_i - 1, 0.0, dnext_ref[...])

        @pl.when(i == 0)
        def _():
            dw_ref[...] = jnp.zeros_like(dw_ref)

        def lane_strip(c, carry):
            lanes = pl.ds(pl.multiple_of(c * LANES, LANES), LANES)
            wv = w_ref[:, lanes]
            d = dext[pl.ds(0, tr), lanes]
            dx = wv[CONV_WIDTH - 1:CONV_WIDTH, :] * d
            for k in range(CONV_WIDTH):
                if k < CONV_WIDTH - 1:
                    dx = dx + wv[k:k + 1, :] * dext[pl.ds(CONV_WIDTH - 1 - k, tr), lanes]
                xs = ext[pl.ds(SUBLANES - (CONV_WIDTH - 1) + k, tr), lanes]
                prod = d * xs
                dw_ref[k * SUBLANES:(k + 1) * SUBLANES, lanes] += prod.reshape(tr // SUBLANES, SUBLANES, LANES).sum(axis=0)
            dx_ref[:, lanes] = dx.astype(dx_ref.dtype)
            return carry

        lax.fori_loop(0, tc // LANES, lane_strip, 0)

    return pl.pallas_call(
        body, name="conv_bwd", grid=(C // tc, n_i),
        in_specs=[pl.BlockSpec((tr, tc), lambda j, i: (i, j)),
                  pl.BlockSpec((SUBLANES, tc), lambda j, i: (jnp.maximum(i * hb - 1, 0), j)),
                  pl.BlockSpec((tr, tc), lambda j, i: (i, j)),
                  pl.BlockSpec((SUBLANES, tc), lambda j, i: (jnp.minimum((i + 1) * hb, S // SUBLANES - 1), j)),
                  pl.BlockSpec((CONV_WIDTH, tc), lambda j, i: (0, j)),
                  pl.BlockSpec(memory_space=pl.ANY)],
        out_specs=[pl.BlockSpec((tr, tc), lambda j, i: (i, cb0 + j)),
                   pl.BlockSpec((CONV_WIDTH * SUBLANES, tc), lambda j, i: (0, j))],
        out_shape=[jax.ShapeDtypeStruct(buf.shape, buf.dtype), jax.ShapeDtypeStruct((CONV_WIDTH * SUBLANES, C), F32)],
        scratch_shapes=[pltpu.VMEM((tr + SUBLANES, tc), F32), pltpu.VMEM((tr + SUBLANES, tc), F32)],
        input_output_aliases={5: 0},
        compiler_params=_params(("parallel", "arbitrary")),
    )(xbc, xbc, dconv, dconv, w, buf)


def _st_dconv(d_inner):
    def fn(xc, dxa, dxb, db_, dc_):
        s = _sigmoid(xc)
        g = jnp.concatenate([dxa + dxb, db_, dc_], axis=1) * (s * (1.0 + xc * (1.0 - s)))
        return g, g
    return fn


def _chunk_setup_t(b_ref, c_ref, dac_ref, dar_ref, L):
    ii = lax.broadcasted_iota(jnp.int32, (L, L), 0)
    jj = lax.broadcasted_iota(jnp.int32, (L, L), 1)
    lower = ii >= jj
    upper = ii <= jj
    cum_c = _tri_dot(lower.astype(BF16), dac_ref[0])
    cum_r = _dot_tri(dar_ref[0], upper.astype(BF16))
    bm = b_ref[...].astype(BF16)
    cm = c_ref[...].astype(BF16)
    return lower, upper, cum_c, cum_r, bm, cm


def _ssd_specs_t(d_inner, n_groups, gw, L, rp, chunk_of):
    bb0 = d_inner // D_STATE
    cb0 = bb0 + n_groups
    return [pl.BlockSpec((L, gw), lambda g, c: (chunk_of(c), g)),
            pl.BlockSpec((L, D_STATE), lambda g, c: (chunk_of(c), bb0 + g)),
            pl.BlockSpec((L, D_STATE), lambda g, c: (chunk_of(c), cb0 + g)),
            pl.BlockSpec((1, rp, L), lambda g, c: (g, 0, chunk_of(c))),
            pl.BlockSpec((1, L, LANES), lambda g, c: (g, chunk_of(c), 0)),
            pl.BlockSpec((1, rp, L), lambda g, c: (g, 0, chunk_of(c)))]


def _ssd_fwd_t(xbc_a, dt_row, da_col, da_row, d_inner, n_groups, R):
    S = xbc_a.shape[0]
    L = min(CHUNK, S)
    NC = S // L
    P, N = SSM_HEADDIM, D_STATE
    gw = R * P
    rp = da_row.shape[1]

    def body(x_ref, b_ref, c_ref, dt_ref, dac_ref, dar_ref, y_ref, st_ref, state, y_t):
        @pl.when(pl.program_id(1) == 0)
        def _():
            state[...] = jnp.zeros_like(state)

        st_ref[0, 0] = state[...]
        lower, upper, cum_c, cum_r, bm, cm = _chunk_setup_t(b_ref, c_ref, dac_ref, dar_ref, L)
        gm_t = _dot_nt(bm, cm)
        dt = dt_ref[0]
        x_t = x_ref[...].T
        for r in range(R):
            cc = cum_c[:, r:r + 1]
            cr = cum_r[r:r + 1, :]
            m_t = (gm_t * jnp.exp(jnp.where(upper, cr - cc, -jnp.inf))).astype(BF16)
            x = x_t[r * P:(r + 1) * P, :] * dt[r:r + 1, :]
            s_r = state[r * P:(r + 1) * P, :]
            y_off = _dot_nt(s_r.astype(BF16), cm) * jnp.exp(cr)
            y_t[r * P:(r + 1) * P, :] = _dot(x.astype(BF16), m_t) + y_off
            last = cr[:, L - 1:L]
            xw = (x * jnp.exp(last - cr)).astype(BF16)
            state[r * P:(r + 1) * P, :] = s_r * jnp.exp(last) + _dot(xw, bm)
        y_ref[...] = y_t[...].T

    return pl.pallas_call(
        body, name="ssd_fwd", grid=(n_groups, NC),
        in_specs=_ssd_specs_t(d_inner, n_groups, gw, L, rp, lambda c: c),
        out_specs=[pl.BlockSpec((L, gw), lambda g, c: (c, g)),
                   pl.BlockSpec((1, 1, gw, N), lambda g, c: (g, c, 0, 0))],
        out_shape=[jax.ShapeDtypeStruct((S, d_inner), F32), jax.ShapeDtypeStruct((n_groups, NC, gw, N), F32)],
        scratch_shapes=[pltpu.VMEM((gw, N), F32), pltpu.VMEM((gw, L), F32)],
        compiler_params=_params(("parallel", "arbitrary")),
    )(xbc_a, xbc_a, xbc_a, dt_row, da_col, da_row)


def _ssd_bwd_t(xbc_a, dt_row, da_col, da_row, states, dy, d_inner, n_groups, R):
    S = xbc_a.shape[0]
    L = min(CHUNK, S)
    NC = S // L
    P, N = SSM_HEADDIM, D_STATE
    gw = R * P
    rp = da_row.shape[1]
    rev = lambda c: NC - 1 - c

    def body(x_ref, b_ref, c_ref, dt_ref, dac_ref, dar_ref, st_ref, dy_ref,
             dx_ref, db_ref, dc_ref, ddt_ref, dcum_ref, dstate, dx_t):
        @pl.when(pl.program_id(1) == 0)
        def _():
            dstate[...] = jnp.zeros_like(dstate)

        lower, upper, cum_c, cum_r, bm, cm = _chunk_setup_t(b_ref, c_ref, dac_ref, dar_ref, L)
        gm = _dot_nt(cm, bm)
        gm_t = _dot_nt(bm, cm)
        dt = dt_ref[0]
        x_t = x_ref[...].T
        dy_t = dy_ref[...].T
        sub = lax.broadcasted_iota(jnp.int32, (rp, L), 0)
        is_last = lax.broadcasted_iota(jnp.int32, (1, L), 1) == L - 1
        d_g = jnp.zeros((L, L), F32)
        d_g_t = jnp.zeros((L, L), F32)
        dc_acc = jnp.zeros((L, N), F32)
        db_acc = jnp.zeros((L, N), F32)
        ddt_out = jnp.zeros((rp, L), F32)
        dcum_out = jnp.zeros((rp, L), F32)
        for r in range(R):
            cc = jnp.broadcast_to(cum_c[:, r:r + 1], (L, L))
            cr = cum_r[r:r + 1, :]
            lam = jnp.exp(jnp.where(lower, cc - cum_r[r:r + 1, :], -jnp.inf))
            lam_t = jnp.exp(jnp.where(upper, cr - cc, -jnp.inf))
            m = gm * lam
            m_t = gm_t * lam_t
            dtr = dt[r:r + 1, :]
            xh = x_t[r * P:(r + 1) * P, :]
            x = xh * dtr
            xb = x.astype(BF16)
            d_y = dy_t[r * P:(r + 1) * P, :]
            d_yb = d_y.astype(BF16)
            s_r = st_ref[0, 0, r * P:(r + 1) * P, :]
            s_rb = s_r.astype(BF16)
            ds_n = dstate[r * P:(r + 1) * P, :]
            ds_nb = ds_n.astype(BF16)
            e = jnp.exp(cr)
            last = cr[:, L - 1:L]
            e_last = jnp.exp(last)
            w = jnp.exp(last - cr)
            d_x = _dot(d_yb, m.astype(BF16))
            d_m = _dot_tn(d_yb, xb)
            d_m_t = _dot_tn(xb, d_yb)
            d_ye = (d_y * e).astype(BF16)
            dc_acc = dc_acc + _dot_tn(d_ye, s_rb)
            ds_part = _dot(d_ye, cm)
            y_off = _dot_nt(s_rb, cm) * e
            dcum = jnp.sum(d_y * y_off, axis=0, keepdims=True)
            d_xw = _dot_nt(ds_nb, bm)
            d_x = d_x + d_xw * w
            dw_w = jnp.sum(d_xw * x, axis=0, keepdims=True) * w
            db_acc = db_acc + _dot_tn((x * w).astype(BF16), ds_nb)
            d_last = jnp.sum(ds_n * s_r, keepdims=True) * e_last + jnp.sum(dw_w, keepdims=True)
            dstate[r * P:(r + 1) * P, :] = e_last * ds_n + ds_part
            d_g = d_g + d_m * lam
            d_g_t = d_g_t + d_m_t * lam_t
            dcum = (dcum - dw_w + jnp.sum(d_m_t * m_t, axis=0, keepdims=True)
                    - jnp.sum(d_m * m, axis=0, keepdims=True) + jnp.where(is_last, d_last, 0.0))
            dx_t[r * P:(r + 1) * P, :] = d_x * dtr
            ddt = jnp.sum(d_x * xh, axis=0, keepdims=True)
            ddt_out = ddt_out + jnp.where(sub == r, ddt, 0.0)
            dcum_out = dcum_out + jnp.where(sub == r, dcum, 0.0)
        dc_ref[...] = dc_acc + _dot(d_g.astype(BF16), bm)
        db_ref[...] = db_acc + _dot(d_g_t.astype(BF16), cm)
        dx_ref[...] = dx_t[...].T
        ddt_ref[0] = ddt_out
        dcum_ref[0] = dcum_out

    gn = n_groups * N
    return pl.pallas_call(
        body, name="ssd_bwd", grid=(n_groups, NC),
        in_specs=_ssd_specs_t(d_inner, n_groups, gw, L, rp, rev) + [
            pl.BlockSpec((1, 1, gw, N), lambda g, c: (g, rev(c), 0, 0)),
            pl.BlockSpec((L, gw), lambda g, c: (rev(c), g))],
        out_specs=[pl.BlockSpec((L, gw), lambda g, c: (rev(c), g)),
                   pl.BlockSpec((L, N), lambda g, c: (rev(c), g)),
                   pl.BlockSpec((L, N), lambda g, c: (rev(c), g)),
                   pl.BlockSpec((1, rp, L), lambda g, c: (g, 0, rev(c))),
                   pl.BlockSpec((1, rp, L), lambda g, c: (g, 0, rev(c)))],
        out_shape=[jax.ShapeDtypeStruct((S, d_inner), F32), jax.ShapeDtypeStruct((S, gn), F32),
                   jax.ShapeDtypeStruct((S, gn), F32), jax.ShapeDtypeStruct((n_groups, rp, S), F32),
                   jax.ShapeDtypeStruct((n_groups, rp, S), F32)],
        scratch_shapes=[pltpu.VMEM((gw, N), F32), pltpu.VMEM((gw, L), F32)],
        compiler_params=_params(("parallel", "arbitrary")),
    )(xbc_a, xbc_a, xbc_a, dt_row, da_col, da_row, states, dy)


def _attn_scale():
    return float(QK_NOPE + QK_ROPE) ** -0.5


def _diag_mask(t):
    return lax.broadcasted_iota(jnp.int32, (t, t), 0) <= lax.broadcasted_iota(jnp.int32, (t, t), 1)


def _walk_wide(lo, hi, tile_step, joint=True):
    n = hi - lo

    def step(j, width):
        if joint:
            tile_step(j, width)
        else:
            for u in range(width):
                tile_step(j + u, 1)

    def quad(t, carry):
        step(lo + 4 * t, 4)
        return carry

    lax.fori_loop(0, n // 4, quad, 0)

    @pl.when(n % 4 >= 2)
    def _():
        step(hi - n % 4, 2)

    @pl.when(n % 2 == 1)
    def _():
        step(hi - 1, 1)


def _carried(main_body, n_in, n_out, n_scratch, carry, grid):
    if carry is None:
        return main_body
    n_ci, n_co = len(carry["ins"]), len(carry["outs"])

    def body(*refs):
        pos = [0]

        def take(n):
            pos[0] += n
            return refs[pos[0] - n: pos[0]]

        ins, c_ins, outs, c_outs, scratch, sems = take(n_in), take(n_ci), take(n_out), take(n_co), take(n_scratch), take(3)
        steps = [pl.program_id(a) for a in range(len(grid))]

        @pl.when(functools.reduce(jnp.logical_and, [s == 0 for s in steps]))
        def _():
            for cp in carry["copies"](c_ins, c_outs, *sems):
                cp.start()

        main_body(*ins, *outs, *scratch)

        @pl.when(functools.reduce(jnp.logical_and, [s == n - 1 for s, n in zip(steps, grid)]))
        def _():
            for cp in carry["copies"](c_ins, c_outs, *sems):
                cp.wait()

    return body


def _carry_call(carry):
    if carry is None:
        return [], [], [], [], []
    sems = [pltpu.SemaphoreType.DMA((carry["n_remote"],)), pltpu.SemaphoreType.DMA((carry["n_remote"],)),
            pltpu.SemaphoreType.DMA((max(carry["n_local"], 1),))]
    return (list(carry["ins"]), [_HBM] * len(carry["ins"]), [_HBM] * len(carry["outs"]), list(carry["outs"]), sems)


def _flash_fwd(q, k, v_t, n_heads, carry=None):
    S = q.shape[0]
    T = min(FLASH_T, S)
    grid = (n_heads, S // T)
    c_args, c_in_specs, c_out_specs, c_out_shapes, c_sems = _carry_call(carry)

    def body(q_ref, k_ref, vt_ref, o_ref, lse_ref, m_s, l_s, acc_t):
        i = pl.program_id(1)
        m_s[...] = jnp.full_like(m_s, -jnp.inf)
        l_s[...] = jnp.zeros_like(l_s)
        acc_t[...] = jnp.zeros_like(acc_t)
        qv = q_ref[...]

        def step(j, width, masked):
            keys = pl.ds(pl.multiple_of(j * T, T), width * T)
            s_t = _dot_nt(k_ref[keys, :], qv)
            if masked:
                s_t = jnp.where(_diag_mask(T), s_t, -jnp.inf)
            m_prev = m_s[...]
            m_new = jnp.maximum(m_prev, jnp.max(s_t, axis=0, keepdims=True))
            alpha = jnp.exp(m_prev - m_new)
            p_t = jnp.exp(s_t - m_new)
            l_s[...] = alpha * l_s[...] + jnp.sum(p_t, axis=0, keepdims=True)
            acc_t[...] = alpha * acc_t[...] + _dot(vt_ref[:, keys], p_t.astype(BF16))
            m_s[...] = m_new

        _walk_wide(0, i, lambda j, width: step(j, width, False))
        step(i, 1, True)
        o_ref[...] = (acc_t[...] / l_s[...]).T
        lse_ref[0] = m_s[...] + jnp.log(l_s[...])

    res = pl.pallas_call(
        _carried(body, 3, 2, 3, carry, grid), name="flash_fwd", grid=grid,
        in_specs=[pl.BlockSpec((T, QK_PAD), lambda h, i: (i, h)),
                  pl.BlockSpec((S, QK_PAD), lambda h, i: (0, h)),
                  pl.BlockSpec((V_DIM, S), lambda h, i: (h, 0))] + c_in_specs,
        out_specs=[pl.BlockSpec((T, V_DIM), lambda h, i: (i, h)),
                   pl.BlockSpec((1, 1, T), lambda h, i: (h, 0, i))] + c_out_specs,
        out_shape=[jax.ShapeDtypeStruct((S, n_heads * V_DIM), F32),
                   jax.ShapeDtypeStruct((n_heads, 1, S), F32)] + c_out_shapes,
        scratch_shapes=[pltpu.VMEM((1, T), F32), pltpu.VMEM((1, T), F32), pltpu.VMEM((V_DIM, T), F32)] + c_sems,
        compiler_params=_params(("arbitrary", "arbitrary")),
    )(q, k, v_t, *c_args)
    return res[0], res[1], list(res[2:])


def _flash_bwd(q, k, v, do, lse_row, delta_row, n_heads, carry=None):
    S = q.shape[0]
    T = min(FLASH_T, S)
    nq = S // T
    grid = (n_heads, S // T)
    c_args, c_in_specs, c_out_specs, c_out_shapes, c_sems = _carry_call(carry)

    def body(q_ref, k_ref, v_ref, do_ref, lse_ref, dl_ref, dq_ref, dk_ref, dv_ref, dk_acc, dv_acc):
        j = pl.program_id(1)

        @pl.when(j == 0)
        def _():
            dq_ref[...] = jnp.zeros_like(dq_ref)

        dk_acc[...] = jnp.zeros_like(dk_acc)
        dv_acc[...] = jnp.zeros_like(dv_acc)
        kv = k_ref[...]
        vv = v_ref[...]

        def step(i, width, masked):
            cols = pl.ds(pl.multiple_of(i * T, T), width * T)
            qt = q_ref[cols, :]
            dot = do_ref[cols, :]
            s_t = _dot_nt(kv, qt)
            if masked:
                s_t = jnp.where(_diag_mask(T), s_t, -jnp.inf)
            p_t = jnp.exp(s_t - lse_ref[0, :, cols])
            dv_acc[...] += _dot(p_t.astype(BF16), dot)
            ds_t = (p_t * (_dot_nt(vv, dot) - dl_ref[0, :, cols])).astype(BF16)
            dk_acc[...] += _dot(ds_t, qt)
            dq_ref[cols, :] += _dot_tn(ds_t, kv)

        step(j, 1, True)
        _walk_wide(j + 1, nq, lambda i, width: step(i, width, False), joint=False)
        dk_ref[...] = dk_acc[...]
        dv_ref[...] = dv_acc[...].astype(dv_ref.dtype)

        @pl.when(j == nq - 1)
        def _():
            dq_ref[...] = dq_ref[...] * _attn_scale()

    res = pl.pallas_call(
        _carried(body, 6, 3, 2, carry, grid), name="flash_bwd", grid=grid,
        in_specs=[pl.BlockSpec((S, QK_PAD), lambda h, j: (0, h)),
                  pl.BlockSpec((T, QK_PAD), lambda h, j: (j, h)),
                  pl.BlockSpec((T, V_DIM), lambda h, j: (j, h)),
                  pl.BlockSpec((S, V_DIM), lambda h, j: (0, h)),
                  pl.BlockSpec((1, 1, S), lambda h, j: (h, 0, 0)),
                  pl.BlockSpec((1, 1, S), lambda h, j: (h, 0, 0))] + c_in_specs,
        out_specs=[pl.BlockSpec((S, QK_PAD), lambda h, j: (0, h)),
                   pl.BlockSpec((T, QK_PAD), lambda h, j: (j, h)),
                   pl.BlockSpec((T, V_DIM), lambda h, j: (j, h))] + c_out_specs,
        out_shape=[jax.ShapeDtypeStruct((S, n_heads * QK_PAD), F32), jax.ShapeDtypeStruct((S, n_heads * QK_PAD), F32),
                   jax.ShapeDtypeStruct((S, n_heads * V_DIM), BF16)] + c_out_shapes,
        scratch_shapes=[pltpu.VMEM((T, QK_PAD), F32), pltpu.VMEM((T, V_DIM), F32)] + c_sems,
        compiler_params=_params(("arbitrary", "arbitrary")),
    )(q, k, v, do, lse_row, delta_row, *c_args)
    return res[0], res[1], res[2], list(res[3:])


def _st_delta(n_heads):
    def fn(do, o):
        prod = do * o
        lane = lax.broadcasted_iota(jnp.int32, (do.shape[0], LANES), 1)
        out = jnp.zeros((do.shape[0], LANES), F32)
        for h in range(n_heads):
            out = out + jnp.where(lane == h, jnp.sum(prod[:, h * V_DIM:(h + 1) * V_DIM], axis=1, keepdims=True), 0.0)
        return out
    return fn


def _pad_cols(w, width):
    return jnp.pad(w, ((0, 0), (0, width - w.shape[1])))


def _dims(x, p, q_norm, kv_norm, w_uq, dt_bias, ssm_norm, conv_b):
    d = dict(S=x.shape[0], D=x.shape[1], PLE=p.shape[1], DQ=q_norm.shape[1], DKV=kv_norm.shape[1],
             NH=w_uq.shape[1] // (QK_NOPE + QK_ROPE), NHS=dt_bias.shape[1], DI=ssm_norm.shape[1],
             CONV=conv_b.shape[1])
    d["G"] = (d["CONV"] - d["DI"]) // (2 * D_STATE)
    d["R"] = d["NHS"] // d["G"]
    return d


def _assemble(name, blocks):
    rows, cols = blocks.shape[1:]
    if name in COL_SHARDED:
        return blocks.transpose(1, 0, 2).reshape(rows, N_DEV * cols)
    return blocks.reshape(N_DEV * rows, cols)


def _by_device(name, g):
    if name in COL_SHARDED:
        return g.reshape(g.shape[0], N_DEV, g.shape[1] // N_DEV).transpose(1, 0, 2)
    return g.reshape(N_DEV, g.shape[0] // N_DEV, g.shape[1])


def _local_step(x, p, positions, target, small, conv_w, wts, late_names=(), late_shards=(), exchange=True):
    wts = dict(wts)
    dm = _dims(x, p, small["q_norm"], small["kv_norm"], wts["w_uq"], small["dt_bias"], small["ssm_norm"],
               small["conv_b"])
    S, D, DQ, DKV, NH, NHS, DI, CONV, G, R = (dm[k] for k in ("S", "D", "DQ", "DKV", "NH", "NHS", "DI", "CONV", "G", "R"))
    rp = -(-R // SUBLANES) * SUBLANES
    L = min(CHUNK, S)

    w_nat = wts["w_in"]
    o = [0]
    for n in (DQ, DKV, QK_ROPE, DI, CONV, NHS, D, D):
        o.append(o[-1] + n)
    pw = (DI, 2 * D, CONV, DQ + DKV, QK_ROPE + NHS)
    po = [sum(pw[:k]) for k in range(len(pw))]
    w_in = jnp.concatenate([w_nat[:, o[3]:o[4]], w_nat[:, o[6]:o[8]], w_nat[:, o[4]:o[5]], w_nat[:, o[0]:o[2]],
                            w_nat[:, o[2]:o[3]], w_nat[:, o[5]:o[6]]], axis=1)
    w_z, w_g, w_xbc, w_cqkv = (w_in[:, po[k]:po[k] + pw[k]] for k in range(4))
    w_kr, w_dt = w_in[:, po[4]:po[4] + QK_ROPE], w_in[:, po[4] + QK_ROPE:]
    zc = lambda n: jnp.zeros((D, n), BF16)
    w_sm = jnp.concatenate([zc(QK_NOPE), w_kr, zc(QK_PAD - QK_NOPE - QK_ROPE), w_dt, zc(LANES - NHS)], axis=1)
    w_q = jnp.pad(wts["w_uq"].reshape(DQ, NH, QK_NOPE + QK_ROPE),
                  ((0, 0), (0, 0), (0, QK_PAD - QK_NOPE - QK_ROPE))).reshape(DQ, NH * QK_PAD)
    ukv = wts["w_ukv"].reshape(DKV, NH, QK_NOPE + V_DIM)
    w_k = jnp.pad(ukv[:, :, :QK_NOPE], ((0, 0), (0, 0), (0, QK_PAD - QK_NOPE))).reshape(DKV, NH * QK_PAD)
    w_v = ukv[:, :, QK_NOPE:].reshape(DKV, NH * V_DIM)
    dt_bias_p, a_log_p = _pad_cols(small["dt_bias"], LANES), _pad_cols(small["a_log"], LANES)
    dskip_rep = jnp.repeat(small["d_skip"], SSM_HEADDIM, axis=1)
    c_tab, a_tab, b_tab = _rope_tables(positions)

    (u,) = _rowwise("pre_norm", _st_pre, [x], [small["mix_norm_pre"]], [(D, BF16)])
    cqkv = _mm("in_cqkv", u, w_cqkv, "nn")
    z = _mm("in_z", u, w_z, "nn")
    xbc = _mm("in_xbc", u, w_xbc, "nn")
    g = _mm("in_gates", u, w_g, "nn")
    sm = _mm("in_small", u, w_sm, "nn")

    lora_fn = _st_lora_norms(DQ)
    cq_n, ckv_n = _rowwise("lora_norms", lora_fn, [cqkv], [small["q_norm"], small["kv_norm"]], [(DQ, BF16), (DKV, BF16)])
    qraw = _mm("up_q", cq_n, w_q, "nn")
    kraw = _mm("up_k", ckv_n, w_k, "nn")
    v = _mm("up_v", ckv_n, w_v, "nn", out_dtype=BF16)
    v_t = _mm("up_v_t", w_v.T, ckv_n, "nt", out_dtype=BF16)
    q, k = _rowwise("rope", _st_rope(NH), [qraw, kraw, sm, c_tab, a_tab, b_tab], [],
                    [(NH * QK_PAD, BF16), (NH * QK_PAD, BF16)])
    attn, lse, late_blocks = _flash_fwd(q, k, v_t, NH, carry=_gather_carry(list(late_shards)) if late_names else None)
    wts.update({n: _assemble(n, b) for n, b in zip(late_names, late_blocks)})

    xbc_c, xbc_a = _conv_fwd(xbc, conv_w, small["conv_b"])
    dt, da = _rowwise("dt", _st_dt, [sm], [dt_bias_p, a_log_p], [(LANES, F32), (LANES, F32)])

    def col_layout(t):
        return _pad_cols(t[:, :NHS].reshape(S, G, R).transpose(1, 0, 2).reshape(G * S, R), LANES).reshape(G, S, LANES)

    def row_layout(t):
        return jnp.pad(t[:, :NHS].reshape(S, G, R).transpose(1, 2, 0), ((0, 0), (0, rp - R), (0, 0)))

    dt_row, da_col, da_row = row_layout(dt), col_layout(da), row_layout(da)
    xs = xbc_a[:, :DI]
    y, states = _ssd_fwd_t(xbc_a, dt_row, da_col, da_row, DI, G, R)
    gn_fn = _st_gated_norm(G)
    (ssm,) = _rowwise("gated_norm", gn_fn, [y, xs, z], [dskip_rep, small["ssm_norm"]], [(DI, BF16)])

    ao = _mm("attn_o", attn, wts["w_attn_o"], "nn")
    so = _mm("ssm_o", ssm, wts["w_ssm_o"], "nn")
    mix_fn = _st_mix(D)
    (mixed,) = _rowwise("mix", mix_fn, [g, ao, so], [], [(D, BF16)])
    mo = _mm("out_proj", mixed, wts["w_out"], "nn")
    h1, f = _rowwise("res1", _st_res_norm, [x, mo], [small["mix_norm_post"], small["ffn_norm_pre"]], [(D, F32), (D, BF16)])
    gt = _mm("ffn_gate", f, wts["w_gate"], "nn")
    up = _mm("ffn_up", f, wts["w_up"], "nn")
    (act,) = _rowwise("swiglu", _st_swiglu, [gt, up], [], [(gt.shape[1], BF16)])
    dn = _mm("ffn_down", act, wts["w_down"], "nn")
    h2, a3 = _rowwise("res2", _st_res_norm, [h1, dn], [small["ffn_norm_post"], small["ple_norm_pre"]], [(D, F32), (D, BF16)])
    gl = _mm("ple_gate", a3, wts["w_ple_gate"], "nn")
    pe = _mm("ple_proj", p, wts["w_ple"], "nn")

    sg = {}
    bg = {}
    dpe, dgl, dh2, d_w, loss_acc = _rowwise_bwd(
        "loss", _st_loss, [pe, gl, h2, target], [small["ple_norm_post"]], [1.0], [0, 1, 2], [BF16, BF16, F32], fwd_sums=(0,))
    sg["ple_norm_post"] = _fold(d_w)
    loss = loss_acc[0:1, :]
    bg["w_ple"] = _mm("d_w_ple", p, dpe, "tn", out_dtype=BF16)
    bg["w_ple_gate"] = _mm("d_w_ple_gate", a3, dgl, "tn", out_dtype=BF16)
    da3 = _mm("d_a3", dgl, wts["w_ple_gate"], "nt")

    dh1, ddn, d_post, d_pre = _rowwise_bwd(
        "res2_bwd", _st_res_norm, [h1, dn], [small["ffn_norm_post"], small["ple_norm_pre"]], [dh2, da3], [0, 1], [F32, BF16])
    sg["ffn_norm_post"], sg["ple_norm_pre"] = _fold(d_post), _fold(d_pre)
    bg["w_down"] = _mm("d_w_down", act, ddn, "tn", out_dtype=BF16)
    dact = _mm("d_act", ddn, wts["w_down"], "nt")
    dgt, dup = _rowwise_bwd("swiglu_bwd", _st_swiglu, [gt, up], [], [dact], [0, 1], [BF16, BF16])
    bg["w_gate"] = _mm("d_w_gate", f, dgt, "tn", out_dtype=BF16)
    bg["w_up"] = _mm("d_w_up", f, dup, "tn", out_dtype=BF16)
    df = _mm("d_f_gate", dgt, wts["w_gate"], "nt")
    df = _mm("d_f_up", dup, wts["w_up"], "nt", acc_in=df)

    dx_res, dmo, d_post, d_pre = _rowwise_bwd(
        "res1_bwd", _st_res_norm, [x, mo], [small["mix_norm_post"], small["ffn_norm_pre"]], [dh1, df], [0, 1], [F32, BF16])
    sg["mix_norm_post"], sg["ffn_norm_pre"] = _fold(d_post), _fold(d_pre)
    bg["w_out"] = _mm("d_w_out", mixed, dmo, "tn", out_dtype=BF16)
    dmixed = _mm("d_mixed", dmo, wts["w_out"], "nt")
    assert po[1] % pw[1] == 0 and po[3] % pw[3] == 0, (po, pw)
    dproj = lax.empty((S, sum(pw)), BF16)
    dproj, dao, dso = _rowwise_bwd("mix_bwd", mix_fn, [g, ao, so], [], [dmixed], [0, 1, 2], [BF16, BF16, BF16],
                                   into={0: (dproj, po[1] // pw[1])})
    bg["w_attn_o"] = _mm("d_w_attn_o", attn, dao, "tn", out_dtype=BF16)
    bg["w_ssm_o"] = _mm("d_w_ssm_o", ssm, dso, "tn", out_dtype=BF16)
    dattn = _mm("d_attn", dao, wts["w_attn_o"], "nt", out_dtype=BF16)
    dssm = _mm("d_ssm", dso, wts["w_ssm_o"], "nt")

    dy, dxs_a, dproj, d_dskip, d_ssmn = _rowwise_bwd(
        "gated_norm_bwd", gn_fn, [y, xs, z], [dskip_rep, small["ssm_norm"]], [dssm], [0, 1, 2], [F32, F32, BF16],
        into={2: (dproj, 0)})
    sg["d_skip"] = _fold(d_dskip).reshape(NHS, SSM_HEADDIM).sum(axis=1).reshape(1, NHS)
    sg["ssm_norm"] = _fold(d_ssmn)
    dxs_b, d_b, d_c, ddt_row, dcum_row = _ssd_bwd_t(xbc_a, dt_row, da_col, da_row, states, dy, DI, G, R)

    def from_row(t):
        return _pad_cols(t[:, :R, :].transpose(2, 0, 1).reshape(S, NHS), LANES)

    ddtraw, d_bias, d_alog = _rowwise("dt_bwd", _st_dt_bwd, [sm, from_row(ddt_row), from_row(dcum_row)],
                                      [dt_bias_p, a_log_p], [(LANES, F32)], accs=(LANES, LANES), tr=L)
    sg["dt_bias"], sg["a_log"] = _fold(d_bias)[:, :NHS], _fold(d_alog)[:, :NHS]
    dconv, d_cb = _rowwise("dconv", _st_dconv(DI), [xbc_c, dxs_a, dxs_b, d_b, d_c], [], [(CONV, F32)], accs=(CONV,))
    sg["conv_b"] = _fold(d_cb)
    dproj, d_cw = _conv_bwd(xbc, dconv, conv_w, dproj, po[2])
    d_conv_w = d_cw.reshape(CONV_WIDTH, SUBLANES, CONV).sum(axis=1)

    (delta,) = _rowwise("attn_delta", _st_delta(NH), [dattn, attn], [], [(LANES, F32)])
    delta = delta[:, :NH].T
    late_sent = [_by_device(n, bg.pop(n)) for n in late_names]
    dq, dk, dv, late_recv = _flash_bwd(q, k, v, dattn, lse, delta.reshape(NH, 1, S), NH,
                                       carry=_scatter_carry(late_sent) if late_names else None)
    dqraw, dkr, dk = _rowwise("rope_bwd", _st_rope_bwd(NH), [dq, dk, c_tab, a_tab, b_tab], [],
                              [(NH * QK_PAD, BF16), (QK_PAD, F32), (NH * QK_PAD, BF16)])
    d_w_q = _mm("d_w_q", cq_n, dqraw, "tn", out_dtype=BF16)
    d_w_k = _mm("d_w_k", ckv_n, dk, "tn", out_dtype=BF16)
    d_w_v = _mm("d_w_v", ckv_n, dv, "tn", out_dtype=BF16)
    dcq_n = _mm("d_cq_n", dqraw, w_q, "nt")
    dckv_n = _mm("d_ckv_n_k", dk, w_k, "nt")
    dckv_n = _mm("d_ckv_n_v", dv, w_v, "nt", acc_in=dckv_n)
    bg["w_uq"] = d_w_q.reshape(DQ, NH, QK_PAD)[:, :, :QK_NOPE + QK_ROPE].reshape(DQ, NH * (QK_NOPE + QK_ROPE))
    bg["w_ukv"] = jnp.concatenate([d_w_k.reshape(DKV, NH, QK_PAD)[:, :, :QK_NOPE], d_w_v.reshape(DKV, NH, V_DIM)],
                                  axis=2).reshape(DKV, NH * (QK_NOPE + V_DIM))
    dproj, d_qn, d_kvn = _rowwise_bwd("lora_norms_bwd", lora_fn, [cqkv], [small["q_norm"], small["kv_norm"]],
                                      [dcq_n, dckv_n], [0], [BF16], into={0: (dproj, po[3] // pw[3])})
    sg["q_norm"], sg["kv_norm"] = _fold(d_qn), _fold(d_kvn)

    d_small = jnp.concatenate([dkr[:, QK_NOPE:QK_NOPE + QK_ROPE], ddtraw[:, :NHS]], axis=1).astype(BF16)
    dproj = lax.dynamic_update_slice(dproj, d_small, (0, po[4]))
    d_w = _mm("d_w_in", u, dproj, "tn", out_dtype=BF16)
    kr = po[4] + QK_ROPE
    bg["w_in"] = jnp.concatenate([d_w[:, po[3]:po[4]], d_w[:, po[4]:kr], d_w[:, po[0]:po[1]], d_w[:, po[2]:po[3]],
                                  d_w[:, kr:kr + NHS], d_w[:, po[1]:po[2]]], axis=1)
    rest_names = tuple(n for n in BIG if n in bg)
    if not exchange:
        du = _mm("d_u", dproj, w_in, "nt")
        grad_x, d_pre = _rowwise_bwd("pre_norm_bwd", _st_pre, [x], [small["mix_norm_pre"]], [du], [0], [F32],
                                     adds={0: dx_res})
        sg["mix_norm_pre"] = _fold(d_pre)
        return loss, grad_x, sg, d_conv_w, bg, {}
    rest_sent = [_by_device(n, bg.pop(n)) for n in rest_names]
    du, rest_recv = _mm("d_u", dproj, w_in, "nt", carry=_scatter_carry(rest_sent))
    grad_x, d_pre = _rowwise_bwd("pre_norm_bwd", _st_pre, [x], [small["mix_norm_pre"]], [du], [0], [F32], adds={0: dx_res})
    sg["mix_norm_pre"] = _fold(d_pre)
    sent = dict(zip(late_names, late_sent), **dict(zip(rest_names, rest_sent)))
    recv = dict(zip(late_names, late_recv), **dict(zip(rest_names, rest_recv)))
    return loss, grad_x, sg, d_conv_w, sent, recv


_HBM = pl.BlockSpec(memory_space=pltpu.HBM)
_FLIPS = ((0, 0, 1), (1, 0, 0), (0, 1, 0), (1, 1, 0), (1, 0, 1), (0, 1, 1), (1, 1, 1))


def _place():
    return lax.axis_index("x"), lax.axis_index("y"), lax.axis_index("c")


def _flipped(place, flip):
    return tuple(1 - v if f else v for v, f in zip(place, flip))


def _gather_carry(blocks):
    nw = len(blocks)

    def copies(ins, outs, send_sems, recv_sems, local_sems):
        x, y, c = _place()
        me = 4 * x + 2 * y + c
        cps = []
        for w in range(nw):
            cps.append(pltpu.make_async_copy(ins[w], outs[w].at[me], local_sems.at[w]))
            for k, flip in enumerate(_FLIPS):
                cps.append(pltpu.make_async_remote_copy(
                    src_ref=ins[w], dst_ref=outs[w].at[me], send_sem=send_sems.at[7 * w + k],
                    recv_sem=recv_sems.at[7 * w + k], device_id=_flipped((x, y, c), flip), device_id_type=MESH))
        return cps

    return dict(ins=blocks, outs=[jax.ShapeDtypeStruct((N_DEV,) + b.shape, b.dtype) for b in blocks],
                n_remote=7 * nw, n_local=nw, copies=copies)


def _scatter_carry(by_dev):
    nw = len(by_dev)

    def copies(ins, outs, send_sems, recv_sems, local_sems):
        x, y, c = _place()
        cps = []
        for w in range(nw):
            for k, flip in enumerate(_FLIPS):
                px, py, pc = _flipped((x, y, c), flip)
                cps.append(pltpu.make_async_remote_copy(
                    src_ref=ins[w].at[4 * px + 2 * py + pc], dst_ref=outs[w].at[k], send_sem=send_sems.at[7 * w + k],
                    recv_sem=recv_sems.at[7 * w + k], device_id=(px, py, pc), device_id_type=MESH))
        return cps

    return dict(ins=by_dev, outs=[jax.ShapeDtypeStruct((7,) + b.shape[1:], b.dtype) for b in by_dev],
                n_remote=7 * nw, n_local=0, copies=copies)


def _all_gather(name, blocks):
    nw = len(blocks)

    def body(*refs):
        x_refs, out_refs = refs[:nw], refs[nw:2 * nw]
        send_sems, recv_sems, local_sems = refs[2 * nw:]
        x, y, c = _place()
        me, sibling = (x, y, c), (x, y, 1 - c)
        chips = [(1 - x, y), (x, 1 - y), (1 - x, 1 - y)]

        def slot(w, px, py, pc):
            return out_refs[w].at[4 * px + 2 * py + pc]

        def copy(w, k, blk, to, src=None):
            return pltpu.make_async_remote_copy(
                src_ref=slot(w, *blk) if src is None else src, dst_ref=slot(w, *blk),
                send_sem=send_sems.at[7 * w + k], recv_sem=recv_sems.at[7 * w + k], device_id=to, device_id_type=MESH)

        mine = [pltpu.make_async_copy(x_refs[w], slot(w, *me), local_sems.at[w]) for w in range(nw)]
        for cp in mine:
            cp.start()
        first = []
        for w in range(nw):
            first.append(copy(w, 0, me, sibling, src=x_refs[w]))
            first += [copy(w, 1 + j, me, (*chip, c), src=x_refs[w]) for j, chip in enumerate(chips)]
        for cp in first:
            cp.start()
        passed = []
        for j, chip in enumerate(chips):
            for w in range(nw):
                copy(w, 1 + j, (*chip, c), me).wait_recv()
                passed.append(copy(w, 4 + j, (*chip, c), sibling))
                passed[-1].start()
        for w in range(nw):
            copy(w, 0, sibling, me).wait_recv()
        for j, chip in enumerate(chips):
            for w in range(nw):
                copy(w, 4 + j, (*chip, 1 - c), me).wait_recv()
        for cp in first + passed:
            cp.wait_send()
        for cp in mine:
            cp.wait()

    return pl.pallas_call(
        body, name=name, out_shape=[jax.ShapeDtypeStruct((N_DEV,) + b.shape, b.dtype) for b in blocks],
        in_specs=[_HBM] * nw, out_specs=[_HBM] * nw,
        scratch_shapes=[pltpu.SemaphoreType.DMA((7 * nw,)), pltpu.SemaphoreType.DMA((7 * nw,)),
                        pltpu.SemaphoreType.DMA((nw,))],
    )(*blocks)


def _lane_pad(n):
    return -(-n // LANES) * LANES


def _pack_small(vecs, mat):
    width = max(sum(_lane_pad(v.shape[1]) for v in vecs), _lane_pad(mat.shape[1]))
    row0 = jnp.concatenate([_pad_cols(v, _lane_pad(v.shape[1])) for v in vecs], axis=1)
    rows = jnp.concatenate([_pad_cols(row0, width), _pad_cols(mat, width)], axis=0)
    return jnp.pad(rows, ((0, SUBLANES - rows.shape[0]), (0, 0)))


def _unpack_small(packed, sizes, mat_cols):
    vecs, off = [], 0
    for n in sizes:
        vecs.append(packed[0:1, off:off + n])
        off += _lane_pad(n)
    return vecs, packed[1:1 + CONV_WIDTH, :mat_cols]


def _adamw(w, g, m, v):
    m = ADAM_B1 * m + (1.0 - ADAM_B1) * g
    v = ADAM_B2 * v + (1.0 - ADAM_B2) * (g * g)
    m_hat = m / (1.0 - ADAM_B1 ** ADAM_STEP)
    v_hat = v / (1.0 - ADAM_B2 ** ADAM_STEP)
    delta = -ADAM_LR * (m_hat / (jnp.sqrt(v_hat) + ADAM_EPS) + ADAM_WD * w)
    return delta, m, v


BIG = ("w_in", "w_uq", "w_ukv", "w_attn_o", "w_ssm_o", "w_out", "w_gate", "w_up", "w_down", "w_ple_gate", "w_ple")
FIRST = ("w_in", "w_uq", "w_ukv")
LATE = ("w_attn_o", "w_ssm_o", "w_out", "w_gate", "w_up", "w_down", "w_ple_gate", "w_ple")
COL_SHARDED = ("w_in", "w_uq", "w_ukv", "w_gate", "w_up", "w_ple")
SMALL = ("mix_norm_pre", "mix_norm_post", "q_norm", "kv_norm", "conv_b", "dt_bias", "a_log", "d_skip", "ssm_norm",
         "ffn_norm_pre", "ffn_norm_post", "ple_norm_pre", "ple_norm_post")
WEIGHTS = ("mix_norm_pre", "mix_norm_post", "w_in", "q_norm", "w_uq", "kv_norm", "w_ukv", "conv_w", "conv_b", "dt_bias",
           "a_log", "d_skip", "ssm_norm", "w_attn_o", "w_ssm_o", "w_out", "ffn_norm_pre", "ffn_norm_post", "w_gate",
           "w_up", "w_down", "ple_norm_pre", "ple_norm_post", "w_ple_gate", "w_ple")


def _step(x, p, positions, target, w, m, v):
    xi, yi, ci = _place()
    me = 4 * xi + 2 * yi + ci

    gathered = _all_gather("gather_weights", [w[n].astype(BF16) for n in FIRST])
    wts = {n: _assemble(n, blocks) for n, blocks in zip(FIRST, gathered)}
    cw_rows, cw_cols = w["conv_w"].shape
    (cw_all,) = _all_gather("gather_conv_w", [jnp.pad(w["conv_w"], ((0, SUBLANES - cw_rows), (0, 0)))])
    conv_w = cw_all[:, :cw_rows, :].transpose(1, 0, 2).reshape(cw_rows, N_DEV * cw_cols)

    small = {n: w[n] for n in SMALL}
    loss, grad_x, sg, d_conv_w, sent, recv = _local_step(
        x, p, positions, target, small, conv_w, wts, LATE, [w[n].astype(BF16) for n in LATE])

    sizes = [w[n].shape[1] for n in SMALL]
    sg_pack = _pack_small([sg[n] for n in SMALL] + [loss], d_conv_w)
    (sg_all,) = _all_gather("gather_small_grads", [sg_pack])
    (sg_sum,) = _rowwise("sum_small_grads", lambda *a: functools.reduce(lambda s, t: s + t, a),
                         [sg_all[k] for k in range(N_DEV)], [], [(sg_pack.shape[1], F32)])
    sg_vecs, d_conv_w_sum = _unpack_small(sg_sum, sizes + [LANES], d_conv_w.shape[1])
    loss = sg_vecs[-1][0, 0]
    grads = dict(zip(SMALL, sg_vecs[:-1]))
    grads["conv_w"] = lax.dynamic_slice_in_dim(d_conv_w_sum, me * cw_cols, cw_cols, axis=1)

    def sum8_then_adamw(wv, mv, vv, own, *others):
        g = functools.reduce(lambda s, t: s + t, others, own)
        return (g,) + _adamw(wv, g, mv, vv)

    delta, new_m, new_v = {}, {}, {}
    for n in BIG:
        cols = w[n].shape[1]
        own = lax.dynamic_index_in_dim(sent[n], me, axis=0, keepdims=False)
        grads[n], delta[n], new_m[n], new_v[n] = _rowwise(
            "adamw_" + n, sum8_then_adamw, [w[n], m[n], v[n], own] + [recv[n][k] for k in range(N_DEV - 1)], [],
            [(cols, F32)] * 4)
    packed = [_pack_small([d[n] for n in SMALL], d["conv_w"]) for d in (w, grads, m, v)]
    outs = _rowwise("adamw_small", _adamw, packed, [], [(packed[0].shape[1], F32)] * 3)
    for d, o in zip((delta, new_m, new_v), outs):
        vecs, mat = _unpack_small(o, sizes, cw_cols)
        d.update(zip(SMALL, vecs))
        d["conv_w"] = mat
    return loss, grad_x, grads, delta, new_m, new_v


def kernel(x, p, positions, mix_norm_pre, mix_norm_post, w_in, q_norm, w_uq, kv_norm, w_ukv, conv_w, conv_b, dt_bias, a_log, d_skip, ssm_norm, w_attn_o, w_ssm_o, w_out, ffn_norm_pre, ffn_norm_post, w_gate, w_up, w_down, ple_norm_pre, ple_norm_post, w_ple_gate, w_ple, loss_target, m_mix_norm_pre, m_mix_norm_post, m_w_in, m_q_norm, m_w_uq, m_kv_norm, m_w_ukv, m_conv_w, m_conv_b, m_dt_bias, m_a_log, m_d_skip, m_ssm_norm, m_w_attn_o, m_w_ssm_o, m_w_out, m_ffn_norm_pre, m_ffn_norm_post, m_w_gate, m_w_up, m_w_down, m_ple_norm_pre, m_ple_norm_post, m_w_ple_gate, m_w_ple, v_mix_norm_pre, v_mix_norm_post, v_w_in, v_q_norm, v_w_uq, v_kv_norm, v_w_ukv, v_conv_w, v_conv_b, v_dt_bias, v_a_log, v_d_skip, v_ssm_norm, v_w_attn_o, v_w_ssm_o, v_w_out, v_ffn_norm_pre, v_ffn_norm_post, v_w_gate, v_w_up, v_w_down, v_ple_norm_pre, v_ple_norm_post, v_w_ple_gate, v_w_ple):
    w_args = (mix_norm_pre, mix_norm_post, w_in, q_norm, w_uq, kv_norm, w_ukv, conv_w, conv_b, dt_bias, a_log, d_skip, ssm_norm, w_attn_o, w_ssm_o, w_out, ffn_norm_pre, ffn_norm_post, w_gate, w_up, w_down, ple_norm_pre, ple_norm_post, w_ple_gate, w_ple)
    m_args = (m_mix_norm_pre, m_mix_norm_post, m_w_in, m_q_norm, m_w_uq, m_kv_norm, m_w_ukv, m_conv_w, m_conv_b, m_dt_bias, m_a_log, m_d_skip, m_ssm_norm, m_w_attn_o, m_w_ssm_o, m_w_out, m_ffn_norm_pre, m_ffn_norm_post, m_w_gate, m_w_up, m_w_down, m_ple_norm_pre, m_ple_norm_post, m_w_ple_gate, m_w_ple)
    v_args = (v_mix_norm_pre, v_mix_norm_post, v_w_in, v_q_norm, v_w_uq, v_kv_norm, v_w_ukv, v_conv_w, v_conv_b, v_dt_bias, v_a_log, v_d_skip, v_ssm_norm, v_w_attn_o, v_w_ssm_o, v_w_out, v_ffn_norm_pre, v_ffn_norm_post, v_w_gate, v_w_up, v_w_down, v_ple_norm_pre, v_ple_norm_post, v_w_ple_gate, v_w_ple)

    def drop_layer(a):
        return a if a.ndim == 2 else a[0]

    w = {n: drop_layer(a) for n, a in zip(WEIGHTS, w_args)}
    m = {n: drop_layer(a) for n, a in zip(WEIGHTS, m_args)}
    v = {n: drop_layer(a) for n, a in zip(WEIGHTS, v_args)}
    loss, grad_x, grads, delta, new_m, new_v = _step(x[0], p[0, 0], positions[0], loss_target[0], w, m, v)
    like = lambda d: [d[n].reshape(a.shape) for n, a in zip(WEIGHTS, w_args)]
    return (loss, grad_x[None], *like(grads), *like(delta), *like(new_m), *like(new_v))
```

```python
import functools

import jax
import jax.numpy as jnp
from jax import lax
from jax.experimental import pallas as pl
from jax.experimental.pallas import tpu as pltpu

F32 = jnp.float32
BF16 = jnp.bfloat16

EPS = 1e-6
QK_NOPE = 128
QK_ROPE = 64
V_DIM = 128
QK_PAD = 256
ROPE_THETA = 10000.0
SSM_HEADDIM = 64
D_STATE = 128
CONV_WIDTH = 4
CHUNK = 256
ADAM_LR = 0.001
ADAM_B1 = 0.9
ADAM_B2 = 0.999
ADAM_EPS = 1e-08
ADAM_WD = 0.01
ADAM_STEP = 10

N_DEV = 8
LANES = 128
SUBLANES = 8
VMEM_LIMIT = 56 * 1024 * 1024
ROW_TILE_BYTES = 16 * 1024 * 1024
STRIP_ROWS = 16
STRIP_ELEMS = 128 * 1024
FLASH_T = 512
FLASH_FWD_RUN = 8
MM_TILE_BYTES = 20 * 1024 * 1024
MESH = pl.DeviceIdType.MESH


def _pick(dim, prefs):
    if dim <= prefs[0]:
        return dim
    for p in prefs:
        if dim % p == 0:
            return p
    return dim


def _tile(dim, cap):
    if dim <= cap:
        return dim
    best = None
    for t in range(LANES, cap + 1, LANES):
        if dim % t == 0:
            best = t
    return best if best is not None else dim


def _params(sem):
    return pltpu.CompilerParams(dimension_semantics=sem, vmem_limit_bytes=VMEM_LIMIT)


def _dot(a, b):
    return lax.dot_general(a, b, (((1,), (0,)), ((), ())), preferred_element_type=F32)


def _dot_nt(a, b):
    return lax.dot_general(a, b, (((1,), (1,)), ((), ())), preferred_element_type=F32)


def _dot_tn(a, b):
    return lax.dot_general(a, b, (((0,), (0,)), ((), ())), preferred_element_type=F32)


def _mm(name, a, b, mode, out_dtype=F32, acc_in=None, carry=None):
    if mode == "nn":
        (M, K), (K2, N) = a.shape, b.shape
    elif mode == "nt":
        (M, K), (N, K2) = a.shape, b.shape
    else:
        (K, M), (K2, N) = a.shape, b.shape
    assert K == K2, (name, a.shape, b.shape, mode)
    tm = _tile(M, 1024)
    tn = _tile(N, 1024 if acc_in is not None else 1536)
    tk = _tile(K, 2048)
    while tk > 512 and 2 * (tm * tk * a.dtype.itemsize + tk * tn * b.dtype.itemsize) > MM_TILE_BYTES:
        tk = _tile(K, tk - LANES)
    nk = K // tk
    dot = {"nn": _dot, "nt": _dot_nt, "tn": _dot_tn}[mode]
    has_acc = acc_in is not None

    def body(*refs):
        if has_acc:
            a_ref, b_ref, c_ref, o_ref, acc = refs
        else:
            a_ref, b_ref, o_ref, acc = refs
        k = pl.program_id(2)

        @pl.when(k == 0)
        def _():
            acc[...] = jnp.zeros_like(acc)

        acc[...] += dot(a_ref[...].astype(BF16), b_ref[...].astype(BF16))

        @pl.when(k == nk - 1)
        def _():
            r = acc[...]
            if has_acc:
                r = r + c_ref[...]
            o_ref[...] = r.astype(o_ref.dtype)

    if mode == "tn":
        a_spec = pl.BlockSpec((tk, tm), lambda i, j, k: (k, i))
    else:
        a_spec = pl.BlockSpec((tm, tk), lambda i, j, k: (i, k))
    if mode == "nt":
        b_spec = pl.BlockSpec((tn, tk), lambda i, j, k: (j, k))
    else:
        b_spec = pl.BlockSpec((tk, tn), lambda i, j, k: (k, j))
    o_spec = pl.BlockSpec((tm, tn), lambda i, j, k: (i, j))
    in_specs = [a_spec, b_spec] + ([o_spec] if has_acc else [])
    args = (a, b) + ((acc_in,) if has_acc else ())
    if carry is None:
        return pl.pallas_call(
            body, name=name, grid=(M // tm, N // tn, nk), in_specs=in_specs, out_specs=o_spec,
            out_shape=jax.ShapeDtypeStruct((M, N), out_dtype), scratch_shapes=[pltpu.VMEM((tm, tn), F32)],
            input_output_aliases=({2: 0} if has_acc and out_dtype == F32 else {}),
            compiler_params=_params(("parallel", "parallel", "arbitrary")),
        )(*args)
    grid = (M // tm, N // tn, nk)
    c_args, c_in_specs, c_out_specs, c_out_shapes, c_sems = _carry_call(carry)
    res = pl.pallas_call(
        _carried(body, len(args), 1, 1, carry, grid), name=name, grid=grid, in_specs=in_specs + c_in_specs,
        out_specs=[o_spec] + c_out_specs, out_shape=[jax.ShapeDtypeStruct((M, N), out_dtype)] + c_out_shapes,
        scratch_shapes=[pltpu.VMEM((tm, tn), F32)] + c_sems,
        compiler_params=_params(("arbitrary", "arbitrary", "arbitrary")),
    )(*args, *c_args)
    return res[0], list(res[1:])


def _row_tile(n_rows, bytes_per_row):
    tr = 1024
    while tr > SUBLANES and tr * bytes_per_row > ROW_TILE_BYTES:
        tr //= 2
    while n_rows % tr:
        tr //= 2
    return tr


def _strip_rows(width):
    return max(STRIP_ROWS, min(4 * STRIP_ROWS, STRIP_ELEMS // width // STRIP_ROWS * STRIP_ROWS))


def _over_strips(tr, strip, work):
    if strip is None or tr <= strip or tr % strip:
        work(slice(None))
        return

    def one(s, carry):
        work(pl.ds(pl.multiple_of(s * strip, strip), strip))
        return carry

    lax.fori_loop(0, tr // strip, one, 0)


def _acc_add(a_ref, v):
    if v.shape[0] == 1:
        a_ref[0:1, :] += v
    else:
        a_ref[...] += v.reshape(v.shape[0] // SUBLANES, SUBLANES, v.shape[1]).sum(axis=0)


def _rowwise(name, fn, rows, bcs, outs, accs=(), tr=None):
    n_rows = rows[0].shape[0]
    strip = _strip_rows(max(r.shape[1] for r in rows)) if tr is None else None
    if tr is None:
        per_row = sum(r.shape[1] * r.dtype.itemsize for r in rows) + sum(w * jnp.dtype(d).itemsize for w, d in outs)
        tr = _row_tile(n_rows, per_row)
    n_r, n_b, n_o, n_a = len(rows), len(bcs), len(outs), len(accs)

    def body(*refs):
        r_refs, b_refs = refs[:n_r], refs[n_r: n_r + n_b]
        o_refs = refs[n_r + n_b: n_r + n_b + n_o]
        a_refs = refs[n_r + n_b + n_o:]
        if n_a:
            @pl.when(pl.program_id(0) == 0)
            def _():
                for a in a_refs:
                    a[...] = jnp.zeros_like(a)

        def work(rws):
            ins = [r[rws, :].astype(F32) for r in r_refs] + [b[...].astype(F32) for b in b_refs]
            res = fn(*ins)
            res = res if isinstance(res, (tuple, list)) else (res,)
            for o, v in zip(o_refs, res[:n_o]):
                o[rws, :] = v.astype(o.dtype)
            for a, v in zip(a_refs, res[n_o:]):
                _acc_add(a, v)

        _over_strips(tr, strip, work)

    in_specs = [pl.BlockSpec((tr, r.shape[1]), lambda i: (i, 0)) for r in rows]
    in_specs += [pl.BlockSpec((1, b.shape[1]), lambda i: (0, 0)) for b in bcs]
    out_specs = [pl.BlockSpec((tr, w), lambda i: (i, 0)) for w, _ in outs]
    out_specs += [pl.BlockSpec((SUBLANES, w), lambda i: (0, 0)) for w in accs]
    out_shape = [jax.ShapeDtypeStruct((n_rows, w), d) for w, d in outs]
    out_shape += [jax.ShapeDtypeStruct((SUBLANES, w), F32) for w in accs]
    res = pl.pallas_call(
        body, name=name, grid=(n_rows // tr,), in_specs=in_specs, out_specs=out_specs, out_shape=out_shape,
        compiler_params=_params(("arbitrary",) if n_a else ("parallel",)),
    )(*rows, *bcs)
    return tuple(res)


def _rowwise_bwd(name, fn, rows, bcs, cts, need_rows, row_dtypes, need_bcs=None, adds=None, fwd_sums=(), tr=None,
                 into=None):
    n_rows = rows[0].shape[0]
    adds = adds or {}
    into = into or {}
    into_keys = sorted(into)
    need_bcs = list(range(len(bcs))) if need_bcs is None else list(need_bcs)
    ct_arrays = [c for c in cts if not isinstance(c, float)]
    add_keys = sorted(adds)
    add_arrays = [adds[k] for k in add_keys]
    strip = _strip_rows(max(r.shape[1] for r in rows)) if tr is None else None
    if tr is None:
        per_row = sum(r.shape[1] * r.dtype.itemsize for r in list(rows) + ct_arrays + add_arrays)
        per_row += sum(rows[i].shape[1] * jnp.dtype(d).itemsize for i, d in zip(need_rows, row_dtypes))
        tr = _row_tile(n_rows, per_row)
    n_r, n_b, n_c, n_ad = len(rows), len(bcs), len(ct_arrays), len(add_arrays)
    n_go, n_gb, n_fs = len(need_rows), len(need_bcs), len(fwd_sums)

    def body(*refs):
        pos = [0]

        def take(n):
            pos[0] += n
            return refs[pos[0] - n: pos[0]]

        r_refs, b_refs, c_refs, ad_refs = take(n_r), take(n_b), take(n_c), take(n_ad)
        take(len(into_keys))
        go_refs = take(n_go)
        acc_refs = refs[pos[0]:]

        @pl.when(pl.program_id(0) == 0)
        def _():
            for a in acc_refs:
                a[...] = jnp.zeros_like(a)

        def wrapped(*a):
            r = fn(*a)
            return tuple(r) if isinstance(r, (tuple, list)) else (r,)

        def work(rws):
            r_t = [r[rws, :].astype(F32) for r in r_refs]
            b_t = [r[...].astype(F32) for r in b_refs]
            outs, vjp = jax.vjp(wrapped, *r_t, *b_t)
            it = iter(c_refs)
            full = tuple(jnp.full(o.shape, c, F32) if isinstance(c, float) else next(it)[rws, :].astype(F32)
                         for o, c in zip(outs, cts))
            grads = vjp(full)
            for o_ref, i in zip(go_refs, need_rows):
                g = grads[i]
                if i in adds:
                    g = g + ad_refs[add_keys.index(i)][rws, :].astype(F32)
                o_ref[rws, :] = g.astype(o_ref.dtype)
            for a, j in zip(acc_refs[:n_gb], need_bcs):
                _acc_add(a, grads[n_r + j])
            for a, j in zip(acc_refs[n_gb:], fwd_sums):
                a[0:1, :] += jnp.full((1, LANES), jnp.sum(outs[j]), F32)

        _over_strips(tr, strip, work)

    def row_spec(w):
        return pl.BlockSpec((tr, w), lambda i: (i, 0))

    in_specs = [row_spec(r.shape[1]) for r in rows]
    in_specs += [pl.BlockSpec((1, b.shape[1]), lambda i: (0, 0)) for b in bcs]
    in_specs += [row_spec(c.shape[1]) for c in ct_arrays] + [row_spec(a.shape[1]) for a in add_arrays]
    in_specs += [pl.BlockSpec(memory_space=pl.ANY) for _ in into_keys]
    out_specs = [row_spec(rows[i].shape[1]) for i in need_rows]
    out_shape = [jax.ShapeDtypeStruct((n_rows, rows[i].shape[1]), d) for i, d in zip(need_rows, row_dtypes)]
    aliases = {}
    for pos_in, k in enumerate(into_keys):
        buf, col_block = into[k]
        out_specs[k] = pl.BlockSpec((tr, rows[need_rows[k]].shape[1]), lambda i, cb=col_block: (i, cb))
        out_shape[k] = jax.ShapeDtypeStruct(buf.shape, buf.dtype)
        aliases[len(in_specs) - len(into_keys) + pos_in] = k
    out_specs += [pl.BlockSpec((SUBLANES, bcs[j].shape[1]), lambda i: (0, 0)) for j in need_bcs]
    out_specs += [pl.BlockSpec((SUBLANES, LANES), lambda i: (0, 0)) for _ in fwd_sums]
    out_shape += [jax.ShapeDtypeStruct((SUBLANES, bcs[j].shape[1]), F32) for j in need_bcs]
    out_shape += [jax.ShapeDtypeStruct((SUBLANES, LANES), F32) for _ in fwd_sums]
    res = pl.pallas_call(
        body, name=name, grid=(n_rows // tr,), in_specs=in_specs, out_specs=out_specs, out_shape=out_shape,
        input_output_aliases=aliases, compiler_params=_params(("arbitrary",)),
    )(*rows, *bcs, *ct_arrays, *add_arrays, *[into[k][0] for k in into_keys])
    return tuple(res)


def _fold(acc):
    return jnp.sum(acc, axis=0, keepdims=True)


def _rms(x, w):
    return x * lax.rsqrt(jnp.mean(x * x, axis=-1, keepdims=True) + EPS) * w


def _sigmoid(x):
    return jax.nn.sigmoid(x)


def _silu(x):
    return x * _sigmoid(x)


def _log1p(u):
    series = u * (1.0 - u * (0.5 - u * (1.0 / 3.0 - u * 0.25)))
    return jnp.where(u < 0.01, series, jnp.log(1.0 + u))


def _softplus(x):
    return jnp.maximum(x, 0.0) + _log1p(jnp.exp(-jnp.abs(x)))


def _st_pre(x, w):
    return _rms(x, w)


def _st_lora_norms(dq):
    def fn(cqkv, qn, kvn):
        return _rms(cqkv[:, :dq], qn), _rms(cqkv[:, dq:], kvn)
    return fn


def _st_gated_norm(n_groups):
    def fn(y, xs, z, dskip, wn):
        yz = (y + dskip * xs) * _silu(z)
        gw = yz.shape[1] // n_groups
        parts = [_rms(yz[:, g * gw:(g + 1) * gw], wn[:, g * gw:(g + 1) * gw]) for g in range(n_groups)]
        return jnp.concatenate(parts, axis=1)
    return fn


def _st_mix(d):
    def fn(g, ao, so):
        return _sigmoid(g[:, :d]) * ao + _sigmoid(g[:, d:]) * so
    return fn


def _st_res_norm(h, y, w_post, w_pre):
    h2 = h + _rms(y, w_post)
    return h2, _rms(h2, w_pre)


def _st_swiglu(gt, up):
    return _silu(gt) * up


def _st_loss(pe, gl, h2, tgt, w_post):
    e = pe * _sigmoid(gl)
    diff = h2 + _rms(e, w_post) - tgt
    return 0.5 * jnp.mean(diff * diff, axis=-1, keepdims=True)


def _rope_tables(positions):
    half = QK_ROPE // 2
    inv_freq = ROPE_THETA ** (-jnp.arange(0, QK_ROPE, 2, dtype=F32) / QK_ROPE)
    ang = positions.astype(F32).reshape(-1, 1) * inv_freq
    cos, sin = jnp.cos(ang), jnp.sin(ang)
    n = ang.shape[0]
    z = lambda w: jnp.zeros((n, w), F32)
    c_tab = jnp.concatenate([jnp.ones((n, QK_NOPE), F32), cos, cos, z(QK_PAD - QK_NOPE - QK_ROPE)], axis=1)
    a_tab = jnp.concatenate([z(QK_NOPE), -sin, z(half), z(QK_PAD - QK_NOPE - QK_ROPE)], axis=1)
    b_tab = jnp.concatenate([z(QK_NOPE), z(half), sin, z(QK_PAD - QK_NOPE - QK_ROPE)], axis=1)
    return c_tab, a_tab, b_tab


def _rot(x, c, a, b):
    half = QK_ROPE // 2
    return x * c + pltpu.roll(x, QK_PAD - half, axis=1) * a + pltpu.roll(x, half, axis=1) * b


def _rot_t(g, c, a, b):
    half = QK_ROPE // 2
    return g * c + pltpu.roll(g * a, half, axis=1) + pltpu.roll(g * b, QK_PAD - half, axis=1)


def _st_rope(n_heads):
    def fn(qraw, kraw, sm, c, a, b):
        kpe = _rot(sm[:, :QK_PAD], c, a, b)
        scale = float(QK_NOPE + QK_ROPE) ** -0.5
        q = [_rot(qraw[:, h * QK_PAD:(h + 1) * QK_PAD], c, a, b) * scale for h in range(n_heads)]
        k = [kraw[:, h * QK_PAD:(h + 1) * QK_PAD] + kpe for h in range(n_heads)]
        return jnp.concatenate(q, axis=1), jnp.concatenate(k, axis=1)
    return fn


def _st_rope_bwd(n_heads):
    def fn(dq, dk, c, a, b):
        dqraw = [_rot_t(dq[:, h * QK_PAD:(h + 1) * QK_PAD], c, a, b) for h in range(n_heads)]
        dks = dk[:, :QK_PAD]
        for h in range(1, n_heads):
            dks = dks + dk[:, h * QK_PAD:(h + 1) * QK_PAD]
        return jnp.concatenate(dqraw, axis=1), _rot_t(dks, c, a, b), dk
    return fn


def _split3(x):
    h1 = x.astype(BF16)
    r1 = x - h1.astype(F32)
    h2 = r1.astype(BF16)
    h3 = (r1 - h2.astype(F32)).astype(BF16)
    return h1, h2, h3


def _tri_dot(tri, x):
    h1, h2, h3 = _split3(x)
    return (_dot(tri, h3) + _dot(tri, h2)) + _dot(tri, h1)


def _dot_tri(x, tri):
    h1, h2, h3 = _split3(x)
    return (_dot(h3, tri) + _dot(h2, tri)) + _dot(h1, tri)


def _st_dt(sm, bias, alog):
    x = sm[:, QK_PAD:] + bias
    dt = _softplus(x)
    return dt, dt * (-jnp.exp(alog))


def _st_dt_bwd(sm, ddt, dcum, bias, alog):
    n = sm.shape[0]
    i = lax.broadcasted_iota(jnp.int32, (n, n), 0)
    j = lax.broadcasted_iota(jnp.int32, (n, n), 1)
    upper = (j >= i).astype(BF16)
    dda = _tri_dot(upper, dcum)
    x = sm[:, QK_PAD:] + bias
    dt = _softplus(x)
    a = -jnp.exp(alog)
    draw = (ddt + dda * a) * _sigmoid(x)
    return draw, draw, dda * dt * a


def _conv_fwd(xbc, w, b):
    S, C = xbc.shape
    tr = _pick(S, (512, 256))
    tc = _pick(C, (1024, 512, 256, 128))
    hb = tr // SUBLANES

    def body(x_ref, halo_ref, w_ref, b_ref, c_ref, a_ref, ext):
        i = pl.program_id(1)
        halo = jnp.where(i == 0, 0.0, halo_ref[...])
        ext[0:SUBLANES, :] = halo
        ext[SUBLANES:, :] = x_ref[...]

        def lane_strip(c, carry):
            lanes = pl.ds(pl.multiple_of(c * LANES, LANES), LANES)
            wv = w_ref[:, lanes]
            acc = b_ref[:, lanes] + wv[CONV_WIDTH - 1:CONV_WIDTH, :] * ext[pl.ds(SUBLANES, tr), lanes]
            for k in range(CONV_WIDTH - 1):
                off = SUBLANES - (CONV_WIDTH - 1) + k
                acc = acc + wv[k:k + 1, :] * ext[pl.ds(off, tr), lanes]
            c_ref[:, lanes] = acc
            a_ref[:, lanes] = _silu(acc)
            return carry

        lax.fori_loop(0, tc // LANES, lane_strip, 0)

    return pl.pallas_call(
        body, name="conv_fwd", grid=(C // tc, S // tr),
        in_specs=[pl.BlockSpec((tr, tc), lambda j, i: (i, j)),
                  pl.BlockSpec((SUBLANES, tc), lambda j, i: (jnp.maximum(i * hb - 1, 0), j)),
                  pl.BlockSpec((CONV_WIDTH, tc), lambda j, i: (0, j)),
                  pl.BlockSpec((1, tc), lambda j, i: (0, j))],
        out_specs=[pl.BlockSpec((tr, tc), lambda j, i: (i, j))] * 2,
        out_shape=[jax.ShapeDtypeStruct((S, C), F32)] * 2,
        scratch_shapes=[pltpu.VMEM((tr + SUBLANES, tc), F32)],
        compiler_params=_params(("parallel", "arbitrary")),
    )(xbc, xbc, w, b)


def _conv_bwd(xbc, dconv, w, buf, col0):
    S, C = xbc.shape
    tr = _pick(S, (512, 256))
    tc = _pick(C, (1024, 512, 256, 128))
    hb = tr // SUBLANES
    n_i = S // tr
    assert col0 % tc == 0, (col0, tc)
    cb0 = col0 // tc

    def body(x_ref, halo_ref, d_ref, dnext_ref, w_ref, buf_ref, dx_ref, dw_ref, ext, dext):
        i = pl.program_id(1)
        ext[0:SUBLANES, :] = jnp.where(i == 0, 0.0, halo_ref[...])
        ext[SUBLANES:, :] = x_ref[...]
        dext[0:tr, :] = d_ref[...]
        dext[tr:, :] = jnp.where(i == n_i - 1, 0.0, dnext_ref[...])

        @pl.when(i == 0)
        def _():
            dw_ref[...] = jnp.zeros_like(dw_ref)

        def lane_strip(c, carry):
            lanes = pl.ds(pl.multiple_of(c * LANES, LANES), LANES)
            wv = w_ref[:, lanes]
            d = dext[pl.ds(0, tr), lanes]
            dx = wv[CONV_WIDTH - 1:CONV_WIDTH, :] * d
            for k in range(CONV_WIDTH):
                if k < CONV_WIDTH - 1:
                    dx = dx + wv[k:k + 1, :] * dext[pl.ds(CONV_WIDTH - 1 - k, tr), lanes]
                xs = ext[pl.ds(SUBLANES - (CONV_WIDTH - 1) + k, tr), lanes]
                prod = d * xs
                dw_ref[k * SUBLANES:(k + 1) * SUBLANES, lanes] += prod.reshape(tr // SUBLANES, SUBLANES, LANES).sum(axis=0)
            dx_ref[:, lanes] = dx.astype(dx_ref.dtype)
            return carry

        lax.fori_loop(0, tc // LANES, lane_strip, 0)

    return pl.pallas_call(
        body, name="conv_bwd", grid=(C // tc, n_i),
        in_specs=[pl.BlockSpec((tr, tc), lambda j, i: (i, j)),
                  pl.BlockSpec((SUBLANES, tc), lambda j, i: (jnp.maximum(i * hb - 1, 0), j)),
                  pl.BlockSpec((tr, tc), lambda j, i: (i, j)),
                  pl.BlockSpec((SUBLANES, tc), lambda j, i: (jnp.minimum((i + 1) * hb, S // SUBLANES - 1), j)),
                  pl.BlockSpec((CONV_WIDTH, tc), lambda j, i: (0, j)),
                  pl.BlockSpec(memory_space=pl.ANY)],
        out_specs=[pl.BlockSpec((tr, tc), lambda j, i: (i, cb0 + j)),
                   pl.BlockSpec((CONV_WIDTH * SUBLANES, tc), lambda j, i: (0, j))],
        out_shape=[jax.ShapeDtypeStruct(buf.shape, buf.dtype), jax.ShapeDtypeStruct((CONV_WIDTH * SUBLANES, C), F32)],
        scratch_shapes=[pltpu.VMEM((tr + SUBLANES, tc), F32), pltpu.VMEM((tr + SUBLANES, tc), F32)],
        input_output_aliases={5: 0},
        compiler_params=_params(("parallel", "arbitrary")),
    )(xbc, xbc, dconv, dconv, w, buf)


def _st_dconv(d_inner):
    def fn(xc, dxa, dxb, db_, dc_):
        s = _sigmoid(xc)
        g = jnp.concatenate([dxa + dxb, db_, dc_], axis=1) * (s * (1.0 + xc * (1.0 - s)))
        return g, g
    return fn


def _chunk_setup_t(b_ref, c_ref, dac_ref, dar_ref, L):
    ii = lax.broadcasted_iota(jnp.int32, (L, L), 0)
    jj = lax.broadcasted_iota(jnp.int32, (L, L), 1)
    lower = ii >= jj
    upper = ii <= jj
    cum_c = _tri_dot(lower.astype(BF16), dac_ref[0])
    cum_r = _dot_tri(dar_ref[0], upper.astype(BF16))
    bm = b_ref[...].astype(BF16)
    cm = c_ref[...].astype(BF16)
    return lower, upper, cum_c, cum_r, bm, cm


def _ssd_specs_t(d_inner, n_groups, gw, L, rp, chunk_of):
    bb0 = d_inner // D_STATE
    cb0 = bb0 + n_groups
    return [pl.BlockSpec((L, gw), lambda g, c: (chunk_of(c), g)),
            pl.BlockSpec((L, D_STATE), lambda g, c: (chunk_of(c), bb0 + g)),
            pl.BlockSpec((L, D_STATE), lambda g, c: (chunk_of(c), cb0 + g)),
            pl.BlockSpec((1, rp, L), lambda g, c: (g, 0, chunk_of(c))),
            pl.BlockSpec((1, L, LANES), lambda g, c: (g, chunk_of(c), 0)),
            pl.BlockSpec((1, rp, L), lambda g, c: (g, 0, chunk_of(c)))]


def _ssd_fwd_t(xbc_a, dt_row, da_col, da_row, d_inner, n_groups, R):
    S = xbc_a.shape[0]
    L = min(CHUNK, S)
    NC = S // L
    P, N = SSM_HEADDIM, D_STATE
    gw = R * P
    rp = da_row.shape[1]

    def body(x_ref, b_ref, c_ref, dt_ref, dac_ref, dar_ref, y_ref, st_ref, state, y_t):
        @pl.when(pl.program_id(1) == 0)
        def _():
            state[...] = jnp.zeros_like(state)

        st_ref[0, 0] = state[...]
        lower, upper, cum_c, cum_r, bm, cm = _chunk_setup_t(b_ref, c_ref, dac_ref, dar_ref, L)
        gm_t = _dot_nt(bm, cm)
        dt = dt_ref[0]
        x_t = x_ref[...].T
        for r in range(R):
            cc = cum_c[:, r:r + 1]
            cr = cum_r[r:r + 1, :]
            m_t = (gm_t * jnp.exp(jnp.where(upper, cr - cc, -jnp.inf))).astype(BF16)
            x = x_t[r * P:(r + 1) * P, :] * dt[r:r + 1, :]
            s_r = state[r * P:(r + 1) * P, :]
            y_off = _dot_nt(s_r.astype(BF16), cm) * jnp.exp(cr)
            y_t[r * P:(r + 1) * P, :] = _dot(x.astype(BF16), m_t) + y_off
            last = cr[:, L - 1:L]
            xw = (x * jnp.exp(last - cr)).astype(BF16)
            state[r * P:(r + 1) * P, :] = s_r * jnp.exp(last) + _dot(xw, bm)
        y_ref[...] = y_t[...].T

    return pl.pallas_call(
        body, name="ssd_fwd", grid=(n_groups, NC),
        in_specs=_ssd_specs_t(d_inner, n_groups, gw, L, rp, lambda c: c),
        out_specs=[pl.BlockSpec((L, gw), lambda g, c: (c, g)),
                   pl.BlockSpec((1, 1, gw, N), lambda g, c: (g, c, 0, 0))],
        out_shape=[jax.ShapeDtypeStruct((S, d_inner), F32), jax.ShapeDtypeStruct((n_groups, NC, gw, N), F32)],
        scratch_shapes=[pltpu.VMEM((gw, N), F32), pltpu.VMEM((gw, L), F32)],
        compiler_params=_params(("parallel", "arbitrary")),
    )(xbc_a, xbc_a, xbc_a, dt_row, da_col, da_row)


def _ssd_bwd_t(xbc_a, dt_row, da_col, da_row, states, dy, d_inner, n_groups, R):
    S = xbc_a.shape[0]
    L = min(CHUNK, S)
    NC = S // L
    P, N = SSM_HEADDIM, D_STATE
    gw = R * P
    rp = da_row.shape[1]
    rev = lambda c: NC - 1 - c

    def body(x_ref, b_ref, c_ref, dt_ref, dac_ref, dar_ref, st_ref, dy_ref,
             dx_ref, db_ref, dc_ref, ddt_ref, dcum_ref, dstate, dx_t):
        @pl.when(pl.program_id(1) == 0)
        def _():
            dstate[...] = jnp.zeros_like(dstate)

        lower, upper, cum_c, cum_r, bm, cm = _chunk_setup_t(b_ref, c_ref, dac_ref, dar_ref, L)
        gm = _dot_nt(cm, bm)
        gm_t = _dot_nt(bm, cm)
        dt = dt_ref[0]
        x_t = x_ref[...].T
        dy_t = dy_ref[...].T
        sub = lax.broadcasted_iota(jnp.int32, (rp, L), 0)
        is_last = lax.broadcasted_iota(jnp.int32, (1, L), 1) == L - 1
        d_g = jnp.zeros((L, L), F32)
        d_g_t = jnp.zeros((L, L), F32)
        dc_acc = jnp.zeros((L, N), F32)
        db_acc = jnp.zeros((L, N), F32)
        ddt_out = jnp.zeros((rp, L), F32)
        dcum_out = jnp.zeros((rp, L), F32)
        for r in range(R):
            cc = jnp.broadcast_to(cum_c[:, r:r + 1], (L, L))
            cr = cum_r[r:r + 1, :]
            lam = jnp.exp(jnp.where(lower, cc - cum_r[r:r + 1, :], -jnp.inf))
            lam_t = jnp.exp(jnp.where(upper, cr - cc, -jnp.inf))
            m = gm * lam
            m_t = gm_t * lam_t
            dtr = dt[r:r + 1, :]
            xh = x_t[r * P:(r + 1) * P, :]
            x = xh * dtr
            xb = x.astype(BF16)
            d_y = dy_t[r * P:(r + 1) * P, :]
            d_yb = d_y.astype(BF16)
            s_r = st_ref[0, 0, r * P:(r + 1) * P, :]
            s_rb = s_r.astype(BF16)
            ds_n = dstate[r * P:(r + 1) * P, :]
            ds_nb = ds_n.astype(BF16)
            e = jnp.exp(cr)
            last = cr[:, L - 1:L]
            e_last = jnp.exp(last)
            w = jnp.exp(last - cr)
            d_x = _dot(d_yb, m.astype(BF16))
            d_m = _dot_tn(d_yb, xb)
            d_m_t = _dot_tn(xb, d_yb)
            d_ye = (d_y * e).astype(BF16)
            dc_acc = dc_acc + _dot_tn(d_ye, s_rb)
            ds_part = _dot(d_ye, cm)
            y_off = _dot_nt(s_rb, cm) * e
            dcum = jnp.sum(d_y * y_off, axis=0, keepdims=True)
            d_xw = _dot_nt(ds_nb, bm)
            d_x = d_x + d_xw * w
            dw_w = jnp.sum(d_xw * x, axis=0, keepdims=True) * w
            db_acc = db_acc + _dot_tn((x * w).astype(BF16), ds_nb)
            d_last = jnp.sum(ds_n * s_r, keepdims=True) * e_last + jnp.sum(dw_w, keepdims=True)
            dstate[r * P:(r + 1) * P, :] = e_last * ds_n + ds_part
            d_g = d_g + d_m * lam
            d_g_t = d_g_t + d_m_t * lam_t
            dcum = (dcum - dw_w + jnp.sum(d_m_t * m_t, axis=0, keepdims=True)
                    - jnp.sum(d_m * m, axis=0, keepdims=True) + jnp.where(is_last, d_last, 0.0))
            dx_t[r * P:(r + 1) * P, :] = d_x * dtr
            ddt = jnp.sum(d_x * xh, axis=0, keepdims=True)
            ddt_out = ddt_out + jnp.where(sub == r, ddt, 0.0)
            dcum_out = dcum_out + jnp.where(sub == r, dcum, 0.0)
        dc_ref[...] = dc_acc + _dot(d_g.astype(BF16), bm)
        db_ref[...] = db_acc + _dot(d_g_t.astype(BF16), cm)
        dx_ref[...] = dx_t[...].T
        ddt_ref[0] = ddt_out
        dcum_ref[0] = dcum_out

    gn = n_groups * N
    return pl.pallas_call(
        body, name="ssd_bwd", grid=(n_groups, NC),
        in_specs=_ssd_specs_t(d_inner, n_groups, gw, L, rp, rev) + [
            pl.BlockSpec((1, 1, gw, N), lambda g, c: (g, rev(c), 0, 0)),
            pl.BlockSpec((L, gw), lambda g, c: (rev(c), g))],
        out_specs=[pl.BlockSpec((L, gw), lambda g, c: (rev(c), g)),
                   pl.BlockSpec((L, N), lambda g, c: (rev(c), g)),
                   pl.BlockSpec((L, N), lambda g, c: (rev(c), g)),
                   pl.BlockSpec((1, rp, L), lambda g, c: (g, 0, rev(c))),
                   pl.BlockSpec((1, rp, L), lambda g, c: (g, 0, rev(c)))],
        out_shape=[jax.ShapeDtypeStruct((S, d_inner), F32), jax.ShapeDtypeStruct((S, gn), F32),
                   jax.ShapeDtypeStruct((S, gn), F32), jax.ShapeDtypeStruct((n_groups, rp, S), F32),
                   jax.ShapeDtypeStruct((n_groups, rp, S), F32)],
        scratch_shapes=[pltpu.VMEM((gw, N), F32), pltpu.VMEM((gw, L), F32)],
        compiler_params=_params(("parallel", "arbitrary")),
    )(xbc_a, xbc_a, xbc_a, dt_row, da_col, da_row, states, dy)


def _attn_scale():
    return float(QK_NOPE + QK_ROPE) ** -0.5


def _diag_mask(t):
    return lax.broadcasted_iota(jnp.int32, (t, t), 0) <= lax.broadcasted_iota(jnp.int32, (t, t), 1)


def _walk_wide(lo, hi, tile_step, joint=True, widest=4):
    n = hi - lo

    def step(j, width):
        if joint:
            tile_step(j, width)
        else:
            for u in range(width):
                tile_step(j + u, 1)

    def widest_run(t, carry):
        step(lo + widest * t, widest)
        return carry

    lax.fori_loop(0, n // widest, widest_run, 0)

    def shorter_run(width):
        @pl.when(n % (2 * width) >= width)
        def _():
            step(hi - n % (2 * width), width)

    width = widest // 2
    while width >= 1:
        shorter_run(width)
        width //= 2


def _carried(main_body, n_in, n_out, n_scratch, carry, grid):
    if carry is None:
        return main_body
    n_ci, n_co = len(carry["ins"]), len(carry["outs"])

    def body(*refs):
        pos = [0]

        def take(n):
            pos[0] += n
            return refs[pos[0] - n: pos[0]]

        ins, c_ins, outs, c_outs, scratch, sems = take(n_in), take(n_ci), take(n_out), take(n_co), take(n_scratch), take(3)
        steps = [pl.program_id(a) for a in range(len(grid))]

        @pl.when(functools.reduce(jnp.logical_and, [s == 0 for s in steps]))
        def _():
            for cp in carry["copies"](c_ins, c_outs, *sems):
                cp.start()

        main_body(*ins, *outs, *scratch)

        @pl.when(functools.reduce(jnp.logical_and, [s == n - 1 for s, n in zip(steps, grid)]))
        def _():
            for cp in carry["copies"](c_ins, c_outs, *sems):
                cp.wait()

    return body


def _carry_call(carry):
    if carry is None:
        return [], [], [], [], []
    sems = [pltpu.SemaphoreType.DMA((carry["n_remote"],)), pltpu.SemaphoreType.DMA((carry["n_remote"],)),
            pltpu.SemaphoreType.DMA((max(carry["n_local"], 1),))]
    return (list(carry["ins"]), [_HBM] * len(carry["ins"]), [_HBM] * len(carry["outs"]), list(carry["outs"]), sems)


def _flash_fwd(q, k, v_t, n_heads, carry=None):
    S = q.shape[0]
    T = min(FLASH_T, S)
    grid = (n_heads, S // T)
    c_args, c_in_specs, c_out_specs, c_out_shapes, c_sems = _carry_call(carry)

    def body(q_ref, k_ref, vt_ref, o_ref, lse_ref, m_s, l_s, acc_t):
        i = pl.program_id(1)
        m_s[...] = jnp.full_like(m_s, -jnp.inf)
        l_s[...] = jnp.zeros_like(l_s)
        acc_t[...] = jnp.zeros_like(acc_t)
        qv = q_ref[...]

        def step(j, width, masked):
            keys = pl.ds(pl.multiple_of(j * T, T), width * T)
            s_t = _dot_nt(k_ref[keys, :], qv)
            if masked:
                s_t = jnp.where(_diag_mask(T), s_t, -jnp.inf)
            m_prev = m_s[...]
            m_new = jnp.maximum(m_prev, jnp.max(s_t, axis=0, keepdims=True))
            alpha = jnp.exp(m_prev - m_new)
            p_t = jnp.exp(s_t - m_new)
            l_s[...] = alpha * l_s[...] + jnp.sum(p_t, axis=0, keepdims=True)
            acc_t[...] = alpha * acc_t[...] + _dot(vt_ref[:, keys], p_t.astype(BF16))
            m_s[...] = m_new

        _walk_wide(0, i, lambda j, width: step(j, width, False), widest=FLASH_FWD_RUN)
        step(i, 1, True)
        o_ref[...] = (acc_t[...] / l_s[...]).T
        lse_ref[0] = m_s[...] + jnp.log(l_s[...])

    res = pl.pallas_call(
        _carried(body, 3, 2, 3, carry, grid), name="flash_fwd", grid=grid,
        in_specs=[pl.BlockSpec((T, QK_PAD), lambda h, i: (i, h)),
                  pl.BlockSpec((S, QK_PAD), lambda h, i: (0, h)),
                  pl.BlockSpec((V_DIM, S), lambda h, i: (h, 0))] + c_in_specs,
        out_specs=[pl.BlockSpec((T, V_DIM), lambda h, i: (i, h)),
                   pl.BlockSpec((1, 1, T), lambda h, i: (h, 0, i))] + c_out_specs,
        out_shape=[jax.ShapeDtypeStruct((S, n_heads * V_DIM), F32),
                   jax.ShapeDtypeStruct((n_heads, 1, S), F32)] + c_out_shapes,
        scratch_shapes=[pltpu.VMEM((1, T), F32), pltpu.VMEM((1, T), F32), pltpu.VMEM((V_DIM, T), F32)] + c_sems,
        compiler_params=_params(("arbitrary", "arbitrary")),
    )(q, k, v_t, *c_args)
    return res[0], res[1], list(res[2:])


def _flash_bwd(q, k, v, do, lse_row, delta_row, n_heads, carry=None):
    S = q.shape[0]
    T = min(FLASH_T, S)
    nq = S // T
    grid = (n_heads, S // T)
    c_args, c_in_specs, c_out_specs, c_out_shapes, c_sems = _carry_call(carry)

    def body(q_ref, k_ref, v_ref, do_ref, lse_ref, dl_ref, dq_ref, dk_ref, dv_ref, dk_acc, dv_acc):
        j = pl.program_id(1)

        @pl.when(j == 0)
        def _():
            dq_ref[...] = jnp.zeros_like(dq_ref)

        dk_acc[...] = jnp.zeros_like(dk_acc)
        dv_acc[...] = jnp.zeros_like(dv_acc)
        kv = k_ref[...]
        vv = v_ref[...]

        def step(i, width, masked):
            cols = pl.ds(pl.multiple_of(i * T, T), width * T)
            qt = q_ref[cols, :]
            dot = do_ref[cols, :]
            s_t = _dot_nt(kv, qt)
            if masked:
                s_t = jnp.where(_diag_mask(T), s_t, -jnp.inf)
            p_t = jnp.exp(s_t - lse_ref[0, :, cols])
            dv_acc[...] += _dot(p_t.astype(BF16), dot)
            ds_t = (p_t * (_dot_nt(vv, dot) - dl_ref[0, :, cols])).astype(BF16)
            dk_acc[...] += _dot(ds_t, qt)
            dq_ref[cols, :] += _dot_tn(ds_t, kv)

        step(j, 1, True)
        _walk_wide(j + 1, nq, lambda i, width: step(i, width, False), joint=False)
        dk_ref[...] = dk_acc[...]
        dv_ref[...] = dv_acc[...].astype(dv_ref.dtype)

        @pl.when(j == nq - 1)
        def _():
            dq_ref[...] = dq_ref[...] * _attn_scale()

    res = pl.pallas_call(
        _carried(body, 6, 3, 2, carry, grid), name="flash_bwd", grid=grid,
        in_specs=[pl.BlockSpec((S, QK_PAD), lambda h, j: (0, h)),
                  pl.BlockSpec((T, QK_PAD), lambda h, j: (j, h)),
                  pl.BlockSpec((T, V_DIM), lambda h, j: (j, h)),
                  pl.BlockSpec((S, V_DIM), lambda h, j: (0, h)),
                  pl.BlockSpec((1, 1, S), lambda h, j: (h, 0, 0)),
                  pl.BlockSpec((1, 1, S), lambda h, j: (h, 0, 0))] + c_in_specs,
        out_specs=[pl.BlockSpec((S, QK_PAD), lambda h, j: (0, h)),
                   pl.BlockSpec((T, QK_PAD), lambda h, j: (j, h)),
                   pl.BlockSpec((T, V_DIM), lambda h, j: (j, h))] + c_out_specs,
        out_shape=[jax.ShapeDtypeStruct((S, n_heads * QK_PAD), F32), jax.ShapeDtypeStruct((S, n_heads * QK_PAD), F32),
                   jax.ShapeDtypeStruct((S, n_heads * V_DIM), BF16)] + c_out_shapes,
        scratch_shapes=[pltpu.VMEM((T, QK_PAD), F32), pltpu.VMEM((T, V_DIM), F32)] + c_sems,
        compiler_params=_params(("arbitrary", "arbitrary")),
    )(q, k, v, do, lse_row, delta_row, *c_args)
    return res[0], res[1], res[2], list(res[3:])


def _st_delta(n_heads):
    def fn(do, o):
        prod = do * o
        lane = lax.broadcasted_iota(jnp.int32, (do.shape[0], LANES), 1)
        out = jnp.zeros((do.shape[0], LANES), F32)
        for h in range(n_heads):
            out = out + jnp.where(lane == h, jnp.sum(prod[:, h * V_DIM:(h + 1) * V_DIM], axis=1, keepdims=True), 0.0)
        return out
    return fn


def _pad_cols(w, width):
    return jnp.pad(w, ((0, 0), (0, width - w.shape[1])))


def _dims(x, p, q_norm, kv_norm, w_uq, dt_bias, ssm_norm, conv_b):
    d = dict(S=x.shape[0], D=x.shape[1], PLE=p.shape[1], DQ=q_norm.shape[1], DKV=kv_norm.shape[1],
             NH=w_uq.shape[1] // (QK_NOPE + QK_ROPE), NHS=dt_bias.shape[1], DI=ssm_norm.shape[1],
             CONV=conv_b.shape[1])
    d["G"] = (d["CONV"] - d["DI"]) // (2 * D_STATE)
    d["R"] = d["NHS"] // d["G"]
    return d


def _assemble(name, blocks):
    rows, cols = blocks.shape[1:]
    if name in COL_SHARDED:
        return blocks.transpose(1, 0, 2).reshape(rows, N_DEV * cols)
    return blocks.reshape(N_DEV * rows, cols)


def _by_device(name, g):
    if name in COL_SHARDED:
        return g.reshape(g.shape[0], N_DEV, g.shape[1] // N_DEV).transpose(1, 0, 2)
    return g.reshape(N_DEV, g.shape[0] // N_DEV, g.shape[1])


def _local_step(x, p, positions, target, small, conv_w, wts, late_names=(), late_shards=(), exchange=True):
    wts = dict(wts)
    dm = _dims(x, p, small["q_norm"], small["kv_norm"], wts["w_uq"], small["dt_bias"], small["ssm_norm"],
               small["conv_b"])
    S, D, DQ, DKV, NH, NHS, DI, CONV, G, R = (dm[k] for k in ("S", "D", "DQ", "DKV", "NH", "NHS", "DI", "CONV", "G", "R"))
    rp = -(-R // SUBLANES) * SUBLANES
    L = min(CHUNK, S)

    w_nat = wts["w_in"]
    o = [0]
    for n in (DQ, DKV, QK_ROPE, DI, CONV, NHS, D, D):
        o.append(o[-1] + n)
    pw = (DI, 2 * D, CONV, DQ + DKV, QK_ROPE + NHS)
    po = [sum(pw[:k]) for k in range(len(pw))]
    w_in = jnp.concatenate([w_nat[:, o[3]:o[4]], w_nat[:, o[6]:o[8]], w_nat[:, o[4]:o[5]], w_nat[:, o[0]:o[2]],
                            w_nat[:, o[2]:o[3]], w_nat[:, o[5]:o[6]]], axis=1)
    w_z, w_g, w_xbc, w_cqkv = (w_in[:, po[k]:po[k] + pw[k]] for k in range(4))
    w_kr, w_dt = w_in[:, po[4]:po[4] + QK_ROPE], w_in[:, po[4] + QK_ROPE:]
    zc = lambda n: jnp.zeros((D, n), BF16)
    w_sm = jnp.concatenate([zc(QK_NOPE), w_kr, zc(QK_PAD - QK_NOPE - QK_ROPE), w_dt, zc(LANES - NHS)], axis=1)
    w_q = jnp.pad(wts["w_uq"].reshape(DQ, NH, QK_NOPE + QK_ROPE),
                  ((0, 0), (0, 0), (0, QK_PAD - QK_NOPE - QK_ROPE))).reshape(DQ, NH * QK_PAD)
    ukv = wts["w_ukv"].reshape(DKV, NH, QK_NOPE + V_DIM)
    w_k = jnp.pad(ukv[:, :, :QK_NOPE], ((0, 0), (0, 0), (0, QK_PAD - QK_NOPE))).reshape(DKV, NH * QK_PAD)
    w_v = ukv[:, :, QK_NOPE:].reshape(DKV, NH * V_DIM)
    dt_bias_p, a_log_p = _pad_cols(small["dt_bias"], LANES), _pad_cols(small["a_log"], LANES)
    dskip_rep = jnp.repeat(small["d_skip"], SSM_HEADDIM, axis=1)
    c_tab, a_tab, b_tab = _rope_tables(positions)

    (u,) = _rowwise("pre_norm", _st_pre, [x], [small["mix_norm_pre"]], [(D, BF16)])
    cqkv = _mm("in_cqkv", u, w_cqkv, "nn")
    z = _mm("in_z", u, w_z, "nn")
    xbc = _mm("in_xbc", u, w_xbc, "nn")
    g = _mm("in_gates", u, w_g, "nn")
    sm = _mm("in_small", u, w_sm, "nn")

    lora_fn = _st_lora_norms(DQ)
    cq_n, ckv_n = _rowwise("lora_norms", lora_fn, [cqkv], [small["q_norm"], small["kv_norm"]], [(DQ, BF16), (DKV, BF16)])
    qraw = _mm("up_q", cq_n, w_q, "nn")
    kraw = _mm("up_k", ckv_n, w_k, "nn")
    v = _mm("up_v", ckv_n, w_v, "nn", out_dtype=BF16)
    v_t = _mm("up_v_t", w_v.T, ckv_n, "nt", out_dtype=BF16)
    q, k = _rowwise("rope", _st_rope(NH), [qraw, kraw, sm, c_tab, a_tab, b_tab], [],
                    [(NH * QK_PAD, BF16), (NH * QK_PAD, BF16)])
    attn, lse, late_blocks = _flash_fwd(q, k, v_t, NH, carry=_gather_carry(list(late_shards)) if late_names else None)
    wts.update({n: _assemble(n, b) for n, b in zip(late_names, late_blocks)})

    xbc_c, xbc_a = _conv_fwd(xbc, conv_w, small["conv_b"])
    dt, da = _rowwise("dt", _st_dt, [sm], [dt_bias_p, a_log_p], [(LANES, F32), (LANES, F32)])

    def col_layout(t):
        return _pad_cols(t[:, :NHS].reshape(S, G, R).transpose(1, 0, 2).reshape(G * S, R), LANES).reshape(G, S, LANES)

    def row_layout(t):
        return jnp.pad(t[:, :NHS].reshape(S, G, R).transpose(1, 2, 0), ((0, 0), (0, rp - R), (0, 0)))

    dt_row, da_col, da_row = row_layout(dt), col_layout(da), row_layout(da)
    xs = xbc_a[:, :DI]
    y, states = _ssd_fwd_t(xbc_a, dt_row, da_col, da_row, DI, G, R)
    gn_fn = _st_gated_norm(G)
    (ssm,) = _rowwise("gated_norm", gn_fn, [y, xs, z], [dskip_rep, small["ssm_norm"]], [(DI, BF16)])

    ao = _mm("attn_o", attn, wts["w_attn_o"], "nn")
    so = _mm("ssm_o", ssm, wts["w_ssm_o"], "nn")
    mix_fn = _st_mix(D)
    (mixed,) = _rowwise("mix", mix_fn, [g, ao, so], [], [(D, BF16)])
    mo = _mm("out_proj", mixed, wts["w_out"], "nn")
    h1, f = _rowwise("res1", _st_res_norm, [x, mo], [small["mix_norm_post"], small["ffn_norm_pre"]], [(D, F32), (D, BF16)])
    gt = _mm("ffn_gate", f, wts["w_gate"], "nn")
    up = _mm("ffn_up", f, wts["w_up"], "nn")
    (act,) = _rowwise("swiglu", _st_swiglu, [gt, up], [], [(gt.shape[1], BF16)])
    dn = _mm("ffn_down", act, wts["w_down"], "nn")
    h2, a3 = _rowwise("res2", _st_res_norm, [h1, dn], [small["ffn_norm_post"], small["ple_norm_pre"]], [(D, F32), (D, BF16)])
    gl = _mm("ple_gate", a3, wts["w_ple_gate"], "nn")
    pe = _mm("ple_proj", p, wts["w_ple"], "nn")

    sg = {}
    bg = {}
    dpe, dgl, dh2, d_w, loss_acc = _rowwise_bwd(
        "loss", _st_loss, [pe, gl, h2, target], [small["ple_norm_post"]], [1.0], [0, 1, 2], [BF16, BF16, F32], fwd_sums=(0,))
    sg["ple_norm_post"] = _fold(d_w)
    loss = loss_acc[0:1, :]
    bg["w_ple"] = _mm("d_w_ple", p, dpe, "tn", out_dtype=BF16)
    bg["w_ple_gate"] = _mm("d_w_ple_gate", a3, dgl, "tn", out_dtype=BF16)
    da3 = _mm("d_a3", dgl, wts["w_ple_gate"], "nt")

    dh1, ddn, d_post, d_pre = _rowwise_bwd(
        "res2_bwd", _st_res_norm, [h1, dn], [small["ffn_norm_post"], small["ple_norm_pre"]], [dh2, da3], [0, 1], [F32, BF16])
    sg["ffn_norm_post"], sg["ple_norm_pre"] = _fold(d_post), _fold(d_pre)
    bg["w_down"] = _mm("d_w_down", act, ddn, "tn", out_dtype=BF16)
    dact = _mm("d_act", ddn, wts["w_down"], "nt")
    dgt, dup = _rowwise_bwd("swiglu_bwd", _st_swiglu, [gt, up], [], [dact], [0, 1], [BF16, BF16])
    bg["w_gate"] = _mm("d_w_gate", f, dgt, "tn", out_dtype=BF16)
    bg["w_up"] = _mm("d_w_up", f, dup, "tn", out_dtype=BF16)
    df = _mm("d_f_gate", dgt, wts["w_gate"], "nt")
    df = _mm("d_f_up", dup, wts["w_up"], "nt", acc_in=df)

    dx_res, dmo, d_post, d_pre = _rowwise_bwd(
        "res1_bwd", _st_res_norm, [x, mo], [small["mix_norm_post"], small["ffn_norm_pre"]], [dh1, df], [0, 1], [F32, BF16])
    sg["mix_norm_post"], sg["ffn_norm_pre"] = _fold(d_post), _fold(d_pre)
    bg["w_out"] = _mm("d_w_out", mixed, dmo, "tn", out_dtype=BF16)
    dmixed = _mm("d_mixed", dmo, wts["w_out"], "nt")
    assert po[1] % pw[1] == 0 and po[3] % pw[3] == 0, (po, pw)
    dproj = lax.empty((S, sum(pw)), BF16)
    dproj, dao, dso = _rowwise_bwd("mix_bwd", mix_fn, [g, ao, so], [], [dmixed], [0, 1, 2], [BF16, BF16, BF16],
                                   into={0: (dproj, po[1] // pw[1])})
    bg["w_attn_o"] = _mm("d_w_attn_o", attn, dao, "tn", out_dtype=BF16)
    bg["w_ssm_o"] = _mm("d_w_ssm_o", ssm, dso, "tn", out_dtype=BF16)
    dattn = _mm("d_attn", dao, wts["w_attn_o"], "nt", out_dtype=BF16)
    dssm = _mm("d_ssm", dso, wts["w_ssm_o"], "nt")

    dy, dxs_a, dproj, d_dskip, d_ssmn = _rowwise_bwd(
        "gated_norm_bwd", gn_fn, [y, xs, z], [dskip_rep, small["ssm_norm"]], [dssm], [0, 1, 2], [F32, F32, BF16],
        into={2: (dproj, 0)})
    sg["d_skip"] = _fold(d_dskip).reshape(NHS, SSM_HEADDIM).sum(axis=1).reshape(1, NHS)
    sg["ssm_norm"] = _fold(d_ssmn)
    dxs_b, d_b, d_c, ddt_row, dcum_row = _ssd_bwd_t(xbc_a, dt_row, da_col, da_row, states, dy, DI, G, R)

    def from_row(t):
        return _pad_cols(t[:, :R, :].transpose(2, 0, 1).reshape(S, NHS), LANES)

    ddtraw, d_bias, d_alog = _rowwise("dt_bwd", _st_dt_bwd, [sm, from_row(ddt_row), from_row(dcum_row)],
                                      [dt_bias_p, a_log_p], [(LANES, F32)], accs=(LANES, LANES), tr=L)
    sg["dt_bias"], sg["a_log"] = _fold(d_bias)[:, :NHS], _fold(d_alog)[:, :NHS]
    dconv, d_cb = _rowwise("dconv", _st_dconv(DI), [xbc_c, dxs_a, dxs_b, d_b, d_c], [], [(CONV, F32)], accs=(CONV,))
    sg["conv_b"] = _fold(d_cb)
    dproj, d_cw = _conv_bwd(xbc, dconv, conv_w, dproj, po[2])
    d_conv_w = d_cw.reshape(CONV_WIDTH, SUBLANES, CONV).sum(axis=1)

    (delta,) = _rowwise("attn_delta", _st_delta(NH), [dattn, attn], [], [(LANES, F32)])
    delta = delta[:, :NH].T
    late_sent = [_by_device(n, bg.pop(n)) for n in late_names]
    dq, dk, dv, late_recv = _flash_bwd(q, k, v, dattn, lse, delta.reshape(NH, 1, S), NH,
                                       carry=_scatter_carry(late_sent) if late_names else None)
    dqraw, dkr, dk = _rowwise("rope_bwd", _st_rope_bwd(NH), [dq, dk, c_tab, a_tab, b_tab], [],
                              [(NH * QK_PAD, BF16), (QK_PAD, F32), (NH * QK_PAD, BF16)])
    d_w_q = _mm("d_w_q", cq_n, dqraw, "tn", out_dtype=BF16)
    d_w_k = _mm("d_w_k", ckv_n, dk, "tn", out_dtype=BF16)
    d_w_v = _mm("d_w_v", ckv_n, dv, "tn", out_dtype=BF16)
    dcq_n = _mm("d_cq_n", dqraw, w_q, "nt")
    dckv_n = _mm("d_ckv_n_k", dk, w_k, "nt")
    dckv_n = _mm("d_ckv_n_v", dv, w_v, "nt", acc_in=dckv_n)
    bg["w_uq"] = d_w_q.reshape(DQ, NH, QK_PAD)[:, :, :QK_NOPE + QK_ROPE].reshape(DQ, NH * (QK_NOPE + QK_ROPE))
    bg["w_ukv"] = jnp.concatenate([d_w_k.reshape(DKV, NH, QK_PAD)[:, :, :QK_NOPE], d_w_v.reshape(DKV, NH, V_DIM)],
                                  axis=2).reshape(DKV, NH * (QK_NOPE + V_DIM))
    dproj, d_qn, d_kvn = _rowwise_bwd("lora_norms_bwd", lora_fn, [cqkv], [small["q_norm"], small["kv_norm"]],
                                      [dcq_n, dckv_n], [0], [BF16], into={0: (dproj, po[3] // pw[3])})
    sg["q_norm"], sg["kv_norm"] = _fold(d_qn), _fold(d_kvn)

    d_small = jnp.concatenate([dkr[:, QK_NOPE:QK_NOPE + QK_ROPE], ddtraw[:, :NHS]], axis=1).astype(BF16)
    dproj = lax.dynamic_update_slice(dproj, d_small, (0, po[4]))
    d_w = _mm("d_w_in", u, dproj, "tn", out_dtype=BF16)
    kr = po[4] + QK_ROPE
    bg["w_in"] = jnp.concatenate([d_w[:, po[3]:po[4]], d_w[:, po[4]:kr], d_w[:, po[0]:po[1]], d_w[:, po[2]:po[3]],
                                  d_w[:, kr:kr + NHS], d_w[:, po[1]:po[2]]], axis=1)
    rest_names = tuple(n for n in BIG if n in bg)
    if not exchange:
        du = _mm("d_u", dproj, w_in, "nt")
        grad_x, d_pre = _rowwise_bwd("pre_norm_bwd", _st_pre, [x], [small["mix_norm_pre"]], [du], [0], [F32],
                                     adds={0: dx_res})
        sg["mix_norm_pre"] = _fold(d_pre)
        return loss, grad_x, sg, d_conv_w, bg, {}
    rest_sent = [_by_device(n, bg.pop(n)) for n in rest_names]
    du, rest_recv = _mm("d_u", dproj, w_in, "nt", carry=_scatter_carry(rest_sent))
    grad_x, d_pre = _rowwise_bwd("pre_norm_bwd", _st_pre, [x], [small["mix_norm_pre"]], [du], [0], [F32], adds={0: dx_res})
    sg["mix_norm_pre"] = _fold(d_pre)
    sent = dict(zip(late_names, late_sent), **dict(zip(rest_names, rest_sent)))
    recv = dict(zip(late_names, late_recv), **dict(zip(rest_names, rest_recv)))
    return loss, grad_x, sg, d_conv_w, sent, recv


_HBM = pl.BlockSpec(memory_space=pltpu.HBM)
_FLIPS = ((0, 0, 1), (1, 0, 0), (0, 1, 0), (1, 1, 0), (1, 0, 1), (0, 1, 1), (1, 1, 1))


def _place():
    return lax.axis_index("x"), lax.axis_index("y"), lax.axis_index("c")


def _flipped(place, flip):
    return tuple(1 - v if f else v for v, f in zip(place, flip))


def _gather_carry(blocks):
    nw = len(blocks)

    def copies(ins, outs, send_sems, recv_sems, local_sems):
        x, y, c = _place()
        me = 4 * x + 2 * y + c
        cps = []
        for w in range(nw):
            cps.append(pltpu.make_async_copy(ins[w], outs[w].at[me], local_sems.at[w]))
            for k, flip in enumerate(_FLIPS):
                cps.append(pltpu.make_async_remote_copy(
                    src_ref=ins[w], dst_ref=outs[w].at[me], send_sem=send_sems.at[7 * w + k],
                    recv_sem=recv_sems.at[7 * w + k], device_id=_flipped((x, y, c), flip), device_id_type=MESH))
        return cps

    return dict(ins=blocks, outs=[jax.ShapeDtypeStruct((N_DEV,) + b.shape, b.dtype) for b in blocks],
                n_remote=7 * nw, n_local=nw, copies=copies)


def _scatter_carry(by_dev):
    nw = len(by_dev)

    def copies(ins, outs, send_sems, recv_sems, local_sems):
        x, y, c = _place()
        cps = []
        for w in range(nw):
            for k, flip in enumerate(_FLIPS):
                px, py, pc = _flipped((x, y, c), flip)
                cps.append(pltpu.make_async_remote_copy(
                    src_ref=ins[w].at[4 * px + 2 * py + pc], dst_ref=outs[w].at[k], send_sem=send_sems.at[7 * w + k],
                    recv_sem=recv_sems.at[7 * w + k], device_id=(px, py, pc), device_id_type=MESH))
        return cps

    return dict(ins=by_dev, outs=[jax.ShapeDtypeStruct((7,) + b.shape[1:], b.dtype) for b in by_dev],
                n_remote=7 * nw, n_local=0, copies=copies)


def _all_gather(name, blocks):
    nw = len(blocks)

    def body(*refs):
        x_refs, out_refs = refs[:nw], refs[nw:2 * nw]
        send_sems, recv_sems, local_sems = refs[2 * nw:]
        x, y, c = _place()
        me, sibling = (x, y, c), (x, y, 1 - c)
        chips = [(1 - x, y), (x, 1 - y), (1 - x, 1 - y)]

        def slot(w, px, py, pc):
            return out_refs[w].at[4 * px + 2 * py + pc]

        def copy(w, k, blk, to, src=None):
            return pltpu.make_async_remote_copy(
                src_ref=slot(w, *blk) if src is None else src, dst_ref=slot(w, *blk),
                send_sem=send_sems.at[7 * w + k], recv_sem=recv_sems.at[7 * w + k], device_id=to, device_id_type=MESH)

        mine = [pltpu.make_async_copy(x_refs[w], slot(w, *me), local_sems.at[w]) for w in range(nw)]
        for cp in mine:
            cp.start()
        first = []
        for w in range(nw):
            first.append(copy(w, 0, me, sibling, src=x_refs[w]))
            first += [copy(w, 1 + j, me, (*chip, c), src=x_refs[w]) for j, chip in enumerate(chips)]
        for cp in first:
            cp.start()
        passed = []
        for j, chip in enumerate(chips):
            for w in range(nw):
                copy(w, 1 + j, (*chip, c), me).wait_recv()
                passed.append(copy(w, 4 + j, (*chip, c), sibling))
                passed[-1].start()
        for w in range(nw):
            copy(w, 0, sibling, me).wait_recv()
        for j, chip in enumerate(chips):
            for w in range(nw):
                copy(w, 4 + j, (*chip, 1 - c), me).wait_recv()
        for cp in first + passed:
            cp.wait_send()
        for cp in mine:
            cp.wait()

    return pl.pallas_call(
        body, name=name, out_shape=[jax.ShapeDtypeStruct((N_DEV,) + b.shape, b.dtype) for b in blocks],
        in_specs=[_HBM] * nw, out_specs=[_HBM] * nw,
        scratch_shapes=[pltpu.SemaphoreType.DMA((7 * nw,)), pltpu.SemaphoreType.DMA((7 * nw,)),
                        pltpu.SemaphoreType.DMA((nw,))],
    )(*blocks)


def _lane_pad(n):
    return -(-n // LANES) * LANES


def _pack_small(vecs, mat):
    width = max(sum(_lane_pad(v.shape[1]) for v in vecs), _lane_pad(mat.shape[1]))
    row0 = jnp.concatenate([_pad_cols(v, _lane_pad(v.shape[1])) for v in vecs], axis=1)
    rows = jnp.concatenate([_pad_cols(row0, width), _pad_cols(mat, width)], axis=0)
    return jnp.pad(rows, ((0, SUBLANES - rows.shape[0]), (0, 0)))


def _unpack_small(packed, sizes, mat_cols):
    vecs, off = [], 0
    for n in sizes:
        vecs.append(packed[0:1, off:off + n])
        off += _lane_pad(n)
    return vecs, packed[1:1 + CONV_WIDTH, :mat_cols]


def _adamw(w, g, m, v):
    m = ADAM_B1 * m + (1.0 - ADAM_B1) * g
    v = ADAM_B2 * v + (1.0 - ADAM_B2) * (g * g)
    m_hat = m / (1.0 - ADAM_B1 ** ADAM_STEP)
    v_hat = v / (1.0 - ADAM_B2 ** ADAM_STEP)
    delta = -ADAM_LR * (m_hat / (jnp.sqrt(v_hat) + ADAM_EPS) + ADAM_WD * w)
    return delta, m, v


BIG = ("w_in", "w_uq", "w_ukv", "w_attn_o", "w_ssm_o", "w_out", "w_gate", "w_up", "w_down", "w_ple_gate", "w_ple")
FIRST = ("w_in", "w_uq", "w_ukv")
LATE = ("w_attn_o", "w_ssm_o", "w_out", "w_gate", "w_up", "w_down", "w_ple_gate", "w_ple")
COL_SHARDED = ("w_in", "w_uq", "w_ukv", "w_gate", "w_up", "w_ple")
SMALL = ("mix_norm_pre", "mix_norm_post", "q_norm", "kv_norm", "conv_b", "dt_bias", "a_log", "d_skip", "ssm_norm",
         "ffn_norm_pre", "ffn_norm_post", "ple_norm_pre", "ple_norm_post")
WEIGHTS = ("mix_norm_pre", "mix_norm_post", "w_in", "q_norm", "w_uq", "kv_norm", "w_ukv", "conv_w", "conv_b", "dt_bias",
           "a_log", "d_skip", "ssm_norm", "w_attn_o", "w_ssm_o", "w_out", "ffn_norm_pre", "ffn_norm_post", "w_gate",
           "w_up", "w_down", "ple_norm_pre", "ple_norm_post", "w_ple_gate", "w_ple")


def _step(x, p, positions, target, w, m, v):
    xi, yi, ci = _place()
    me = 4 * xi + 2 * yi + ci

    gathered = _all_gather("gather_weights", [w[n].astype(BF16) for n in FIRST])
    wts = {n: _assemble(n, blocks) for n, blocks in zip(FIRST, gathered)}
    cw_rows, cw_cols = w["conv_w"].shape
    (cw_all,) = _all_gather("gather_conv_w", [jnp.pad(w["conv_w"], ((0, SUBLANES - cw_rows), (0, 0)))])
    conv_w = cw_all[:, :cw_rows, :].transpose(1, 0, 2).reshape(cw_rows, N_DEV * cw_cols)

    small = {n: w[n] for n in SMALL}
    loss, grad_x, sg, d_conv_w, sent, recv = _local_step(
        x, p, positions, target, small, conv_w, wts, LATE, [w[n].astype(BF16) for n in LATE])

    sizes = [w[n].shape[1] for n in SMALL]
    sg_pack = _pack_small([sg[n] for n in SMALL] + [loss], d_conv_w)
    (sg_all,) = _all_gather("gather_small_grads", [sg_pack])
    (sg_sum,) = _rowwise("sum_small_grads", lambda *a: functools.reduce(lambda s, t: s + t, a),
                         [sg_all[k] for k in range(N_DEV)], [], [(sg_pack.shape[1], F32)])
    sg_vecs, d_conv_w_sum = _unpack_small(sg_sum, sizes + [LANES], d_conv_w.shape[1])
    loss = sg_vecs[-1][0, 0]
    grads = dict(zip(SMALL, sg_vecs[:-1]))
    grads["conv_w"] = lax.dynamic_slice_in_dim(d_conv_w_sum, me * cw_cols, cw_cols, axis=1)

    def sum8_then_adamw(wv, mv, vv, own, *others):
        g = functools.reduce(lambda s, t: s + t, others, own)
        return (g,) + _adamw(wv, g, mv, vv)

    delta, new_m, new_v = {}, {}, {}
    for n in BIG:
        cols = w[n].shape[1]
        own = lax.dynamic_index_in_dim(sent[n], me, axis=0, keepdims=False)
        grads[n], delta[n], new_m[n], new_v[n] = _rowwise(
            "adamw_" + n, sum8_then_adamw, [w[n], m[n], v[n], own] + [recv[n][k] for k in range(N_DEV - 1)], [],
            [(cols, F32)] * 4)
    packed = [_pack_small([d[n] for n in SMALL], d["conv_w"]) for d in (w, grads, m, v)]
    outs = _rowwise("adamw_small", _adamw, packed, [], [(packed[0].shape[1], F32)] * 3)
    for d, o in zip((delta, new_m, new_v), outs):
        vecs, mat = _unpack_small(o, sizes, cw_cols)
        d.update(zip(SMALL, vecs))
        d["conv_w"] = mat
    return loss, grad_x, grads, delta, new_m, new_v


def kernel(x, p, positions, mix_norm_pre, mix_norm_post, w_in, q_norm, w_uq, kv_norm, w_ukv, conv_w, conv_b, dt_bias, a_log, d_skip, ssm_norm, w_attn_o, w_ssm_o, w_out, ffn_norm_pre, ffn_norm_post, w_gate, w_up, w_down, ple_norm_pre, ple_norm_post, w_ple_gate, w_ple, loss_target, m_mix_norm_pre, m_mix_norm_post, m_w_in, m_q_norm, m_w_uq, m_kv_norm, m_w_ukv, m_conv_w, m_conv_b, m_dt_bias, m_a_log, m_d_skip, m_ssm_norm, m_w_attn_o, m_w_ssm_o, m_w_out, m_ffn_norm_pre, m_ffn_norm_post, m_w_gate, m_w_up, m_w_down, m_ple_norm_pre, m_ple_norm_post, m_w_ple_gate, m_w_ple, v_mix_norm_pre, v_mix_norm_post, v_w_in, v_q_norm, v_w_uq, v_kv_norm, v_w_ukv, v_conv_w, v_conv_b, v_dt_bias, v_a_log, v_d_skip, v_ssm_norm, v_w_attn_o, v_w_ssm_o, v_w_out, v_ffn_norm_pre, v_ffn_norm_post, v_w_gate, v_w_up, v_w_down, v_ple_norm_pre, v_ple_norm_post, v_w_ple_gate, v_w_ple):
    w_args = (mix_norm_pre, mix_norm_post, w_in, q_norm, w_uq, kv_norm, w_ukv, conv_w, conv_b, dt_bias, a_log, d_skip, ssm_norm, w_attn_o, w_ssm_o, w_out, ffn_norm_pre, ffn_norm_post, w_gate, w_up, w_down, ple_norm_pre, ple_norm_post, w_ple_gate, w_ple)
    m_args = (m_mix_norm_pre, m_mix_norm_post, m_w_in, m_q_norm, m_w_uq, m_kv_norm, m_w_ukv, m_conv_w, m_conv_b, m_dt_bias, m_a_log, m_d_skip, m_ssm_norm, m_w_attn_o, m_w_ssm_o, m_w_out, m_ffn_norm_pre, m_ffn_norm_post, m_w_gate, m_w_up, m_w_down, m_ple_norm_pre, m_ple_norm_post, m_w_ple_gate, m_w_ple)
    v_args = (v_mix_norm_pre, v_mix_norm_post, v_w_in, v_q_norm, v_w_uq, v_kv_norm, v_w_ukv, v_conv_w, v_conv_b, v_dt_bias, v_a_log, v_d_skip, v_ssm_norm, v_w_attn_o, v_w_ssm_o, v_w_out, v_ffn_norm_pre, v_ffn_norm_post, v_w_gate, v_w_up, v_w_down, v_ple_norm_pre, v_ple_norm_post, v_w_ple_gate, v_w_ple)

    def drop_layer(a):
        return a if a.ndim == 2 else a[0]

    w = {n: drop_layer(a) for n, a in zip(WEIGHTS, w_args)}
    m = {n: drop_layer(a) for n, a in zip(WEIGHTS, m_args)}
    v = {n: drop_layer(a) for n, a in zip(WEIGHTS, v_args)}
    loss, grad_x, grads, delta, new_m, new_v = _step(x[0], p[0, 0], positions[0], loss_target[0], w, m, v)
    like = lambda d: [d[n].reshape(a.shape) for n, a in zip(WEIGHTS, w_args)]
    return (loss, grad_x[None], *like(grads), *like(delta), *like(new_m), *like(new_v))
```

```python
import functools

import jax
import jax.numpy as jnp
from jax import lax
from jax.experimental import pallas as pl
from jax.experimental.pallas import tpu as pltpu

F32 = jnp.float32
BF16 = jnp.bfloat16

EPS = 1e-6
QK_NOPE = 128
QK_ROPE = 64
V_DIM = 128
QK_PAD = 256
ROPE_THETA = 10000.0
SSM_HEADDIM = 64
D_STATE = 128
CONV_WIDTH = 4
CHUNK = 256
ADAM_LR = 0.001
ADAM_B1 = 0.9
ADAM_B2 = 0.999
ADAM_EPS = 1e-08
ADAM_WD = 0.01
ADAM_STEP = 10

N_DEV = 8
LANES = 128
SUBLANES = 8
VMEM_LIMIT = 56 * 1024 * 1024
ROW_TILE_BYTES = 16 * 1024 * 1024
STRIP_ROWS = 16
STRIP_ELEMS = 128 * 1024
FLASH_T = 512
FLASH_FWD_RUN = 8
MM_TILE_BYTES = 20 * 1024 * 1024
MESH = pl.DeviceIdType.MESH


def _pick(dim, prefs):
    if dim <= prefs[0]:
        return dim
    for p in prefs:
        if dim % p == 0:
            return p
    return dim


def _tile(dim, cap):
    if dim <= cap:
        return dim
    best = None
    for t in range(LANES, cap + 1, LANES):
        if dim % t == 0:
            best = t
    return best if best is not None else dim


def _params(sem):
    return pltpu.CompilerParams(dimension_semantics=sem, vmem_limit_bytes=VMEM_LIMIT)


def _dot(a, b):
    return lax.dot_general(a, b, (((1,), (0,)), ((), ())), preferred_element_type=F32)


def _dot_nt(a, b):
    return lax.dot_general(a, b, (((1,), (1,)), ((), ())), preferred_element_type=F32)


def _dot_tn(a, b):
    return lax.dot_general(a, b, (((0,), (0,)), ((), ())), preferred_element_type=F32)


def _mm(name, a, b, mode, out_dtype=F32, acc_in=None, carry=None):
    if mode == "nn":
        (M, K), (K2, N) = a.shape, b.shape
    elif mode == "nt":
        (M, K), (N, K2) = a.shape, b.shape
    else:
        (K, M), (K2, N) = a.shape, b.shape
    assert K == K2, (name, a.shape, b.shape, mode)
    tm = _tile(M, 1024)
    tn = _tile(N, 1024 if acc_in is not None else 1536)
    tk = _tile(K, 2048)
    while tk > 512 and 2 * (tm * tk * a.dtype.itemsize + tk * tn * b.dtype.itemsize) > MM_TILE_BYTES:
        tk = _tile(K, tk - LANES)
    nk = K // tk
    dot = {"nn": _dot, "nt": _dot_nt, "tn": _dot_tn}[mode]
    has_acc = acc_in is not None

    def body(*refs):
        if has_acc:
            a_ref, b_ref, c_ref, o_ref, acc = refs
        else:
            a_ref, b_ref, o_ref, acc = refs
        k = pl.program_id(2)

        def product():
            return dot(a_ref[...].astype(BF16), b_ref[...].astype(BF16))

        def finish(r):
            if has_acc:
                r = r + c_ref[...]
            o_ref[...] = r.astype(o_ref.dtype)

        if nk == 1:
            finish(product())
            return

        @pl.when(k == 0)
        def _():
            acc[...] = product()

        @pl.when(jnp.logical_and(k > 0, k < nk - 1))
        def _():
            acc[...] += product()

        @pl.when(k == nk - 1)
        def _():
            finish(acc[...] + product())

    if mode == "tn":
        a_spec = pl.BlockSpec((tk, tm), lambda i, j, k: (k, i))
    else:
        a_spec = pl.BlockSpec((tm, tk), lambda i, j, k: (i, k))
    if mode == "nt":
        b_spec = pl.BlockSpec((tn, tk), lambda i, j, k: (j, k))
    else:
        b_spec = pl.BlockSpec((tk, tn), lambda i, j, k: (k, j))
    o_spec = pl.BlockSpec((tm, tn), lambda i, j, k: (i, j))
    in_specs = [a_spec, b_spec] + ([o_spec] if has_acc else [])
    args = (a, b) + ((acc_in,) if has_acc else ())
    if carry is None:
        return pl.pallas_call(
            body, name=name, grid=(M // tm, N // tn, nk), in_specs=in_specs, out_specs=o_spec,
            out_shape=jax.ShapeDtypeStruct((M, N), out_dtype), scratch_shapes=[pltpu.VMEM((tm, tn), F32)],
            input_output_aliases=({2: 0} if has_acc and out_dtype == F32 else {}),
            compiler_params=_params(("parallel", "parallel", "arbitrary")),
        )(*args)
    grid = (M // tm, N // tn, nk)
    c_args, c_in_specs, c_out_specs, c_out_shapes, c_sems = _carry_call(carry)
    res = pl.pallas_call(
        _carried(body, len(args), 1, 1, carry, grid), name=name, grid=grid, in_specs=in_specs + c_in_specs,
        out_specs=[o_spec] + c_out_specs, out_shape=[jax.ShapeDtypeStruct((M, N), out_dtype)] + c_out_shapes,
        scratch_shapes=[pltpu.VMEM((tm, tn), F32)] + c_sems,
        compiler_params=_params(("arbitrary", "arbitrary", "arbitrary")),
    )(*args, *c_args)
    return res[0], list(res[1:])


def _row_tile(n_rows, bytes_per_row):
    tr = 1024
    while tr > SUBLANES and tr * bytes_per_row > ROW_TILE_BYTES:
        tr //= 2
    while n_rows % tr:
        tr //= 2
    return tr


def _strip_rows(width):
    return max(STRIP_ROWS, min(4 * STRIP_ROWS, STRIP_ELEMS // width // STRIP_ROWS * STRIP_ROWS))


def _over_strips(tr, strip, work):
    if strip is None or tr <= strip or tr % strip:
        work(slice(None))
        return

    def one(s, carry):
        work(pl.ds(pl.multiple_of(s * strip, strip), strip))
        return carry

    lax.fori_loop(0, tr // strip, one, 0)


def _acc_add(a_ref, v):
    if v.shape[0] == 1:
        a_ref[0:1, :] += v
    else:
        a_ref[...] += v.reshape(v.shape[0] // SUBLANES, SUBLANES, v.shape[1]).sum(axis=0)


def _rowwise(name, fn, rows, bcs, outs, accs=(), tr=None):
    n_rows = rows[0].shape[0]
    strip = _strip_rows(max(r.shape[1] for r in rows)) if tr is None else None
    if tr is None:
        per_row = sum(r.shape[1] * r.dtype.itemsize for r in rows) + sum(w * jnp.dtype(d).itemsize for w, d in outs)
        tr = _row_tile(n_rows, per_row)
    n_r, n_b, n_o, n_a = len(rows), len(bcs), len(outs), len(accs)

    def body(*refs):
        r_refs, b_refs = refs[:n_r], refs[n_r: n_r + n_b]
        o_refs = refs[n_r + n_b: n_r + n_b + n_o]
        a_refs = refs[n_r + n_b + n_o:]
        if n_a:
            @pl.when(pl.program_id(0) == 0)
            def _():
                for a in a_refs:
                    a[...] = jnp.zeros_like(a)

        def work(rws):
            ins = [r[rws, :].astype(F32) for r in r_refs] + [b[...].astype(F32) for b in b_refs]
            res = fn(*ins)
            res = res if isinstance(res, (tuple, list)) else (res,)
            for o, v in zip(o_refs, res[:n_o]):
                o[rws, :] = v.astype(o.dtype)
            for a, v in zip(a_refs, res[n_o:]):
                _acc_add(a, v)

        _over_strips(tr, strip, work)

    in_specs = [pl.BlockSpec((tr, r.shape[1]), lambda i: (i, 0)) for r in rows]
    in_specs += [pl.BlockSpec((1, b.shape[1]), lambda i: (0, 0)) for b in bcs]
    out_specs = [pl.BlockSpec((tr, w), lambda i: (i, 0)) for w, _ in outs]
    out_specs += [pl.BlockSpec((SUBLANES, w), lambda i: (0, 0)) for w in accs]
    out_shape = [jax.ShapeDtypeStruct((n_rows, w), d) for w, d in outs]
    out_shape += [jax.ShapeDtypeStruct((SUBLANES, w), F32) for w in accs]
    res = pl.pallas_call(
        body, name=name, grid=(n_rows // tr,), in_specs=in_specs, out_specs=out_specs, out_shape=out_shape,
        compiler_params=_params(("arbitrary",) if n_a else ("parallel",)),
    )(*rows, *bcs)
    return tuple(res)


def _rowwise_bwd(name, fn, rows, bcs, cts, need_rows, row_dtypes, need_bcs=None, adds=None, fwd_sums=(), tr=None,
                 into=None):
    n_rows = rows[0].shape[0]
    adds = adds or {}
    into = into or {}
    into_keys = sorted(into)
    need_bcs = list(range(len(bcs))) if need_bcs is None else list(need_bcs)
    ct_arrays = [c for c in cts if not isinstance(c, float)]
    add_keys = sorted(adds)
    add_arrays = [adds[k] for k in add_keys]
    strip = _strip_rows(max(r.shape[1] for r in rows)) if tr is None else None
    if tr is None:
        per_row = sum(r.shape[1] * r.dtype.itemsize for r in list(rows) + ct_arrays + add_arrays)
        per_row += sum(rows[i].shape[1] * jnp.dtype(d).itemsize for i, d in zip(need_rows, row_dtypes))
        tr = _row_tile(n_rows, per_row)
    n_r, n_b, n_c, n_ad = len(rows), len(bcs), len(ct_arrays), len(add_arrays)
    n_go, n_gb, n_fs = len(need_rows), len(need_bcs), len(fwd_sums)

    def body(*refs):
        pos = [0]

        def take(n):
            pos[0] += n
            return refs[pos[0] - n: pos[0]]

        r_refs, b_refs, c_refs, ad_refs = take(n_r), take(n_b), take(n_c), take(n_ad)
        take(len(into_keys))
        go_refs = take(n_go)
        acc_refs = refs[pos[0]:]

        @pl.when(pl.program_id(0) == 0)
        def _():
            for a in acc_refs:
                a[...] = jnp.zeros_like(a)

        def wrapped(*a):
            r = fn(*a)
            return tuple(r) if isinstance(r, (tuple, list)) else (r,)

        def work(rws):
            r_t = [r[rws, :].astype(F32) for r in r_refs]
            b_t = [r[...].astype(F32) for r in b_refs]
            outs, vjp = jax.vjp(wrapped, *r_t, *b_t)
            it = iter(c_refs)
            full = tuple(jnp.full(o.shape, c, F32) if isinstance(c, float) else next(it)[rws, :].astype(F32)
                         for o, c in zip(outs, cts))
            grads = vjp(full)
            for o_ref, i in zip(go_refs, need_rows):
                g = grads[i]
                if i in adds:
                    g = g + ad_refs[add_keys.index(i)][rws, :].astype(F32)
                o_ref[rws, :] = g.astype(o_ref.dtype)
            for a, j in zip(acc_refs[:n_gb], need_bcs):
                _acc_add(a, grads[n_r + j])
            for a, j in zip(acc_refs[n_gb:], fwd_sums):
                a[0:1, :] += jnp.full((1, LANES), jnp.sum(outs[j]), F32)

        _over_strips(tr, strip, work)

    def row_spec(w):
        return pl.BlockSpec((tr, w), lambda i: (i, 0))

    in_specs = [row_spec(r.shape[1]) for r in rows]
    in_specs += [pl.BlockSpec((1, b.shape[1]), lambda i: (0, 0)) for b in bcs]
    in_specs += [row_spec(c.shape[1]) for c in ct_arrays] + [row_spec(a.shape[1]) for a in add_arrays]
    in_specs += [pl.BlockSpec(memory_space=pl.ANY) for _ in into_keys]
    out_specs = [row_spec(rows[i].shape[1]) for i in need_rows]
    out_shape = [jax.ShapeDtypeStruct((n_rows, rows[i].shape[1]), d) for i, d in zip(need_rows, row_dtypes)]
    aliases = {}
    for pos_in, k in enumerate(into_keys):
        buf, col_block = into[k]
        out_specs[k] = pl.BlockSpec((tr, rows[need_rows[k]].shape[1]), lambda i, cb=col_block: (i, cb))
        out_shape[k] = jax.ShapeDtypeStruct(buf.shape, buf.dtype)
        aliases[len(in_specs) - len(into_keys) + pos_in] = k
    out_specs += [pl.BlockSpec((SUBLANES, bcs[j].shape[1]), lambda i: (0, 0)) for j in need_bcs]
    out_specs += [pl.BlockSpec((SUBLANES, LANES), lambda i: (0, 0)) for _ in fwd_sums]
    out_shape += [jax.ShapeDtypeStruct((SUBLANES, bcs[j].shape[1]), F32) for j in need_bcs]
    out_shape += [jax.ShapeDtypeStruct((SUBLANES, LANES), F32) for _ in fwd_sums]
    res = pl.pallas_call(
        body, name=name, grid=(n_rows // tr,), in_specs=in_specs, out_specs=out_specs, out_shape=out_shape,
        input_output_aliases=aliases, compiler_params=_params(("arbitrary",)),
    )(*rows, *bcs, *ct_arrays, *add_arrays, *[into[k][0] for k in into_keys])
    return tuple(res)


def _fold(acc):
    return jnp.sum(acc, axis=0, keepdims=True)


def _rms(x, w):
    return x * lax.rsqrt(jnp.mean(x * x, axis=-1, keepdims=True) + EPS) * w


def _sigmoid(x):
    return jax.nn.sigmoid(x)


def _silu(x):
    return x * _sigmoid(x)


def _log1p(u):
    series = u * (1.0 - u * (0.5 - u * (1.0 / 3.0 - u * 0.25)))
    return jnp.where(u < 0.01, series, jnp.log(1.0 + u))


def _softplus(x):
    return jnp.maximum(x, 0.0) + _log1p(jnp.exp(-jnp.abs(x)))


def _st_pre(x, w):
    return _rms(x, w)


def _st_lora_norms(dq):
    def fn(cqkv, qn, kvn):
        return _rms(cqkv[:, :dq], qn), _rms(cqkv[:, dq:], kvn)
    return fn


def _st_gated_norm(n_groups):
    def fn(y, xs, z, dskip, wn):
        yz = (y + dskip * xs) * _silu(z)
        gw = yz.shape[1] // n_groups
        parts = [_rms(yz[:, g * gw:(g + 1) * gw], wn[:, g * gw:(g + 1) * gw]) for g in range(n_groups)]
        return jnp.concatenate(parts, axis=1)
    return fn


def _st_mix(d):
    def fn(g, ao, so):
        return _sigmoid(g[:, :d]) * ao + _sigmoid(g[:, d:]) * so
    return fn


def _st_res_norm(h, y, w_post, w_pre):
    h2 = h + _rms(y, w_post)
    return h2, _rms(h2, w_pre)


def _st_swiglu(gt, up):
    return _silu(gt) * up


def _st_loss(pe, gl, h2, tgt, w_post):
    e = pe * _sigmoid(gl)
    diff = h2 + _rms(e, w_post) - tgt
    return 0.5 * jnp.mean(diff * diff, axis=-1, keepdims=True)


def _rope_tables(positions):
    half = QK_ROPE // 2
    inv_freq = ROPE_THETA ** (-jnp.arange(0, QK_ROPE, 2, dtype=F32) / QK_ROPE)
    ang = positions.astype(F32).reshape(-1, 1) * inv_freq
    cos, sin = jnp.cos(ang), jnp.sin(ang)
    n = ang.shape[0]
    z = lambda w: jnp.zeros((n, w), F32)
    c_tab = jnp.concatenate([jnp.ones((n, QK_NOPE), F32), cos, cos, z(QK_PAD - QK_NOPE - QK_ROPE)], axis=1)
    a_tab = jnp.concatenate([z(QK_NOPE), -sin, z(half), z(QK_PAD - QK_NOPE - QK_ROPE)], axis=1)
    b_tab = jnp.concatenate([z(QK_NOPE), z(half), sin, z(QK_PAD - QK_NOPE - QK_ROPE)], axis=1)
    return c_tab, a_tab, b_tab


def _rot(x, c, a, b):
    half = QK_ROPE // 2
    return x * c + pltpu.roll(x, QK_PAD - half, axis=1) * a + pltpu.roll(x, half, axis=1) * b


def _rot_t(g, c, a, b):
    half = QK_ROPE // 2
    return g * c + pltpu.roll(g * a, half, axis=1) + pltpu.roll(g * b, QK_PAD - half, axis=1)


def _st_rope(n_heads):
    def fn(qraw, kraw, sm, c, a, b):
        kpe = _rot(sm[:, :QK_PAD], c, a, b)
        scale = float(QK_NOPE + QK_ROPE) ** -0.5
        q = [_rot(qraw[:, h * QK_PAD:(h + 1) * QK_PAD], c, a, b) * scale for h in range(n_heads)]
        k = [kraw[:, h * QK_PAD:(h + 1) * QK_PAD] + kpe for h in range(n_heads)]
        return jnp.concatenate(q, axis=1), jnp.concatenate(k, axis=1)
    return fn


def _st_rope_bwd(n_heads):
    def fn(dq, dk, c, a, b):
        dqraw = [_rot_t(dq[:, h * QK_PAD:(h + 1) * QK_PAD], c, a, b) for h in range(n_heads)]
        dks = dk[:, :QK_PAD]
        for h in range(1, n_heads):
            dks = dks + dk[:, h * QK_PAD:(h + 1) * QK_PAD]
        return jnp.concatenate(dqraw, axis=1), _rot_t(dks, c, a, b), dk
    return fn


def _split3(x):
    h1 = x.astype(BF16)
    r1 = x - h1.astype(F32)
    h2 = r1.astype(BF16)
    h3 = (r1 - h2.astype(F32)).astype(BF16)
    return h1, h2, h3


def _tri_dot(tri, x):
    h1, h2, h3 = _split3(x)
    return (_dot(tri, h3) + _dot(tri, h2)) + _dot(tri, h1)


def _dot_tri(x, tri):
    h1, h2, h3 = _split3(x)
    return (_dot(h3, tri) + _dot(h2, tri)) + _dot(h1, tri)


def _st_dt(sm, bias, alog):
    x = sm[:, QK_PAD:] + bias
    dt = _softplus(x)
    return dt, dt * (-jnp.exp(alog))


def _st_dt_bwd(sm, ddt, dcum, bias, alog):
    n = sm.shape[0]
    i = lax.broadcasted_iota(jnp.int32, (n, n), 0)
    j = lax.broadcasted_iota(jnp.int32, (n, n), 1)
    upper = (j >= i).astype(BF16)
    dda = _tri_dot(upper, dcum)
    x = sm[:, QK_PAD:] + bias
    dt = _softplus(x)
    a = -jnp.exp(alog)
    draw = (ddt + dda * a) * _sigmoid(x)
    return draw, draw, dda * dt * a


def _conv_fwd(xbc, w, b):
    S, C = xbc.shape
    tr = _pick(S, (512, 256))
    tc = _pick(C, (1024, 512, 256, 128))
    hb = tr // SUBLANES

    def body(x_ref, halo_ref, w_ref, b_ref, c_ref, a_ref, ext):
        i = pl.program_id(1)
        halo = jnp.where(i == 0, 0.0, halo_ref[...])
        ext[0:SUBLANES, :] = halo
        ext[SUBLANES:, :] = x_ref[...]

        def lane_strip(c, carry):
            lanes = pl.ds(pl.multiple_of(c * LANES, LANES), LANES)
            wv = w_ref[:, lanes]
            acc = b_ref[:, lanes] + wv[CONV_WIDTH - 1:CONV_WIDTH, :] * ext[pl.ds(SUBLANES, tr), lanes]
            for k in range(CONV_WIDTH - 1):
                off = SUBLANES - (CONV_WIDTH - 1) + k
                acc = acc + wv[k:k + 1, :] * ext[pl.ds(off, tr), lanes]
            c_ref[:, lanes] = acc
            a_ref[:, lanes] = _silu(acc)
            return carry

        lax.fori_loop(0, tc // LANES, lane_strip, 0)

    return pl.pallas_call(
        body, name="conv_fwd", grid=(C // tc, S // tr),
        in_specs=[pl.BlockSpec((tr, tc), lambda j, i: (i, j)),
                  pl.BlockSpec((SUBLANES, tc), lambda j, i: (jnp.maximum(i * hb - 1, 0), j)),
                  pl.BlockSpec((CONV_WIDTH, tc), lambda j, i: (0, j)),
                  pl.BlockSpec((1, tc), lambda j, i: (0, j))],
        out_specs=[pl.BlockSpec((tr, tc), lambda j, i: (i, j))] * 2,
        out_shape=[jax.ShapeDtypeStruct((S, C), F32)] * 2,
        scratch_shapes=[pltpu.VMEM((tr + SUBLANES, tc), F32)],
        compiler_params=_params(("parallel", "arbitrary")),
    )(xbc, xbc, w, b)


def _conv_bwd(xbc, dconv, w, buf, col0):
    S, C = xbc.shape
    tr = _pick(S, (512, 256))
    tc = _pick(C, (1024, 512, 256, 128))
    hb = tr // SUBLANES
    n_i = S // tr
    assert col0 % tc == 0, (col0, tc)
    cb0 = col0 // tc

    def body(x_ref, halo_ref, d_ref, dnext_ref, w_ref, buf_ref, dx_ref, dw_ref, ext, dext):
        i = pl.program_id(1)
        ext[0:SUBLANES, :] = jnp.where(i == 0, 0.0, halo_ref[...])
        ext[SUBLANES:, :] = x_ref[...]
        dext[0:tr, :] = d_ref[...]
        dext[tr:, :] = jnp.where(i == n_i - 1, 0.0, dnext_ref[...])

        @pl.when(i == 0)
        def _():
            dw_ref[...] = jnp.zeros_like(dw_ref)

        def lane_strip(c, carry):
            lanes = pl.ds(pl.multiple_of(c * LANES, LANES), LANES)
            wv = w_ref[:, lanes]
            d = dext[pl.ds(0, tr), lanes]
            dx = wv[CONV_WIDTH - 1:CONV_WIDTH, :] * d
            for k in range(CONV_WIDTH):
                if k < CONV_WIDTH - 1:
                    dx = dx + wv[k:k + 1, :] * dext[pl.ds(CONV_WIDTH - 1 - k, tr), lanes]
                xs = ext[pl.ds(SUBLANES - (CONV_WIDTH - 1) + k, tr), lanes]
                prod = d * xs
                dw_ref[k * SUBLANES:(k + 1) * SUBLANES, lanes] += prod.reshape(tr // SUBLANES, SUBLANES, LANES).sum(axis=0)
            dx_ref[:, lanes] = dx.astype(dx_ref.dtype)
            return carry

        lax.fori_loop(0, tc // LANES, lane_strip, 0)

    return pl.pallas_call(
        body, name="conv_bwd", grid=(C // tc, n_i),
        in_specs=[pl.BlockSpec((tr, tc), lambda j, i: (i, j)),
                  pl.BlockSpec((SUBLANES, tc), lambda j, i: (jnp.maximum(i * hb - 1, 0), j)),
                  pl.BlockSpec((tr, tc), lambda j, i: (i, j)),
                  pl.BlockSpec((SUBLANES, tc), lambda j, i: (jnp.minimum((i + 1) * hb, S // SUBLANES - 1), j)),
                  pl.BlockSpec((CONV_WIDTH, tc), lambda j, i: (0, j)),
                  pl.BlockSpec(memory_space=pl.ANY)],
        out_specs=[pl.BlockSpec((tr, tc), lambda j, i: (i, cb0 + j)),
                   pl.BlockSpec((CONV_WIDTH * SUBLANES, tc), lambda j, i: (0, j))],
        out_shape=[jax.ShapeDtypeStruct(buf.shape, buf.dtype), jax.ShapeDtypeStruct((CONV_WIDTH * SUBLANES, C), F32)],
        scratch_shapes=[pltpu.VMEM((tr + SUBLANES, tc), F32), pltpu.VMEM((tr + SUBLANES, tc), F32)],
        input_output_aliases={5: 0},
        compiler_params=_params(("parallel", "arbitrary")),
    )(xbc, xbc, dconv, dconv, w, buf)


def _st_dconv(d_inner):
    def fn(xc, dxa, dxb, db_, dc_):
        s = _sigmoid(xc)
        g = jnp.concatenate([dxa + dxb, db_, dc_], axis=1) * (s * (1.0 + xc * (1.0 - s)))
        return g, g
    return fn


def _chunk_setup_t(b_ref, c_ref, dac_ref, dar_ref, L):
    ii = lax.broadcasted_iota(jnp.int32, (L, L), 0)
    jj = lax.broadcasted_iota(jnp.int32, (L, L), 1)
    lower = ii >= jj
    upper = ii <= jj
    cum_c = _tri_dot(lower.astype(BF16), dac_ref[0])
    cum_r = _dot_tri(dar_ref[0], upper.astype(BF16))
    bm = b_ref[...].astype(BF16)
    cm = c_ref[...].astype(BF16)
    return lower, upper, cum_c, cum_r, bm, cm


def _ssd_specs_t(d_inner, n_groups, gw, L, rp, chunk_of):
    bb0 = d_inner // D_STATE
    cb0 = bb0 + n_groups
    return [pl.BlockSpec((L, gw), lambda g, c: (chunk_of(c), g)),
            pl.BlockSpec((L, D_STATE), lambda g, c: (chunk_of(c), bb0 + g)),
            pl.BlockSpec((L, D_STATE), lambda g, c: (chunk_of(c), cb0 + g)),
            pl.BlockSpec((1, rp, L), lambda g, c: (g, 0, chunk_of(c))),
            pl.BlockSpec((1, L, LANES), lambda g, c: (g, chunk_of(c), 0)),
            pl.BlockSpec((1, rp, L), lambda g, c: (g, 0, chunk_of(c)))]


def _ssd_fwd_t(xbc_a, dt_row, da_col, da_row, d_inner, n_groups, R):
    S = xbc_a.shape[0]
    L = min(CHUNK, S)
    NC = S // L
    P, N = SSM_HEADDIM, D_STATE
    gw = R * P
    rp = da_row.shape[1]

    def body(x_ref, b_ref, c_ref, dt_ref, dac_ref, dar_ref, y_ref, st_ref, state, y_t):
        @pl.when(pl.program_id(1) == 0)
        def _():
            state[...] = jnp.zeros_like(state)

        st_ref[0, 0] = state[...]
        lower, upper, cum_c, cum_r, bm, cm = _chunk_setup_t(b_ref, c_ref, dac_ref, dar_ref, L)
        gm_t = _dot_nt(bm, cm)
        dt = dt_ref[0]
        x_t = x_ref[...].T
        for r in range(R):
            cc = cum_c[:, r:r + 1]
            cr = cum_r[r:r + 1, :]
            m_t = (gm_t * jnp.exp(jnp.where(upper, cr - cc, -jnp.inf))).astype(BF16)
            x = x_t[r * P:(r + 1) * P, :] * dt[r:r + 1, :]
            s_r = state[r * P:(r + 1) * P, :]
            y_off = _dot_nt(s_r.astype(BF16), cm) * jnp.exp(cr)
            y_t[r * P:(r + 1) * P, :] = _dot(x.astype(BF16), m_t) + y_off
            last = cr[:, L - 1:L]
            xw = (x * jnp.exp(last - cr)).astype(BF16)
            state[r * P:(r + 1) * P, :] = s_r * jnp.exp(last) + _dot(xw, bm)
        y_ref[...] = y_t[...].T

    return pl.pallas_call(
        body, name="ssd_fwd", grid=(n_groups, NC),
        in_specs=_ssd_specs_t(d_inner, n_groups, gw, L, rp, lambda c: c),
        out_specs=[pl.BlockSpec((L, gw), lambda g, c: (c, g)),
                   pl.BlockSpec((1, 1, gw, N), lambda g, c: (g, c, 0, 0))],
        out_shape=[jax.ShapeDtypeStruct((S, d_inner), F32), jax.ShapeDtypeStruct((n_groups, NC, gw, N), F32)],
        scratch_shapes=[pltpu.VMEM((gw, N), F32), pltpu.VMEM((gw, L), F32)],
        compiler_params=_params(("parallel", "arbitrary")),
    )(xbc_a, xbc_a, xbc_a, dt_row, da_col, da_row)


def _ssd_bwd_t(xbc_a, dt_row, da_col, da_row, states, dy, d_inner, n_groups, R):
    S = xbc_a.shape[0]
    L = min(CHUNK, S)
    NC = S // L
    P, N = SSM_HEADDIM, D_STATE
    gw = R * P
    rp = da_row.shape[1]
    rev = lambda c: NC - 1 - c

    def body(x_ref, b_ref, c_ref, dt_ref, dac_ref, dar_ref, st_ref, dy_ref,
             dx_ref, db_ref, dc_ref, ddt_ref, dcum_ref, dstate, dx_t):
        @pl.when(pl.program_id(1) == 0)
        def _():
            dstate[...] = jnp.zeros_like(dstate)

        lower, upper, cum_c, cum_r, bm, cm = _chunk_setup_t(b_ref, c_ref, dac_ref, dar_ref, L)
        gm = _dot_nt(cm, bm)
        gm_t = _dot_nt(bm, cm)
        dt = dt_ref[0]
        x_t = x_ref[...].T
        dy_t = dy_ref[...].T
        sub = lax.broadcasted_iota(jnp.int32, (rp, L), 0)
        is_last = lax.broadcasted_iota(jnp.int32, (1, L), 1) == L - 1
        d_g = jnp.zeros((L, L), F32)
        d_g_t = jnp.zeros((L, L), F32)
        dc_acc = jnp.zeros((L, N), F32)
        db_acc = jnp.zeros((L, N), F32)
        ddt_out = jnp.zeros((rp, L), F32)
        dcum_out = jnp.zeros((rp, L), F32)
        for r in range(R):
            cc = jnp.broadcast_to(cum_c[:, r:r + 1], (L, L))
            cr = cum_r[r:r + 1, :]
            lam = jnp.exp(jnp.where(lower, cc - cum_r[r:r + 1, :], -jnp.inf))
            lam_t = jnp.exp(jnp.where(upper, cr - cc, -jnp.inf))
            m = gm * lam
            m_t = gm_t * lam_t
            dtr = dt[r:r + 1, :]
            xh = x_t[r * P:(r + 1) * P, :]
            x = xh * dtr
            xb = x.astype(BF16)
            d_y = dy_t[r * P:(r + 1) * P, :]
            d_yb = d_y.astype(BF16)
            s_r = st_ref[0, 0, r * P:(r + 1) * P, :]
            s_rb = s_r.astype(BF16)
            ds_n = dstate[r * P:(r + 1) * P, :]
            ds_nb = ds_n.astype(BF16)
            e = jnp.exp(cr)
            last = cr[:, L - 1:L]
            e_last = jnp.exp(last)
            w = jnp.exp(last - cr)
            d_x = _dot(d_yb, m.astype(BF16))
            d_m = _dot_tn(d_yb, xb)
            d_m_t = _dot_tn(xb, d_yb)
            d_ye = (d_y * e).astype(BF16)
            dc_acc = dc_acc + _dot_tn(d_ye, s_rb)
            ds_part = _dot(d_ye, cm)
            y_off = _dot_nt(s_rb, cm) * e
            dcum = jnp.sum(d_y * y_off, axis=0, keepdims=True)
            d_xw = _dot_nt(ds_nb, bm)
            d_x = d_x + d_xw * w
            dw_w = jnp.sum(d_xw * x, axis=0, keepdims=True) * w
            db_acc = db_acc + _dot_tn((x * w).astype(BF16), ds_nb)
            d_last = jnp.sum(ds_n * s_r, keepdims=True) * e_last + jnp.sum(dw_w, keepdims=True)
            dstate[r * P:(r + 1) * P, :] = e_last * ds_n + ds_part
            d_g = d_g + d_m * lam
            d_g_t = d_g_t + d_m_t * lam_t
            dcum = (dcum - dw_w + jnp.sum(d_m_t * m_t, axis=0, keepdims=True)
                    - jnp.sum(d_m * m, axis=0, keepdims=True) + jnp.where(is_last, d_last, 0.0))
            dx_t[r * P:(r + 1) * P, :] = d_x * dtr
            ddt = jnp.sum(d_x * xh, axis=0, keepdims=True)
            ddt_out = ddt_out + jnp.where(sub == r, ddt, 0.0)
            dcum_out = dcum_out + jnp.where(sub == r, dcum, 0.0)
        dc_ref[...] = dc_acc + _dot(d_g.astype(BF16), bm)
        db_ref[...] = db_acc + _dot(d_g_t.astype(BF16), cm)
        dx_ref[...] = dx_t[...].T
        ddt_ref[0] = ddt_out
        dcum_ref[0] = dcum_out

    gn = n_groups * N
    return pl.pallas_call(
        body, name="ssd_bwd", grid=(n_groups, NC),
        in_specs=_ssd_specs_t(d_inner, n_groups, gw, L, rp, rev) + [
            pl.BlockSpec((1, 1, gw, N), lambda g, c: (g, rev(c), 0, 0)),
            pl.BlockSpec((L, gw), lambda g, c: (rev(c), g))],
        out_specs=[pl.BlockSpec((L, gw), lambda g, c: (rev(c), g)),
                   pl.BlockSpec((L, N), lambda g, c: (rev(c), g)),
                   pl.BlockSpec((L, N), lambda g, c: (rev(c), g)),
                   pl.BlockSpec((1, rp, L), lambda g, c: (g, 0, rev(c))),
                   pl.BlockSpec((1, rp, L), lambda g, c: (g, 0, rev(c)))],
        out_shape=[jax.ShapeDtypeStruct((S, d_inner), F32), jax.ShapeDtypeStruct((S, gn), F32),
                   jax.ShapeDtypeStruct((S, gn), F32), jax.ShapeDtypeStruct((n_groups, rp, S), F32),
                   jax.ShapeDtypeStruct((n_groups, rp, S), F32)],
        scratch_shapes=[pltpu.VMEM((gw, N), F32), pltpu.VMEM((gw, L), F32)],
        compiler_params=_params(("parallel", "arbitrary")),
    )(xbc_a, xbc_a, xbc_a, dt_row, da_col, da_row, states, dy)


def _attn_scale():
    return float(QK_NOPE + QK_ROPE) ** -0.5


def _diag_mask(t):
    return lax.broadcasted_iota(jnp.int32, (t, t), 0) <= lax.broadcasted_iota(jnp.int32, (t, t), 1)


def _walk_wide(lo, hi, tile_step, joint=True, widest=4):
    n = hi - lo

    def step(j, width):
        if joint:
            tile_step(j, width)
        else:
            for u in range(width):
                tile_step(j + u, 1)

    def widest_run(t, carry):
        step(lo + widest * t, widest)
        return carry

    lax.fori_loop(0, n // widest, widest_run, 0)

    def shorter_run(width):
        @pl.when(n % (2 * width) >= width)
        def _():
            step(hi - n % (2 * width), width)

    width = widest // 2
    while width >= 1:
        shorter_run(width)
        width //= 2


def _carried(main_body, n_in, n_out, n_scratch, carry, grid):
    if carry is None:
        return main_body
    n_ci, n_co = len(carry["ins"]), len(carry["outs"])

    def body(*refs):
        pos = [0]

        def take(n):
            pos[0] += n
            return refs[pos[0] - n: pos[0]]

        ins, c_ins, outs, c_outs, scratch, sems = take(n_in), take(n_ci), take(n_out), take(n_co), take(n_scratch), take(3)
        steps = [pl.program_id(a) for a in range(len(grid))]

        @pl.when(functools.reduce(jnp.logical_and, [s == 0 for s in steps]))
        def _():
            for cp in carry["copies"](c_ins, c_outs, *sems):
                cp.start()

        main_body(*ins, *outs, *scratch)

        @pl.when(functools.reduce(jnp.logical_and, [s == n - 1 for s, n in zip(steps, grid)]))
        def _():
            for cp in carry["copies"](c_ins, c_outs, *sems):
                cp.wait()

    return body


def _carry_call(carry):
    if carry is None:
        return [], [], [], [], []
    sems = [pltpu.SemaphoreType.DMA((carry["n_remote"],)), pltpu.SemaphoreType.DMA((carry["n_remote"],)),
            pltpu.SemaphoreType.DMA((max(carry["n_local"], 1),))]
    return (list(carry["ins"]), [_HBM] * len(carry["ins"]), [_HBM] * len(carry["outs"]), list(carry["outs"]), sems)


def _flash_fwd(q, k, v_t, n_heads, carry=None):
    S = q.shape[0]
    T = min(FLASH_T, S)
    grid = (n_heads, S // T)
    c_args, c_in_specs, c_out_specs, c_out_shapes, c_sems = _carry_call(carry)

    def body(q_ref, k_ref, vt_ref, o_ref, lse_ref, m_s, l_s, acc_t):
        i = pl.program_id(1)
        m_s[...] = jnp.full_like(m_s, -jnp.inf)
        l_s[...] = jnp.zeros_like(l_s)
        acc_t[...] = jnp.zeros_like(acc_t)
        qv = q_ref[...]

        def step(j, width, masked):
            keys = pl.ds(pl.multiple_of(j * T, T), width * T)
            s_t = _dot_nt(k_ref[keys, :], qv)
            if masked:
                s_t = jnp.where(_diag_mask(T), s_t, -jnp.inf)
            m_prev = m_s[...]
            m_new = jnp.maximum(m_prev, jnp.max(s_t, axis=0, keepdims=True))
            alpha = jnp.exp(m_prev - m_new)
            p_t = jnp.exp(s_t - m_new)
            l_s[...] = alpha * l_s[...] + jnp.sum(p_t, axis=0, keepdims=True)
            acc_t[...] = alpha * acc_t[...] + _dot(vt_ref[:, keys], p_t.astype(BF16))
            m_s[...] = m_new

        _walk_wide(0, i, lambda j, width: step(j, width, False), widest=min(FLASH_FWD_RUN, S // T))
        step(i, 1, True)
        o_ref[...] = (acc_t[...] / l_s[...]).T
        lse_ref[0] = m_s[...] + jnp.log(l_s[...])

    res = pl.pallas_call(
        _carried(body, 3, 2, 3, carry, grid), name="flash_fwd", grid=grid,
        in_specs=[pl.BlockSpec((T, QK_PAD), lambda h, i: (i, h)),
                  pl.BlockSpec((S, QK_PAD), lambda h, i: (0, h)),
                  pl.BlockSpec((V_DIM, S), lambda h, i: (h, 0))] + c_in_specs,
        out_specs=[pl.BlockSpec((T, V_DIM), lambda h, i: (i, h)),
                   pl.BlockSpec((1, 1, T), lambda h, i: (h, 0, i))] + c_out_specs,
        out_shape=[jax.ShapeDtypeStruct((S, n_heads * V_DIM), F32),
                   jax.ShapeDtypeStruct((n_heads, 1, S), F32)] + c_out_shapes,
        scratch_shapes=[pltpu.VMEM((1, T), F32), pltpu.VMEM((1, T), F32), pltpu.VMEM((V_DIM, T), F32)] + c_sems,
        compiler_params=_params(("arbitrary", "arbitrary")),
    )(q, k, v_t, *c_args)
    return res[0], res[1], list(res[2:])


def _flash_bwd(q, k, v, do, lse_row, delta_row, n_heads, carry=None):
    S = q.shape[0]
    T = min(FLASH_T, S)
    nq = S // T
    grid = (n_heads, S // T)
    c_args, c_in_specs, c_out_specs, c_out_shapes, c_sems = _carry_call(carry)

    def body(q_ref, k_ref, v_ref, do_ref, lse_ref, dl_ref, dq_ref, dk_ref, dv_ref, dk_acc, dv_acc):
        j = pl.program_id(1)

        @pl.when(j == 0)
        def _():
            dq_ref[...] = jnp.zeros_like(dq_ref)

        dk_acc[...] = jnp.zeros_like(dk_acc)
        dv_acc[...] = jnp.zeros_like(dv_acc)
        kv = k_ref[...]
        vv = v_ref[...]

        def step(i, width, masked):
            cols = pl.ds(pl.multiple_of(i * T, T), width * T)
            qt = q_ref[cols, :]
            dot = do_ref[cols, :]
            s_t = _dot_nt(kv, qt)
            if masked:
                s_t = jnp.where(_diag_mask(T), s_t, -jnp.inf)
            p_t = jnp.exp(s_t - lse_ref[0, :, cols])
            dv_acc[...] += _dot(p_t.astype(BF16), dot)
            ds_t = (p_t * (_dot_nt(vv, dot) - dl_ref[0, :, cols])).astype(BF16)
            dk_acc[...] += _dot(ds_t, qt)
            dq_ref[cols, :] += _dot_tn(ds_t, kv)

        step(j, 1, True)
        _walk_wide(j + 1, nq, lambda i, width: step(i, width, False), joint=False)
        dk_ref[...] = dk_acc[...]
        dv_ref[...] = dv_acc[...].astype(dv_ref.dtype)

        @pl.when(j == nq - 1)
        def _():
            dq_ref[...] = dq_ref[...] * _attn_scale()

    res = pl.pallas_call(
        _carried(body, 6, 3, 2, carry, grid), name="flash_bwd", grid=grid,
        in_specs=[pl.BlockSpec((S, QK_PAD), lambda h, j: (0, h)),
                  pl.BlockSpec((T, QK_PAD), lambda h, j: (j, h)),
                  pl.BlockSpec((T, V_DIM), lambda h, j: (j, h)),
                  pl.BlockSpec((S, V_DIM), lambda h, j: (0, h)),
                  pl.BlockSpec((1, 1, S), lambda h, j: (h, 0, 0)),
                  pl.BlockSpec((1, 1, S), lambda h, j: (h, 0, 0))] + c_in_specs,
        out_specs=[pl.BlockSpec((S, QK_PAD), lambda h, j: (0, h)),
                   pl.BlockSpec((T, QK_PAD), lambda h, j: (j, h)),
                   pl.BlockSpec((T, V_DIM), lambda h, j: (j, h))] + c_out_specs,
        out_shape=[jax.ShapeDtypeStruct((S, n_heads * QK_PAD), F32), jax.ShapeDtypeStruct((S, n_heads * QK_PAD), F32),
                   jax.ShapeDtypeStruct((S, n_heads * V_DIM), BF16)] + c_out_shapes,
        scratch_shapes=[pltpu.VMEM((T, QK_PAD), F32), pltpu.VMEM((T, V_DIM), F32)] + c_sems,
        compiler_params=_params(("arbitrary", "arbitrary")),
    )(q, k, v, do, lse_row, delta_row, *c_args)
    return res[0], res[1], res[2], list(res[3:])


def _st_delta(n_heads):
    def fn(do, o):
        prod = do * o
        lane = lax.broadcasted_iota(jnp.int32, (do.shape[0], LANES), 1)
        out = jnp.zeros((do.shape[0], LANES), F32)
        for h in range(n_heads):
            out = out + jnp.where(lane == h, jnp.sum(prod[:, h * V_DIM:(h + 1) * V_DIM], axis=1, keepdims=True), 0.0)
        return out
    return fn


def _pad_cols(w, width):
    return jnp.pad(w, ((0, 0), (0, width - w.shape[1])))


def _dims(x, p, q_norm, kv_norm, w_uq, dt_bias, ssm_norm, conv_b):
    d = dict(S=x.shape[0], D=x.shape[1], PLE=p.shape[1], DQ=q_norm.shape[1], DKV=kv_norm.shape[1],
             NH=w_uq.shape[1] // (QK_NOPE + QK_ROPE), NHS=dt_bias.shape[1], DI=ssm_norm.shape[1],
             CONV=conv_b.shape[1])
    d["G"] = (d["CONV"] - d["DI"]) // (2 * D_STATE)
    d["R"] = d["NHS"] // d["G"]
    return d


def _assemble(name, blocks):
    rows, cols = blocks.shape[1:]
    if name in COL_SHARDED:
        return blocks.transpose(1, 0, 2).reshape(rows, N_DEV * cols)
    return blocks.reshape(N_DEV * rows, cols)


def _by_device(name, g):
    if name in COL_SHARDED:
        return g.reshape(g.shape[0], N_DEV, g.shape[1] // N_DEV).transpose(1, 0, 2)
    return g.reshape(N_DEV, g.shape[0] // N_DEV, g.shape[1])


def _local_step(x, p, positions, target, small, conv_w, wts, late_names=(), late_shards=(), exchange=True):
    wts = dict(wts)
    dm = _dims(x, p, small["q_norm"], small["kv_norm"], wts["w_uq"], small["dt_bias"], small["ssm_norm"],
               small["conv_b"])
    S, D, DQ, DKV, NH, NHS, DI, CONV, G, R = (dm[k] for k in ("S", "D", "DQ", "DKV", "NH", "NHS", "DI", "CONV", "G", "R"))
    rp = -(-R // SUBLANES) * SUBLANES
    L = min(CHUNK, S)

    w_nat = wts["w_in"]
    o = [0]
    for n in (DQ, DKV, QK_ROPE, DI, CONV, NHS, D, D):
        o.append(o[-1] + n)
    pw = (DI, 2 * D, CONV, DQ + DKV, QK_ROPE + NHS)
    po = [sum(pw[:k]) for k in range(len(pw))]
    w_in = jnp.concatenate([w_nat[:, o[3]:o[4]], w_nat[:, o[6]:o[8]], w_nat[:, o[4]:o[5]], w_nat[:, o[0]:o[2]],
                            w_nat[:, o[2]:o[3]], w_nat[:, o[5]:o[6]]], axis=1)
    w_z, w_g, w_xbc, w_cqkv = (w_in[:, po[k]:po[k] + pw[k]] for k in range(4))
    w_kr, w_dt = w_in[:, po[4]:po[4] + QK_ROPE], w_in[:, po[4] + QK_ROPE:]
    zc = lambda n: jnp.zeros((D, n), BF16)
    w_sm = jnp.concatenate([zc(QK_NOPE), w_kr, zc(QK_PAD - QK_NOPE - QK_ROPE), w_dt, zc(LANES - NHS)], axis=1)
    w_q = jnp.pad(wts["w_uq"].reshape(DQ, NH, QK_NOPE + QK_ROPE),
                  ((0, 0), (0, 0), (0, QK_PAD - QK_NOPE - QK_ROPE))).reshape(DQ, NH * QK_PAD)
    ukv = wts["w_ukv"].reshape(DKV, NH, QK_NOPE + V_DIM)
    w_k = jnp.pad(ukv[:, :, :QK_NOPE], ((0, 0), (0, 0), (0, QK_PAD - QK_NOPE))).reshape(DKV, NH * QK_PAD)
    w_v = ukv[:, :, QK_NOPE:].reshape(DKV, NH * V_DIM)
    dt_bias_p, a_log_p = _pad_cols(small["dt_bias"], LANES), _pad_cols(small["a_log"], LANES)
    dskip_rep = jnp.repeat(small["d_skip"], SSM_HEADDIM, axis=1)
    c_tab, a_tab, b_tab = _rope_tables(positions)

    (u,) = _rowwise("pre_norm", _st_pre, [x], [small["mix_norm_pre"]], [(D, BF16)])
    cqkv = _mm("in_cqkv", u, w_cqkv, "nn")
    z = _mm("in_z", u, w_z, "nn")
    xbc = _mm("in_xbc", u, w_xbc, "nn")
    g = _mm("in_gates", u, w_g, "nn")
    sm = _mm("in_small", u, w_sm, "nn")

    lora_fn = _st_lora_norms(DQ)
    cq_n, ckv_n = _rowwise("lora_norms", lora_fn, [cqkv], [small["q_norm"], small["kv_norm"]], [(DQ, BF16), (DKV, BF16)])
    qraw = _mm("up_q", cq_n, w_q, "nn")
    kraw = _mm("up_k", ckv_n, w_k, "nn")
    v = _mm("up_v", ckv_n, w_v, "nn", out_dtype=BF16)
    v_t = _mm("up_v_t", w_v.T, ckv_n, "nt", out_dtype=BF16)
    q, k = _rowwise("rope", _st_rope(NH), [qraw, kraw, sm, c_tab, a_tab, b_tab], [],
                    [(NH * QK_PAD, BF16), (NH * QK_PAD, BF16)])
    attn, lse, late_blocks = _flash_fwd(q, k, v_t, NH, carry=_gather_carry(list(late_shards)) if late_names else None)
    wts.update({n: _assemble(n, b) for n, b in zip(late_names, late_blocks)})

    xbc_c, xbc_a = _conv_fwd(xbc, conv_w, small["conv_b"])
    dt, da = _rowwise("dt", _st_dt, [sm], [dt_bias_p, a_log_p], [(LANES, F32), (LANES, F32)])

    def col_layout(t):
        return _pad_cols(t[:, :NHS].reshape(S, G, R).transpose(1, 0, 2).reshape(G * S, R), LANES).reshape(G, S, LANES)

    def row_layout(t):
        return jnp.pad(t[:, :NHS].reshape(S, G, R).transpose(1, 2, 0), ((0, 0), (0, rp - R), (0, 0)))

    dt_row, da_col, da_row = row_layout(dt), col_layout(da), row_layout(da)
    xs = xbc_a[:, :DI]
    y, states = _ssd_fwd_t(xbc_a, dt_row, da_col, da_row, DI, G, R)
    gn_fn = _st_gated_norm(G)
    (ssm,) = _rowwise("gated_norm", gn_fn, [y, xs, z], [dskip_rep, small["ssm_norm"]], [(DI, BF16)])

    ao = _mm("attn_o", attn, wts["w_attn_o"], "nn")
    so = _mm("ssm_o", ssm, wts["w_ssm_o"], "nn")
    mix_fn = _st_mix(D)
    (mixed,) = _rowwise("mix", mix_fn, [g, ao, so], [], [(D, BF16)])
    mo = _mm("out_proj", mixed, wts["w_out"], "nn")
    h1, f = _rowwise("res1", _st_res_norm, [x, mo], [small["mix_norm_post"], small["ffn_norm_pre"]], [(D, F32), (D, BF16)])
    gt = _mm("ffn_gate", f, wts["w_gate"], "nn")
    up = _mm("ffn_up", f, wts["w_up"], "nn")
    (act,) = _rowwise("swiglu", _st_swiglu, [gt, up], [], [(gt.shape[1], BF16)])
    dn = _mm("ffn_down", act, wts["w_down"], "nn")
    h2, a3 = _rowwise("res2", _st_res_norm, [h1, dn], [small["ffn_norm_post"], small["ple_norm_pre"]], [(D, F32), (D, BF16)])
    gl = _mm("ple_gate", a3, wts["w_ple_gate"], "nn")
    pe = _mm("ple_proj", p, wts["w_ple"], "nn")

    sg = {}
    bg = {}
    dpe, dgl, dh2, d_w, loss_acc = _rowwise_bwd(
        "loss", _st_loss, [pe, gl, h2, target], [small["ple_norm_post"]], [1.0], [0, 1, 2], [BF16, BF16, F32], fwd_sums=(0,))
    sg["ple_norm_post"] = _fold(d_w)
    loss = loss_acc[0:1, :]
    bg["w_ple"] = _mm("d_w_ple", p, dpe, "tn", out_dtype=BF16)
    bg["w_ple_gate"] = _mm("d_w_ple_gate", a3, dgl, "tn", out_dtype=BF16)
    da3 = _mm("d_a3", dgl, wts["w_ple_gate"], "nt")

    dh1, ddn, d_post, d_pre = _rowwise_bwd(
        "res2_bwd", _st_res_norm, [h1, dn], [small["ffn_norm_post"], small["ple_norm_pre"]], [dh2, da3], [0, 1], [F32, BF16])
    sg["ffn_norm_post"], sg["ple_norm_pre"] = _fold(d_post), _fold(d_pre)
    bg["w_down"] = _mm("d_w_down", act, ddn, "tn", out_dtype=BF16)
    dact = _mm("d_act", ddn, wts["w_down"], "nt")
    dgt, dup = _rowwise_bwd("swiglu_bwd", _st_swiglu, [gt, up], [], [dact], [0, 1], [BF16, BF16])
    bg["w_gate"] = _mm("d_w_gate", f, dgt, "tn", out_dtype=BF16)
    bg["w_up"] = _mm("d_w_up", f, dup, "tn", out_dtype=BF16)
    df = _mm("d_f_gate", dgt, wts["w_gate"], "nt")
    df = _mm("d_f_up", dup, wts["w_up"], "nt", acc_in=df)

    dx_res, dmo, d_post, d_pre = _rowwise_bwd(
        "res1_bwd", _st_res_norm, [x, mo], [small["mix_norm_post"], small["ffn_norm_pre"]], [dh1, df], [0, 1], [F32, BF16])
    sg["mix_norm_post"], sg["ffn_norm_pre"] = _fold(d_post), _fold(d_pre)
    bg["w_out"] = _mm("d_w_out", mixed, dmo, "tn", out_dtype=BF16)
    dmixed = _mm("d_mixed", dmo, wts["w_out"], "nt")
    assert po[1] % pw[1] == 0 and po[3] % pw[3] == 0, (po, pw)
    dproj = lax.empty((S, sum(pw)), BF16)
    dproj, dao, dso = _rowwise_bwd("mix_bwd", mix_fn, [g, ao, so], [], [dmixed], [0, 1, 2], [BF16, BF16, BF16],
                                   into={0: (dproj, po[1] // pw[1])})
    bg["w_attn_o"] = _mm("d_w_attn_o", attn, dao, "tn", out_dtype=BF16)
    bg["w_ssm_o"] = _mm("d_w_ssm_o", ssm, dso, "tn", out_dtype=BF16)
    dattn = _mm("d_attn", dao, wts["w_attn_o"], "nt", out_dtype=BF16)
    dssm = _mm("d_ssm", dso, wts["w_ssm_o"], "nt")

    dy, dxs_a, dproj, d_dskip, d_ssmn = _rowwise_bwd(
        "gated_norm_bwd", gn_fn, [y, xs, z], [dskip_rep, small["ssm_norm"]], [dssm], [0, 1, 2], [F32, F32, BF16],
        into={2: (dproj, 0)})
    sg["d_skip"] = _fold(d_dskip).reshape(NHS, SSM_HEADDIM).sum(axis=1).reshape(1, NHS)
    sg["ssm_norm"] = _fold(d_ssmn)
    dxs_b, d_b, d_c, ddt_row, dcum_row = _ssd_bwd_t(xbc_a, dt_row, da_col, da_row, states, dy, DI, G, R)

    def from_row(t):
        return _pad_cols(t[:, :R, :].transpose(2, 0, 1).reshape(S, NHS), LANES)

    ddtraw, d_bias, d_alog = _rowwise("dt_bwd", _st_dt_bwd, [sm, from_row(ddt_row), from_row(dcum_row)],
                                      [dt_bias_p, a_log_p], [(LANES, F32)], accs=(LANES, LANES), tr=L)
    sg["dt_bias"], sg["a_log"] = _fold(d_bias)[:, :NHS], _fold(d_alog)[:, :NHS]
    dconv, d_cb = _rowwise("dconv", _st_dconv(DI), [xbc_c, dxs_a, dxs_b, d_b, d_c], [], [(CONV, F32)], accs=(CONV,))
    sg["conv_b"] = _fold(d_cb)
    dproj, d_cw = _conv_bwd(xbc, dconv, conv_w, dproj, po[2])
    d_conv_w = d_cw.reshape(CONV_WIDTH, SUBLANES, CONV).sum(axis=1)

    (delta,) = _rowwise("attn_delta", _st_delta(NH), [dattn, attn], [], [(LANES, F32)])
    delta = delta[:, :NH].T
    late_sent = [_by_device(n, bg.pop(n)) for n in late_names]
    dq, dk, dv, late_recv = _flash_bwd(q, k, v, dattn, lse, delta.reshape(NH, 1, S), NH,
                                       carry=_scatter_carry(late_sent) if late_names else None)
    dqraw, dkr, dk = _rowwise("rope_bwd", _st_rope_bwd(NH), [dq, dk, c_tab, a_tab, b_tab], [],
                              [(NH * QK_PAD, BF16), (QK_PAD, F32), (NH * QK_PAD, BF16)])
    d_w_q = _mm("d_w_q", cq_n, dqraw, "tn", out_dtype=BF16)
    d_w_k = _mm("d_w_k", ckv_n, dk, "tn", out_dtype=BF16)
    d_w_v = _mm("d_w_v", ckv_n, dv, "tn", out_dtype=BF16)
    dcq_n = _mm("d_cq_n", dqraw, w_q, "nt")
    dckv_n = _mm("d_ckv_n_k", dk, w_k, "nt")
    dckv_n = _mm("d_ckv_n_v", dv, w_v, "nt", acc_in=dckv_n)
    bg["w_uq"] = d_w_q.reshape(DQ, NH, QK_PAD)[:, :, :QK_NOPE + QK_ROPE].reshape(DQ, NH * (QK_NOPE + QK_ROPE))
    bg["w_ukv"] = jnp.concatenate([d_w_k.reshape(DKV, NH, QK_PAD)[:, :, :QK_NOPE], d_w_v.reshape(DKV, NH, V_DIM)],
                                  axis=2).reshape(DKV, NH * (QK_NOPE + V_DIM))
    dproj, d_qn, d_kvn = _rowwise_bwd("lora_norms_bwd", lora_fn, [cqkv], [small["q_norm"], small["kv_norm"]],
                                      [dcq_n, dckv_n], [0], [BF16], into={0: (dproj, po[3] // pw[3])})
    sg["q_norm"], sg["kv_norm"] = _fold(d_qn), _fold(d_kvn)

    d_small = jnp.concatenate([dkr[:, QK_NOPE:QK_NOPE + QK_ROPE], ddtraw[:, :NHS]], axis=1).astype(BF16)
    dproj = lax.dynamic_update_slice(dproj, d_small, (0, po[4]))
    d_w = _mm("d_w_in", u, dproj, "tn", out_dtype=BF16)
    kr = po[4] + QK_ROPE
    bg["w_in"] = jnp.concatenate([d_w[:, po[3]:po[4]], d_w[:, po[4]:kr], d_w[:, po[0]:po[1]], d_w[:, po[2]:po[3]],
                                  d_w[:, kr:kr + NHS], d_w[:, po[1]:po[2]]], axis=1)
    rest_names = tuple(n for n in BIG if n in bg)
    if not exchange:
        du = _mm("d_u", dproj, w_in, "nt")
        grad_x, d_pre = _rowwise_bwd("pre_norm_bwd", _st_pre, [x], [small["mix_norm_pre"]], [du], [0], [F32],
                                     adds={0: dx_res})
        sg["mix_norm_pre"] = _fold(d_pre)
        return loss, grad_x, sg, d_conv_w, bg, {}
    rest_sent = [_by_device(n, bg.pop(n)) for n in rest_names]
    du, rest_recv = _mm("d_u", dproj, w_in, "nt", carry=_scatter_carry(rest_sent))
    grad_x, d_pre = _rowwise_bwd("pre_norm_bwd", _st_pre, [x], [small["mix_norm_pre"]], [du], [0], [F32], adds={0: dx_res})
    sg["mix_norm_pre"] = _fold(d_pre)
    sent = dict(zip(late_names, late_sent), **dict(zip(rest_names, rest_sent)))
    recv = dict(zip(late_names, late_recv), **dict(zip(rest_names, rest_recv)))
    return loss, grad_x, sg, d_conv_w, sent, recv


_HBM = pl.BlockSpec(memory_space=pltpu.HBM)
_FLIPS = ((0, 0, 1), (1, 0, 0), (0, 1, 0), (1, 1, 0), (1, 0, 1), (0, 1, 1), (1, 1, 1))


def _place():
    return lax.axis_index("x"), lax.axis_index("y"), lax.axis_index("c")


def _flipped(place, flip):
    return tuple(1 - v if f else v for v, f in zip(place, flip))


def _gather_carry(blocks):
    nw = len(blocks)

    def copies(ins, outs, send_sems, recv_sems, local_sems):
        x, y, c = _place()
        me = 4 * x + 2 * y + c
        cps = []
        for w in range(nw):
            cps.append(pltpu.make_async_copy(ins[w], outs[w].at[me], local_sems.at[w]))
            for k, flip in enumerate(_FLIPS):
                cps.append(pltpu.make_async_remote_copy(
                    src_ref=ins[w], dst_ref=outs[w].at[me], send_sem=send_sems.at[7 * w + k],
                    recv_sem=recv_sems.at[7 * w + k], device_id=_flipped((x, y, c), flip), device_id_type=MESH))
        return cps

    return dict(ins=blocks, outs=[jax.ShapeDtypeStruct((N_DEV,) + b.shape, b.dtype) for b in blocks],
                n_remote=7 * nw, n_local=nw, copies=copies)


def _scatter_carry(by_dev):
    nw = len(by_dev)

    def copies(ins, outs, send_sems, recv_sems, local_sems):
        x, y, c = _place()
        cps = []
        for w in range(nw):
            for k, flip in enumerate(_FLIPS):
                px, py, pc = _flipped((x, y, c), flip)
                cps.append(pltpu.make_async_remote_copy(
                    src_ref=ins[w].at[4 * px + 2 * py + pc], dst_ref=outs[w].at[k], send_sem=send_sems.at[7 * w + k],
                    recv_sem=recv_sems.at[7 * w + k], device_id=(px, py, pc), device_id_type=MESH))
        return cps

    return dict(ins=by_dev, outs=[jax.ShapeDtypeStruct((7,) + b.shape[1:], b.dtype) for b in by_dev],
                n_remote=7 * nw, n_local=0, copies=copies)


def _all_gather(name, blocks):
    nw = len(blocks)

    def body(*refs):
        x_refs, out_refs = refs[:nw], refs[nw:2 * nw]
        send_sems, recv_sems, local_sems = refs[2 * nw:]
        x, y, c = _place()
        me, sibling = (x, y, c), (x, y, 1 - c)
        chips = [(1 - x, y), (x, 1 - y), (1 - x, 1 - y)]

        def slot(w, px, py, pc):
            return out_refs[w].at[4 * px + 2 * py + pc]

        def copy(w, k, blk, to, src=None):
            return pltpu.make_async_remote_copy(
                src_ref=slot(w, *blk) if src is None else src, dst_ref=slot(w, *blk),
                send_sem=send_sems.at[7 * w + k], recv_sem=recv_sems.at[7 * w + k], device_id=to, device_id_type=MESH)

        mine = [pltpu.make_async_copy(x_refs[w], slot(w, *me), local_sems.at[w]) for w in range(nw)]
        for cp in mine:
            cp.start()
        first = []
        for w in range(nw):
            first.append(copy(w, 0, me, sibling, src=x_refs[w]))
            first += [copy(w, 1 + j, me, (*chip, c), src=x_refs[w]) for j, chip in enumerate(chips)]
        for cp in first:
            cp.start()
        passed = []
        for j, chip in enumerate(chips):
            for w in range(nw):
                copy(w, 1 + j, (*chip, c), me).wait_recv()
                passed.append(copy(w, 4 + j, (*chip, c), sibling))
                passed[-1].start()
        for w in range(nw):
            copy(w, 0, sibling, me).wait_recv()
        for j, chip in enumerate(chips):
            for w in range(nw):
                copy(w, 4 + j, (*chip, 1 - c), me).wait_recv()
        for cp in first + passed:
            cp.wait_send()
        for cp in mine:
            cp.wait()

    return pl.pallas_call(
        body, name=name, out_shape=[jax.ShapeDtypeStruct((N_DEV,) + b.shape, b.dtype) for b in blocks],
        in_specs=[_HBM] * nw, out_specs=[_HBM] * nw,
        scratch_shapes=[pltpu.SemaphoreType.DMA((7 * nw,)), pltpu.SemaphoreType.DMA((7 * nw,)),
                        pltpu.SemaphoreType.DMA((nw,))],
    )(*blocks)


def _lane_pad(n):
    return -(-n // LANES) * LANES


def _pack_small(vecs, mat):
    width = max(sum(_lane_pad(v.shape[1]) for v in vecs), _lane_pad(mat.shape[1]))
    row0 = jnp.concatenate([_pad_cols(v, _lane_pad(v.shape[1])) for v in vecs], axis=1)
    rows = jnp.concatenate([_pad_cols(row0, width), _pad_cols(mat, width)], axis=0)
    return jnp.pad(rows, ((0, SUBLANES - rows.shape[0]), (0, 0)))


def _unpack_small(packed, sizes, mat_cols):
    vecs, off = [], 0
    for n in sizes:
        vecs.append(packed[0:1, off:off + n])
        off += _lane_pad(n)
    return vecs, packed[1:1 + CONV_WIDTH, :mat_cols]


def _adamw(w, g, m, v):
    m = ADAM_B1 * m + (1.0 - ADAM_B1) * g
    v = ADAM_B2 * v + (1.0 - ADAM_B2) * (g * g)
    m_hat = m / (1.0 - ADAM_B1 ** ADAM_STEP)
    v_hat = v / (1.0 - ADAM_B2 ** ADAM_STEP)
    delta = -ADAM_LR * (m_hat / (jnp.sqrt(v_hat) + ADAM_EPS) + ADAM_WD * w)
    return delta, m, v


BIG = ("w_in", "w_uq", "w_ukv", "w_attn_o", "w_ssm_o", "w_out", "w_gate", "w_up", "w_down", "w_ple_gate", "w_ple")
FIRST = ("w_in", "w_uq", "w_ukv")
LATE = ("w_attn_o", "w_ssm_o", "w_out", "w_gate", "w_up", "w_down", "w_ple_gate", "w_ple")
COL_SHARDED = ("w_in", "w_uq", "w_ukv", "w_gate", "w_up", "w_ple")
SMALL = ("mix_norm_pre", "mix_norm_post", "q_norm", "kv_norm", "conv_b", "dt_bias", "a_log", "d_skip", "ssm_norm",
         "ffn_norm_pre", "ffn_norm_post", "ple_norm_pre", "ple_norm_post")
WEIGHTS = ("mix_norm_pre", "mix_norm_post", "w_in", "q_norm", "w_uq", "kv_norm", "w_ukv", "conv_w", "conv_b", "dt_bias",
           "a_log", "d_skip", "ssm_norm", "w_attn_o", "w_ssm_o", "w_out", "ffn_norm_pre", "ffn_norm_post", "w_gate",
           "w_up", "w_down", "ple_norm_pre", "ple_norm_post", "w_ple_gate", "w_ple")


def _step(x, p, positions, target, w, m, v):
    xi, yi, ci = _place()
    me = 4 * xi + 2 * yi + ci

    gathered = _all_gather("gather_weights", [w[n].astype(BF16) for n in FIRST])
    wts = {n: _assemble(n, blocks) for n, blocks in zip(FIRST, gathered)}
    cw_rows, cw_cols = w["conv_w"].shape
    (cw_all,) = _all_gather("gather_conv_w", [jnp.pad(w["conv_w"], ((0, SUBLANES - cw_rows), (0, 0)))])
    conv_w = cw_all[:, :cw_rows, :].transpose(1, 0, 2).reshape(cw_rows, N_DEV * cw_cols)

    small = {n: w[n] for n in SMALL}
    loss, grad_x, sg, d_conv_w, sent, recv = _local_step(
        x, p, positions, target, small, conv_w, wts, LATE, [w[n].astype(BF16) for n in LATE])

    sizes = [w[n].shape[1] for n in SMALL]
    sg_pack = _pack_small([sg[n] for n in SMALL] + [loss], d_conv_w)
    (sg_all,) = _all_gather("gather_small_grads", [sg_pack])
    (sg_sum,) = _rowwise("sum_small_grads", lambda *a: functools.reduce(lambda s, t: s + t, a),
                         [sg_all[k] for k in range(N_DEV)], [], [(sg_pack.shape[1], F32)])
    sg_vecs, d_conv_w_sum = _unpack_small(sg_sum, sizes + [LANES], d_conv_w.shape[1])
    loss = sg_vecs[-1][0, 0]
    grads = dict(zip(SMALL, sg_vecs[:-1]))
    grads["conv_w"] = lax.dynamic_slice_in_dim(d_conv_w_sum, me * cw_cols, cw_cols, axis=1)

    def sum8_then_adamw(wv, mv, vv, own, *others):
        g = functools.reduce(lambda s, t: s + t, others, own)
        return (g,) + _adamw(wv, g, mv, vv)

    delta, new_m, new_v = {}, {}, {}
    for n in BIG:
        cols = w[n].shape[1]
        own = lax.dynamic_index_in_dim(sent[n], me, axis=0, keepdims=False)
        grads[n], delta[n], new_m[n], new_v[n] = _rowwise(
            "adamw_" + n, sum8_then_adamw, [w[n], m[n], v[n], own] + [recv[n][k] for k in range(N_DEV - 1)], [],
            [(cols, F32)] * 4)
    packed = [_pack_small([d[n] for n in SMALL], d["conv_w"]) for d in (w, grads, m, v)]
    outs = _rowwise("adamw_small", _adamw, packed, [], [(packed[0].shape[1], F32)] * 3)
    for d, o in zip((delta, new_m, new_v), outs):
        vecs, mat = _unpack_small(o, sizes, cw_cols)
        d.update(zip(SMALL, vecs))
        d["conv_w"] = mat
    return loss, grad_x, grads, delta, new_m, new_v


def kernel(x, p, positions, mix_norm_pre, mix_norm_post, w_in, q_norm, w_uq, kv_norm, w_ukv, conv_w, conv_b, dt_bias, a_log, d_skip, ssm_norm, w_attn_o, w_ssm_o, w_out, ffn_norm_pre, ffn_norm_post, w_gate, w_up, w_down, ple_norm_pre, ple_norm_post, w_ple_gate, w_ple, loss_target, m_mix_norm_pre, m_mix_norm_post, m_w_in, m_q_norm, m_w_uq, m_kv_norm, m_w_ukv, m_conv_w, m_conv_b, m_dt_bias, m_a_log, m_d_skip, m_ssm_norm, m_w_attn_o, m_w_ssm_o, m_w_out, m_ffn_norm_pre, m_ffn_norm_post, m_w_gate, m_w_up, m_w_down, m_ple_norm_pre, m_ple_norm_post, m_w_ple_gate, m_w_ple, v_mix_norm_pre, v_mix_norm_post, v_w_in, v_q_norm, v_w_uq, v_kv_norm, v_w_ukv, v_conv_w, v_conv_b, v_dt_bias, v_a_log, v_d_skip, v_ssm_norm, v_w_attn_o, v_w_ssm_o, v_w_out, v_ffn_norm_pre, v_ffn_norm_post, v_w_gate, v_w_up, v_w_down, v_ple_norm_pre, v_ple_norm_post, v_w_ple_gate, v_w_ple):
    w_args = (mix_norm_pre, mix_norm_post, w_in, q_norm, w_uq, kv_norm, w_ukv, conv_w, conv_b, dt_bias, a_log, d_skip, ssm_norm, w_attn_o, w_ssm_o, w_out, ffn_norm_pre, ffn_norm_post, w_gate, w_up, w_down, ple_norm_pre, ple_norm_post, w_ple_gate, w_ple)
    m_args = (m_mix_norm_pre, m_mix_norm_post, m_w_in, m_q_norm, m_w_uq, m_kv_norm, m_w_ukv, m_conv_w, m_conv_b, m_dt_bias, m_a_log, m_d_skip, m_ssm_norm, m_w_attn_o, m_w_ssm_o, m_w_out, m_ffn_norm_pre, m_ffn_norm_post, m_w_gate, m_w_up, m_w_down, m_ple_norm_pre, m_ple_norm_post, m_w_ple_gate, m_w_ple)
    v_args = (v_mix_norm_pre, v_mix_norm_post, v_w_in, v_q_norm, v_w_uq, v_kv_norm, v_w_ukv, v_conv_w, v_conv_b, v_dt_bias, v_a_log, v_d_skip, v_ssm_norm, v_w_attn_o, v_w_ssm_o, v_w_out, v_ffn_norm_pre, v_ffn_norm_post, v_w_gate, v_w_up, v_w_down, v_ple_norm_pre, v_ple_norm_post, v_w_ple_gate, v_w_ple)

    def drop_layer(a):
        return a if a.ndim == 2 else a[0]

    w = {n: drop_layer(a) for n, a in zip(WEIGHTS, w_args)}
    m = {n: drop_layer(a) for n, a in zip(WEIGHTS, m_args)}
    v = {n: drop_layer(a) for n, a in zip(WEIGHTS, v_args)}
    loss, grad_x, grads, delta, new_m, new_v = _step(x[0], p[0, 0], positions[0], loss_target[0], w, m, v)
    like = lambda d: [d[n].reshape(a.shape) for n, a in zip(WEIGHTS, w_args)]
    return (loss, grad_x[None], *like(grads), *like(delta), *like(new_m), *like(new_v))
```

```python
import functools

import jax
import jax.numpy as jnp
from jax import lax
from jax.experimental import pallas as pl
from jax.experimental.pallas import tpu as pltpu

F32 = jnp.float32
BF16 = jnp.bfloat16

EPS = 1e-6
QK_NOPE = 128
QK_ROPE = 64
V_DIM = 128
QK_PAD = 256
ROPE_THETA = 10000.0
SSM_HEADDIM = 64
D_STATE = 128
CONV_WIDTH = 4
CHUNK = 256
ADAM_LR = 0.001
ADAM_B1 = 0.9
ADAM_B2 = 0.999
ADAM_EPS = 1e-08
ADAM_WD = 0.01
ADAM_STEP = 10

N_DEV = 8
LANES = 128
SUBLANES = 8
VMEM_LIMIT = 56 * 1024 * 1024
ROW_TILE_BYTES = 16 * 1024 * 1024
STRIP_ROWS = 16
STRIP_ELEMS = 128 * 1024
FLASH_T = 512
FLASH_FWD_RUN = 8
FLASH_BWD_RUN = 8
MM_TILE_BYTES = 20 * 1024 * 1024
MESH = pl.DeviceIdType.MESH


def _pick(dim, prefs):
    if dim <= prefs[0]:
        return dim
    for p in prefs:
        if dim % p == 0:
            return p
    return dim


def _tile(dim, cap):
    if dim <= cap:
        return dim
    best = None
    for t in range(LANES, cap + 1, LANES):
        if dim % t == 0:
            best = t
    return best if best is not None else dim


def _params(sem):
    return pltpu.CompilerParams(dimension_semantics=sem, vmem_limit_bytes=VMEM_LIMIT)


def _dot(a, b):
    return lax.dot_general(a, b, (((1,), (0,)), ((), ())), preferred_element_type=F32)


def _dot_nt(a, b):
    return lax.dot_general(a, b, (((1,), (1,)), ((), ())), preferred_element_type=F32)


def _dot_tn(a, b):
    return lax.dot_general(a, b, (((0,), (0,)), ((), ())), preferred_element_type=F32)


def _mm(name, a, b, mode, out_dtype=F32, acc_in=None, carry=None):
    if mode == "nn":
        (M, K), (K2, N) = a.shape, b.shape
    elif mode == "nt":
        (M, K), (N, K2) = a.shape, b.shape
    else:
        (K, M), (K2, N) = a.shape, b.shape
    assert K == K2, (name, a.shape, b.shape, mode)
    tm = _tile(M, 1024)
    tn = _tile(N, 1024 if acc_in is not None else 1536)
    tk = _tile(K, 2048)
    while tk > 512 and 2 * (tm * tk * a.dtype.itemsize + tk * tn * b.dtype.itemsize) > MM_TILE_BYTES:
        tk = _tile(K, tk - LANES)
    nk = K // tk
    dot = {"nn": _dot, "nt": _dot_nt, "tn": _dot_tn}[mode]
    has_acc = acc_in is not None

    def body(*refs):
        if has_acc:
            a_ref, b_ref, c_ref, o_ref, acc = refs
        else:
            a_ref, b_ref, o_ref, acc = refs
        k = pl.program_id(2)

        def product():
            return dot(a_ref[...].astype(BF16), b_ref[...].astype(BF16))

        def finish(r):
            if has_acc:
                r = r + c_ref[...]
            o_ref[...] = r.astype(o_ref.dtype)

        if nk == 1:
            finish(product())
            return

        @pl.when(k == 0)
        def _():
            acc[...] = product()

        @pl.when(jnp.logical_and(k > 0, k < nk - 1))
        def _():
            acc[...] += product()

        @pl.when(k == nk - 1)
        def _():
            finish(acc[...] + product())

    if mode == "tn":
        a_spec = pl.BlockSpec((tk, tm), lambda i, j, k: (k, i))
    else:
        a_spec = pl.BlockSpec((tm, tk), lambda i, j, k: (i, k))
    if mode == "nt":
        b_spec = pl.BlockSpec((tn, tk), lambda i, j, k: (j, k))
    else:
        b_spec = pl.BlockSpec((tk, tn), lambda i, j, k: (k, j))
    o_spec = pl.BlockSpec((tm, tn), lambda i, j, k: (i, j))
    in_specs = [a_spec, b_spec] + ([o_spec] if has_acc else [])
    args = (a, b) + ((acc_in,) if has_acc else ())
    if carry is None:
        return pl.pallas_call(
            body, name=name, grid=(M // tm, N // tn, nk), in_specs=in_specs, out_specs=o_spec,
            out_shape=jax.ShapeDtypeStruct((M, N), out_dtype), scratch_shapes=[pltpu.VMEM((tm, tn), F32)],
            input_output_aliases=({2: 0} if has_acc and out_dtype == F32 else {}),
            compiler_params=_params(("parallel", "parallel", "arbitrary")),
        )(*args)
    grid = (M // tm, N // tn, nk)
    c_args, c_in_specs, c_out_specs, c_out_shapes, c_sems = _carry_call(carry)
    res = pl.pallas_call(
        _carried(body, len(args), 1, 1, carry, grid), name=name, grid=grid, in_specs=in_specs + c_in_specs,
        out_specs=[o_spec] + c_out_specs, out_shape=[jax.ShapeDtypeStruct((M, N), out_dtype)] + c_out_shapes,
        scratch_shapes=[pltpu.VMEM((tm, tn), F32)] + c_sems,
        compiler_params=_params(("arbitrary", "arbitrary", "arbitrary")),
    )(*args, *c_args)
    return res[0], list(res[1:])


def _row_tile(n_rows, bytes_per_row):
    tr = 1024
    while tr > SUBLANES and tr * bytes_per_row > ROW_TILE_BYTES:
        tr //= 2
    while n_rows % tr:
        tr //= 2
    return tr


def _strip_rows(width):
    return max(STRIP_ROWS, min(4 * STRIP_ROWS, STRIP_ELEMS // width // STRIP_ROWS * STRIP_ROWS))


def _over_strips(tr, strip, work):
    if strip is None or tr <= strip or tr % strip:
        work(slice(None))
        return

    def one(s, carry):
        work(pl.ds(pl.multiple_of(s * strip, strip), strip))
        return carry

    lax.fori_loop(0, tr // strip, one, 0)


def _acc_add(a_ref, v):
    if v.shape[0] == 1:
        a_ref[0:1, :] += v
    else:
        a_ref[...] += v.reshape(v.shape[0] // SUBLANES, SUBLANES, v.shape[1]).sum(axis=0)


def _rowwise(name, fn, rows, bcs, outs, accs=(), tr=None):
    n_rows = rows[0].shape[0]
    strip = _strip_rows(max(r.shape[1] for r in rows)) if tr is None else None
    if tr is None:
        per_row = sum(r.shape[1] * r.dtype.itemsize for r in rows) + sum(w * jnp.dtype(d).itemsize for w, d in outs)
        tr = _row_tile(n_rows, per_row)
    n_r, n_b, n_o, n_a = len(rows), len(bcs), len(outs), len(accs)

    def body(*refs):
        r_refs, b_refs = refs[:n_r], refs[n_r: n_r + n_b]
        o_refs = refs[n_r + n_b: n_r + n_b + n_o]
        a_refs = refs[n_r + n_b + n_o:]
        if n_a:
            @pl.when(pl.program_id(0) == 0)
            def _():
                for a in a_refs:
                    a[...] = jnp.zeros_like(a)

        def work(rws):
            ins = [r[rws, :].astype(F32) for r in r_refs] + [b[...].astype(F32) for b in b_refs]
            res = fn(*ins)
            res = res if isinstance(res, (tuple, list)) else (res,)
            for o, v in zip(o_refs, res[:n_o]):
                o[rws, :] = v.astype(o.dtype)
            for a, v in zip(a_refs, res[n_o:]):
                _acc_add(a, v)

        _over_strips(tr, strip, work)

    in_specs = [pl.BlockSpec((tr, r.shape[1]), lambda i: (i, 0)) for r in rows]
    in_specs += [pl.BlockSpec((1, b.shape[1]), lambda i: (0, 0)) for b in bcs]
    out_specs = [pl.BlockSpec((tr, w), lambda i: (i, 0)) for w, _ in outs]
    out_specs += [pl.BlockSpec((SUBLANES, w), lambda i: (0, 0)) for w in accs]
    out_shape = [jax.ShapeDtypeStruct((n_rows, w), d) for w, d in outs]
    out_shape += [jax.ShapeDtypeStruct((SUBLANES, w), F32) for w in accs]
    res = pl.pallas_call(
        body, name=name, grid=(n_rows // tr,), in_specs=in_specs, out_specs=out_specs, out_shape=out_shape,
        compiler_params=_params(("arbitrary",) if n_a else ("parallel",)),
    )(*rows, *bcs)
    return tuple(res)


def _rowwise_bwd(name, fn, rows, bcs, cts, need_rows, row_dtypes, need_bcs=None, adds=None, fwd_sums=(), tr=None,
                 into=None):
    n_rows = rows[0].shape[0]
    adds = adds or {}
    into = into or {}
    into_keys = sorted(into)
    need_bcs = list(range(len(bcs))) if need_bcs is None else list(need_bcs)
    ct_arrays = [c for c in cts if not isinstance(c, float)]
    add_keys = sorted(adds)
    add_arrays = [adds[k] for k in add_keys]
    strip = _strip_rows(max(r.shape[1] for r in rows)) if tr is None else None
    if tr is None:
        per_row = sum(r.shape[1] * r.dtype.itemsize for r in list(rows) + ct_arrays + add_arrays)
        per_row += sum(rows[i].shape[1] * jnp.dtype(d).itemsize for i, d in zip(need_rows, row_dtypes))
        tr = _row_tile(n_rows, per_row)
    n_r, n_b, n_c, n_ad = len(rows), len(bcs), len(ct_arrays), len(add_arrays)
    n_go, n_gb, n_fs = len(need_rows), len(need_bcs), len(fwd_sums)

    def body(*refs):
        pos = [0]

        def take(n):
            pos[0] += n
            return refs[pos[0] - n: pos[0]]

        r_refs, b_refs, c_refs, ad_refs = take(n_r), take(n_b), take(n_c), take(n_ad)
        take(len(into_keys))
        go_refs = take(n_go)
        acc_refs = refs[pos[0]:]

        @pl.when(pl.program_id(0) == 0)
        def _():
            for a in acc_refs:
                a[...] = jnp.zeros_like(a)

        def wrapped(*a):
            r = fn(*a)
            return tuple(r) if isinstance(r, (tuple, list)) else (r,)

        def work(rws):
            r_t = [r[rws, :].astype(F32) for r in r_refs]
            b_t = [r[...].astype(F32) for r in b_refs]
            outs, vjp = jax.vjp(wrapped, *r_t, *b_t)
            it = iter(c_refs)
            full = tuple(jnp.full(o.shape, c, F32) if isinstance(c, float) else next(it)[rws, :].astype(F32)
                         for o, c in zip(outs, cts))
            grads = vjp(full)
            for o_ref, i in zip(go_refs, need_rows):
                g = grads[i]
                if i in adds:
                    g = g + ad_refs[add_keys.index(i)][rws, :].astype(F32)
                o_ref[rws, :] = g.astype(o_ref.dtype)
            for a, j in zip(acc_refs[:n_gb], need_bcs):
                _acc_add(a, grads[n_r + j])
            for a, j in zip(acc_refs[n_gb:], fwd_sums):
                a[0:1, :] += jnp.full((1, LANES), jnp.sum(outs[j]), F32)

        _over_strips(tr, strip, work)

    def row_spec(w):
        return pl.BlockSpec((tr, w), lambda i: (i, 0))

    in_specs = [row_spec(r.shape[1]) for r in rows]
    in_specs += [pl.BlockSpec((1, b.shape[1]), lambda i: (0, 0)) for b in bcs]
    in_specs += [row_spec(c.shape[1]) for c in ct_arrays] + [row_spec(a.shape[1]) for a in add_arrays]
    in_specs += [pl.BlockSpec(memory_space=pl.ANY) for _ in into_keys]
    out_specs = [row_spec(rows[i].shape[1]) for i in need_rows]
    out_shape = [jax.ShapeDtypeStruct((n_rows, rows[i].shape[1]), d) for i, d in zip(need_rows, row_dtypes)]
    aliases = {}
    for pos_in, k in enumerate(into_keys):
        buf, col_block = into[k]
        out_specs[k] = pl.BlockSpec((tr, rows[need_rows[k]].shape[1]), lambda i, cb=col_block: (i, cb))
        out_shape[k] = jax.ShapeDtypeStruct(buf.shape, buf.dtype)
        aliases[len(in_specs) - len(into_keys) + pos_in] = k
    out_specs += [pl.BlockSpec((SUBLANES, bcs[j].shape[1]), lambda i: (0, 0)) for j in need_bcs]
    out_specs += [pl.BlockSpec((SUBLANES, LANES), lambda i: (0, 0)) for _ in fwd_sums]
    out_shape += [jax.ShapeDtypeStruct((SUBLANES, bcs[j].shape[1]), F32) for j in need_bcs]
    out_shape += [jax.ShapeDtypeStruct((SUBLANES, LANES), F32) for _ in fwd_sums]
    res = pl.pallas_call(
        body, name=name, grid=(n_rows // tr,), in_specs=in_specs, out_specs=out_specs, out_shape=out_shape,
        input_output_aliases=aliases, compiler_params=_params(("arbitrary",)),
    )(*rows, *bcs, *ct_arrays, *add_arrays, *[into[k][0] for k in into_keys])
    return tuple(res)


def _fold(acc):
    return jnp.sum(acc, axis=0, keepdims=True)


def _rms(x, w):
    return x * lax.rsqrt(jnp.mean(x * x, axis=-1, keepdims=True) + EPS) * w


def _sigmoid(x):
    return jax.nn.sigmoid(x)


def _silu(x):
    return x * _sigmoid(x)


def _log1p(u):
    series = u * (1.0 - u * (0.5 - u * (1.0 / 3.0 - u * 0.25)))
    return jnp.where(u < 0.01, series, jnp.log(1.0 + u))


def _softplus(x):
    return jnp.maximum(x, 0.0) + _log1p(jnp.exp(-jnp.abs(x)))


def _st_pre(x, w):
    return _rms(x, w)


def _st_lora_norms(dq):
    def fn(cqkv, qn, kvn):
        return _rms(cqkv[:, :dq], qn), _rms(cqkv[:, dq:], kvn)
    return fn


def _st_gated_norm(n_groups):
    def fn(y, xs, z, dskip, wn):
        yz = (y + dskip * xs) * _silu(z)
        gw = yz.shape[1] // n_groups
        parts = [_rms(yz[:, g * gw:(g + 1) * gw], wn[:, g * gw:(g + 1) * gw]) for g in range(n_groups)]
        return jnp.concatenate(parts, axis=1)
    return fn


def _st_mix(d):
    def fn(g, ao, so):
        return _sigmoid(g[:, :d]) * ao + _sigmoid(g[:, d:]) * so
    return fn


def _st_res_norm(h, y, w_post, w_pre):
    h2 = h + _rms(y, w_post)
    return h2, _rms(h2, w_pre)


def _st_swiglu(gt, up):
    return _silu(gt) * up


def _st_loss(pe, gl, h2, tgt, w_post):
    e = pe * _sigmoid(gl)
    diff = h2 + _rms(e, w_post) - tgt
    return 0.5 * jnp.mean(diff * diff, axis=-1, keepdims=True)


def _rope_tables(positions):
    half = QK_ROPE // 2
    inv_freq = ROPE_THETA ** (-jnp.arange(0, QK_ROPE, 2, dtype=F32) / QK_ROPE)
    ang = positions.astype(F32).reshape(-1, 1) * inv_freq
    cos, sin = jnp.cos(ang), jnp.sin(ang)
    n = ang.shape[0]
    z = lambda w: jnp.zeros((n, w), F32)
    c_tab = jnp.concatenate([jnp.ones((n, QK_NOPE), F32), cos, cos, z(QK_PAD - QK_NOPE - QK_ROPE)], axis=1)
    a_tab = jnp.concatenate([z(QK_NOPE), -sin, z(half), z(QK_PAD - QK_NOPE - QK_ROPE)], axis=1)
    b_tab = jnp.concatenate([z(QK_NOPE), z(half), sin, z(QK_PAD - QK_NOPE - QK_ROPE)], axis=1)
    return c_tab, a_tab, b_tab


def _rot(x, c, a, b):
    half = QK_ROPE // 2
    return x * c + pltpu.roll(x, QK_PAD - half, axis=1) * a + pltpu.roll(x, half, axis=1) * b


def _rot_t(g, c, a, b):
    half = QK_ROPE // 2
    return g * c + pltpu.roll(g * a, half, axis=1) + pltpu.roll(g * b, QK_PAD - half, axis=1)


def _st_rope(n_heads):
    def fn(qraw, kraw, sm, c, a, b):
        kpe = _rot(sm[:, :QK_PAD], c, a, b)
        scale = float(QK_NOPE + QK_ROPE) ** -0.5
        q = [_rot(qraw[:, h * QK_PAD:(h + 1) * QK_PAD], c, a, b) * scale for h in range(n_heads)]
        k = [kraw[:, h * QK_PAD:(h + 1) * QK_PAD] + kpe for h in range(n_heads)]
        return jnp.concatenate(q, axis=1), jnp.concatenate(k, axis=1)
    return fn


def _st_rope_bwd(n_heads):
    def fn(dq, dk, c, a, b):
        dqraw = [_rot_t(dq[:, h * QK_PAD:(h + 1) * QK_PAD], c, a, b) for h in range(n_heads)]
        dks = dk[:, :QK_PAD]
        for h in range(1, n_heads):
            dks = dks + dk[:, h * QK_PAD:(h + 1) * QK_PAD]
        return jnp.concatenate(dqraw, axis=1), _rot_t(dks, c, a, b), dk
    return fn


def _split3(x):
    h1 = x.astype(BF16)
    r1 = x - h1.astype(F32)
    h2 = r1.astype(BF16)
    h3 = (r1 - h2.astype(F32)).astype(BF16)
    return h1, h2, h3


def _tri_dot(tri, x):
    h1, h2, h3 = _split3(x)
    return (_dot(tri, h3) + _dot(tri, h2)) + _dot(tri, h1)


def _dot_tri(x, tri):
    h1, h2, h3 = _split3(x)
    return (_dot(h3, tri) + _dot(h2, tri)) + _dot(h1, tri)


def _st_dt(sm, bias, alog):
    x = sm[:, QK_PAD:] + bias
    dt = _softplus(x)
    return dt, dt * (-jnp.exp(alog))


def _st_dt_bwd(sm, ddt, dcum, bias, alog):
    n = sm.shape[0]
    i = lax.broadcasted_iota(jnp.int32, (n, n), 0)
    j = lax.broadcasted_iota(jnp.int32, (n, n), 1)
    upper = (j >= i).astype(BF16)
    dda = _tri_dot(upper, dcum)
    x = sm[:, QK_PAD:] + bias
    dt = _softplus(x)
    a = -jnp.exp(alog)
    draw = (ddt + dda * a) * _sigmoid(x)
    return draw, draw, dda * dt * a


def _conv_fwd(xbc, w, b):
    S, C = xbc.shape
    tr = _pick(S, (512, 256))
    tc = _pick(C, (1024, 512, 256, 128))
    hb = tr // SUBLANES

    def body(x_ref, halo_ref, w_ref, b_ref, c_ref, a_ref, ext):
        i = pl.program_id(1)
        halo = jnp.where(i == 0, 0.0, halo_ref[...])
        ext[0:SUBLANES, :] = halo
        ext[SUBLANES:, :] = x_ref[...]

        def lane_strip(c, carry):
            lanes = pl.ds(pl.multiple_of(c * LANES, LANES), LANES)
            wv = w_ref[:, lanes]
            acc = b_ref[:, lanes] + wv[CONV_WIDTH - 1:CONV_WIDTH, :] * ext[pl.ds(SUBLANES, tr), lanes]
            for k in range(CONV_WIDTH - 1):
                off = SUBLANES - (CONV_WIDTH - 1) + k
                acc = acc + wv[k:k + 1, :] * ext[pl.ds(off, tr), lanes]
            c_ref[:, lanes] = acc
            a_ref[:, lanes] = _silu(acc)
            return carry

        lax.fori_loop(0, tc // LANES, lane_strip, 0)

    return pl.pallas_call(
        body, name="conv_fwd", grid=(C // tc, S // tr),
        in_specs=[pl.BlockSpec((tr, tc), lambda j, i: (i, j)),
                  pl.BlockSpec((SUBLANES, tc), lambda j, i: (jnp.maximum(i * hb - 1, 0), j)),
                  pl.BlockSpec((CONV_WIDTH, tc), lambda j, i: (0, j)),
                  pl.BlockSpec((1, tc), lambda j, i: (0, j))],
        out_specs=[pl.BlockSpec((tr, tc), lambda j, i: (i, j))] * 2,
        out_shape=[jax.ShapeDtypeStruct((S, C), F32)] * 2,
        scratch_shapes=[pltpu.VMEM((tr + SUBLANES, tc), F32)],
        compiler_params=_params(("parallel", "arbitrary")),
    )(xbc, xbc, w, b)


def _conv_bwd(xbc, dconv, w, buf, col0):
    S, C = xbc.shape
    tr = _pick(S, (512, 256))
    tc = _pick(C, (1024, 512, 256, 128))
    hb = tr // SUBLANES
    n_i = S // tr
    assert col0 % tc == 0, (col0, tc)
    cb0 = col0 // tc

    def body(x_ref, halo_ref, d_ref, dnext_ref, w_ref, buf_ref, dx_ref, dw_ref, ext, dext):
        i = pl.program_id(1)
        ext[0:SUBLANES, :] = jnp.where(i == 0, 0.0, halo_ref[...])
        ext[SUBLANES:, :] = x_ref[...]
        dext[0:tr, :] = d_ref[...]
        dext[tr:, :] = jnp.where(i == n_i - 1, 0.0, dnext_ref[...])

        @pl.when(i == 0)
        def _():
            dw_ref[...] = jnp.zeros_like(dw_ref)

        def lane_strip(c, carry):
            lanes = pl.ds(pl.multiple_of(c * LANES, LANES), LANES)
            wv = w_ref[:, lanes]
            d = dext[pl.ds(0, tr), lanes]
            dx = wv[CONV_WIDTH - 1:CONV_WIDTH, :] * d
            for k in range(CONV_WIDTH):
                if k < CONV_WIDTH - 1:
                    dx = dx + wv[k:k + 1, :] * dext[pl.ds(CONV_WIDTH - 1 - k, tr), lanes]
                xs = ext[pl.ds(SUBLANES - (CONV_WIDTH - 1) + k, tr), lanes]
                prod = d * xs
                dw_ref[k * SUBLANES:(k + 1) * SUBLANES, lanes] += prod.reshape(tr // SUBLANES, SUBLANES, LANES).sum(axis=0)
            dx_ref[:, lanes] = dx.astype(dx_ref.dtype)
            return carry

        lax.fori_loop(0, tc // LANES, lane_strip, 0)

    return pl.pallas_call(
        body, name="conv_bwd", grid=(C // tc, n_i),
        in_specs=[pl.BlockSpec((tr, tc), lambda j, i: (i, j)),
                  pl.BlockSpec((SUBLANES, tc), lambda j, i: (jnp.maximum(i * hb - 1, 0), j)),
                  pl.BlockSpec((tr, tc), lambda j, i: (i, j)),
                  pl.BlockSpec((SUBLANES, tc), lambda j, i: (jnp.minimum((i + 1) * hb, S // SUBLANES - 1), j)),
                  pl.BlockSpec((CONV_WIDTH, tc), lambda j, i: (0, j)),
                  pl.BlockSpec(memory_space=pl.ANY)],
        out_specs=[pl.BlockSpec((tr, tc), lambda j, i: (i, cb0 + j)),
                   pl.BlockSpec((CONV_WIDTH * SUBLANES, tc), lambda j, i: (0, j))],
        out_shape=[jax.ShapeDtypeStruct(buf.shape, buf.dtype), jax.ShapeDtypeStruct((CONV_WIDTH * SUBLANES, C), F32)],
        scratch_shapes=[pltpu.VMEM((tr + SUBLANES, tc), F32), pltpu.VMEM((tr + SUBLANES, tc), F32)],
        input_output_aliases={5: 0},
        compiler_params=_params(("parallel", "arbitrary")),
    )(xbc, xbc, dconv, dconv, w, buf)


def _st_dconv(d_inner):
    def fn(xc, dxa, dxb, db_, dc_):
        s = _sigmoid(xc)
        g = jnp.concatenate([dxa + dxb, db_, dc_], axis=1) * (s * (1.0 + xc * (1.0 - s)))
        return g, g
    return fn


def _chunk_setup_t(b_ref, c_ref, dac_ref, dar_ref, L):
    ii = lax.broadcasted_iota(jnp.int32, (L, L), 0)
    jj = lax.broadcasted_iota(jnp.int32, (L, L), 1)
    lower = ii >= jj
    upper = ii <= jj
    cum_c = _tri_dot(lower.astype(BF16), dac_ref[0])
    cum_r = _dot_tri(dar_ref[0], upper.astype(BF16))
    bm = b_ref[...].astype(BF16)
    cm = c_ref[...].astype(BF16)
    return lower, upper, cum_c, cum_r, bm, cm


def _ssd_specs_t(d_inner, n_groups, gw, L, rp, chunk_of):
    bb0 = d_inner // D_STATE
    cb0 = bb0 + n_groups
    return [pl.BlockSpec((L, gw), lambda g, c: (chunk_of(c), g)),
            pl.BlockSpec((L, D_STATE), lambda g, c: (chunk_of(c), bb0 + g)),
            pl.BlockSpec((L, D_STATE), lambda g, c: (chunk_of(c), cb0 + g)),
            pl.BlockSpec((1, rp, L), lambda g, c: (g, 0, chunk_of(c))),
            pl.BlockSpec((1, L, LANES), lambda g, c: (g, chunk_of(c), 0)),
            pl.BlockSpec((1, rp, L), lambda g, c: (g, 0, chunk_of(c)))]


def _ssd_fwd_t(xbc_a, dt_row, da_col, da_row, d_inner, n_groups, R):
    S = xbc_a.shape[0]
    L = min(CHUNK, S)
    NC = S // L
    P, N = SSM_HEADDIM, D_STATE
    gw = R * P
    rp = da_row.shape[1]

    def body(x_ref, b_ref, c_ref, dt_ref, dac_ref, dar_ref, y_ref, st_ref, state, y_t):
        @pl.when(pl.program_id(1) == 0)
        def _():
            state[...] = jnp.zeros_like(state)

        st_ref[0, 0] = state[...]
        lower, upper, cum_c, cum_r, bm, cm = _chunk_setup_t(b_ref, c_ref, dac_ref, dar_ref, L)
        gm_t = _dot_nt(bm, cm)
        dt = dt_ref[0]
        x_t = x_ref[...].T
        for r in range(R):
            cc = cum_c[:, r:r + 1]
            cr = cum_r[r:r + 1, :]
            m_t = (gm_t * jnp.exp(jnp.where(upper, cr - cc, -jnp.inf))).astype(BF16)
            x = x_t[r * P:(r + 1) * P, :] * dt[r:r + 1, :]
            s_r = state[r * P:(r + 1) * P, :]
            y_off = _dot_nt(s_r.astype(BF16), cm) * jnp.exp(cr)
            y_t[r * P:(r + 1) * P, :] = _dot(x.astype(BF16), m_t) + y_off
            last = cr[:, L - 1:L]
            xw = (x * jnp.exp(last - cr)).astype(BF16)
            state[r * P:(r + 1) * P, :] = s_r * jnp.exp(last) + _dot(xw, bm)
        y_ref[...] = y_t[...].T

    return pl.pallas_call(
        body, name="ssd_fwd", grid=(n_groups, NC),
        in_specs=_ssd_specs_t(d_inner, n_groups, gw, L, rp, lambda c: c),
        out_specs=[pl.BlockSpec((L, gw), lambda g, c: (c, g)),
                   pl.BlockSpec((1, 1, gw, N), lambda g, c: (g, c, 0, 0))],
        out_shape=[jax.ShapeDtypeStruct((S, d_inner), F32), jax.ShapeDtypeStruct((n_groups, NC, gw, N), F32)],
        scratch_shapes=[pltpu.VMEM((gw, N), F32), pltpu.VMEM((gw, L), F32)],
        compiler_params=_params(("parallel", "arbitrary")),
    )(xbc_a, xbc_a, xbc_a, dt_row, da_col, da_row)


def _ssd_bwd_t(xbc_a, dt_row, da_col, da_row, states, dy, d_inner, n_groups, R):
    S = xbc_a.shape[0]
    L = min(CHUNK, S)
    NC = S // L
    P, N = SSM_HEADDIM, D_STATE
    gw = R * P
    rp = da_row.shape[1]
    rev = lambda c: NC - 1 - c

    def body(x_ref, b_ref, c_ref, dt_ref, dac_ref, dar_ref, st_ref, dy_ref,
             dx_ref, db_ref, dc_ref, ddt_ref, dcum_ref, dstate, dx_t):
        @pl.when(pl.program_id(1) == 0)
        def _():
            dstate[...] = jnp.zeros_like(dstate)

        lower, upper, cum_c, cum_r, bm, cm = _chunk_setup_t(b_ref, c_ref, dac_ref, dar_ref, L)
        gm = _dot_nt(cm, bm)
        gm_t = _dot_nt(bm, cm)
        dt = dt_ref[0]
        x_t = x_ref[...].T
        dy_t = dy_ref[...].T
        sub = lax.broadcasted_iota(jnp.int32, (rp, L), 0)
        is_last = lax.broadcasted_iota(jnp.int32, (1, L), 1) == L - 1
        d_g = jnp.zeros((L, L), F32)
        d_g_t = jnp.zeros((L, L), F32)
        dc_acc = jnp.zeros((L, N), F32)
        db_acc = jnp.zeros((L, N), F32)
        ddt_out = jnp.zeros((rp, L), F32)
        dcum_out = jnp.zeros((rp, L), F32)
        for r in range(R):
            cc = jnp.broadcast_to(cum_c[:, r:r + 1], (L, L))
            cr = cum_r[r:r + 1, :]
            lam = jnp.exp(jnp.where(lower, cc - cum_r[r:r + 1, :], -jnp.inf))
            lam_t = jnp.exp(jnp.where(upper, cr - cc, -jnp.inf))
            m = gm * lam
            m_t = gm_t * lam_t
            dtr = dt[r:r + 1, :]
            xh = x_t[r * P:(r + 1) * P, :]
            x = xh * dtr
            xb = x.astype(BF16)
            d_y = dy_t[r * P:(r + 1) * P, :]
            d_yb = d_y.astype(BF16)
            s_r = st_ref[0, 0, r * P:(r + 1) * P, :]
            s_rb = s_r.astype(BF16)
            ds_n = dstate[r * P:(r + 1) * P, :]
            ds_nb = ds_n.astype(BF16)
            e = jnp.exp(cr)
            last = cr[:, L - 1:L]
            e_last = jnp.exp(last)
            w = jnp.exp(last - cr)
            d_x = _dot(d_yb, m.astype(BF16))
            d_m = _dot_tn(d_yb, xb)
            d_m_t = _dot_tn(xb, d_yb)
            d_ye = (d_y * e).astype(BF16)
            dc_acc = dc_acc + _dot_tn(d_ye, s_rb)
            ds_part = _dot(d_ye, cm)
            y_off = _dot_nt(s_rb, cm) * e
            dcum = jnp.sum(d_y * y_off, axis=0, keepdims=True)
            d_xw = _dot_nt(ds_nb, bm)
            d_x = d_x + d_xw * w
            dw_w = jnp.sum(d_xw * x, axis=0, keepdims=True) * w
            db_acc = db_acc + _dot_tn((x * w).astype(BF16), ds_nb)
            d_last = jnp.sum(ds_n * s_r, keepdims=True) * e_last + jnp.sum(dw_w, keepdims=True)
            dstate[r * P:(r + 1) * P, :] = e_last * ds_n + ds_part
            d_g = d_g + d_m * lam
            d_g_t = d_g_t + d_m_t * lam_t
            dcum = (dcum - dw_w + jnp.sum(d_m_t * m_t, axis=0, keepdims=True)
                    - jnp.sum(d_m * m, axis=0, keepdims=True) + jnp.where(is_last, d_last, 0.0))
            dx_t[r * P:(r + 1) * P, :] = d_x * dtr
            ddt = jnp.sum(d_x * xh, axis=0, keepdims=True)
            ddt_out = ddt_out + jnp.where(sub == r, ddt, 0.0)
            dcum_out = dcum_out + jnp.where(sub == r, dcum, 0.0)
        dc_ref[...] = dc_acc + _dot(d_g.astype(BF16), bm)
        db_ref[...] = db_acc + _dot(d_g_t.astype(BF16), cm)
        dx_ref[...] = dx_t[...].T
        ddt_ref[0] = ddt_out
        dcum_ref[0] = dcum_out

    gn = n_groups * N
    return pl.pallas_call(
        body, name="ssd_bwd", grid=(n_groups, NC),
        in_specs=_ssd_specs_t(d_inner, n_groups, gw, L, rp, rev) + [
            pl.BlockSpec((1, 1, gw, N), lambda g, c: (g, rev(c), 0, 0)),
            pl.BlockSpec((L, gw), lambda g, c: (rev(c), g))],
        out_specs=[pl.BlockSpec((L, gw), lambda g, c: (rev(c), g)),
                   pl.BlockSpec((L, N), lambda g, c: (rev(c), g)),
                   pl.BlockSpec((L, N), lambda g, c: (rev(c), g)),
                   pl.BlockSpec((1, rp, L), lambda g, c: (g, 0, rev(c))),
                   pl.BlockSpec((1, rp, L), lambda g, c: (g, 0, rev(c)))],
        out_shape=[jax.ShapeDtypeStruct((S, d_inner), F32), jax.ShapeDtypeStruct((S, gn), F32),
                   jax.ShapeDtypeStruct((S, gn), F32), jax.ShapeDtypeStruct((n_groups, rp, S), F32),
                   jax.ShapeDtypeStruct((n_groups, rp, S), F32)],
        scratch_shapes=[pltpu.VMEM((gw, N), F32), pltpu.VMEM((gw, L), F32)],
        compiler_params=_params(("parallel", "arbitrary")),
    )(xbc_a, xbc_a, xbc_a, dt_row, da_col, da_row, states, dy)


def _attn_scale():
    return float(QK_NOPE + QK_ROPE) ** -0.5


def _diag_mask(t):
    return lax.broadcasted_iota(jnp.int32, (t, t), 0) <= lax.broadcasted_iota(jnp.int32, (t, t), 1)


def _walk_wide(lo, hi, tile_step, joint=True, widest=4):
    n = hi - lo

    def step(j, width):
        if joint:
            tile_step(j, width)
        else:
            for u in range(width):
                tile_step(j + u, 1)

    def widest_run(t, carry):
        step(lo + widest * t, widest)
        return carry

    lax.fori_loop(0, n // widest, widest_run, 0)

    def shorter_run(width):
        @pl.when(n % (2 * width) >= width)
        def _():
            step(hi - n % (2 * width), width)

    width = widest // 2
    while width >= 1:
        shorter_run(width)
        width //= 2


def _carried(main_body, n_in, n_out, n_scratch, carry, grid):
    if carry is None:
        return main_body
    n_ci, n_co = len(carry["ins"]), len(carry["outs"])

    def body(*refs):
        pos = [0]

        def take(n):
            pos[0] += n
            return refs[pos[0] - n: pos[0]]

        ins, c_ins, outs, c_outs, scratch, sems = take(n_in), take(n_ci), take(n_out), take(n_co), take(n_scratch), take(3)
        steps = [pl.program_id(a) for a in range(len(grid))]

        @pl.when(functools.reduce(jnp.logical_and, [s == 0 for s in steps]))
        def _():
            for cp in carry["copies"](c_ins, c_outs, *sems):
                cp.start()

        main_body(*ins, *outs, *scratch)

        @pl.when(functools.reduce(jnp.logical_and, [s == n - 1 for s, n in zip(steps, grid)]))
        def _():
            for cp in carry["copies"](c_ins, c_outs, *sems):
                cp.wait()

    return body


def _carry_call(carry):
    if carry is None:
        return [], [], [], [], []
    sems = [pltpu.SemaphoreType.DMA((carry["n_remote"],)), pltpu.SemaphoreType.DMA((carry["n_remote"],)),
            pltpu.SemaphoreType.DMA((max(carry["n_local"], 1),))]
    return (list(carry["ins"]), [_HBM] * len(carry["ins"]), [_HBM] * len(carry["outs"]), list(carry["outs"]), sems)


def _flash_fwd(q, k, v_t, n_heads, carry=None):
    S = q.shape[0]
    T = min(FLASH_T, S)
    grid = (n_heads, S // T)
    c_args, c_in_specs, c_out_specs, c_out_shapes, c_sems = _carry_call(carry)

    def body(q_ref, k_ref, vt_ref, o_ref, lse_ref, m_s, l_s, acc_t):
        i = pl.program_id(1)
        m_s[...] = jnp.full_like(m_s, -jnp.inf)
        l_s[...] = jnp.zeros_like(l_s)
        acc_t[...] = jnp.zeros_like(acc_t)
        qv = q_ref[...]

        def step(j, width, masked):
            keys = pl.ds(pl.multiple_of(j * T, T), width * T)
            s_t = _dot_nt(k_ref[keys, :], qv)
            if masked:
                s_t = jnp.where(_diag_mask(T), s_t, -jnp.inf)
            m_prev = m_s[...]
            m_new = jnp.maximum(m_prev, jnp.max(s_t, axis=0, keepdims=True))
            alpha = jnp.exp(m_prev - m_new)
            p_t = jnp.exp(s_t - m_new)
            l_s[...] = alpha * l_s[...] + jnp.sum(p_t, axis=0, keepdims=True)
            acc_t[...] = alpha * acc_t[...] + _dot(vt_ref[:, keys], p_t.astype(BF16))
            m_s[...] = m_new

        _walk_wide(0, i, lambda j, width: step(j, width, False), widest=min(FLASH_FWD_RUN, S // T))
        step(i, 1, True)
        o_ref[...] = (acc_t[...] / l_s[...]).T
        lse_ref[0] = m_s[...] + jnp.log(l_s[...])

    res = pl.pallas_call(
        _carried(body, 3, 2, 3, carry, grid), name="flash_fwd", grid=grid,
        in_specs=[pl.BlockSpec((T, QK_PAD), lambda h, i: (i, h)),
                  pl.BlockSpec((S, QK_PAD), lambda h, i: (0, h)),
                  pl.BlockSpec((V_DIM, S), lambda h, i: (h, 0))] + c_in_specs,
        out_specs=[pl.BlockSpec((T, V_DIM), lambda h, i: (i, h)),
                   pl.BlockSpec((1, 1, T), lambda h, i: (h, 0, i))] + c_out_specs,
        out_shape=[jax.ShapeDtypeStruct((S, n_heads * V_DIM), F32),
                   jax.ShapeDtypeStruct((n_heads, 1, S), F32)] + c_out_shapes,
        scratch_shapes=[pltpu.VMEM((1, T), F32), pltpu.VMEM((1, T), F32), pltpu.VMEM((V_DIM, T), F32)] + c_sems,
        compiler_params=_params(("arbitrary", "arbitrary")),
    )(q, k, v_t, *c_args)
    return res[0], res[1], list(res[2:])


def _flash_bwd(q, k, v, do, lse_row, delta_row, n_heads, carry=None):
    S = q.shape[0]
    T = min(FLASH_T, S)
    nq = S // T
    grid = (n_heads, S // T)
    c_args, c_in_specs, c_out_specs, c_out_shapes, c_sems = _carry_call(carry)

    def body(q_ref, k_ref, v_ref, do_ref, lse_ref, dl_ref, dq_ref, dk_ref, dv_ref, dk_acc, dv_acc):
        j = pl.program_id(1)

        @pl.when(j == 0)
        def _():
            dq_ref[...] = jnp.zeros_like(dq_ref)

        dk_acc[...] = jnp.zeros_like(dk_acc)
        dv_acc[...] = jnp.zeros_like(dv_acc)
        kv = k_ref[...]
        vv = v_ref[...]

        def step(i, width, masked):
            cols = pl.ds(pl.multiple_of(i * T, T), width * T)
            qt = q_ref[cols, :]
            dot = do_ref[cols, :]
            s_t = _dot_nt(kv, qt)
            if masked:
                s_t = jnp.where(_diag_mask(T), s_t, -jnp.inf)
            p_t = jnp.exp(s_t - lse_ref[0, :, cols])
            dv_acc[...] += _dot(p_t.astype(BF16), dot)
            ds_t = (p_t * (_dot_nt(vv, dot) - dl_ref[0, :, cols])).astype(BF16)
            dk_acc[...] += _dot(ds_t, qt)
            dq_ref[cols, :] += _dot_tn(ds_t, kv)

        step(j, 1, True)
        _walk_wide(j + 1, nq, lambda i, width: step(i, width, False), joint=False, widest=min(FLASH_BWD_RUN, nq))
        dk_ref[...] = dk_acc[...]
        dv_ref[...] = dv_acc[...].astype(dv_ref.dtype)

        @pl.when(j == nq - 1)
        def _():
            dq_ref[...] = dq_ref[...] * _attn_scale()

    res = pl.pallas_call(
        _carried(body, 6, 3, 2, carry, grid), name="flash_bwd", grid=grid,
        in_specs=[pl.BlockSpec((S, QK_PAD), lambda h, j: (0, h)),
                  pl.BlockSpec((T, QK_PAD), lambda h, j: (j, h)),
                  pl.BlockSpec((T, V_DIM), lambda h, j: (j, h)),
                  pl.BlockSpec((S, V_DIM), lambda h, j: (0, h)),
                  pl.BlockSpec((1, 1, S), lambda h, j: (h, 0, 0)),
                  pl.BlockSpec((1, 1, S), lambda h, j: (h, 0, 0))] + c_in_specs,
        out_specs=[pl.BlockSpec((S, QK_PAD), lambda h, j: (0, h)),
                   pl.BlockSpec((T, QK_PAD), lambda h, j: (j, h)),
                   pl.BlockSpec((T, V_DIM), lambda h, j: (j, h))] + c_out_specs,
        out_shape=[jax.ShapeDtypeStruct((S, n_heads * QK_PAD), F32), jax.ShapeDtypeStruct((S, n_heads * QK_PAD), F32),
                   jax.ShapeDtypeStruct((S, n_heads * V_DIM), BF16)] + c_out_shapes,
        scratch_shapes=[pltpu.VMEM((T, QK_PAD), F32), pltpu.VMEM((T, V_DIM), F32)] + c_sems,
        compiler_params=_params(("arbitrary", "arbitrary")),
    )(q, k, v, do, lse_row, delta_row, *c_args)
    return res[0], res[1], res[2], list(res[3:])


def _st_delta(n_heads):
    def fn(do, o):
        prod = do * o
        lane = lax.broadcasted_iota(jnp.int32, (do.shape[0], LANES), 1)
        out = jnp.zeros((do.shape[0], LANES), F32)
        for h in range(n_heads):
            out = out + jnp.where(lane == h, jnp.sum(prod[:, h * V_DIM:(h + 1) * V_DIM], axis=1, keepdims=True), 0.0)
        return out
    return fn


def _pad_cols(w, width):
    return jnp.pad(w, ((0, 0), (0, width - w.shape[1])))


def _dims(x, p, q_norm, kv_norm, w_uq, dt_bias, ssm_norm, conv_b):
    d = dict(S=x.shape[0], D=x.shape[1], PLE=p.shape[1], DQ=q_norm.shape[1], DKV=kv_norm.shape[1],
             NH=w_uq.shape[1] // (QK_NOPE + QK_ROPE), NHS=dt_bias.shape[1], DI=ssm_norm.shape[1],
             CONV=conv_b.shape[1])
    d["G"] = (d["CONV"] - d["DI"]) // (2 * D_STATE)
    d["R"] = d["NHS"] // d["G"]
    return d


def _assemble(name, blocks):
    rows, cols = blocks.shape[1:]
    if name in COL_SHARDED:
        return blocks.transpose(1, 0, 2).reshape(rows, N_DEV * cols)
    return blocks.reshape(N_DEV * rows, cols)


def _by_device(name, g):
    if name in COL_SHARDED:
        return g.reshape(g.shape[0], N_DEV, g.shape[1] // N_DEV).transpose(1, 0, 2)
    return g.reshape(N_DEV, g.shape[0] // N_DEV, g.shape[1])


def _local_step(x, p, positions, target, small, conv_w, wts, late_names=(), late_shards=(), exchange=True):
    wts = dict(wts)
    dm = _dims(x, p, small["q_norm"], small["kv_norm"], wts["w_uq"], small["dt_bias"], small["ssm_norm"],
               small["conv_b"])
    S, D, DQ, DKV, NH, NHS, DI, CONV, G, R = (dm[k] for k in ("S", "D", "DQ", "DKV", "NH", "NHS", "DI", "CONV", "G", "R"))
    rp = -(-R // SUBLANES) * SUBLANES
    L = min(CHUNK, S)

    w_nat = wts["w_in"]
    o = [0]
    for n in (DQ, DKV, QK_ROPE, DI, CONV, NHS, D, D):
        o.append(o[-1] + n)
    pw = (DI, 2 * D, CONV, DQ + DKV, QK_ROPE + NHS)
    po = [sum(pw[:k]) for k in range(len(pw))]
    w_in = jnp.concatenate([w_nat[:, o[3]:o[4]], w_nat[:, o[6]:o[8]], w_nat[:, o[4]:o[5]], w_nat[:, o[0]:o[2]],
                            w_nat[:, o[2]:o[3]], w_nat[:, o[5]:o[6]]], axis=1)
    w_z, w_g, w_xbc, w_cqkv = (w_in[:, po[k]:po[k] + pw[k]] for k in range(4))
    w_kr, w_dt = w_in[:, po[4]:po[4] + QK_ROPE], w_in[:, po[4] + QK_ROPE:]
    zc = lambda n: jnp.zeros((D, n), BF16)
    w_sm = jnp.concatenate([zc(QK_NOPE), w_kr, zc(QK_PAD - QK_NOPE - QK_ROPE), w_dt, zc(LANES - NHS)], axis=1)
    w_q = jnp.pad(wts["w_uq"].reshape(DQ, NH, QK_NOPE + QK_ROPE),
                  ((0, 0), (0, 0), (0, QK_PAD - QK_NOPE - QK_ROPE))).reshape(DQ, NH * QK_PAD)
    ukv = wts["w_ukv"].reshape(DKV, NH, QK_NOPE + V_DIM)
    w_k = jnp.pad(ukv[:, :, :QK_NOPE], ((0, 0), (0, 0), (0, QK_PAD - QK_NOPE))).reshape(DKV, NH * QK_PAD)
    w_v = ukv[:, :, QK_NOPE:].reshape(DKV, NH * V_DIM)
    dt_bias_p, a_log_p = _pad_cols(small["dt_bias"], LANES), _pad_cols(small["a_log"], LANES)
    dskip_rep = jnp.repeat(small["d_skip"], SSM_HEADDIM, axis=1)
    c_tab, a_tab, b_tab = _rope_tables(positions)

    (u,) = _rowwise("pre_norm", _st_pre, [x], [small["mix_norm_pre"]], [(D, BF16)])
    cqkv = _mm("in_cqkv", u, w_cqkv, "nn")
    z = _mm("in_z", u, w_z, "nn")
    xbc = _mm("in_xbc", u, w_xbc, "nn")
    g = _mm("in_gates", u, w_g, "nn")
    sm = _mm("in_small", u, w_sm, "nn")

    lora_fn = _st_lora_norms(DQ)
    cq_n, ckv_n = _rowwise("lora_norms", lora_fn, [cqkv], [small["q_norm"], small["kv_norm"]], [(DQ, BF16), (DKV, BF16)])
    qraw = _mm("up_q", cq_n, w_q, "nn")
    kraw = _mm("up_k", ckv_n, w_k, "nn")
    v = _mm("up_v", ckv_n, w_v, "nn", out_dtype=BF16)
    v_t = _mm("up_v_t", w_v.T, ckv_n, "nt", out_dtype=BF16)
    q, k = _rowwise("rope", _st_rope(NH), [qraw, kraw, sm, c_tab, a_tab, b_tab], [],
                    [(NH * QK_PAD, BF16), (NH * QK_PAD, BF16)])
    attn, lse, late_blocks = _flash_fwd(q, k, v_t, NH, carry=_gather_carry(list(late_shards)) if late_names else None)
    wts.update({n: _assemble(n, b) for n, b in zip(late_names, late_blocks)})

    xbc_c, xbc_a = _conv_fwd(xbc, conv_w, small["conv_b"])
    dt, da = _rowwise("dt", _st_dt, [sm], [dt_bias_p, a_log_p], [(LANES, F32), (LANES, F32)])

    def col_layout(t):
        return _pad_cols(t[:, :NHS].reshape(S, G, R).transpose(1, 0, 2).reshape(G * S, R), LANES).reshape(G, S, LANES)

    def row_layout(t):
        return jnp.pad(t[:, :NHS].reshape(S, G, R).transpose(1, 2, 0), ((0, 0), (0, rp - R), (0, 0)))

    dt_row, da_col, da_row = row_layout(dt), col_layout(da), row_layout(da)
    xs = xbc_a[:, :DI]
    y, states = _ssd_fwd_t(xbc_a, dt_row, da_col, da_row, DI, G, R)
    gn_fn = _st_gated_norm(G)
    (ssm,) = _rowwise("gated_norm", gn_fn, [y, xs, z], [dskip_rep, small["ssm_norm"]], [(DI, BF16)])

    ao = _mm("attn_o", attn, wts["w_attn_o"], "nn")
    so = _mm("ssm_o", ssm, wts["w_ssm_o"], "nn")
    mix_fn = _st_mix(D)
    (mixed,) = _rowwise("mix", mix_fn, [g, ao, so], [], [(D, BF16)])
    mo = _mm("out_proj", mixed, wts["w_out"], "nn")
    h1, f = _rowwise("res1", _st_res_norm, [x, mo], [small["mix_norm_post"], small["ffn_norm_pre"]], [(D, F32), (D, BF16)])
    gt = _mm("ffn_gate", f, wts["w_gate"], "nn")
    up = _mm("ffn_up", f, wts["w_up"], "nn")
    (act,) = _rowwise("swiglu", _st_swiglu, [gt, up], [], [(gt.shape[1], BF16)])
    dn = _mm("ffn_down", act, wts["w_down"], "nn")
    h2, a3 = _rowwise("res2", _st_res_norm, [h1, dn], [small["ffn_norm_post"], small["ple_norm_pre"]], [(D, F32), (D, BF16)])
    gl = _mm("ple_gate", a3, wts["w_ple_gate"], "nn")
    pe = _mm("ple_proj", p, wts["w_ple"], "nn")

    sg = {}
    bg = {}
    dpe, dgl, dh2, d_w, loss_acc = _rowwise_bwd(
        "loss", _st_loss, [pe, gl, h2, target], [small["ple_norm_post"]], [1.0], [0, 1, 2], [BF16, BF16, F32], fwd_sums=(0,))
    sg["ple_norm_post"] = _fold(d_w)
    loss = loss_acc[0:1, :]
    bg["w_ple"] = _mm("d_w_ple", p, dpe, "tn", out_dtype=BF16)
    bg["w_ple_gate"] = _mm("d_w_ple_gate", a3, dgl, "tn", out_dtype=BF16)
    da3 = _mm("d_a3", dgl, wts["w_ple_gate"], "nt")

    dh1, ddn, d_post, d_pre = _rowwise_bwd(
        "res2_bwd", _st_res_norm, [h1, dn], [small["ffn_norm_post"], small["ple_norm_pre"]], [dh2, da3], [0, 1], [F32, BF16])
    sg["ffn_norm_post"], sg["ple_norm_pre"] = _fold(d_post), _fold(d_pre)
    bg["w_down"] = _mm("d_w_down", act, ddn, "tn", out_dtype=BF16)
    dact = _mm("d_act", ddn, wts["w_down"], "nt")
    dgt, dup = _rowwise_bwd("swiglu_bwd", _st_swiglu, [gt, up], [], [dact], [0, 1], [BF16, BF16])
    bg["w_gate"] = _mm("d_w_gate", f, dgt, "tn", out_dtype=BF16)
    bg["w_up"] = _mm("d_w_up", f, dup, "tn", out_dtype=BF16)
    df = _mm("d_f_gate", dgt, wts["w_gate"], "nt")
    df = _mm("d_f_up", dup, wts["w_up"], "nt", acc_in=df)

    dx_res, dmo, d_post, d_pre = _rowwise_bwd(
        "res1_bwd", _st_res_norm, [x, mo], [small["mix_norm_post"], small["ffn_norm_pre"]], [dh1, df], [0, 1], [F32, BF16])
    sg["mix_norm_post"], sg["ffn_norm_pre"] = _fold(d_post), _fold(d_pre)
    bg["w_out"] = _mm("d_w_out", mixed, dmo, "tn", out_dtype=BF16)
    dmixed = _mm("d_mixed", dmo, wts["w_out"], "nt")
    assert po[1] % pw[1] == 0 and po[3] % pw[3] == 0, (po, pw)
    dproj = lax.empty((S, sum(pw)), BF16)
    dproj, dao, dso = _rowwise_bwd("mix_bwd", mix_fn, [g, ao, so], [], [dmixed], [0, 1, 2], [BF16, BF16, BF16],
                                   into={0: (dproj, po[1] // pw[1])})
    bg["w_attn_o"] = _mm("d_w_attn_o", attn, dao, "tn", out_dtype=BF16)
    bg["w_ssm_o"] = _mm("d_w_ssm_o", ssm, dso, "tn", out_dtype=BF16)
    dattn = _mm("d_attn", dao, wts["w_attn_o"], "nt", out_dtype=BF16)
    dssm = _mm("d_ssm", dso, wts["w_ssm_o"], "nt")

    dy, dxs_a, dproj, d_dskip, d_ssmn = _rowwise_bwd(
        "gated_norm_bwd", gn_fn, [y, xs, z], [dskip_rep, small["ssm_norm"]], [dssm], [0, 1, 2], [F32, F32, BF16],
        into={2: (dproj, 0)})
    sg["d_skip"] = _fold(d_dskip).reshape(NHS, SSM_HEADDIM).sum(axis=1).reshape(1, NHS)
    sg["ssm_norm"] = _fold(d_ssmn)
    dxs_b, d_b, d_c, ddt_row, dcum_row = _ssd_bwd_t(xbc_a, dt_row, da_col, da_row, states, dy, DI, G, R)

    def from_row(t):
        return _pad_cols(t[:, :R, :].transpose(2, 0, 1).reshape(S, NHS), LANES)

    ddtraw, d_bias, d_alog = _rowwise("dt_bwd", _st_dt_bwd, [sm, from_row(ddt_row), from_row(dcum_row)],
                                      [dt_bias_p, a_log_p], [(LANES, F32)], accs=(LANES, LANES), tr=L)
    sg["dt_bias"], sg["a_log"] = _fold(d_bias)[:, :NHS], _fold(d_alog)[:, :NHS]
    dconv, d_cb = _rowwise("dconv", _st_dconv(DI), [xbc_c, dxs_a, dxs_b, d_b, d_c], [], [(CONV, F32)], accs=(CONV,))
    sg["conv_b"] = _fold(d_cb)
    dproj, d_cw = _conv_bwd(xbc, dconv, conv_w, dproj, po[2])
    d_conv_w = d_cw.reshape(CONV_WIDTH, SUBLANES, CONV).sum(axis=1)

    (delta,) = _rowwise("attn_delta", _st_delta(NH), [dattn, attn], [], [(LANES, F32)])
    delta = delta[:, :NH].T
    late_sent = [_by_device(n, bg.pop(n)) for n in late_names]
    dq, dk, dv, late_recv = _flash_bwd(q, k, v, dattn, lse, delta.reshape(NH, 1, S), NH,
                                       carry=_scatter_carry(late_sent) if late_names else None)
    dqraw, dkr, dk = _rowwise("rope_bwd", _st_rope_bwd(NH), [dq, dk, c_tab, a_tab, b_tab], [],
                              [(NH * QK_PAD, BF16), (QK_PAD, F32), (NH * QK_PAD, BF16)])
    d_w_q = _mm("d_w_q", cq_n, dqraw, "tn", out_dtype=BF16)
    d_w_k = _mm("d_w_k", ckv_n, dk, "tn", out_dtype=BF16)
    d_w_v = _mm("d_w_v", ckv_n, dv, "tn", out_dtype=BF16)
    dcq_n = _mm("d_cq_n", dqraw, w_q, "nt")
    dckv_n = _mm("d_ckv_n_k", dk, w_k, "nt")
    dckv_n = _mm("d_ckv_n_v", dv, w_v, "nt", acc_in=dckv_n)
    bg["w_uq"] = d_w_q.reshape(DQ, NH, QK_PAD)[:, :, :QK_NOPE + QK_ROPE].reshape(DQ, NH * (QK_NOPE + QK_ROPE))
    bg["w_ukv"] = jnp.concatenate([d_w_k.reshape(DKV, NH, QK_PAD)[:, :, :QK_NOPE], d_w_v.reshape(DKV, NH, V_DIM)],
                                  axis=2).reshape(DKV, NH * (QK_NOPE + V_DIM))
    dproj, d_qn, d_kvn = _rowwise_bwd("lora_norms_bwd", lora_fn, [cqkv], [small["q_norm"], small["kv_norm"]],
                                      [dcq_n, dckv_n], [0], [BF16], into={0: (dproj, po[3] // pw[3])})
    sg["q_norm"], sg["kv_norm"] = _fold(d_qn), _fold(d_kvn)

    d_small = jnp.concatenate([dkr[:, QK_NOPE:QK_NOPE + QK_ROPE], ddtraw[:, :NHS]], axis=1).astype(BF16)
    dproj = lax.dynamic_update_slice(dproj, d_small, (0, po[4]))
    d_w = _mm("d_w_in", u, dproj, "tn", out_dtype=BF16)
    kr = po[4] + QK_ROPE
    bg["w_in"] = jnp.concatenate([d_w[:, po[3]:po[4]], d_w[:, po[4]:kr], d_w[:, po[0]:po[1]], d_w[:, po[2]:po[3]],
                                  d_w[:, kr:kr + NHS], d_w[:, po[1]:po[2]]], axis=1)
    rest_names = tuple(n for n in BIG if n in bg)
    if not exchange:
        du = _mm("d_u", dproj, w_in, "nt")
        grad_x, d_pre = _rowwise_bwd("pre_norm_bwd", _st_pre, [x], [small["mix_norm_pre"]], [du], [0], [F32],
                                     adds={0: dx_res})
        sg["mix_norm_pre"] = _fold(d_pre)
        return loss, grad_x, sg, d_conv_w, bg, {}
    rest_sent = [_by_device(n, bg.pop(n)) for n in rest_names]
    du, rest_recv = _mm("d_u", dproj, w_in, "nt", carry=_scatter_carry(rest_sent))
    grad_x, d_pre = _rowwise_bwd("pre_norm_bwd", _st_pre, [x], [small["mix_norm_pre"]], [du], [0], [F32], adds={0: dx_res})
    sg["mix_norm_pre"] = _fold(d_pre)
    sent = dict(zip(late_names, late_sent), **dict(zip(rest_names, rest_sent)))
    recv = dict(zip(late_names, late_recv), **dict(zip(rest_names, rest_recv)))
    return loss, grad_x, sg, d_conv_w, sent, recv


_HBM = pl.BlockSpec(memory_space=pltpu.HBM)
_FLIPS = ((0, 0, 1), (1, 0, 0), (0, 1, 0), (1, 1, 0), (1, 0, 1), (0, 1, 1), (1, 1, 1))


def _place():
    return lax.axis_index("x"), lax.axis_index("y"), lax.axis_index("c")


def _flipped(place, flip):
    return tuple(1 - v if f else v for v, f in zip(place, flip))


def _gather_carry(blocks):
    nw = len(blocks)

    def copies(ins, outs, send_sems, recv_sems, local_sems):
        x, y, c = _place()
        me = 4 * x + 2 * y + c
        cps = []
        for w in range(nw):
            cps.append(pltpu.make_async_copy(ins[w], outs[w].at[me], local_sems.at[w]))
            for k, flip in enumerate(_FLIPS):
                cps.append(pltpu.make_async_remote_copy(
                    src_ref=ins[w], dst_ref=outs[w].at[me], send_sem=send_sems.at[7 * w + k],
                    recv_sem=recv_sems.at[7 * w + k], device_id=_flipped((x, y, c), flip), device_id_type=MESH))
        return cps

    return dict(ins=blocks, outs=[jax.ShapeDtypeStruct((N_DEV,) + b.shape, b.dtype) for b in blocks],
                n_remote=7 * nw, n_local=nw, copies=copies)


def _scatter_carry(by_dev):
    nw = len(by_dev)

    def copies(ins, outs, send_sems, recv_sems, local_sems):
        x, y, c = _place()
        cps = []
        for w in range(nw):
            for k, flip in enumerate(_FLIPS):
                px, py, pc = _flipped((x, y, c), flip)
                cps.append(pltpu.make_async_remote_copy(
                    src_ref=ins[w].at[4 * px + 2 * py + pc], dst_ref=outs[w].at[k], send_sem=send_sems.at[7 * w + k],
                    recv_sem=recv_sems.at[7 * w + k], device_id=(px, py, pc), device_id_type=MESH))
        return cps

    return dict(ins=by_dev, outs=[jax.ShapeDtypeStruct((7,) + b.shape[1:], b.dtype) for b in by_dev],
                n_remote=7 * nw, n_local=0, copies=copies)


def _all_gather(name, blocks):
    nw = len(blocks)

    def body(*refs):
        x_refs, out_refs = refs[:nw], refs[nw:2 * nw]
        send_sems, recv_sems, local_sems = refs[2 * nw:]
        x, y, c = _place()
        me, sibling = (x, y, c), (x, y, 1 - c)
        chips = [(1 - x, y), (x, 1 - y), (1 - x, 1 - y)]

        def slot(w, px, py, pc):
            return out_refs[w].at[4 * px + 2 * py + pc]

        def copy(w, k, blk, to, src=None):
            return pltpu.make_async_remote_copy(
                src_ref=slot(w, *blk) if src is None else src, dst_ref=slot(w, *blk),
                send_sem=send_sems.at[7 * w + k], recv_sem=recv_sems.at[7 * w + k], device_id=to, device_id_type=MESH)

        mine = [pltpu.make_async_copy(x_refs[w], slot(w, *me), local_sems.at[w]) for w in range(nw)]
        for cp in mine:
            cp.start()
        first = []
        for w in range(nw):
            first.append(copy(w, 0, me, sibling, src=x_refs[w]))
            first += [copy(w, 1 + j, me, (*chip, c), src=x_refs[w]) for j, chip in enumerate(chips)]
        for cp in first:
            cp.start()
        passed = []
        for j, chip in enumerate(chips):
            for w in range(nw):
                copy(w, 1 + j, (*chip, c), me).wait_recv()
                passed.append(copy(w, 4 + j, (*chip, c), sibling))
                passed[-1].start()
        for w in range(nw):
            copy(w, 0, sibling, me).wait_recv()
        for j, chip in enumerate(chips):
            for w in range(nw):
                copy(w, 4 + j, (*chip, 1 - c), me).wait_recv()
        for cp in first + passed:
            cp.wait_send()
        for cp in mine:
            cp.wait()

    return pl.pallas_call(
        body, name=name, out_shape=[jax.ShapeDtypeStruct((N_DEV,) + b.shape, b.dtype) for b in blocks],
        in_specs=[_HBM] * nw, out_specs=[_HBM] * nw,
        scratch_shapes=[pltpu.SemaphoreType.DMA((7 * nw,)), pltpu.SemaphoreType.DMA((7 * nw,)),
                        pltpu.SemaphoreType.DMA((nw,))],
    )(*blocks)


def _lane_pad(n):
    return -(-n // LANES) * LANES


def _pack_small(vecs, mat):
    width = max(sum(_lane_pad(v.shape[1]) for v in vecs), _lane_pad(mat.shape[1]))
    row0 = jnp.concatenate([_pad_cols(v, _lane_pad(v.shape[1])) for v in vecs], axis=1)
    rows = jnp.concatenate([_pad_cols(row0, width), _pad_cols(mat, width)], axis=0)
    return jnp.pad(rows, ((0, SUBLANES - rows.shape[0]), (0, 0)))


def _unpack_small(packed, sizes, mat_cols):
    vecs, off = [], 0
    for n in sizes:
        vecs.append(packed[0:1, off:off + n])
        off += _lane_pad(n)
    return vecs, packed[1:1 + CONV_WIDTH, :mat_cols]


def _adamw(w, g, m, v):
    m = ADAM_B1 * m + (1.0 - ADAM_B1) * g
    v = ADAM_B2 * v + (1.0 - ADAM_B2) * (g * g)
    m_hat = m / (1.0 - ADAM_B1 ** ADAM_STEP)
    v_hat = v / (1.0 - ADAM_B2 ** ADAM_STEP)
    delta = -ADAM_LR * (m_hat / (jnp.sqrt(v_hat) + ADAM_EPS) + ADAM_WD * w)
    return delta, m, v


BIG = ("w_in", "w_uq", "w_ukv", "w_attn_o", "w_ssm_o", "w_out", "w_gate", "w_up", "w_down", "w_ple_gate", "w_ple")
FIRST = ("w_in", "w_uq", "w_ukv")
LATE = ("w_attn_o", "w_ssm_o", "w_out", "w_gate", "w_up", "w_down", "w_ple_gate", "w_ple")
COL_SHARDED = ("w_in", "w_uq", "w_ukv", "w_gate", "w_up", "w_ple")
SMALL = ("mix_norm_pre", "mix_norm_post", "q_norm", "kv_norm", "conv_b", "dt_bias", "a_log", "d_skip", "ssm_norm",
         "ffn_norm_pre", "ffn_norm_post", "ple_norm_pre", "ple_norm_post")
WEIGHTS = ("mix_norm_pre", "mix_norm_post", "w_in", "q_norm", "w_uq", "kv_norm", "w_ukv", "conv_w", "conv_b", "dt_bias",
           "a_log", "d_skip", "ssm_norm", "w_attn_o", "w_ssm_o", "w_out", "ffn_norm_pre", "ffn_norm_post", "w_gate",
           "w_up", "w_down", "ple_norm_pre", "ple_norm_post", "w_ple_gate", "w_ple")


def _step(x, p, positions, target, w, m, v):
    xi, yi, ci = _place()
    me = 4 * xi + 2 * yi + ci

    gathered = _all_gather("gather_weights", [w[n].astype(BF16) for n in FIRST])
    wts = {n: _assemble(n, blocks) for n, blocks in zip(FIRST, gathered)}
    cw_rows, cw_cols = w["conv_w"].shape
    (cw_all,) = _all_gather("gather_conv_w", [jnp.pad(w["conv_w"], ((0, SUBLANES - cw_rows), (0, 0)))])
    conv_w = cw_all[:, :cw_rows, :].transpose(1, 0, 2).reshape(cw_rows, N_DEV * cw_cols)

    small = {n: w[n] for n in SMALL}
    loss, grad_x, sg, d_conv_w, sent, recv = _local_step(
        x, p, positions, target, small, conv_w, wts, LATE, [w[n].astype(BF16) for n in LATE])

    sizes = [w[n].shape[1] for n in SMALL]
    sg_pack = _pack_small([sg[n] for n in SMALL] + [loss], d_conv_w)
    (sg_all,) = _all_gather("gather_small_grads", [sg_pack])
    (sg_sum,) = _rowwise("sum_small_grads", lambda *a: functools.reduce(lambda s, t: s + t, a),
                         [sg_all[k] for k in range(N_DEV)], [], [(sg_pack.shape[1], F32)])
    sg_vecs, d_conv_w_sum = _unpack_small(sg_sum, sizes + [LANES], d_conv_w.shape[1])
    loss = sg_vecs[-1][0, 0]
    grads = dict(zip(SMALL, sg_vecs[:-1]))
    grads["conv_w"] = lax.dynamic_slice_in_dim(d_conv_w_sum, me * cw_cols, cw_cols, axis=1)

    def sum8_then_adamw(wv, mv, vv, own, *others):
        g = functools.reduce(lambda s, t: s + t, others, own)
        return (g,) + _adamw(wv, g, mv, vv)

    delta, new_m, new_v = {}, {}, {}
    for n in BIG:
        cols = w[n].shape[1]
        own = lax.dynamic_index_in_dim(sent[n], me, axis=0, keepdims=False)
        grads[n], delta[n], new_m[n], new_v[n] = _rowwise(
            "adamw_" + n, sum8_then_adamw, [w[n], m[n], v[n], own] + [recv[n][k] for k in range(N_DEV - 1)], [],
            [(cols, F32)] * 4)
    packed = [_pack_small([d[n] for n in SMALL], d["conv_w"]) for d in (w, grads, m, v)]
    outs = _rowwise("adamw_small", _adamw, packed, [], [(packed[0].shape[1], F32)] * 3)
    for d, o in zip((delta, new_m, new_v), outs):
        vecs, mat = _unpack_small(o, sizes, cw_cols)
        d.update(zip(SMALL, vecs))
        d["conv_w"] = mat
    return loss, grad_x, grads, delta, new_m, new_v


def kernel(x, p, positions, mix_norm_pre, mix_norm_post, w_in, q_norm, w_uq, kv_norm, w_ukv, conv_w, conv_b, dt_bias, a_log, d_skip, ssm_norm, w_attn_o, w_ssm_o, w_out, ffn_norm_pre, ffn_norm_post, w_gate, w_up, w_down, ple_norm_pre, ple_norm_post, w_ple_gate, w_ple, loss_target, m_mix_norm_pre, m_mix_norm_post, m_w_in, m_q_norm, m_w_uq, m_kv_norm, m_w_ukv, m_conv_w, m_conv_b, m_dt_bias, m_a_log, m_d_skip, m_ssm_norm, m_w_attn_o, m_w_ssm_o, m_w_out, m_ffn_norm_pre, m_ffn_norm_post, m_w_gate, m_w_up, m_w_down, m_ple_norm_pre, m_ple_norm_post, m_w_ple_gate, m_w_ple, v_mix_norm_pre, v_mix_norm_post, v_w_in, v_q_norm, v_w_uq, v_kv_norm, v_w_ukv, v_conv_w, v_conv_b, v_dt_bias, v_a_log, v_d_skip, v_ssm_norm, v_w_attn_o, v_w_ssm_o, v_w_out, v_ffn_norm_pre, v_ffn_norm_post, v_w_gate, v_w_up, v_w_down, v_ple_norm_pre, v_ple_norm_post, v_w_ple_gate, v_w_ple):
    w_args = (mix_norm_pre, mix_norm_post, w_in, q_norm, w_uq, kv_norm, w_ukv, conv_w, conv_b, dt_bias, a_log, d_skip, ssm_norm, w_attn_o, w_ssm_o, w_out, ffn_norm_pre, ffn_norm_post, w_gate, w_up, w_down, ple_norm_pre, ple_norm_post, w_ple_gate, w_ple)
    m_args = (m_mix_norm_pre, m_mix_norm_post, m_w_in, m_q_norm, m_w_uq, m_kv_norm, m_w_ukv, m_conv_w, m_conv_b, m_dt_bias, m_a_log, m_d_skip, m_ssm_norm, m_w_attn_o, m_w_ssm_o, m_w_out, m_ffn_norm_pre, m_ffn_norm_post, m_w_gate, m_w_up, m_w_down, m_ple_norm_pre, m_ple_norm_post, m_w_ple_gate, m_w_ple)
    v_args = (v_mix_norm_pre, v_mix_norm_post, v_w_in, v_q_norm, v_w_uq, v_kv_norm, v_w_ukv, v_conv_w, v_conv_b, v_dt_bias, v_a_log, v_d_skip, v_ssm_norm, v_w_attn_o, v_w_ssm_o, v_w_out, v_ffn_norm_pre, v_ffn_norm_post, v_w_gate, v_w_up, v_w_down, v_ple_norm_pre, v_ple_norm_post, v_w_ple_gate, v_w_ple)

    def drop_layer(a):
        return a if a.ndim == 2 else a[0]

    w = {n: drop_layer(a) for n, a in zip(WEIGHTS, w_args)}
    m = {n: drop_layer(a) for n, a in zip(WEIGHTS, m_args)}
    v = {n: drop_layer(a) for n, a in zip(WEIGHTS, v_args)}
    loss, grad_x, grads, delta, new_m, new_v = _step(x[0], p[0, 0], positions[0], loss_target[0], w, m, v)
    like = lambda d: [d[n].reshape(a.shape) for n, a in zip(WEIGHTS, w_args)]
    return (loss, grad_x[None], *like(grads), *like(delta), *like(new_m), *like(new_v))
```

```python
import functools

import jax
import jax.numpy as jnp
from jax import lax
from jax.experimental import pallas as pl
from jax.experimental.pallas import tpu as pltpu

F32 = jnp.float32
BF16 = jnp.bfloat16

EPS = 1e-6
QK_NOPE = 128
QK_ROPE = 64
V_DIM = 128
QK_PAD = 256
ROPE_THETA = 10000.0
SSM_HEADDIM = 64
D_STATE = 128
CONV_WIDTH = 4
CHUNK = 256
ADAM_LR = 0.001
ADAM_B1 = 0.9
ADAM_B2 = 0.999
ADAM_EPS = 1e-08
ADAM_WD = 0.01
ADAM_STEP = 10

N_DEV = 8
LANES = 128
SUBLANES = 8
VMEM_LIMIT = 56 * 1024 * 1024
ROW_TILE_BYTES = 16 * 1024 * 1024
STRIP_ROWS = 16
STRIP_ELEMS = 128 * 1024
FLASH_T = 512
FLASH_FWD_RUN = 8
FLASH_BWD_RUN = 8
MM_TILE_BYTES = 20 * 1024 * 1024
MESH = pl.DeviceIdType.MESH


def _pick(dim, prefs):
    if dim <= prefs[0]:
        return dim
    for p in prefs:
        if dim % p == 0:
            return p
    return dim


def _tile(dim, cap):
    if dim <= cap:
        return dim
    best = None
    for t in range(LANES, cap + 1, LANES):
        if dim % t == 0:
            best = t
    return best if best is not None else dim


def _params(sem):
    return pltpu.CompilerParams(dimension_semantics=sem, vmem_limit_bytes=VMEM_LIMIT)


def _dot(a, b):
    return lax.dot_general(a, b, (((1,), (0,)), ((), ())), preferred_element_type=F32)


def _dot_nt(a, b):
    return lax.dot_general(a, b, (((1,), (1,)), ((), ())), preferred_element_type=F32)


def _dot_tn(a, b):
    return lax.dot_general(a, b, (((0,), (0,)), ((), ())), preferred_element_type=F32)


def _mm(name, a, b, mode, out_dtype=F32, acc_in=None, carry=None):
    if mode == "nn":
        (M, K), (K2, N) = a.shape, b.shape
    elif mode == "nt":
        (M, K), (N, K2) = a.shape, b.shape
    else:
        (K, M), (K2, N) = a.shape, b.shape
    assert K == K2, (name, a.shape, b.shape, mode)
    tm = _tile(M, 1024)
    tn = _tile(N, 1024 if acc_in is not None else 1536)
    tk = _tile(K, 2048)
    while tk > 512 and 2 * (tm * tk * a.dtype.itemsize + tk * tn * b.dtype.itemsize) > MM_TILE_BYTES:
        tk = _tile(K, tk - LANES)
    nk = K // tk
    dot = {"nn": _dot, "nt": _dot_nt, "tn": _dot_tn}[mode]
    has_acc = acc_in is not None

    def body(*refs):
        if has_acc:
            a_ref, b_ref, c_ref, o_ref, acc = refs
        else:
            a_ref, b_ref, o_ref, acc = refs
        k = pl.program_id(2)

        def product():
            return dot(a_ref[...].astype(BF16), b_ref[...].astype(BF16))

        def finish(r):
            if has_acc:
                r = r + c_ref[...]
            o_ref[...] = r.astype(o_ref.dtype)

        if nk == 1:
            finish(product())
            return

        @pl.when(k == 0)
        def _():
            acc[...] = product()

        @pl.when(jnp.logical_and(k > 0, k < nk - 1))
        def _():
            acc[...] += product()

        @pl.when(k == nk - 1)
        def _():
            finish(acc[...] + product())

    if mode == "tn":
        a_spec = pl.BlockSpec((tk, tm), lambda i, j, k: (k, i))
    else:
        a_spec = pl.BlockSpec((tm, tk), lambda i, j, k: (i, k))
    if mode == "nt":
        b_spec = pl.BlockSpec((tn, tk), lambda i, j, k: (j, k))
    else:
        b_spec = pl.BlockSpec((tk, tn), lambda i, j, k: (k, j))
    o_spec = pl.BlockSpec((tm, tn), lambda i, j, k: (i, j))
    in_specs = [a_spec, b_spec] + ([o_spec] if has_acc else [])
    args = (a, b) + ((acc_in,) if has_acc else ())
    if carry is None:
        return pl.pallas_call(
            body, name=name, grid=(M // tm, N // tn, nk), in_specs=in_specs, out_specs=o_spec,
            out_shape=jax.ShapeDtypeStruct((M, N), out_dtype), scratch_shapes=[pltpu.VMEM((tm, tn), F32)],
            input_output_aliases=({2: 0} if has_acc and out_dtype == F32 else {}),
            compiler_params=_params(("parallel", "parallel", "arbitrary")),
        )(*args)
    grid = (M // tm, N // tn, nk)
    c_args, c_in_specs, c_out_specs, c_out_shapes, c_sems = _carry_call(carry)
    res = pl.pallas_call(
        _carried(body, len(args), 1, 1, carry, grid), name=name, grid=grid, in_specs=in_specs + c_in_specs,
        out_specs=[o_spec] + c_out_specs, out_shape=[jax.ShapeDtypeStruct((M, N), out_dtype)] + c_out_shapes,
        scratch_shapes=[pltpu.VMEM((tm, tn), F32)] + c_sems,
        compiler_params=_params(("arbitrary", "arbitrary", "arbitrary")),
    )(*args, *c_args)
    return res[0], list(res[1:])


def _row_tile(n_rows, bytes_per_row):
    tr = 1024
    while tr > SUBLANES and tr * bytes_per_row > ROW_TILE_BYTES:
        tr //= 2
    while n_rows % tr:
        tr //= 2
    return tr


def _strip_rows(width):
    return max(STRIP_ROWS, min(4 * STRIP_ROWS, STRIP_ELEMS // width // STRIP_ROWS * STRIP_ROWS))


def _over_strips(tr, strip, work):
    if strip is None or tr <= strip or tr % strip:
        work(slice(None))
        return

    def one(s, carry):
        work(pl.ds(pl.multiple_of(s * strip, strip), strip))
        return carry

    lax.fori_loop(0, tr // strip, one, 0)


def _acc_add(a_ref, v):
    if v.shape[0] == 1:
        a_ref[0:1, :] += v
    else:
        a_ref[...] += v.reshape(v.shape[0] // SUBLANES, SUBLANES, v.shape[1]).sum(axis=0)


def _rowwise(name, fn, rows, bcs, outs, accs=(), tr=None):
    n_rows = rows[0].shape[0]
    strip = _strip_rows(max(r.shape[1] for r in rows)) if tr is None else None
    if tr is None:
        per_row = sum(r.shape[1] * r.dtype.itemsize for r in rows) + sum(w * jnp.dtype(d).itemsize for w, d in outs)
        tr = _row_tile(n_rows, per_row)
    n_r, n_b, n_o, n_a = len(rows), len(bcs), len(outs), len(accs)

    def body(*refs):
        r_refs, b_refs = refs[:n_r], refs[n_r: n_r + n_b]
        o_refs = refs[n_r + n_b: n_r + n_b + n_o]
        a_refs = refs[n_r + n_b + n_o:]
        if n_a:
            @pl.when(pl.program_id(0) == 0)
            def _():
                for a in a_refs:
                    a[...] = jnp.zeros_like(a)

        def work(rws):
            ins = [r[rws, :].astype(F32) for r in r_refs] + [b[...].astype(F32) for b in b_refs]
            res = fn(*ins)
            res = res if isinstance(res, (tuple, list)) else (res,)
            for o, v in zip(o_refs, res[:n_o]):
                o[rws, :] = v.astype(o.dtype)
            for a, v in zip(a_refs, res[n_o:]):
                _acc_add(a, v)

        _over_strips(tr, strip, work)

    in_specs = [pl.BlockSpec((tr, r.shape[1]), lambda i: (i, 0)) for r in rows]
    in_specs += [pl.BlockSpec((1, b.shape[1]), lambda i: (0, 0)) for b in bcs]
    out_specs = [pl.BlockSpec((tr, w), lambda i: (i, 0)) for w, _ in outs]
    out_specs += [pl.BlockSpec((SUBLANES, w), lambda i: (0, 0)) for w in accs]
    out_shape = [jax.ShapeDtypeStruct((n_rows, w), d) for w, d in outs]
    out_shape += [jax.ShapeDtypeStruct((SUBLANES, w), F32) for w in accs]
    res = pl.pallas_call(
        body, name=name, grid=(n_rows // tr,), in_specs=in_specs, out_specs=out_specs, out_shape=out_shape,
        compiler_params=_params(("arbitrary",) if n_a else ("parallel",)),
    )(*rows, *bcs)
    return tuple(res)


def _rowwise_bwd(name, fn, rows, bcs, cts, need_rows, row_dtypes, need_bcs=None, adds=None, fwd_sums=(), tr=None,
                 into=None):
    n_rows = rows[0].shape[0]
    adds = adds or {}
    into = into or {}
    into_keys = sorted(into)
    need_bcs = list(range(len(bcs))) if need_bcs is None else list(need_bcs)
    ct_arrays = [c for c in cts if not isinstance(c, float)]
    add_keys = sorted(adds)
    add_arrays = [adds[k] for k in add_keys]
    strip = _strip_rows(max(r.shape[1] for r in rows)) if tr is None else None
    if tr is None:
        per_row = sum(r.shape[1] * r.dtype.itemsize for r in list(rows) + ct_arrays + add_arrays)
        per_row += sum(rows[i].shape[1] * jnp.dtype(d).itemsize for i, d in zip(need_rows, row_dtypes))
        tr = _row_tile(n_rows, per_row)
    n_r, n_b, n_c, n_ad = len(rows), len(bcs), len(ct_arrays), len(add_arrays)
    n_go, n_gb, n_fs = len(need_rows), len(need_bcs), len(fwd_sums)

    def body(*refs):
        pos = [0]

        def take(n):
            pos[0] += n
            return refs[pos[0] - n: pos[0]]

        r_refs, b_refs, c_refs, ad_refs = take(n_r), take(n_b), take(n_c), take(n_ad)
        take(len(into_keys))
        go_refs = take(n_go)
        acc_refs = refs[pos[0]:]

        @pl.when(pl.program_id(0) == 0)
        def _():
            for a in acc_refs:
                a[...] = jnp.zeros_like(a)

        def wrapped(*a):
            r = fn(*a)
            return tuple(r) if isinstance(r, (tuple, list)) else (r,)

        def work(rws):
            r_t = [r[rws, :].astype(F32) for r in r_refs]
            b_t = [r[...].astype(F32) for r in b_refs]
            outs, vjp = jax.vjp(wrapped, *r_t, *b_t)
            it = iter(c_refs)
            full = tuple(jnp.full(o.shape, c, F32) if isinstance(c, float) else next(it)[rws, :].astype(F32)
                         for o, c in zip(outs, cts))
            grads = vjp(full)
            for o_ref, i in zip(go_refs, need_rows):
                g = grads[i]
                if i in adds:
                    g = g + ad_refs[add_keys.index(i)][rws, :].astype(F32)
                o_ref[rws, :] = g.astype(o_ref.dtype)
            for a, j in zip(acc_refs[:n_gb], need_bcs):
                _acc_add(a, grads[n_r + j])
            for a, j in zip(acc_refs[n_gb:], fwd_sums):
                a[0:1, :] += jnp.full((1, LANES), jnp.sum(outs[j]), F32)

        _over_strips(tr, strip, work)

    def row_spec(w):
        return pl.BlockSpec((tr, w), lambda i: (i, 0))

    in_specs = [row_spec(r.shape[1]) for r in rows]
    in_specs += [pl.BlockSpec((1, b.shape[1]), lambda i: (0, 0)) for b in bcs]
    in_specs += [row_spec(c.shape[1]) for c in ct_arrays] + [row_spec(a.shape[1]) for a in add_arrays]
    in_specs += [pl.BlockSpec(memory_space=pl.ANY) for _ in into_keys]
    out_specs = [row_spec(rows[i].shape[1]) for i in need_rows]
    out_shape = [jax.ShapeDtypeStruct((n_rows, rows[i].shape[1]), d) for i, d in zip(need_rows, row_dtypes)]
    aliases = {}
    for pos_in, k in enumerate(into_keys):
        buf, col_block = into[k]
        out_specs[k] = pl.BlockSpec((tr, rows[need_rows[k]].shape[1]), lambda i, cb=col_block: (i, cb))
        out_shape[k] = jax.ShapeDtypeStruct(buf.shape, buf.dtype)
        aliases[len(in_specs) - len(into_keys) + pos_in] = k
    out_specs += [pl.BlockSpec((SUBLANES, bcs[j].shape[1]), lambda i: (0, 0)) for j in need_bcs]
    out_specs += [pl.BlockSpec((SUBLANES, LANES), lambda i: (0, 0)) for _ in fwd_sums]
    out_shape += [jax.ShapeDtypeStruct((SUBLANES, bcs[j].shape[1]), F32) for j in need_bcs]
    out_shape += [jax.ShapeDtypeStruct((SUBLANES, LANES), F32) for _ in fwd_sums]
    res = pl.pallas_call(
        body, name=name, grid=(n_rows // tr,), in_specs=in_specs, out_specs=out_specs, out_shape=out_shape,
        input_output_aliases=aliases, compiler_params=_params(("arbitrary",)),
    )(*rows, *bcs, *ct_arrays, *add_arrays, *[into[k][0] for k in into_keys])
    return tuple(res)


def _fold(acc):
    return jnp.sum(acc, axis=0, keepdims=True)


def _rms(x, w):
    return x * lax.rsqrt(jnp.mean(x * x, axis=-1, keepdims=True) + EPS) * w


def _sigmoid(x):
    return jax.nn.sigmoid(x)


def _silu(x):
    return x * _sigmoid(x)


def _log1p(u):
    series = u * (1.0 - u * (0.5 - u * (1.0 / 3.0 - u * 0.25)))
    return jnp.where(u < 0.01, series, jnp.log(1.0 + u))


def _softplus(x):
    return jnp.maximum(x, 0.0) + _log1p(jnp.exp(-jnp.abs(x)))


def _st_pre(x, w):
    return _rms(x, w)


def _st_lora_norms(dq):
    def fn(cqkv, qn, kvn):
        return _rms(cqkv[:, :dq], qn), _rms(cqkv[:, dq:], kvn)
    return fn


def _st_gated_norm(n_groups):
    def fn(y, xs, z, dskip, wn):
        yz = (y + dskip * xs) * _silu(z)
        gw = yz.shape[1] // n_groups
        parts = [_rms(yz[:, g * gw:(g + 1) * gw], wn[:, g * gw:(g + 1) * gw]) for g in range(n_groups)]
        return jnp.concatenate(parts, axis=1)
    return fn


def _st_mix(d):
    def fn(g, ao, so):
        return _sigmoid(g[:, :d]) * ao + _sigmoid(g[:, d:]) * so
    return fn


def _st_res_norm(h, y, w_post, w_pre):
    h2 = h + _rms(y, w_post)
    return h2, _rms(h2, w_pre)


def _st_swiglu(gt, up):
    return _silu(gt) * up


def _st_loss(pe, gl, h2, tgt, w_post):
    e = pe * _sigmoid(gl)
    diff = h2 + _rms(e, w_post) - tgt
    return 0.5 * jnp.mean(diff * diff, axis=-1, keepdims=True)


def _rope_tables(positions):
    half = QK_ROPE // 2
    inv_freq = ROPE_THETA ** (-jnp.arange(0, QK_ROPE, 2, dtype=F32) / QK_ROPE)
    ang = positions.astype(F32).reshape(-1, 1) * inv_freq
    cos, sin = jnp.cos(ang), jnp.sin(ang)
    n = ang.shape[0]
    z = lambda w: jnp.zeros((n, w), F32)
    c_tab = jnp.concatenate([jnp.ones((n, QK_NOPE), F32), cos, cos, z(QK_PAD - QK_NOPE - QK_ROPE)], axis=1)
    a_tab = jnp.concatenate([z(QK_NOPE), -sin, z(half), z(QK_PAD - QK_NOPE - QK_ROPE)], axis=1)
    b_tab = jnp.concatenate([z(QK_NOPE), z(half), sin, z(QK_PAD - QK_NOPE - QK_ROPE)], axis=1)
    return c_tab, a_tab, b_tab


def _rot(x, c, a, b):
    half = QK_ROPE // 2
    return x * c + pltpu.roll(x, QK_PAD - half, axis=1) * a + pltpu.roll(x, half, axis=1) * b


def _rot_t(g, c, a, b):
    half = QK_ROPE // 2
    return g * c + pltpu.roll(g * a, half, axis=1) + pltpu.roll(g * b, QK_PAD - half, axis=1)


def _st_rope(n_heads):
    def fn(qraw, kraw, sm, c, a, b):
        kpe = _rot(sm[:, :QK_PAD], c, a, b)
        scale = float(QK_NOPE + QK_ROPE) ** -0.5
        q = [_rot(qraw[:, h * QK_PAD:(h + 1) * QK_PAD], c, a, b) * scale for h in range(n_heads)]
        k = [kraw[:, h * QK_PAD:(h + 1) * QK_PAD] + kpe for h in range(n_heads)]
        return jnp.concatenate(q, axis=1), jnp.concatenate(k, axis=1)
    return fn


def _st_rope_bwd(n_heads):
    def fn(dq, dk, c, a, b):
        dqraw = [_rot_t(dq[:, h * QK_PAD:(h + 1) * QK_PAD], c, a, b) for h in range(n_heads)]
        dks = dk[:, :QK_PAD]
        for h in range(1, n_heads):
            dks = dks + dk[:, h * QK_PAD:(h + 1) * QK_PAD]
        return jnp.concatenate(dqraw, axis=1), _rot_t(dks, c, a, b), dk
    return fn


def _split3(x):
    h1 = x.astype(BF16)
    r1 = x - h1.astype(F32)
    h2 = r1.astype(BF16)
    h3 = (r1 - h2.astype(F32)).astype(BF16)
    return h1, h2, h3


def _tri_dot(tri, x):
    h1, h2, h3 = _split3(x)
    return (_dot(tri, h3) + _dot(tri, h2)) + _dot(tri, h1)


def _dot_tri(x, tri):
    h1, h2, h3 = _split3(x)
    return (_dot(h3, tri) + _dot(h2, tri)) + _dot(h1, tri)


def _st_dt(sm, bias, alog):
    x = sm[:, QK_PAD:] + bias
    dt = _softplus(x)
    return dt, dt * (-jnp.exp(alog))


def _st_dt_bwd(sm, ddt, dcum, bias, alog):
    n = sm.shape[0]
    i = lax.broadcasted_iota(jnp.int32, (n, n), 0)
    j = lax.broadcasted_iota(jnp.int32, (n, n), 1)
    upper = (j >= i).astype(BF16)
    dda = _tri_dot(upper, dcum)
    x = sm[:, QK_PAD:] + bias
    dt = _softplus(x)
    a = -jnp.exp(alog)
    draw = (ddt + dda * a) * _sigmoid(x)
    return draw, draw, dda * dt * a


def _conv_fwd(xbc, w, b):
    S, C = xbc.shape
    tr = _pick(S, (512, 256))
    tc = _pick(C, (1024, 512, 256, 128))
    hb = tr // SUBLANES

    def body(x_ref, halo_ref, w_ref, b_ref, c_ref, a_ref, ext):
        i = pl.program_id(1)
        halo = jnp.where(i == 0, 0.0, halo_ref[...])
        ext[0:SUBLANES, :] = halo
        ext[SUBLANES:, :] = x_ref[...]

        def lane_strip(c, carry):
            lanes = pl.ds(pl.multiple_of(c * LANES, LANES), LANES)
            wv = w_ref[:, lanes]
            acc = b_ref[:, lanes] + wv[CONV_WIDTH - 1:CONV_WIDTH, :] * ext[pl.ds(SUBLANES, tr), lanes]
            for k in range(CONV_WIDTH - 1):
                off = SUBLANES - (CONV_WIDTH - 1) + k
                acc = acc + wv[k:k + 1, :] * ext[pl.ds(off, tr), lanes]
            c_ref[:, lanes] = acc
            a_ref[:, lanes] = _silu(acc)
            return carry

        lax.fori_loop(0, tc // LANES, lane_strip, 0)

    return pl.pallas_call(
        body, name="conv_fwd", grid=(C // tc, S // tr),
        in_specs=[pl.BlockSpec((tr, tc), lambda j, i: (i, j)),
                  pl.BlockSpec((SUBLANES, tc), lambda j, i: (jnp.maximum(i * hb - 1, 0), j)),
                  pl.BlockSpec((CONV_WIDTH, tc), lambda j, i: (0, j)),
                  pl.BlockSpec((1, tc), lambda j, i: (0, j))],
        out_specs=[pl.BlockSpec((tr, tc), lambda j, i: (i, j))] * 2,
        out_shape=[jax.ShapeDtypeStruct((S, C), F32)] * 2,
        scratch_shapes=[pltpu.VMEM((tr + SUBLANES, tc), F32)],
        compiler_params=_params(("parallel", "arbitrary")),
    )(xbc, xbc, w, b)


def _conv_bwd(xbc, dconv, w, buf, col0):
    S, C = xbc.shape
    tr = _pick(S, (512, 256))
    tc = _pick(C, (1024, 512, 256, 128))
    hb = tr // SUBLANES
    n_i = S // tr
    assert col0 % tc == 0, (col0, tc)
    cb0 = col0 // tc

    def body(x_ref, halo_ref, d_ref, dnext_ref, w_ref, buf_ref, dx_ref, dw_ref, ext, dext):
        i = pl.program_id(1)
        ext[0:SUBLANES, :] = jnp.where(i == 0, 0.0, halo_ref[...])
        ext[SUBLANES:, :] = x_ref[...]
        dext[0:tr, :] = d_ref[...]
        dext[tr:, :] = jnp.where(i == n_i - 1, 0.0, dnext_ref[...])

        @pl.when(i == 0)
        def _():
            dw_ref[...] = jnp.zeros_like(dw_ref)

        def lane_strip(c, carry):
            lanes = pl.ds(pl.multiple_of(c * LANES, LANES), LANES)
            wv = w_ref[:, lanes]
            d = dext[pl.ds(0, tr), lanes]
            dx = wv[CONV_WIDTH - 1:CONV_WIDTH, :] * d
            for k in range(CONV_WIDTH):
                if k < CONV_WIDTH - 1:
                    dx = dx + wv[k:k + 1, :] * dext[pl.ds(CONV_WIDTH - 1 - k, tr), lanes]
                xs = ext[pl.ds(SUBLANES - (CONV_WIDTH - 1) + k, tr), lanes]
                prod = d * xs
                dw_ref[k * SUBLANES:(k + 1) * SUBLANES, lanes] += prod.reshape(tr // SUBLANES, SUBLANES, LANES).sum(axis=0)
            dx_ref[:, lanes] = dx.astype(dx_ref.dtype)
            return carry

        lax.fori_loop(0, tc // LANES, lane_strip, 0)

    return pl.pallas_call(
        body, name="conv_bwd", grid=(C // tc, n_i),
        in_specs=[pl.BlockSpec((tr, tc), lambda j, i: (i, j)),
                  pl.BlockSpec((SUBLANES, tc), lambda j, i: (jnp.maximum(i * hb - 1, 0), j)),
                  pl.BlockSpec((tr, tc), lambda j, i: (i, j)),
                  pl.BlockSpec((SUBLANES, tc), lambda j, i: (jnp.minimum((i + 1) * hb, S // SUBLANES - 1), j)),
                  pl.BlockSpec((CONV_WIDTH, tc), lambda j, i: (0, j)),
                  pl.BlockSpec(memory_space=pl.ANY)],
        out_specs=[pl.BlockSpec((tr, tc), lambda j, i: (i, cb0 + j)),
                   pl.BlockSpec((CONV_WIDTH * SUBLANES, tc), lambda j, i: (0, j))],
        out_shape=[jax.ShapeDtypeStruct(buf.shape, buf.dtype), jax.ShapeDtypeStruct((CONV_WIDTH * SUBLANES, C), F32)],
        scratch_shapes=[pltpu.VMEM((tr + SUBLANES, tc), F32), pltpu.VMEM((tr + SUBLANES, tc), F32)],
        input_output_aliases={5: 0},
        compiler_params=_params(("parallel", "arbitrary")),
    )(xbc, xbc, dconv, dconv, w, buf)


def _st_dconv(d_inner):
    def fn(xc, dxa, dxb, db_, dc_):
        s = _sigmoid(xc)
        g = jnp.concatenate([dxa + dxb, db_, dc_], axis=1) * (s * (1.0 + xc * (1.0 - s)))
        return g, g
    return fn


def _chunk_setup_t(b_ref, c_ref, dac_ref, dar_ref, L):
    ii = lax.broadcasted_iota(jnp.int32, (L, L), 0)
    jj = lax.broadcasted_iota(jnp.int32, (L, L), 1)
    lower = ii >= jj
    upper = ii <= jj
    cum_c = _tri_dot(lower.astype(BF16), dac_ref[0])
    cum_r = _dot_tri(dar_ref[0], upper.astype(BF16))
    bm = b_ref[...].astype(BF16)
    cm = c_ref[...].astype(BF16)
    return lower, upper, cum_c, cum_r, bm, cm


def _ssd_specs_t(d_inner, n_groups, gw, L, rp, chunk_of):
    bb0 = d_inner // D_STATE
    cb0 = bb0 + n_groups
    return [pl.BlockSpec((L, gw), lambda g, c: (chunk_of(c), g)),
            pl.BlockSpec((L, D_STATE), lambda g, c: (chunk_of(c), bb0 + g)),
            pl.BlockSpec((L, D_STATE), lambda g, c: (chunk_of(c), cb0 + g)),
            pl.BlockSpec((1, rp, L), lambda g, c: (g, 0, chunk_of(c))),
            pl.BlockSpec((1, L, LANES), lambda g, c: (g, chunk_of(c), 0)),
            pl.BlockSpec((1, rp, L), lambda g, c: (g, 0, chunk_of(c)))]


def _ssd_fwd_t(xbc_a, dt_row, da_col, da_row, d_inner, n_groups, R):
    S = xbc_a.shape[0]
    L = min(CHUNK, S)
    NC = S // L
    P, N = SSM_HEADDIM, D_STATE
    gw = R * P
    rp = da_row.shape[1]

    def body(x_ref, b_ref, c_ref, dt_ref, dac_ref, dar_ref, y_ref, st_ref, state, y_t):
        @pl.when(pl.program_id(1) == 0)
        def _():
            state[...] = jnp.zeros_like(state)

        st_ref[0, 0] = state[...]
        lower, upper, cum_c, cum_r, bm, cm = _chunk_setup_t(b_ref, c_ref, dac_ref, dar_ref, L)
        gm_t = _dot_nt(bm, cm)
        dt = dt_ref[0]
        x_t = x_ref[...].T
        for r in range(R):
            cc = cum_c[:, r:r + 1]
            cr = cum_r[r:r + 1, :]
            m_t = (gm_t * jnp.exp(jnp.where(upper, cr - cc, -jnp.inf))).astype(BF16)
            x = x_t[r * P:(r + 1) * P, :] * dt[r:r + 1, :]
            s_r = state[r * P:(r + 1) * P, :]
            y_off = _dot_nt(s_r.astype(BF16), cm) * jnp.exp(cr)
            y_t[r * P:(r + 1) * P, :] = _dot(x.astype(BF16), m_t) + y_off
            last = cr[:, L - 1:L]
            xw = (x * jnp.exp(last - cr)).astype(BF16)
            state[r * P:(r + 1) * P, :] = s_r * jnp.exp(last) + _dot(xw, bm)
        y_ref[...] = y_t[...].T

    return pl.pallas_call(
        body, name="ssd_fwd", grid=(n_groups, NC),
        in_specs=_ssd_specs_t(d_inner, n_groups, gw, L, rp, lambda c: c),
        out_specs=[pl.BlockSpec((L, gw), lambda g, c: (c, g)),
                   pl.BlockSpec((1, 1, gw, N), lambda g, c: (g, c, 0, 0))],
        out_shape=[jax.ShapeDtypeStruct((S, d_inner), F32), jax.ShapeDtypeStruct((n_groups, NC, gw, N), F32)],
        scratch_shapes=[pltpu.VMEM((gw, N), F32), pltpu.VMEM((gw, L), F32)],
        compiler_params=_params(("parallel", "arbitrary")),
    )(xbc_a, xbc_a, xbc_a, dt_row, da_col, da_row)


def _ssd_bwd_t(xbc_a, dt_row, da_col, da_row, states, dy, d_inner, n_groups, R):
    S = xbc_a.shape[0]
    L = min(CHUNK, S)
    NC = S // L
    P, N = SSM_HEADDIM, D_STATE
    gw = R * P
    rp = da_row.shape[1]
    rev = lambda c: NC - 1 - c

    def body(x_ref, b_ref, c_ref, dt_ref, dac_ref, dar_ref, st_ref, dy_ref,
             dx_ref, db_ref, dc_ref, ddt_ref, dcum_ref, dstate, dx_t):
        @pl.when(pl.program_id(1) == 0)
        def _():
            dstate[...] = jnp.zeros_like(dstate)

        lower, upper, cum_c, cum_r, bm, cm = _chunk_setup_t(b_ref, c_ref, dac_ref, dar_ref, L)
        gm = _dot_nt(cm, bm)
        gm_t = _dot_nt(bm, cm)
        dt = dt_ref[0]
        x_t = x_ref[...].T
        dy_t = dy_ref[...].T
        sub = lax.broadcasted_iota(jnp.int32, (rp, L), 0)
        is_last = lax.broadcasted_iota(jnp.int32, (1, L), 1) == L - 1
        d_g = jnp.zeros((L, L), F32)
        d_g_t = jnp.zeros((L, L), F32)
        dc_acc = jnp.zeros((L, N), F32)
        db_acc = jnp.zeros((L, N), F32)
        ddt_out = jnp.zeros((rp, L), F32)
        dcum_out = jnp.zeros((rp, L), F32)
        for r in range(R):
            cc = jnp.broadcast_to(cum_c[:, r:r + 1], (L, L))
            cr = cum_r[r:r + 1, :]
            lam = jnp.exp(jnp.where(lower, cc - cum_r[r:r + 1, :], -jnp.inf))
            lam_t = jnp.exp(jnp.where(upper, cr - cc, -jnp.inf))
            m = gm * lam
            m_t = gm_t * lam_t
            dtr = dt[r:r + 1, :]
            xh = x_t[r * P:(r + 1) * P, :]
            x = xh * dtr
            xb = x.astype(BF16)
            d_y = dy_t[r * P:(r + 1) * P, :]
            d_yb = d_y.astype(BF16)
            s_r = st_ref[0, 0, r * P:(r + 1) * P, :]
            s_rb = s_r.astype(BF16)
            ds_n = dstate[r * P:(r + 1) * P, :]
            ds_nb = ds_n.astype(BF16)
            e = jnp.exp(cr)
            last = cr[:, L - 1:L]
            e_last = jnp.exp(last)
            w = jnp.exp(last - cr)
            d_x = _dot(d_yb, m.astype(BF16))
            d_m = _dot_tn(d_yb, xb)
            d_m_t = _dot_tn(xb, d_yb)
            d_ye = (d_y * e).astype(BF16)
            dc_acc = dc_acc + _dot_tn(d_ye, s_rb)
            ds_part = _dot(d_ye, cm)
            y_off = _dot_nt(s_rb, cm) * e
            dcum = jnp.sum(d_y * y_off, axis=0, keepdims=True)
            d_xw = _dot_nt(ds_nb, bm)
            d_x = d_x + d_xw * w
            dw_w = jnp.sum(d_xw * x, axis=0, keepdims=True) * w
            db_acc = db_acc + _dot_tn((x * w).astype(BF16), ds_nb)
            d_last = jnp.sum(ds_n * s_r, keepdims=True) * e_last + jnp.sum(dw_w, keepdims=True)
            dstate[r * P:(r + 1) * P, :] = e_last * ds_n + ds_part
            d_g = d_g + d_m * lam
            d_g_t = d_g_t + d_m_t * lam_t
            dcum = (dcum - dw_w + jnp.sum(d_m_t * m_t, axis=0, keepdims=True)
                    - jnp.sum(d_m * m, axis=0, keepdims=True) + jnp.where(is_last, d_last, 0.0))
            dx_t[r * P:(r + 1) * P, :] = d_x * dtr
            ddt = jnp.sum(d_x * xh, axis=0, keepdims=True)
            ddt_out = ddt_out + jnp.where(sub == r, ddt, 0.0)
            dcum_out = dcum_out + jnp.where(sub == r, dcum, 0.0)
        dc_ref[...] = dc_acc + _dot(d_g.astype(BF16), bm)
        db_ref[...] = db_acc + _dot(d_g_t.astype(BF16), cm)
        dx_ref[...] = dx_t[...].T
        ddt_ref[0] = ddt_out
        dcum_ref[0] = dcum_out

    gn = n_groups * N
    return pl.pallas_call(
        body, name="ssd_bwd", grid=(n_groups, NC),
        in_specs=_ssd_specs_t(d_inner, n_groups, gw, L, rp, rev) + [
            pl.BlockSpec((1, 1, gw, N), lambda g, c: (g, rev(c), 0, 0)),
            pl.BlockSpec((L, gw), lambda g, c: (rev(c), g))],
        out_specs=[pl.BlockSpec((L, gw), lambda g, c: (rev(c), g)),
                   pl.BlockSpec((L, N), lambda g, c: (rev(c), g)),
                   pl.BlockSpec((L, N), lambda g, c: (rev(c), g)),
                   pl.BlockSpec((1, rp, L), lambda g, c: (g, 0, rev(c))),
                   pl.BlockSpec((1, rp, L), lambda g, c: (g, 0, rev(c)))],
        out_shape=[jax.ShapeDtypeStruct((S, d_inner), F32), jax.ShapeDtypeStruct((S, gn), F32),
                   jax.ShapeDtypeStruct((S, gn), F32), jax.ShapeDtypeStruct((n_groups, rp, S), F32),
                   jax.ShapeDtypeStruct((n_groups, rp, S), F32)],
        scratch_shapes=[pltpu.VMEM((gw, N), F32), pltpu.VMEM((gw, L), F32)],
        compiler_params=_params(("parallel", "arbitrary")),
    )(xbc_a, xbc_a, xbc_a, dt_row, da_col, da_row, states, dy)


def _attn_scale():
    return float(QK_NOPE + QK_ROPE) ** -0.5


def _diag_mask(t):
    return lax.broadcasted_iota(jnp.int32, (t, t), 0) <= lax.broadcasted_iota(jnp.int32, (t, t), 1)


def _walk_wide(lo, hi, tile_step, joint=True, widest=4):
    n = hi - lo

    def step(j, width):
        if joint:
            tile_step(j, width)
        else:
            for u in range(width):
                tile_step(j + u, 1)

    def widest_run(t, carry):
        step(lo + widest * t, widest)
        return carry

    lax.fori_loop(0, n // widest, widest_run, 0)

    def shorter_run(width):
        @pl.when(n % (2 * width) >= width)
        def _():
            step(hi - n % (2 * width), width)

    width = widest // 2
    while width >= 1:
        shorter_run(width)
        width //= 2


def _carried(main_body, n_in, n_out, n_scratch, carry, grid):
    if carry is None:
        return main_body
    n_ci, n_co = len(carry["ins"]), len(carry["outs"])

    def body(*refs):
        pos = [0]

        def take(n):
            pos[0] += n
            return refs[pos[0] - n: pos[0]]

        ins, c_ins, outs, c_outs, scratch, sems = take(n_in), take(n_ci), take(n_out), take(n_co), take(n_scratch), take(3)
        steps = [pl.program_id(a) for a in range(len(grid))]

        @pl.when(functools.reduce(jnp.logical_and, [s == 0 for s in steps]))
        def _():
            for cp in carry["copies"](c_ins, c_outs, *sems):
                cp.start()

        main_body(*ins, *outs, *scratch)

        @pl.when(functools.reduce(jnp.logical_and, [s == n - 1 for s, n in zip(steps, grid)]))
        def _():
            for cp in carry["copies"](c_ins, c_outs, *sems):
                cp.wait()

    return body


def _carry_call(carry):
    if carry is None:
        return [], [], [], [], []
    sems = [pltpu.SemaphoreType.DMA((carry["n_remote"],)), pltpu.SemaphoreType.DMA((carry["n_remote"],)),
            pltpu.SemaphoreType.DMA((max(carry["n_local"], 1),))]
    return (list(carry["ins"]), [_HBM] * len(carry["ins"]), [_HBM] * len(carry["outs"]), list(carry["outs"]), sems)


def _flash_fwd(q, k, v_t, n_heads, carry=None):
    S = q.shape[0]
    T = min(FLASH_T, S)
    grid = (n_heads, S // T)
    c_args, c_in_specs, c_out_specs, c_out_shapes, c_sems = _carry_call(carry)

    def body(q_ref, k_ref, vt_ref, o_ref, ob_ref, lse_ref, m_s, l_s, acc_t):
        i = pl.program_id(1)
        m_s[...] = jnp.full_like(m_s, -jnp.inf)
        l_s[...] = jnp.zeros_like(l_s)
        acc_t[...] = jnp.zeros_like(acc_t)
        qv = q_ref[...]

        def step(j, width, masked):
            keys = pl.ds(pl.multiple_of(j * T, T), width * T)
            s_t = _dot_nt(k_ref[keys, :], qv)
            if masked:
                s_t = jnp.where(_diag_mask(T), s_t, -jnp.inf)
            m_prev = m_s[...]
            m_new = jnp.maximum(m_prev, jnp.max(s_t, axis=0, keepdims=True))
            alpha = jnp.exp(m_prev - m_new)
            p_t = jnp.exp(s_t - m_new)
            l_s[...] = alpha * l_s[...] + jnp.sum(p_t, axis=0, keepdims=True)
            acc_t[...] = alpha * acc_t[...] + _dot(vt_ref[:, keys], p_t.astype(BF16))
            m_s[...] = m_new

        _walk_wide(0, i, lambda j, width: step(j, width, False), widest=min(FLASH_FWD_RUN, S // T))
        step(i, 1, True)
        o = (acc_t[...] / l_s[...]).T
        o_ref[...] = o
        ob_ref[...] = o.astype(BF16)
        lse_ref[0] = m_s[...] + jnp.log(l_s[...])

    res = pl.pallas_call(
        _carried(body, 3, 3, 3, carry, grid), name="flash_fwd", grid=grid,
        in_specs=[pl.BlockSpec((T, QK_PAD), lambda h, i: (i, h)),
                  pl.BlockSpec((S, QK_PAD), lambda h, i: (0, h)),
                  pl.BlockSpec((V_DIM, S), lambda h, i: (h, 0))] + c_in_specs,
        out_specs=[pl.BlockSpec((T, V_DIM), lambda h, i: (i, h)),
                   pl.BlockSpec((T, V_DIM), lambda h, i: (i, h)),
                   pl.BlockSpec((1, 1, T), lambda h, i: (h, 0, i))] + c_out_specs,
        out_shape=[jax.ShapeDtypeStruct((S, n_heads * V_DIM), F32), jax.ShapeDtypeStruct((S, n_heads * V_DIM), BF16),
                   jax.ShapeDtypeStruct((n_heads, 1, S), F32)] + c_out_shapes,
        scratch_shapes=[pltpu.VMEM((1, T), F32), pltpu.VMEM((1, T), F32), pltpu.VMEM((V_DIM, T), F32)] + c_sems,
        compiler_params=_params(("arbitrary", "arbitrary")),
    )(q, k, v_t, *c_args)
    return res[0], res[1], res[2], list(res[3:])


def _flash_bwd(q, k, v, do, lse_row, delta_row, n_heads, carry=None):
    S = q.shape[0]
    T = min(FLASH_T, S)
    nq = S // T
    grid = (n_heads, S // T)
    c_args, c_in_specs, c_out_specs, c_out_shapes, c_sems = _carry_call(carry)

    def body(q_ref, k_ref, v_ref, do_ref, lse_ref, dl_ref, dq_ref, dk_ref, dv_ref, dk_acc, dv_acc):
        j = pl.program_id(1)

        @pl.when(j == 0)
        def _():
            dq_ref[...] = jnp.zeros_like(dq_ref)

        dk_acc[...] = jnp.zeros_like(dk_acc)
        dv_acc[...] = jnp.zeros_like(dv_acc)
        kv = k_ref[...]
        vv = v_ref[...]

        def step(i, width, masked):
            cols = pl.ds(pl.multiple_of(i * T, T), width * T)
            qt = q_ref[cols, :]
            dot = do_ref[cols, :]
            s_t = _dot_nt(kv, qt)
            if masked:
                s_t = jnp.where(_diag_mask(T), s_t, -jnp.inf)
            p_t = jnp.exp(s_t - lse_ref[0, :, cols])
            dv_acc[...] += _dot(p_t.astype(BF16), dot)
            ds_t = (p_t * (_dot_nt(vv, dot) - dl_ref[0, :, cols])).astype(BF16)
            dk_acc[...] += _dot(ds_t, qt)
            dq_ref[cols, :] += _dot_tn(ds_t, kv)

        step(j, 1, True)
        _walk_wide(j + 1, nq, lambda i, width: step(i, width, False), joint=False, widest=min(FLASH_BWD_RUN, nq))
        dk_ref[...] = dk_acc[...]
        dv_ref[...] = dv_acc[...].astype(dv_ref.dtype)

        @pl.when(j == nq - 1)
        def _():
            dq_ref[...] = dq_ref[...] * _attn_scale()

    res = pl.pallas_call(
        _carried(body, 6, 3, 2, carry, grid), name="flash_bwd", grid=grid,
        in_specs=[pl.BlockSpec((S, QK_PAD), lambda h, j: (0, h)),
                  pl.BlockSpec((T, QK_PAD), lambda h, j: (j, h)),
                  pl.BlockSpec((T, V_DIM), lambda h, j: (j, h)),
                  pl.BlockSpec((S, V_DIM), lambda h, j: (0, h)),
                  pl.BlockSpec((1, 1, S), lambda h, j: (h, 0, 0)),
                  pl.BlockSpec((1, 1, S), lambda h, j: (h, 0, 0))] + c_in_specs,
        out_specs=[pl.BlockSpec((S, QK_PAD), lambda h, j: (0, h)),
                   pl.BlockSpec((T, QK_PAD), lambda h, j: (j, h)),
                   pl.BlockSpec((T, V_DIM), lambda h, j: (j, h))] + c_out_specs,
        out_shape=[jax.ShapeDtypeStruct((S, n_heads * QK_PAD), F32), jax.ShapeDtypeStruct((S, n_heads * QK_PAD), F32),
                   jax.ShapeDtypeStruct((S, n_heads * V_DIM), BF16)] + c_out_shapes,
        scratch_shapes=[pltpu.VMEM((T, QK_PAD), F32), pltpu.VMEM((T, V_DIM), F32)] + c_sems,
        compiler_params=_params(("arbitrary", "arbitrary")),
    )(q, k, v, do, lse_row, delta_row, *c_args)
    return res[0], res[1], res[2], list(res[3:])


def _st_delta(n_heads):
    def fn(do, o):
        prod = do * o
        lane = lax.broadcasted_iota(jnp.int32, (do.shape[0], LANES), 1)
        out = jnp.zeros((do.shape[0], LANES), F32)
        for h in range(n_heads):
            out = out + jnp.where(lane == h, jnp.sum(prod[:, h * V_DIM:(h + 1) * V_DIM], axis=1, keepdims=True), 0.0)
        return out
    return fn


def _pad_cols(w, width):
    return jnp.pad(w, ((0, 0), (0, width - w.shape[1])))


def _dims(x, p, q_norm, kv_norm, w_uq, dt_bias, ssm_norm, conv_b):
    d = dict(S=x.shape[0], D=x.shape[1], PLE=p.shape[1], DQ=q_norm.shape[1], DKV=kv_norm.shape[1],
             NH=w_uq.shape[1] // (QK_NOPE + QK_ROPE), NHS=dt_bias.shape[1], DI=ssm_norm.shape[1],
             CONV=conv_b.shape[1])
    d["G"] = (d["CONV"] - d["DI"]) // (2 * D_STATE)
    d["R"] = d["NHS"] // d["G"]
    return d


def _assemble(name, blocks):
    rows, cols = blocks.shape[1:]
    if name in COL_SHARDED:
        return blocks.transpose(1, 0, 2).reshape(rows, N_DEV * cols)
    return blocks.reshape(N_DEV * rows, cols)


def _by_device(name, g):
    if name in COL_SHARDED:
        return g.reshape(g.shape[0], N_DEV, g.shape[1] // N_DEV).transpose(1, 0, 2)
    return g.reshape(N_DEV, g.shape[0] // N_DEV, g.shape[1])


def _local_step(x, p, positions, target, small, conv_w, wts, late_names=(), late_shards=(), exchange=True):
    wts = dict(wts)
    dm = _dims(x, p, small["q_norm"], small["kv_norm"], wts["w_uq"], small["dt_bias"], small["ssm_norm"],
               small["conv_b"])
    S, D, DQ, DKV, NH, NHS, DI, CONV, G, R = (dm[k] for k in ("S", "D", "DQ", "DKV", "NH", "NHS", "DI", "CONV", "G", "R"))
    rp = -(-R // SUBLANES) * SUBLANES
    L = min(CHUNK, S)

    w_nat = wts["w_in"]
    o = [0]
    for n in (DQ, DKV, QK_ROPE, DI, CONV, NHS, D, D):
        o.append(o[-1] + n)
    pw = (DI, 2 * D, CONV, DQ + DKV, QK_ROPE + NHS)
    po = [sum(pw[:k]) for k in range(len(pw))]
    w_in = jnp.concatenate([w_nat[:, o[3]:o[4]], w_nat[:, o[6]:o[8]], w_nat[:, o[4]:o[5]], w_nat[:, o[0]:o[2]],
                            w_nat[:, o[2]:o[3]], w_nat[:, o[5]:o[6]]], axis=1)
    w_z, w_g, w_xbc, w_cqkv = (w_in[:, po[k]:po[k] + pw[k]] for k in range(4))
    w_kr, w_dt = w_in[:, po[4]:po[4] + QK_ROPE], w_in[:, po[4] + QK_ROPE:]
    zc = lambda n: jnp.zeros((D, n), BF16)
    w_sm = jnp.concatenate([zc(QK_NOPE), w_kr, zc(QK_PAD - QK_NOPE - QK_ROPE), w_dt, zc(LANES - NHS)], axis=1)
    w_q = jnp.pad(wts["w_uq"].reshape(DQ, NH, QK_NOPE + QK_ROPE),
                  ((0, 0), (0, 0), (0, QK_PAD - QK_NOPE - QK_ROPE))).reshape(DQ, NH * QK_PAD)
    ukv = wts["w_ukv"].reshape(DKV, NH, QK_NOPE + V_DIM)
    w_k = jnp.pad(ukv[:, :, :QK_NOPE], ((0, 0), (0, 0), (0, QK_PAD - QK_NOPE))).reshape(DKV, NH * QK_PAD)
    w_v = ukv[:, :, QK_NOPE:].reshape(DKV, NH * V_DIM)
    dt_bias_p, a_log_p = _pad_cols(small["dt_bias"], LANES), _pad_cols(small["a_log"], LANES)
    dskip_rep = jnp.repeat(small["d_skip"], SSM_HEADDIM, axis=1)
    c_tab, a_tab, b_tab = _rope_tables(positions)

    (u,) = _rowwise("pre_norm", _st_pre, [x], [small["mix_norm_pre"]], [(D, BF16)])
    cqkv = _mm("in_cqkv", u, w_cqkv, "nn")
    z = _mm("in_z", u, w_z, "nn")
    xbc = _mm("in_xbc", u, w_xbc, "nn")
    g = _mm("in_gates", u, w_g, "nn")
    sm = _mm("in_small", u, w_sm, "nn")

    lora_fn = _st_lora_norms(DQ)
    cq_n, ckv_n = _rowwise("lora_norms", lora_fn, [cqkv], [small["q_norm"], small["kv_norm"]], [(DQ, BF16), (DKV, BF16)])
    qraw = _mm("up_q", cq_n, w_q, "nn")
    kraw = _mm("up_k", ckv_n, w_k, "nn")
    v = _mm("up_v", ckv_n, w_v, "nn", out_dtype=BF16)
    v_t = _mm("up_v_t", w_v.T, ckv_n, "nt", out_dtype=BF16)
    q, k = _rowwise("rope", _st_rope(NH), [qraw, kraw, sm, c_tab, a_tab, b_tab], [],
                    [(NH * QK_PAD, BF16), (NH * QK_PAD, BF16)])
    attn, attn_b, lse, late_blocks = _flash_fwd(q, k, v_t, NH, carry=_gather_carry(list(late_shards)) if late_names else None)
    wts.update({n: _assemble(n, b) for n, b in zip(late_names, late_blocks)})

    xbc_c, xbc_a = _conv_fwd(xbc, conv_w, small["conv_b"])
    dt, da = _rowwise("dt", _st_dt, [sm], [dt_bias_p, a_log_p], [(LANES, F32), (LANES, F32)])

    def col_layout(t):
        return _pad_cols(t[:, :NHS].reshape(S, G, R).transpose(1, 0, 2).reshape(G * S, R), LANES).reshape(G, S, LANES)

    def row_layout(t):
        return jnp.pad(t[:, :NHS].reshape(S, G, R).transpose(1, 2, 0), ((0, 0), (0, rp - R), (0, 0)))

    dt_row, da_col, da_row = row_layout(dt), col_layout(da), row_layout(da)
    xs = xbc_a[:, :DI]
    y, states = _ssd_fwd_t(xbc_a, dt_row, da_col, da_row, DI, G, R)
    gn_fn = _st_gated_norm(G)
    (ssm,) = _rowwise("gated_norm", gn_fn, [y, xs, z], [dskip_rep, small["ssm_norm"]], [(DI, BF16)])

    ao = _mm("attn_o", attn_b, wts["w_attn_o"], "nn")
    so = _mm("ssm_o", ssm, wts["w_ssm_o"], "nn")
    mix_fn = _st_mix(D)
    (mixed,) = _rowwise("mix", mix_fn, [g, ao, so], [], [(D, BF16)])
    mo = _mm("out_proj", mixed, wts["w_out"], "nn")
    h1, f = _rowwise("res1", _st_res_norm, [x, mo], [small["mix_norm_post"], small["ffn_norm_pre"]], [(D, F32), (D, BF16)])
    gt = _mm("ffn_gate", f, wts["w_gate"], "nn")
    up = _mm("ffn_up", f, wts["w_up"], "nn")
    (act,) = _rowwise("swiglu", _st_swiglu, [gt, up], [], [(gt.shape[1], BF16)])
    dn = _mm("ffn_down", act, wts["w_down"], "nn")
    h2, a3 = _rowwise("res2", _st_res_norm, [h1, dn], [small["ffn_norm_post"], small["ple_norm_pre"]], [(D, F32), (D, BF16)])
    gl = _mm("ple_gate", a3, wts["w_ple_gate"], "nn")
    pe = _mm("ple_proj", p, wts["w_ple"], "nn")

    sg = {}
    bg = {}
    dpe, dgl, dh2, d_w, loss_acc = _rowwise_bwd(
        "loss", _st_loss, [pe, gl, h2, target], [small["ple_norm_post"]], [1.0], [0, 1, 2], [BF16, BF16, F32], fwd_sums=(0,))
    sg["ple_norm_post"] = _fold(d_w)
    loss = loss_acc[0:1, :]
    bg["w_ple"] = _mm("d_w_ple", p, dpe, "tn", out_dtype=BF16)
    bg["w_ple_gate"] = _mm("d_w_ple_gate", a3, dgl, "tn", out_dtype=BF16)
    da3 = _mm("d_a3", dgl, wts["w_ple_gate"], "nt")

    dh1, ddn, d_post, d_pre = _rowwise_bwd(
        "res2_bwd", _st_res_norm, [h1, dn], [small["ffn_norm_post"], small["ple_norm_pre"]], [dh2, da3], [0, 1], [F32, BF16])
    sg["ffn_norm_post"], sg["ple_norm_pre"] = _fold(d_post), _fold(d_pre)
    bg["w_down"] = _mm("d_w_down", act, ddn, "tn", out_dtype=BF16)
    dact = _mm("d_act", ddn, wts["w_down"], "nt")
    dgt, dup = _rowwise_bwd("swiglu_bwd", _st_swiglu, [gt, up], [], [dact], [0, 1], [BF16, BF16])
    bg["w_gate"] = _mm("d_w_gate", f, dgt, "tn", out_dtype=BF16)
    bg["w_up"] = _mm("d_w_up", f, dup, "tn", out_dtype=BF16)
    df = _mm("d_f_gate", dgt, wts["w_gate"], "nt")
    df = _mm("d_f_up", dup, wts["w_up"], "nt", acc_in=df)

    dx_res, dmo, d_post, d_pre = _rowwise_bwd(
        "res1_bwd", _st_res_norm, [x, mo], [small["mix_norm_post"], small["ffn_norm_pre"]], [dh1, df], [0, 1], [F32, BF16])
    sg["mix_norm_post"], sg["ffn_norm_pre"] = _fold(d_post), _fold(d_pre)
    bg["w_out"] = _mm("d_w_out", mixed, dmo, "tn", out_dtype=BF16)
    dmixed = _mm("d_mixed", dmo, wts["w_out"], "nt")
    assert po[1] % pw[1] == 0 and po[3] % pw[3] == 0, (po, pw)
    dproj = lax.empty((S, sum(pw)), BF16)
    dproj, dao, dso = _rowwise_bwd("mix_bwd", mix_fn, [g, ao, so], [], [dmixed], [0, 1, 2], [BF16, BF16, BF16],
                                   into={0: (dproj, po[1] // pw[1])})
    bg["w_attn_o"] = _mm("d_w_attn_o", attn_b, dao, "tn", out_dtype=BF16)
    bg["w_ssm_o"] = _mm("d_w_ssm_o", ssm, dso, "tn", out_dtype=BF16)
    dattn = _mm("d_attn", dao, wts["w_attn_o"], "nt", out_dtype=BF16)
    dssm = _mm("d_ssm", dso, wts["w_ssm_o"], "nt")

    dy, dxs_a, dproj, d_dskip, d_ssmn = _rowwise_bwd(
        "gated_norm_bwd", gn_fn, [y, xs, z], [dskip_rep, small["ssm_norm"]], [dssm], [0, 1, 2], [F32, F32, BF16],
        into={2: (dproj, 0)})
    sg["d_skip"] = _fold(d_dskip).reshape(NHS, SSM_HEADDIM).sum(axis=1).reshape(1, NHS)
    sg["ssm_norm"] = _fold(d_ssmn)
    dxs_b, d_b, d_c, ddt_row, dcum_row = _ssd_bwd_t(xbc_a, dt_row, da_col, da_row, states, dy, DI, G, R)

    def from_row(t):
        return _pad_cols(t[:, :R, :].transpose(2, 0, 1).reshape(S, NHS), LANES)

    ddtraw, d_bias, d_alog = _rowwise("dt_bwd", _st_dt_bwd, [sm, from_row(ddt_row), from_row(dcum_row)],
                                      [dt_bias_p, a_log_p], [(LANES, F32)], accs=(LANES, LANES), tr=L)
    sg["dt_bias"], sg["a_log"] = _fold(d_bias)[:, :NHS], _fold(d_alog)[:, :NHS]
    dconv, d_cb = _rowwise("dconv", _st_dconv(DI), [xbc_c, dxs_a, dxs_b, d_b, d_c], [], [(CONV, F32)], accs=(CONV,))
    sg["conv_b"] = _fold(d_cb)
    dproj, d_cw = _conv_bwd(xbc, dconv, conv_w, dproj, po[2])
    d_conv_w = d_cw.reshape(CONV_WIDTH, SUBLANES, CONV).sum(axis=1)

    (delta,) = _rowwise("attn_delta", _st_delta(NH), [dattn, attn], [], [(LANES, F32)])
    delta = delta[:, :NH].T
    late_sent = [_by_device(n, bg.pop(n)) for n in late_names]
    dq, dk, dv, late_recv = _flash_bwd(q, k, v, dattn, lse, delta.reshape(NH, 1, S), NH,
                                       carry=_scatter_carry(late_sent) if late_names else None)
    dqraw, dkr, dk = _rowwise("rope_bwd", _st_rope_bwd(NH), [dq, dk, c_tab, a_tab, b_tab], [],
                              [(NH * QK_PAD, BF16), (QK_PAD, F32), (NH * QK_PAD, BF16)])
    d_w_q = _mm("d_w_q", cq_n, dqraw, "tn", out_dtype=BF16)
    d_w_k = _mm("d_w_k", ckv_n, dk, "tn", out_dtype=BF16)
    d_w_v = _mm("d_w_v", ckv_n, dv, "tn", out_dtype=BF16)
    dcq_n = _mm("d_cq_n", dqraw, w_q, "nt")
    dckv_n = _mm("d_ckv_n_k", dk, w_k, "nt")
    dckv_n = _mm("d_ckv_n_v", dv, w_v, "nt", acc_in=dckv_n)
    bg["w_uq"] = d_w_q.reshape(DQ, NH, QK_PAD)[:, :, :QK_NOPE + QK_ROPE].reshape(DQ, NH * (QK_NOPE + QK_ROPE))
    bg["w_ukv"] = jnp.concatenate([d_w_k.reshape(DKV, NH, QK_PAD)[:, :, :QK_NOPE], d_w_v.reshape(DKV, NH, V_DIM)],
                                  axis=2).reshape(DKV, NH * (QK_NOPE + V_DIM))
    dproj, d_qn, d_kvn = _rowwise_bwd("lora_norms_bwd", lora_fn, [cqkv], [small["q_norm"], small["kv_norm"]],
                                      [dcq_n, dckv_n], [0], [BF16], into={0: (dproj, po[3] // pw[3])})
    sg["q_norm"], sg["kv_norm"] = _fold(d_qn), _fold(d_kvn)

    d_small = jnp.concatenate([dkr[:, QK_NOPE:QK_NOPE + QK_ROPE], ddtraw[:, :NHS]], axis=1).astype(BF16)
    dproj = lax.dynamic_update_slice(dproj, d_small, (0, po[4]))
    d_w = _mm("d_w_in", u, dproj, "tn", out_dtype=BF16)
    kr = po[4] + QK_ROPE
    bg["w_in"] = jnp.concatenate([d_w[:, po[3]:po[4]], d_w[:, po[4]:kr], d_w[:, po[0]:po[1]], d_w[:, po[2]:po[3]],
                                  d_w[:, kr:kr + NHS], d_w[:, po[1]:po[2]]], axis=1)
    rest_names = tuple(n for n in BIG if n in bg)
    if not exchange:
        du = _mm("d_u", dproj, w_in, "nt")
        grad_x, d_pre = _rowwise_bwd("pre_norm_bwd", _st_pre, [x], [small["mix_norm_pre"]], [du], [0], [F32],
                                     adds={0: dx_res})
        sg["mix_norm_pre"] = _fold(d_pre)
        return loss, grad_x, sg, d_conv_w, bg, {}
    rest_sent = [_by_device(n, bg.pop(n)) for n in rest_names]
    du, rest_recv = _mm("d_u", dproj, w_in, "nt", carry=_scatter_carry(rest_sent))
    grad_x, d_pre = _rowwise_bwd("pre_norm_bwd", _st_pre, [x], [small["mix_norm_pre"]], [du], [0], [F32], adds={0: dx_res})
    sg["mix_norm_pre"] = _fold(d_pre)
    sent = dict(zip(late_names, late_sent), **dict(zip(rest_names, rest_sent)))
    recv = dict(zip(late_names, late_recv), **dict(zip(rest_names, rest_recv)))
    return loss, grad_x, sg, d_conv_w, sent, recv


_HBM = pl.BlockSpec(memory_space=pltpu.HBM)
_FLIPS = ((0, 0, 1), (1, 0, 0), (0, 1, 0), (1, 1, 0), (1, 0, 1), (0, 1, 1), (1, 1, 1))


def _place():
    return lax.axis_index("x"), lax.axis_index("y"), lax.axis_index("c")


def _flipped(place, flip):
    return tuple(1 - v if f else v for v, f in zip(place, flip))


def _gather_carry(blocks):
    nw = len(blocks)

    def copies(ins, outs, send_sems, recv_sems, local_sems):
        x, y, c = _place()
        me = 4 * x + 2 * y + c
        cps = []
        for w in range(nw):
            cps.append(pltpu.make_async_copy(ins[w], outs[w].at[me], local_sems.at[w]))
            for k, flip in enumerate(_FLIPS):
                cps.append(pltpu.make_async_remote_copy(
                    src_ref=ins[w], dst_ref=outs[w].at[me], send_sem=send_sems.at[7 * w + k],
                    recv_sem=recv_sems.at[7 * w + k], device_id=_flipped((x, y, c), flip), device_id_type=MESH))
        return cps

    return dict(ins=blocks, outs=[jax.ShapeDtypeStruct((N_DEV,) + b.shape, b.dtype) for b in blocks],
                n_remote=7 * nw, n_local=nw, copies=copies)


def _scatter_carry(by_dev):
    nw = len(by_dev)

    def copies(ins, outs, send_sems, recv_sems, local_sems):
        x, y, c = _place()
        cps = []
        for w in range(nw):
            for k, flip in enumerate(_FLIPS):
                px, py, pc = _flipped((x, y, c), flip)
                cps.append(pltpu.make_async_remote_copy(
                    src_ref=ins[w].at[4 * px + 2 * py + pc], dst_ref=outs[w].at[k], send_sem=send_sems.at[7 * w + k],
                    recv_sem=recv_sems.at[7 * w + k], device_id=(px, py, pc), device_id_type=MESH))
        return cps

    return dict(ins=by_dev, outs=[jax.ShapeDtypeStruct((7,) + b.shape[1:], b.dtype) for b in by_dev],
                n_remote=7 * nw, n_local=0, copies=copies)


def _all_gather(name, blocks):
    nw = len(blocks)

    def body(*refs):
        x_refs, out_refs = refs[:nw], refs[nw:2 * nw]
        send_sems, recv_sems, local_sems = refs[2 * nw:]
        x, y, c = _place()
        me, sibling = (x, y, c), (x, y, 1 - c)
        chips = [(1 - x, y), (x, 1 - y), (1 - x, 1 - y)]

        def slot(w, px, py, pc):
            return out_refs[w].at[4 * px + 2 * py + pc]

        def copy(w, k, blk, to, src=None):
            return pltpu.make_async_remote_copy(
                src_ref=slot(w, *blk) if src is None else src, dst_ref=slot(w, *blk),
                send_sem=send_sems.at[7 * w + k], recv_sem=recv_sems.at[7 * w + k], device_id=to, device_id_type=MESH)

        mine = [pltpu.make_async_copy(x_refs[w], slot(w, *me), local_sems.at[w]) for w in range(nw)]
        for cp in mine:
            cp.start()
        first = []
        for w in range(nw):
            first.append(copy(w, 0, me, sibling, src=x_refs[w]))
            first += [copy(w, 1 + j, me, (*chip, c), src=x_refs[w]) for j, chip in enumerate(chips)]
        for cp in first:
            cp.start()
        passed = []
        for j, chip in enumerate(chips):
            for w in range(nw):
                copy(w, 1 + j, (*chip, c), me).wait_recv()
                passed.append(copy(w, 4 + j, (*chip, c), sibling))
                passed[-1].start()
        for w in range(nw):
            copy(w, 0, sibling, me).wait_recv()
        for j, chip in enumerate(chips):
            for w in range(nw):
                copy(w, 4 + j, (*chip, 1 - c), me).wait_recv()
        for cp in first + passed:
            cp.wait_send()
        for cp in mine:
            cp.wait()

    return pl.pallas_call(
        body, name=name, out_shape=[jax.ShapeDtypeStruct((N_DEV,) + b.shape, b.dtype) for b in blocks],
        in_specs=[_HBM] * nw, out_specs=[_HBM] * nw,
        scratch_shapes=[pltpu.SemaphoreType.DMA((7 * nw,)), pltpu.SemaphoreType.DMA((7 * nw,)),
                        pltpu.SemaphoreType.DMA((nw,))],
    )(*blocks)


def _lane_pad(n):
    return -(-n // LANES) * LANES


def _pack_small(vecs, mat):
    width = max(sum(_lane_pad(v.shape[1]) for v in vecs), _lane_pad(mat.shape[1]))
    row0 = jnp.concatenate([_pad_cols(v, _lane_pad(v.shape[1])) for v in vecs], axis=1)
    rows = jnp.concatenate([_pad_cols(row0, width), _pad_cols(mat, width)], axis=0)
    return jnp.pad(rows, ((0, SUBLANES - rows.shape[0]), (0, 0)))


def _unpack_small(packed, sizes, mat_cols):
    vecs, off = [], 0
    for n in sizes:
        vecs.append(packed[0:1, off:off + n])
        off += _lane_pad(n)
    return vecs, packed[1:1 + CONV_WIDTH, :mat_cols]


def _adamw(w, g, m, v):
    m = ADAM_B1 * m + (1.0 - ADAM_B1) * g
    v = ADAM_B2 * v + (1.0 - ADAM_B2) * (g * g)
    m_hat = m / (1.0 - ADAM_B1 ** ADAM_STEP)
    v_hat = v / (1.0 - ADAM_B2 ** ADAM_STEP)
    delta = -ADAM_LR * (m_hat / (jnp.sqrt(v_hat) + ADAM_EPS) + ADAM_WD * w)
    return delta, m, v


BIG = ("w_in", "w_uq", "w_ukv", "w_attn_o", "w_ssm_o", "w_out", "w_gate", "w_up", "w_down", "w_ple_gate", "w_ple")
FIRST = ("w_in", "w_uq", "w_ukv")
LATE = ("w_attn_o", "w_ssm_o", "w_out", "w_gate", "w_up", "w_down", "w_ple_gate", "w_ple")
COL_SHARDED = ("w_in", "w_uq", "w_ukv", "w_gate", "w_up", "w_ple")
SMALL = ("mix_norm_pre", "mix_norm_post", "q_norm", "kv_norm", "conv_b", "dt_bias", "a_log", "d_skip", "ssm_norm",
         "ffn_norm_pre", "ffn_norm_post", "ple_norm_pre", "ple_norm_post")
WEIGHTS = ("mix_norm_pre", "mix_norm_post", "w_in", "q_norm", "w_uq", "kv_norm", "w_ukv", "conv_w", "conv_b", "dt_bias",
           "a_log", "d_skip", "ssm_norm", "w_attn_o", "w_ssm_o", "w_out", "ffn_norm_pre", "ffn_norm_post", "w_gate",
           "w_up", "w_down", "ple_norm_pre", "ple_norm_post", "w_ple_gate", "w_ple")


def _step(x, p, positions, target, w, m, v):
    xi, yi, ci = _place()
    me = 4 * xi + 2 * yi + ci

    gathered = _all_gather("gather_weights", [w[n].astype(BF16) for n in FIRST])
    wts = {n: _assemble(n, blocks) for n, blocks in zip(FIRST, gathered)}
    cw_rows, cw_cols = w["conv_w"].shape
    (cw_all,) = _all_gather("gather_conv_w", [jnp.pad(w["conv_w"], ((0, SUBLANES - cw_rows), (0, 0)))])
    conv_w = cw_all[:, :cw_rows, :].transpose(1, 0, 2).reshape(cw_rows, N_DEV * cw_cols)

    small = {n: w[n] for n in SMALL}
    loss, grad_x, sg, d_conv_w, sent, recv = _local_step(
        x, p, positions, target, small, conv_w, wts, LATE, [w[n].astype(BF16) for n in LATE])

    sizes = [w[n].shape[1] for n in SMALL]
    sg_pack = _pack_small([sg[n] for n in SMALL] + [loss], d_conv_w)
    (sg_all,) = _all_gather("gather_small_grads", [sg_pack])
    (sg_sum,) = _rowwise("sum_small_grads", lambda *a: functools.reduce(lambda s, t: s + t, a),
                         [sg_all[k] for k in range(N_DEV)], [], [(sg_pack.shape[1], F32)])
    sg_vecs, d_conv_w_sum = _unpack_small(sg_sum, sizes + [LANES], d_conv_w.shape[1])
    loss = sg_vecs[-1][0, 0]
    grads = dict(zip(SMALL, sg_vecs[:-1]))
    grads["conv_w"] = lax.dynamic_slice_in_dim(d_conv_w_sum, me * cw_cols, cw_cols, axis=1)

    def sum8_then_adamw(wv, mv, vv, own, *others):
        g = functools.reduce(lambda s, t: s + t, others, own)
        return (g,) + _adamw(wv, g, mv, vv)

    delta, new_m, new_v = {}, {}, {}
    for n in BIG:
        cols = w[n].shape[1]
        own = lax.dynamic_index_in_dim(sent[n], me, axis=0, keepdims=False)
        grads[n], delta[n], new_m[n], new_v[n] = _rowwise(
            "adamw_" + n, sum8_then_adamw, [w[n], m[n], v[n], own] + [recv[n][k] for k in range(N_DEV - 1)], [],
            [(cols, F32)] * 4)
    packed = [_pack_small([d[n] for n in SMALL], d["conv_w"]) for d in (w, grads, m, v)]
    outs = _rowwise("adamw_small", _adamw, packed, [], [(packed[0].shape[1], F32)] * 3)
    for d, o in zip((delta, new_m, new_v), outs):
        vecs, mat = _unpack_small(o, sizes, cw_cols)
        d.update(zip(SMALL, vecs))
        d["conv_w"] = mat
    return loss, grad_x, grads, delta, new_m, new_v


def kernel(x, p, positions, mix_norm_pre, mix_norm_post, w_in, q_norm, w_uq, kv_norm, w_ukv, conv_w, conv_b, dt_bias, a_log, d_skip, ssm_norm, w_attn_o, w_ssm_o, w_out, ffn_norm_pre, ffn_norm_post, w_gate, w_up, w_down, ple_norm_pre, ple_norm_post, w_ple_gate, w_ple, loss_target, m_mix_norm_pre, m_mix_norm_post, m_w_in, m_q_norm, m_w_uq, m_kv_norm, m_w_ukv, m_conv_w, m_conv_b, m_dt_bias, m_a_log, m_d_skip, m_ssm_norm, m_w_attn_o, m_w_ssm_o, m_w_out, m_ffn_norm_pre, m_ffn_norm_post, m_w_gate, m_w_up, m_w_down, m_ple_norm_pre, m_ple_norm_post, m_w_ple_gate, m_w_ple, v_mix_norm_pre, v_mix_norm_post, v_w_in, v_q_norm, v_w_uq, v_kv_norm, v_w_ukv, v_conv_w, v_conv_b, v_dt_bias, v_a_log, v_d_skip, v_ssm_norm, v_w_attn_o, v_w_ssm_o, v_w_out, v_ffn_norm_pre, v_ffn_norm_post, v_w_gate, v_w_up, v_w_down, v_ple_norm_pre, v_ple_norm_post, v_w_ple_gate, v_w_ple):
    w_args = (mix_norm_pre, mix_norm_post, w_in, q_norm, w_uq, kv_norm, w_ukv, conv_w, conv_b, dt_bias, a_log, d_skip, ssm_norm, w_attn_o, w_ssm_o, w_out, ffn_norm_pre, ffn_norm_post, w_gate, w_up, w_down, ple_norm_pre, ple_norm_post, w_ple_gate, w_ple)
    m_args = (m_mix_norm_pre, m_mix_norm_post, m_w_in, m_q_norm, m_w_uq, m_kv_norm, m_w_ukv, m_conv_w, m_conv_b, m_dt_bias, m_a_log, m_d_skip, m_ssm_norm, m_w_attn_o, m_w_ssm_o, m_w_out, m_ffn_norm_pre, m_ffn_norm_post, m_w_gate, m_w_up, m_w_down, m_ple_norm_pre, m_ple_norm_post, m_w_ple_gate, m_w_ple)
    v_args = (v_mix_norm_pre, v_mix_norm_post, v_w_in, v_q_norm, v_w_uq, v_kv_norm, v_w_ukv, v_conv_w, v_conv_b, v_dt_bias, v_a_log, v_d_skip, v_ssm_norm, v_w_attn_o, v_w_ssm_o, v_w_out, v_ffn_norm_pre, v_ffn_norm_post, v_w_gate, v_w_up, v_w_down, v_ple_norm_pre, v_ple_norm_post, v_w_ple_gate, v_w_ple)

    def drop_layer(a):
        return a if a.ndim == 2 else a[0]

    w = {n: drop_layer(a) for n, a in zip(WEIGHTS, w_args)}
    m = {n: drop_layer(a) for n, a in zip(WEIGHTS, m_args)}
    v = {n: drop_layer(a) for n, a in zip(WEIGHTS, v_args)}
    loss, grad_x, grads, delta, new_m, new_v = _step(x[0], p[0, 0], positions[0], loss_target[0], w, m, v)
    like = lambda d: [d[n].reshape(a.shape) for n, a in zip(WEIGHTS, w_args)]
    return (loss, grad_x[None], *like(grads), *like(delta), *like(new_m), *like(new_v))
```
